```python
import jax, jax.numpy as jnp
from jax import lax
import numpy as np

D_MODEL = 1024
BATCH = 32
SEQ = 256
DEPTH = 4
DEC_BATCH = 4
DEC_SEQ = 1024
PAST_LEN = 512

GRID_W = 64
MIXER_KINDS = ('attn', 'mlstm', 'gla')
N_MIXERS = 3
EPS = 1e-6
ATT_HEADS = 16
ATT_KV_HEADS = 4
ATT_GROUP = ATT_HEADS // ATT_KV_HEADS
HEAD_DIM = D_MODEL // ATT_HEADS
WINDOW = 128
Q_BLOCK = 128
ROPE_BASE = 10000.0
MLSTM_HEADS = 8
MLSTM_DK = D_MODEL // MLSTM_HEADS
MLSTM_DV = 2 * D_MODEL // MLSTM_HEADS
MLSTM_CHUNK = 64
CONV_W = 3
GLA_HEADS = 4
GLA_DK = D_MODEL // 2 // GLA_HEADS
GLA_DV = D_MODEL // GLA_HEADS
GLA_GATE_RANK = 16
GLA_TAU = 16.0
GLA_CHUNK = 16
N_EXPERTS = 16
N_GROUPS = 4
EXPERTS_PER_GROUP = N_EXPERTS // N_GROUPS
TOP_K = 2
D_EXPERT = D_MODEL // 2

kernel_name = 'hybrid_dit_attn_mlstm_gla_moe_step'


def rmsnorm(x, g):
    xf = x.astype(jnp.float32)
    y = xf * lax.rsqrt(jnp.mean(xf * xf, axis=-1, keepdims=True) + EPS)
    return (y * g.astype(jnp.float32)).astype(x.dtype)


def head_rmsnorm(x, g):
    xf = x.astype(jnp.float32)
    y = xf * lax.rsqrt(jnp.mean(xf * xf, axis=-1, keepdims=True) + EPS)
    return y * g.astype(jnp.float32).reshape(x.shape[2], x.shape[3])


def adaln(cvec, w_mod, b_mod):
    m = jax.nn.silu(cvec) @ w_mod + b_mod
    return [t[:, None, :] for t in jnp.split(m, 6, axis=-1)]


def flip(x):
    return jnp.flip(x, axis=1)


def short_conv(x, w):
    xp = jnp.pad(x, ((0, 0), (1, 1), (0, 0)))
    return xp[:, :-2] * w[0] + xp[:, 1:-1] * w[1] + xp[:, 2:] * w[2]


def axial_rope(x):
    T = x.shape[1]
    rows = T // GRID_W
    row = jnp.repeat(jnp.arange(rows, dtype=jnp.int32), GRID_W)
    col = jnp.tile(jnp.arange(GRID_W, dtype=jnp.int32), rows)
    half = HEAD_DIM // 2
    nf = half // 2
    inv = ROPE_BASE ** (-jnp.arange(nf, dtype=jnp.float32) / nf)
    xf = x.astype(jnp.float32)

    def rot(xa, pos):
        ang = pos.astype(jnp.float32)[:, None] * inv[None, :]
        cos = jnp.cos(ang)[None, :, None, :]
        sin = jnp.sin(ang)[None, :, None, :]
        x1, x2 = xa[..., :nf], xa[..., nf:]
        return jnp.concatenate([x1 * cos - x2 * sin, x2 * cos + x1 * sin], axis=-1)

    return jnp.concatenate([rot(xf[..., :half], row), rot(xf[..., half:], col)], axis=-1).astype(x.dtype)


def attend(qb, k, v, mask, sink):
    s = jnp.einsum('bqkgd,bnkd->bkgqn', qb, k).astype(jnp.float32) * HEAD_DIM ** -0.5
    if mask is not None:
        s = jnp.where(mask, s, -jnp.inf)
    sk = sink.astype(jnp.float32).reshape(1, ATT_KV_HEADS, ATT_GROUP, 1, 1)
    m = jnp.maximum(s.max(axis=-1, keepdims=True), sk)
    p = jnp.exp(s - m)
    den = p.sum(axis=-1, keepdims=True) + jnp.exp(sk - m)
    return jnp.einsum('bkgqn,bnkd->bqkgd', (p / den).astype(v.dtype), v)


def attn_qkv(h, w_qkv):
    B, T, _ = h.shape
    qkv = h @ w_qkv
    qw = ATT_HEADS * HEAD_DIM
    kw = ATT_KV_HEADS * HEAD_DIM
    q = qkv[..., :qw].reshape(B, T, ATT_HEADS, HEAD_DIM)
    k = qkv[..., qw:qw + kw].reshape(B, T, ATT_KV_HEADS, HEAD_DIM)
    v = qkv[..., qw + kw:].reshape(B, T, ATT_KV_HEADS, HEAD_DIM)
    return q, k, v


def q_blocks(q):
    B, T = q.shape[:2]
    nb = T // Q_BLOCK
    return q.reshape(B, nb, Q_BLOCK, ATT_KV_HEADS, ATT_GROUP, HEAD_DIM).transpose(1, 0, 2, 3, 4, 5)


def merge_blocks(out, B, T):
    return out.transpose(1, 0, 2, 3, 4, 5).reshape(B, T, ATT_HEADS * HEAD_DIM)


def attn_context(h, w_qkv, sink, w_o):
    B, T, _ = h.shape
    q, k, v = attn_qkv(h, w_qkv)
    out = lax.map(lambda qb: attend(qb, k, v, None, sink), q_blocks(q))
    return merge_blocks(out, B, T) @ w_o, (k, v)


def attn_latent(h, state, w_qkv, sink, w_o):
    ck, cv = state
    B, T, _ = h.shape
    Lc = ck.shape[1]
    q, k, v = attn_qkv(h, w_qkv)
    q = axial_rope(q)
    k = axial_rope(k)
    pad = ((0, 0), (WINDOW, WINDOW), (0, 0), (0, 0))
    k_pad = jnp.pad(k, pad)
    v_pad = jnp.pad(v, pad)
    span = Q_BLOCK + 2 * WINDOW
    ctx_mask = jnp.ones((Q_BLOCK, Lc), dtype=bool)

    def block(args):
        qb, i = args
        start = i * Q_BLOCK
        kw = lax.dynamic_slice_in_dim(k_pad, start, span, axis=1)
        vw = lax.dynamic_slice_in_dim(v_pad, start, span, axis=1)
        qpos = start + jnp.arange(Q_BLOCK)
        kpos = start - WINDOW + jnp.arange(span)
        win = (jnp.abs(qpos[:, None] - kpos[None, :]) <= WINDOW) & (kpos >= 0)[None, :] & (kpos < T)[None, :]
        mask = jnp.concatenate([ctx_mask, win], axis=1)
        return attend(qb, jnp.concatenate([ck.astype(kw.dtype), kw], axis=1),
                      jnp.concatenate([cv.astype(vw.dtype), vw], axis=1), mask, sink)

    out = lax.map(block, (q_blocks(q), jnp.arange(T // Q_BLOCK)))
    return merge_blocks(out, B, T) @ w_o


def mlstm_scan(q, k, v, log_i, log_f, C0, n0, m0):
    B, T, H, dk = q.shape
    dv = v.shape[-1]
    L = MLSTM_CHUNK
    nc = T // L

    def chunks(x):
        return x.reshape(B, nc, L, *x.shape[2:]).swapaxes(0, 1)

    causal = jnp.tril(jnp.ones((L, L), dtype=bool))

    def step(carry, xs):
        C, n, m = carry
        qc, kc, vc, ic, fc = xs
        b = jnp.cumsum(fc, axis=1)
        dmat = b[:, :, None, :] - b[:, None, :, :] + ic[:, None, :, :]
        dmat = jnp.where(causal[None, :, :, None], dmat, -jnp.inf)
        inter = b + m[:, None, :]
        m_t = jnp.maximum(inter, dmat.max(axis=2))
        w = jnp.exp(dmat - m_t[:, :, None, :])
        qk = jnp.einsum('bthd,bshd->btsh', qc, kc) * w
        sc = jnp.exp(inter - m_t)
        num = sc[..., None] * jnp.einsum('bhvd,bthd->bthv', C, qc) + jnp.einsum('btsh,bshv->bthv', qk, vc)
        den = sc * jnp.einsum('bhd,bthd->bth', n, qc) + qk.sum(axis=2)
        hc = num / jnp.maximum(jnp.abs(den), jnp.exp(-m_t))[..., None]
        b_last = b[:, -1]
        wlog = b_last[:, None, :] - b + ic
        m_new = jnp.maximum(b_last + m, wlog.max(axis=1))
        decay = jnp.exp(b_last + m - m_new)
        ws = jnp.exp(wlog - m_new[:, None, :])
        C_new = decay[..., None, None] * C + jnp.einsum('bsh,bshv,bshd->bhvd', ws, vc, kc)
        n_new = decay[..., None] * n + jnp.einsum('bsh,bshd->bhd', ws, kc)
        return (C_new, n_new, m_new), hc

    xs = (chunks(q), chunks(k), chunks(v), chunks(log_i), chunks(log_f))
    state, hs = lax.scan(step, (C0, n0, m0), xs)
    return hs.swapaxes(0, 1).reshape(B, T, H, dv), state


def mlstm_mixer(h, state, w_in, b_gates, conv_w, norm_g, w_out):
    B, T, _ = h.shape
    f32 = jnp.float32
    qk_w = MLSTM_HEADS * MLSTM_DK
    v_w = MLSTM_HEADS * MLSTM_DV
    p = h @ w_in
    qk = jax.nn.silu(short_conv(p[..., :2 * qk_w], conv_w))
    q = qk[..., :qk_w].reshape(B, T, MLSTM_HEADS, MLSTM_DK).astype(f32)
    k = qk[..., qk_w:].reshape(B, T, MLSTM_HEADS, MLSTM_DK).astype(f32) * MLSTM_DK ** -0.5
    v = p[..., 2 * qk_w:2 * qk_w + v_w].reshape(B, T, MLSTM_HEADS, MLSTM_DV).astype(f32)
    o = jax.nn.sigmoid(p[..., 2 * qk_w + v_w:2 * qk_w + 2 * v_w])
    g = (p[..., 2 * qk_w + 2 * v_w:] + b_gates).astype(f32).reshape(B, T, 2, 2, MLSTM_HEADS)
    C0, n0, m0 = (s.astype(f32) for s in state)
    hf, (Cf, nf, mf) = mlstm_scan(q, k, v, g[:, :, 0, 0], jax.nn.log_sigmoid(g[:, :, 0, 1]),
                                  C0[:, 0], n0[:, 0], m0[:, 0])
    hb, (Cb, nb, mb) = mlstm_scan(flip(q), flip(k), flip(v), flip(g[:, :, 1, 0]),
                                  flip(jax.nn.log_sigmoid(g[:, :, 1, 1])), C0[:, 1], n0[:, 1], m0[:, 1])
    hsum = head_rmsnorm(hf + flip(hb), norm_g).astype(h.dtype).reshape(B, T, v_w)
    y = (o * hsum) @ w_out
    return y, (jnp.stack([Cf, Cb], axis=1), jnp.stack([nf, nb], axis=1), jnp.stack([mf, mb], axis=1))


def gla_scan(q, k, v, log_a, S0):
    B, T, H, dk = q.shape
    dv = v.shape[-1]
    L = GLA_CHUNK
    nc = T // L
    qc = q.reshape(B, nc, L, H, dk)
    kc = k.reshape(B, nc, L, H, dk)
    vc = v.reshape(B, nc, L, H, dv)
    bc = jnp.cumsum(log_a.reshape(B, nc, L, H, dk), axis=2)
    causal = jnp.tril(jnp.ones((L, L), dtype=bool))
    diff = bc[:, :, :, None] - bc[:, :, None, :]
    diff = jnp.where(causal[None, None, :, :, None, None], diff, -jnp.inf)
    A = jnp.einsum('bcthd,bcshd,bctshd->bchts', qc, kc, jnp.exp(diff))
    o_intra = jnp.einsum('bchts,bcshv->bcthv', A, vc)
    b_last = bc[:, :, -1]
    k_dec = kc * jnp.exp(b_last[:, :, None] - bc)
    chunk_kv = jnp.einsum('bcshd,bcshv->bchdv', k_dec, vc)

    def step(S, xs):
        dec, kv = xs
        return jnp.exp(dec)[..., None] * S + kv, S

    S_final, S_prev = lax.scan(step, S0, (b_last.swapaxes(0, 1), chunk_kv.swapaxes(0, 1)))
    o_inter = jnp.einsum('bcthd,cbhdv->bcthv', qc * jnp.exp(bc), S_prev)
    return (o_intra + o_inter).reshape(B, T, H, dv), S_final


def gla_mixer(h, state, w_in, w_a1, w_a2, b_a, norm_g, w_out):
    B, T, _ = h.shape
    f32 = jnp.float32
    kw = GLA_HEADS * GLA_DK
    vw = GLA_HEADS * GLA_DV
    p = h @ w_in
    q = p[..., :kw].reshape(B, T, GLA_HEADS, GLA_DK).astype(f32) * GLA_DK ** -0.5
    k = p[..., kw:2 * kw].reshape(B, T, GLA_HEADS, GLA_DK).astype(f32)
    v = p[..., 2 * kw:2 * kw + vw].reshape(B, T, GLA_HEADS, GLA_DV).astype(f32)
    r = p[..., 2 * kw + vw:]

    def log_gate(d):
        z = (h @ w_a1[d]) @ w_a2[d] + b_a[d]
        return jax.nn.log_sigmoid(z.astype(f32)).reshape(B, T, GLA_HEADS, GLA_DK) / GLA_TAU

    (S0,) = state
    S0 = S0.astype(f32)
    of, Sf = gla_scan(q, k, v, log_gate(0), S0[:, 0])
    ob, Sb = gla_scan(flip(q), flip(k), flip(v), flip(log_gate(1)), S0[:, 1])
    o = head_rmsnorm(of + flip(ob), norm_g).astype(h.dtype).reshape(B, T, vw)
    y = (o * jax.nn.silu(r)) @ w_out
    return y, (jnp.stack([Sf, Sb], axis=1),)


def moe(h, router_w, router_b, wg, wu, wd):
    B, T, D = h.shape
    x = h.reshape(B * T, D)
    scores = jax.nn.sigmoid((x @ router_w).astype(jnp.float32))
    sel = scores + router_b.astype(jnp.float32)
    gscore = lax.top_k(sel.reshape(-1, N_GROUPS, EXPERTS_PER_GROUP), 2)[0].sum(axis=-1)
    gbest = jnp.argmax(gscore, axis=-1)
    in_group = (jnp.arange(N_EXPERTS) // EXPERTS_PER_GROUP)[None, :] == gbest[:, None]
    _, idx = lax.top_k(jnp.where(in_group, sel, -jnp.inf), TOP_K)
    w = jnp.take_along_axis(scores, idx, axis=-1)
    w = w / w.sum(axis=-1, keepdims=True)
    combine = jnp.einsum('nk,nke->ne', w, jax.nn.one_hot(idx, N_EXPERTS, dtype=jnp.float32))
    hid = jax.nn.silu(jnp.einsum('nd,edf->nef', x, wg)) * jnp.einsum('nd,edf->nef', x, wu)
    y = jnp.einsum('nef,efd->nd', hid * combine[..., None].astype(hid.dtype), wd)
    return y.reshape(B, T, D)


def setup_inputs(seed: int = 0) -> dict:
    key = jax.random.key(seed)
    ks = iter(jax.random.split(key, 48))
    f32 = jnp.float32

    def nrm(shape, scale=1.0):
        return jax.random.normal(next(ks), shape, f32) * scale

    D = D_MODEL
    qk_w = MLSTM_HEADS * MLSTM_DK
    v_w = MLSTM_HEADS * MLSTM_DV
    att_cols = (ATT_HEADS + 2 * ATT_KV_HEADS) * HEAD_DIM
    gla_cols = 2 * GLA_HEADS * GLA_DK + 2 * GLA_HEADS * GLA_DV
    mlstm_cols = 2 * qk_w + 2 * v_w + 4 * MLSTM_HEADS
    kv_shape = (DEC_BATCH, PAST_LEN, ATT_KV_HEADS, HEAD_DIM)
    d = {}
    d['x_prompt'] = nrm((BATCH, SEQ, D))
    d['x_sample'] = nrm((DEC_BATCH, DEC_SEQ, D))
    d['cache_k_0'] = nrm(kv_shape)
    d['cache_v_0'] = nrm(kv_shape)
    d['state_mlstm_C_1'] = nrm((DEC_BATCH, 2, MLSTM_HEADS, MLSTM_DV, MLSTM_DK), 0.1)
    d['state_mlstm_n_1'] = nrm((DEC_BATCH, 2, MLSTM_HEADS, MLSTM_DK), 0.1)
    d['state_mlstm_m_1'] = nrm((DEC_BATCH, 2, MLSTM_HEADS), 0.5)
    d['state_gla_S_2'] = nrm((DEC_BATCH, 2, GLA_HEADS, GLA_DK, GLA_DV), 0.1)
    d['cache_k_3'] = nrm(kv_shape)
    d['cache_v_3'] = nrm(kv_shape)
    d['c'] = nrm((DEC_BATCH, D))
    d['c_ctx'] = nrm((D,))
    d['w_mod'] = nrm((DEPTH, D, 6 * D), 0.5 * D ** -0.5)
    d['b_mod'] = nrm((DEPTH, 6 * D), 0.01)
    d['norm1_g'] = 1.0 + nrm((DEPTH, D), 0.1)
    d['norm2_g'] = 1.0 + nrm((DEPTH, D), 0.1)
    d['final_g'] = 1.0 + nrm((D,), 0.1)
    d['router_w'] = nrm((D, N_EXPERTS), D ** -0.5)
    d['router_b'] = nrm((N_EXPERTS,), 0.01)
    d['moe_wg'] = nrm((DEPTH, N_EXPERTS, D, D_EXPERT), D ** -0.5)
    d['moe_wu'] = nrm((DEPTH, N_EXPERTS, D, D_EXPERT), D ** -0.5)
    d['moe_wd'] = nrm((DEPTH, N_EXPERTS, D_EXPERT, D), D_EXPERT ** -0.5)
    d['attn0_w_qkv'] = nrm((D, att_cols), D ** -0.5)
    d['attn0_sink'] = nrm((ATT_HEADS,))
    d['attn0_w_o'] = nrm((ATT_HEADS * HEAD_DIM, D), (ATT_HEADS * HEAD_DIM) ** -0.5)
    d['mlstm1_w_in'] = nrm((D, mlstm_cols), D ** -0.5)
    d['mlstm1_b_gates'] = (nrm((2, 2, MLSTM_HEADS), 0.1) + jnp.array([0.0, 3.0], f32)[None, :, None]).reshape(-1)
    d['mlstm1_conv'] = nrm((CONV_W, 2 * qk_w), 0.2) + jnp.array([0.0, 1.0, 0.0], f32)[:, None]
    d['mlstm1_norm_g'] = 1.0 + nrm((v_w,), 0.1)
    d['mlstm1_w_out'] = nrm((v_w, D), v_w ** -0.5)
    d['gla2_w_in'] = nrm((D, gla_cols), D ** -0.5)
    d['gla2_w_a1'] = nrm((2, D, GLA_GATE_RANK), D ** -0.5)
    d['gla2_w_a2'] = nrm((2, GLA_GATE_RANK, GLA_HEADS * GLA_DK), GLA_GATE_RANK ** -0.5)
    d['gla2_b_a'] = nrm((2, GLA_HEADS * GLA_DK), 0.5)
    d['gla2_norm_g'] = 1.0 + nrm((GLA_HEADS * GLA_DV,), 0.1)
    d['gla2_w_out'] = nrm((GLA_HEADS * GLA_DV, D), (GLA_HEADS * GLA_DV) ** -0.5)
    d['attn3_w_qkv'] = nrm((D, att_cols), D ** -0.5)
    d['attn3_sink'] = nrm((ATT_HEADS,))
    d['attn3_w_o'] = nrm((ATT_HEADS * HEAD_DIM, D), (ATT_HEADS * HEAD_DIM) ** -0.5)
    return d


def reference(x_prompt, x_sample, cache_k_0, cache_v_0, state_mlstm_C_1, state_mlstm_n_1, state_mlstm_m_1,
              state_gla_S_2, cache_k_3, cache_v_3, c, c_ctx, w_mod, b_mod, norm1_g, norm2_g, final_g,
              router_w, router_b, moe_wg, moe_wu, moe_wd, attn0_w_qkv, attn0_sink, attn0_w_o,
              mlstm1_w_in, mlstm1_b_gates, mlstm1_conv, mlstm1_norm_g, mlstm1_w_out,
              gla2_w_in, gla2_w_a1, gla2_w_a2, gla2_b_a, gla2_norm_g, gla2_w_out,
              attn3_w_qkv, attn3_sink, attn3_w_o):
    f32 = jnp.float32
    layer_params = (
        ((attn0_w_qkv, attn0_sink, attn0_w_o), (cache_k_0, cache_v_0)),
        ((mlstm1_w_in, mlstm1_b_gates, mlstm1_conv, mlstm1_norm_g, mlstm1_w_out),
         (state_mlstm_C_1, state_mlstm_n_1, state_mlstm_m_1)),
        ((gla2_w_in, gla2_w_a1, gla2_w_a2, gla2_b_a, gla2_norm_g, gla2_w_out), (state_gla_S_2,)),
        ((attn3_w_qkv, attn3_sink, attn3_w_o), (cache_k_3, cache_v_3)),
    )
    xc, xl = x_prompt, x_sample
    nbc = x_prompt.shape[0]
    new_state = []
    for i in range(DEPTH):
        kind = MIXER_KINDS[i % N_MIXERS]
        params, cache = layer_params[i]
        sc1, ac1, gc1, sc2, ac2, gc2 = adaln(c_ctx[None, :], w_mod[i], b_mod[i])
        sl1, al1, gl1, sl2, al2, gl2 = adaln(c, w_mod[i], b_mod[i])
        hc = rmsnorm(xc, norm1_g[i]) * (1 + ac1) + sc1
        hl = rmsnorm(xl, norm1_g[i]) * (1 + al1) + sl1
        if kind == 'attn':
            yc, st = attn_context(hc, *params)
            yl = attn_latent(hl, cache, *params)
        elif kind == 'mlstm':
            zero = (jnp.zeros((nbc, 2, MLSTM_HEADS, MLSTM_DV, MLSTM_DK), f32),
                    jnp.zeros((nbc, 2, MLSTM_HEADS, MLSTM_DK), f32),
                    jnp.zeros((nbc, 2, MLSTM_HEADS), f32))
            yc, st = mlstm_mixer(hc, zero, *params)
            yl, _ = mlstm_mixer(hl, cache, *params)
        else:
            zero = (jnp.zeros((nbc, 2, GLA_HEADS, GLA_DK, GLA_DV), f32),)
            yc, st = gla_mixer(hc, zero, *params)
            yl, _ = gla_mixer(hl, cache, *params)
        new_state.extend(s.astype(x_prompt.dtype) for s in st)
        xc = xc + gc1 * yc
        xl = xl + gl1 * yl
        hc2 = rmsnorm(xc, norm2_g[i]) * (1 + ac2) + sc2
        hl2 = rmsnorm(xl, norm2_g[i]) * (1 + al2) + sl2
        xc = xc + gc2 * moe(hc2, router_w, router_b, moe_wg[i], moe_wu[i], moe_wd[i])
        xl = xl + gl2 * moe(hl2, router_w, router_b, moe_wg[i], moe_wu[i], moe_wd[i])
    y_prompt = rmsnorm(xc, final_g)
    y_sample = rmsnorm(xl, final_g)
    k0, v0, c1, n1, m1, s2, k3, v3 = new_state
    return (y_prompt, y_sample, k0, v0, c1, n1, m1, s2, k3, v3)
```

```python
import functools

import jax
import jax.numpy as jnp
from jax import lax
from jax.experimental import pallas as pl
from jax.experimental.pallas import tpu as pltpu

F32 = jnp.float32
BF16 = jnp.bfloat16
HI = lax.Precision.HIGHEST

EPS = 1e-6
GRID_W = 64
ATT_HEADS = 16
ATT_KV = 4
ATT_GROUP = ATT_HEADS // ATT_KV
HEAD_DIM = 64
WINDOW = 128
Q_BLOCK = 128
ROPE_BASE = 10000.0
ML_HEADS = 8
ML_DK = 128
ML_DV = 256
ML_CHUNK = 256
GLA_HEADS = 4
GLA_DK = 128
GLA_DV = 256
GLA_RANK = 16
GLA_TAU = 16.0
GLA_SUB = 16
GLA_BLOCK = 256
N_EXPERTS = 16
N_GROUPS = 4
GROUP_SIZE = N_EXPERTS // N_GROUPS
LANES = 128
VMEM_LIMIT = 56 * 1024 * 1024


def _cparams(*sem):
    return pltpu.CompilerParams(dimension_semantics=sem, vmem_limit_bytes=VMEM_LIMIT)


def _dot(a, b):
    return jnp.dot(a, b, preferred_element_type=F32)


def _dot_nt(a, b, precision=None):
    return lax.dot_general(a, b, (((1,), (1,)), ((), ())), precision=precision, preferred_element_type=F32)


def _dot_tn(a, b):
    return lax.dot_general(a, b, (((0,), (0,)), ((), ())), preferred_element_type=F32)


def _sigmoid(x):
    return 1.0 / (1.0 + jnp.exp(-x))


def _silu(x):
    return x * _sigmoid(x)


def _log_sigmoid(x):
    return jnp.minimum(x, 0.0) - jnp.log(1.0 + jnp.exp(-jnp.abs(x)))


def _rms_rows(x, g):
    ms = jnp.mean(x * x, axis=-1, keepdims=True)
    return x * lax.rsqrt(ms + EPS) * g


def _mod_kernel(c_ref, w_ref, b_ref, o_ref):
    s = _silu(c_ref[...])
    o_ref[0] = _dot(s.astype(BF16), w_ref[0].astype(BF16)) + b_ref[0]


def _modulation(cvec, w_mod, b_mod):
    depth, d, n6 = w_mod.shape
    tn = 1536
    return pl.pallas_call(
        _mod_kernel,
        grid=(depth, n6 // tn),
        in_specs=[pl.BlockSpec((8, d), lambda l, j: (0, 0)),
                  pl.BlockSpec((1, d, tn), lambda l, j: (l, 0, j)),
                  pl.BlockSpec((1, 1, tn), lambda l, j: (l, 0, j))],
        out_specs=pl.BlockSpec((1, 8, tn), lambda l, j: (l, 0, j)),
        out_shape=jax.ShapeDtypeStruct((depth, 8, n6), F32),
        compiler_params=_cparams("parallel", "parallel"),
        name="adaln_modulation",
    )(cvec, w_mod, b_mod.reshape(depth, 1, n6))


def _route(h, rwt, rb):
    tm = h.shape[0]
    logits = _dot_nt(rwt, h, precision=HI)
    scores = _sigmoid(logits)
    sel = scores + rb
    rows = [sel[e:e + 1, :] for e in range(N_EXPERTS)]
    gscore = []
    for g in range(N_GROUPS):
        r = rows[GROUP_SIZE * g:GROUP_SIZE * (g + 1)]
        best = None
        for i in range(GROUP_SIZE):
            for j in range(i + 1, GROUP_SIZE):
                pair = r[i] + r[j]
                best = pair if best is None else jnp.maximum(best, pair)
        gscore.append(best)
    gmax = functools.reduce(jnp.maximum, gscore)
    taken = jnp.zeros_like(gmax)
    weights = []
    for g in range(N_GROUPS):
        hit = jnp.where(gscore[g] == gmax, 1.0, 0.0) * (1.0 - taken)
        taken = taken + hit
        r = rows[GROUP_SIZE * g:GROUP_SIZE * (g + 1)]
        for i in range(GROUP_SIZE):
            rank = jnp.zeros_like(gmax)
            for j in range(GROUP_SIZE):
                if j == i:
                    continue
                beats = (r[j] >= r[i]) if j < i else (r[j] > r[i])
                rank = rank + jnp.where(beats, 1.0, 0.0)
            chosen = hit * jnp.where(rank < 2.0, 1.0, 0.0)
            e = GROUP_SIZE * g + i
            weights.append(jnp.where(chosen > 0.5, scores[e:e + 1, :], 0.0))
    wsum = functools.reduce(lambda a, b: a + b, weights)
    comb_t = jnp.concatenate([w / wsum for w in weights] + [jnp.zeros((LANES - N_EXPERTS, tm), F32)], axis=0)
    return comb_t.T


def _finish_rows(x, g_ref, mod_refs, router_refs, out_refs, h_dtype):
    h = _rms_rows(x, g_ref[...])
    if mod_refs is not None:
        a_ref, s_ref = mod_refs
        h = h * (1.0 + a_ref[0]) + s_ref[0]
    h_ref = out_refs[0]
    h_ref[...] = h.astype(h_dtype)
    if router_refs is not None:
        rwt_ref, rb_ref = router_refs
        out_refs[1][...] = _route(h, rwt_ref[...], rb_ref[...])


def _rownorm_kernel(*refs, has_res, has_mod, out_x, h_dtype):
    refs = list(refs)
    x_ref = refs.pop(0)
    x = x_ref[...]
    if has_res:
        y_ref = refs.pop(0)
        gate_ref = refs.pop(0)
        x = x + gate_ref[0] * y_ref[...]
    g_ref = refs.pop(0)
    mod_refs = (refs.pop(0), refs.pop(0)) if has_mod else None
    if out_x:
        refs.pop(0)[...] = x
    _finish_rows(x, g_ref, mod_refs, None, refs, h_dtype)


def _mod_spec(n_mod, rows, tm, d):
    per = (rows // n_mod) // tm
    return pl.BlockSpec((1, 1, d), lambda i: (i // per, 0, 0))


def _rownorm(x, g, *, res=None, mod=None, out_x=False, h_dtype=BF16, tm=512):
    rows, d = x.shape
    row_spec = pl.BlockSpec((tm, d), lambda i: (i, 0))
    args, specs = [x], [row_spec]
    if res is not None:
        y, gate = res
        args += [y, gate]
        specs += [row_spec, _mod_spec(gate.shape[0], rows, tm, d)]
    args.append(g.reshape(1, d))
    specs.append(pl.BlockSpec((1, d), lambda i: (0, 0)))
    if mod is not None:
        for m in mod:
            args.append(m)
            specs.append(_mod_spec(m.shape[0], rows, tm, d))
    out_shape, out_specs = [], []
    if out_x:
        out_shape.append(jax.ShapeDtypeStruct((rows, d), F32))
        out_specs.append(row_spec)
    out_shape.append(jax.ShapeDtypeStruct((rows, d), h_dtype))
    out_specs.append(row_spec)
    outs = pl.pallas_call(
        functools.partial(_rownorm_kernel, has_res=res is not None, has_mod=mod is not None, out_x=out_x,
                          h_dtype=h_dtype),
        grid=(rows // tm,), in_specs=specs, out_specs=out_specs, out_shape=out_shape,
        compiler_params=_cparams("parallel"), name="rownorm",
    )(*args)
    return outs if out_x else outs[0]


def _mm_kernel(a_ref, w_ref, o_ref):
    o_ref[...] = _dot(a_ref[...], w_ref[...]).astype(o_ref.dtype)


def _matmul(a, w, *, out_dtype=F32, tm=512, tn=512):
    m, k = a.shape
    n = w.shape[1]
    tn = min(tn, n)
    return pl.pallas_call(
        _mm_kernel, grid=(m // tm, n // tn),
        in_specs=[pl.BlockSpec((tm, k), lambda i, j: (i, 0)), pl.BlockSpec((k, tn), lambda i, j: (0, j))],
        out_specs=pl.BlockSpec((tm, tn), lambda i, j: (i, j)),
        out_shape=jax.ShapeDtypeStruct((m, n), out_dtype),
        compiler_params=_cparams("parallel", "parallel"), name="matmul",
    )(a, w)


def _head_norm(x, g, n_heads, dv):
    outs = []
    for h in range(n_heads):
        xs = x[:, h * dv:(h + 1) * dv]
        ms = jnp.mean(xs * xs, axis=-1, keepdims=True)
        outs.append(xs * lax.rsqrt(ms + EPS) * g[:, h * dv:(h + 1) * dv])
    return jnp.concatenate(outs, axis=1)


def _proj_kernel(*refs, pre):
    refs = list(refs)
    if pre == "plain":
        a = refs.pop(0)[...]
    else:
        f_ref, b_ref, p_ref, hg_ref = refs.pop(0), refs.pop(0), refs.pop(0), refs.pop(0)
        hsum = f_ref[...] + b_ref[...]
        if pre == "mlstm":
            a = _sigmoid(p_ref[...]) * _head_norm(hsum, hg_ref[...], ML_HEADS, ML_DV)
        else:
            a = _head_norm(hsum, hg_ref[...], GLA_HEADS, GLA_DV) * _silu(p_ref[...])
        a = a.astype(BF16)
    w_ref, x_ref, gate_ref, g_ref, a_ref, s_ref, rwt_ref, rb_ref = refs[:8]
    x = x_ref[...] + gate_ref[0] * _dot(a, w_ref[...])
    refs[8][...] = x
    _finish_rows(x, g_ref, (a_ref, s_ref), (rwt_ref, rb_ref), refs[9:], BF16)


def _proj(pre, pre_args, w_out, x, gate, g, scale, shift, rwt, rb, *, tm=256):
    rows, d = x.shape
    k = w_out.shape[0]
    row_spec = pl.BlockSpec((tm, d), lambda i: (i, 0))
    if pre == "plain":
        args, specs = [pre_args[0]], [pl.BlockSpec((tm, k), lambda i: (i, 0))]
    else:
        hf, hb, p, col_block, hg = pre_args
        wide = pl.BlockSpec((tm, k), lambda i: (i, 0))
        args = [hf, hb, p, hg.reshape(1, k)]
        specs = [wide, wide, pl.BlockSpec((tm, k), lambda i: (i, col_block)), pl.BlockSpec((1, k), lambda i: (0, 0))]
    args += [w_out, x, gate, g.reshape(1, d), scale, shift, rwt, rb]
    specs += [pl.BlockSpec((k, d), lambda i: (0, 0)), row_spec, _mod_spec(gate.shape[0], rows, tm, d),
              pl.BlockSpec((1, d), lambda i: (0, 0)), _mod_spec(scale.shape[0], rows, tm, d),
              _mod_spec(shift.shape[0], rows, tm, d), pl.BlockSpec(rwt.shape, lambda i: (0, 0)),
              pl.BlockSpec(rb.shape, lambda i: (0, 0))]
    return pl.pallas_call(
        functools.partial(_proj_kernel, pre=pre), grid=(rows // tm,), in_specs=specs,
        out_specs=[row_spec, row_spec, pl.BlockSpec((tm, LANES), lambda i: (i, 0))],
        out_shape=[jax.ShapeDtypeStruct((rows, d), F32), jax.ShapeDtypeStruct((rows, d), BF16),
                   jax.ShapeDtypeStruct((rows, LANES), F32)],
        compiler_params=_cparams("parallel"), name="proj_" + pre,
    )(*args)


def _moe_kernel(x_ref, comb_ref, wg_ref, wu_ref, wd_ref, o_ref):
    e = pl.program_id(1)
    x = x_ref[...]
    gate = _dot(x, wg_ref[0])
    up = _dot(x, wu_ref[0])
    comb = comb_ref[...]
    lane = lax.broadcasted_iota(jnp.int32, comb.shape, 1)
    ce = jnp.sum(jnp.where(lane == e, comb, 0.0), axis=1, keepdims=True)
    hid = _silu(gate) * up * ce
    y = _dot(hid.astype(BF16), wd_ref[0])

    @pl.when(e == 0)
    def _():
        o_ref[...] = y

    @pl.when(e > 0)
    def _():
        o_ref[...] += y


def _moe(h, comb, wg, wu, wd, *, tm=1024):
    rows, d = h.shape
    n_e, _, f = wg.shape
    return pl.pallas_call(
        _moe_kernel, grid=(rows // tm, n_e),
        in_specs=[pl.BlockSpec((tm, d), lambda i, e: (i, 0)), pl.BlockSpec((tm, LANES), lambda i, e: (i, 0)),
                  pl.BlockSpec((1, d, f), lambda i, e: (e, 0, 0)), pl.BlockSpec((1, d, f), lambda i, e: (e, 0, 0)),
                  pl.BlockSpec((1, f, d), lambda i, e: (e, 0, 0))],
        out_specs=pl.BlockSpec((tm, d), lambda i, e: (i, 0)),
        out_shape=jax.ShapeDtypeStruct((rows, d), F32),
        compiler_params=_cparams("parallel", "arbitrary"), name="moe",
    )(h, comb, wg, wu, wd)


def _softmax_av(scores, values, sink_col):
    m = sink_col
    for s in scores:
        m = jnp.maximum(m, jnp.max(s, axis=-1, keepdims=True))
    den = jnp.exp(sink_col - m)
    acc = None
    for s, v in zip(scores, values):
        p = jnp.exp(s - m)
        den = den + jnp.sum(p, axis=-1, keepdims=True)
        pv = _dot(p.astype(BF16), v)
        acc = pv if acc is None else acc + pv
    return acc / den


def _sink_column(sink_ref, kv, rows):
    return jnp.concatenate([jnp.full((rows, 1), sink_ref[kv * ATT_GROUP + g], F32) for g in range(ATT_GROUP)], axis=0)


def _attn_ctx_kernel(sink_ref, qkv_ref, o_ref):
    t = qkv_ref.shape[0]
    qw = ATT_HEADS * HEAD_DIM
    kw = ATT_KV * HEAD_DIM
    heads_out = []
    for kv in range(ATT_KV):
        q = jnp.concatenate(
            [qkv_ref[:, (kv * ATT_GROUP + g) * HEAD_DIM:(kv * ATT_GROUP + g + 1) * HEAD_DIM] for g in range(ATT_GROUP)],
            axis=0).astype(BF16)
        k = qkv_ref[:, qw + kv * HEAD_DIM:qw + (kv + 1) * HEAD_DIM].astype(BF16)
        v = qkv_ref[:, qw + kw + kv * HEAD_DIM:qw + kw + (kv + 1) * HEAD_DIM].astype(BF16)
        s = _dot_nt(q, k) * HEAD_DIM ** -0.5
        o = _softmax_av([s], [v], _sink_column(sink_ref, kv, t))
        heads_out += [o[g * t:(g + 1) * t] for g in range(ATT_GROUP)]
    o_ref[...] = jnp.concatenate(heads_out, axis=1).astype(o_ref.dtype)


def _attn_ctx(qkv, sink, n_seq, seq_len):
    rows, cols = qkv.shape
    return pl.pallas_call(
        _attn_ctx_kernel, grid=(n_seq,),
        in_specs=[pl.BlockSpec(memory_space=pltpu.SMEM), pl.BlockSpec((seq_len, cols), lambda b: (b, 0))],
        out_specs=pl.BlockSpec((seq_len, ATT_HEADS * HEAD_DIM), lambda b: (b, 0)),
        out_shape=jax.ShapeDtypeStruct((rows, ATT_HEADS * HEAD_DIM), BF16),
        compiler_params=_cparams("parallel"), name="attn_context",
    )(sink, qkv)


def _rope_block(x, cos, sin_signed):
    lane = lax.broadcasted_iota(jnp.int32, x.shape, 1)
    nf = HEAD_DIM // 4
    partner = jnp.where((lane % (2 * nf)) < nf, pltpu.roll(x, LANES - nf, axis=1), pltpu.roll(x, nf, axis=1))
    return x * cos + partner * sin_signed


def _attn_lat_kernel(sink_ref, qkv_ref, ck_ref, cv_ref, cos_ref, sin_ref, o_ref, k_scr):
    i = pl.program_id(1)
    t = qkv_ref.shape[0]
    qw = ATT_HEADS * HEAD_DIM
    kw = ATT_KV * HEAD_DIM
    span = Q_BLOCK + 2 * WINDOW

    @pl.when(i == 0)
    def _():
        for c in range(kw // LANES):
            blk = qkv_ref[:, qw + c * LANES:qw + (c + 1) * LANES]
            k_scr[:, c * LANES:(c + 1) * LANES] = _rope_block(blk, cos_ref[...], sin_ref[...]).astype(BF16)

    r0 = pl.multiple_of(i * Q_BLOCK, Q_BLOCK)
    ws = pl.multiple_of(jnp.clip(r0 - WINDOW, 0, t - span), Q_BLOCK)
    cos_q = cos_ref[pl.ds(r0, Q_BLOCK), :]
    sin_q = sin_ref[pl.ds(r0, Q_BLOCK), :]
    qpos = r0 + lax.broadcasted_iota(jnp.int32, (Q_BLOCK, span), 0)
    kpos = ws + lax.broadcasted_iota(jnp.int32, (Q_BLOCK, span), 1)
    band = jnp.abs(qpos - kpos) <= WINDOW
    band = jnp.concatenate([band] * ATT_GROUP, axis=0)
    heads_out = []
    for kv in range(ATT_KV):
        heads = []
        for g in range(ATT_GROUP):
            h = kv * ATT_GROUP + g
            c, half = divmod(h * HEAD_DIM, LANES)
            blk = _rope_block(qkv_ref[pl.ds(r0, Q_BLOCK), c * LANES:(c + 1) * LANES], cos_q, sin_q)
            heads.append(blk[:, half:half + HEAD_DIM])
        q = jnp.concatenate(heads, axis=0).astype(BF16)
        ck = ck_ref[0, :, kv * HEAD_DIM:(kv + 1) * HEAD_DIM].astype(BF16)
        cv = cv_ref[0, :, kv * HEAD_DIM:(kv + 1) * HEAD_DIM].astype(BF16)
        kwin = k_scr[pl.ds(ws, span), kv * HEAD_DIM:(kv + 1) * HEAD_DIM]
        vwin = qkv_ref[pl.ds(ws, span), qw + kw + kv * HEAD_DIM:qw + kw + (kv + 1) * HEAD_DIM].astype(BF16)
        s_ctx = _dot_nt(q, ck) * HEAD_DIM ** -0.5
        s_win = jnp.where(band, _dot_nt(q, kwin) * HEAD_DIM ** -0.5, -jnp.inf)
        o = _softmax_av([s_ctx, s_win], [cv, vwin], _sink_column(sink_ref, kv, Q_BLOCK))
        heads_out += [o[g * Q_BLOCK:(g + 1) * Q_BLOCK] for g in range(ATT_GROUP)]
    o_ref[...] = jnp.concatenate(heads_out, axis=1).astype(o_ref.dtype)


def _rope_tables(seq_len):
    pos = jnp.arange(seq_len, dtype=jnp.int32)
    row = (pos // GRID_W).astype(F32)
    col = (pos % GRID_W).astype(F32)
    nf = HEAD_DIM // 4
    inv = ROPE_BASE ** (-jnp.arange(nf, dtype=F32) / nf)
    ang_r = row[:, None] * inv[None, :]
    ang_c = col[:, None] * inv[None, :]
    cos_h = jnp.concatenate([jnp.cos(ang_r), jnp.cos(ang_r), jnp.cos(ang_c), jnp.cos(ang_c)], axis=1)
    sin_h = jnp.concatenate([-jnp.sin(ang_r), jnp.sin(ang_r), -jnp.sin(ang_c), jnp.sin(ang_c)], axis=1)
    reps = LANES // HEAD_DIM
    return jnp.tile(cos_h, (1, reps)), jnp.tile(sin_h, (1, reps))


def _attn_lat(qkv, cache_k, cache_v, sink, n_seq, seq_len):
    rows, cols = qkv.shape
    past = cache_k.shape[1]
    kw = ATT_KV * HEAD_DIM
    cos, sin = _rope_tables(seq_len)
    return pl.pallas_call(
        _attn_lat_kernel, grid=(n_seq, seq_len // Q_BLOCK),
        in_specs=[pl.BlockSpec(memory_space=pltpu.SMEM),
                  pl.BlockSpec((seq_len, cols), lambda b, i: (b, 0)),
                  pl.BlockSpec((1, past, kw), lambda b, i: (b, 0, 0)),
                  pl.BlockSpec((1, past, kw), lambda b, i: (b, 0, 0)),
                  pl.BlockSpec((seq_len, LANES), lambda b, i: (0, 0)),
                  pl.BlockSpec((seq_len, LANES), lambda b, i: (0, 0))],
        out_specs=pl.BlockSpec((Q_BLOCK, ATT_HEADS * HEAD_DIM), lambda b, i: (b * (seq_len // Q_BLOCK) + i, 0)),
        out_shape=jax.ShapeDtypeStruct((rows, ATT_HEADS * HEAD_DIM), BF16),
        scratch_shapes=[pltpu.VMEM((seq_len, kw), BF16)],
        compiler_params=_cparams("parallel", "arbitrary"), name="attn_latent",
    )(sink, qkv, cache_k.reshape(n_seq, past, kw), cache_v.reshape(n_seq, past, kw), cos, sin)


def _ml_gates_kernel(h_ref, w_ref, wt_ref, b_ref, bt_ref, g_ref, gt_ref):
    h = h_ref[...]
    g_ref[...] = _dot(h, w_ref[...]) + b_ref[...]
    gt_ref[...] = _dot_nt(wt_ref[...], h) + bt_ref[...]


def _ml_gates(h, w_gates, b_gates, *, tm=512):
    rows, d = h.shape
    ng = w_gates.shape[1]
    w_pad = jnp.pad(w_gates, ((0, 0), (0, LANES - ng))).astype(BF16)
    b_pad = jnp.pad(b_gates, (0, LANES - ng)).reshape(1, LANES)
    return pl.pallas_call(
        _ml_gates_kernel, grid=(rows // tm,),
        in_specs=[pl.BlockSpec((tm, d), lambda i: (i, 0)), pl.BlockSpec((d, LANES), lambda i: (0, 0)),
                  pl.BlockSpec((ng, d), lambda i: (0, 0)), pl.BlockSpec((1, LANES), lambda i: (0, 0)),
                  pl.BlockSpec((ng, 1), lambda i: (0, 0))],
        out_specs=[pl.BlockSpec((tm, LANES), lambda i: (i, 0)), pl.BlockSpec((ng, tm), lambda i: (0, i))],
        out_shape=[jax.ShapeDtypeStruct((rows, LANES), F32), jax.ShapeDtypeStruct((ng, rows), F32)],
        compiler_params=_cparams("parallel"), name="mlstm_gates",
    )(h, w_pad, w_gates.T.astype(BF16), b_pad, b_gates.reshape(ng, 1))


def _ml_conv_kernel(p_ref, w_ref, o_ref, *, k_scale):
    j = pl.program_id(1)
    x = p_ref[...]
    t = x.shape[0]
    row = lax.broadcasted_iota(jnp.int32, x.shape, 0)
    prev = jnp.where(row == 0, 0.0, pltpu.roll(x, 1, axis=0))
    nxt = jnp.where(row == t - 1, 0.0, pltpu.roll(x, t - 1, axis=0))
    y = prev * w_ref[0:1, :] + x * w_ref[1:2, :] + nxt * w_ref[2:3, :]
    scale = jnp.where(j >= pl.num_programs(1) // 2, k_scale, 1.0).astype(F32)
    o_ref[...] = (_silu(y) * scale).astype(o_ref.dtype)


def _ml_conv(p, conv_w, n_seq, seq_len, *, tn=512):
    rows = p.shape[0]
    width = conv_w.shape[1]
    return pl.pallas_call(
        functools.partial(_ml_conv_kernel, k_scale=ML_DK ** -0.5), grid=(n_seq, width // tn),
        in_specs=[pl.BlockSpec((seq_len, tn), lambda b, j: (b, j)), pl.BlockSpec((3, tn), lambda b, j: (0, j))],
        out_specs=pl.BlockSpec((seq_len, tn), lambda b, j: (b, j)),
        out_shape=jax.ShapeDtypeStruct((rows, width), BF16),
        compiler_params=_cparams("parallel", "parallel"), name="mlstm_conv",
    )(p, conv_w)


def _ml_scan_kernel(*refs, zero_init):
    refs = list(refs)
    dirs = [tuple(refs[0:5]), tuple(refs[5:10])]
    refs = refs[10:]
    if not zero_init:
        c0_ref, n0_ref, m0_ref = refs[:3]
        refs = refs[3:]
    hf_ref, hb_ref, c_ref, n_ref, m_ref = refs
    h_out = (hf_ref, hb_ref)
    c = pl.program_id(1)
    last = pl.num_programs(1) - 1

    @pl.when(c == 0)
    def _():
        if zero_init:
            c_ref[...] = jnp.zeros_like(c_ref)
            n_ref[...] = jnp.zeros_like(n_ref)
            m_ref[...] = jnp.zeros_like(m_ref)
        else:
            c_ref[...] = c0_ref[...]
            n_ref[...] = n0_ref[...]
            m_ref[...] = m0_ref[...]

    length = hf_ref.shape[0]
    ti = lax.broadcasted_iota(jnp.int32, (length, length), 0)
    si = lax.broadcasted_iota(jnp.int32, (length, length), 1)
    for d in range(2):
        q_ref, k_ref, v_ref, g_ref, gt_ref = dirs[d]
        causal = (ti >= si) if d == 0 else (ti <= si)
        tri = jnp.where(causal, 1.0, 0.0).astype(F32)
        f_col = _log_sigmoid(g_ref[...])
        f_row = _log_sigmoid(gt_ref[...])
        b_col = jnp.dot(tri, f_col, precision=HI, preferred_element_type=F32)
        b_row = _dot_nt(f_row, tri, precision=HI)
        edge = length - 1 if d == 0 else 0
        for h in range(ML_HEADS):
            ji = d * 2 * ML_HEADS + h
            jf = ji + ML_HEADS
            bc = b_col[:, jf:jf + 1]
            br = b_row[jf:jf + 1, :]
            i_row = gt_ref[ji:ji + 1, :]
            i_col = g_ref[:, ji:ji + 1]
            m_prev = m_ref[0, d, h][:, 0:1]
            q = q_ref[:, h * ML_DK:(h + 1) * ML_DK]
            k = k_ref[:, h * ML_DK:(h + 1) * ML_DK]
            v = v_ref[:, h * ML_DV:(h + 1) * ML_DV].astype(BF16)
            cst = c_ref[0, d, h]
            nst = n_ref[0, d, h]
            dmat = jnp.where(causal, bc - br + i_row, -jnp.inf)
            inter = bc + m_prev
            m_t = jnp.maximum(inter, jnp.max(dmat, axis=1, keepdims=True))
            w = jnp.exp(dmat - m_t)
            qk = _dot_nt(q, k) * w
            sc = jnp.exp(inter - m_t)
            num = sc * _dot_nt(q, cst.astype(BF16)) + _dot(qk.astype(BF16), v)
            den = sc * jnp.sum(q.astype(F32) * nst, axis=1, keepdims=True) + jnp.sum(qk, axis=1, keepdims=True)
            h_out[d][:, h * ML_DV:(h + 1) * ML_DV] = num / jnp.maximum(jnp.abs(den), jnp.exp(-m_t))
            b_last = br[:, edge:edge + 1]
            wlog_row = b_last - br + i_row
            wlog_col = b_last - bc + i_col
            m_new = jnp.maximum(b_last + m_prev, jnp.max(wlog_row, axis=1, keepdims=True))
            decay = jnp.exp(b_last + m_prev - m_new)
            ws = jnp.exp(wlog_col - m_new)
            vw = (ws * v_ref[:, h * ML_DV:(h + 1) * ML_DV]).astype(BF16)
            c_ref[0, d, h] = decay * cst + _dot_tn(vw, k)
            n_ref[0, d, h] = decay * nst + jnp.sum(ws * k.astype(F32), axis=0, keepdims=True)
            m_ref[0, d, h] = jnp.broadcast_to(m_new, (1, ML_DK))


def _ml_scan(qk, p, g, gt, state, n_seq, seq_len):
    rows = qk.shape[0]
    length = min(ML_CHUNK, seq_len)
    nc = seq_len // length
    qw = ML_HEADS * ML_DK
    vw = ML_HEADS * ML_DV
    ng = gt.shape[0]

    def fwd(b, c):
        return b * nc + c

    def bwd(b, c):
        return b * nc + nc - 1 - c

    args, specs = [], []
    for pos in (fwd, bwd):
        args += [qk, qk, p, g, gt]
        specs += [pl.BlockSpec((length, qw), lambda b, c, pos=pos: (pos(b, c), 0)),
                  pl.BlockSpec((length, qw), lambda b, c, pos=pos: (pos(b, c), 1)),
                  pl.BlockSpec((length, vw), lambda b, c, pos=pos: (pos(b, c), 2 * qw // vw)),
                  pl.BlockSpec((length, LANES), lambda b, c, pos=pos: (pos(b, c), 0)),
                  pl.BlockSpec((ng, length), lambda b, c, pos=pos: (0, pos(b, c)))]
    c_spec = pl.BlockSpec((1, 2, ML_HEADS, ML_DV, ML_DK), lambda b, c: (b, 0, 0, 0, 0))
    n_spec = pl.BlockSpec((1, 2, ML_HEADS, 1, ML_DK), lambda b, c: (b, 0, 0, 0, 0))
    zero_init = state is None
    if not zero_init:
        c0, n0, m0 = state
        args += [c0, n0.reshape(n_seq, 2, ML_HEADS, 1, ML_DK),
                 jnp.broadcast_to(m0[..., None, None], (n_seq, 2, ML_HEADS, 1, ML_DK))]
        specs += [c_spec, n_spec, n_spec]
    hf, hb, c_fin, n_fin, m_fin = pl.pallas_call(
        functools.partial(_ml_scan_kernel, zero_init=zero_init), grid=(n_seq, nc), in_specs=specs,
        out_specs=[pl.BlockSpec((length, vw), lambda b, c: (fwd(b, c), 0)),
                   pl.BlockSpec((length, vw), lambda b, c: (bwd(b, c), 0)), c_spec, n_spec, n_spec],
        out_shape=[jax.ShapeDtypeStruct((rows, vw), F32), jax.ShapeDtypeStruct((rows, vw), F32),
                   jax.ShapeDtypeStruct((n_seq, 2, ML_HEADS, ML_DV, ML_DK), F32),
                   jax.ShapeDtypeStruct((n_seq, 2, ML_HEADS, 1, ML_DK), F32),
                   jax.ShapeDtypeStruct((n_seq, 2, ML_HEADS, 1, ML_DK), F32)],
        compiler_params=_cparams("parallel", "arbitrary"), name="mlstm_scan",
    )(*args)
    return hf, hb, (c_fin, n_fin[:, :, :, 0, :], m_fin[:, :, :, 0, 0])


def _gla_scan_kernel(*refs, zero_init):
    refs = list(refs)
    dirs = [tuple(refs[0:4]), tuple(refs[4:8])]
    w2_ref, ba_ref = refs[8:10]
    refs = refs[10:]
    if not zero_init:
        s0_ref = refs.pop(0)
    of_ref, ob_ref, s_ref, st_scr, la_scr = refs
    o_out = (of_ref, ob_ref)
    c = pl.program_id(1)
    last = pl.num_programs(1) - 1
    kw = GLA_HEADS * GLA_DK
    n_sub = of_ref.shape[0] // GLA_SUB

    @pl.when(c == 0)
    def _():
        for d in range(2):
            for h in range(GLA_HEADS):
                st_scr[d, h] = jnp.zeros((GLA_DV, GLA_DK), F32) if zero_init else s0_ref[0, d, h].T

    for d in range(2):
        u = dirs[d][3][...].astype(BF16)
        z = _dot(u, w2_ref[:, d * kw:(d + 1) * kw]) + ba_ref[:, d * kw:(d + 1) * kw]
        la_scr[d] = _log_sigmoid(z) / GLA_TAU

    ti = lax.broadcasted_iota(jnp.int32, (GLA_SUB, GLA_SUB), 0)
    si = lax.broadcasted_iota(jnp.int32, (GLA_SUB, GLA_SUB), 1)
    t_col = lax.broadcasted_iota(jnp.int32, (GLA_SUB, 1), 0)
    s_lane = lax.broadcasted_iota(jnp.int32, (GLA_SUB, GLA_SUB), 1)

    def sub_chunk(j, carry):
        for d in range(2):
            q_ref, k_ref, v_ref, _ = dirs[d]
            r0 = pl.multiple_of((j if d == 0 else n_sub - 1 - j) * GLA_SUB, GLA_SUB)
            tri = jnp.where((ti >= si) if d == 0 else (ti <= si), 1.0, 0.0).astype(F32)
            bc_all = jnp.dot(tri, la_scr[d, pl.ds(r0, GLA_SUB), :], precision=HI, preferred_element_type=F32)
            edge = GLA_SUB - 1 if d == 0 else 0
            for h in range(GLA_HEADS):
                bc = bc_all[:, h * GLA_DK:(h + 1) * GLA_DK]
                q = q_ref[pl.ds(r0, GLA_SUB), h * GLA_DK:(h + 1) * GLA_DK] * GLA_DK ** -0.5
                k = k_ref[pl.ds(r0, GLA_SUB), h * GLA_DK:(h + 1) * GLA_DK]
                v = v_ref[pl.ds(r0, GLA_SUB), h * GLA_DV:(h + 1) * GLA_DV].astype(BF16)
                a = jnp.zeros((GLA_SUB, GLA_SUB), F32)
                for s in range(GLA_SUB):
                    valid = (t_col >= s) if d == 0 else (t_col <= s)
                    decay = jnp.exp(jnp.where(valid, bc - bc[s:s + 1, :], -jnp.inf))
                    col = jnp.sum(q * (k[s:s + 1, :] * decay), axis=1, keepdims=True)
                    a = a + jnp.where(s_lane == s, col, 0.0)
                st = st_scr[d, h]
                o = _dot(a.astype(BF16), v) + _dot_nt((q * jnp.exp(bc)).astype(BF16), st.astype(BF16))
                o_out[d][pl.ds(r0, GLA_SUB), h * GLA_DV:(h + 1) * GLA_DV] = o
                b_last = bc[edge:edge + 1, :]
                k_dec = (k * jnp.exp(b_last - bc)).astype(BF16)
                st_scr[d, h] = jnp.exp(b_last) * st + _dot_tn(v, k_dec)
        return carry

    lax.fori_loop(0, n_sub, sub_chunk, 0)

    @pl.when(c == last)
    def _():
        for d in range(2):
            for h in range(GLA_HEADS):
                s_ref[0, d, h] = st_scr[d, h].T


def _gla_scan(p, u, w2, b_a, state, n_seq, seq_len):
    rows = p.shape[0]
    length = min(GLA_BLOCK, seq_len)
    nc = seq_len // length
    kw = GLA_HEADS * GLA_DK
    vw = GLA_HEADS * GLA_DV

    def fwd(b, c):
        return b * nc + c

    def bwd(b, c):
        return b * nc + nc - 1 - c

    args, specs = [], []
    for pos in (fwd, bwd):
        args += [p, p, p, u]
        specs += [pl.BlockSpec((length, kw), lambda b, c, pos=pos: (pos(b, c), 0)),
                  pl.BlockSpec((length, kw), lambda b, c, pos=pos: (pos(b, c), 1)),
                  pl.BlockSpec((length, vw), lambda b, c, pos=pos: (pos(b, c), 2 * kw // vw)),
                  pl.BlockSpec((length, LANES), lambda b, c, pos=pos: (pos(b, c), 0))]
    args += [w2, b_a]
    specs += [pl.BlockSpec(w2.shape, lambda b, c: (0, 0)), pl.BlockSpec(b_a.shape, lambda b, c: (0, 0))]
    s_spec = pl.BlockSpec((1, 2, GLA_HEADS, GLA_DK, GLA_DV), lambda b, c: (b, 0, 0, 0, 0))
    zero_init = state is None
    if not zero_init:
        args.append(state)
        specs.append(s_spec)
    return pl.pallas_call(
        functools.partial(_gla_scan_kernel, zero_init=zero_init), grid=(n_seq, nc), in_specs=specs,
        out_specs=[pl.BlockSpec((length, vw), lambda b, c: (fwd(b, c), 0)),
                   pl.BlockSpec((length, vw), lambda b, c: (bwd(b, c), 0)), s_spec],
        out_shape=[jax.ShapeDtypeStruct((rows, vw), F32), jax.ShapeDtypeStruct((rows, vw), F32),
                   jax.ShapeDtypeStruct((n_seq, 2, GLA_HEADS, GLA_DK, GLA_DV), F32)],
        scratch_shapes=[pltpu.VMEM((2, GLA_HEADS, GLA_DV, GLA_DK), F32), pltpu.VMEM((2, length, kw), F32)],
        compiler_params=_cparams("parallel", "arbitrary"), name="gla_scan",
    )(*args)


def kernel(x_prompt, x_sample, cache_k_0, cache_v_0, state_mlstm_C_1, state_mlstm_n_1, state_mlstm_m_1, state_gla_S_2, cache_k_3, cache_v_3, c, c_ctx, w_mod, b_mod, norm1_g, norm2_g, final_g, router_w, router_b, moe_wg, moe_wu, moe_wd, attn0_w_qkv, attn0_sink, attn0_w_o, mlstm1_w_in, mlstm1_b_gates, mlstm1_conv, mlstm1_norm_g, mlstm1_w_out, gla2_w_in, gla2_w_a1, gla2_w_a2, gla2_b_a, gla2_norm_g, gla2_w_out, attn3_w_qkv, attn3_sink, attn3_w_o):
    n_ctx, ctx_len, d = x_prompt.shape
    n_lat, lat_len, _ = x_sample.shape
    depth = w_mod.shape[0]

    cvec = jnp.concatenate([c_ctx[None, :], c, jnp.zeros((8 - 1 - n_lat, d), F32)], axis=0)
    mod = _modulation(cvec, w_mod, b_mod).reshape(depth, 8, 6, 1, d)

    def mods(layer, kind, latent):
        return mod[layer, 1:1 + n_lat, kind] if latent else mod[layer, 0:1, kind]

    rwt = router_w.T
    rb = router_b.reshape(-1, 1)
    wg = moe_wg.astype(BF16)
    wu = moe_wu.astype(BF16)
    wd = moe_wd.astype(BF16)
    attn_w = {0: (attn0_w_qkv.astype(BF16), attn0_sink, attn0_w_o.astype(BF16), cache_k_0, cache_v_0),
              3: (attn3_w_qkv.astype(BF16), attn3_sink, attn3_w_o.astype(BF16), cache_k_3, cache_v_3)}
    ml_qw = ML_HEADS * ML_DK
    ml_vw = ML_HEADS * ML_DV
    ml_main = 2 * ml_qw + 2 * ml_vw
    ml_w_main = mlstm1_w_in[:, :ml_main].astype(BF16)
    ml_w_gates = mlstm1_w_in[:, ml_main:]
    ml_w_out = mlstm1_w_out.astype(BF16)
    gla_kw = GLA_HEADS * GLA_DK
    gla_w_in = gla2_w_in.astype(BF16)
    gla_w_a1 = jnp.pad(jnp.concatenate([gla2_w_a1[0], gla2_w_a1[1]], axis=1),
                       ((0, 0), (0, LANES - 2 * GLA_RANK))).astype(BF16)
    gla_w2 = jnp.zeros((LANES, 2 * gla_kw), F32)
    gla_w2 = gla_w2.at[:GLA_RANK, :gla_kw].set(gla2_w_a2[0]).at[GLA_RANK:2 * GLA_RANK, gla_kw:].set(gla2_w_a2[1])
    gla_w2 = gla_w2.astype(BF16)
    gla_ba = gla2_b_a.reshape(1, 2 * gla_kw)
    gla_w_out = gla2_w_out.astype(BF16)

    def run_stream(x, n_seq, seq_len, latent):
        states = []
        h = _rownorm(x, norm1_g[0], mod=(mods(0, 1, latent), mods(0, 0, latent)))
        for layer in range(depth):
            tail = (mods(layer, 2, latent), norm2_g[layer], mods(layer, 4, latent), mods(layer, 3, latent), rwt, rb)
            kind = layer % 3
            if kind == 0:
                w_qkv, sink, w_o, ck, cv = attn_w[layer]
                qkv = _matmul(h, w_qkv)
                if latent:
                    att = _attn_lat(qkv, ck, cv, sink, n_seq, seq_len)
                else:
                    att = _attn_ctx(qkv, sink, n_seq, seq_len)
                    qw = ATT_HEADS * HEAD_DIM
                    kw = ATT_KV * HEAD_DIM
                    states.append(qkv[:, qw:qw + kw].reshape(n_seq, seq_len, ATT_KV, HEAD_DIM))
                    states.append(qkv[:, qw + kw:].reshape(n_seq, seq_len, ATT_KV, HEAD_DIM))
                x, h2, comb = _proj("plain", (att,), w_o, x, *tail)
            elif kind == 1:
                p = _matmul(h, ml_w_main)
                g, gt = _ml_gates(h, ml_w_gates, mlstm1_b_gates)
                qk = _ml_conv(p, mlstm1_conv, n_seq, seq_len)
                st = (state_mlstm_C_1, state_mlstm_n_1, state_mlstm_m_1) if latent else None
                hf, hb, fin = _ml_scan(qk, p, g, gt, st, n_seq, seq_len)
                if not latent:
                    states.extend(fin)
                x, h2, comb = _proj("mlstm", (hf, hb, p, (2 * ml_qw + ml_vw) // ml_vw, mlstm1_norm_g), ml_w_out,
                                    x, *tail)
            else:
                p = _matmul(h, gla_w_in)
                u = _matmul(h, gla_w_a1)
                of, ob, s_fin = _gla_scan(p, u, gla_w2, gla_ba, state_gla_S_2 if latent else None, n_seq, seq_len)
                if not latent:
                    states.append(s_fin)
                gla_vw = GLA_HEADS * GLA_DV
                x, h2, comb = _proj("gla", (of, ob, p, (2 * gla_kw + gla_vw) // gla_vw, gla2_norm_g), gla_w_out,
                                    x, *tail)
            y = _moe(h2, comb, wg[layer], wu[layer], wd[layer])
            gate2 = mods(layer, 5, latent)
            if layer + 1 < depth:
                x, h = _rownorm(x, norm1_g[layer + 1], res=(y, gate2),
                                mod=(mods(layer + 1, 1, latent), mods(layer + 1, 0, latent)), out_x=True)
            else:
                out = _rownorm(x, final_g, res=(y, gate2), h_dtype=F32)
        return out.reshape(n_seq, seq_len, d), states

    y_prompt, new_state = run_stream(x_prompt.reshape(n_ctx * ctx_len, d), n_ctx, ctx_len, False)
    y_sample, _ = run_stream(x_sample.reshape(n_lat * lat_len, d), n_lat, lat_len, True)
    return (y_prompt, y_sample, *new_state)
```

```python
import functools

import jax
import jax.numpy as jnp
from jax import lax
from jax.experimental import pallas as pl
from jax.experimental.pallas import tpu as pltpu

F32 = jnp.float32
BF16 = jnp.bfloat16
HI = lax.Precision.HIGHEST

EPS = 1e-6
GRID_W = 64
ATT_HEADS = 16
ATT_KV = 4
ATT_GROUP = ATT_HEADS // ATT_KV
HEAD_DIM = 64
WINDOW = 128
Q_BLOCK = 128
ROPE_BASE = 10000.0
ML_HEADS = 8
ML_DK = 128
ML_DV = 256
ML_CHUNK = 256
GLA_HEADS = 4
GLA_DK = 128
GLA_DV = 256
GLA_RANK = 16
GLA_TAU = 16.0
GLA_SUB = 16
GLA_BLOCK = 256
N_EXPERTS = 16
N_GROUPS = 4
GROUP_SIZE = N_EXPERTS // N_GROUPS
LANES = 128
VMEM_LIMIT = 56 * 1024 * 1024


def _cparams(*sem):
    return pltpu.CompilerParams(dimension_semantics=sem, vmem_limit_bytes=VMEM_LIMIT)


def _dot(a, b):
    return jnp.dot(a, b, preferred_element_type=F32)


def _dot_nt(a, b, precision=None):
    return lax.dot_general(a, b, (((1,), (1,)), ((), ())), precision=precision, preferred_element_type=F32)


def _dot_tn(a, b):
    return lax.dot_general(a, b, (((0,), (0,)), ((), ())), preferred_element_type=F32)


def _sigmoid(x):
    return 1.0 / (1.0 + jnp.exp(-x))


def _silu(x):
    return x * _sigmoid(x)


def _log_sigmoid(x):
    return jnp.minimum(x, 0.0) - jnp.log(1.0 + jnp.exp(-jnp.abs(x)))


def _rms_rows(x, g):
    ms = jnp.mean(x * x, axis=-1, keepdims=True)
    return x * lax.rsqrt(ms + EPS) * g


def _mod_kernel(c_ref, w_ref, b_ref, o_ref):
    s = _silu(c_ref[...])
    o_ref[0] = _dot(s.astype(BF16), w_ref[0].astype(BF16)) + b_ref[0]


def _modulation(cvec, w_mod, b_mod):
    depth, d, n6 = w_mod.shape
    tn = 1536
    return pl.pallas_call(
        _mod_kernel,
        grid=(depth, n6 // tn),
        in_specs=[pl.BlockSpec((8, d), lambda l, j: (0, 0)),
                  pl.BlockSpec((1, d, tn), lambda l, j: (l, 0, j)),
                  pl.BlockSpec((1, 1, tn), lambda l, j: (l, 0, j))],
        out_specs=pl.BlockSpec((1, 8, tn), lambda l, j: (l, 0, j)),
        out_shape=jax.ShapeDtypeStruct((depth, 8, n6), F32),
        compiler_params=_cparams("parallel", "parallel"),
        name="adaln_modulation",
    )(cvec, w_mod, b_mod.reshape(depth, 1, n6))


def _route(h, rwt, rb, carry):
    tm = h.shape[0]
    logits = _dot_nt(rwt, h, precision=HI)
    scores = _sigmoid(logits)
    sel = scores + rb
    rows = [sel[e:e + 1, :] for e in range(N_EXPERTS)]
    gscore = []
    for g in range(N_GROUPS):
        r = rows[GROUP_SIZE * g:GROUP_SIZE * (g + 1)]
        best = None
        for i in range(GROUP_SIZE):
            for j in range(i + 1, GROUP_SIZE):
                pair = r[i] + r[j]
                best = pair if best is None else jnp.maximum(best, pair)
        gscore.append(best)
    gmax = functools.reduce(jnp.maximum, gscore)
    taken = jnp.zeros_like(gmax)
    weights = []
    picks = []
    for g in range(N_GROUPS):
        hit = jnp.where(gscore[g] == gmax, 1.0, 0.0) * (1.0 - taken)
        taken = taken + hit
        r = rows[GROUP_SIZE * g:GROUP_SIZE * (g + 1)]
        for i in range(GROUP_SIZE):
            rank = jnp.zeros_like(gmax)
            for j in range(GROUP_SIZE):
                if j == i:
                    continue
                beats = (r[j] >= r[i]) if j < i else (r[j] > r[i])
                rank = rank + jnp.where(beats, 1.0, 0.0)
            chosen = hit * jnp.where(rank < 2.0, 1.0, 0.0)
            e = GROUP_SIZE * g + i
            picks.append(chosen)
            weights.append(jnp.where(chosen > 0.5, scores[e:e + 1, :], 0.0))
    wsum = functools.reduce(lambda a, b: a + b, weights)
    e_a = functools.reduce(jnp.minimum, [jnp.where(p > 0.5, float(e), float(N_EXPERTS)) for e, p in enumerate(picks)])
    e_b = functools.reduce(jnp.maximum, [jnp.where(p > 0.5, float(e), -1.0) for e, p in enumerate(picks)])
    picked = jnp.concatenate(picks, axis=0)
    before = (lax.broadcasted_iota(jnp.int32, (tm, tm), 0) < lax.broadcasted_iota(jnp.int32, (tm, tm), 1))
    rank = _dot(picked.astype(BF16), jnp.where(before, 1.0, 0.0).astype(BF16)) + carry
    zero = jnp.zeros_like(wsum)
    r_a, r_b, w_a, w_b = zero, zero, zero, zero
    for e in range(N_EXPERTS):
        is_a = e_a == float(e)
        is_b = e_b == float(e)
        r_a = r_a + jnp.where(is_a, rank[e:e + 1, :], 0.0)
        r_b = r_b + jnp.where(is_b, rank[e:e + 1, :], 0.0)
        w_a = w_a + jnp.where(is_a, weights[e], 0.0)
        w_b = w_b + jnp.where(is_b, weights[e], 0.0)
    meta = jnp.concatenate([e_a, e_b, r_a, r_b, jnp.zeros((4, tm), F32)], axis=0).astype(jnp.int32)
    wcol = jnp.concatenate([w_a / wsum, w_b / wsum, jnp.zeros((LANES - 2, tm), F32)], axis=0).T
    return meta, wcol, carry + jnp.sum(picked, axis=1, keepdims=True)


def _norm_mod(x, g_ref, mod_refs):
    h = _rms_rows(x, g_ref[...])
    if mod_refs is not None:
        a_ref, s_ref = mod_refs
        h = h * (1.0 + a_ref[0]) + s_ref[0]
    return h


def _rownorm_kernel(x_ref, g_ref, a_ref, s_ref, h_ref):
    h_ref[...] = _norm_mod(x_ref[...], g_ref, (a_ref, s_ref)).astype(h_ref.dtype)


def _mod_spec(n_mod, rows, tm, d, n_prefetch=0):
    per = (rows // n_mod) // tm
    return pl.BlockSpec((1, 1, d), lambda i, *_: (i // per, 0, 0))


def _rownorm(x, g, scale, shift, *, tm=512):
    rows, d = x.shape
    row_spec = pl.BlockSpec((tm, d), lambda i: (i, 0))
    return pl.pallas_call(
        _rownorm_kernel, grid=(rows // tm,),
        in_specs=[row_spec, pl.BlockSpec((1, d), lambda i: (0, 0)), _mod_spec(scale.shape[0], rows, tm, d),
                  _mod_spec(shift.shape[0], rows, tm, d)],
        out_specs=row_spec, out_shape=jax.ShapeDtypeStruct((rows, d), BF16),
        compiler_params=_cparams("parallel"), name="rownorm",
    )(x, g.reshape(1, d), scale, shift)


def _mm_kernel(a_ref, w_ref, o_ref):
    o_ref[...] = _dot(a_ref[...].astype(BF16), w_ref[...]).astype(o_ref.dtype)


def _matmul(a, w, *, out_dtype=F32, tm=1024):
    m, k = a.shape
    n = w.shape[1]
    tn = next(t for t in (1024, 768, 512, LANES) if n % t == 0)
    return pl.pallas_call(
        _mm_kernel, grid=(m // tm, n // tn),
        in_specs=[pl.BlockSpec((tm, k), lambda i, j: (i, 0)), pl.BlockSpec((k, tn), lambda i, j: (0, j))],
        out_specs=pl.BlockSpec((tm, tn), lambda i, j: (i, j)),
        out_shape=jax.ShapeDtypeStruct((m, n), out_dtype),
        compiler_params=_cparams("parallel", "parallel"), name="matmul",
    )(a, w)


def _head_norm(x, g, n_heads, dv):
    outs = []
    for h in range(n_heads):
        xs = x[:, h * dv:(h + 1) * dv]
        ms = jnp.mean(xs * xs, axis=-1, keepdims=True)
        outs.append(xs * lax.rsqrt(ms + EPS) * g[:, h * dv:(h + 1) * dv])
    return jnp.concatenate(outs, axis=1)


def _proj_kernel(*refs, pre):
    refs = list(refs)
    if pre == "plain":
        a = refs.pop(0)[...]
    else:
        f_ref, b_ref, p_ref, hg_ref = refs.pop(0), refs.pop(0), refs.pop(0), refs.pop(0)
        hsum = f_ref[...] + b_ref[...]
        if pre == "mlstm":
            a = _sigmoid(p_ref[...]) * _head_norm(hsum, hg_ref[...], ML_HEADS, ML_DV)
        else:
            a = _head_norm(hsum, hg_ref[...], GLA_HEADS, GLA_DV) * _silu(p_ref[...])
        a = a.astype(BF16)
    w_ref, x_ref, gate_ref, g_ref, a_ref, s_ref, rwt_ref, rb_ref = refs[:8]
    xo_ref, h_ref, meta_ref, wcol_ref, count_ref, carry_ref = refs[8:]

    @pl.when(pl.program_id(0) == 0)
    def _():
        carry_ref[...] = jnp.zeros_like(carry_ref)

    x = x_ref[...] + gate_ref[0] * _dot(a, w_ref[...])
    xo_ref[...] = x
    h = _norm_mod(x, g_ref, (a_ref, s_ref))
    h_ref[...] = h
    meta, wcol, carry = _route(h, rwt_ref[...], rb_ref[...], carry_ref[:, 0:1])
    meta_ref[...] = meta
    wcol_ref[...] = wcol
    carry_ref[...] = jnp.broadcast_to(carry, carry_ref.shape)
    count_ref[...] = jnp.broadcast_to(carry, count_ref.shape).astype(jnp.int32)


def _proj(pre, pre_args, w_out, x, gate, g, scale, shift, rwt, rb, *, tm=256):
    rows, d = x.shape
    k = w_out.shape[0]
    row_spec = pl.BlockSpec((tm, d), lambda i: (i, 0))
    if pre == "plain":
        args, specs = [pre_args[0]], [pl.BlockSpec((tm, k), lambda i: (i, 0))]
    else:
        hf, hb, p, col_block, hg = pre_args
        wide = pl.BlockSpec((tm, k), lambda i: (i, 0))
        args = [hf, hb, p, hg.reshape(1, k)]
        specs = [wide, wide, pl.BlockSpec((tm, k), lambda i: (i, col_block)), pl.BlockSpec((1, k), lambda i: (0, 0))]
    args += [w_out, x, gate, g.reshape(1, d), scale, shift, rwt, rb]
    specs += [pl.BlockSpec((k, d), lambda i: (0, 0)), row_spec, _mod_spec(gate.shape[0], rows, tm, d),
              pl.BlockSpec((1, d), lambda i: (0, 0)), _mod_spec(scale.shape[0], rows, tm, d),
              _mod_spec(shift.shape[0], rows, tm, d), pl.BlockSpec(rwt.shape, lambda i: (0, 0)),
              pl.BlockSpec(rb.shape, lambda i: (0, 0))]
    return pl.pallas_call(
        functools.partial(_proj_kernel, pre=pre), grid=(rows // tm,), in_specs=specs,
        out_specs=[row_spec, row_spec, pl.BlockSpec((8, tm), lambda i: (0, i)),
                   pl.BlockSpec((tm, LANES), lambda i: (i, 0)), pl.BlockSpec((N_EXPERTS, LANES), lambda i: (0, 0))],
        out_shape=[jax.ShapeDtypeStruct((rows, d), F32), jax.ShapeDtypeStruct((rows, d), F32),
                   jax.ShapeDtypeStruct((8, rows), jnp.int32), jax.ShapeDtypeStruct((rows, LANES), F32),
                   jax.ShapeDtypeStruct((N_EXPERTS, LANES), jnp.int32)],
        scratch_shapes=[pltpu.VMEM((N_EXPERTS, LANES), F32)],
        compiler_params=_cparams("arbitrary"), name="proj_" + pre,
    )(*args)


MOE_TILE = 256
MOE_TILE_SHIFT = 8
MOE_TOKENS = 256
ROW_UNROLL = 8


def _slot_tiles(rows):
    return (2 * rows) // MOE_TILE + N_EXPERTS


def _expert_offsets(cnt_ref, off_ref):
    def per_expert(e, k):
        off_ref[e] = k * MOE_TILE
        return k + ((cnt_ref[e] + MOE_TILE - 1) >> MOE_TILE_SHIFT)
    return lax.fori_loop(0, N_EXPERTS, per_expert, 0)


def _dispatch_kernel(ea_ref, eb_ref, ra_ref, rb_ref, cnt_ref, h_ref, xs_ref, info_ref, off_ref, zero_ref, sem):
    i = pl.program_id(0)
    tm = h_ref.shape[0]
    n_tiles = info_ref.shape[0] - 1

    def tile_copy(tile):
        return pltpu.make_async_copy(zero_ref, xs_ref.at[pl.ds(tile * MOE_TILE, MOE_TILE), :], sem)

    @pl.when(i == 0)
    def _():
        zero_ref[...] = jnp.zeros_like(zero_ref)
        used = _expert_offsets(cnt_ref, off_ref)

        def per_expert(e, _):
            first = off_ref[e] >> MOE_TILE_SHIFT
            nt = (cnt_ref[e] + MOE_TILE - 1) >> MOE_TILE_SHIFT

            def fill(j, _):
                info_ref[first + j] = e
                return 0
            lax.fori_loop(0, nt, fill, 0)

            @pl.when(nt > 0)
            def _():
                tile_copy(first + nt - 1).start()
                tile_copy(first + nt - 1).wait()
            return 0
        lax.fori_loop(0, N_EXPERTS, per_expert, 0)
        info_ref[n_tiles] = used

        def tail(j, _):
            info_ref[j] = N_EXPERTS - 1
            tile_copy(j).start()
            tile_copy(j).wait()
            return 0
        lax.fori_loop(used, n_tiles, tail, 0)

    base = i * tm

    def row_copy(t, slot):
        return pltpu.make_async_copy(h_ref.at[pl.ds(t, 1), :], xs_ref.at[pl.ds(slot, 1), :], sem)

    def issue(t, _):
        g = base + t
        row_copy(t, off_ref[ea_ref[g]] + ra_ref[g]).start()
        row_copy(t, off_ref[eb_ref[g]] + rb_ref[g]).start()
        return 0
    lax.fori_loop(0, tm, issue, 0, unroll=ROW_UNROLL)
    for _ in range(2):
        pltpu.make_async_copy(h_ref, xs_ref.at[pl.ds(0, tm), :], sem).wait()


def _dispatch(h, meta, counts):
    rows, d = h.shape
    n_tiles = _slot_tiles(rows)
    tm = MOE_TOKENS
    grid_spec = pltpu.PrefetchScalarGridSpec(
        num_scalar_prefetch=5, grid=(rows // tm,),
        in_specs=[pl.BlockSpec((tm, d), lambda i, *_: (i, 0))],
        out_specs=[pl.BlockSpec(memory_space=pl.ANY), pl.BlockSpec(memory_space=pltpu.SMEM)],
        scratch_shapes=[pltpu.SMEM((N_EXPERTS,), jnp.int32), pltpu.VMEM((MOE_TILE, d), F32),
                        pltpu.SemaphoreType.DMA(())])
    return pl.pallas_call(
        _dispatch_kernel, grid_spec=grid_spec,
        out_shape=[jax.ShapeDtypeStruct((n_tiles * MOE_TILE, d), F32),
                   jax.ShapeDtypeStruct((n_tiles + 1,), jnp.int32)],
        compiler_params=_cparams("arbitrary"), name="moe_dispatch",
    )(meta[0], meta[1], meta[2], meta[3], counts, h)


def _ffn_kernel(info_ref, xs_ref, wg_ref, wu_ref, wd_ref, ys_ref, wg_s, wu_s, wd_s):
    i = pl.program_id(0)
    used = info_ref[info_ref.shape[0] - 1]
    fresh = jnp.logical_or(i == 0, info_ref[i] != info_ref[jnp.maximum(i - 1, 0)])

    @pl.when(jnp.logical_and(i < used, fresh))
    def _():
        wg_s[...] = wg_ref[0].astype(BF16)
        wu_s[...] = wu_ref[0].astype(BF16)
        wd_s[...] = wd_ref[0].astype(BF16)

    @pl.when(i < used)
    def _():
        x = xs_ref[...].astype(BF16)
        hid = _silu(_dot(x, wg_s[...])) * _dot(x, wu_s[...])
        ys_ref[...] = _dot(hid.astype(BF16), wd_s[...])

    @pl.when(i >= used)
    def _():
        ys_ref[...] = jnp.zeros_like(ys_ref)


def _ffn(xs, info, wg, wu, wd):
    slots, d = xs.shape
    n_tiles = slots // MOE_TILE
    f = wg.shape[2]

    def w_map(i, info):
        return (info[i], 0, 0)

    grid_spec = pltpu.PrefetchScalarGridSpec(
        num_scalar_prefetch=1, grid=(n_tiles,),
        in_specs=[pl.BlockSpec((MOE_TILE, d), lambda i, info: (jnp.minimum(i, info[n_tiles] - 1), 0)),
                  pl.BlockSpec((1, d, f), w_map), pl.BlockSpec((1, d, f), w_map), pl.BlockSpec((1, f, d), w_map)],
        out_specs=pl.BlockSpec((MOE_TILE, d), lambda i, info: (i, 0)),
        scratch_shapes=[pltpu.VMEM((d, f), BF16), pltpu.VMEM((d, f), BF16), pltpu.VMEM((f, d), BF16)])
    return pl.pallas_call(
        _ffn_kernel, grid_spec=grid_spec, out_shape=jax.ShapeDtypeStruct((slots, d), F32),
        compiler_params=_cparams("arbitrary"), name="moe_ffn",
    )(info, xs, wg, wu, wd)


def _combine_kernel(*refs, has_mod, out_x):
    refs = list(refs)
    ea_ref, eb_ref, ra_ref, rb_ref, cnt_ref, x_ref, ys_ref, wcol_ref, gate_ref, g_ref = refs[:10]
    refs = refs[10:]
    mod_refs = (refs.pop(0), refs.pop(0)) if has_mod else None
    xo_ref = refs.pop(0) if out_x else None
    h_ref, off_ref, buf_a, buf_b, sems = refs
    i = pl.program_id(0)
    tm = x_ref.shape[0]

    def issue(tile, slot):
        base = tile * tm

        def body(t, _):
            g = base + t
            sa = off_ref[ea_ref[g]] + ra_ref[g]
            sb = off_ref[eb_ref[g]] + rb_ref[g]
            pltpu.make_async_copy(ys_ref.at[pl.ds(sa, 1), :], buf_a.at[slot, pl.ds(t, 1), :], sems.at[slot]).start()
            pltpu.make_async_copy(ys_ref.at[pl.ds(sb, 1), :], buf_b.at[slot, pl.ds(t, 1), :], sems.at[slot]).start()
            return 0
        lax.fori_loop(0, tm, body, 0, unroll=ROW_UNROLL)

    @pl.when(i == 0)
    def _():
        _expert_offsets(cnt_ref, off_ref)
        issue(0, 0)

    @pl.when(i + 1 < pl.num_programs(0))
    def _():
        issue(i + 1, (i + 1) % 2)

    slot = i % 2
    for buf in (buf_a, buf_b):
        pltpu.make_async_copy(ys_ref.at[pl.ds(0, tm), :], buf.at[slot], sems.at[slot]).wait()
    y = wcol_ref[:, 0:1] * buf_a[slot] + wcol_ref[:, 1:2] * buf_b[slot]
    x = x_ref[...] + gate_ref[0] * y
    if out_x:
        xo_ref[...] = x
    h_ref[...] = _norm_mod(x, g_ref, mod_refs).astype(h_ref.dtype)


def _combine(x, ys, meta, counts, wcol, gate, g, *, mod=None, out_x=False, h_dtype=BF16):
    rows, d = x.shape
    tm = MOE_TOKENS
    row_spec = pl.BlockSpec((tm, d), lambda i, *_: (i, 0))
    args = [x, ys, wcol, gate, g.reshape(1, d)]
    specs = [row_spec, pl.BlockSpec(memory_space=pl.ANY), pl.BlockSpec((tm, LANES), lambda i, *_: (i, 0)),
             _mod_spec(gate.shape[0], rows, tm, d), pl.BlockSpec((1, d), lambda i, *_: (0, 0))]
    if mod is not None:
        for m in mod:
            args.append(m)
            specs.append(_mod_spec(m.shape[0], rows, tm, d))
    out_shape, out_specs = [], []
    if out_x:
        out_shape.append(jax.ShapeDtypeStruct((rows, d), F32))
        out_specs.append(row_spec)
    out_shape.append(jax.ShapeDtypeStruct((rows, d), h_dtype))
    out_specs.append(row_spec)
    grid_spec = pltpu.PrefetchScalarGridSpec(
        num_scalar_prefetch=5, grid=(rows // tm,), in_specs=specs, out_specs=out_specs,
        scratch_shapes=[pltpu.SMEM((N_EXPERTS,), jnp.int32), pltpu.VMEM((2, tm, d), F32), pltpu.VMEM((2, tm, d), F32),
                        pltpu.SemaphoreType.DMA((2,))])
    outs = pl.pallas_call(
        functools.partial(_combine_kernel, has_mod=mod is not None, out_x=out_x), grid_spec=grid_spec,
        out_shape=out_shape, compiler_params=_cparams("arbitrary"), name="moe_combine",
    )(meta[0], meta[1], meta[2], meta[3], counts, *args)
    return outs if out_x else outs[0]


def _softmax_av(scores, values, sink_col):
    m = sink_col
    for s in scores:
        m = jnp.maximum(m, jnp.max(s, axis=-1, keepdims=True))
    den = jnp.exp(sink_col - m)
    acc = None
    for s, v in zip(scores, values):
        p = jnp.exp(s - m)
        den = den + jnp.sum(p, axis=-1, keepdims=True)
        pv = _dot(p.astype(BF16), v)
        acc = pv if acc is None else acc + pv
    return acc / den


def _sink_column(sink_ref, kv, rows):
    return jnp.concatenate([jnp.full((rows, 1), sink_ref[kv * ATT_GROUP + g], F32) for g in range(ATT_GROUP)], axis=0)


def _attn_ctx_kernel(sink_ref, qkv_ref, o_ref):
    t = qkv_ref.shape[0]
    qw = ATT_HEADS * HEAD_DIM
    kw = ATT_KV * HEAD_DIM
    heads_out = []
    for kv in range(ATT_KV):
        q = jnp.concatenate(
            [qkv_ref[:, (kv * ATT_GROUP + g) * HEAD_DIM:(kv * ATT_GROUP + g + 1) * HEAD_DIM] for g in range(ATT_GROUP)],
            axis=0).astype(BF16)
        k = qkv_ref[:, qw + kv * HEAD_DIM:qw + (kv + 1) * HEAD_DIM].astype(BF16)
        v = qkv_ref[:, qw + kw + kv * HEAD_DIM:qw + kw + (kv + 1) * HEAD_DIM].astype(BF16)
        s = _dot_nt(q, k) * HEAD_DIM ** -0.5
        o = _softmax_av([s], [v], _sink_column(sink_ref, kv, t))
        heads_out += [o[g * t:(g + 1) * t] for g in range(ATT_GROUP)]
    o_ref[...] = jnp.concatenate(heads_out, axis=1).astype(o_ref.dtype)


def _attn_ctx(qkv, sink, n_seq, seq_len):
    rows, cols = qkv.shape
    return pl.pallas_call(
        _attn_ctx_kernel, grid=(n_seq,),
        in_specs=[pl.BlockSpec(memory_space=pltpu.SMEM), pl.BlockSpec((seq_len, cols), lambda b: (b, 0))],
        out_specs=pl.BlockSpec((seq_len, ATT_HEADS * HEAD_DIM), lambda b: (b, 0)),
        out_shape=jax.ShapeDtypeStruct((rows, ATT_HEADS * HEAD_DIM), BF16),
        compiler_params=_cparams("parallel"), name="attn_context",
    )(sink, qkv)


def _rope_block(x, cos, sin_signed):
    lane = lax.broadcasted_iota(jnp.int32, x.shape, 1)
    nf = HEAD_DIM // 4
    partner = jnp.where((lane % (2 * nf)) < nf, pltpu.roll(x, LANES - nf, axis=1), pltpu.roll(x, nf, axis=1))
    return x * cos + partner * sin_signed


def _attn_lat_kernel(sink_ref, qkv_ref, ck_ref, cv_ref, cos_ref, sin_ref, o_ref, k_scr):
    i = pl.program_id(1)
    t = qkv_ref.shape[0]
    qw = ATT_HEADS * HEAD_DIM
    kw = ATT_KV * HEAD_DIM
    span = Q_BLOCK + 2 * WINDOW

    @pl.when(i == 0)
    def _():
        for c in range(kw // LANES):
            blk = qkv_ref[:, qw + c * LANES:qw + (c + 1) * LANES]
            k_scr[:, c * LANES:(c + 1) * LANES] = _rope_block(blk, cos_ref[...], sin_ref[...]).astype(BF16)

    r0 = pl.multiple_of(i * Q_BLOCK, Q_BLOCK)
    ws = pl.multiple_of(jnp.clip(r0 - WINDOW, 0, t - span), Q_BLOCK)
    cos_q = cos_ref[pl.ds(r0, Q_BLOCK), :]
    sin_q = sin_ref[pl.ds(r0, Q_BLOCK), :]
    qpos = r0 + lax.broadcasted_iota(jnp.int32, (Q_BLOCK, span), 0)
    kpos = ws + lax.broadcasted_iota(jnp.int32, (Q_BLOCK, span), 1)
    band = jnp.abs(qpos - kpos) <= WINDOW
    band = jnp.concatenate([band] * ATT_GROUP, axis=0)
    heads_out = []
    for kv in range(ATT_KV):
        heads = []
        for g in range(ATT_GROUP):
            h = kv * ATT_GROUP + g
            c, half = divmod(h * HEAD_DIM, LANES)
            blk = _rope_block(qkv_ref[pl.ds(r0, Q_BLOCK), c * LANES:(c + 1) * LANES], cos_q, sin_q)
            heads.append(blk[:, half:half + HEAD_DIM])
        q = jnp.concatenate(heads, axis=0).astype(BF16)
        ck = ck_ref[0, :, kv * HEAD_DIM:(kv + 1) * HEAD_DIM].astype(BF16)
        cv = cv_ref[0, :, kv * HEAD_DIM:(kv + 1) * HEAD_DIM].astype(BF16)
        kwin = k_scr[pl.ds(ws, span), kv * HEAD_DIM:(kv + 1) * HEAD_DIM]
        vwin = qkv_ref[pl.ds(ws, span), qw + kw + kv * HEAD_DIM:qw + kw + (kv + 1) * HEAD_DIM].astype(BF16)
        s_ctx = _dot_nt(q, ck) * HEAD_DIM ** -0.5
        s_win = jnp.where(band, _dot_nt(q, kwin) * HEAD_DIM ** -0.5, -jnp.inf)
        o = _softmax_av([s_ctx, s_win], [cv, vwin], _sink_column(sink_ref, kv, Q_BLOCK))
        heads_out += [o[g * Q_BLOCK:(g + 1) * Q_BLOCK] for g in range(ATT_GROUP)]
    o_ref[...] = jnp.concatenate(heads_out, axis=1).astype(o_ref.dtype)


def _rope_tables(seq_len):
    pos = jnp.arange(seq_len, dtype=jnp.int32)
    row = (pos // GRID_W).astype(F32)
    col = (pos % GRID_W).astype(F32)
    nf = HEAD_DIM // 4
    inv = ROPE_BASE ** (-jnp.arange(nf, dtype=F32) / nf)
    ang_r = row[:, None] * inv[None, :]
    ang_c = col[:, None] * inv[None, :]
    cos_h = jnp.concatenate([jnp.cos(ang_r), jnp.cos(ang_r), jnp.cos(ang_c), jnp.cos(ang_c)], axis=1)
    sin_h = jnp.concatenate([-jnp.sin(ang_r), jnp.sin(ang_r), -jnp.sin(ang_c), jnp.sin(ang_c)], axis=1)
    reps = LANES // HEAD_DIM
    return jnp.tile(cos_h, (1, reps)), jnp.tile(sin_h, (1, reps))


def _attn_lat(qkv, cache_k, cache_v, sink, n_seq, seq_len):
    rows, cols = qkv.shape
    past = cache_k.shape[1]
    kw = ATT_KV * HEAD_DIM
    cos, sin = _rope_tables(seq_len)
    return pl.pallas_call(
        _attn_lat_kernel, grid=(n_seq, seq_len // Q_BLOCK),
        in_specs=[pl.BlockSpec(memory_space=pltpu.SMEM),
                  pl.BlockSpec((seq_len, cols), lambda b, i: (b, 0)),
                  pl.BlockSpec((1, past, kw), lambda b, i: (b, 0, 0)),
                  pl.BlockSpec((1, past, kw), lambda b, i: (b, 0, 0)),
                  pl.BlockSpec((seq_len, LANES), lambda b, i: (0, 0)),
                  pl.BlockSpec((seq_len, LANES), lambda b, i: (0, 0))],
        out_specs=pl.BlockSpec((Q_BLOCK, ATT_HEADS * HEAD_DIM), lambda b, i: (b * (seq_len // Q_BLOCK) + i, 0)),
        out_shape=jax.ShapeDtypeStruct((rows, ATT_HEADS * HEAD_DIM), BF16),
        scratch_shapes=[pltpu.VMEM((seq_len, kw), BF16)],
        compiler_params=_cparams("parallel", "arbitrary"), name="attn_latent",
    )(sink, qkv, cache_k.reshape(n_seq, past, kw), cache_v.reshape(n_seq, past, kw), cos, sin)


def _ml_gates_kernel(h_ref, w_ref, wt_ref, b_ref, bt_ref, g_ref, gt_ref):
    h = h_ref[...]
    g_ref[...] = jnp.dot(h, w_ref[...], precision=HI, preferred_element_type=F32) + b_ref[...]
    gt_ref[...] = _dot_nt(wt_ref[...], h, precision=HI) + bt_ref[...]


def _ml_gates(h, w_gates, b_gates, *, tm=512):
    rows, d = h.shape
    ng = w_gates.shape[1]
    w_pad = jnp.pad(w_gates, ((0, 0), (0, LANES - ng)))
    b_pad = jnp.pad(b_gates, (0, LANES - ng)).reshape(1, LANES)
    return pl.pallas_call(
        _ml_gates_kernel, grid=(rows // tm,),
        in_specs=[pl.BlockSpec((tm, d), lambda i: (i, 0)), pl.BlockSpec((d, LANES), lambda i: (0, 0)),
                  pl.BlockSpec((ng, d), lambda i: (0, 0)), pl.BlockSpec((1, LANES), lambda i: (0, 0)),
                  pl.BlockSpec((ng, 1), lambda i: (0, 0))],
        out_specs=[pl.BlockSpec((tm, LANES), lambda i: (i, 0)), pl.BlockSpec((ng, tm), lambda i: (0, i))],
        out_shape=[jax.ShapeDtypeStruct((rows, LANES), F32), jax.ShapeDtypeStruct((ng, rows), F32)],
        compiler_params=_cparams("parallel"), name="mlstm_gates",
    )(h, w_pad, w_gates.T, b_pad, b_gates.reshape(ng, 1))


def _ml_conv_kernel(p_ref, w_ref, o_ref, *, k_scale):
    j = pl.program_id(1)
    x = p_ref[...]
    t = x.shape[0]
    row = lax.broadcasted_iota(jnp.int32, x.shape, 0)
    prev = jnp.where(row == 0, 0.0, pltpu.roll(x, 1, axis=0))
    nxt = jnp.where(row == t - 1, 0.0, pltpu.roll(x, t - 1, axis=0))
    y = prev * w_ref[0:1, :] + x * w_ref[1:2, :] + nxt * w_ref[2:3, :]
    scale = jnp.where(j >= pl.num_programs(1) // 2, k_scale, 1.0).astype(F32)
    o_ref[...] = (_silu(y) * scale).astype(o_ref.dtype)


def _ml_conv(p, conv_w, n_seq, seq_len, *, tn=512):
    rows = p.shape[0]
    width = conv_w.shape[1]
    return pl.pallas_call(
        functools.partial(_ml_conv_kernel, k_scale=ML_DK ** -0.5), grid=(n_seq, width // tn),
        in_specs=[pl.BlockSpec((seq_len, tn), lambda b, j: (b, j)), pl.BlockSpec((3, tn), lambda b, j: (0, j))],
        out_specs=pl.BlockSpec((seq_len, tn), lambda b, j: (b, j)),
        out_shape=jax.ShapeDtypeStruct((rows, width), BF16),
        compiler_params=_cparams("parallel", "parallel"), name="mlstm_conv",
    )(p, conv_w)


def _ml_scan_kernel(*refs, zero_init):
    refs = list(refs)
    dirs = [tuple(refs[0:5]), tuple(refs[5:10])]
    refs = refs[10:]
    if not zero_init:
        c0_ref, n0_ref, m0_ref = refs[:3]
        refs = refs[3:]
    hf_ref, hb_ref, c_ref, n_ref, m_ref = refs
    h_out = (hf_ref, hb_ref)
    c = pl.program_id(1)
    last = pl.num_programs(1) - 1

    @pl.when(c == 0)
    def _():
        if zero_init:
            c_ref[...] = jnp.zeros_like(c_ref)
            n_ref[...] = jnp.zeros_like(n_ref)
            m_ref[...] = jnp.zeros_like(m_ref)
        else:
            c_ref[...] = c0_ref[...]
            n_ref[...] = n0_ref[...]
            m_ref[...] = m0_ref[...]

    length = hf_ref.shape[0]
    ti = lax.broadcasted_iota(jnp.int32, (length, length), 0)
    si = lax.broadcasted_iota(jnp.int32, (length, length), 1)
    for d in range(2):
        q_ref, k_ref, v_ref, g_ref, gt_ref = dirs[d]
        causal = (ti >= si) if d == 0 else (ti <= si)
        tri = jnp.where(causal, 1.0, 0.0).astype(F32)
        f_col = _log_sigmoid(g_ref[...])
        f_row = _log_sigmoid(gt_ref[...])
        b_col = jnp.dot(tri, f_col, precision=HI, preferred_element_type=F32)
        b_row = _dot_nt(f_row, tri, precision=HI)
        edge = length - 1 if d == 0 else 0
        for h in range(ML_HEADS):
            ji = d * 2 * ML_HEADS + h
            jf = ji + ML_HEADS
            bc = b_col[:, jf:jf + 1]
            br = b_row[jf:jf + 1, :]
            i_row = gt_ref[ji:ji + 1, :]
            i_col = g_ref[:, ji:ji + 1]
            m_prev = m_ref[0, d, h][:, 0:1]
            q = q_ref[:, h * ML_DK:(h + 1) * ML_DK]
            k = k_ref[:, h * ML_DK:(h + 1) * ML_DK]
            v = v_ref[:, h * ML_DV:(h + 1) * ML_DV].astype(BF16)
            cst = c_ref[0, d, h]
            nst = n_ref[0, d, h]
            dmat = jnp.where(causal, bc - br + i_row, -jnp.inf)
            inter = bc + m_prev
            m_t = jnp.maximum(inter, jnp.max(dmat, axis=1, keepdims=True))
            w = jnp.exp(dmat - m_t)
            qk = _dot_nt(q, k) * w
            sc = jnp.exp(inter - m_t)
            num = sc * _dot_nt(q, cst.astype(BF16)) + _dot(qk.astype(BF16), v)
            den = sc * jnp.sum(q.astype(F32) * nst, axis=1, keepdims=True) + jnp.sum(qk, axis=1, keepdims=True)
            h_out[d][:, h * ML_DV:(h + 1) * ML_DV] = num / jnp.maximum(jnp.abs(den), jnp.exp(-m_t))
            b_last = br[:, edge:edge + 1]
            wlog_row = b_last - br + i_row
            wlog_col = b_last - bc + i_col
            m_new = jnp.maximum(b_last + m_prev, jnp.max(wlog_row, axis=1, keepdims=True))
            decay = jnp.exp(b_last + m_prev - m_new)
            ws = jnp.exp(wlog_col - m_new)
            vw = (ws * v_ref[:, h * ML_DV:(h + 1) * ML_DV]).astype(BF16)
            c_ref[0, d, h] = decay * cst + _dot_tn(vw, k)
            n_ref[0, d, h] = decay * nst + jnp.sum(ws * k.astype(F32), axis=0, keepdims=True)
            m_ref[0, d, h] = jnp.broadcast_to(m_new, (1, ML_DK))


def _ml_scan(qk, p, g, gt, state, n_seq, seq_len):
    rows = qk.shape[0]
    length = min(ML_CHUNK, seq_len)
    nc = seq_len // length
    qw = ML_HEADS * ML_DK
    vw = ML_HEADS * ML_DV
    ng = gt.shape[0]

    def fwd(b, c):
        return b * nc + c

    def bwd(b, c):
        return b * nc + nc - 1 - c

    args, specs = [], []
    for pos in (fwd, bwd):
        args += [qk, qk, p, g, gt]
        specs += [pl.BlockSpec((length, qw), lambda b, c, pos=pos: (pos(b, c), 0)),
                  pl.BlockSpec((length, qw), lambda b, c, pos=pos: (pos(b, c), 1)),
                  pl.BlockSpec((length, vw), lambda b, c, pos=pos: (pos(b, c), 2 * qw // vw)),
                  pl.BlockSpec((length, LANES), lambda b, c, pos=pos: (pos(b, c), 0)),
                  pl.BlockSpec((ng, length), lambda b, c, pos=pos: (0, pos(b, c)))]
    c_spec = pl.BlockSpec((1, 2, ML_HEADS, ML_DV, ML_DK), lambda b, c: (b, 0, 0, 0, 0))
    n_spec = pl.BlockSpec((1, 2, ML_HEADS, 1, ML_DK), lambda b, c: (b, 0, 0, 0, 0))
    zero_init = state is None
    if not zero_init:
        c0, n0, m0 = state
        args += [c0, n0.reshape(n_seq, 2, ML_HEADS, 1, ML_DK),
                 jnp.broadcast_to(m0[..., None, None], (n_seq, 2, ML_HEADS, 1, ML_DK))]
        specs += [c_spec, n_spec, n_spec]
    hf, hb, c_fin, n_fin, m_fin = pl.pallas_call(
        functools.partial(_ml_scan_kernel, zero_init=zero_init), grid=(n_seq, nc), in_specs=specs,
        out_specs=[pl.BlockSpec((length, vw), lambda b, c: (fwd(b, c), 0)),
                   pl.BlockSpec((length, vw), lambda b, c: (bwd(b, c), 0)), c_spec, n_spec, n_spec],
        out_shape=[jax.ShapeDtypeStruct((rows, vw), F32), jax.ShapeDtypeStruct((rows, vw), F32),
                   jax.ShapeDtypeStruct((n_seq, 2, ML_HEADS, ML_DV, ML_DK), F32),
                   jax.ShapeDtypeStruct((n_seq, 2, ML_HEADS, 1, ML_DK), F32),
                   jax.ShapeDtypeStruct((n_seq, 2, ML_HEADS, 1, ML_DK), F32)],
        compiler_params=_cparams("parallel", "arbitrary"), name="mlstm_scan",
    )(*args)
    return hf, hb, (c_fin, n_fin[:, :, :, 0, :], m_fin[:, :, :, 0, 0])


def _gla_scan_kernel(*refs, zero_init):
    refs = list(refs)
    dirs = [tuple(refs[0:4]), tuple(refs[4:8])]
    w2_ref, ba_ref = refs[8:10]
    refs = refs[10:]
    if not zero_init:
        s0_ref = refs.pop(0)
    of_ref, ob_ref, s_ref, st_scr, la_scr = refs
    o_out = (of_ref, ob_ref)
    c = pl.program_id(1)
    last = pl.num_programs(1) - 1
    kw = GLA_HEADS * GLA_DK
    n_sub = of_ref.shape[0] // GLA_SUB

    @pl.when(c == 0)
    def _():
        for d in range(2):
            for h in range(GLA_HEADS):
                st_scr[d, h] = jnp.zeros((GLA_DV, GLA_DK), F32) if zero_init else s0_ref[0, d, h].T

    for d in range(2):
        u = dirs[d][3][...].astype(BF16)
        z = _dot(u, w2_ref[:, d * kw:(d + 1) * kw]) + ba_ref[:, d * kw:(d + 1) * kw]
        la_scr[d] = _log_sigmoid(z) / GLA_TAU

    ti = lax.broadcasted_iota(jnp.int32, (GLA_SUB, GLA_SUB), 0)
    si = lax.broadcasted_iota(jnp.int32, (GLA_SUB, GLA_SUB), 1)
    t_col = lax.broadcasted_iota(jnp.int32, (GLA_SUB, 1), 0)
    s_lane = lax.broadcasted_iota(jnp.int32, (GLA_SUB, GLA_SUB), 1)

    def sub_chunk(j, carry):
        for d in range(2):
            q_ref, k_ref, v_ref, _ = dirs[d]
            r0 = pl.multiple_of((j if d == 0 else n_sub - 1 - j) * GLA_SUB, GLA_SUB)
            tri = jnp.where((ti >= si) if d == 0 else (ti <= si), 1.0, 0.0).astype(F32)
            bc_all = jnp.dot(tri, la_scr[d, pl.ds(r0, GLA_SUB), :], precision=HI, preferred_element_type=F32)
            edge = GLA_SUB - 1 if d == 0 else 0
            for h in range(GLA_HEADS):
                bc = bc_all[:, h * GLA_DK:(h + 1) * GLA_DK]
                q = q_ref[pl.ds(r0, GLA_SUB), h * GLA_DK:(h + 1) * GLA_DK] * GLA_DK ** -0.5
                k = k_ref[pl.ds(r0, GLA_SUB), h * GLA_DK:(h + 1) * GLA_DK]
                v = v_ref[pl.ds(r0, GLA_SUB), h * GLA_DV:(h + 1) * GLA_DV].astype(BF16)
                a = jnp.zeros((GLA_SUB, GLA_SUB), F32)
                for s in range(GLA_SUB):
                    valid = (t_col >= s) if d == 0 else (t_col <= s)
                    decay = jnp.exp(jnp.where(valid, bc - bc[s:s + 1, :], -jnp.inf))
                    col = jnp.sum(q * (k[s:s + 1, :] * decay), axis=1, keepdims=True)
                    a = a + jnp.where(s_lane == s, col, 0.0)
                st = st_scr[d, h]
                o = _dot(a.astype(BF16), v) + _dot_nt((q * jnp.exp(bc)).astype(BF16), st.astype(BF16))
                o_out[d][pl.ds(r0, GLA_SUB), h * GLA_DV:(h + 1) * GLA_DV] = o
                b_last = bc[edge:edge + 1, :]
                k_dec = (k * jnp.exp(b_last - bc)).astype(BF16)
                st_scr[d, h] = jnp.exp(b_last) * st + _dot_tn(v, k_dec)
        return carry

    lax.fori_loop(0, n_sub, sub_chunk, 0)

    @pl.when(c == last)
    def _():
        for d in range(2):
            for h in range(GLA_HEADS):
                s_ref[0, d, h] = st_scr[d, h].T


def _gla_scan(p, u, w2, b_a, state, n_seq, seq_len):
    rows = p.shape[0]
    length = min(GLA_BLOCK, seq_len)
    nc = seq_len // length
    kw = GLA_HEADS * GLA_DK
    vw = GLA_HEADS * GLA_DV

    def fwd(b, c):
        return b * nc + c

    def bwd(b, c):
        return b * nc + nc - 1 - c

    args, specs = [], []
    for pos in (fwd, bwd):
        args += [p, p, p, u]
        specs += [pl.BlockSpec((length, kw), lambda b, c, pos=pos: (pos(b, c), 0)),
                  pl.BlockSpec((length, kw), lambda b, c, pos=pos: (pos(b, c), 1)),
                  pl.BlockSpec((length, vw), lambda b, c, pos=pos: (pos(b, c), 2 * kw // vw)),
                  pl.BlockSpec((length, LANES), lambda b, c, pos=pos: (pos(b, c), 0))]
    args += [w2, b_a]
    specs += [pl.BlockSpec(w2.shape, lambda b, c: (0, 0)), pl.BlockSpec(b_a.shape, lambda b, c: (0, 0))]
    s_spec = pl.BlockSpec((1, 2, GLA_HEADS, GLA_DK, GLA_DV), lambda b, c: (b, 0, 0, 0, 0))
    zero_init = state is None
    if not zero_init:
        args.append(state)
        specs.append(s_spec)
    return pl.pallas_call(
        functools.partial(_gla_scan_kernel, zero_init=zero_init), grid=(n_seq, nc), in_specs=specs,
        out_specs=[pl.BlockSpec((length, vw), lambda b, c: (fwd(b, c), 0)),
                   pl.BlockSpec((length, vw), lambda b, c: (bwd(b, c), 0)), s_spec],
        out_shape=[jax.ShapeDtypeStruct((rows, vw), F32), jax.ShapeDtypeStruct((rows, vw), F32),
                   jax.ShapeDtypeStruct((n_seq, 2, GLA_HEADS, GLA_DK, GLA_DV), F32)],
        scratch_shapes=[pltpu.VMEM((2, GLA_HEADS, GLA_DV, GLA_DK), F32), pltpu.VMEM((2, length, kw), F32)],
        compiler_params=_cparams("parallel", "arbitrary"), name="gla_scan",
    )(*args)


def kernel(x_prompt, x_sample, cache_k_0, cache_v_0, state_mlstm_C_1, state_mlstm_n_1, state_mlstm_m_1, state_gla_S_2, cache_k_3, cache_v_3, c, c_ctx, w_mod, b_mod, norm1_g, norm2_g, final_g, router_w, router_b, moe_wg, moe_wu, moe_wd, attn0_w_qkv, attn0_sink, attn0_w_o, mlstm1_w_in, mlstm1_b_gates, mlstm1_conv, mlstm1_norm_g, mlstm1_w_out, gla2_w_in, gla2_w_a1, gla2_w_a2, gla2_b_a, gla2_norm_g, gla2_w_out, attn3_w_qkv, attn3_sink, attn3_w_o):
    n_ctx, ctx_len, d = x_prompt.shape
    n_lat, lat_len, _ = x_sample.shape
    depth = w_mod.shape[0]

    cvec = jnp.concatenate([c_ctx[None, :], c, jnp.zeros((8 - 1 - n_lat, d), F32)], axis=0)
    mod = _modulation(cvec, w_mod, b_mod).reshape(depth, 8, 6, 1, d)

    def mods(layer, kind, latent):
        return mod[layer, 1:1 + n_lat, kind] if latent else mod[layer, 0:1, kind]

    rwt = router_w.T
    rb = router_b.reshape(-1, 1)
    attn_w = {0: (attn0_w_qkv.astype(BF16), attn0_sink, attn0_w_o.astype(BF16), cache_k_0, cache_v_0),
              3: (attn3_w_qkv.astype(BF16), attn3_sink, attn3_w_o.astype(BF16), cache_k_3, cache_v_3)}
    ml_qw = ML_HEADS * ML_DK
    ml_vw = ML_HEADS * ML_DV
    ml_main = 2 * ml_qw + 2 * ml_vw
    ml_w_main = mlstm1_w_in[:, :ml_main].astype(BF16)
    ml_w_gates = mlstm1_w_in[:, ml_main:]
    ml_w_out = mlstm1_w_out.astype(BF16)
    gla_kw = GLA_HEADS * GLA_DK
    gla_w_in = gla2_w_in.astype(BF16)
    gla_w_a1 = jnp.pad(jnp.concatenate([gla2_w_a1[0], gla2_w_a1[1]], axis=1),
                       ((0, 0), (0, LANES - 2 * GLA_RANK))).astype(BF16)
    gla_w2 = jnp.zeros((LANES, 2 * gla_kw), F32)
    gla_w2 = gla_w2.at[:GLA_RANK, :gla_kw].set(gla2_w_a2[0]).at[GLA_RANK:2 * GLA_RANK, gla_kw:].set(gla2_w_a2[1])
    gla_w2 = gla_w2.astype(BF16)
    gla_ba = gla2_b_a.reshape(1, 2 * gla_kw)
    gla_w_out = gla2_w_out.astype(BF16)

    def run_stream(x, n_seq, seq_len, latent):
        states = []
        h = _rownorm(x, norm1_g[0], mods(0, 1, latent), mods(0, 0, latent))
        for layer in range(depth):
            tail = (mods(layer, 2, latent), norm2_g[layer], mods(layer, 4, latent), mods(layer, 3, latent), rwt, rb)
            kind = layer % 3
            if kind == 0:
                w_qkv, sink, w_o, ck, cv = attn_w[layer]
                qkv = _matmul(h, w_qkv)
                if latent:
                    att = _attn_lat(qkv, ck, cv, sink, n_seq, seq_len)
                else:
                    att = _attn_ctx(qkv, sink, n_seq, seq_len)
                    qw = ATT_HEADS * HEAD_DIM
                    kw = ATT_KV * HEAD_DIM
                    states.append(qkv[:, qw:qw + kw].reshape(n_seq, seq_len, ATT_KV, HEAD_DIM))
                    states.append(qkv[:, qw + kw:].reshape(n_seq, seq_len, ATT_KV, HEAD_DIM))
                x, h2, meta, wcol, counts = _proj("plain", (att,), w_o, x, *tail)
            elif kind == 1:
                p = _matmul(h, ml_w_main)
                g, gt = _ml_gates(h, ml_w_gates, mlstm1_b_gates)
                qk = _ml_conv(p, mlstm1_conv, n_seq, seq_len)
                st = (state_mlstm_C_1, state_mlstm_n_1, state_mlstm_m_1) if latent else None
                hf, hb, fin = _ml_scan(qk, p, g, gt, st, n_seq, seq_len)
                if not latent:
                    states.extend(fin)
                x, h2, meta, wcol, counts = _proj("mlstm", (hf, hb, p, (2 * ml_qw + ml_vw) // ml_vw, mlstm1_norm_g), ml_w_out,
                                    x, *tail)
            else:
                p = _matmul(h, gla_w_in)
                u = _matmul(h, gla_w_a1)
                of, ob, s_fin = _gla_scan(p, u, gla_w2, gla_ba, state_gla_S_2 if latent else None, n_seq, seq_len)
                if not latent:
                    states.append(s_fin)
                gla_vw = GLA_HEADS * GLA_DV
                x, h2, meta, wcol, counts = _proj("gla", (of, ob, p, (2 * gla_kw + gla_vw) // gla_vw, gla2_norm_g), gla_w_out,
                                    x, *tail)
            cnt = counts[:, 0]
            xs, info = _dispatch(h2, meta, cnt)
            ys = _ffn(xs, info, moe_wg[layer], moe_wu[layer], moe_wd[layer])
            gate2 = mods(layer, 5, latent)
            if layer + 1 < depth:
                x, h = _combine(x, ys, meta, cnt, wcol, gate2, norm1_g[layer + 1],
                                mod=(mods(layer + 1, 1, latent), mods(layer + 1, 0, latent)), out_x=True,
                                h_dtype=F32 if (layer + 1) % 3 == 1 else BF16)
            else:
                out = _combine(x, ys, meta, cnt, wcol, gate2, final_g, h_dtype=F32)
        return out.reshape(n_seq, seq_len, d), states

    y_prompt, new_state = run_stream(x_prompt.reshape(n_ctx * ctx_len, d), n_ctx, ctx_len, False)
    y_sample, _ = run_stream(x_sample.reshape(n_lat * lat_len, d), n_lat, lat_len, True)
    return (y_prompt, y_sample, *new_state)
```

```python
import functools

import jax
import jax.numpy as jnp
from jax import lax
from jax.experimental import pallas as pl
from jax.experimental.pallas import tpu as pltpu

F32 = jnp.float32
BF16 = jnp.bfloat16
HI = lax.Precision.HIGHEST

EPS = 1e-6
LOG2E = 1.4426950408889634
GRID_W = 64
ATT_HEADS = 16
ATT_KV = 4
ATT_GROUP = ATT_HEADS // ATT_KV
HEAD_DIM = 64
WINDOW = 128
Q_BLOCK = 128
ROPE_BASE = 10000.0
ML_HEADS = 8
ML_DK = 128
ML_DV = 256
ML_CHUNK = 128
GLA_HEADS = 4
GLA_DK = 128
GLA_DV = 256
GLA_RANK = 16
GLA_TAU = 16.0
GLA_SUB = 16
GLA_BLOCK = 256
N_EXPERTS = 16
N_GROUPS = 4
GROUP_SIZE = N_EXPERTS // N_GROUPS
LANES = 128
VMEM_LIMIT = 56 * 1024 * 1024


def _cparams(*sem):
    return pltpu.CompilerParams(dimension_semantics=sem, vmem_limit_bytes=VMEM_LIMIT)


def _dot(a, b):
    return jnp.dot(a, b, preferred_element_type=F32)


def _dot_nt(a, b, precision=None):
    return lax.dot_general(a, b, (((1,), (1,)), ((), ())), precision=precision, preferred_element_type=F32)


def _dot_tn(a, b):
    return lax.dot_general(a, b, (((0,), (0,)), ((), ())), preferred_element_type=F32)


def _sigmoid(x):
    return 1.0 / (1.0 + jnp.exp(-x))


def _silu(x):
    return x * _sigmoid(x)


def _log_sigmoid(x):
    return jnp.minimum(x, 0.0) - jnp.log(1.0 + jnp.exp(-jnp.abs(x)))


def _rms_rows(x, g):
    ms = jnp.mean(x * x, axis=-1, keepdims=True)
    return x * lax.rsqrt(ms + EPS) * g


def _mod_kernel(c_ref, w_ref, b_ref, o_ref):
    s = _silu(c_ref[...])
    o_ref[0] = _dot(s.astype(BF16), w_ref[0].astype(BF16)) + b_ref[0]


def _modulation(cvec, w_mod, b_mod):
    depth, d, n6 = w_mod.shape
    tn = 1536
    return pl.pallas_call(
        _mod_kernel,
        grid=(depth, n6 // tn),
        in_specs=[pl.BlockSpec((8, d), lambda l, j: (0, 0)),
                  pl.BlockSpec((1, d, tn), lambda l, j: (l, 0, j)),
                  pl.BlockSpec((1, 1, tn), lambda l, j: (l, 0, j))],
        out_specs=pl.BlockSpec((1, 8, tn), lambda l, j: (l, 0, j)),
        out_shape=jax.ShapeDtypeStruct((depth, 8, n6), F32),
        compiler_params=_cparams("parallel", "parallel"),
        name="adaln_modulation",
    )(cvec, w_mod, b_mod.reshape(depth, 1, n6))


def _route(h, rwt, rb, carry):
    tm = h.shape[0]
    logits = _dot_nt(rwt, h, precision=HI)
    scores = _sigmoid(logits)
    sel = scores + rb
    rows = [sel[e:e + 1, :] for e in range(N_EXPERTS)]
    gscore = []
    for g in range(N_GROUPS):
        r = rows[GROUP_SIZE * g:GROUP_SIZE * (g + 1)]
        best = None
        for i in range(GROUP_SIZE):
            for j in range(i + 1, GROUP_SIZE):
                pair = r[i] + r[j]
                best = pair if best is None else jnp.maximum(best, pair)
        gscore.append(best)
    gmax = functools.reduce(jnp.maximum, gscore)
    taken = jnp.zeros_like(gmax)
    weights = []
    picks = []
    for g in range(N_GROUPS):
        hit = jnp.where(gscore[g] == gmax, 1.0, 0.0) * (1.0 - taken)
        taken = taken + hit
        r = rows[GROUP_SIZE * g:GROUP_SIZE * (g + 1)]
        for i in range(GROUP_SIZE):
            rank = jnp.zeros_like(gmax)
            for j in range(GROUP_SIZE):
                if j == i:
                    continue
                beats = (r[j] >= r[i]) if j < i else (r[j] > r[i])
                rank = rank + jnp.where(beats, 1.0, 0.0)
            chosen = hit * jnp.where(rank < 2.0, 1.0, 0.0)
            e = GROUP_SIZE * g + i
            picks.append(chosen)
            weights.append(jnp.where(chosen > 0.5, scores[e:e + 1, :], 0.0))
    wsum = functools.reduce(lambda a, b: a + b, weights)
    e_a = functools.reduce(jnp.minimum, [jnp.where(p > 0.5, float(e), float(N_EXPERTS)) for e, p in enumerate(picks)])
    e_b = functools.reduce(jnp.maximum, [jnp.where(p > 0.5, float(e), -1.0) for e, p in enumerate(picks)])
    picked = jnp.concatenate(picks, axis=0)
    before = (lax.broadcasted_iota(jnp.int32, (tm, tm), 0) < lax.broadcasted_iota(jnp.int32, (tm, tm), 1))
    rank = _dot(picked.astype(BF16), jnp.where(before, 1.0, 0.0).astype(BF16)) + carry
    zero = jnp.zeros_like(wsum)
    r_a, r_b, w_a, w_b = zero, zero, zero, zero
    for e in range(N_EXPERTS):
        is_a = e_a == float(e)
        is_b = e_b == float(e)
        r_a = r_a + jnp.where(is_a, rank[e:e + 1, :], 0.0)
        r_b = r_b + jnp.where(is_b, rank[e:e + 1, :], 0.0)
        w_a = w_a + jnp.where(is_a, weights[e], 0.0)
        w_b = w_b + jnp.where(is_b, weights[e], 0.0)
    meta = jnp.concatenate([e_a, e_b, r_a, r_b, jnp.zeros((4, tm), F32)], axis=0).astype(jnp.int32)
    wcol = jnp.concatenate([w_a / wsum, w_b / wsum, jnp.zeros((LANES - 2, tm), F32)], axis=0).T
    return meta, wcol, carry + jnp.sum(picked, axis=1, keepdims=True)


def _norm_mod(x, g_ref, mod_refs):
    h = _rms_rows(x, g_ref[...])
    if mod_refs is not None:
        a_ref, s_ref = mod_refs
        h = h * (1.0 + a_ref[0]) + s_ref[0]
    return h


def _rownorm_kernel(x_ref, g_ref, a_ref, s_ref, h_ref):
    h_ref[...] = _norm_mod(x_ref[...], g_ref, (a_ref, s_ref)).astype(h_ref.dtype)


def _mod_spec(n_mod, rows, tm, d, n_prefetch=0):
    per = (rows // n_mod) // tm
    return pl.BlockSpec((1, 1, d), lambda i, *_: (i // per, 0, 0))


def _rownorm(x, g, scale, shift, *, tm=512):
    rows, d = x.shape
    row_spec = pl.BlockSpec((tm, d), lambda i: (i, 0))
    return pl.pallas_call(
        _rownorm_kernel, grid=(rows // tm,),
        in_specs=[row_spec, pl.BlockSpec((1, d), lambda i: (0, 0)), _mod_spec(scale.shape[0], rows, tm, d),
                  _mod_spec(shift.shape[0], rows, tm, d)],
        out_specs=row_spec, out_shape=jax.ShapeDtypeStruct((rows, d), BF16),
        compiler_params=_cparams("parallel"), name="rownorm",
    )(x, g.reshape(1, d), scale, shift)


def _mm_kernel(a_ref, w_ref, o_ref):
    o_ref[...] = _dot(a_ref[...].astype(BF16), w_ref[...]).astype(o_ref.dtype)


def _matmul(a, w, *, out_dtype=F32, tm=1024):
    m, k = a.shape
    n = w.shape[1]
    tn = next(t for t in (1024, 768, 512, LANES) if n % t == 0)
    return pl.pallas_call(
        _mm_kernel, grid=(m // tm, n // tn),
        in_specs=[pl.BlockSpec((tm, k), lambda i, j: (i, 0)), pl.BlockSpec((k, tn), lambda i, j: (0, j))],
        out_specs=pl.BlockSpec((tm, tn), lambda i, j: (i, j)),
        out_shape=jax.ShapeDtypeStruct((m, n), out_dtype),
        compiler_params=_cparams("parallel", "parallel"), name="matmul",
    )(a, w)


def _head_norm(x, g, n_heads, dv):
    outs = []
    for h in range(n_heads):
        xs = x[:, h * dv:(h + 1) * dv]
        ms = jnp.mean(xs * xs, axis=-1, keepdims=True)
        outs.append(xs * lax.rsqrt(ms + EPS) * g[:, h * dv:(h + 1) * dv])
    return jnp.concatenate(outs, axis=1)


def _proj_kernel(*refs, pre):
    refs = list(refs)
    if pre == "plain":
        a = refs.pop(0)[...]
    else:
        f_ref, b_ref, p_ref, hg_ref = refs.pop(0), refs.pop(0), refs.pop(0), refs.pop(0)
        hsum = f_ref[...] + b_ref[...]
        if pre == "mlstm":
            a = _sigmoid(p_ref[...]) * _head_norm(hsum, hg_ref[...], ML_HEADS, ML_DV)
        else:
            a = _head_norm(hsum, hg_ref[...], GLA_HEADS, GLA_DV) * _silu(p_ref[...])
        a = a.astype(BF16)
    w_ref, x_ref, gate_ref, g_ref, a_ref, s_ref, rwt_ref, rb_ref = refs[:8]
    xo_ref, h_ref, meta_ref, wcol_ref, count_ref, carry_ref = refs[8:]

    @pl.when(pl.program_id(0) == 0)
    def _():
        carry_ref[...] = jnp.zeros_like(carry_ref)

    x = x_ref[...] + gate_ref[0] * _dot(a, w_ref[...])
    xo_ref[...] = x
    h = _norm_mod(x, g_ref, (a_ref, s_ref))
    h_ref[...] = h
    meta, wcol, carry = _route(h, rwt_ref[...], rb_ref[...], carry_ref[:, 0:1])
    meta_ref[...] = meta
    wcol_ref[...] = wcol
    carry_ref[...] = jnp.broadcast_to(carry, carry_ref.shape)
    count_ref[...] = jnp.broadcast_to(carry, count_ref.shape).astype(jnp.int32)


def _proj(pre, pre_args, w_out, x, gate, g, scale, shift, rwt, rb, *, tm=256):
    rows, d = x.shape
    k = w_out.shape[0]
    row_spec = pl.BlockSpec((tm, d), lambda i: (i, 0))
    if pre == "plain":
        args, specs = [pre_args[0]], [pl.BlockSpec((tm, k), lambda i: (i, 0))]
    else:
        hf, hb, p, col_block, hg = pre_args
        wide = pl.BlockSpec((tm, k), lambda i: (i, 0))
        args = [hf, hb, p, hg.reshape(1, k)]
        specs = [wide, wide, pl.BlockSpec((tm, k), lambda i: (i, col_block)), pl.BlockSpec((1, k), lambda i: (0, 0))]
    args += [w_out, x, gate, g.reshape(1, d), scale, shift, rwt, rb]
    specs += [pl.BlockSpec((k, d), lambda i: (0, 0)), row_spec, _mod_spec(gate.shape[0], rows, tm, d),
              pl.BlockSpec((1, d), lambda i: (0, 0)), _mod_spec(scale.shape[0], rows, tm, d),
              _mod_spec(shift.shape[0], rows, tm, d), pl.BlockSpec(rwt.shape, lambda i: (0, 0)),
              pl.BlockSpec(rb.shape, lambda i: (0, 0))]
    return pl.pallas_call(
        functools.partial(_proj_kernel, pre=pre), grid=(rows // tm,), in_specs=specs,
        out_specs=[row_spec, row_spec, pl.BlockSpec((8, tm), lambda i: (0, i)),
                   pl.BlockSpec((tm, LANES), lambda i: (i, 0)), pl.BlockSpec((N_EXPERTS, LANES), lambda i: (0, 0))],
        out_shape=[jax.ShapeDtypeStruct((rows, d), F32), jax.ShapeDtypeStruct((rows, d), F32),
                   jax.ShapeDtypeStruct((8, rows), jnp.int32), jax.ShapeDtypeStruct((rows, LANES), F32),
                   jax.ShapeDtypeStruct((N_EXPERTS, LANES), jnp.int32)],
        scratch_shapes=[pltpu.VMEM((N_EXPERTS, LANES), F32)],
        compiler_params=_cparams("arbitrary"), name="proj_" + pre,
    )(*args)


MOE_TILE = 256
MOE_TILE_SHIFT = 8
MOE_TOKENS = 256
ROW_UNROLL = 8


def _slot_tiles(rows):
    return (2 * rows) // MOE_TILE + N_EXPERTS


def _expert_offsets(cnt_ref, off_ref):
    def per_expert(e, k):
        off_ref[e] = k * MOE_TILE
        return k + ((cnt_ref[e] + MOE_TILE - 1) >> MOE_TILE_SHIFT)
    return lax.fori_loop(0, N_EXPERTS, per_expert, 0)


def _dispatch_kernel(ea_ref, eb_ref, ra_ref, rb_ref, cnt_ref, h_ref, xs_ref, info_ref, off_ref, zero_ref, sem):
    i = pl.program_id(0)
    tm = h_ref.shape[0]
    n_tiles = info_ref.shape[0] - 1

    def tile_copy(tile):
        return pltpu.make_async_copy(zero_ref, xs_ref.at[pl.ds(tile * MOE_TILE, MOE_TILE), :], sem)

    @pl.when(i == 0)
    def _():
        zero_ref[...] = jnp.zeros_like(zero_ref)
        used = _expert_offsets(cnt_ref, off_ref)

        def per_expert(e, _):
            first = off_ref[e] >> MOE_TILE_SHIFT
            nt = (cnt_ref[e] + MOE_TILE - 1) >> MOE_TILE_SHIFT

            def fill(j, _):
                info_ref[first + j] = e
                return 0
            lax.fori_loop(0, nt, fill, 0)

            @pl.when(nt > 0)
            def _():
                tile_copy(first + nt - 1).start()
                tile_copy(first + nt - 1).wait()
            return 0
        lax.fori_loop(0, N_EXPERTS, per_expert, 0)
        info_ref[n_tiles] = used

        def tail(j, _):
            info_ref[j] = N_EXPERTS - 1
            tile_copy(j).start()
            tile_copy(j).wait()
            return 0
        lax.fori_loop(used, n_tiles, tail, 0)

    base = i * tm

    def row_copy(t, slot):
        return pltpu.make_async_copy(h_ref.at[pl.ds(t, 1), :], xs_ref.at[pl.ds(slot, 1), :], sem)

    def issue(t, _):
        g = base + t
        row_copy(t, off_ref[ea_ref[g]] + ra_ref[g]).start(priority=0)
        row_copy(t, off_ref[eb_ref[g]] + rb_ref[g]).start(priority=1)
        return 0
    lax.fori_loop(0, tm, issue, 0, unroll=ROW_UNROLL)
    for _ in range(2):
        pltpu.make_async_copy(h_ref, xs_ref.at[pl.ds(0, tm), :], sem).wait()


def _dispatch(h, meta, counts):
    rows, d = h.shape
    n_tiles = _slot_tiles(rows)
    tm = MOE_TOKENS
    grid_spec = pltpu.PrefetchScalarGridSpec(
        num_scalar_prefetch=5, grid=(rows // tm,),
        in_specs=[pl.BlockSpec((tm, d), lambda i, *_: (i, 0))],
        out_specs=[pl.BlockSpec(memory_space=pl.ANY), pl.BlockSpec(memory_space=pltpu.SMEM)],
        scratch_shapes=[pltpu.SMEM((N_EXPERTS,), jnp.int32), pltpu.VMEM((MOE_TILE, d), F32),
                        pltpu.SemaphoreType.DMA(())])
    return pl.pallas_call(
        _dispatch_kernel, grid_spec=grid_spec,
        out_shape=[jax.ShapeDtypeStruct((n_tiles * MOE_TILE, d), F32),
                   jax.ShapeDtypeStruct((n_tiles + 1,), jnp.int32)],
        compiler_params=_cparams("arbitrary"), name="moe_dispatch",
    )(meta[0], meta[1], meta[2], meta[3], counts, h)


def _ffn_kernel(info_ref, xs_ref, wg_ref, wu_ref, wd_ref, ys_ref, wg_s, wu_s, wd_s):
    i = pl.program_id(0)
    used = info_ref[info_ref.shape[0] - 1]
    fresh = jnp.logical_or(i == 0, info_ref[i] != info_ref[jnp.maximum(i - 1, 0)])

    @pl.when(jnp.logical_and(i < used, fresh))
    def _():
        wg_s[...] = wg_ref[0, 0].astype(BF16)
        wu_s[...] = wu_ref[0, 0].astype(BF16)
        wd_s[...] = wd_ref[0, 0].astype(BF16)

    @pl.when(i < used)
    def _():
        x = xs_ref[...].astype(BF16)
        hid = _silu(_dot(x, wg_s[...])) * _dot(x, wu_s[...])
        ys_ref[...] = _dot(hid.astype(BF16), wd_s[...])

    @pl.when(i >= used)
    def _():
        ys_ref[...] = jnp.zeros_like(ys_ref)


def _ffn(xs, info, wg, wu, wd, layer):
    slots, d = xs.shape
    n_tiles = slots // MOE_TILE
    f = wg.shape[3]

    def w_map(i, info):
        return (layer, info[i], 0, 0)

    grid_spec = pltpu.PrefetchScalarGridSpec(
        num_scalar_prefetch=1, grid=(n_tiles,),
        in_specs=[pl.BlockSpec((MOE_TILE, d), lambda i, info: (jnp.minimum(i, info[n_tiles] - 1), 0)),
                  pl.BlockSpec((1, 1, d, f), w_map), pl.BlockSpec((1, 1, d, f), w_map),
                  pl.BlockSpec((1, 1, f, d), w_map)],
        out_specs=pl.BlockSpec((MOE_TILE, d), lambda i, info: (i, 0)),
        scratch_shapes=[pltpu.VMEM((d, f), BF16), pltpu.VMEM((d, f), BF16), pltpu.VMEM((f, d), BF16)])
    return pl.pallas_call(
        _ffn_kernel, grid_spec=grid_spec, out_shape=jax.ShapeDtypeStruct((slots, d), F32),
        compiler_params=_cparams("arbitrary"), name="moe_ffn",
    )(info, xs, wg, wu, wd)


def _combine_kernel(*refs, has_mod, out_x):
    refs = list(refs)
    ea_ref, eb_ref, ra_ref, rb_ref, cnt_ref, x_ref, ys_ref, wcol_ref, gate_ref, g_ref = refs[:10]
    refs = refs[10:]
    mod_refs = (refs.pop(0), refs.pop(0)) if has_mod else None
    xo_ref = refs.pop(0) if out_x else None
    h_ref, off_ref, buf_a, buf_b, sems = refs
    i = pl.program_id(0)
    tm = x_ref.shape[0]

    def issue(tile, slot):
        base = tile * tm

        def body(t, _):
            g = base + t
            sa = off_ref[ea_ref[g]] + ra_ref[g]
            sb = off_ref[eb_ref[g]] + rb_ref[g]
            pltpu.make_async_copy(ys_ref.at[pl.ds(sa, 1), :], buf_a.at[slot, pl.ds(t, 1), :],
                                  sems.at[slot]).start(priority=0)
            pltpu.make_async_copy(ys_ref.at[pl.ds(sb, 1), :], buf_b.at[slot, pl.ds(t, 1), :],
                                  sems.at[slot]).start(priority=1)
            return 0
        lax.fori_loop(0, tm, body, 0, unroll=ROW_UNROLL)

    @pl.when(i == 0)
    def _():
        _expert_offsets(cnt_ref, off_ref)
        issue(0, 0)

    @pl.when(i + 1 < pl.num_programs(0))
    def _():
        issue(i + 1, (i + 1) % 2)

    slot = i % 2
    for buf in (buf_a, buf_b):
        pltpu.make_async_copy(ys_ref.at[pl.ds(0, tm), :], buf.at[slot], sems.at[slot]).wait()
    y = wcol_ref[:, 0:1] * buf_a[slot] + wcol_ref[:, 1:2] * buf_b[slot]
    x = x_ref[...] + gate_ref[0] * y
    if out_x:
        xo_ref[...] = x
    h_ref[...] = _norm_mod(x, g_ref, mod_refs).astype(h_ref.dtype)


def _combine(x, ys, meta, counts, wcol, gate, g, *, mod=None, out_x=False, h_dtype=BF16):
    rows, d = x.shape
    tm = MOE_TOKENS
    row_spec = pl.BlockSpec((tm, d), lambda i, *_: (i, 0))
    args = [x, ys, wcol, gate, g.reshape(1, d)]
    specs = [row_spec, pl.BlockSpec(memory_space=pl.ANY), pl.BlockSpec((tm, LANES), lambda i, *_: (i, 0)),
             _mod_spec(gate.shape[0], rows, tm, d), pl.BlockSpec((1, d), lambda i, *_: (0, 0))]
    if mod is not None:
        for m in mod:
            args.append(m)
            specs.append(_mod_spec(m.shape[0], rows, tm, d))
    out_shape, out_specs = [], []
    if out_x:
        out_shape.append(jax.ShapeDtypeStruct((rows, d), F32))
        out_specs.append(row_spec)
    out_shape.append(jax.ShapeDtypeStruct((rows, d), h_dtype))
    out_specs.append(row_spec)
    grid_spec = pltpu.PrefetchScalarGridSpec(
        num_scalar_prefetch=5, grid=(rows // tm,), in_specs=specs, out_specs=out_specs,
        scratch_shapes=[pltpu.SMEM((N_EXPERTS,), jnp.int32), pltpu.VMEM((2, tm, d), F32), pltpu.VMEM((2, tm, d), F32),
                        pltpu.SemaphoreType.DMA((2,))])
    outs = pl.pallas_call(
        functools.partial(_combine_kernel, has_mod=mod is not None, out_x=out_x), grid_spec=grid_spec,
        out_shape=out_shape, compiler_params=_cparams("arbitrary"), name="moe_combine",
    )(meta[0], meta[1], meta[2], meta[3], counts, *args)
    return outs if out_x else outs[0]


def _softmax_av(scores, values, sink_col):
    m = sink_col
    for s in scores:
        m = jnp.maximum(m, jnp.max(s, axis=-1, keepdims=True))
    den = jnp.exp(sink_col - m)
    acc = None
    for s, v in zip(scores, values):
        p = jnp.exp(s - m)
        den = den + jnp.sum(p, axis=-1, keepdims=True)
        pv = _dot(p.astype(BF16), v)
        acc = pv if acc is None else acc + pv
    return acc / den


def _sink_column(sink_ref, kv, rows):
    return jnp.concatenate([jnp.full((rows, 1), sink_ref[kv * ATT_GROUP + g], F32) for g in range(ATT_GROUP)], axis=0)


def _attn_ctx_kernel(sink_ref, qkv_ref, o_ref):
    t = qkv_ref.shape[0]
    qw = ATT_HEADS * HEAD_DIM
    kw = ATT_KV * HEAD_DIM
    heads_out = []
    for kv in range(ATT_KV):
        q = jnp.concatenate(
            [qkv_ref[:, (kv * ATT_GROUP + g) * HEAD_DIM:(kv * ATT_GROUP + g + 1) * HEAD_DIM] for g in range(ATT_GROUP)],
            axis=0).astype(BF16)
        k = qkv_ref[:, qw + kv * HEAD_DIM:qw + (kv + 1) * HEAD_DIM].astype(BF16)
        v = qkv_ref[:, qw + kw + kv * HEAD_DIM:qw + kw + (kv + 1) * HEAD_DIM].astype(BF16)
        s = _dot_nt(q, k) * HEAD_DIM ** -0.5
        o = _softmax_av([s], [v], _sink_column(sink_ref, kv, t))
        heads_out += [o[g * t:(g + 1) * t] for g in range(ATT_GROUP)]
    o_ref[...] = jnp.concatenate(heads_out, axis=1).astype(o_ref.dtype)


def _attn_ctx(qkv, sink, n_seq, seq_len):
    rows, cols = qkv.shape
    return pl.pallas_call(
        _attn_ctx_kernel, grid=(n_seq,),
        in_specs=[pl.BlockSpec(memory_space=pltpu.SMEM), pl.BlockSpec((seq_len, cols), lambda b: (b, 0))],
        out_specs=pl.BlockSpec((seq_len, ATT_HEADS * HEAD_DIM), lambda b: (b, 0)),
        out_shape=jax.ShapeDtypeStruct((rows, ATT_HEADS * HEAD_DIM), BF16),
        compiler_params=_cparams("parallel"), name="attn_context",
    )(sink, qkv)


def _rope_block(x, cos, sin_signed):
    lane = lax.broadcasted_iota(jnp.int32, x.shape, 1)
    nf = HEAD_DIM // 4
    partner = jnp.where((lane % (2 * nf)) < nf, pltpu.roll(x, LANES - nf, axis=1), pltpu.roll(x, nf, axis=1))
    return x * cos + partner * sin_signed


def _attn_lat_kernel(sink_ref, qkv_ref, ck_ref, cv_ref, cos_ref, sin_ref, o_ref, k_scr):
    i = pl.program_id(1)
    t = qkv_ref.shape[0]
    qw = ATT_HEADS * HEAD_DIM
    kw = ATT_KV * HEAD_DIM
    span = Q_BLOCK + 2 * WINDOW

    @pl.when(i == 0)
    def _():
        for c in range(kw // LANES):
            blk = qkv_ref[:, qw + c * LANES:qw + (c + 1) * LANES]
            k_scr[:, c * LANES:(c + 1) * LANES] = _rope_block(blk, cos_ref[...], sin_ref[...]).astype(BF16)

    r0 = pl.multiple_of(i * Q_BLOCK, Q_BLOCK)
    ws = pl.multiple_of(jnp.clip(r0 - WINDOW, 0, t - span), Q_BLOCK)
    cos_q = cos_ref[pl.ds(r0, Q_BLOCK), :]
    sin_q = sin_ref[pl.ds(r0, Q_BLOCK), :]
    qpos = r0 + lax.broadcasted_iota(jnp.int32, (Q_BLOCK, span), 0)
    kpos = ws + lax.broadcasted_iota(jnp.int32, (Q_BLOCK, span), 1)
    band = jnp.abs(qpos - kpos) <= WINDOW
    band = jnp.concatenate([band] * ATT_GROUP, axis=0)
    heads_out = []
    for kv in range(ATT_KV):
        heads = []
        for g in range(ATT_GROUP):
            h = kv * ATT_GROUP + g
            c, half = divmod(h * HEAD_DIM, LANES)
            blk = _rope_block(qkv_ref[pl.ds(r0, Q_BLOCK), c * LANES:(c + 1) * LANES], cos_q, sin_q)
            heads.append(blk[:, half:half + HEAD_DIM])
        q = jnp.concatenate(heads, axis=0).astype(BF16)
        ck = ck_ref[0, :, kv * HEAD_DIM:(kv + 1) * HEAD_DIM].astype(BF16)
        cv = cv_ref[0, :, kv * HEAD_DIM:(kv + 1) * HEAD_DIM].astype(BF16)
        kwin = k_scr[pl.ds(ws, span), kv * HEAD_DIM:(kv + 1) * HEAD_DIM]
        vwin = qkv_ref[pl.ds(ws, span), qw + kw + kv * HEAD_DIM:qw + kw + (kv + 1) * HEAD_DIM].astype(BF16)
        s_ctx = _dot_nt(q, ck) * HEAD_DIM ** -0.5
        s_win = jnp.where(band, _dot_nt(q, kwin) * HEAD_DIM ** -0.5, -jnp.inf)
        o = _softmax_av([s_ctx, s_win], [cv, vwin], _sink_column(sink_ref, kv, Q_BLOCK))
        heads_out += [o[g * Q_BLOCK:(g + 1) * Q_BLOCK] for g in range(ATT_GROUP)]
    o_ref[...] = jnp.concatenate(heads_out, axis=1).astype(o_ref.dtype)


def _rope_tables(seq_len):
    pos = jnp.arange(seq_len, dtype=jnp.int32)
    row = (pos // GRID_W).astype(F32)
    col = (pos % GRID_W).astype(F32)
    nf = HEAD_DIM // 4
    inv = ROPE_BASE ** (-jnp.arange(nf, dtype=F32) / nf)
    ang_r = row[:, None] * inv[None, :]
    ang_c = col[:, None] * inv[None, :]
    cos_h = jnp.concatenate([jnp.cos(ang_r), jnp.cos(ang_r), jnp.cos(ang_c), jnp.cos(ang_c)], axis=1)
    sin_h = jnp.concatenate([-jnp.sin(ang_r), jnp.sin(ang_r), -jnp.sin(ang_c), jnp.sin(ang_c)], axis=1)
    reps = LANES // HEAD_DIM
    return jnp.tile(cos_h, (1, reps)), jnp.tile(sin_h, (1, reps))


def _attn_lat(qkv, cache_k, cache_v, sink, n_seq, seq_len):
    rows, cols = qkv.shape
    past = cache_k.shape[1]
    kw = ATT_KV * HEAD_DIM
    cos, sin = _rope_tables(seq_len)
    return pl.pallas_call(
        _attn_lat_kernel, grid=(n_seq, seq_len // Q_BLOCK),
        in_specs=[pl.BlockSpec(memory_space=pltpu.SMEM),
                  pl.BlockSpec((seq_len, cols), lambda b, i: (b, 0)),
                  pl.BlockSpec((1, past, kw), lambda b, i: (b, 0, 0)),
                  pl.BlockSpec((1, past, kw), lambda b, i: (b, 0, 0)),
                  pl.BlockSpec((seq_len, LANES), lambda b, i: (0, 0)),
                  pl.BlockSpec((seq_len, LANES), lambda b, i: (0, 0))],
        out_specs=pl.BlockSpec((Q_BLOCK, ATT_HEADS * HEAD_DIM), lambda b, i: (b * (seq_len // Q_BLOCK) + i, 0)),
        out_shape=jax.ShapeDtypeStruct((rows, ATT_HEADS * HEAD_DIM), BF16),
        scratch_shapes=[pltpu.VMEM((seq_len, kw), BF16)],
        compiler_params=_cparams("parallel", "arbitrary"), name="attn_latent",
    )(sink, qkv, cache_k.reshape(n_seq, past, kw), cache_v.reshape(n_seq, past, kw), cos, sin)


def _ml_gates_kernel(h_ref, w_ref, wt_ref, b_ref, bt_ref, g_ref, gt_ref):
    h = h_ref[...]
    g_ref[...] = jnp.dot(h, w_ref[...], precision=HI, preferred_element_type=F32) + b_ref[...]
    gt_ref[...] = _dot_nt(wt_ref[...], h, precision=HI) + bt_ref[...]


def _ml_gates(h, w_gates, b_gates, *, tm=512):
    rows, d = h.shape
    ng = w_gates.shape[1]
    w_pad = jnp.pad(w_gates, ((0, 0), (0, LANES - ng)))
    b_pad = jnp.pad(b_gates, (0, LANES - ng)).reshape(1, LANES)
    return pl.pallas_call(
        _ml_gates_kernel, grid=(rows // tm,),
        in_specs=[pl.BlockSpec((tm, d), lambda i: (i, 0)), pl.BlockSpec((d, LANES), lambda i: (0, 0)),
                  pl.BlockSpec((ng, d), lambda i: (0, 0)), pl.BlockSpec((1, LANES), lambda i: (0, 0)),
                  pl.BlockSpec((ng, 1), lambda i: (0, 0))],
        out_specs=[pl.BlockSpec((tm, LANES), lambda i: (i, 0)), pl.BlockSpec((ng, tm), lambda i: (0, i))],
        out_shape=[jax.ShapeDtypeStruct((rows, LANES), F32), jax.ShapeDtypeStruct((ng, rows), F32)],
        compiler_params=_cparams("parallel"), name="mlstm_gates",
    )(h, w_pad, w_gates.T, b_pad, b_gates.reshape(ng, 1))


def _ml_conv_kernel(p_ref, w_ref, o_ref, *, k_scale):
    j = pl.program_id(1)
    x = p_ref[...]
    t = x.shape[0]
    row = lax.broadcasted_iota(jnp.int32, x.shape, 0)
    prev = jnp.where(row == 0, 0.0, pltpu.roll(x, 1, axis=0))
    nxt = jnp.where(row == t - 1, 0.0, pltpu.roll(x, t - 1, axis=0))
    y = prev * w_ref[0:1, :] + x * w_ref[1:2, :] + nxt * w_ref[2:3, :]
    scale = jnp.where(j >= pl.num_programs(1) // 2, k_scale, 1.0).astype(F32)
    o_ref[...] = (_silu(y) * scale).astype(o_ref.dtype)


def _ml_conv(p, conv_w, n_seq, seq_len, *, tn=512):
    rows = p.shape[0]
    width = conv_w.shape[1]
    return pl.pallas_call(
        functools.partial(_ml_conv_kernel, k_scale=ML_DK ** -0.5), grid=(n_seq, width // tn),
        in_specs=[pl.BlockSpec((seq_len, tn), lambda b, j: (b, j)), pl.BlockSpec((3, tn), lambda b, j: (0, j))],
        out_specs=pl.BlockSpec((seq_len, tn), lambda b, j: (b, j)),
        out_shape=jax.ShapeDtypeStruct((rows, width), BF16),
        compiler_params=_cparams("parallel", "parallel"), name="mlstm_conv",
    )(p, conv_w)


def _ml_scan_kernel(*refs, zero_init):
    refs = list(refs)
    dirs = [tuple(refs[0:5]), tuple(refs[5:10])]
    refs = refs[10:]
    if not zero_init:
        c0_ref, n0_ref, m0_ref = refs[:3]
        refs = refs[3:]
    hf_ref, hb_ref, c_ref, n_ref, m_ref = refs
    h_out = (hf_ref, hb_ref)
    c = pl.program_id(1)
    last = pl.num_programs(1) - 1

    @pl.when(c == 0)
    def _():
        if zero_init:
            c_ref[...] = jnp.zeros_like(c_ref)
            n_ref[...] = jnp.zeros_like(n_ref)
            m_ref[...] = jnp.zeros_like(m_ref)
        else:
            c_ref[...] = c0_ref[...]
            n_ref[...] = n0_ref[...]
            m_ref[...] = m0_ref[...]

    length = hf_ref.shape[0]
    ti = lax.broadcasted_iota(jnp.int32, (length, length), 0)
    si = lax.broadcasted_iota(jnp.int32, (length, length), 1)
    for d in range(2):
        q_ref, k_ref, v_ref, g_ref, gt_ref = dirs[d]
        causal = (ti >= si) if d == 0 else (ti <= si)
        tri = jnp.where(causal, 1.0, 0.0).astype(F32)
        f_col = _log_sigmoid(g_ref[...])
        f_row = _log_sigmoid(gt_ref[...])
        b_col = jnp.dot(tri, f_col, precision=HI, preferred_element_type=F32)
        b_row = _dot_nt(f_row, tri, precision=HI)
        edge = length - 1 if d == 0 else 0
        for h in range(ML_HEADS):
            ji = d * 2 * ML_HEADS + h
            jf = ji + ML_HEADS
            bc = b_col[:, jf:jf + 1]
            br = b_row[jf:jf + 1, :]
            i_row = gt_ref[ji:ji + 1, :]
            i_col = g_ref[:, ji:ji + 1]
            m_prev = m_ref[0, d, h][:, 0:1]
            q = q_ref[:, h * ML_DK:(h + 1) * ML_DK]
            k = k_ref[:, h * ML_DK:(h + 1) * ML_DK]
            v = v_ref[:, h * ML_DV:(h + 1) * ML_DV].astype(BF16)
            cst = c_ref[0, d, h]
            nst = n_ref[0, d, h]
            a_row = i_row - br
            amat = jnp.where(causal, a_row, -jnp.inf)
            u = jnp.maximum(m_prev, jnp.max(amat, axis=1, keepdims=True))
            qk = (_dot_nt(q, k) * jnp.exp(amat - u)).astype(BF16)
            sc = jnp.exp(m_prev - u)
            state_ext = jnp.concatenate([cst, jnp.broadcast_to(nst, (LANES, ML_DK))], axis=0).astype(BF16)
            v_ext = jnp.concatenate([v, jnp.ones((length, LANES), BF16)], axis=1)
            tot = sc * _dot_nt(q, state_ext) + _dot(qk, v_ext)
            inv = 1.0 / jnp.maximum(jnp.abs(tot[:, ML_DV:]), jnp.exp(-(bc + u)))
            h_out[d][:, h * ML_DV:(h + 1) * ML_DV] = tot[:, :ML_DV] * jnp.concatenate([inv] * (ML_DV // LANES), axis=1)
            b_last = br[:, edge:edge + 1]
            wlog_row = b_last + a_row
            m_new = jnp.maximum(b_last + m_prev, jnp.max(wlog_row, axis=1, keepdims=True))
            decay = jnp.exp(b_last + m_prev - m_new)
            ws_row = jnp.exp(wlog_row - m_new)
            ws_col = jnp.exp(b_last - bc + i_col - m_new)
            kw = (ws_col * k.astype(F32)).astype(BF16)
            c_ref[0, d, h] = decay * cst + _dot_tn(v, kw)
            n_ref[0, d, h] = decay * nst + _dot(jnp.broadcast_to(ws_row, (8, length)).astype(BF16), k)[0:1]
            m_ref[0, d, h] = jnp.broadcast_to(m_new, (1, ML_DK))


def _ml_scan(qk, p, g, gt, state, n_seq, seq_len):
    rows = qk.shape[0]
    length = min(ML_CHUNK, seq_len)
    nc = seq_len // length
    qw = ML_HEADS * ML_DK
    vw = ML_HEADS * ML_DV
    ng = gt.shape[0]

    def fwd(b, c):
        return b * nc + c

    def bwd(b, c):
        return b * nc + nc - 1 - c

    args, specs = [], []
    for pos in (fwd, bwd):
        args += [qk, qk, p, g, gt]
        specs += [pl.BlockSpec((length, qw), lambda b, c, pos=pos: (pos(b, c), 0)),
                  pl.BlockSpec((length, qw), lambda b, c, pos=pos: (pos(b, c), 1)),
                  pl.BlockSpec((length, vw), lambda b, c, pos=pos: (pos(b, c), 2 * qw // vw)),
                  pl.BlockSpec((length, LANES), lambda b, c, pos=pos: (pos(b, c), 0)),
                  pl.BlockSpec((ng, length), lambda b, c, pos=pos: (0, pos(b, c)))]
    c_spec = pl.BlockSpec((1, 2, ML_HEADS, ML_DV, ML_DK), lambda b, c: (b, 0, 0, 0, 0))
    n_spec = pl.BlockSpec((1, 2, ML_HEADS, 1, ML_DK), lambda b, c: (b, 0, 0, 0, 0))
    zero_init = state is None
    if not zero_init:
        c0, n0, m0 = state
        args += [c0, n0.reshape(n_seq, 2, ML_HEADS, 1, ML_DK),
                 jnp.broadcast_to(m0[..., None, None], (n_seq, 2, ML_HEADS, 1, ML_DK))]
        specs += [c_spec, n_spec, n_spec]
    hf, hb, c_fin, n_fin, m_fin = pl.pallas_call(
        functools.partial(_ml_scan_kernel, zero_init=zero_init), grid=(n_seq, nc), in_specs=specs,
        out_specs=[pl.BlockSpec((length, vw), lambda b, c: (fwd(b, c), 0)),
                   pl.BlockSpec((length, vw), lambda b, c: (bwd(b, c), 0)), c_spec, n_spec, n_spec],
        out_shape=[jax.ShapeDtypeStruct((rows, vw), F32), jax.ShapeDtypeStruct((rows, vw), F32),
                   jax.ShapeDtypeStruct((n_seq, 2, ML_HEADS, ML_DV, ML_DK), F32),
                   jax.ShapeDtypeStruct((n_seq, 2, ML_HEADS, 1, ML_DK), F32),
                   jax.ShapeDtypeStruct((n_seq, 2, ML_HEADS, 1, ML_DK), F32)],
        compiler_params=_cparams("parallel", "arbitrary"), name="mlstm_scan",
    )(*args)
    return hf, hb, (c_fin, n_fin[:, :, :, 0, :], m_fin[:, :, :, 0, 0])


def _gla_scan_kernel(*refs, zero_init):
    refs = list(refs)
    dirs = [tuple(refs[0:4]), tuple(refs[4:8])]
    w2_ref, ba_ref = refs[8:10]
    refs = refs[10:]
    if not zero_init:
        s0_ref = refs.pop(0)
    of_ref, ob_ref, s_ref, st_scr, la_scr = refs
    o_out = (of_ref, ob_ref)
    c = pl.program_id(1)
    last = pl.num_programs(1) - 1
    kw = GLA_HEADS * GLA_DK
    n_sub = of_ref.shape[0] // GLA_SUB

    @pl.when(c == 0)
    def _():
        for d in range(2):
            for h in range(GLA_HEADS):
                st_scr[d, h] = jnp.zeros((GLA_DV, GLA_DK), F32) if zero_init else s0_ref[0, d, h].T

    for d in range(2):
        u = dirs[d][3][...].astype(BF16)
        z = _dot(u, w2_ref[:, d * kw:(d + 1) * kw]) + ba_ref[:, d * kw:(d + 1) * kw]
        la_scr[d] = _log_sigmoid(z) / GLA_TAU

    ti = lax.broadcasted_iota(jnp.int32, (GLA_SUB, GLA_SUB), 0)
    si = lax.broadcasted_iota(jnp.int32, (GLA_SUB, GLA_SUB), 1)
    s_lane = lax.broadcasted_iota(jnp.int32, (GLA_SUB, GLA_SUB), 1)

    def sub_chunk(j, carry):
        for d in range(2):
            q_ref, k_ref, v_ref, _ = dirs[d]
            r0 = pl.multiple_of((j if d == 0 else n_sub - 1 - j) * GLA_SUB, GLA_SUB)
            causal = (ti >= si) if d == 0 else (ti <= si)
            tri = jnp.where(causal, 1.0, 0.0).astype(F32)
            bc_all = jnp.dot(tri, la_scr[d, pl.ds(r0, GLA_SUB), :], precision=HI, preferred_element_type=F32)
            edge = GLA_SUB - 1 if d == 0 else 0
            for h in range(GLA_HEADS):
                bc = bc_all[:, h * GLA_DK:(h + 1) * GLA_DK]
                q = q_ref[pl.ds(r0, GLA_SUB), h * GLA_DK:(h + 1) * GLA_DK] * GLA_DK ** -0.5
                k = k_ref[pl.ds(r0, GLA_SUB), h * GLA_DK:(h + 1) * GLA_DK]
                v = v_ref[pl.ds(r0, GLA_SUB), h * GLA_DV:(h + 1) * GLA_DV].astype(BF16)
                bc2 = bc * LOG2E
                a = jnp.zeros((GLA_SUB, GLA_SUB), F32)
                for s in range(GLA_SUB):
                    decay = jnp.exp2(bc2 - bc2[s:s + 1, :])
                    col = jnp.sum(q * (k[s:s + 1, :] * decay), axis=1, keepdims=True)
                    a = jnp.where(s_lane == s, col, a)
                a = jnp.where(causal, a, 0.0)
                st = st_scr[d, h]
                o = _dot(a.astype(BF16), v) + _dot_nt((q * jnp.exp2(bc2)).astype(BF16), st.astype(BF16))
                o_out[d][pl.ds(r0, GLA_SUB), h * GLA_DV:(h + 1) * GLA_DV] = o
                b_last = bc2[edge:edge + 1, :]
                k_dec = (k * jnp.exp2(b_last - bc2)).astype(BF16)
                st_scr[d, h] = jnp.exp2(b_last) * st + _dot_tn(v, k_dec)
        return carry

    lax.fori_loop(0, n_sub, sub_chunk, 0)

    @pl.when(c == last)
    def _():
        for d in range(2):
            for h in range(GLA_HEADS):
                s_ref[0, d, h] = st_scr[d, h].T


def _gla_scan(p, u, w2, b_a, state, n_seq, seq_len):
    rows = p.shape[0]
    length = min(GLA_BLOCK, seq_len)
    nc = seq_len // length
    kw = GLA_HEADS * GLA_DK
    vw = GLA_HEADS * GLA_DV

    def fwd(b, c):
        return b * nc + c

    def bwd(b, c):
        return b * nc + nc - 1 - c

    args, specs = [], []
    for pos in (fwd, bwd):
        args += [p, p, p, u]
        specs += [pl.BlockSpec((length, kw), lambda b, c, pos=pos: (pos(b, c), 0)),
                  pl.BlockSpec((length, kw), lambda b, c, pos=pos: (pos(b, c), 1)),
                  pl.BlockSpec((length, vw), lambda b, c, pos=pos: (pos(b, c), 2 * kw // vw)),
                  pl.BlockSpec((length, LANES), lambda b, c, pos=pos: (pos(b, c), 0))]
    args += [w2, b_a]
    specs += [pl.BlockSpec(w2.shape, lambda b, c: (0, 0)), pl.BlockSpec(b_a.shape, lambda b, c: (0, 0))]
    s_spec = pl.BlockSpec((1, 2, GLA_HEADS, GLA_DK, GLA_DV), lambda b, c: (b, 0, 0, 0, 0))
    zero_init = state is None
    if not zero_init:
        args.append(state)
        specs.append(s_spec)
    return pl.pallas_call(
        functools.partial(_gla_scan_kernel, zero_init=zero_init), grid=(n_seq, nc), in_specs=specs,
        out_specs=[pl.BlockSpec((length, vw), lambda b, c: (fwd(b, c), 0)),
                   pl.BlockSpec((length, vw), lambda b, c: (bwd(b, c), 0)), s_spec],
        out_shape=[jax.ShapeDtypeStruct((rows, vw), F32), jax.ShapeDtypeStruct((rows, vw), F32),
                   jax.ShapeDtypeStruct((n_seq, 2, GLA_HEADS, GLA_DK, GLA_DV), F32)],
        scratch_shapes=[pltpu.VMEM((2, GLA_HEADS, GLA_DV, GLA_DK), F32), pltpu.VMEM((2, length, kw), F32)],
        compiler_params=_cparams("parallel", "arbitrary"), name="gla_scan",
    )(*args)


def kernel(x_prompt, x_sample, cache_k_0, cache_v_0, state_mlstm_C_1, state_mlstm_n_1, state_mlstm_m_1, state_gla_S_2, cache_k_3, cache_v_3, c, c_ctx, w_mod, b_mod, norm1_g, norm2_g, final_g, router_w, router_b, moe_wg, moe_wu, moe_wd, attn0_w_qkv, attn0_sink, attn0_w_o, mlstm1_w_in, mlstm1_b_gates, mlstm1_conv, mlstm1_norm_g, mlstm1_w_out, gla2_w_in, gla2_w_a1, gla2_w_a2, gla2_b_a, gla2_norm_g, gla2_w_out, attn3_w_qkv, attn3_sink, attn3_w_o):
    n_ctx, ctx_len, d = x_prompt.shape
    n_lat, lat_len, _ = x_sample.shape
    depth = w_mod.shape[0]

    cvec = jnp.concatenate([c_ctx[None, :], c, jnp.zeros((8 - 1 - n_lat, d), F32)], axis=0)
    mod = _modulation(cvec, w_mod, b_mod).reshape(depth, 8, 6, 1, d)

    def mods(layer, kind, latent):
        return mod[layer, 1:1 + n_lat, kind] if latent else mod[layer, 0:1, kind]

    rwt = router_w.T
    rb = router_b.reshape(-1, 1)
    attn_w = {0: (attn0_w_qkv.astype(BF16), attn0_sink, attn0_w_o.astype(BF16), cache_k_0, cache_v_0),
              3: (attn3_w_qkv.astype(BF16), attn3_sink, attn3_w_o.astype(BF16), cache_k_3, cache_v_3)}
    ml_qw = ML_HEADS * ML_DK
    ml_vw = ML_HEADS * ML_DV
    ml_main = 2 * ml_qw + 2 * ml_vw
    ml_w_main = mlstm1_w_in[:, :ml_main].astype(BF16)
    ml_w_gates = mlstm1_w_in[:, ml_main:]
    ml_w_out = mlstm1_w_out.astype(BF16)
    gla_kw = GLA_HEADS * GLA_DK
    gla_w_in = gla2_w_in.astype(BF16)
    gla_w_a1 = jnp.pad(jnp.concatenate([gla2_w_a1[0], gla2_w_a1[1]], axis=1),
                       ((0, 0), (0, LANES - 2 * GLA_RANK))).astype(BF16)
    gla_w2 = jnp.zeros((LANES, 2 * gla_kw), F32)
    gla_w2 = gla_w2.at[:GLA_RANK, :gla_kw].set(gla2_w_a2[0]).at[GLA_RANK:2 * GLA_RANK, gla_kw:].set(gla2_w_a2[1])
    gla_w2 = gla_w2.astype(BF16)
    gla_ba = gla2_b_a.reshape(1, 2 * gla_kw)
    gla_w_out = gla2_w_out.astype(BF16)

    def run_stream(x, n_seq, seq_len, latent):
        states = []
        h = _rownorm(x, norm1_g[0], mods(0, 1, latent), mods(0, 0, latent))
        for layer in range(depth):
            tail = (mods(layer, 2, latent), norm2_g[layer], mods(layer, 4, latent), mods(layer, 3, latent), rwt, rb)
            kind = layer % 3
            if kind == 0:
                w_qkv, sink, w_o, ck, cv = attn_w[layer]
                qkv = _matmul(h, w_qkv)
                if latent:
                    att = _attn_lat(qkv, ck, cv, sink, n_seq, seq_len)
                else:
                    att = _attn_ctx(qkv, sink, n_seq, seq_len)
                    qw = ATT_HEADS * HEAD_DIM
                    kw = ATT_KV * HEAD_DIM
                    states.append(qkv[:, qw:qw + kw].reshape(n_seq, seq_len, ATT_KV, HEAD_DIM))
                    states.append(qkv[:, qw + kw:].reshape(n_seq, seq_len, ATT_KV, HEAD_DIM))
                x, h2, meta, wcol, counts = _proj("plain", (att,), w_o, x, *tail)
            elif kind == 1:
                p = _matmul(h, ml_w_main)
                g, gt = _ml_gates(h, ml_w_gates, mlstm1_b_gates)
                qk = _ml_conv(p, mlstm1_conv, n_seq, seq_len)
                st = (state_mlstm_C_1, state_mlstm_n_1, state_mlstm_m_1) if latent else None
                hf, hb, fin = _ml_scan(qk, p, g, gt, st, n_seq, seq_len)
                if not latent:
                    states.extend(fin)
                x, h2, meta, wcol, counts = _proj("mlstm", (hf, hb, p, (2 * ml_qw + ml_vw) // ml_vw, mlstm1_norm_g), ml_w_out,
                                    x, *tail)
            else:
                p = _matmul(h, gla_w_in)
                u = _matmul(h, gla_w_a1)
                of, ob, s_fin = _gla_scan(p, u, gla_w2, gla_ba, state_gla_S_2 if latent else None, n_seq, seq_len)
                if not latent:
                    states.append(s_fin)
                gla_vw = GLA_HEADS * GLA_DV
                x, h2, meta, wcol, counts = _proj("gla", (of, ob, p, (2 * gla_kw + gla_vw) // gla_vw, gla2_norm_g), gla_w_out,
                                    x, *tail)
            cnt = counts[:, 0]
            xs, info = _dispatch(h2, meta, cnt)
            ys = _ffn(xs, info, moe_wg, moe_wu, moe_wd, layer)
            gate2 = mods(layer, 5, latent)
            if layer + 1 < depth:
                x, h = _combine(x, ys, meta, cnt, wcol, gate2, norm1_g[layer + 1],
                                mod=(mods(layer + 1, 1, latent), mods(layer + 1, 0, latent)), out_x=True,
                                h_dtype=F32 if (layer + 1) % 3 == 1 else BF16)
            else:
                out = _combine(x, ys, meta, cnt, wcol, gate2, final_g, h_dtype=F32)
        return out.reshape(n_seq, seq_len, d), states

    y_prompt, new_state = run_stream(x_prompt.reshape(n_ctx * ctx_len, d), n_ctx, ctx_len, False)
    y_sample, _ = run_stream(x_sample.reshape(n_lat * lat_len, d), n_lat, lat_len, True)
    return (y_prompt, y_sample, *new_state)
```

```python
import functools

import jax
import jax.numpy as jnp
from jax import lax
from jax.experimental import pallas as pl
from jax.experimental.pallas import tpu as pltpu

F32 = jnp.float32
BF16 = jnp.bfloat16
HI = lax.Precision.HIGHEST

EPS = 1e-6
LOG2E = 1.4426950408889634
GRID_W = 64
ATT_HEADS = 16
ATT_KV = 4
ATT_GROUP = ATT_HEADS // ATT_KV
HEAD_DIM = 64
WINDOW = 128
Q_BLOCK = 128
ROPE_BASE = 10000.0
ML_HEADS = 8
ML_DK = 128
ML_DV = 256
ML_CHUNK = 128
GLA_HEADS = 4
GLA_DK = 128
GLA_DV = 256
GLA_RANK = 16
GLA_TAU = 16.0
GLA_SUB = 16
GLA_BLOCK = 256
N_EXPERTS = 16
N_GROUPS = 4
GROUP_SIZE = N_EXPERTS // N_GROUPS
LANES = 128
VMEM_LIMIT = 56 * 1024 * 1024


def _cparams(*sem):
    return pltpu.CompilerParams(dimension_semantics=sem, vmem_limit_bytes=VMEM_LIMIT)


def _dot(a, b):
    return jnp.dot(a, b, preferred_element_type=F32)


def _dot_nt(a, b, precision=None):
    return lax.dot_general(a, b, (((1,), (1,)), ((), ())), precision=precision, preferred_element_type=F32)


def _dot_tn(a, b):
    return lax.dot_general(a, b, (((0,), (0,)), ((), ())), preferred_element_type=F32)


def _sigmoid(x):
    return 1.0 / (1.0 + jnp.exp(-x))


def _silu(x):
    return x * _sigmoid(x)


def _log_sigmoid(x):
    return jnp.minimum(x, 0.0) - jnp.log(1.0 + jnp.exp(-jnp.abs(x)))


def _rms_rows(x, g):
    ms = jnp.mean(x * x, axis=-1, keepdims=True)
    return x * lax.rsqrt(ms + EPS) * g


def _mod_kernel(c_ref, w_ref, b_ref, o_ref):
    s = _silu(c_ref[...])
    o_ref[0] = _dot(s.astype(BF16), w_ref[0].astype(BF16)) + b_ref[0]


def _modulation(cvec, w_mod, b_mod):
    depth, d, n6 = w_mod.shape
    tn = 1536
    return pl.pallas_call(
        _mod_kernel,
        grid=(depth, n6 // tn),
        in_specs=[pl.BlockSpec((8, d), lambda l, j: (0, 0)),
                  pl.BlockSpec((1, d, tn), lambda l, j: (l, 0, j)),
                  pl.BlockSpec((1, 1, tn), lambda l, j: (l, 0, j))],
        out_specs=pl.BlockSpec((1, 8, tn), lambda l, j: (l, 0, j)),
        out_shape=jax.ShapeDtypeStruct((depth, 8, n6), F32),
        compiler_params=_cparams("parallel", "parallel"),
        name="adaln_modulation",
    )(cvec, w_mod, b_mod.reshape(depth, 1, n6))


def _route(h, rwt, rb, carry):
    tm = h.shape[0]
    h_hi = h.astype(BF16)
    h_lo = (h - h_hi.astype(F32)).astype(BF16)
    by_hi = _dot_nt(rwt, h_hi)
    logits = by_hi[:N_EXPERTS] + by_hi[N_EXPERTS:] + _dot_nt(rwt[:N_EXPERTS], h_lo)
    scores = _sigmoid(logits)
    sel = scores + rb
    expert = lax.broadcasted_iota(jnp.int32, sel.shape, 0)
    pos = expert % GROUP_SIZE
    grp = expert // GROUP_SIZE

    def mate(x, k):
        ahead = pltpu.roll(x, N_EXPERTS - k, axis=0)
        behind = pltpu.roll(x, GROUP_SIZE - k, axis=0)
        return jnp.where(pos + k < GROUP_SIZE, ahead, behind)

    beaten = jnp.zeros_like(sel)
    for k in range(1, GROUP_SIZE):
        other = mate(sel, k)
        other_first = (pos + k) % GROUP_SIZE < pos
        beaten = beaten + jnp.where(other_first, jnp.where(other >= sel, 1.0, 0.0), jnp.where(other > sel, 1.0, 0.0))
    top2 = jnp.where(beaten < 2.0, sel, 0.0)
    gscore = top2
    for k in range(1, GROUP_SIZE):
        gscore = gscore + mate(top2, k)
    lost = jnp.zeros_like(sel)
    for k in range(1, N_GROUPS):
        other = pltpu.roll(gscore, N_EXPERTS - GROUP_SIZE * k, axis=0)
        other_first = (grp + k) % N_GROUPS < grp
        lost = lost + jnp.where(other_first, jnp.where(other >= gscore, 1.0, 0.0),
                                jnp.where(other > gscore, 1.0, 0.0))
    picked = jnp.where(lost < 0.5, jnp.where(beaten < 2.0, 1.0, 0.0), 0.0)
    chosen = picked > 0.5
    weight = jnp.where(chosen, scores, 0.0)
    wsum = jnp.sum(weight, axis=0, keepdims=True)
    e_f = expert.astype(F32)
    e_a = jnp.min(jnp.where(chosen, e_f, float(N_EXPERTS)), axis=0, keepdims=True)
    e_b = jnp.max(jnp.where(chosen, e_f, -1.0), axis=0, keepdims=True)
    before = (lax.broadcasted_iota(jnp.int32, (tm, tm), 0) < lax.broadcasted_iota(jnp.int32, (tm, tm), 1))
    rank = _dot(picked.astype(BF16), jnp.where(before, 1.0, 0.0).astype(BF16)) + carry
    is_a = e_f == e_a
    is_b = e_f == e_b
    r_a = jnp.sum(jnp.where(is_a, rank, 0.0), axis=0, keepdims=True)
    r_b = jnp.sum(jnp.where(is_b, rank, 0.0), axis=0, keepdims=True)
    w_a = jnp.sum(jnp.where(is_a, weight, 0.0), axis=0, keepdims=True)
    w_b = jnp.sum(jnp.where(is_b, weight, 0.0), axis=0, keepdims=True)
    meta = jnp.concatenate([e_a, e_b, r_a, r_b, jnp.zeros((4, tm), F32)], axis=0).astype(jnp.int32)
    wcol = jnp.concatenate([w_a / wsum, w_b / wsum, jnp.zeros((LANES - 2, tm), F32)], axis=0).T
    return meta, wcol, carry + jnp.sum(picked, axis=1, keepdims=True)


def _norm_mod(x, g_ref, mod_refs):
    h = _rms_rows(x, g_ref[...])
    if mod_refs is not None:
        a_ref, s_ref = mod_refs
        h = h * (1.0 + a_ref[0]) + s_ref[0]
    return h


def _rownorm_kernel(x_ref, g_ref, a_ref, s_ref, h_ref):
    h_ref[...] = _norm_mod(x_ref[...], g_ref, (a_ref, s_ref)).astype(h_ref.dtype)


def _mod_spec(n_mod, rows, tm, d, n_prefetch=0):
    per = (rows // n_mod) // tm
    return pl.BlockSpec((1, 1, d), lambda i, *_: (i // per, 0, 0))


def _rownorm(x, g, scale, shift, *, tm=512):
    rows, d = x.shape
    row_spec = pl.BlockSpec((tm, d), lambda i: (i, 0))
    return pl.pallas_call(
        _rownorm_kernel, grid=(rows // tm,),
        in_specs=[row_spec, pl.BlockSpec((1, d), lambda i: (0, 0)), _mod_spec(scale.shape[0], rows, tm, d),
                  _mod_spec(shift.shape[0], rows, tm, d)],
        out_specs=row_spec, out_shape=jax.ShapeDtypeStruct((rows, d), BF16),
        compiler_params=_cparams("parallel"), name="rownorm",
    )(x, g.reshape(1, d), scale, shift)


def _mm_kernel(a_ref, w_ref, o_ref):
    o_ref[...] = _dot(a_ref[...].astype(BF16), w_ref[...]).astype(o_ref.dtype)


def _matmul(a, w, *, out_dtype=F32, tm=1024):
    m, k = a.shape
    n = w.shape[1]
    tn = next(t for t in (1024, 768, 512, LANES) if n % t == 0)
    return pl.pallas_call(
        _mm_kernel, grid=(m // tm, n // tn),
        in_specs=[pl.BlockSpec((tm, k), lambda i, j: (i, 0)), pl.BlockSpec((k, tn), lambda i, j: (0, j))],
        out_specs=pl.BlockSpec((tm, tn), lambda i, j: (i, j)),
        out_shape=jax.ShapeDtypeStruct((m, n), out_dtype),
        compiler_params=_cparams("parallel", "parallel"), name="matmul",
    )(a, w)


def _head_norm(x, g, n_heads, dv):
    outs = []
    for h in range(n_heads):
        xs = x[:, h * dv:(h + 1) * dv]
        ms = jnp.mean(xs * xs, axis=-1, keepdims=True)
        outs.append(xs * lax.rsqrt(ms + EPS) * g[:, h * dv:(h + 1) * dv])
    return jnp.concatenate(outs, axis=1)


def _proj_kernel(*refs, pre):
    refs = list(refs)
    if pre == "plain":
        a = refs.pop(0)[...]
    else:
        f_ref, b_ref, p_ref, hg_ref = refs.pop(0), refs.pop(0), refs.pop(0), refs.pop(0)
        hsum = f_ref[...] + b_ref[...]
        if pre == "mlstm":
            a = _sigmoid(p_ref[...]) * _head_norm(hsum, hg_ref[...], ML_HEADS, ML_DV)
        else:
            a = _head_norm(hsum, hg_ref[...], GLA_HEADS, GLA_DV) * _silu(p_ref[...])
        a = a.astype(BF16)
    w_ref, x_ref, gate_ref, g_ref, a_ref, s_ref, rwt_ref, rb_ref = refs[:8]
    xo_ref, h_ref, meta_ref, wcol_ref, count_ref, carry_ref = refs[8:]

    @pl.when(pl.program_id(0) == 0)
    def _():
        carry_ref[...] = jnp.zeros_like(carry_ref)

    x = x_ref[...] + gate_ref[0] * _dot(a, w_ref[...])
    xo_ref[...] = x
    h = _norm_mod(x, g_ref, (a_ref, s_ref))
    h_ref[...] = h
    meta, wcol, carry = _route(h, rwt_ref[...], rb_ref[...], carry_ref[:, 0:1])
    meta_ref[...] = meta
    wcol_ref[...] = wcol
    carry_ref[...] = jnp.broadcast_to(carry, carry_ref.shape)
    count_ref[...] = jnp.broadcast_to(carry, count_ref.shape).astype(jnp.int32)


def _proj(pre, pre_args, w_out, x, gate, g, scale, shift, rwt, rb, *, tm=256):
    rows, d = x.shape
    k = w_out.shape[0]
    row_spec = pl.BlockSpec((tm, d), lambda i: (i, 0))
    if pre == "plain":
        args, specs = [pre_args[0]], [pl.BlockSpec((tm, k), lambda i: (i, 0))]
    else:
        hf, hb, p, col_block, hg = pre_args
        wide = pl.BlockSpec((tm, k), lambda i: (i, 0))
        args = [hf, hb, p, hg.reshape(1, k)]
        specs = [wide, wide, pl.BlockSpec((tm, k), lambda i: (i, col_block)), pl.BlockSpec((1, k), lambda i: (0, 0))]
    args += [w_out, x, gate, g.reshape(1, d), scale, shift, rwt, rb]
    specs += [pl.BlockSpec((k, d), lambda i: (0, 0)), row_spec, _mod_spec(gate.shape[0], rows, tm, d),
              pl.BlockSpec((1, d), lambda i: (0, 0)), _mod_spec(scale.shape[0], rows, tm, d),
              _mod_spec(shift.shape[0], rows, tm, d), pl.BlockSpec(rwt.shape, lambda i: (0, 0)),
              pl.BlockSpec(rb.shape, lambda i: (0, 0))]
    return pl.pallas_call(
        functools.partial(_proj_kernel, pre=pre), grid=(rows // tm,), in_specs=specs,
        out_specs=[row_spec, row_spec, pl.BlockSpec((8, tm), lambda i: (0, i)),
                   pl.BlockSpec((tm, LANES), lambda i: (i, 0)), pl.BlockSpec((N_EXPERTS, LANES), lambda i: (0, 0))],
        out_shape=[jax.ShapeDtypeStruct((rows, d), F32), jax.ShapeDtypeStruct((rows, d), F32),
                   jax.ShapeDtypeStruct((8, rows), jnp.int32), jax.ShapeDtypeStruct((rows, LANES), F32),
                   jax.ShapeDtypeStruct((N_EXPERTS, LANES), jnp.int32)],
        scratch_shapes=[pltpu.VMEM((N_EXPERTS, LANES), F32)],
        compiler_params=_cparams("arbitrary"), name="proj_" + pre,
    )(*args)


MOE_TILE = 256
MOE_TILE_SHIFT = 8
MOE_TOKENS = 256
ROW_UNROLL = 8


def _slot_tiles(rows):
    return (2 * rows) // MOE_TILE + N_EXPERTS


def _expert_offsets(cnt_ref, off_ref):
    def per_expert(e, k):
        off_ref[e] = k * MOE_TILE
        return k + ((cnt_ref[e] + MOE_TILE - 1) >> MOE_TILE_SHIFT)
    return lax.fori_loop(0, N_EXPERTS, per_expert, 0)


def _dispatch_kernel(ea_ref, eb_ref, ra_ref, rb_ref, cnt_ref, h_ref, xs_ref, info_ref, off_ref, zero_ref, sem):
    i = pl.program_id(0)
    tm = h_ref.shape[0]
    n_tiles = info_ref.shape[0] - 1

    def tile_copy(tile):
        return pltpu.make_async_copy(zero_ref, xs_ref.at[pl.ds(tile * MOE_TILE, MOE_TILE), :], sem)

    @pl.when(i == 0)
    def _():
        zero_ref[...] = jnp.zeros_like(zero_ref)
        used = _expert_offsets(cnt_ref, off_ref)

        def per_expert(e, _):
            first = off_ref[e] >> MOE_TILE_SHIFT
            nt = (cnt_ref[e] + MOE_TILE - 1) >> MOE_TILE_SHIFT

            def fill(j, _):
                info_ref[first + j] = e
                return 0
            lax.fori_loop(0, nt, fill, 0)

            @pl.when(nt > 0)
            def _():
                tile_copy(first + nt - 1).start()
                tile_copy(first + nt - 1).wait()
            return 0
        lax.fori_loop(0, N_EXPERTS, per_expert, 0)
        info_ref[n_tiles] = used

        def tail(j, _):
            info_ref[j] = N_EXPERTS - 1
            tile_copy(j).start()
            tile_copy(j).wait()
            return 0
        lax.fori_loop(used, n_tiles, tail, 0)

    base = i * tm

    def row_copy(t, slot):
        return pltpu.make_async_copy(h_ref.at[pl.ds(t, 1), :], xs_ref.at[pl.ds(slot, 1), :], sem)

    def issue(t, _):
        g = base + t
        row_copy(t, off_ref[ea_ref[g]] + ra_ref[g]).start(priority=0)
        row_copy(t, off_ref[eb_ref[g]] + rb_ref[g]).start(priority=1)
        return 0
    lax.fori_loop(0, tm, issue, 0, unroll=ROW_UNROLL)
    for _ in range(2):
        pltpu.make_async_copy(h_ref, xs_ref.at[pl.ds(0, tm), :], sem).wait()


def _dispatch(h, meta, counts):
    rows, d = h.shape
    n_tiles = _slot_tiles(rows)
    tm = MOE_TOKENS
    grid_spec = pltpu.PrefetchScalarGridSpec(
        num_scalar_prefetch=5, grid=(rows // tm,),
        in_specs=[pl.BlockSpec((tm, d), lambda i, *_: (i, 0))],
        out_specs=[pl.BlockSpec(memory_space=pl.ANY), pl.BlockSpec(memory_space=pltpu.SMEM)],
        scratch_shapes=[pltpu.SMEM((N_EXPERTS,), jnp.int32), pltpu.VMEM((MOE_TILE, d), F32),
                        pltpu.SemaphoreType.DMA(())])
    return pl.pallas_call(
        _dispatch_kernel, grid_spec=grid_spec,
        out_shape=[jax.ShapeDtypeStruct((n_tiles * MOE_TILE, d), F32),
                   jax.ShapeDtypeStruct((n_tiles + 1,), jnp.int32)],
        compiler_params=_cparams("arbitrary"), name="moe_dispatch",
    )(meta[0], meta[1], meta[2], meta[3], counts, h)


def _ffn_kernel(info_ref, xs_ref, wg_ref, wu_ref, wd_ref, ys_ref, wg_s, wu_s, wd_s):
    i = pl.program_id(0)
    used = info_ref[info_ref.shape[0] - 1]
    fresh = jnp.logical_or(i == 0, info_ref[i] != info_ref[jnp.maximum(i - 1, 0)])

    @pl.when(jnp.logical_and(i < used, fresh))
    def _():
        wg_s[...] = wg_ref[0, 0].astype(BF16)
        wu_s[...] = wu_ref[0, 0].astype(BF16)
        wd_s[...] = wd_ref[0, 0].astype(BF16)

    @pl.when(i < used)
    def _():
        x = xs_ref[...].astype(BF16)
        hid = _silu(_dot(x, wg_s[...])) * _dot(x, wu_s[...])
        ys_ref[...] = _dot(hid.astype(BF16), wd_s[...])

    @pl.when(i >= used)
    def _():
        ys_ref[...] = jnp.zeros_like(ys_ref)


def _ffn(xs, info, wg, wu, wd, layer):
    slots, d = xs.shape
    n_tiles = slots // MOE_TILE
    f = wg.shape[3]

    def w_map(i, info):
        return (layer, info[i], 0, 0)

    grid_spec = pltpu.PrefetchScalarGridSpec(
        num_scalar_prefetch=1, grid=(n_tiles,),
        in_specs=[pl.BlockSpec((MOE_TILE, d), lambda i, info: (jnp.minimum(i, info[n_tiles] - 1), 0)),
                  pl.BlockSpec((1, 1, d, f), w_map), pl.BlockSpec((1, 1, d, f), w_map),
                  pl.BlockSpec((1, 1, f, d), w_map)],
        out_specs=pl.BlockSpec((MOE_TILE, d), lambda i, info: (i, 0)),
        scratch_shapes=[pltpu.VMEM((d, f), BF16), pltpu.VMEM((d, f), BF16), pltpu.VMEM((f, d), BF16)])
    return pl.pallas_call(
        _ffn_kernel, grid_spec=grid_spec, out_shape=jax.ShapeDtypeStruct((slots, d), F32),
        compiler_params=_cparams("arbitrary"), name="moe_ffn",
    )(info, xs, wg, wu, wd)


def _combine_kernel(*refs, has_mod, out_x):
    refs = list(refs)
    ea_ref, eb_ref, ra_ref, rb_ref, cnt_ref, x_ref, ys_ref, wcol_ref, gate_ref, g_ref = refs[:10]
    refs = refs[10:]
    mod_refs = (refs.pop(0), refs.pop(0)) if has_mod else None
    xo_ref = refs.pop(0) if out_x else None
    h_ref, off_ref, buf_a, buf_b, sems = refs
    i = pl.program_id(0)
    tm = x_ref.shape[0]

    def issue(tile, slot):
        base = tile * tm

        def body(t, _):
            g = base + t
            sa = off_ref[ea_ref[g]] + ra_ref[g]
            sb = off_ref[eb_ref[g]] + rb_ref[g]
            pltpu.make_async_copy(ys_ref.at[pl.ds(sa, 1), :], buf_a.at[slot, pl.ds(t, 1), :],
                                  sems.at[slot]).start(priority=0)
            pltpu.make_async_copy(ys_ref.at[pl.ds(sb, 1), :], buf_b.at[slot, pl.ds(t, 1), :],
                                  sems.at[slot]).start(priority=1)
            return 0
        lax.fori_loop(0, tm, body, 0, unroll=ROW_UNROLL)

    @pl.when(i == 0)
    def _():
        _expert_offsets(cnt_ref, off_ref)
        issue(0, 0)

    @pl.when(i + 1 < pl.num_programs(0))
    def _():
        issue(i + 1, (i + 1) % 2)

    slot = i % 2
    for buf in (buf_a, buf_b):
        pltpu.make_async_copy(ys_ref.at[pl.ds(0, tm), :], buf.at[slot], sems.at[slot]).wait()
    y = wcol_ref[:, 0:1] * buf_a[slot] + wcol_ref[:, 1:2] * buf_b[slot]
    x = x_ref[...] + gate_ref[0] * y
    if out_x:
        xo_ref[...] = x
    h_ref[...] = _norm_mod(x, g_ref, mod_refs).astype(h_ref.dtype)


def _combine(x, ys, meta, counts, wcol, gate, g, *, mod=None, out_x=False, h_dtype=BF16):
    rows, d = x.shape
    tm = MOE_TOKENS
    row_spec = pl.BlockSpec((tm, d), lambda i, *_: (i, 0))
    args = [x, ys, wcol, gate, g.reshape(1, d)]
    specs = [row_spec, pl.BlockSpec(memory_space=pl.ANY), pl.BlockSpec((tm, LANES), lambda i, *_: (i, 0)),
             _mod_spec(gate.shape[0], rows, tm, d), pl.BlockSpec((1, d), lambda i, *_: (0, 0))]
    if mod is not None:
        for m in mod:
            args.append(m)
            specs.append(_mod_spec(m.shape[0], rows, tm, d))
    out_shape, out_specs = [], []
    if out_x:
        out_shape.append(jax.ShapeDtypeStruct((rows, d), F32))
        out_specs.append(row_spec)
    out_shape.append(jax.ShapeDtypeStruct((rows, d), h_dtype))
    out_specs.append(row_spec)
    grid_spec = pltpu.PrefetchScalarGridSpec(
        num_scalar_prefetch=5, grid=(rows // tm,), in_specs=specs, out_specs=out_specs,
        scratch_shapes=[pltpu.SMEM((N_EXPERTS,), jnp.int32), pltpu.VMEM((2, tm, d), F32), pltpu.VMEM((2, tm, d), F32),
                        pltpu.SemaphoreType.DMA((2,))])
    outs = pl.pallas_call(
        functools.partial(_combine_kernel, has_mod=mod is not None, out_x=out_x), grid_spec=grid_spec,
        out_shape=out_shape, compiler_params=_cparams("arbitrary"), name="moe_combine",
    )(meta[0], meta[1], meta[2], meta[3], counts, *args)
    return outs if out_x else outs[0]


def _softmax_av(scores, values, sink_col):
    m = sink_col
    for s in scores:
        m = jnp.maximum(m, jnp.max(s, axis=-1, keepdims=True))
    den = jnp.exp(sink_col - m)
    acc = None
    for s, v in zip(scores, values):
        p = jnp.exp(s - m)
        den = den + jnp.sum(p, axis=-1, keepdims=True)
        pv = _dot(p.astype(BF16), v)
        acc = pv if acc is None else acc + pv
    return acc / den


def _sink_column(sink_ref, kv, rows):
    return jnp.concatenate([jnp.full((rows, 1), sink_ref[kv * ATT_GROUP + g], F32) for g in range(ATT_GROUP)], axis=0)


def _attn_ctx_kernel(sink_ref, qkv_ref, o_ref):
    t = qkv_ref.shape[0]
    qw = ATT_HEADS * HEAD_DIM
    kw = ATT_KV * HEAD_DIM
    heads_out = []
    for kv in range(ATT_KV):
        q = jnp.concatenate(
            [qkv_ref[:, (kv * ATT_GROUP + g) * HEAD_DIM:(kv * ATT_GROUP + g + 1) * HEAD_DIM] for g in range(ATT_GROUP)],
            axis=0).astype(BF16)
        k = qkv_ref[:, qw + kv * HEAD_DIM:qw + (kv + 1) * HEAD_DIM].astype(BF16)
        v = qkv_ref[:, qw + kw + kv * HEAD_DIM:qw + kw + (kv + 1) * HEAD_DIM].astype(BF16)
        s = _dot_nt(q, k) * HEAD_DIM ** -0.5
        o = _softmax_av([s], [v], _sink_column(sink_ref, kv, t))
        heads_out += [o[g * t:(g + 1) * t] for g in range(ATT_GROUP)]
    o_ref[...] = jnp.concatenate(heads_out, axis=1).astype(o_ref.dtype)


def _attn_ctx(qkv, sink, n_seq, seq_len):
    rows, cols = qkv.shape
    return pl.pallas_call(
        _attn_ctx_kernel, grid=(n_seq,),
        in_specs=[pl.BlockSpec(memory_space=pltpu.SMEM), pl.BlockSpec((seq_len, cols), lambda b: (b, 0))],
        out_specs=pl.BlockSpec((seq_len, ATT_HEADS * HEAD_DIM), lambda b: (b, 0)),
        out_shape=jax.ShapeDtypeStruct((rows, ATT_HEADS * HEAD_DIM), BF16),
        compiler_params=_cparams("parallel"), name="attn_context",
    )(sink, qkv)


def _rope_block(x, cos, sin_signed):
    lane = lax.broadcasted_iota(jnp.int32, x.shape, 1)
    nf = HEAD_DIM // 4
    partner = jnp.where((lane % (2 * nf)) < nf, pltpu.roll(x, LANES - nf, axis=1), pltpu.roll(x, nf, axis=1))
    return x * cos + partner * sin_signed


def _attn_lat_kernel(sink_ref, qkv_ref, ck_ref, cv_ref, cos_ref, sin_ref, o_ref, k_scr):
    i = pl.program_id(1)
    t = qkv_ref.shape[0]
    qw = ATT_HEADS * HEAD_DIM
    kw = ATT_KV * HEAD_DIM
    span = Q_BLOCK + 2 * WINDOW

    @pl.when(i == 0)
    def _():
        for c in range(kw // LANES):
            blk = qkv_ref[:, qw + c * LANES:qw + (c + 1) * LANES]
            k_scr[:, c * LANES:(c + 1) * LANES] = _rope_block(blk, cos_ref[...], sin_ref[...]).astype(BF16)

    r0 = pl.multiple_of(i * Q_BLOCK, Q_BLOCK)
    ws = pl.multiple_of(jnp.clip(r0 - WINDOW, 0, t - span), Q_BLOCK)
    cos_q = cos_ref[pl.ds(r0, Q_BLOCK), :]
    sin_q = sin_ref[pl.ds(r0, Q_BLOCK), :]
    qpos = r0 + lax.broadcasted_iota(jnp.int32, (Q_BLOCK, span), 0)
    kpos = ws + lax.broadcasted_iota(jnp.int32, (Q_BLOCK, span), 1)
    band = jnp.abs(qpos - kpos) <= WINDOW
    band = jnp.concatenate([band] * ATT_GROUP, axis=0)
    heads_out = []
    for kv in range(ATT_KV):
        heads = []
        for g in range(ATT_GROUP):
            h = kv * ATT_GROUP + g
            c, half = divmod(h * HEAD_DIM, LANES)
            blk = _rope_block(qkv_ref[pl.ds(r0, Q_BLOCK), c * LANES:(c + 1) * LANES], cos_q, sin_q)
            heads.append(blk[:, half:half + HEAD_DIM])
        q = jnp.concatenate(heads, axis=0).astype(BF16)
        ck = ck_ref[0, :, kv * HEAD_DIM:(kv + 1) * HEAD_DIM].astype(BF16)
        cv = cv_ref[0, :, kv * HEAD_DIM:(kv + 1) * HEAD_DIM].astype(BF16)
        kwin = k_scr[pl.ds(ws, span), kv * HEAD_DIM:(kv + 1) * HEAD_DIM]
        vwin = qkv_ref[pl.ds(ws, span), qw + kw + kv * HEAD_DIM:qw + kw + (kv + 1) * HEAD_DIM].astype(BF16)
        s_ctx = _dot_nt(q, ck) * HEAD_DIM ** -0.5
        s_win = jnp.where(band, _dot_nt(q, kwin) * HEAD_DIM ** -0.5, -jnp.inf)
        o = _softmax_av([s_ctx, s_win], [cv, vwin], _sink_column(sink_ref, kv, Q_BLOCK))
        heads_out += [o[g * Q_BLOCK:(g + 1) * Q_BLOCK] for g in range(ATT_GROUP)]
    o_ref[...] = jnp.concatenate(heads_out, axis=1).astype(o_ref.dtype)


def _rope_tables(seq_len):
    pos = jnp.arange(seq_len, dtype=jnp.int32)
    row = (pos // GRID_W).astype(F32)
    col = (pos % GRID_W).astype(F32)
    nf = HEAD_DIM // 4
    inv = ROPE_BASE ** (-jnp.arange(nf, dtype=F32) / nf)
    ang_r = row[:, None] * inv[None, :]
    ang_c = col[:, None] * inv[None, :]
    cos_h = jnp.concatenate([jnp.cos(ang_r), jnp.cos(ang_r), jnp.cos(ang_c), jnp.cos(ang_c)], axis=1)
    sin_h = jnp.concatenate([-jnp.sin(ang_r), jnp.sin(ang_r), -jnp.sin(ang_c), jnp.sin(ang_c)], axis=1)
    reps = LANES // HEAD_DIM
    return jnp.tile(cos_h, (1, reps)), jnp.tile(sin_h, (1, reps))


def _attn_lat(qkv, cache_k, cache_v, sink, n_seq, seq_len):
    rows, cols = qkv.shape
    past = cache_k.shape[1]
    kw = ATT_KV * HEAD_DIM
    cos, sin = _rope_tables(seq_len)
    return pl.pallas_call(
        _attn_lat_kernel, grid=(n_seq, seq_len // Q_BLOCK),
        in_specs=[pl.BlockSpec(memory_space=pltpu.SMEM),
                  pl.BlockSpec((seq_len, cols), lambda b, i: (b, 0)),
                  pl.BlockSpec((1, past, kw), lambda b, i: (b, 0, 0)),
                  pl.BlockSpec((1, past, kw), lambda b, i: (b, 0, 0)),
                  pl.BlockSpec((seq_len, LANES), lambda b, i: (0, 0)),
                  pl.BlockSpec((seq_len, LANES), lambda b, i: (0, 0))],
        out_specs=pl.BlockSpec((Q_BLOCK, ATT_HEADS * HEAD_DIM), lambda b, i: (b * (seq_len // Q_BLOCK) + i, 0)),
        out_shape=jax.ShapeDtypeStruct((rows, ATT_HEADS * HEAD_DIM), BF16),
        scratch_shapes=[pltpu.VMEM((seq_len, kw), BF16)],
        compiler_params=_cparams("parallel", "arbitrary"), name="attn_latent",
    )(sink, qkv, cache_k.reshape(n_seq, past, kw), cache_v.reshape(n_seq, past, kw), cos, sin)


def _ml_gates_kernel(h_ref, w_ref, wt_ref, b_ref, bt_ref, g_ref, gt_ref):
    h = h_ref[...]
    g_ref[...] = jnp.dot(h, w_ref[...], precision=HI, preferred_element_type=F32) + b_ref[...]
    gt_ref[...] = _dot_nt(wt_ref[...], h, precision=HI) + bt_ref[...]


def _ml_gates(h, w_gates, b_gates, *, tm=512):
    rows, d = h.shape
    ng = w_gates.shape[1]
    w_pad = jnp.pad(w_gates, ((0, 0), (0, LANES - ng)))
    b_pad = jnp.pad(b_gates, (0, LANES - ng)).reshape(1, LANES)
    return pl.pallas_call(
        _ml_gates_kernel, grid=(rows // tm,),
        in_specs=[pl.BlockSpec((tm, d), lambda i: (i, 0)), pl.BlockSpec((d, LANES), lambda i: (0, 0)),
                  pl.BlockSpec((ng, d), lambda i: (0, 0)), pl.BlockSpec((1, LANES), lambda i: (0, 0)),
                  pl.BlockSpec((ng, 1), lambda i: (0, 0))],
        out_specs=[pl.BlockSpec((tm, LANES), lambda i: (i, 0)), pl.BlockSpec((ng, tm), lambda i: (0, i))],
        out_shape=[jax.ShapeDtypeStruct((rows, LANES), F32), jax.ShapeDtypeStruct((ng, rows), F32)],
        compiler_params=_cparams("parallel"), name="mlstm_gates",
    )(h, w_pad, w_gates.T, b_pad, b_gates.reshape(ng, 1))


def _ml_conv_kernel(p_ref, w_ref, o_ref, *, k_scale):
    j = pl.program_id(1)
    x = p_ref[...]
    t = x.shape[0]
    row = lax.broadcasted_iota(jnp.int32, x.shape, 0)
    prev = jnp.where(row == 0, 0.0, pltpu.roll(x, 1, axis=0))
    nxt = jnp.where(row == t - 1, 0.0, pltpu.roll(x, t - 1, axis=0))
    y = prev * w_ref[0:1, :] + x * w_ref[1:2, :] + nxt * w_ref[2:3, :]
    scale = jnp.where(j >= pl.num_programs(1) // 2, k_scale, 1.0).astype(F32)
    o_ref[...] = (_silu(y) * scale).astype(o_ref.dtype)


def _ml_conv(p, conv_w, n_seq, seq_len, *, tn=512):
    rows = p.shape[0]
    width = conv_w.shape[1]
    return pl.pallas_call(
        functools.partial(_ml_conv_kernel, k_scale=ML_DK ** -0.5), grid=(n_seq, width // tn),
        in_specs=[pl.BlockSpec((seq_len, tn), lambda b, j: (b, j)), pl.BlockSpec((3, tn), lambda b, j: (0, j))],
        out_specs=pl.BlockSpec((seq_len, tn), lambda b, j: (b, j)),
        out_shape=jax.ShapeDtypeStruct((rows, width), BF16),
        compiler_params=_cparams("parallel", "parallel"), name="mlstm_conv",
    )(p, conv_w)


def _ml_scan_kernel(*refs, zero_init):
    refs = list(refs)
    dirs = [tuple(refs[0:5]), tuple(refs[5:10])]
    refs = refs[10:]
    if not zero_init:
        c0_ref, n0_ref, m0_ref = refs[:3]
        refs = refs[3:]
    hf_ref, hb_ref, c_ref, n_ref, m_ref = refs
    h_out = (hf_ref, hb_ref)
    c = pl.program_id(1)
    last = pl.num_programs(1) - 1

    @pl.when(c == 0)
    def _():
        if zero_init:
            c_ref[...] = jnp.zeros_like(c_ref)
            n_ref[...] = jnp.zeros_like(n_ref)
            m_ref[...] = jnp.zeros_like(m_ref)
        else:
            c_ref[...] = c0_ref[...]
            n_ref[...] = n0_ref[...]
            m_ref[...] = m0_ref[...]

    length = hf_ref.shape[0]
    ti = lax.broadcasted_iota(jnp.int32, (length, length), 0)
    si = lax.broadcasted_iota(jnp.int32, (length, length), 1)
    for d in range(2):
        q_ref, k_ref, v_ref, g_ref, gt_ref = dirs[d]
        causal = (ti >= si) if d == 0 else (ti <= si)
        tri = jnp.where(causal, 1.0, 0.0).astype(F32)
        f_col = _log_sigmoid(g_ref[...])
        f_row = _log_sigmoid(gt_ref[...])
        b_col = jnp.dot(tri, f_col, precision=HI, preferred_element_type=F32)
        b_row = _dot_nt(f_row, tri, precision=HI)
        edge = length - 1 if d == 0 else 0
        for h in range(ML_HEADS):
            ji = d * 2 * ML_HEADS + h
            jf = ji + ML_HEADS
            bc = b_col[:, jf:jf + 1]
            br = b_row[jf:jf + 1, :]
            i_row = gt_ref[ji:ji + 1, :]
            i_col = g_ref[:, ji:ji + 1]
            m_prev = m_ref[0, d, h][:, 0:1]
            q = q_ref[:, h * ML_DK:(h + 1) * ML_DK]
            k = k_ref[:, h * ML_DK:(h + 1) * ML_DK]
            v = v_ref[:, h * ML_DV:(h + 1) * ML_DV].astype(BF16)
            cst = c_ref[0, d, h]
            nst = n_ref[0, d, h]
            a_row = i_row - br
            amat = jnp.where(causal, a_row, -jnp.inf)
            u = jnp.maximum(m_prev, jnp.max(amat, axis=1, keepdims=True))
            qk = (_dot_nt(q, k) * jnp.exp(amat - u)).astype(BF16)
            sc = jnp.exp(m_prev - u)
            state_ext = jnp.concatenate([cst, jnp.broadcast_to(nst, (LANES, ML_DK))], axis=0).astype(BF16)
            v_ext = jnp.concatenate([v, jnp.ones((length, LANES), BF16)], axis=1)
            tot = sc * _dot_nt(q, state_ext) + _dot(qk, v_ext)
            inv = 1.0 / jnp.maximum(jnp.abs(tot[:, ML_DV:]), jnp.exp(-(bc + u)))
            h_out[d][:, h * ML_DV:(h + 1) * ML_DV] = tot[:, :ML_DV] * jnp.concatenate([inv] * (ML_DV // LANES), axis=1)
            b_last = br[:, edge:edge + 1]
            wlog_row = b_last + a_row
            m_new = jnp.maximum(b_last + m_prev, jnp.max(wlog_row, axis=1, keepdims=True))
            decay = jnp.exp(b_last + m_prev - m_new)
            ws_row = jnp.exp(wlog_row - m_new)
            ws_col = jnp.exp(b_last - bc + i_col - m_new)
            kw = (ws_col * k.astype(F32)).astype(BF16)
            c_ref[0, d, h] = decay * cst + _dot_tn(v, kw)
            n_ref[0, d, h] = decay * nst + _dot(jnp.broadcast_to(ws_row, (8, length)).astype(BF16), k)[0:1]
            m_ref[0, d, h] = jnp.broadcast_to(m_new, (1, ML_DK))


def _ml_scan(qk, p, g, gt, state, n_seq, seq_len):
    rows = qk.shape[0]
    length = min(ML_CHUNK, seq_len)
    nc = seq_len // length
    qw = ML_HEADS * ML_DK
    vw = ML_HEADS * ML_DV
    ng = gt.shape[0]

    def fwd(b, c):
        return b * nc + c

    def bwd(b, c):
        return b * nc + nc - 1 - c

    args, specs = [], []
    for pos in (fwd, bwd):
        args += [qk, qk, p, g, gt]
        specs += [pl.BlockSpec((length, qw), lambda b, c, pos=pos: (pos(b, c), 0)),
                  pl.BlockSpec((length, qw), lambda b, c, pos=pos: (pos(b, c), 1)),
                  pl.BlockSpec((length, vw), lambda b, c, pos=pos: (pos(b, c), 2 * qw // vw)),
                  pl.BlockSpec((length, LANES), lambda b, c, pos=pos: (pos(b, c), 0)),
                  pl.BlockSpec((ng, length), lambda b, c, pos=pos: (0, pos(b, c)))]
    c_spec = pl.BlockSpec((1, 2, ML_HEADS, ML_DV, ML_DK), lambda b, c: (b, 0, 0, 0, 0))
    n_spec = pl.BlockSpec((1, 2, ML_HEADS, 1, ML_DK), lambda b, c: (b, 0, 0, 0, 0))
    zero_init = state is None
    if not zero_init:
        c0, n0, m0 = state
        args += [c0, n0.reshape(n_seq, 2, ML_HEADS, 1, ML_DK),
                 jnp.broadcast_to(m0[..., None, None], (n_seq, 2, ML_HEADS, 1, ML_DK))]
        specs += [c_spec, n_spec, n_spec]
    hf, hb, c_fin, n_fin, m_fin = pl.pallas_call(
        functools.partial(_ml_scan_kernel, zero_init=zero_init), grid=(n_seq, nc), in_specs=specs,
        out_specs=[pl.BlockSpec((length, vw), lambda b, c: (fwd(b, c), 0)),
                   pl.BlockSpec((length, vw), lambda b, c: (bwd(b, c), 0)), c_spec, n_spec, n_spec],
        out_shape=[jax.ShapeDtypeStruct((rows, vw), F32), jax.ShapeDtypeStruct((rows, vw), F32),
                   jax.ShapeDtypeStruct((n_seq, 2, ML_HEADS, ML_DV, ML_DK), F32),
                   jax.ShapeDtypeStruct((n_seq, 2, ML_HEADS, 1, ML_DK), F32),
                   jax.ShapeDtypeStruct((n_seq, 2, ML_HEADS, 1, ML_DK), F32)],
        compiler_params=_cparams("parallel", "arbitrary"), name="mlstm_scan",
    )(*args)
    return hf, hb, (c_fin, n_fin[:, :, :, 0, :], m_fin[:, :, :, 0, 0])


def _gla_scan_kernel(*refs, zero_init):
    refs = list(refs)
    dirs = [tuple(refs[0:4]), tuple(refs[4:8])]
    w2_ref, ba_ref = refs[8:10]
    refs = refs[10:]
    if not zero_init:
        s0_ref = refs.pop(0)
    of_ref, ob_ref, s_ref, st_scr, la_scr = refs
    o_out = (of_ref, ob_ref)
    c = pl.program_id(1)
    last = pl.num_programs(1) - 1
    kw = GLA_HEADS * GLA_DK
    n_sub = of_ref.shape[0] // GLA_SUB

    @pl.when(c == 0)
    def _():
        for d in range(2):
            for h in range(GLA_HEADS):
                st_scr[d, h] = jnp.zeros((GLA_DV, GLA_DK), F32) if zero_init else s0_ref[0, d, h].T

    for d in range(2):
        u = dirs[d][3][...].astype(BF16)
        z = _dot(u, w2_ref[:, d * kw:(d + 1) * kw]) + ba_ref[:, d * kw:(d + 1) * kw]
        la_scr[d] = _log_sigmoid(z) / GLA_TAU

    ti = lax.broadcasted_iota(jnp.int32, (GLA_SUB, GLA_SUB), 0)
    si = lax.broadcasted_iota(jnp.int32, (GLA_SUB, GLA_SUB), 1)
    s_lane = lax.broadcasted_iota(jnp.int32, (GLA_SUB, GLA_SUB), 1)

    def sub_chunk(j, carry):
        for d in range(2):
            q_ref, k_ref, v_ref, _ = dirs[d]
            r0 = pl.multiple_of((j if d == 0 else n_sub - 1 - j) * GLA_SUB, GLA_SUB)
            causal = (ti >= si) if d == 0 else (ti <= si)
            tri = jnp.where(causal, 1.0, 0.0).astype(F32)
            bc_all = jnp.dot(tri, la_scr[d, pl.ds(r0, GLA_SUB), :], precision=HI, preferred_element_type=F32)
            edge = GLA_SUB - 1 if d == 0 else 0
            for h in range(GLA_HEADS):
                bc = bc_all[:, h * GLA_DK:(h + 1) * GLA_DK]
                q = q_ref[pl.ds(r0, GLA_SUB), h * GLA_DK:(h + 1) * GLA_DK] * GLA_DK ** -0.5
                k = k_ref[pl.ds(r0, GLA_SUB), h * GLA_DK:(h + 1) * GLA_DK]
                v = v_ref[pl.ds(r0, GLA_SUB), h * GLA_DV:(h + 1) * GLA_DV].astype(BF16)
                bc2 = bc * LOG2E
                a = jnp.zeros((GLA_SUB, GLA_SUB), F32)
                for s in range(GLA_SUB):
                    decay = jnp.exp2(bc2 - bc2[s:s + 1, :])
                    col = jnp.sum(q * (k[s:s + 1, :] * decay), axis=1, keepdims=True)
                    a = jnp.where(s_lane == s, col, a)
                a = jnp.where(causal, a, 0.0)
                st = st_scr[d, h]
                o = _dot(a.astype(BF16), v) + _dot_nt((q * jnp.exp2(bc2)).astype(BF16), st.astype(BF16))
                o_out[d][pl.ds(r0, GLA_SUB), h * GLA_DV:(h + 1) * GLA_DV] = o
                b_last = bc2[edge:edge + 1, :]
                k_dec = (k * jnp.exp2(b_last - bc2)).astype(BF16)
                st_scr[d, h] = jnp.exp2(b_last) * st + _dot_tn(v, k_dec)
        return carry

    lax.fori_loop(0, n_sub, sub_chunk, 0)

    @pl.when(c == last)
    def _():
        for d in range(2):
            for h in range(GLA_HEADS):
                s_ref[0, d, h] = st_scr[d, h].T


def _gla_scan(p, u, w2, b_a, state, n_seq, seq_len):
    rows = p.shape[0]
    length = min(GLA_BLOCK, seq_len)
    nc = seq_len // length
    kw = GLA_HEADS * GLA_DK
    vw = GLA_HEADS * GLA_DV

    def fwd(b, c):
        return b * nc + c

    def bwd(b, c):
        return b * nc + nc - 1 - c

    args, specs = [], []
    for pos in (fwd, bwd):
        args += [p, p, p, u]
        specs += [pl.BlockSpec((length, kw), lambda b, c, pos=pos: (pos(b, c), 0)),
                  pl.BlockSpec((length, kw), lambda b, c, pos=pos: (pos(b, c), 1)),
                  pl.BlockSpec((length, vw), lambda b, c, pos=pos: (pos(b, c), 2 * kw // vw)),
                  pl.BlockSpec((length, LANES), lambda b, c, pos=pos: (pos(b, c), 0))]
    args += [w2, b_a]
    specs += [pl.BlockSpec(w2.shape, lambda b, c: (0, 0)), pl.BlockSpec(b_a.shape, lambda b, c: (0, 0))]
    s_spec = pl.BlockSpec((1, 2, GLA_HEADS, GLA_DK, GLA_DV), lambda b, c: (b, 0, 0, 0, 0))
    zero_init = state is None
    if not zero_init:
        args.append(state)
        specs.append(s_spec)
    return pl.pallas_call(
        functools.partial(_gla_scan_kernel, zero_init=zero_init), grid=(n_seq, nc), in_specs=specs,
        out_specs=[pl.BlockSpec((length, vw), lambda b, c: (fwd(b, c), 0)),
                   pl.BlockSpec((length, vw), lambda b, c: (bwd(b, c), 0)), s_spec],
        out_shape=[jax.ShapeDtypeStruct((rows, vw), F32), jax.ShapeDtypeStruct((rows, vw), F32),
                   jax.ShapeDtypeStruct((n_seq, 2, GLA_HEADS, GLA_DK, GLA_DV), F32)],
        scratch_shapes=[pltpu.VMEM((2, GLA_HEADS, GLA_DV, GLA_DK), F32), pltpu.VMEM((2, length, kw), F32)],
        compiler_params=_cparams("parallel", "arbitrary"), name="gla_scan",
    )(*args)


def kernel(x_prompt, x_sample, cache_k_0, cache_v_0, state_mlstm_C_1, state_mlstm_n_1, state_mlstm_m_1, state_gla_S_2, cache_k_3, cache_v_3, c, c_ctx, w_mod, b_mod, norm1_g, norm2_g, final_g, router_w, router_b, moe_wg, moe_wu, moe_wd, attn0_w_qkv, attn0_sink, attn0_w_o, mlstm1_w_in, mlstm1_b_gates, mlstm1_conv, mlstm1_norm_g, mlstm1_w_out, gla2_w_in, gla2_w_a1, gla2_w_a2, gla2_b_a, gla2_norm_g, gla2_w_out, attn3_w_qkv, attn3_sink, attn3_w_o):
    n_ctx, ctx_len, d = x_prompt.shape
    n_lat, lat_len, _ = x_sample.shape
    depth = w_mod.shape[0]

    cvec = jnp.concatenate([c_ctx[None, :], c, jnp.zeros((8 - 1 - n_lat, d), F32)], axis=0)
    mod = _modulation(cvec, w_mod, b_mod).reshape(depth, 8, 6, 1, d)

    def mods(layer, kind, latent):
        return mod[layer, 1:1 + n_lat, kind] if latent else mod[layer, 0:1, kind]

    rw_hi = router_w.T.astype(BF16)
    rw_lo = (router_w.T - rw_hi.astype(F32)).astype(BF16)
    rwt = jnp.concatenate([rw_hi, rw_lo], axis=0)
    rb = router_b.reshape(-1, 1)
    attn_w = {0: (attn0_w_qkv.astype(BF16), attn0_sink, attn0_w_o.astype(BF16), cache_k_0, cache_v_0),
              3: (attn3_w_qkv.astype(BF16), attn3_sink, attn3_w_o.astype(BF16), cache_k_3, cache_v_3)}
    ml_qw = ML_HEADS * ML_DK
    ml_vw = ML_HEADS * ML_DV
    ml_main = 2 * ml_qw + 2 * ml_vw
    ml_w_main = mlstm1_w_in[:, :ml_main].astype(BF16)
    ml_w_gates = mlstm1_w_in[:, ml_main:]
    ml_w_out = mlstm1_w_out.astype(BF16)
    gla_kw = GLA_HEADS * GLA_DK
    gla_w_in = gla2_w_in.astype(BF16)
    gla_w_a1 = jnp.pad(jnp.concatenate([gla2_w_a1[0], gla2_w_a1[1]], axis=1),
                       ((0, 0), (0, LANES - 2 * GLA_RANK))).astype(BF16)
    gla_w2 = jnp.zeros((LANES, 2 * gla_kw), F32)
    gla_w2 = gla_w2.at[:GLA_RANK, :gla_kw].set(gla2_w_a2[0]).at[GLA_RANK:2 * GLA_RANK, gla_kw:].set(gla2_w_a2[1])
    gla_w2 = gla_w2.astype(BF16)
    gla_ba = gla2_b_a.reshape(1, 2 * gla_kw)
    gla_w_out = gla2_w_out.astype(BF16)

    def run_stream(x, n_seq, seq_len, latent):
        states = []
        h = _rownorm(x, norm1_g[0], mods(0, 1, latent), mods(0, 0, latent))
        for layer in range(depth):
            tail = (mods(layer, 2, latent), norm2_g[layer], mods(layer, 4, latent), mods(layer, 3, latent), rwt, rb)
            kind = layer % 3
            if kind == 0:
                w_qkv, sink, w_o, ck, cv = attn_w[layer]
                qkv = _matmul(h, w_qkv)
                if latent:
                    att = _attn_lat(qkv, ck, cv, sink, n_seq, seq_len)
                else:
                    att = _attn_ctx(qkv, sink, n_seq, seq_len)
                    qw = ATT_HEADS * HEAD_DIM
                    kw = ATT_KV * HEAD_DIM
                    states.append(qkv[:, qw:qw + kw].reshape(n_seq, seq_len, ATT_KV, HEAD_DIM))
                    states.append(qkv[:, qw + kw:].reshape(n_seq, seq_len, ATT_KV, HEAD_DIM))
                x, h2, meta, wcol, counts = _proj("plain", (att,), w_o, x, *tail)
            elif kind == 1:
                p = _matmul(h, ml_w_main)
                g, gt = _ml_gates(h, ml_w_gates, mlstm1_b_gates)
                qk = _ml_conv(p, mlstm1_conv, n_seq, seq_len)
                st = (state_mlstm_C_1, state_mlstm_n_1, state_mlstm_m_1) if latent else None
                hf, hb, fin = _ml_scan(qk, p, g, gt, st, n_seq, seq_len)
                if not latent:
                    states.extend(fin)
                x, h2, meta, wcol, counts = _proj("mlstm", (hf, hb, p, (2 * ml_qw + ml_vw) // ml_vw, mlstm1_norm_g), ml_w_out,
                                    x, *tail)
            else:
                p = _matmul(h, gla_w_in)
                u = _matmul(h, gla_w_a1)
                of, ob, s_fin = _gla_scan(p, u, gla_w2, gla_ba, state_gla_S_2 if latent else None, n_seq, seq_len)
                if not latent:
                    states.append(s_fin)
                gla_vw = GLA_HEADS * GLA_DV
                x, h2, meta, wcol, counts = _proj("gla", (of, ob, p, (2 * gla_kw + gla_vw) // gla_vw, gla2_norm_g), gla_w_out,
                                    x, *tail)
            cnt = counts[:, 0]
            xs, info = _dispatch(h2, meta, cnt)
            ys = _ffn(xs, info, moe_wg, moe_wu, moe_wd, layer)
            gate2 = mods(layer, 5, latent)
            if layer + 1 < depth:
                x, h = _combine(x, ys, meta, cnt, wcol, gate2, norm1_g[layer + 1],
                                mod=(mods(layer + 1, 1, latent), mods(layer + 1, 0, latent)), out_x=True,
                                h_dtype=F32 if (layer + 1) % 3 == 1 else BF16)
            else:
                out = _combine(x, ys, meta, cnt, wcol, gate2, final_g, h_dtype=F32)
        return out.reshape(n_seq, seq_len, d), states

    y_prompt, new_state = run_stream(x_prompt.reshape(n_ctx * ctx_len, d), n_ctx, ctx_len, False)
    y_sample, _ = run_stream(x_sample.reshape(n_lat * lat_len, d), n_lat, lat_len, True)
    return (y_prompt, y_sample, *new_state)
```

```python
import functools

import jax
import jax.numpy as jnp
from jax import lax
from jax.experimental import pallas as pl
from jax.experimental.pallas import tpu as pltpu

F32 = jnp.float32
BF16 = jnp.bfloat16
HI = lax.Precision.HIGHEST

EPS = 1e-6
LOG2E = 1.4426950408889634
GRID_W = 64
ATT_HEADS = 16
ATT_KV = 4
ATT_GROUP = ATT_HEADS // ATT_KV
HEAD_DIM = 64
WINDOW = 128
Q_BLOCK = 128
ROPE_BASE = 10000.0
ML_HEADS = 8
ML_DK = 128
ML_DV = 256
ML_CHUNK = 128
GLA_HEADS = 4
GLA_DK = 128
GLA_DV = 256
GLA_RANK = 16
GLA_TAU = 16.0
GLA_SUB = 16
GLA_BLOCK = 256
N_EXPERTS = 16
N_GROUPS = 4
GROUP_SIZE = N_EXPERTS // N_GROUPS
LANES = 128
VMEM_LIMIT = 56 * 1024 * 1024


def _cparams(*sem):
    return pltpu.CompilerParams(dimension_semantics=sem, vmem_limit_bytes=VMEM_LIMIT)


def _dot(a, b):
    return jnp.dot(a, b, preferred_element_type=F32)


def _dot_nt(a, b, precision=None):
    return lax.dot_general(a, b, (((1,), (1,)), ((), ())), precision=precision, preferred_element_type=F32)


def _dot_tn(a, b):
    return lax.dot_general(a, b, (((0,), (0,)), ((), ())), preferred_element_type=F32)


def _sigmoid(x):
    return 1.0 / (1.0 + jnp.exp(-x))


def _silu(x):
    return x * _sigmoid(x)


def _log_sigmoid(x):
    return jnp.minimum(x, 0.0) - jnp.log(1.0 + jnp.exp(-jnp.abs(x)))


def _rms_rows(x, g):
    ms = jnp.mean(x * x, axis=-1, keepdims=True)
    return x * lax.rsqrt(ms + EPS) * g


def _mod_kernel(c_ref, w_ref, b_ref, o_ref):
    s = _silu(c_ref[...])
    o_ref[0] = _dot(s.astype(BF16), w_ref[0].astype(BF16)) + b_ref[0]


def _modulation(cvec, w_mod, b_mod):
    depth, d, n6 = w_mod.shape
    tn = 1536
    return pl.pallas_call(
        _mod_kernel,
        grid=(depth, n6 // tn),
        in_specs=[pl.BlockSpec((8, d), lambda l, j: (0, 0)),
                  pl.BlockSpec((1, d, tn), lambda l, j: (l, 0, j)),
                  pl.BlockSpec((1, 1, tn), lambda l, j: (l, 0, j))],
        out_specs=pl.BlockSpec((1, 8, tn), lambda l, j: (l, 0, j)),
        out_shape=jax.ShapeDtypeStruct((depth, 8, n6), F32),
        compiler_params=_cparams("parallel", "parallel"),
        name="adaln_modulation",
    )(cvec, w_mod, b_mod.reshape(depth, 1, n6))


def _route(h, rwt, rb, carry):
    tm = h.shape[0]
    h_hi = h.astype(BF16)
    h_lo = (h - h_hi.astype(F32)).astype(BF16)
    by_hi = _dot_nt(rwt, h_hi)
    logits = by_hi[:N_EXPERTS] + by_hi[N_EXPERTS:] + _dot_nt(rwt[:N_EXPERTS], h_lo)
    scores = _sigmoid(logits)
    sel = scores + rb
    expert = lax.broadcasted_iota(jnp.int32, sel.shape, 0)
    pos = expert % GROUP_SIZE
    grp = expert // GROUP_SIZE

    def mate(x, k):
        ahead = pltpu.roll(x, N_EXPERTS - k, axis=0)
        behind = pltpu.roll(x, GROUP_SIZE - k, axis=0)
        return jnp.where(pos + k < GROUP_SIZE, ahead, behind)

    beaten = jnp.zeros_like(sel)
    for k in range(1, GROUP_SIZE):
        other = mate(sel, k)
        other_first = (pos + k) % GROUP_SIZE < pos
        beaten = beaten + jnp.where(other_first, jnp.where(other >= sel, 1.0, 0.0), jnp.where(other > sel, 1.0, 0.0))
    top2 = jnp.where(beaten < 2.0, sel, 0.0)
    gscore = top2
    for k in range(1, GROUP_SIZE):
        gscore = gscore + mate(top2, k)
    lost = jnp.zeros_like(sel)
    for k in range(1, N_GROUPS):
        other = pltpu.roll(gscore, N_EXPERTS - GROUP_SIZE * k, axis=0)
        other_first = (grp + k) % N_GROUPS < grp
        lost = lost + jnp.where(other_first, jnp.where(other >= gscore, 1.0, 0.0),
                                jnp.where(other > gscore, 1.0, 0.0))
    picked = jnp.where(lost < 0.5, jnp.where(beaten < 2.0, 1.0, 0.0), 0.0)
    chosen = picked > 0.5
    weight = jnp.where(chosen, scores, 0.0)
    wsum = jnp.sum(weight, axis=0, keepdims=True)
    e_f = expert.astype(F32)
    e_a = jnp.min(jnp.where(chosen, e_f, float(N_EXPERTS)), axis=0, keepdims=True)
    e_b = jnp.max(jnp.where(chosen, e_f, -1.0), axis=0, keepdims=True)
    before = (lax.broadcasted_iota(jnp.int32, (tm, tm), 0) < lax.broadcasted_iota(jnp.int32, (tm, tm), 1))
    rank = _dot(picked.astype(BF16), jnp.where(before, 1.0, 0.0).astype(BF16)) + carry
    is_a = e_f == e_a
    is_b = e_f == e_b
    r_a = jnp.sum(jnp.where(is_a, rank, 0.0), axis=0, keepdims=True)
    r_b = jnp.sum(jnp.where(is_b, rank, 0.0), axis=0, keepdims=True)
    w_a = jnp.sum(jnp.where(is_a, weight, 0.0), axis=0, keepdims=True)
    w_b = jnp.sum(jnp.where(is_b, weight, 0.0), axis=0, keepdims=True)
    meta = jnp.concatenate([e_a, e_b, r_a, r_b, jnp.zeros((4, tm), F32)], axis=0).astype(jnp.int32)
    wcol = jnp.concatenate([w_a / wsum, w_b / wsum, jnp.zeros((LANES - 2, tm), F32)], axis=0).T
    return meta, wcol, carry + jnp.sum(picked, axis=1, keepdims=True)


def _norm_mod(x, g_ref, mod_refs):
    h = _rms_rows(x, g_ref[...])
    if mod_refs is not None:
        a_ref, s_ref = mod_refs
        h = h * (1.0 + a_ref[0]) + s_ref[0]
    return h


def _rownorm_kernel(x_ref, g_ref, a_ref, s_ref, h_ref):
    h_ref[...] = _norm_mod(x_ref[...], g_ref, (a_ref, s_ref)).astype(h_ref.dtype)


def _mod_spec(n_mod, rows, tm, d, n_prefetch=0):
    per = (rows // n_mod) // tm
    return pl.BlockSpec((1, 1, d), lambda i, *_: (i // per, 0, 0))


def _rownorm(x, g, scale, shift, *, tm=512):
    rows, d = x.shape
    row_spec = pl.BlockSpec((tm, d), lambda i: (i, 0))
    return pl.pallas_call(
        _rownorm_kernel, grid=(rows // tm,),
        in_specs=[row_spec, pl.BlockSpec((1, d), lambda i: (0, 0)), _mod_spec(scale.shape[0], rows, tm, d),
                  _mod_spec(shift.shape[0], rows, tm, d)],
        out_specs=row_spec, out_shape=jax.ShapeDtypeStruct((rows, d), BF16),
        compiler_params=_cparams("parallel"), name="rownorm",
    )(x, g.reshape(1, d), scale, shift)


def _mm_kernel(a_ref, w_ref, o_ref):
    o_ref[...] = _dot(a_ref[...].astype(BF16), w_ref[...]).astype(o_ref.dtype)


def _matmul(a, w, *, out_dtype=F32, tm=1024):
    m, k = a.shape
    n = w.shape[1]
    tn = next(t for t in (1024, 768, 512, LANES) if n % t == 0)
    return pl.pallas_call(
        _mm_kernel, grid=(m // tm, n // tn),
        in_specs=[pl.BlockSpec((tm, k), lambda i, j: (i, 0)), pl.BlockSpec((k, tn), lambda i, j: (0, j))],
        out_specs=pl.BlockSpec((tm, tn), lambda i, j: (i, j)),
        out_shape=jax.ShapeDtypeStruct((m, n), out_dtype),
        compiler_params=_cparams("parallel", "parallel"), name="matmul",
    )(a, w)


def _head_norm(x, g, n_heads, dv):
    outs = []
    for h in range(n_heads):
        xs = x[:, h * dv:(h + 1) * dv]
        ms = jnp.mean(xs * xs, axis=-1, keepdims=True)
        outs.append(xs * lax.rsqrt(ms + EPS) * g[:, h * dv:(h + 1) * dv])
    return jnp.concatenate(outs, axis=1)


def _proj_kernel(*refs, pre):
    refs = list(refs)
    if pre == "plain":
        a = refs.pop(0)[...]
    else:
        f_ref, b_ref, p_ref, hg_ref = refs.pop(0), refs.pop(0), refs.pop(0), refs.pop(0)
        hsum = f_ref[...] + b_ref[...]
        if pre == "mlstm":
            a = _sigmoid(p_ref[...]) * _head_norm(hsum, hg_ref[...], ML_HEADS, ML_DV)
        else:
            a = _head_norm(hsum, hg_ref[...], GLA_HEADS, GLA_DV) * _silu(p_ref[...])
        a = a.astype(BF16)
    w_ref, x_ref, gate_ref, g_ref, a_ref, s_ref, rwt_ref, rb_ref = refs[:8]
    xo_ref, h_ref, meta_ref, wcol_ref, count_ref, carry_ref = refs[8:]

    @pl.when(pl.program_id(0) == 0)
    def _():
        carry_ref[...] = jnp.zeros_like(carry_ref)

    x = x_ref[...] + gate_ref[0] * _dot(a, w_ref[...])
    xo_ref[...] = x
    h = _norm_mod(x, g_ref, (a_ref, s_ref))
    h_ref[...] = h
    meta, wcol, carry = _route(h, rwt_ref[...], rb_ref[...], carry_ref[:, 0:1])
    meta_ref[...] = meta
    wcol_ref[...] = wcol
    carry_ref[...] = jnp.broadcast_to(carry, carry_ref.shape)
    count_ref[...] = jnp.broadcast_to(carry, count_ref.shape).astype(jnp.int32)


def _proj(pre, pre_args, w_out, x, gate, g, scale, shift, rwt, rb, *, tm=256):
    rows, d = x.shape
    k = w_out.shape[0]
    row_spec = pl.BlockSpec((tm, d), lambda i: (i, 0))
    if pre == "plain":
        args, specs = [pre_args[0]], [pl.BlockSpec((tm, k), lambda i: (i, 0))]
    else:
        hf, hb, p, col_block, hg = pre_args
        wide = pl.BlockSpec((tm, k), lambda i: (i, 0))
        args = [hf, hb, p, hg.reshape(1, k)]
        specs = [wide, wide, pl.BlockSpec((tm, k), lambda i: (i, col_block)), pl.BlockSpec((1, k), lambda i: (0, 0))]
    args += [w_out, x, gate, g.reshape(1, d), scale, shift, rwt, rb]
    specs += [pl.BlockSpec((k, d), lambda i: (0, 0)), row_spec, _mod_spec(gate.shape[0], rows, tm, d),
              pl.BlockSpec((1, d), lambda i: (0, 0)), _mod_spec(scale.shape[0], rows, tm, d),
              _mod_spec(shift.shape[0], rows, tm, d), pl.BlockSpec(rwt.shape, lambda i: (0, 0)),
              pl.BlockSpec(rb.shape, lambda i: (0, 0))]
    return pl.pallas_call(
        functools.partial(_proj_kernel, pre=pre), grid=(rows // tm,), in_specs=specs,
        out_specs=[row_spec, row_spec, pl.BlockSpec((8, tm), lambda i: (0, i)),
                   pl.BlockSpec((tm, LANES), lambda i: (i, 0)), pl.BlockSpec((N_EXPERTS, LANES), lambda i: (0, 0))],
        out_shape=[jax.ShapeDtypeStruct((rows, d), F32), jax.ShapeDtypeStruct((rows, d), F32),
                   jax.ShapeDtypeStruct((8, rows), jnp.int32), jax.ShapeDtypeStruct((rows, LANES), F32),
                   jax.ShapeDtypeStruct((N_EXPERTS, LANES), jnp.int32)],
        scratch_shapes=[pltpu.VMEM((N_EXPERTS, LANES), F32)],
        compiler_params=_cparams("arbitrary"), name="proj_" + pre,
    )(*args)


MOE_TILE = 256
MOE_TILE_SHIFT = 8
MOE_HALF = MOE_TILE // 2
MOE_TOKENS = 256
ROW_UNROLL = 8


def _slot_tiles(rows):
    return (2 * rows) // MOE_TILE + N_EXPERTS


def _expert_offsets(cnt_ref, off_ref):
    def per_expert(e, k):
        off_ref[e] = k * MOE_TILE
        return k + ((cnt_ref[e] + MOE_TILE - 1) >> MOE_TILE_SHIFT)
    return lax.fori_loop(0, N_EXPERTS, per_expert, 0)


def _slotmap_kernel(ea_ref, eb_ref, ra_ref, rb_ref, cnt_ref, info_ref, src_ref, off_ref):
    n_tiles = info_ref.shape[0] - 1
    used = _expert_offsets(cnt_ref, off_ref)

    def per_expert(e, _):
        first = off_ref[e] >> MOE_TILE_SHIFT
        nt = (cnt_ref[e] + MOE_TILE - 1) >> MOE_TILE_SHIFT

        def fill(j, _):
            info_ref[first + j] = e
            return 0
        lax.fori_loop(0, nt, fill, 0)

        def pad(s, _):
            src_ref[s] = 0
            return 0
        lax.fori_loop(off_ref[e] + cnt_ref[e], off_ref[e] + nt * MOE_TILE, pad, 0)
        return 0
    lax.fori_loop(0, N_EXPERTS, per_expert, 0)
    info_ref[n_tiles] = used

    def tail(j, _):
        info_ref[j] = N_EXPERTS - 1
        return 0
    lax.fori_loop(used, n_tiles, tail, 0)

    def tail_src(s, _):
        src_ref[s] = 0
        return 0
    lax.fori_loop(used * MOE_TILE, src_ref.shape[0], tail_src, 0)

    def token(g, _):
        src_ref[off_ref[ea_ref[g]] + ra_ref[g]] = g
        src_ref[off_ref[eb_ref[g]] + rb_ref[g]] = g
        return 0
    lax.fori_loop(0, ea_ref.shape[0], token, 0, unroll=ROW_UNROLL)


def _slotmap(meta, counts):
    rows = meta.shape[1]
    n_tiles = _slot_tiles(rows)
    grid_spec = pltpu.PrefetchScalarGridSpec(
        num_scalar_prefetch=5, grid=(1,), in_specs=[],
        out_specs=[pl.BlockSpec(memory_space=pltpu.SMEM), pl.BlockSpec(memory_space=pltpu.SMEM)],
        scratch_shapes=[pltpu.SMEM((N_EXPERTS,), jnp.int32)])
    return pl.pallas_call(
        _slotmap_kernel, grid_spec=grid_spec,
        out_shape=[jax.ShapeDtypeStruct((n_tiles + 1,), jnp.int32),
                   jax.ShapeDtypeStruct((n_tiles * MOE_TILE + MOE_HALF,), jnp.int32)],
        compiler_params=_cparams("arbitrary"), name="moe_slotmap",
    )(meta[0], meta[1], meta[2], meta[3], counts)


def _ffn_kernel(info_ref, src_ref, h_ref, wg_ref, wu_ref, wd_ref, ys_ref, wg_s, wu_s, wd_s, xa, xb, sems):
    i = pl.program_id(0)
    n_tiles = info_ref.shape[0] - 1
    used = info_ref[n_tiles]
    fresh = jnp.logical_or(i == 0, info_ref[i] != info_ref[jnp.maximum(i - 1, 0)])
    base = i * MOE_TILE

    def gather(buf, sem, first_slot):
        for t in range(MOE_HALF):
            tok = src_ref[first_slot + t]
            pltpu.make_async_copy(h_ref.at[pl.ds(tok, 1), :], buf.at[pl.ds(t, 1), :], sem).start(priority=t % 2)

    def block_copy(buf, sem):
        return pltpu.make_async_copy(h_ref.at[pl.ds(0, MOE_HALF), :], buf, sem)

    def mlp(buf):
        x = buf[...].astype(BF16)
        hid = _silu(_dot(x, wg_s[...])) * _dot(x, wu_s[...])
        return _dot(hid.astype(BF16), wd_s[...])

    @pl.when(i == 0)
    def _():
        gather(xa, sems.at[0], 0)

    @pl.when(jnp.logical_and(i < used, fresh))
    def _():
        wg_s[...] = wg_ref[0, 0].astype(BF16)
        wu_s[...] = wu_ref[0, 0].astype(BF16)
        wd_s[...] = wd_ref[0, 0].astype(BF16)

    @pl.when(i < used)
    def _():
        block_copy(xa, sems.at[0]).wait()
        gather(xb, sems.at[1], base + MOE_HALF)
        ys_ref[:MOE_HALF, :] = mlp(xa)
        block_copy(xb, sems.at[1]).wait()
        gather(xa, sems.at[0], base + MOE_TILE)
        ys_ref[MOE_HALF:, :] = mlp(xb)

    @pl.when(i >= used)
    def _():
        block_copy(xa, sems.at[0]).wait()
        ys_ref[...] = jnp.zeros_like(ys_ref)
        block_copy(xa, sems.at[0]).start()

    @pl.when(i == pl.num_programs(0) - 1)
    def _():
        block_copy(xa, sems.at[0]).wait()


def _ffn(h, info, src, wg, wu, wd, layer):
    d = h.shape[1]
    n_tiles = info.shape[0] - 1
    f = wg.shape[3]

    def w_map(i, info, src):
        return (layer, info[i], 0, 0)

    grid_spec = pltpu.PrefetchScalarGridSpec(
        num_scalar_prefetch=2, grid=(n_tiles,),
        in_specs=[pl.BlockSpec(memory_space=pl.ANY),
                  pl.BlockSpec((1, 1, d, f), w_map), pl.BlockSpec((1, 1, d, f), w_map),
                  pl.BlockSpec((1, 1, f, d), w_map)],
        out_specs=pl.BlockSpec((MOE_TILE, d), lambda i, info, src: (i, 0)),
        scratch_shapes=[pltpu.VMEM((d, f), BF16), pltpu.VMEM((d, f), BF16), pltpu.VMEM((f, d), BF16),
                        pltpu.VMEM((MOE_HALF, d), F32), pltpu.VMEM((MOE_HALF, d), F32),
                        pltpu.SemaphoreType.DMA((2,))])
    return pl.pallas_call(
        _ffn_kernel, grid_spec=grid_spec, out_shape=jax.ShapeDtypeStruct((n_tiles * MOE_TILE, d), F32),
        compiler_params=_cparams("arbitrary"), name="moe_ffn",
    )(info, src, h, wg, wu, wd)


def _combine_kernel(*refs, has_mod, out_x):
    refs = list(refs)
    ea_ref, eb_ref, ra_ref, rb_ref, cnt_ref, x_ref, ys_ref, wcol_ref, gate_ref, g_ref = refs[:10]
    refs = refs[10:]
    mod_refs = (refs.pop(0), refs.pop(0)) if has_mod else None
    xo_ref = refs.pop(0) if out_x else None
    h_ref, off_ref, buf_a, buf_b, sems = refs
    i = pl.program_id(0)
    tm = x_ref.shape[0]

    def issue(tile, slot):
        base = tile * tm

        def body(t, _):
            g = base + t
            sa = off_ref[ea_ref[g]] + ra_ref[g]
            sb = off_ref[eb_ref[g]] + rb_ref[g]
            pltpu.make_async_copy(ys_ref.at[pl.ds(sa, 1), :], buf_a.at[slot, pl.ds(t, 1), :],
                                  sems.at[slot]).start(priority=0)
            pltpu.make_async_copy(ys_ref.at[pl.ds(sb, 1), :], buf_b.at[slot, pl.ds(t, 1), :],
                                  sems.at[slot]).start(priority=1)
            return 0
        lax.fori_loop(0, tm, body, 0, unroll=ROW_UNROLL)

    @pl.when(i == 0)
    def _():
        _expert_offsets(cnt_ref, off_ref)
        issue(0, 0)

    @pl.when(i + 1 < pl.num_programs(0))
    def _():
        issue(i + 1, (i + 1) % 2)

    slot = i % 2
    for buf in (buf_a, buf_b):
        pltpu.make_async_copy(ys_ref.at[pl.ds(0, tm), :], buf.at[slot], sems.at[slot]).wait()
    y = wcol_ref[:, 0:1] * buf_a[slot] + wcol_ref[:, 1:2] * buf_b[slot]
    x = x_ref[...] + gate_ref[0] * y
    if out_x:
        xo_ref[...] = x
    h_ref[...] = _norm_mod(x, g_ref, mod_refs).astype(h_ref.dtype)


def _combine(x, ys, meta, counts, wcol, gate, g, *, mod=None, out_x=False, h_dtype=BF16):
    rows, d = x.shape
    tm = MOE_TOKENS
    row_spec = pl.BlockSpec((tm, d), lambda i, *_: (i, 0))
    args = [x, ys, wcol, gate, g.reshape(1, d)]
    specs = [row_spec, pl.BlockSpec(memory_space=pl.ANY), pl.BlockSpec((tm, LANES), lambda i, *_: (i, 0)),
             _mod_spec(gate.shape[0], rows, tm, d), pl.BlockSpec((1, d), lambda i, *_: (0, 0))]
    if mod is not None:
        for m in mod:
            args.append(m)
            specs.append(_mod_spec(m.shape[0], rows, tm, d))
    out_shape, out_specs = [], []
    if out_x:
        out_shape.append(jax.ShapeDtypeStruct((rows, d), F32))
        out_specs.append(row_spec)
    out_shape.append(jax.ShapeDtypeStruct((rows, d), h_dtype))
    out_specs.append(row_spec)
    grid_spec = pltpu.PrefetchScalarGridSpec(
        num_scalar_prefetch=5, grid=(rows // tm,), in_specs=specs, out_specs=out_specs,
        scratch_shapes=[pltpu.SMEM((N_EXPERTS,), jnp.int32), pltpu.VMEM((2, tm, d), F32), pltpu.VMEM((2, tm, d), F32),
                        pltpu.SemaphoreType.DMA((2,))])
    outs = pl.pallas_call(
        functools.partial(_combine_kernel, has_mod=mod is not None, out_x=out_x), grid_spec=grid_spec,
        out_shape=out_shape, compiler_params=_cparams("arbitrary"), name="moe_combine",
    )(meta[0], meta[1], meta[2], meta[3], counts, *args)
    return outs if out_x else outs[0]


def _softmax_av(scores, values, sink_col):
    m = sink_col
    for s in scores:
        m = jnp.maximum(m, jnp.max(s, axis=-1, keepdims=True))
    den = jnp.exp(sink_col - m)
    acc = None
    for s, v in zip(scores, values):
        p = jnp.exp(s - m)
        den = den + jnp.sum(p, axis=-1, keepdims=True)
        pv = _dot(p.astype(BF16), v)
        acc = pv if acc is None else acc + pv
    return acc / den


def _sink_column(sink_ref, kv, rows):
    return jnp.concatenate([jnp.full((rows, 1), sink_ref[kv * ATT_GROUP + g], F32) for g in range(ATT_GROUP)], axis=0)


def _attn_ctx_kernel(sink_ref, qkv_ref, o_ref):
    t = qkv_ref.shape[0]
    qw = ATT_HEADS * HEAD_DIM
    kw = ATT_KV * HEAD_DIM
    heads_out = []
    for kv in range(ATT_KV):
        q = jnp.concatenate(
            [qkv_ref[:, (kv * ATT_GROUP + g) * HEAD_DIM:(kv * ATT_GROUP + g + 1) * HEAD_DIM] for g in range(ATT_GROUP)],
            axis=0).astype(BF16)
        k = qkv_ref[:, qw + kv * HEAD_DIM:qw + (kv + 1) * HEAD_DIM].astype(BF16)
        v = qkv_ref[:, qw + kw + kv * HEAD_DIM:qw + kw + (kv + 1) * HEAD_DIM].astype(BF16)
        s = _dot_nt(q, k) * HEAD_DIM ** -0.5
        o = _softmax_av([s], [v], _sink_column(sink_ref, kv, t))
        heads_out += [o[g * t:(g + 1) * t] for g in range(ATT_GROUP)]
    o_ref[...] = jnp.concatenate(heads_out, axis=1).astype(o_ref.dtype)


def _attn_ctx(qkv, sink, n_seq, seq_len):
    rows, cols = qkv.shape
    return pl.pallas_call(
        _attn_ctx_kernel, grid=(n_seq,),
        in_specs=[pl.BlockSpec(memory_space=pltpu.SMEM), pl.BlockSpec((seq_len, cols), lambda b: (b, 0))],
        out_specs=pl.BlockSpec((seq_len, ATT_HEADS * HEAD_DIM), lambda b: (b, 0)),
        out_shape=jax.ShapeDtypeStruct((rows, ATT_HEADS * HEAD_DIM), BF16),
        compiler_params=_cparams("parallel"), name="attn_context",
    )(sink, qkv)


def _rope_block(x, cos, sin_signed):
    lane = lax.broadcasted_iota(jnp.int32, x.shape, 1)
    nf = HEAD_DIM // 4
    partner = jnp.where((lane % (2 * nf)) < nf, pltpu.roll(x, LANES - nf, axis=1), pltpu.roll(x, nf, axis=1))
    return x * cos + partner * sin_signed


def _attn_lat_kernel(sink_ref, qkv_ref, ck_ref, cv_ref, cos_ref, sin_ref, o_ref, k_scr):
    i = pl.program_id(1)
    t = qkv_ref.shape[0]
    qw = ATT_HEADS * HEAD_DIM
    kw = ATT_KV * HEAD_DIM
    span = Q_BLOCK + 2 * WINDOW

    @pl.when(i == 0)
    def _():
        for c in range(kw // LANES):
            blk = qkv_ref[:, qw + c * LANES:qw + (c + 1) * LANES]
            k_scr[:, c * LANES:(c + 1) * LANES] = _rope_block(blk, cos_ref[...], sin_ref[...]).astype(BF16)

    r0 = pl.multiple_of(i * Q_BLOCK, Q_BLOCK)
    ws = pl.multiple_of(jnp.clip(r0 - WINDOW, 0, t - span), Q_BLOCK)
    cos_q = cos_ref[pl.ds(r0, Q_BLOCK), :]
    sin_q = sin_ref[pl.ds(r0, Q_BLOCK), :]
    qpos = r0 + lax.broadcasted_iota(jnp.int32, (Q_BLOCK, span), 0)
    kpos = ws + lax.broadcasted_iota(jnp.int32, (Q_BLOCK, span), 1)
    band = jnp.abs(qpos - kpos) <= WINDOW
    band = jnp.concatenate([band] * ATT_GROUP, axis=0)
    heads_out = []
    for kv in range(ATT_KV):
        heads = []
        for g in range(ATT_GROUP):
            h = kv * ATT_GROUP + g
            c, half = divmod(h * HEAD_DIM, LANES)
            blk = _rope_block(qkv_ref[pl.ds(r0, Q_BLOCK), c * LANES:(c + 1) * LANES], cos_q, sin_q)
            heads.append(blk[:, half:half + HEAD_DIM])
        q = jnp.concatenate(heads, axis=0).astype(BF16)
        ck = ck_ref[0, :, kv * HEAD_DIM:(kv + 1) * HEAD_DIM].astype(BF16)
        cv = cv_ref[0, :, kv * HEAD_DIM:(kv + 1) * HEAD_DIM].astype(BF16)
        kwin = k_scr[pl.ds(ws, span), kv * HEAD_DIM:(kv + 1) * HEAD_DIM]
        vwin = qkv_ref[pl.ds(ws, span), qw + kw + kv * HEAD_DIM:qw + kw + (kv + 1) * HEAD_DIM].astype(BF16)
        s_ctx = _dot_nt(q, ck) * HEAD_DIM ** -0.5
        s_win = jnp.where(band, _dot_nt(q, kwin) * HEAD_DIM ** -0.5, -jnp.inf)
        o = _softmax_av([s_ctx, s_win], [cv, vwin], _sink_column(sink_ref, kv, Q_BLOCK))
        heads_out += [o[g * Q_BLOCK:(g + 1) * Q_BLOCK] for g in range(ATT_GROUP)]
    o_ref[...] = jnp.concatenate(heads_out, axis=1).astype(o_ref.dtype)


def _rope_tables(seq_len):
    pos = jnp.arange(seq_len, dtype=jnp.int32)
    row = (pos // GRID_W).astype(F32)
    col = (pos % GRID_W).astype(F32)
    nf = HEAD_DIM // 4
    inv = ROPE_BASE ** (-jnp.arange(nf, dtype=F32) / nf)
    ang_r = row[:, None] * inv[None, :]
    ang_c = col[:, None] * inv[None, :]
    cos_h = jnp.concatenate([jnp.cos(ang_r), jnp.cos(ang_r), jnp.cos(ang_c), jnp.cos(ang_c)], axis=1)
    sin_h = jnp.concatenate([-jnp.sin(ang_r), jnp.sin(ang_r), -jnp.sin(ang_c), jnp.sin(ang_c)], axis=1)
    reps = LANES // HEAD_DIM
    return jnp.tile(cos_h, (1, reps)), jnp.tile(sin_h, (1, reps))


def _attn_lat(qkv, cache_k, cache_v, sink, n_seq, seq_len):
    rows, cols = qkv.shape
    past = cache_k.shape[1]
    kw = ATT_KV * HEAD_DIM
    cos, sin = _rope_tables(seq_len)
    return pl.pallas_call(
        _attn_lat_kernel, grid=(n_seq, seq_len // Q_BLOCK),
        in_specs=[pl.BlockSpec(memory_space=pltpu.SMEM),
                  pl.BlockSpec((seq_len, cols), lambda b, i: (b, 0)),
                  pl.BlockSpec((1, past, kw), lambda b, i: (b, 0, 0)),
                  pl.BlockSpec((1, past, kw), lambda b, i: (b, 0, 0)),
                  pl.BlockSpec((seq_len, LANES), lambda b, i: (0, 0)),
                  pl.BlockSpec((seq_len, LANES), lambda b, i: (0, 0))],
        out_specs=pl.BlockSpec((Q_BLOCK, ATT_HEADS * HEAD_DIM), lambda b, i: (b * (seq_len // Q_BLOCK) + i, 0)),
        out_shape=jax.ShapeDtypeStruct((rows, ATT_HEADS * HEAD_DIM), BF16),
        scratch_shapes=[pltpu.VMEM((seq_len, kw), BF16)],
        compiler_params=_cparams("parallel", "arbitrary"), name="attn_latent",
    )(sink, qkv, cache_k.reshape(n_seq, past, kw), cache_v.reshape(n_seq, past, kw), cos, sin)


def _ml_gates_kernel(h_ref, w_ref, wt_ref, b_ref, bt_ref, g_ref, gt_ref):
    h = h_ref[...]
    g_ref[...] = jnp.dot(h, w_ref[...], precision=HI, preferred_element_type=F32) + b_ref[...]
    gt_ref[...] = _dot_nt(wt_ref[...], h, precision=HI) + bt_ref[...]


def _ml_gates(h, w_gates, b_gates, *, tm=512):
    rows, d = h.shape
    ng = w_gates.shape[1]
    w_pad = jnp.pad(w_gates, ((0, 0), (0, LANES - ng)))
    b_pad = jnp.pad(b_gates, (0, LANES - ng)).reshape(1, LANES)
    return pl.pallas_call(
        _ml_gates_kernel, grid=(rows // tm,),
        in_specs=[pl.BlockSpec((tm, d), lambda i: (i, 0)), pl.BlockSpec((d, LANES), lambda i: (0, 0)),
                  pl.BlockSpec((ng, d), lambda i: (0, 0)), pl.BlockSpec((1, LANES), lambda i: (0, 0)),
                  pl.BlockSpec((ng, 1), lambda i: (0, 0))],
        out_specs=[pl.BlockSpec((tm, LANES), lambda i: (i, 0)), pl.BlockSpec((ng, tm), lambda i: (0, i))],
        out_shape=[jax.ShapeDtypeStruct((rows, LANES), F32), jax.ShapeDtypeStruct((ng, rows), F32)],
        compiler_params=_cparams("parallel"), name="mlstm_gates",
    )(h, w_pad, w_gates.T, b_pad, b_gates.reshape(ng, 1))


def _ml_conv_kernel(p_ref, w_ref, o_ref, *, k_scale):
    j = pl.program_id(1)
    x = p_ref[...]
    t = x.shape[0]
    row = lax.broadcasted_iota(jnp.int32, x.shape, 0)
    prev = jnp.where(row == 0, 0.0, pltpu.roll(x, 1, axis=0))
    nxt = jnp.where(row == t - 1, 0.0, pltpu.roll(x, t - 1, axis=0))
    y = prev * w_ref[0:1, :] + x * w_ref[1:2, :] + nxt * w_ref[2:3, :]
    scale = jnp.where(j >= pl.num_programs(1) // 2, k_scale, 1.0).astype(F32)
    o_ref[...] = (_silu(y) * scale).astype(o_ref.dtype)


def _ml_conv(p, conv_w, n_seq, seq_len, *, tn=512):
    rows = p.shape[0]
    width = conv_w.shape[1]
    return pl.pallas_call(
        functools.partial(_ml_conv_kernel, k_scale=ML_DK ** -0.5), grid=(n_seq, width // tn),
        in_specs=[pl.BlockSpec((seq_len, tn), lambda b, j: (b, j)), pl.BlockSpec((3, tn), lambda b, j: (0, j))],
        out_specs=pl.BlockSpec((seq_len, tn), lambda b, j: (b, j)),
        out_shape=jax.ShapeDtypeStruct((rows, width), BF16),
        compiler_params=_cparams("parallel", "parallel"), name="mlstm_conv",
    )(p, conv_w)


def _ml_scan_kernel(*refs, zero_init):
    refs = list(refs)
    dirs = [tuple(refs[0:5]), tuple(refs[5:10])]
    refs = refs[10:]
    if not zero_init:
        c0_ref, n0_ref, m0_ref = refs[:3]
        refs = refs[3:]
    hf_ref, hb_ref, c_ref, n_ref, m_ref = refs
    h_out = (hf_ref, hb_ref)
    c = pl.program_id(1)
    last = pl.num_programs(1) - 1

    @pl.when(c == 0)
    def _():
        if zero_init:
            c_ref[...] = jnp.zeros_like(c_ref)
            n_ref[...] = jnp.zeros_like(n_ref)
            m_ref[...] = jnp.zeros_like(m_ref)
        else:
            c_ref[...] = c0_ref[...]
            n_ref[...] = n0_ref[...]
            m_ref[...] = m0_ref[...]

    length = hf_ref.shape[0]
    ti = lax.broadcasted_iota(jnp.int32, (length, length), 0)
    si = lax.broadcasted_iota(jnp.int32, (length, length), 1)
    for d in range(2):
        q_ref, k_ref, v_ref, g_ref, gt_ref = dirs[d]
        causal = (ti >= si) if d == 0 else (ti <= si)
        tri = jnp.where(causal, 1.0, 0.0).astype(F32)
        f_col = _log_sigmoid(g_ref[...])
        f_row = _log_sigmoid(gt_ref[...])
        b_col = jnp.dot(tri, f_col, precision=HI, preferred_element_type=F32)
        b_row = _dot_nt(f_row, tri, precision=HI)
        edge = length - 1 if d == 0 else 0
        for h in range(ML_HEADS):
            ji = d * 2 * ML_HEADS + h
            jf = ji + ML_HEADS
            bc = b_col[:, jf:jf + 1]
            br = b_row[jf:jf + 1, :]
            i_row = gt_ref[ji:ji + 1, :]
            i_col = g_ref[:, ji:ji + 1]
            m_prev = m_ref[0, d, h][:, 0:1]
            q = q_ref[:, h * ML_DK:(h + 1) * ML_DK]
            k = k_ref[:, h * ML_DK:(h + 1) * ML_DK]
            v = v_ref[:, h * ML_DV:(h + 1) * ML_DV].astype(BF16)
            cst = c_ref[0, d, h]
            nst = n_ref[0, d, h]
            a_row = i_row - br
            amat = jnp.where(causal, a_row, -jnp.inf)
            u = jnp.maximum(m_prev, jnp.max(amat, axis=1, keepdims=True))
            qk = (_dot_nt(q, k) * jnp.exp(amat - u)).astype(BF16)
            sc = jnp.exp(m_prev - u)
            state_ext = jnp.concatenate([cst, jnp.broadcast_to(nst, (LANES, ML_DK))], axis=0).astype(BF16)
            v_ext = jnp.concatenate([v, jnp.ones((length, LANES), BF16)], axis=1)
            tot = sc * _dot_nt(q, state_ext) + _dot(qk, v_ext)
            inv = 1.0 / jnp.maximum(jnp.abs(tot[:, ML_DV:]), jnp.exp(-(bc + u)))
            h_out[d][:, h * ML_DV:(h + 1) * ML_DV] = tot[:, :ML_DV] * jnp.concatenate([inv] * (ML_DV // LANES), axis=1)
            b_last = br[:, edge:edge + 1]
            wlog_row = b_last + a_row
            m_new = jnp.maximum(b_last + m_prev, jnp.max(wlog_row, axis=1, keepdims=True))
            decay = jnp.exp(b_last + m_prev - m_new)
            ws_row = jnp.exp(wlog_row - m_new)
            ws_col = jnp.exp(b_last - bc + i_col - m_new)
            kw = (ws_col * k.astype(F32)).astype(BF16)
            c_ref[0, d, h] = decay * cst + _dot_tn(v, kw)
            n_ref[0, d, h] = decay * nst + _dot(jnp.broadcast_to(ws_row, (8, length)).astype(BF16), k)[0:1]
            m_ref[0, d, h] = jnp.broadcast_to(m_new, (1, ML_DK))


def _ml_scan(qk, p, g, gt, state, n_seq, seq_len):
    rows = qk.shape[0]
    length = min(ML_CHUNK, seq_len)
    nc = seq_len // length
    qw = ML_HEADS * ML_DK
    vw = ML_HEADS * ML_DV
    ng = gt.shape[0]

    def fwd(b, c):
        return b * nc + c

    def bwd(b, c):
        return b * nc + nc - 1 - c

    args, specs = [], []
    for pos in (fwd, bwd):
        args += [qk, qk, p, g, gt]
        specs += [pl.BlockSpec((length, qw), lambda b, c, pos=pos: (pos(b, c), 0)),
                  pl.BlockSpec((length, qw), lambda b, c, pos=pos: (pos(b, c), 1)),
                  pl.BlockSpec((length, vw), lambda b, c, pos=pos: (pos(b, c), 2 * qw // vw)),
                  pl.BlockSpec((length, LANES), lambda b, c, pos=pos: (pos(b, c), 0)),
                  pl.BlockSpec((ng, length), lambda b, c, pos=pos: (0, pos(b, c)))]
    c_spec = pl.BlockSpec((1, 2, ML_HEADS, ML_DV, ML_DK), lambda b, c: (b, 0, 0, 0, 0))
    n_spec = pl.BlockSpec((1, 2, ML_HEADS, 1, ML_DK), lambda b, c: (b, 0, 0, 0, 0))
    zero_init = state is None
    if not zero_init:
        c0, n0, m0 = state
        args += [c0, n0.reshape(n_seq, 2, ML_HEADS, 1, ML_DK),
                 jnp.broadcast_to(m0[..., None, None], (n_seq, 2, ML_HEADS, 1, ML_DK))]
        specs += [c_spec, n_spec, n_spec]
    hf, hb, c_fin, n_fin, m_fin = pl.pallas_call(
        functools.partial(_ml_scan_kernel, zero_init=zero_init), grid=(n_seq, nc), in_specs=specs,
        out_specs=[pl.BlockSpec((length, vw), lambda b, c: (fwd(b, c), 0)),
                   pl.BlockSpec((length, vw), lambda b, c: (bwd(b, c), 0)), c_spec, n_spec, n_spec],
        out_shape=[jax.ShapeDtypeStruct((rows, vw), F32), jax.ShapeDtypeStruct((rows, vw), F32),
                   jax.ShapeDtypeStruct((n_seq, 2, ML_HEADS, ML_DV, ML_DK), F32),
                   jax.ShapeDtypeStruct((n_seq, 2, ML_HEADS, 1, ML_DK), F32),
                   jax.ShapeDtypeStruct((n_seq, 2, ML_HEADS, 1, ML_DK), F32)],
        compiler_params=_cparams("parallel", "arbitrary"), name="mlstm_scan",
    )(*args)
    return hf, hb, (c_fin, n_fin[:, :, :, 0, :], m_fin[:, :, :, 0, 0])


def _gla_scan_kernel(*refs, zero_init):
    refs = list(refs)
    dirs = [tuple(refs[0:4]), tuple(refs[4:8])]
    w2_ref, ba_ref = refs[8:10]
    refs = refs[10:]
    if not zero_init:
        s0_ref = refs.pop(0)
    of_ref, ob_ref, s_ref, st_scr, la_scr = refs
    o_out = (of_ref, ob_ref)
    c = pl.program_id(1)
    last = pl.num_programs(1) - 1
    kw = GLA_HEADS * GLA_DK
    n_sub = of_ref.shape[0] // GLA_SUB

    @pl.when(c == 0)
    def _():
        for d in range(2):
            for h in range(GLA_HEADS):
                st_scr[d, h] = jnp.zeros((GLA_DV, GLA_DK), F32) if zero_init else s0_ref[0, d, h].T

    for d in range(2):
        u = dirs[d][3][...].astype(BF16)
        z = _dot(u, w2_ref[:, d * kw:(d + 1) * kw]) + ba_ref[:, d * kw:(d + 1) * kw]
        la_scr[d] = _log_sigmoid(z) / GLA_TAU

    ti = lax.broadcasted_iota(jnp.int32, (GLA_SUB, GLA_SUB), 0)
    si = lax.broadcasted_iota(jnp.int32, (GLA_SUB, GLA_SUB), 1)
    s_lane = lax.broadcasted_iota(jnp.int32, (GLA_SUB, GLA_SUB), 1)

    def sub_chunk(j, carry):
        for d in range(2):
            q_ref, k_ref, v_ref, _ = dirs[d]
            r0 = pl.multiple_of((j if d == 0 else n_sub - 1 - j) * GLA_SUB, GLA_SUB)
            causal = (ti >= si) if d == 0 else (ti <= si)
            tri = jnp.where(causal, 1.0, 0.0).astype(F32)
            bc_all = jnp.dot(tri, la_scr[d, pl.ds(r0, GLA_SUB), :], precision=HI, preferred_element_type=F32)
            edge = GLA_SUB - 1 if d == 0 else 0
            for h in range(GLA_HEADS):
                bc = bc_all[:, h * GLA_DK:(h + 1) * GLA_DK]
                q = q_ref[pl.ds(r0, GLA_SUB), h * GLA_DK:(h + 1) * GLA_DK] * GLA_DK ** -0.5
                k = k_ref[pl.ds(r0, GLA_SUB), h * GLA_DK:(h + 1) * GLA_DK]
                v = v_ref[pl.ds(r0, GLA_SUB), h * GLA_DV:(h + 1) * GLA_DV].astype(BF16)
                bc2 = bc * LOG2E
                a = jnp.zeros((GLA_SUB, GLA_SUB), F32)
                for s in range(GLA_SUB):
                    decay = jnp.exp2(bc2 - bc2[s:s + 1, :])
                    col = jnp.sum(q * (k[s:s + 1, :] * decay), axis=1, keepdims=True)
                    a = jnp.where(s_lane == s, col, a)
                a = jnp.where(causal, a, 0.0)
                st = st_scr[d, h]
                o = _dot(a.astype(BF16), v) + _dot_nt((q * jnp.exp2(bc2)).astype(BF16), st.astype(BF16))
                o_out[d][pl.ds(r0, GLA_SUB), h * GLA_DV:(h + 1) * GLA_DV] = o
                b_last = bc2[edge:edge + 1, :]
                k_dec = (k * jnp.exp2(b_last - bc2)).astype(BF16)
                st_scr[d, h] = jnp.exp2(b_last) * st + _dot_tn(v, k_dec)
        return carry

    lax.fori_loop(0, n_sub, sub_chunk, 0)

    @pl.when(c == last)
    def _():
        for d in range(2):
            for h in range(GLA_HEADS):
                s_ref[0, d, h] = st_scr[d, h].T


def _gla_scan(p, u, w2, b_a, state, n_seq, seq_len):
    rows = p.shape[0]
    length = min(GLA_BLOCK, seq_len)
    nc = seq_len // length
    kw = GLA_HEADS * GLA_DK
    vw = GLA_HEADS * GLA_DV

    def fwd(b, c):
        return b * nc + c

    def bwd(b, c):
        return b * nc + nc - 1 - c

    args, specs = [], []
    for pos in (fwd, bwd):
        args += [p, p, p, u]
        specs += [pl.BlockSpec((length, kw), lambda b, c, pos=pos: (pos(b, c), 0)),
                  pl.BlockSpec((length, kw), lambda b, c, pos=pos: (pos(b, c), 1)),
                  pl.BlockSpec((length, vw), lambda b, c, pos=pos: (pos(b, c), 2 * kw // vw)),
                  pl.BlockSpec((length, LANES), lambda b, c, pos=pos: (pos(b, c), 0))]
    args += [w2, b_a]
    specs += [pl.BlockSpec(w2.shape, lambda b, c: (0, 0)), pl.BlockSpec(b_a.shape, lambda b, c: (0, 0))]
    s_spec = pl.BlockSpec((1, 2, GLA_HEADS, GLA_DK, GLA_DV), lambda b, c: (b, 0, 0, 0, 0))
    zero_init = state is None
    if not zero_init:
        args.append(state)
        specs.append(s_spec)
    return pl.pallas_call(
        functools.partial(_gla_scan_kernel, zero_init=zero_init), grid=(n_seq, nc), in_specs=specs,
        out_specs=[pl.BlockSpec((length, vw), lambda b, c: (fwd(b, c), 0)),
                   pl.BlockSpec((length, vw), lambda b, c: (bwd(b, c), 0)), s_spec],
        out_shape=[jax.ShapeDtypeStruct((rows, vw), F32), jax.ShapeDtypeStruct((rows, vw), F32),
                   jax.ShapeDtypeStruct((n_seq, 2, GLA_HEADS, GLA_DK, GLA_DV), F32)],
        scratch_shapes=[pltpu.VMEM((2, GLA_HEADS, GLA_DV, GLA_DK), F32), pltpu.VMEM((2, length, kw), F32)],
        compiler_params=_cparams("parallel", "arbitrary"), name="gla_scan",
    )(*args)


def kernel(x_prompt, x_sample, cache_k_0, cache_v_0, state_mlstm_C_1, state_mlstm_n_1, state_mlstm_m_1, state_gla_S_2, cache_k_3, cache_v_3, c, c_ctx, w_mod, b_mod, norm1_g, norm2_g, final_g, router_w, router_b, moe_wg, moe_wu, moe_wd, attn0_w_qkv, attn0_sink, attn0_w_o, mlstm1_w_in, mlstm1_b_gates, mlstm1_conv, mlstm1_norm_g, mlstm1_w_out, gla2_w_in, gla2_w_a1, gla2_w_a2, gla2_b_a, gla2_norm_g, gla2_w_out, attn3_w_qkv, attn3_sink, attn3_w_o):
    n_ctx, ctx_len, d = x_prompt.shape
    n_lat, lat_len, _ = x_sample.shape
    depth = w_mod.shape[0]

    cvec = jnp.concatenate([c_ctx[None, :], c, jnp.zeros((8 - 1 - n_lat, d), F32)], axis=0)
    mod = _modulation(cvec, w_mod, b_mod).reshape(depth, 8, 6, 1, d)

    def mods(layer, kind, latent):
        return mod[layer, 1:1 + n_lat, kind] if latent else mod[layer, 0:1, kind]

    rw_hi = router_w.T.astype(BF16)
    rw_lo = (router_w.T - rw_hi.astype(F32)).astype(BF16)
    rwt = jnp.concatenate([rw_hi, rw_lo], axis=0)
    rb = router_b.reshape(-1, 1)
    attn_w = {0: (attn0_w_qkv.astype(BF16), attn0_sink, attn0_w_o.astype(BF16), cache_k_0, cache_v_0),
              3: (attn3_w_qkv.astype(BF16), attn3_sink, attn3_w_o.astype(BF16), cache_k_3, cache_v_3)}
    ml_qw = ML_HEADS * ML_DK
    ml_vw = ML_HEADS * ML_DV
    ml_main = 2 * ml_qw + 2 * ml_vw
    ml_w_main = mlstm1_w_in[:, :ml_main].astype(BF16)
    ml_w_gates = mlstm1_w_in[:, ml_main:]
    ml_w_out = mlstm1_w_out.astype(BF16)
    gla_kw = GLA_HEADS * GLA_DK
    gla_w_in = gla2_w_in.astype(BF16)
    gla_w_a1 = jnp.pad(jnp.concatenate([gla2_w_a1[0], gla2_w_a1[1]], axis=1),
                       ((0, 0), (0, LANES - 2 * GLA_RANK))).astype(BF16)
    gla_w2 = jnp.zeros((LANES, 2 * gla_kw), F32)
    gla_w2 = gla_w2.at[:GLA_RANK, :gla_kw].set(gla2_w_a2[0]).at[GLA_RANK:2 * GLA_RANK, gla_kw:].set(gla2_w_a2[1])
    gla_w2 = gla_w2.astype(BF16)
    gla_ba = gla2_b_a.reshape(1, 2 * gla_kw)
    gla_w_out = gla2_w_out.astype(BF16)

    def run_stream(x, n_seq, seq_len, latent):
        states = []
        h = _rownorm(x, norm1_g[0], mods(0, 1, latent), mods(0, 0, latent))
        for layer in range(depth):
            tail = (mods(layer, 2, latent), norm2_g[layer], mods(layer, 4, latent), mods(layer, 3, latent), rwt, rb)
            kind = layer % 3
            if kind == 0:
                w_qkv, sink, w_o, ck, cv = attn_w[layer]
                qkv = _matmul(h, w_qkv)
                if latent:
                    att = _attn_lat(qkv, ck, cv, sink, n_seq, seq_len)
                else:
                    att = _attn_ctx(qkv, sink, n_seq, seq_len)
                    qw = ATT_HEADS * HEAD_DIM
                    kw = ATT_KV * HEAD_DIM
                    states.append(qkv[:, qw:qw + kw].reshape(n_seq, seq_len, ATT_KV, HEAD_DIM))
                    states.append(qkv[:, qw + kw:].reshape(n_seq, seq_len, ATT_KV, HEAD_DIM))
                x, h2, meta, wcol, counts = _proj("plain", (att,), w_o, x, *tail)
            elif kind == 1:
                p = _matmul(h, ml_w_main)
                g, gt = _ml_gates(h, ml_w_gates, mlstm1_b_gates)
                qk = _ml_conv(p, mlstm1_conv, n_seq, seq_len)
                st = (state_mlstm_C_1, state_mlstm_n_1, state_mlstm_m_1) if latent else None
                hf, hb, fin = _ml_scan(qk, p, g, gt, st, n_seq, seq_len)
                if not latent:
                    states.extend(fin)
                x, h2, meta, wcol, counts = _proj("mlstm", (hf, hb, p, (2 * ml_qw + ml_vw) // ml_vw, mlstm1_norm_g), ml_w_out,
                                    x, *tail)
            else:
                p = _matmul(h, gla_w_in)
                u = _matmul(h, gla_w_a1)
                of, ob, s_fin = _gla_scan(p, u, gla_w2, gla_ba, state_gla_S_2 if latent else None, n_seq, seq_len)
                if not latent:
                    states.append(s_fin)
                gla_vw = GLA_HEADS * GLA_DV
                x, h2, meta, wcol, counts = _proj("gla", (of, ob, p, (2 * gla_kw + gla_vw) // gla_vw, gla2_norm_g), gla_w_out,
                                    x, *tail)
            cnt = counts[:, 0]
            info, src = _slotmap(meta, cnt)
            ys = _ffn(h2, info, src, moe_wg, moe_wu, moe_wd, layer)
            gate2 = mods(layer, 5, latent)
            if layer + 1 < depth:
                x, h = _combine(x, ys, meta, cnt, wcol, gate2, norm1_g[layer + 1],
                                mod=(mods(layer + 1, 1, latent), mods(layer + 1, 0, latent)), out_x=True,
                                h_dtype=F32 if (layer + 1) % 3 == 1 else BF16)
            else:
                out = _combine(x, ys, meta, cnt, wcol, gate2, final_g, h_dtype=F32)
        return out.reshape(n_seq, seq_len, d), states

    y_prompt, new_state = run_stream(x_prompt.reshape(n_ctx * ctx_len, d), n_ctx, ctx_len, False)
    y_sample, _ = run_stream(x_sample.reshape(n_lat * lat_len, d), n_lat, lat_len, True)
    return (y_prompt, y_sample, *new_state)
```

```python
import functools

import jax
import jax.numpy as jnp
from jax import lax
from jax.experimental import pallas as pl
from jax.experimental.pallas import tpu as pltpu

F32 = jnp.float32
BF16 = jnp.bfloat16
HI = lax.Precision.HIGHEST

EPS = 1e-6
LOG2E = 1.4426950408889634
GRID_W = 64
ATT_HEADS = 16
ATT_KV = 4
ATT_GROUP = ATT_HEADS // ATT_KV
HEAD_DIM = 64
WINDOW = 128
Q_BLOCK = 128
ROPE_BASE = 10000.0
ML_HEADS = 8
ML_DK = 128
ML_DV = 256
ML_CHUNK = 128
GLA_HEADS = 4
GLA_DK = 128
GLA_DV = 256
GLA_RANK = 16
GLA_TAU = 16.0
GLA_SUB = 16
GLA_BLOCK = 256
N_EXPERTS = 16
N_GROUPS = 4
GROUP_SIZE = N_EXPERTS // N_GROUPS
LANES = 128
VMEM_LIMIT = 56 * 1024 * 1024


def _cparams(*sem):
    return pltpu.CompilerParams(dimension_semantics=sem, vmem_limit_bytes=VMEM_LIMIT)


def _dot(a, b):
    return jnp.dot(a, b, preferred_element_type=F32)


def _dot_nt(a, b, precision=None):
    return lax.dot_general(a, b, (((1,), (1,)), ((), ())), precision=precision, preferred_element_type=F32)


def _dot_tn(a, b):
    return lax.dot_general(a, b, (((0,), (0,)), ((), ())), preferred_element_type=F32)


def _sigmoid(x):
    return 1.0 / (1.0 + jnp.exp(-x))


def _silu(x):
    return x * _sigmoid(x)


def _log_sigmoid(x):
    return jnp.minimum(x, 0.0) - jnp.log(1.0 + jnp.exp(-jnp.abs(x)))


def _rms_rows(x, g):
    ms = jnp.mean(x * x, axis=-1, keepdims=True)
    return x * lax.rsqrt(ms + EPS) * g


def _mod_kernel(c_ref, w_ref, b_ref, o_ref):
    s = _silu(c_ref[...])
    o_ref[0] = _dot(s.astype(BF16), w_ref[0].astype(BF16)) + b_ref[0]


def _modulation(cvec, w_mod, b_mod):
    depth, d, n6 = w_mod.shape
    tn = 1536
    return pl.pallas_call(
        _mod_kernel,
        grid=(depth, n6 // tn),
        in_specs=[pl.BlockSpec((8, d), lambda l, j: (0, 0)),
                  pl.BlockSpec((1, d, tn), lambda l, j: (l, 0, j)),
                  pl.BlockSpec((1, 1, tn), lambda l, j: (l, 0, j))],
        out_specs=pl.BlockSpec((1, 8, tn), lambda l, j: (l, 0, j)),
        out_shape=jax.ShapeDtypeStruct((depth, 8, n6), F32),
        compiler_params=_cparams("parallel", "parallel"),
        name="adaln_modulation",
    )(cvec, w_mod, b_mod.reshape(depth, 1, n6))


def _route(h, rwt, rb, carry):
    tm = h.shape[0]
    h_hi = h.astype(BF16)
    h_lo = (h - h_hi.astype(F32)).astype(BF16)
    by_hi = _dot_nt(rwt, h_hi)
    logits = by_hi[:N_EXPERTS] + by_hi[N_EXPERTS:] + _dot_nt(rwt[:N_EXPERTS], h_lo)
    scores = _sigmoid(logits)
    sel = scores + rb
    expert = lax.broadcasted_iota(jnp.int32, sel.shape, 0)
    pos = expert % GROUP_SIZE
    grp = expert // GROUP_SIZE

    def mate(x, k):
        ahead = pltpu.roll(x, N_EXPERTS - k, axis=0)
        behind = pltpu.roll(x, GROUP_SIZE - k, axis=0)
        return jnp.where(pos + k < GROUP_SIZE, ahead, behind)

    beaten = jnp.zeros_like(sel)
    for k in range(1, GROUP_SIZE):
        other = mate(sel, k)
        other_first = (pos + k) % GROUP_SIZE < pos
        beaten = beaten + jnp.where(other_first, jnp.where(other >= sel, 1.0, 0.0), jnp.where(other > sel, 1.0, 0.0))
    top2 = jnp.where(beaten < 2.0, sel, 0.0)
    gscore = top2
    for k in range(1, GROUP_SIZE):
        gscore = gscore + mate(top2, k)
    lost = jnp.zeros_like(sel)
    for k in range(1, N_GROUPS):
        other = pltpu.roll(gscore, N_EXPERTS - GROUP_SIZE * k, axis=0)
        other_first = (grp + k) % N_GROUPS < grp
        lost = lost + jnp.where(other_first, jnp.where(other >= gscore, 1.0, 0.0),
                                jnp.where(other > gscore, 1.0, 0.0))
    picked = jnp.where(lost < 0.5, jnp.where(beaten < 2.0, 1.0, 0.0), 0.0)
    chosen = picked > 0.5
    weight = jnp.where(chosen, scores, 0.0)
    wsum = jnp.sum(weight, axis=0, keepdims=True)
    e_f = expert.astype(F32)
    e_a = jnp.min(jnp.where(chosen, e_f, float(N_EXPERTS)), axis=0, keepdims=True)
    e_b = jnp.max(jnp.where(chosen, e_f, -1.0), axis=0, keepdims=True)
    before = (lax.broadcasted_iota(jnp.int32, (tm, tm), 0) < lax.broadcasted_iota(jnp.int32, (tm, tm), 1))
    rank = _dot(picked.astype(BF16), jnp.where(before, 1.0, 0.0).astype(BF16)) + carry
    is_a = e_f == e_a
    is_b = e_f == e_b
    r_a = jnp.sum(jnp.where(is_a, rank, 0.0), axis=0, keepdims=True)
    r_b = jnp.sum(jnp.where(is_b, rank, 0.0), axis=0, keepdims=True)
    w_a = jnp.sum(jnp.where(is_a, weight, 0.0), axis=0, keepdims=True)
    w_b = jnp.sum(jnp.where(is_b, weight, 0.0), axis=0, keepdims=True)
    meta = jnp.concatenate([e_a, e_b, r_a, r_b, jnp.zeros((4, tm), F32)], axis=0).astype(jnp.int32)
    wcol = jnp.concatenate([w_a / wsum, w_b / wsum, jnp.zeros((LANES - 2, tm), F32)], axis=0).T
    return meta, wcol, carry + jnp.sum(picked, axis=1, keepdims=True)


def _norm_mod(x, g_ref, mod_refs):
    h = _rms_rows(x, g_ref[...])
    if mod_refs is not None:
        a_ref, s_ref = mod_refs
        h = h * (1.0 + a_ref[0]) + s_ref[0]
    return h


def _rownorm_kernel(x_ref, g_ref, a_ref, s_ref, h_ref):
    h_ref[...] = _norm_mod(x_ref[...], g_ref, (a_ref, s_ref)).astype(h_ref.dtype)


def _mod_spec(n_mod, rows, tm, d, n_prefetch=0):
    per = (rows // n_mod) // tm
    return pl.BlockSpec((1, 1, d), lambda i, *_: (i // per, 0, 0))


def _rownorm(x, g, scale, shift, *, tm=512):
    rows, d = x.shape
    row_spec = pl.BlockSpec((tm, d), lambda i: (i, 0))
    return pl.pallas_call(
        _rownorm_kernel, grid=(rows // tm,),
        in_specs=[row_spec, pl.BlockSpec((1, d), lambda i: (0, 0)), _mod_spec(scale.shape[0], rows, tm, d),
                  _mod_spec(shift.shape[0], rows, tm, d)],
        out_specs=row_spec, out_shape=jax.ShapeDtypeStruct((rows, d), BF16),
        compiler_params=_cparams("parallel"), name="rownorm",
    )(x, g.reshape(1, d), scale, shift)


def _mm_kernel(a_ref, w_ref, o_ref):
    o_ref[...] = _dot(a_ref[...].astype(BF16), w_ref[...]).astype(o_ref.dtype)


def _matmul(a, w, *, out_dtype=F32, tm=1024):
    m, k = a.shape
    n = w.shape[1]
    tn = next(t for t in (1024, 768, 512, LANES) if n % t == 0)
    return pl.pallas_call(
        _mm_kernel, grid=(m // tm, n // tn),
        in_specs=[pl.BlockSpec((tm, k), lambda i, j: (i, 0)), pl.BlockSpec((k, tn), lambda i, j: (0, j))],
        out_specs=pl.BlockSpec((tm, tn), lambda i, j: (i, j)),
        out_shape=jax.ShapeDtypeStruct((m, n), out_dtype),
        compiler_params=_cparams("parallel", "parallel"), name="matmul",
    )(a, w)


def _head_norm(x, g, n_heads, dv):
    outs = []
    for h in range(n_heads):
        xs = x[:, h * dv:(h + 1) * dv]
        ms = jnp.mean(xs * xs, axis=-1, keepdims=True)
        outs.append(xs * lax.rsqrt(ms + EPS) * g[:, h * dv:(h + 1) * dv])
    return jnp.concatenate(outs, axis=1)


def _proj_kernel(*refs, pre):
    refs = list(refs)
    if pre == "plain":
        a = refs.pop(0)[...]
    else:
        f_ref, b_ref, p_ref, hg_ref = refs.pop(0), refs.pop(0), refs.pop(0), refs.pop(0)
        hsum = f_ref[...] + b_ref[...]
        if pre == "mlstm":
            a = _sigmoid(p_ref[...]) * _head_norm(hsum, hg_ref[...], ML_HEADS, ML_DV)
        else:
            a = _head_norm(hsum, hg_ref[...], GLA_HEADS, GLA_DV) * _silu(p_ref[...])
        a = a.astype(BF16)
    w_ref, x_ref, gate_ref, g_ref, a_ref, s_ref, rwt_ref, rb_ref = refs[:8]
    xo_ref, h_ref, meta_ref, wcol_ref, count_ref, carry_ref = refs[8:]

    @pl.when(pl.program_id(0) == 0)
    def _():
        carry_ref[...] = jnp.zeros_like(carry_ref)

    x = x_ref[...] + gate_ref[0] * _dot(a, w_ref[...])
    xo_ref[...] = x
    h = _norm_mod(x, g_ref, (a_ref, s_ref))
    h_ref[...] = h
    meta, wcol, carry = _route(h, rwt_ref[...], rb_ref[...], carry_ref[:, 0:1])
    meta_ref[...] = meta
    wcol_ref[...] = wcol
    carry_ref[...] = jnp.broadcast_to(carry, carry_ref.shape)
    count_ref[...] = jnp.broadcast_to(carry, count_ref.shape).astype(jnp.int32)


def _proj(pre, pre_args, w_out, x, gate, g, scale, shift, rwt, rb, *, tm=256):
    rows, d = x.shape
    k = w_out.shape[0]
    row_spec = pl.BlockSpec((tm, d), lambda i: (i, 0))
    if pre == "plain":
        args, specs = [pre_args[0]], [pl.BlockSpec((tm, k), lambda i: (i, 0))]
    else:
        hf, hb, p, col_block, hg = pre_args
        wide = pl.BlockSpec((tm, k), lambda i: (i, 0))
        args = [hf, hb, p, hg.reshape(1, k)]
        specs = [wide, wide, pl.BlockSpec((tm, k), lambda i: (i, col_block)), pl.BlockSpec((1, k), lambda i: (0, 0))]
    args += [w_out, x, gate, g.reshape(1, d), scale, shift, rwt, rb]
    specs += [pl.BlockSpec((k, d), lambda i: (0, 0)), row_spec, _mod_spec(gate.shape[0], rows, tm, d),
              pl.BlockSpec((1, d), lambda i: (0, 0)), _mod_spec(scale.shape[0], rows, tm, d),
              _mod_spec(shift.shape[0], rows, tm, d), pl.BlockSpec(rwt.shape, lambda i: (0, 0)),
              pl.BlockSpec(rb.shape, lambda i: (0, 0))]
    return pl.pallas_call(
        functools.partial(_proj_kernel, pre=pre), grid=(rows // tm,), in_specs=specs,
        out_specs=[row_spec, row_spec, pl.BlockSpec((8, tm), lambda i: (0, i)),
                   pl.BlockSpec((tm, LANES), lambda i: (i, 0)), pl.BlockSpec((N_EXPERTS, LANES), lambda i: (0, 0))],
        out_shape=[jax.ShapeDtypeStruct((rows, d), F32), jax.ShapeDtypeStruct((rows, d), F32),
                   jax.ShapeDtypeStruct((8, rows), jnp.int32), jax.ShapeDtypeStruct((rows, LANES), F32),
                   jax.ShapeDtypeStruct((N_EXPERTS, LANES), jnp.int32)],
        scratch_shapes=[pltpu.VMEM((N_EXPERTS, LANES), F32)],
        compiler_params=_cparams("arbitrary"), name="proj_" + pre,
    )(*args)


MOE_TILE = 256
MOE_TILE_SHIFT = 8
MOE_TOKENS = 256
ROW_UNROLL = 8


def _slot_tiles(rows):
    return (2 * rows) // MOE_TILE + N_EXPERTS


def _expert_offsets(cnt_ref, off_ref):
    def per_expert(e, k):
        off_ref[e] = k * MOE_TILE
        return k + ((cnt_ref[e] + MOE_TILE - 1) >> MOE_TILE_SHIFT)
    return lax.fori_loop(0, N_EXPERTS, per_expert, 0)


def _slots_kernel(cnt_ref, meta_ref, slot_ref, off_ref):
    @pl.when(pl.program_id(0) == 0)
    def _():
        _expert_offsets(cnt_ref, off_ref)

    e_a, e_b = meta_ref[0:1, :], meta_ref[1:2, :]
    off_a = jnp.zeros_like(e_a)
    off_b = jnp.zeros_like(e_b)
    for e in range(N_EXPERTS):
        off_a = jnp.where(e_a == e, off_ref[e], off_a)
        off_b = jnp.where(e_b == e, off_ref[e], off_b)
    slot_ref[...] = jnp.concatenate([off_a + meta_ref[2:3, :], off_b + meta_ref[3:4, :],
                                     jnp.zeros((6, e_a.shape[1]), jnp.int32)], axis=0)


def _slots(meta, counts, *, tm=1024):
    rows = meta.shape[1]
    grid_spec = pltpu.PrefetchScalarGridSpec(
        num_scalar_prefetch=1, grid=(rows // tm,),
        in_specs=[pl.BlockSpec((8, tm), lambda i, cnt: (0, i))],
        out_specs=pl.BlockSpec((8, tm), lambda i, cnt: (0, i)),
        scratch_shapes=[pltpu.SMEM((N_EXPERTS,), jnp.int32)])
    return pl.pallas_call(
        _slots_kernel, grid_spec=grid_spec, out_shape=jax.ShapeDtypeStruct((8, rows), jnp.int32),
        compiler_params=_cparams("arbitrary"), name="moe_slots",
    )(counts, meta)


def _dispatch_kernel(sa_ref, sb_ref, cnt_ref, h_ref, xs_ref, info_ref, off_ref, zero_ref, sem):
    i = pl.program_id(0)
    tm = h_ref.shape[0]
    n_tiles = info_ref.shape[0] - 1

    def tile_copy(tile):
        return pltpu.make_async_copy(zero_ref, xs_ref.at[pl.ds(tile * MOE_TILE, MOE_TILE), :], sem)

    @pl.when(i == 0)
    def _():
        zero_ref[...] = jnp.zeros_like(zero_ref)
        used = _expert_offsets(cnt_ref, off_ref)

        def per_expert(e, _):
            first = off_ref[e] >> MOE_TILE_SHIFT
            nt = (cnt_ref[e] + MOE_TILE - 1) >> MOE_TILE_SHIFT

            def fill(j, _):
                info_ref[first + j] = e
                return 0
            lax.fori_loop(0, nt, fill, 0)

            @pl.when(nt > 0)
            def _():
                tile_copy(first + nt - 1).start()
                tile_copy(first + nt - 1).wait()
            return 0
        lax.fori_loop(0, N_EXPERTS, per_expert, 0)
        info_ref[n_tiles] = used

        def tail(j, _):
            info_ref[j] = N_EXPERTS - 1
            tile_copy(j).start()
            tile_copy(j).wait()
            return 0
        lax.fori_loop(used, n_tiles, tail, 0)

    base = i * tm

    def row_copy(t, slot):
        return pltpu.make_async_copy(h_ref.at[pl.ds(t, 1), :], xs_ref.at[pl.ds(slot, 1), :], sem)

    for t in range(tm):
        row_copy(t, sa_ref[base + t]).start(priority=0)
        row_copy(t, sb_ref[base + t]).start(priority=1)
    for _ in range(2):
        pltpu.make_async_copy(h_ref, xs_ref.at[pl.ds(0, tm), :], sem).wait()


def _dispatch(h, slots, counts):
    rows, d = h.shape
    n_tiles = _slot_tiles(rows)
    tm = MOE_TOKENS
    grid_spec = pltpu.PrefetchScalarGridSpec(
        num_scalar_prefetch=3, grid=(rows // tm,),
        in_specs=[pl.BlockSpec((tm, d), lambda i, *_: (i, 0))],
        out_specs=[pl.BlockSpec(memory_space=pl.ANY), pl.BlockSpec(memory_space=pltpu.SMEM)],
        scratch_shapes=[pltpu.SMEM((N_EXPERTS,), jnp.int32), pltpu.VMEM((MOE_TILE, d), F32),
                        pltpu.SemaphoreType.DMA(())])
    return pl.pallas_call(
        _dispatch_kernel, grid_spec=grid_spec,
        out_shape=[jax.ShapeDtypeStruct((n_tiles * MOE_TILE, d), F32),
                   jax.ShapeDtypeStruct((n_tiles + 1,), jnp.int32)],
        compiler_params=_cparams("arbitrary"), name="moe_dispatch",
    )(slots[0], slots[1], counts, h)


def _ffn_kernel(info_ref, xs_ref, wg_ref, wu_ref, wd_ref, ys_ref, wg_s, wu_s, wd_s):
    i = pl.program_id(0)
    used = info_ref[info_ref.shape[0] - 1]
    fresh = jnp.logical_or(i == 0, info_ref[i] != info_ref[jnp.maximum(i - 1, 0)])

    @pl.when(jnp.logical_and(i < used, fresh))
    def _():
        wg_s[...] = wg_ref[0, 0].astype(BF16)
        wu_s[...] = wu_ref[0, 0].astype(BF16)
        wd_s[...] = wd_ref[0, 0].astype(BF16)

    @pl.when(i < used)
    def _():
        x = xs_ref[...].astype(BF16)
        hid = _silu(_dot(x, wg_s[...])) * _dot(x, wu_s[...])
        ys_ref[...] = _dot(hid.astype(BF16), wd_s[...])

    @pl.when(i >= used)
    def _():
        ys_ref[...] = jnp.zeros_like(ys_ref)


def _ffn(xs, info, wg, wu, wd, layer):
    slots, d = xs.shape
    n_tiles = slots // MOE_TILE
    f = wg.shape[3]

    def w_map(i, info):
        return (layer, info[i], 0, 0)

    grid_spec = pltpu.PrefetchScalarGridSpec(
        num_scalar_prefetch=1, grid=(n_tiles,),
        in_specs=[pl.BlockSpec((MOE_TILE, d), lambda i, info: (jnp.minimum(i, info[n_tiles] - 1), 0)),
                  pl.BlockSpec((1, 1, d, f), w_map), pl.BlockSpec((1, 1, d, f), w_map),
                  pl.BlockSpec((1, 1, f, d), w_map)],
        out_specs=pl.BlockSpec((MOE_TILE, d), lambda i, info: (i, 0)),
        scratch_shapes=[pltpu.VMEM((d, f), BF16), pltpu.VMEM((d, f), BF16), pltpu.VMEM((f, d), BF16)])
    return pl.pallas_call(
        _ffn_kernel, grid_spec=grid_spec, out_shape=jax.ShapeDtypeStruct((slots, d), F32),
        compiler_params=_cparams("arbitrary"), name="moe_ffn",
    )(info, xs, wg, wu, wd)


def _combine_kernel(*refs, has_mod, out_x):
    refs = list(refs)
    sa_ref, sb_ref, x_ref, ys_ref, wcol_ref, gate_ref, g_ref = refs[:7]
    refs = refs[7:]
    mod_refs = (refs.pop(0), refs.pop(0)) if has_mod else None
    xo_ref = refs.pop(0) if out_x else None
    h_ref, buf_a, buf_b, sems = refs
    i = pl.program_id(0)
    tm = x_ref.shape[0]

    def issue(tile, slot):
        base = tile * tm
        for t in range(tm):
            pltpu.make_async_copy(ys_ref.at[pl.ds(sa_ref[base + t], 1), :], buf_a.at[slot, pl.ds(t, 1), :],
                                  sems.at[slot]).start(priority=0)
            pltpu.make_async_copy(ys_ref.at[pl.ds(sb_ref[base + t], 1), :], buf_b.at[slot, pl.ds(t, 1), :],
                                  sems.at[slot]).start(priority=1)

    @pl.when(i == 0)
    def _():
        issue(0, 0)

    @pl.when(i + 1 < pl.num_programs(0))
    def _():
        issue(i + 1, (i + 1) % 2)

    slot = i % 2
    for buf in (buf_a, buf_b):
        pltpu.make_async_copy(ys_ref.at[pl.ds(0, tm), :], buf.at[slot], sems.at[slot]).wait()
    y = wcol_ref[:, 0:1] * buf_a[slot] + wcol_ref[:, 1:2] * buf_b[slot]
    x = x_ref[...] + gate_ref[0] * y
    if out_x:
        xo_ref[...] = x
    h_ref[...] = _norm_mod(x, g_ref, mod_refs).astype(h_ref.dtype)


def _combine(x, ys, slots, wcol, gate, g, *, mod=None, out_x=False, h_dtype=BF16):
    rows, d = x.shape
    tm = MOE_TOKENS
    row_spec = pl.BlockSpec((tm, d), lambda i, *_: (i, 0))
    args = [x, ys, wcol, gate, g.reshape(1, d)]
    specs = [row_spec, pl.BlockSpec(memory_space=pl.ANY), pl.BlockSpec((tm, LANES), lambda i, *_: (i, 0)),
             _mod_spec(gate.shape[0], rows, tm, d), pl.BlockSpec((1, d), lambda i, *_: (0, 0))]
    if mod is not None:
        for m in mod:
            args.append(m)
            specs.append(_mod_spec(m.shape[0], rows, tm, d))
    out_shape, out_specs = [], []
    if out_x:
        out_shape.append(jax.ShapeDtypeStruct((rows, d), F32))
        out_specs.append(row_spec)
    out_shape.append(jax.ShapeDtypeStruct((rows, d), h_dtype))
    out_specs.append(row_spec)
    grid_spec = pltpu.PrefetchScalarGridSpec(
        num_scalar_prefetch=2, grid=(rows // tm,), in_specs=specs, out_specs=out_specs,
        scratch_shapes=[pltpu.VMEM((2, tm, d), F32), pltpu.VMEM((2, tm, d), F32), pltpu.SemaphoreType.DMA((2,))])
    outs = pl.pallas_call(
        functools.partial(_combine_kernel, has_mod=mod is not None, out_x=out_x), grid_spec=grid_spec,
        out_shape=out_shape, compiler_params=_cparams("arbitrary"), name="moe_combine",
    )(slots[0], slots[1], *args)
    return outs if out_x else outs[0]


def _softmax_av(scores, values, sink_col):
    m = sink_col
    for s in scores:
        m = jnp.maximum(m, jnp.max(s, axis=-1, keepdims=True))
    den = jnp.exp(sink_col - m)
    acc = None
    for s, v in zip(scores, values):
        p = jnp.exp(s - m)
        den = den + jnp.sum(p, axis=-1, keepdims=True)
        pv = _dot(p.astype(BF16), v)
        acc = pv if acc is None else acc + pv
    return acc / den


def _sink_column(sink_ref, kv, rows):
    return jnp.concatenate([jnp.full((rows, 1), sink_ref[kv * ATT_GROUP + g], F32) for g in range(ATT_GROUP)], axis=0)


def _attn_ctx_kernel(sink_ref, qkv_ref, o_ref):
    t = qkv_ref.shape[0]
    qw = ATT_HEADS * HEAD_DIM
    kw = ATT_KV * HEAD_DIM
    heads_out = []
    for kv in range(ATT_KV):
        q = jnp.concatenate(
            [qkv_ref[:, (kv * ATT_GROUP + g) * HEAD_DIM:(kv * ATT_GROUP + g + 1) * HEAD_DIM] for g in range(ATT_GROUP)],
            axis=0).astype(BF16)
        k = qkv_ref[:, qw + kv * HEAD_DIM:qw + (kv + 1) * HEAD_DIM].astype(BF16)
        v = qkv_ref[:, qw + kw + kv * HEAD_DIM:qw + kw + (kv + 1) * HEAD_DIM].astype(BF16)
        s = _dot_nt(q, k) * HEAD_DIM ** -0.5
        o = _softmax_av([s], [v], _sink_column(sink_ref, kv, t))
        heads_out += [o[g * t:(g + 1) * t] for g in range(ATT_GROUP)]
    o_ref[...] = jnp.concatenate(heads_out, axis=1).astype(o_ref.dtype)


def _attn_ctx(qkv, sink, n_seq, seq_len):
    rows, cols = qkv.shape
    return pl.pallas_call(
        _attn_ctx_kernel, grid=(n_seq,),
        in_specs=[pl.BlockSpec(memory_space=pltpu.SMEM), pl.BlockSpec((seq_len, cols), lambda b: (b, 0))],
        out_specs=pl.BlockSpec((seq_len, ATT_HEADS * HEAD_DIM), lambda b: (b, 0)),
        out_shape=jax.ShapeDtypeStruct((rows, ATT_HEADS * HEAD_DIM), BF16),
        compiler_params=_cparams("parallel"), name="attn_context",
    )(sink, qkv)


def _rope_block(x, cos, sin_signed):
    lane = lax.broadcasted_iota(jnp.int32, x.shape, 1)
    nf = HEAD_DIM // 4
    partner = jnp.where((lane % (2 * nf)) < nf, pltpu.roll(x, LANES - nf, axis=1), pltpu.roll(x, nf, axis=1))
    return x * cos + partner * sin_signed


def _attn_lat_kernel(sink_ref, qkv_ref, ck_ref, cv_ref, cos_ref, sin_ref, o_ref, k_scr):
    i = pl.program_id(1)
    t = qkv_ref.shape[0]
    qw = ATT_HEADS * HEAD_DIM
    kw = ATT_KV * HEAD_DIM
    span = Q_BLOCK + 2 * WINDOW

    @pl.when(i == 0)
    def _():
        for c in range(kw // LANES):
            blk = qkv_ref[:, qw + c * LANES:qw + (c + 1) * LANES]
            k_scr[:, c * LANES:(c + 1) * LANES] = _rope_block(blk, cos_ref[...], sin_ref[...]).astype(BF16)

    r0 = pl.multiple_of(i * Q_BLOCK, Q_BLOCK)
    ws = pl.multiple_of(jnp.clip(r0 - WINDOW, 0, t - span), Q_BLOCK)
    cos_q = cos_ref[pl.ds(r0, Q_BLOCK), :]
    sin_q = sin_ref[pl.ds(r0, Q_BLOCK), :]
    qpos = r0 + lax.broadcasted_iota(jnp.int32, (Q_BLOCK, span), 0)
    kpos = ws + lax.broadcasted_iota(jnp.int32, (Q_BLOCK, span), 1)
    band = jnp.abs(qpos - kpos) <= WINDOW
    band = jnp.concatenate([band] * ATT_GROUP, axis=0)
    heads_out = []
    for kv in range(ATT_KV):
        heads = []
        for g in range(ATT_GROUP):
            h = kv * ATT_GROUP + g
            c, half = divmod(h * HEAD_DIM, LANES)
            blk = _rope_block(qkv_ref[pl.ds(r0, Q_BLOCK), c * LANES:(c + 1) * LANES], cos_q, sin_q)
            heads.append(blk[:, half:half + HEAD_DIM])
        q = jnp.concatenate(heads, axis=0).astype(BF16)
        ck = ck_ref[0, :, kv * HEAD_DIM:(kv + 1) * HEAD_DIM].astype(BF16)
        cv = cv_ref[0, :, kv * HEAD_DIM:(kv + 1) * HEAD_DIM].astype(BF16)
        kwin = k_scr[pl.ds(ws, span), kv * HEAD_DIM:(kv + 1) * HEAD_DIM]
        vwin = qkv_ref[pl.ds(ws, span), qw + kw + kv * HEAD_DIM:qw + kw + (kv + 1) * HEAD_DIM].astype(BF16)
        s_ctx = _dot_nt(q, ck) * HEAD_DIM ** -0.5
        s_win = jnp.where(band, _dot_nt(q, kwin) * HEAD_DIM ** -0.5, -jnp.inf)
        o = _softmax_av([s_ctx, s_win], [cv, vwin], _sink_column(sink_ref, kv, Q_BLOCK))
        heads_out += [o[g * Q_BLOCK:(g + 1) * Q_BLOCK] for g in range(ATT_GROUP)]
    o_ref[...] = jnp.concatenate(heads_out, axis=1).astype(o_ref.dtype)


def _rope_tables(seq_len):
    pos = jnp.arange(seq_len, dtype=jnp.int32)
    row = (pos // GRID_W).astype(F32)
    col = (pos % GRID_W).astype(F32)
    nf = HEAD_DIM // 4
    inv = ROPE_BASE ** (-jnp.arange(nf, dtype=F32) / nf)
    ang_r = row[:, None] * inv[None, :]
    ang_c = col[:, None] * inv[None, :]
    cos_h = jnp.concatenate([jnp.cos(ang_r), jnp.cos(ang_r), jnp.cos(ang_c), jnp.cos(ang_c)], axis=1)
    sin_h = jnp.concatenate([-jnp.sin(ang_r), jnp.sin(ang_r), -jnp.sin(ang_c), jnp.sin(ang_c)], axis=1)
    reps = LANES // HEAD_DIM
    return jnp.tile(cos_h, (1, reps)), jnp.tile(sin_h, (1, reps))


def _attn_lat(qkv, cache_k, cache_v, sink, n_seq, seq_len):
    rows, cols = qkv.shape
    past = cache_k.shape[1]
    kw = ATT_KV * HEAD_DIM
    cos, sin = _rope_tables(seq_len)
    return pl.pallas_call(
        _attn_lat_kernel, grid=(n_seq, seq_len // Q_BLOCK),
        in_specs=[pl.BlockSpec(memory_space=pltpu.SMEM),
                  pl.BlockSpec((seq_len, cols), lambda b, i: (b, 0)),
                  pl.BlockSpec((1, past, kw), lambda b, i: (b, 0, 0)),
                  pl.BlockSpec((1, past, kw), lambda b, i: (b, 0, 0)),
                  pl.BlockSpec((seq_len, LANES), lambda b, i: (0, 0)),
                  pl.BlockSpec((seq_len, LANES), lambda b, i: (0, 0))],
        out_specs=pl.BlockSpec((Q_BLOCK, ATT_HEADS * HEAD_DIM), lambda b, i: (b * (seq_len // Q_BLOCK) + i, 0)),
        out_shape=jax.ShapeDtypeStruct((rows, ATT_HEADS * HEAD_DIM), BF16),
        scratch_shapes=[pltpu.VMEM((seq_len, kw), BF16)],
        compiler_params=_cparams("parallel", "arbitrary"), name="attn_latent",
    )(sink, qkv, cache_k.reshape(n_seq, past, kw), cache_v.reshape(n_seq, past, kw), cos, sin)


def _ml_gates_kernel(h_ref, w_ref, wt_ref, b_ref, bt_ref, g_ref, gt_ref):
    h = h_ref[...]
    g_ref[...] = jnp.dot(h, w_ref[...], precision=HI, preferred_element_type=F32) + b_ref[...]
    gt_ref[...] = _dot_nt(wt_ref[...], h, precision=HI) + bt_ref[...]


def _ml_gates(h, w_gates, b_gates, *, tm=512):
    rows, d = h.shape
    ng = w_gates.shape[1]
    w_pad = jnp.pad(w_gates, ((0, 0), (0, LANES - ng)))
    b_pad = jnp.pad(b_gates, (0, LANES - ng)).reshape(1, LANES)
    return pl.pallas_call(
        _ml_gates_kernel, grid=(rows // tm,),
        in_specs=[pl.BlockSpec((tm, d), lambda i: (i, 0)), pl.BlockSpec((d, LANES), lambda i: (0, 0)),
                  pl.BlockSpec((ng, d), lambda i: (0, 0)), pl.BlockSpec((1, LANES), lambda i: (0, 0)),
                  pl.BlockSpec((ng, 1), lambda i: (0, 0))],
        out_specs=[pl.BlockSpec((tm, LANES), lambda i: (i, 0)), pl.BlockSpec((ng, tm), lambda i: (0, i))],
        out_shape=[jax.ShapeDtypeStruct((rows, LANES), F32), jax.ShapeDtypeStruct((ng, rows), F32)],
        compiler_params=_cparams("parallel"), name="mlstm_gates",
    )(h, w_pad, w_gates.T, b_pad, b_gates.reshape(ng, 1))


def _ml_conv_kernel(p_ref, w_ref, o_ref, *, k_scale):
    j = pl.program_id(1)
    x = p_ref[...]
    t = x.shape[0]
    row = lax.broadcasted_iota(jnp.int32, x.shape, 0)
    prev = jnp.where(row == 0, 0.0, pltpu.roll(x, 1, axis=0))
    nxt = jnp.where(row == t - 1, 0.0, pltpu.roll(x, t - 1, axis=0))
    y = prev * w_ref[0:1, :] + x * w_ref[1:2, :] + nxt * w_ref[2:3, :]
    scale = jnp.where(j >= pl.num_programs(1) // 2, k_scale, 1.0).astype(F32)
    o_ref[...] = (_silu(y) * scale).astype(o_ref.dtype)


def _ml_conv(p, conv_w, n_seq, seq_len, *, tn=512):
    rows = p.shape[0]
    width = conv_w.shape[1]
    return pl.pallas_call(
        functools.partial(_ml_conv_kernel, k_scale=ML_DK ** -0.5), grid=(n_seq, width // tn),
        in_specs=[pl.BlockSpec((seq_len, tn), lambda b, j: (b, j)), pl.BlockSpec((3, tn), lambda b, j: (0, j))],
        out_specs=pl.BlockSpec((seq_len, tn), lambda b, j: (b, j)),
        out_shape=jax.ShapeDtypeStruct((rows, width), BF16),
        compiler_params=_cparams("parallel", "parallel"), name="mlstm_conv",
    )(p, conv_w)


def _ml_scan_kernel(*refs, zero_init):
    refs = list(refs)
    dirs = [tuple(refs[0:5]), tuple(refs[5:10])]
    refs = refs[10:]
    if not zero_init:
        c0_ref, n0_ref, m0_ref = refs[:3]
        refs = refs[3:]
    hf_ref, hb_ref, c_ref, n_ref, m_ref = refs
    h_out = (hf_ref, hb_ref)
    c = pl.program_id(1)
    last = pl.num_programs(1) - 1

    @pl.when(c == 0)
    def _():
        if zero_init:
            c_ref[...] = jnp.zeros_like(c_ref)
            n_ref[...] = jnp.zeros_like(n_ref)
            m_ref[...] = jnp.zeros_like(m_ref)
        else:
            c_ref[...] = c0_ref[...]
            n_ref[...] = n0_ref[...]
            m_ref[...] = m0_ref[...]

    length = hf_ref.shape[0]
    ti = lax.broadcasted_iota(jnp.int32, (length, length), 0)
    si = lax.broadcasted_iota(jnp.int32, (length, length), 1)
    for d in range(2):
        q_ref, k_ref, v_ref, g_ref, gt_ref = dirs[d]
        causal = (ti >= si) if d == 0 else (ti <= si)
        tri = jnp.where(causal, 1.0, 0.0).astype(F32)
        f_col = _log_sigmoid(g_ref[...])
        f_row = _log_sigmoid(gt_ref[...])
        b_col = jnp.dot(tri, f_col, precision=HI, preferred_element_type=F32)
        b_row = _dot_nt(f_row, tri, precision=HI)
        edge = length - 1 if d == 0 else 0
        for h in range(ML_HEADS):
            ji = d * 2 * ML_HEADS + h
            jf = ji + ML_HEADS
            bc = b_col[:, jf:jf + 1]
            br = b_row[jf:jf + 1, :]
            i_row = gt_ref[ji:ji + 1, :]
            i_col = g_ref[:, ji:ji + 1]
            m_prev = m_ref[0, d, h][:, 0:1]
            q = q_ref[:, h * ML_DK:(h + 1) * ML_DK]
            k = k_ref[:, h * ML_DK:(h + 1) * ML_DK]
            v = v_ref[:, h * ML_DV:(h + 1) * ML_DV].astype(BF16)
            cst = c_ref[0, d, h]
            nst = n_ref[0, d, h]
            a_row = i_row - br
            amat = jnp.where(causal, a_row, -jnp.inf)
            u = jnp.maximum(m_prev, jnp.max(amat, axis=1, keepdims=True))
            qk = (_dot_nt(q, k) * jnp.exp(amat - u)).astype(BF16)
            sc = jnp.exp(m_prev - u)
            state_ext = jnp.concatenate([cst, jnp.broadcast_to(nst, (LANES, ML_DK))], axis=0).astype(BF16)
            v_ext = jnp.concatenate([v, jnp.ones((length, LANES), BF16)], axis=1)
            tot = sc * _dot_nt(q, state_ext) + _dot(qk, v_ext)
            inv = 1.0 / jnp.maximum(jnp.abs(tot[:, ML_DV:]), jnp.exp(-(bc + u)))
            h_out[d][:, h * ML_DV:(h + 1) * ML_DV] = tot[:, :ML_DV] * jnp.concatenate([inv] * (ML_DV // LANES), axis=1)
            b_last = br[:, edge:edge + 1]
            wlog_row = b_last + a_row
            m_new = jnp.maximum(b_last + m_prev, jnp.max(wlog_row, axis=1, keepdims=True))
            decay = jnp.exp(b_last + m_prev - m_new)
            ws_row = jnp.exp(wlog_row - m_new)
            ws_col = jnp.exp(b_last - bc + i_col - m_new)
            kw = (ws_col * k.astype(F32)).astype(BF16)
            c_ref[0, d, h] = decay * cst + _dot_tn(v, kw)
            n_ref[0, d, h] = decay * nst + _dot(jnp.broadcast_to(ws_row, (8, length)).astype(BF16), k)[0:1]
            m_ref[0, d, h] = jnp.broadcast_to(m_new, (1, ML_DK))


def _ml_scan(qk, p, g, gt, state, n_seq, seq_len):
    rows = qk.shape[0]
    length = min(ML_CHUNK, seq_len)
    nc = seq_len // length
    qw = ML_HEADS * ML_DK
    vw = ML_HEADS * ML_DV
    ng = gt.shape[0]

    def fwd(b, c):
        return b * nc + c

    def bwd(b, c):
        return b * nc + nc - 1 - c

    args, specs = [], []
    for pos in (fwd, bwd):
        args += [qk, qk, p, g, gt]
        specs += [pl.BlockSpec((length, qw), lambda b, c, pos=pos: (pos(b, c), 0)),
                  pl.BlockSpec((length, qw), lambda b, c, pos=pos: (pos(b, c), 1)),
                  pl.BlockSpec((length, vw), lambda b, c, pos=pos: (pos(b, c), 2 * qw // vw)),
                  pl.BlockSpec((length, LANES), lambda b, c, pos=pos: (pos(b, c), 0)),
                  pl.BlockSpec((ng, length), lambda b, c, pos=pos: (0, pos(b, c)))]
    c_spec = pl.BlockSpec((1, 2, ML_HEADS, ML_DV, ML_DK), lambda b, c: (b, 0, 0, 0, 0))
    n_spec = pl.BlockSpec((1, 2, ML_HEADS, 1, ML_DK), lambda b, c: (b, 0, 0, 0, 0))
    zero_init = state is None
    if not zero_init:
        c0, n0, m0 = state
        args += [c0, n0.reshape(n_seq, 2, ML_HEADS, 1, ML_DK),
                 jnp.broadcast_to(m0[..., None, None], (n_seq, 2, ML_HEADS, 1, ML_DK))]
        specs += [c_spec, n_spec, n_spec]
    hf, hb, c_fin, n_fin, m_fin = pl.pallas_call(
        functools.partial(_ml_scan_kernel, zero_init=zero_init), grid=(n_seq, nc), in_specs=specs,
        out_specs=[pl.BlockSpec((length, vw), lambda b, c: (fwd(b, c), 0)),
                   pl.BlockSpec((length, vw), lambda b, c: (bwd(b, c), 0)), c_spec, n_spec, n_spec],
        out_shape=[jax.ShapeDtypeStruct((rows, vw), F32), jax.ShapeDtypeStruct((rows, vw), F32),
                   jax.ShapeDtypeStruct((n_seq, 2, ML_HEADS, ML_DV, ML_DK), F32),
                   jax.ShapeDtypeStruct((n_seq, 2, ML_HEADS, 1, ML_DK), F32),
                   jax.ShapeDtypeStruct((n_seq, 2, ML_HEADS, 1, ML_DK), F32)],
        compiler_params=_cparams("parallel", "arbitrary"), name="mlstm_scan",
    )(*args)
    return hf, hb, (c_fin, n_fin[:, :, :, 0, :], m_fin[:, :, :, 0, 0])


def _gla_scan_kernel(*refs, zero_init):
    refs = list(refs)
    dirs = [tuple(refs[0:4]), tuple(refs[4:8])]
    w2_ref, ba_ref = refs[8:10]
    refs = refs[10:]
    if not zero_init:
        s0_ref = refs.pop(0)
    of_ref, ob_ref, s_ref, st_scr, la_scr = refs
    o_out = (of_ref, ob_ref)
    c = pl.program_id(1)
    last = pl.num_programs(1) - 1
    kw = GLA_HEADS * GLA_DK
    n_sub = of_ref.shape[0] // GLA_SUB

    @pl.when(c == 0)
    def _():
        for d in range(2):
            for h in range(GLA_HEADS):
                st_scr[d, h] = jnp.zeros((GLA_DV, GLA_DK), F32) if zero_init else s0_ref[0, d, h].T

    for d in range(2):
        u = dirs[d][3][...].astype(BF16)
        z = _dot(u, w2_ref[:, d * kw:(d + 1) * kw]) + ba_ref[:, d * kw:(d + 1) * kw]
        la_scr[d] = _log_sigmoid(z) / GLA_TAU

    ti = lax.broadcasted_iota(jnp.int32, (GLA_SUB, GLA_SUB), 0)
    si = lax.broadcasted_iota(jnp.int32, (GLA_SUB, GLA_SUB), 1)
    s_lane = lax.broadcasted_iota(jnp.int32, (GLA_SUB, GLA_SUB), 1)

    def sub_chunk(j, carry):
        for d in range(2):
            q_ref, k_ref, v_ref, _ = dirs[d]
            r0 = pl.multiple_of((j if d == 0 else n_sub - 1 - j) * GLA_SUB, GLA_SUB)
            causal = (ti >= si) if d == 0 else (ti <= si)
            tri = jnp.where(causal, 1.0, 0.0).astype(F32)
            bc_all = jnp.dot(tri, la_scr[d, pl.ds(r0, GLA_SUB), :], precision=HI, preferred_element_type=F32)
            edge = GLA_SUB - 1 if d == 0 else 0
            for h in range(GLA_HEADS):
                bc = bc_all[:, h * GLA_DK:(h + 1) * GLA_DK]
                q = q_ref[pl.ds(r0, GLA_SUB), h * GLA_DK:(h + 1) * GLA_DK] * GLA_DK ** -0.5
                k = k_ref[pl.ds(r0, GLA_SUB), h * GLA_DK:(h + 1) * GLA_DK]
                v = v_ref[pl.ds(r0, GLA_SUB), h * GLA_DV:(h + 1) * GLA_DV].astype(BF16)
                bc2 = bc * LOG2E
                a = jnp.zeros((GLA_SUB, GLA_SUB), F32)
                for s in range(GLA_SUB):
                    decay = jnp.exp2(bc2 - bc2[s:s + 1, :])
                    col = jnp.sum(q * (k[s:s + 1, :] * decay), axis=1, keepdims=True)
                    a = jnp.where(s_lane == s, col, a)
                a = jnp.where(causal, a, 0.0)
                st = st_scr[d, h]
                o = _dot(a.astype(BF16), v) + _dot_nt((q * jnp.exp2(bc2)).astype(BF16), st.astype(BF16))
                o_out[d][pl.ds(r0, GLA_SUB), h * GLA_DV:(h + 1) * GLA_DV] = o
                b_last = bc2[edge:edge + 1, :]
                k_dec = (k * jnp.exp2(b_last - bc2)).astype(BF16)
                st_scr[d, h] = jnp.exp2(b_last) * st + _dot_tn(v, k_dec)
        return carry

    lax.fori_loop(0, n_sub, sub_chunk, 0)

    @pl.when(c == last)
    def _():
        for d in range(2):
            for h in range(GLA_HEADS):
                s_ref[0, d, h] = st_scr[d, h].T


def _gla_scan(p, u, w2, b_a, state, n_seq, seq_len):
    rows = p.shape[0]
    length = min(GLA_BLOCK, seq_len)
    nc = seq_len // length
    kw = GLA_HEADS * GLA_DK
    vw = GLA_HEADS * GLA_DV

    def fwd(b, c):
        return b * nc + c

    def bwd(b, c):
        return b * nc + nc - 1 - c

    args, specs = [], []
    for pos in (fwd, bwd):
        args += [p, p, p, u]
        specs += [pl.BlockSpec((length, kw), lambda b, c, pos=pos: (pos(b, c), 0)),
                  pl.BlockSpec((length, kw), lambda b, c, pos=pos: (pos(b, c), 1)),
                  pl.BlockSpec((length, vw), lambda b, c, pos=pos: (pos(b, c), 2 * kw // vw)),
                  pl.BlockSpec((length, LANES), lambda b, c, pos=pos: (pos(b, c), 0))]
    args += [w2, b_a]
    specs += [pl.BlockSpec(w2.shape, lambda b, c: (0, 0)), pl.BlockSpec(b_a.shape, lambda b, c: (0, 0))]
    s_spec = pl.BlockSpec((1, 2, GLA_HEADS, GLA_DK, GLA_DV), lambda b, c: (b, 0, 0, 0, 0))
    zero_init = state is None
    if not zero_init:
        args.append(state)
        specs.append(s_spec)
    return pl.pallas_call(
        functools.partial(_gla_scan_kernel, zero_init=zero_init), grid=(n_seq, nc), in_specs=specs,
        out_specs=[pl.BlockSpec((length, vw), lambda b, c: (fwd(b, c), 0)),
                   pl.BlockSpec((length, vw), lambda b, c: (bwd(b, c), 0)), s_spec],
        out_shape=[jax.ShapeDtypeStruct((rows, vw), F32), jax.ShapeDtypeStruct((rows, vw), F32),
                   jax.ShapeDtypeStruct((n_seq, 2, GLA_HEADS, GLA_DK, GLA_DV), F32)],
        scratch_shapes=[pltpu.VMEM((2, GLA_HEADS, GLA_DV, GLA_DK), F32), pltpu.VMEM((2, length, kw), F32)],
        compiler_params=_cparams("parallel", "arbitrary"), name="gla_scan",
    )(*args)


def kernel(x_prompt, x_sample, cache_k_0, cache_v_0, state_mlstm_C_1, state_mlstm_n_1, state_mlstm_m_1, state_gla_S_2, cache_k_3, cache_v_3, c, c_ctx, w_mod, b_mod, norm1_g, norm2_g, final_g, router_w, router_b, moe_wg, moe_wu, moe_wd, attn0_w_qkv, attn0_sink, attn0_w_o, mlstm1_w_in, mlstm1_b_gates, mlstm1_conv, mlstm1_norm_g, mlstm1_w_out, gla2_w_in, gla2_w_a1, gla2_w_a2, gla2_b_a, gla2_norm_g, gla2_w_out, attn3_w_qkv, attn3_sink, attn3_w_o):
    n_ctx, ctx_len, d = x_prompt.shape
    n_lat, lat_len, _ = x_sample.shape
    depth = w_mod.shape[0]

    cvec = jnp.concatenate([c_ctx[None, :], c, jnp.zeros((8 - 1 - n_lat, d), F32)], axis=0)
    mod = _modulation(cvec, w_mod, b_mod).reshape(depth, 8, 6, 1, d)

    def mods(layer, kind, latent):
        return mod[layer, 1:1 + n_lat, kind] if latent else mod[layer, 0:1, kind]

    rw_hi = router_w.T.astype(BF16)
    rw_lo = (router_w.T - rw_hi.astype(F32)).astype(BF16)
    rwt = jnp.concatenate([rw_hi, rw_lo], axis=0)
    rb = router_b.reshape(-1, 1)
    attn_w = {0: (attn0_w_qkv.astype(BF16), attn0_sink, attn0_w_o.astype(BF16), cache_k_0, cache_v_0),
              3: (attn3_w_qkv.astype(BF16), attn3_sink, attn3_w_o.astype(BF16), cache_k_3, cache_v_3)}
    ml_qw = ML_HEADS * ML_DK
    ml_vw = ML_HEADS * ML_DV
    ml_main = 2 * ml_qw + 2 * ml_vw
    ml_w_main = mlstm1_w_in[:, :ml_main].astype(BF16)
    ml_w_gates = mlstm1_w_in[:, ml_main:]
    ml_w_out = mlstm1_w_out.astype(BF16)
    gla_kw = GLA_HEADS * GLA_DK
    gla_w_in = gla2_w_in.astype(BF16)
    gla_w_a1 = jnp.pad(jnp.concatenate([gla2_w_a1[0], gla2_w_a1[1]], axis=1),
                       ((0, 0), (0, LANES - 2 * GLA_RANK))).astype(BF16)
    gla_w2 = jnp.zeros((LANES, 2 * gla_kw), F32)
    gla_w2 = gla_w2.at[:GLA_RANK, :gla_kw].set(gla2_w_a2[0]).at[GLA_RANK:2 * GLA_RANK, gla_kw:].set(gla2_w_a2[1])
    gla_w2 = gla_w2.astype(BF16)
    gla_ba = gla2_b_a.reshape(1, 2 * gla_kw)
    gla_w_out = gla2_w_out.astype(BF16)

    def run_stream(x, n_seq, seq_len, latent):
        states = []
        h = _rownorm(x, norm1_g[0], mods(0, 1, latent), mods(0, 0, latent))
        for layer in range(depth):
            tail = (mods(layer, 2, latent), norm2_g[layer], mods(layer, 4, latent), mods(layer, 3, latent), rwt, rb)
            kind = layer % 3
            if kind == 0:
                w_qkv, sink, w_o, ck, cv = attn_w[layer]
                qkv = _matmul(h, w_qkv)
                if latent:
                    att = _attn_lat(qkv, ck, cv, sink, n_seq, seq_len)
                else:
                    att = _attn_ctx(qkv, sink, n_seq, seq_len)
                    qw = ATT_HEADS * HEAD_DIM
                    kw = ATT_KV * HEAD_DIM
                    states.append(qkv[:, qw:qw + kw].reshape(n_seq, seq_len, ATT_KV, HEAD_DIM))
                    states.append(qkv[:, qw + kw:].reshape(n_seq, seq_len, ATT_KV, HEAD_DIM))
                x, h2, meta, wcol, counts = _proj("plain", (att,), w_o, x, *tail)
            elif kind == 1:
                p = _matmul(h, ml_w_main)
                g, gt = _ml_gates(h, ml_w_gates, mlstm1_b_gates)
                qk = _ml_conv(p, mlstm1_conv, n_seq, seq_len)
                st = (state_mlstm_C_1, state_mlstm_n_1, state_mlstm_m_1) if latent else None
                hf, hb, fin = _ml_scan(qk, p, g, gt, st, n_seq, seq_len)
                if not latent:
                    states.extend(fin)
                x, h2, meta, wcol, counts = _proj("mlstm", (hf, hb, p, (2 * ml_qw + ml_vw) // ml_vw, mlstm1_norm_g), ml_w_out,
                                    x, *tail)
            else:
                p = _matmul(h, gla_w_in)
                u = _matmul(h, gla_w_a1)
                of, ob, s_fin = _gla_scan(p, u, gla_w2, gla_ba, state_gla_S_2 if latent else None, n_seq, seq_len)
                if not latent:
                    states.append(s_fin)
                gla_vw = GLA_HEADS * GLA_DV
                x, h2, meta, wcol, counts = _proj("gla", (of, ob, p, (2 * gla_kw + gla_vw) // gla_vw, gla2_norm_g), gla_w_out,
                                    x, *tail)
            cnt = counts[:, 0]
            slots = _slots(meta, cnt)
            xs, info = _dispatch(h2, slots, cnt)
            ys = _ffn(xs, info, moe_wg, moe_wu, moe_wd, layer)
            gate2 = mods(layer, 5, latent)
            if layer + 1 < depth:
                x, h = _combine(x, ys, slots, wcol, gate2, norm1_g[layer + 1],
                                mod=(mods(layer + 1, 1, latent), mods(layer + 1, 0, latent)), out_x=True,
                                h_dtype=F32 if (layer + 1) % 3 == 1 else BF16)
            else:
                out = _combine(x, ys, slots, wcol, gate2, final_g, h_dtype=F32)
        return out.reshape(n_seq, seq_len, d), states

    y_prompt, new_state = run_stream(x_prompt.reshape(n_ctx * ctx_len, d), n_ctx, ctx_len, False)
    y_sample, _ = run_stream(x_sample.reshape(n_lat * lat_len, d), n_lat, lat_len, True)
    return (y_prompt, y_sample, *new_state)
```

```python
import functools

import jax
import jax.numpy as jnp
from jax import lax
from jax.experimental import pallas as pl
from jax.experimental.pallas import tpu as pltpu

F32 = jnp.float32
BF16 = jnp.bfloat16
HI = lax.Precision.HIGHEST

EPS = 1e-6
LOG2E = 1.4426950408889634
GRID_W = 64
ATT_HEADS = 16
ATT_KV = 4
ATT_GROUP = ATT_HEADS // ATT_KV
HEAD_DIM = 64
WINDOW = 128
Q_BLOCK = 128
ROPE_BASE = 10000.0
ML_HEADS = 8
ML_DK = 128
ML_DV = 256
ML_CHUNK = 128
GLA_HEADS = 4
GLA_DK = 128
GLA_DV = 256
GLA_RANK = 16
GLA_TAU = 16.0
GLA_SUB = 16
GLA_BLOCK = 256
N_EXPERTS = 16
N_GROUPS = 4
GROUP_SIZE = N_EXPERTS // N_GROUPS
LANES = 128
VMEM_LIMIT = 56 * 1024 * 1024


def _cparams(*sem):
    return pltpu.CompilerParams(dimension_semantics=sem, vmem_limit_bytes=VMEM_LIMIT)


def _dot(a, b):
    return jnp.dot(a, b, preferred_element_type=F32)


def _dot_nt(a, b, precision=None):
    return lax.dot_general(a, b, (((1,), (1,)), ((), ())), precision=precision, preferred_element_type=F32)


def _dot_tn(a, b):
    return lax.dot_general(a, b, (((0,), (0,)), ((), ())), preferred_element_type=F32)


def _sigmoid(x):
    return 1.0 / (1.0 + jnp.exp(-x))


def _silu(x):
    return x * _sigmoid(x)


def _log_sigmoid(x):
    return jnp.minimum(x, 0.0) - jnp.log(1.0 + jnp.exp(-jnp.abs(x)))


def _rms_rows(x, g):
    ms = jnp.mean(x * x, axis=-1, keepdims=True)
    return x * lax.rsqrt(ms + EPS) * g


def _mod_kernel(c_ref, w_ref, b_ref, o_ref):
    s = _silu(c_ref[...])
    o_ref[0] = _dot(s.astype(BF16), w_ref[0].astype(BF16)) + b_ref[0]


def _modulation(cvec, w_mod, b_mod):
    depth, d, n6 = w_mod.shape
    tn = 1536
    return pl.pallas_call(
        _mod_kernel,
        grid=(depth, n6 // tn),
        in_specs=[pl.BlockSpec((8, d), lambda l, j: (0, 0)),
                  pl.BlockSpec((1, d, tn), lambda l, j: (l, 0, j)),
                  pl.BlockSpec((1, 1, tn), lambda l, j: (l, 0, j))],
        out_specs=pl.BlockSpec((1, 8, tn), lambda l, j: (l, 0, j)),
        out_shape=jax.ShapeDtypeStruct((depth, 8, n6), F32),
        compiler_params=_cparams("parallel", "parallel"),
        name="adaln_modulation",
    )(cvec, w_mod, b_mod.reshape(depth, 1, n6))


def _route(h, rwt, rb, carry):
    tm = h.shape[0]
    h_hi = h.astype(BF16)
    h_lo = (h - h_hi.astype(F32)).astype(BF16)
    by_hi = _dot_nt(rwt, h_hi)
    logits = by_hi[:N_EXPERTS] + by_hi[N_EXPERTS:] + _dot_nt(rwt[:N_EXPERTS], h_lo)
    scores = _sigmoid(logits)
    sel = scores + rb
    expert = lax.broadcasted_iota(jnp.int32, sel.shape, 0)
    pos = expert % GROUP_SIZE
    grp = expert // GROUP_SIZE

    def mate(x, k):
        ahead = pltpu.roll(x, N_EXPERTS - k, axis=0)
        behind = pltpu.roll(x, GROUP_SIZE - k, axis=0)
        return jnp.where(pos + k < GROUP_SIZE, ahead, behind)

    beaten = jnp.zeros_like(sel)
    for k in range(1, GROUP_SIZE):
        other = mate(sel, k)
        other_first = (pos + k) % GROUP_SIZE < pos
        beaten = beaten + jnp.where(other_first, jnp.where(other >= sel, 1.0, 0.0), jnp.where(other > sel, 1.0, 0.0))
    top2 = jnp.where(beaten < 2.0, sel, 0.0)
    gscore = top2
    for k in range(1, GROUP_SIZE):
        gscore = gscore + mate(top2, k)
    lost = jnp.zeros_like(sel)
    for k in range(1, N_GROUPS):
        other = pltpu.roll(gscore, N_EXPERTS - GROUP_SIZE * k, axis=0)
        other_first = (grp + k) % N_GROUPS < grp
        lost = lost + jnp.where(other_first, jnp.where(other >= gscore, 1.0, 0.0),
                                jnp.where(other > gscore, 1.0, 0.0))
    picked = jnp.where(lost < 0.5, jnp.where(beaten < 2.0, 1.0, 0.0), 0.0)
    chosen = picked > 0.5
    weight = jnp.where(chosen, scores, 0.0)
    wsum = jnp.sum(weight, axis=0, keepdims=True)
    e_f = expert.astype(F32)
    e_a = jnp.min(jnp.where(chosen, e_f, float(N_EXPERTS)), axis=0, keepdims=True)
    e_b = jnp.max(jnp.where(chosen, e_f, -1.0), axis=0, keepdims=True)
    before = (lax.broadcasted_iota(jnp.int32, (tm, tm), 0) < lax.broadcasted_iota(jnp.int32, (tm, tm), 1))
    rank = _dot(picked.astype(BF16), jnp.where(before, 1.0, 0.0).astype(BF16)) + carry
    is_a = e_f == e_a
    is_b = e_f == e_b
    r_a = jnp.sum(jnp.where(is_a, rank, 0.0), axis=0, keepdims=True)
    r_b = jnp.sum(jnp.where(is_b, rank, 0.0), axis=0, keepdims=True)
    w_a = jnp.sum(jnp.where(is_a, weight, 0.0), axis=0, keepdims=True)
    w_b = jnp.sum(jnp.where(is_b, weight, 0.0), axis=0, keepdims=True)
    meta = jnp.concatenate([e_a, e_b, r_a, r_b, jnp.zeros((4, tm), F32)], axis=0).astype(jnp.int32)
    wcol = jnp.concatenate([w_a / wsum, w_b / wsum, jnp.zeros((LANES - 2, tm), F32)], axis=0).T
    return meta, wcol, carry + jnp.sum(picked, axis=1, keepdims=True)


def _norm_mod(x, g_ref, mod_refs):
    h = _rms_rows(x, g_ref[...])
    if mod_refs is not None:
        a_ref, s_ref = mod_refs
        h = h * (1.0 + a_ref[0]) + s_ref[0]
    return h


def _rownorm_kernel(x_ref, g_ref, a_ref, s_ref, h_ref):
    h_ref[...] = _norm_mod(x_ref[...], g_ref, (a_ref, s_ref)).astype(h_ref.dtype)


def _mod_spec(n_mod, rows, tm, d, n_prefetch=0):
    per = (rows // n_mod) // tm
    return pl.BlockSpec((1, 1, d), lambda i, *_: (i // per, 0, 0))


def _rownorm(x, g, scale, shift, *, tm=512):
    rows, d = x.shape
    row_spec = pl.BlockSpec((tm, d), lambda i: (i, 0))
    return pl.pallas_call(
        _rownorm_kernel, grid=(rows // tm,),
        in_specs=[row_spec, pl.BlockSpec((1, d), lambda i: (0, 0)), _mod_spec(scale.shape[0], rows, tm, d),
                  _mod_spec(shift.shape[0], rows, tm, d)],
        out_specs=row_spec, out_shape=jax.ShapeDtypeStruct((rows, d), BF16),
        compiler_params=_cparams("parallel"), name="rownorm",
    )(x, g.reshape(1, d), scale, shift)


def _mm_kernel(a_ref, w_ref, o_ref):
    o_ref[...] = _dot(a_ref[...].astype(BF16), w_ref[...]).astype(o_ref.dtype)


def _matmul(a, w, *, out_dtype=F32, tm=1024):
    m, k = a.shape
    n = w.shape[1]
    tn = next(t for t in (1024, 768, 512, LANES) if n % t == 0)
    return pl.pallas_call(
        _mm_kernel, grid=(m // tm, n // tn),
        in_specs=[pl.BlockSpec((tm, k), lambda i, j: (i, 0)), pl.BlockSpec((k, tn), lambda i, j: (0, j))],
        out_specs=pl.BlockSpec((tm, tn), lambda i, j: (i, j)),
        out_shape=jax.ShapeDtypeStruct((m, n), out_dtype),
        compiler_params=_cparams("parallel", "parallel"), name="matmul",
    )(a, w)


def _head_norm(x, g, n_heads, dv):
    outs = []
    for h in range(n_heads):
        xs = x[:, h * dv:(h + 1) * dv]
        ms = jnp.mean(xs * xs, axis=-1, keepdims=True)
        outs.append(xs * lax.rsqrt(ms + EPS) * g[:, h * dv:(h + 1) * dv])
    return jnp.concatenate(outs, axis=1)


def _proj_kernel(*refs, pre):
    refs = list(refs)
    if pre == "plain":
        a = refs.pop(0)[...]
    else:
        f_ref, b_ref, p_ref, hg_ref = refs.pop(0), refs.pop(0), refs.pop(0), refs.pop(0)
        hsum = f_ref[...] + b_ref[...]
        if pre == "mlstm":
            a = _sigmoid(p_ref[...]) * _head_norm(hsum, hg_ref[...], ML_HEADS, ML_DV)
        else:
            a = _head_norm(hsum, hg_ref[...], GLA_HEADS, GLA_DV) * _silu(p_ref[...])
        a = a.astype(BF16)
    w_ref, x_ref, gate_ref, g_ref, a_ref, s_ref, rwt_ref, rb_ref, count0_ref = refs[:9]
    xo_ref, h_ref, meta_ref, wcol_ref, count_ref, carry_ref = refs[9:]

    @pl.when(pl.program_id(0) == 0)
    def _():
        carry_ref[...] = count0_ref[...].astype(F32)

    x = x_ref[...] + gate_ref[0] * _dot(a, w_ref[...])
    xo_ref[...] = x
    h = _norm_mod(x, g_ref, (a_ref, s_ref))
    h_ref[...] = h
    meta, wcol, carry = _route(h, rwt_ref[...], rb_ref[...], carry_ref[:, 0:1])
    meta_ref[...] = meta
    wcol_ref[...] = wcol
    carry_ref[...] = jnp.broadcast_to(carry, carry_ref.shape)
    count_ref[...] = jnp.broadcast_to(carry, count_ref.shape).astype(jnp.int32)


def _proj(pre, pre_args, w_out, x, gate, g, scale, shift, rwt, rb, count0, *, tm=256):
    rows, d = x.shape
    k = w_out.shape[0]
    row_spec = pl.BlockSpec((tm, d), lambda i: (i, 0))
    if pre == "plain":
        args, specs = [pre_args[0]], [pl.BlockSpec((tm, k), lambda i: (i, 0))]
    else:
        hf, hb, p, col_block, hg = pre_args
        wide = pl.BlockSpec((tm, k), lambda i: (i, 0))
        args = [hf, hb, p, hg.reshape(1, k)]
        specs = [wide, wide, pl.BlockSpec((tm, k), lambda i: (i, col_block)), pl.BlockSpec((1, k), lambda i: (0, 0))]
    args += [w_out, x, gate, g.reshape(1, d), scale, shift, rwt, rb, count0]
    specs += [pl.BlockSpec((k, d), lambda i: (0, 0)), row_spec, _mod_spec(gate.shape[0], rows, tm, d),
              pl.BlockSpec((1, d), lambda i: (0, 0)), _mod_spec(scale.shape[0], rows, tm, d),
              _mod_spec(shift.shape[0], rows, tm, d), pl.BlockSpec(rwt.shape, lambda i: (0, 0)),
              pl.BlockSpec(rb.shape, lambda i: (0, 0)), pl.BlockSpec(count0.shape, lambda i: (0, 0))]
    return pl.pallas_call(
        functools.partial(_proj_kernel, pre=pre), grid=(rows // tm,), in_specs=specs,
        out_specs=[row_spec, row_spec, pl.BlockSpec((8, tm), lambda i: (0, i)),
                   pl.BlockSpec((tm, LANES), lambda i: (i, 0)), pl.BlockSpec((N_EXPERTS, LANES), lambda i: (0, 0))],
        out_shape=[jax.ShapeDtypeStruct((rows, d), F32), jax.ShapeDtypeStruct((rows, d), F32),
                   jax.ShapeDtypeStruct((8, rows), jnp.int32), jax.ShapeDtypeStruct((rows, LANES), F32),
                   jax.ShapeDtypeStruct((N_EXPERTS, LANES), jnp.int32)],
        scratch_shapes=[pltpu.VMEM((N_EXPERTS, LANES), F32)],
        compiler_params=_cparams("arbitrary"), name="proj_" + pre,
    )(*args)


MOE_TILE = 256
MOE_TILE_SHIFT = 8
MOE_TOKENS = 256
ROW_UNROLL = 8


def _slot_tiles(rows):
    return (2 * rows) // MOE_TILE + N_EXPERTS


def _expert_offsets(cnt_ref, off_ref):
    def per_expert(e, k):
        off_ref[e] = k * MOE_TILE
        return k + ((cnt_ref[e] + MOE_TILE - 1) >> MOE_TILE_SHIFT)
    return lax.fori_loop(0, N_EXPERTS, per_expert, 0)


def _slots_kernel(cnt_ref, meta_ref, slot_ref, off_ref):
    @pl.when(pl.program_id(0) == 0)
    def _():
        _expert_offsets(cnt_ref, off_ref)

    e_a, e_b = meta_ref[0:1, :], meta_ref[1:2, :]
    off_a = jnp.zeros_like(e_a)
    off_b = jnp.zeros_like(e_b)
    for e in range(N_EXPERTS):
        off_a = jnp.where(e_a == e, off_ref[e], off_a)
        off_b = jnp.where(e_b == e, off_ref[e], off_b)
    slot_ref[...] = jnp.concatenate([off_a + meta_ref[2:3, :], off_b + meta_ref[3:4, :],
                                     jnp.zeros((6, e_a.shape[1]), jnp.int32)], axis=0)


def _slots(meta, counts, *, tm=1024):
    rows = meta.shape[1]
    grid_spec = pltpu.PrefetchScalarGridSpec(
        num_scalar_prefetch=1, grid=(rows // tm,),
        in_specs=[pl.BlockSpec((8, tm), lambda i, cnt: (0, i))],
        out_specs=pl.BlockSpec((8, tm), lambda i, cnt: (0, i)),
        scratch_shapes=[pltpu.SMEM((N_EXPERTS,), jnp.int32)])
    return pl.pallas_call(
        _slots_kernel, grid_spec=grid_spec, out_shape=jax.ShapeDtypeStruct((8, rows), jnp.int32),
        compiler_params=_cparams("arbitrary"), name="moe_slots",
    )(counts, meta)


def _dispatch_kernel(*refs, steps):
    sa_ref, sb_ref, cnt_ref = refs[:3]
    h_refs = refs[3:3 + len(steps)]
    xs_ref, info_ref, off_ref, zero_ref, sem = refs[3 + len(steps):]
    i = pl.program_id(0)
    tm = h_refs[0].shape[0]
    n_tiles = info_ref.shape[0] - 1

    def tile_copy(tile):
        return pltpu.make_async_copy(zero_ref, xs_ref.at[pl.ds(tile * MOE_TILE, MOE_TILE), :], sem)

    @pl.when(i == 0)
    def _():
        zero_ref[...] = jnp.zeros_like(zero_ref)
        used = _expert_offsets(cnt_ref, off_ref)

        def per_expert(e, _):
            first = off_ref[e] >> MOE_TILE_SHIFT
            nt = (cnt_ref[e] + MOE_TILE - 1) >> MOE_TILE_SHIFT

            def fill(j, _):
                info_ref[first + j] = e
                return 0
            lax.fori_loop(0, nt, fill, 0)

            @pl.when(nt > 0)
            def _():
                tile_copy(first + nt - 1).start()
                tile_copy(first + nt - 1).wait()
            return 0
        lax.fori_loop(0, N_EXPERTS, per_expert, 0)
        info_ref[n_tiles] = used

        def tail(j, _):
            info_ref[j] = N_EXPERTS - 1
            tile_copy(j).start()
            tile_copy(j).wait()
            return 0
        lax.fori_loop(used, n_tiles, tail, 0)

    base = i * tm

    def copy_rows(h_ref):
        def row_copy(t, slot):
            return pltpu.make_async_copy(h_ref.at[pl.ds(t, 1), :], xs_ref.at[pl.ds(slot, 1), :], sem)

        for t in range(tm):
            row_copy(t, sa_ref[base + t]).start(priority=0)
            row_copy(t, sb_ref[base + t]).start(priority=1)
        for _ in range(2):
            pltpu.make_async_copy(h_ref, xs_ref.at[pl.ds(0, tm), :], sem).wait()

    first = 0
    for h_ref, n in zip(h_refs, steps):
        pl.when(jnp.logical_and(i >= first, i < first + n))(functools.partial(copy_rows, h_ref))
        first += n


def _dispatch(hs, slots, counts):
    d = hs[0].shape[1]
    tm = MOE_TOKENS
    steps = tuple(h.shape[0] // tm for h in hs)
    n_tiles = _slot_tiles(sum(h.shape[0] for h in hs))
    specs, first = [], 0
    for n in steps:
        specs.append(pl.BlockSpec((tm, d), lambda i, *_, first=first, n=n: (jnp.clip(i - first, 0, n - 1), 0)))
        first += n
    grid_spec = pltpu.PrefetchScalarGridSpec(
        num_scalar_prefetch=3, grid=(sum(steps),), in_specs=specs,
        out_specs=[pl.BlockSpec(memory_space=pl.ANY), pl.BlockSpec(memory_space=pltpu.SMEM)],
        scratch_shapes=[pltpu.SMEM((N_EXPERTS,), jnp.int32), pltpu.VMEM((MOE_TILE, d), F32),
                        pltpu.SemaphoreType.DMA(())])
    return pl.pallas_call(
        functools.partial(_dispatch_kernel, steps=steps), grid_spec=grid_spec,
        out_shape=[jax.ShapeDtypeStruct((n_tiles * MOE_TILE, d), F32),
                   jax.ShapeDtypeStruct((n_tiles + 1,), jnp.int32)],
        compiler_params=_cparams("arbitrary"), name="moe_dispatch",
    )(slots[0], slots[1], counts, *hs)


def _ffn_kernel(info_ref, xs_ref, wg_ref, wu_ref, wd_ref, ys_ref, wg_s, wu_s, wd_s):
    i = pl.program_id(0)
    used = info_ref[info_ref.shape[0] - 1]
    fresh = jnp.logical_or(i == 0, info_ref[i] != info_ref[jnp.maximum(i - 1, 0)])

    @pl.when(jnp.logical_and(i < used, fresh))
    def _():
        wg_s[...] = wg_ref[0, 0].astype(BF16)
        wu_s[...] = wu_ref[0, 0].astype(BF16)
        wd_s[...] = wd_ref[0, 0].astype(BF16)

    @pl.when(i < used)
    def _():
        x = xs_ref[...].astype(BF16)
        hid = _silu(_dot(x, wg_s[...])) * _dot(x, wu_s[...])
        ys_ref[...] = _dot(hid.astype(BF16), wd_s[...])

    @pl.when(i >= used)
    def _():
        ys_ref[...] = jnp.zeros_like(ys_ref)


def _ffn(xs, info, wg, wu, wd, layer):
    slots, d = xs.shape
    n_tiles = slots // MOE_TILE
    f = wg.shape[3]

    def w_map(i, info):
        return (layer, info[i], 0, 0)

    grid_spec = pltpu.PrefetchScalarGridSpec(
        num_scalar_prefetch=1, grid=(n_tiles,),
        in_specs=[pl.BlockSpec((MOE_TILE, d), lambda i, info: (jnp.minimum(i, info[n_tiles] - 1), 0)),
                  pl.BlockSpec((1, 1, d, f), w_map), pl.BlockSpec((1, 1, d, f), w_map),
                  pl.BlockSpec((1, 1, f, d), w_map)],
        out_specs=pl.BlockSpec((MOE_TILE, d), lambda i, info: (i, 0)),
        scratch_shapes=[pltpu.VMEM((d, f), BF16), pltpu.VMEM((d, f), BF16), pltpu.VMEM((f, d), BF16)])
    return pl.pallas_call(
        _ffn_kernel, grid_spec=grid_spec, out_shape=jax.ShapeDtypeStruct((slots, d), F32),
        compiler_params=_cparams("arbitrary"), name="moe_ffn",
    )(info, xs, wg, wu, wd)


def _combine_kernel(*refs, has_mod, out_x):
    refs = list(refs)
    sa_ref, sb_ref, x_ref, ys_ref, wcol_ref, gate_ref, g_ref = refs[:7]
    refs = refs[7:]
    mod_refs = (refs.pop(0), refs.pop(0)) if has_mod else None
    xo_ref = refs.pop(0) if out_x else None
    h_ref, buf_a, buf_b, sems = refs
    i = pl.program_id(0)
    tm = x_ref.shape[0]

    def issue(tile, slot):
        base = tile * tm
        for t in range(tm):
            pltpu.make_async_copy(ys_ref.at[pl.ds(sa_ref[base + t], 1), :], buf_a.at[slot, pl.ds(t, 1), :],
                                  sems.at[slot]).start(priority=0)
            pltpu.make_async_copy(ys_ref.at[pl.ds(sb_ref[base + t], 1), :], buf_b.at[slot, pl.ds(t, 1), :],
                                  sems.at[slot]).start(priority=1)

    @pl.when(i == 0)
    def _():
        issue(0, 0)

    @pl.when(i + 1 < pl.num_programs(0))
    def _():
        issue(i + 1, (i + 1) % 2)

    slot = i % 2
    for buf in (buf_a, buf_b):
        pltpu.make_async_copy(ys_ref.at[pl.ds(0, tm), :], buf.at[slot], sems.at[slot]).wait()
    y = wcol_ref[:, 0:1] * buf_a[slot] + wcol_ref[:, 1:2] * buf_b[slot]
    x = x_ref[...] + gate_ref[0] * y
    if out_x:
        xo_ref[...] = x
    h_ref[...] = _norm_mod(x, g_ref, mod_refs).astype(h_ref.dtype)


def _combine(x, ys, slots, wcol, gate, g, *, mod=None, out_x=False, h_dtype=BF16):
    rows, d = x.shape
    tm = MOE_TOKENS
    row_spec = pl.BlockSpec((tm, d), lambda i, *_: (i, 0))
    args = [x, ys, wcol, gate, g.reshape(1, d)]
    specs = [row_spec, pl.BlockSpec(memory_space=pl.ANY), pl.BlockSpec((tm, LANES), lambda i, *_: (i, 0)),
             _mod_spec(gate.shape[0], rows, tm, d), pl.BlockSpec((1, d), lambda i, *_: (0, 0))]
    if mod is not None:
        for m in mod:
            args.append(m)
            specs.append(_mod_spec(m.shape[0], rows, tm, d))
    out_shape, out_specs = [], []
    if out_x:
        out_shape.append(jax.ShapeDtypeStruct((rows, d), F32))
        out_specs.append(row_spec)
    out_shape.append(jax.ShapeDtypeStruct((rows, d), h_dtype))
    out_specs.append(row_spec)
    grid_spec = pltpu.PrefetchScalarGridSpec(
        num_scalar_prefetch=2, grid=(rows // tm,), in_specs=specs, out_specs=out_specs,
        scratch_shapes=[pltpu.VMEM((2, tm, d), F32), pltpu.VMEM((2, tm, d), F32), pltpu.SemaphoreType.DMA((2,))])
    outs = pl.pallas_call(
        functools.partial(_combine_kernel, has_mod=mod is not None, out_x=out_x), grid_spec=grid_spec,
        out_shape=out_shape, compiler_params=_cparams("arbitrary"), name="moe_combine",
    )(slots[0], slots[1], *args)
    return outs if out_x else outs[0]


def _softmax_av(scores, values, sink_col):
    m = sink_col
    for s in scores:
        m = jnp.maximum(m, jnp.max(s, axis=-1, keepdims=True))
    den = jnp.exp(sink_col - m)
    acc = None
    for s, v in zip(scores, values):
        p = jnp.exp(s - m)
        den = den + jnp.sum(p, axis=-1, keepdims=True)
        pv = _dot(p.astype(BF16), v)
        acc = pv if acc is None else acc + pv
    return acc / den


def _sink_column(sink_ref, kv, rows):
    return jnp.concatenate([jnp.full((rows, 1), sink_ref[kv * ATT_GROUP + g], F32) for g in range(ATT_GROUP)], axis=0)


def _attn_ctx_kernel(sink_ref, qkv_ref, o_ref):
    t = qkv_ref.shape[0]
    qw = ATT_HEADS * HEAD_DIM
    kw = ATT_KV * HEAD_DIM
    heads_out = []
    for kv in range(ATT_KV):
        q = jnp.concatenate(
            [qkv_ref[:, (kv * ATT_GROUP + g) * HEAD_DIM:(kv * ATT_GROUP + g + 1) * HEAD_DIM] for g in range(ATT_GROUP)],
            axis=0).astype(BF16)
        k = qkv_ref[:, qw + kv * HEAD_DIM:qw + (kv + 1) * HEAD_DIM].astype(BF16)
        v = qkv_ref[:, qw + kw + kv * HEAD_DIM:qw + kw + (kv + 1) * HEAD_DIM].astype(BF16)
        s = _dot_nt(q, k) * HEAD_DIM ** -0.5
        o = _softmax_av([s], [v], _sink_column(sink_ref, kv, t))
        heads_out += [o[g * t:(g + 1) * t] for g in range(ATT_GROUP)]
    o_ref[...] = jnp.concatenate(heads_out, axis=1).astype(o_ref.dtype)


def _attn_ctx(qkv, sink, n_seq, seq_len):
    rows, cols = qkv.shape
    return pl.pallas_call(
        _attn_ctx_kernel, grid=(n_seq,),
        in_specs=[pl.BlockSpec(memory_space=pltpu.SMEM), pl.BlockSpec((seq_len, cols), lambda b: (b, 0))],
        out_specs=pl.BlockSpec((seq_len, ATT_HEADS * HEAD_DIM), lambda b: (b, 0)),
        out_shape=jax.ShapeDtypeStruct((rows, ATT_HEADS * HEAD_DIM), BF16),
        compiler_params=_cparams("parallel"), name="attn_context",
    )(sink, qkv)


def _rope_block(x, cos, sin_signed):
    lane = lax.broadcasted_iota(jnp.int32, x.shape, 1)
    nf = HEAD_DIM // 4
    partner = jnp.where((lane % (2 * nf)) < nf, pltpu.roll(x, LANES - nf, axis=1), pltpu.roll(x, nf, axis=1))
    return x * cos + partner * sin_signed


def _attn_lat_kernel(sink_ref, qkv_ref, ck_ref, cv_ref, cos_ref, sin_ref, o_ref, k_scr):
    i = pl.program_id(1)
    t = qkv_ref.shape[0]
    qw = ATT_HEADS * HEAD_DIM
    kw = ATT_KV * HEAD_DIM
    span = Q_BLOCK + 2 * WINDOW

    @pl.when(i == 0)
    def _():
        for c in range(kw // LANES):
            blk = qkv_ref[:, qw + c * LANES:qw + (c + 1) * LANES]
            k_scr[:, c * LANES:(c + 1) * LANES] = _rope_block(blk, cos_ref[...], sin_ref[...]).astype(BF16)

    r0 = pl.multiple_of(i * Q_BLOCK, Q_BLOCK)
    ws = pl.multiple_of(jnp.clip(r0 - WINDOW, 0, t - span), Q_BLOCK)
    cos_q = cos_ref[pl.ds(r0, Q_BLOCK), :]
    sin_q = sin_ref[pl.ds(r0, Q_BLOCK), :]
    qpos = r0 + lax.broadcasted_iota(jnp.int32, (Q_BLOCK, span), 0)
    kpos = ws + lax.broadcasted_iota(jnp.int32, (Q_BLOCK, span), 1)
    band = jnp.abs(qpos - kpos) <= WINDOW
    band = jnp.concatenate([band] * ATT_GROUP, axis=0)
    heads_out = []
    for kv in range(ATT_KV):
        heads = []
        for g in range(ATT_GROUP):
            h = kv * ATT_GROUP + g
            c, half = divmod(h * HEAD_DIM, LANES)
            blk = _rope_block(qkv_ref[pl.ds(r0, Q_BLOCK), c * LANES:(c + 1) * LANES], cos_q, sin_q)
            heads.append(blk[:, half:half + HEAD_DIM])
        q = jnp.concatenate(heads, axis=0).astype(BF16)
        ck = ck_ref[0, :, kv * HEAD_DIM:(kv + 1) * HEAD_DIM].astype(BF16)
        cv = cv_ref[0, :, kv * HEAD_DIM:(kv + 1) * HEAD_DIM].astype(BF16)
        kwin = k_scr[pl.ds(ws, span), kv * HEAD_DIM:(kv + 1) * HEAD_DIM]
        vwin = qkv_ref[pl.ds(ws, span), qw + kw + kv * HEAD_DIM:qw + kw + (kv + 1) * HEAD_DIM].astype(BF16)
        s_ctx = _dot_nt(q, ck) * HEAD_DIM ** -0.5
        s_win = jnp.where(band, _dot_nt(q, kwin) * HEAD_DIM ** -0.5, -jnp.inf)
        o = _softmax_av([s_ctx, s_win], [cv, vwin], _sink_column(sink_ref, kv, Q_BLOCK))
        heads_out += [o[g * Q_BLOCK:(g + 1) * Q_BLOCK] for g in range(ATT_GROUP)]
    o_ref[...] = jnp.concatenate(heads_out, axis=1).astype(o_ref.dtype)


def _rope_tables(seq_len):
    pos = jnp.arange(seq_len, dtype=jnp.int32)
    row = (pos // GRID_W).astype(F32)
    col = (pos % GRID_W).astype(F32)
    nf = HEAD_DIM // 4
    inv = ROPE_BASE ** (-jnp.arange(nf, dtype=F32) / nf)
    ang_r = row[:, None] * inv[None, :]
    ang_c = col[:, None] * inv[None, :]
    cos_h = jnp.concatenate([jnp.cos(ang_r), jnp.cos(ang_r), jnp.cos(ang_c), jnp.cos(ang_c)], axis=1)
    sin_h = jnp.concatenate([-jnp.sin(ang_r), jnp.sin(ang_r), -jnp.sin(ang_c), jnp.sin(ang_c)], axis=1)
    reps = LANES // HEAD_DIM
    return jnp.tile(cos_h, (1, reps)), jnp.tile(sin_h, (1, reps))


def _attn_lat(qkv, cache_k, cache_v, sink, n_seq, seq_len):
    rows, cols = qkv.shape
    past = cache_k.shape[1]
    kw = ATT_KV * HEAD_DIM
    cos, sin = _rope_tables(seq_len)
    return pl.pallas_call(
        _attn_lat_kernel, grid=(n_seq, seq_len // Q_BLOCK),
        in_specs=[pl.BlockSpec(memory_space=pltpu.SMEM),
                  pl.BlockSpec((seq_len, cols), lambda b, i: (b, 0)),
                  pl.BlockSpec((1, past, kw), lambda b, i: (b, 0, 0)),
                  pl.BlockSpec((1, past, kw), lambda b, i: (b, 0, 0)),
                  pl.BlockSpec((seq_len, LANES), lambda b, i: (0, 0)),
                  pl.BlockSpec((seq_len, LANES), lambda b, i: (0, 0))],
        out_specs=pl.BlockSpec((Q_BLOCK, ATT_HEADS * HEAD_DIM), lambda b, i: (b * (seq_len // Q_BLOCK) + i, 0)),
        out_shape=jax.ShapeDtypeStruct((rows, ATT_HEADS * HEAD_DIM), BF16),
        scratch_shapes=[pltpu.VMEM((seq_len, kw), BF16)],
        compiler_params=_cparams("parallel", "arbitrary"), name="attn_latent",
    )(sink, qkv, cache_k.reshape(n_seq, past, kw), cache_v.reshape(n_seq, past, kw), cos, sin)


def _split_bf16(w):
    hi = w.astype(BF16)
    return hi, (w - hi.astype(F32)).astype(BF16)


def _ml_gates_kernel(h_ref, w_ref, wt_ref, b_ref, bt_ref, g_ref, gt_ref):
    h = h_ref[...]
    h_hi = h.astype(BF16)
    h_lo = (h - h_hi.astype(F32)).astype(BF16)
    ng = gt_ref.shape[0]
    by_hi = _dot(h_hi, w_ref[...])
    g_ref[...] = by_hi[:, :LANES] + by_hi[:, LANES:] + _dot(h_lo, w_ref[:, :LANES]) + b_ref[...]
    by_hi_t = _dot_nt(wt_ref[...], h_hi)
    gt_ref[...] = by_hi_t[:ng] + by_hi_t[ng:] + _dot_nt(wt_ref[:ng, :], h_lo) + bt_ref[...]


def _ml_gates(h, w_gates, b_gates, *, tm=512):
    rows, d = h.shape
    ng = w_gates.shape[1]
    w_hi, w_lo = _split_bf16(jnp.pad(w_gates, ((0, 0), (0, LANES - ng))))
    wt_hi, wt_lo = _split_bf16(w_gates.T)
    b_pad = jnp.pad(b_gates, (0, LANES - ng)).reshape(1, LANES)
    return pl.pallas_call(
        _ml_gates_kernel, grid=(rows // tm,),
        in_specs=[pl.BlockSpec((tm, d), lambda i: (i, 0)), pl.BlockSpec((d, 2 * LANES), lambda i: (0, 0)),
                  pl.BlockSpec((2 * ng, d), lambda i: (0, 0)), pl.BlockSpec((1, LANES), lambda i: (0, 0)),
                  pl.BlockSpec((ng, 1), lambda i: (0, 0))],
        out_specs=[pl.BlockSpec((tm, LANES), lambda i: (i, 0)), pl.BlockSpec((ng, tm), lambda i: (0, i))],
        out_shape=[jax.ShapeDtypeStruct((rows, LANES), F32), jax.ShapeDtypeStruct((ng, rows), F32)],
        compiler_params=_cparams("parallel"), name="mlstm_gates",
    )(h, jnp.concatenate([w_hi, w_lo], axis=1), jnp.concatenate([wt_hi, wt_lo], axis=0), b_pad,
      b_gates.reshape(ng, 1))


def _ml_qk_kernel(h_ref, w_ref, cw_ref, o_ref, *, seq_len, k_scale):
    j = pl.program_id(1)
    x = _dot(h_ref[...].astype(BF16), w_ref[...])
    t = x.shape[0]
    pos = lax.broadcasted_iota(jnp.int32, x.shape, 0) % seq_len
    prev = jnp.where(pos == 0, 0.0, pltpu.roll(x, 1, axis=0))
    nxt = jnp.where(pos == seq_len - 1, 0.0, pltpu.roll(x, t - 1, axis=0))
    y = prev * cw_ref[0:1, :] + x * cw_ref[1:2, :] + nxt * cw_ref[2:3, :]
    scale = jnp.where(j >= pl.num_programs(1) // 2, k_scale, 1.0).astype(F32)
    o_ref[...] = (_silu(y) * scale).astype(o_ref.dtype)


def _ml_qk(h, w_qk, conv_w, seq_len, *, tm=1024, tn=1024):
    rows, d = h.shape
    width = w_qk.shape[1]
    return pl.pallas_call(
        functools.partial(_ml_qk_kernel, seq_len=seq_len, k_scale=ML_DK ** -0.5), grid=(rows // tm, width // tn),
        in_specs=[pl.BlockSpec((tm, d), lambda i, j: (i, 0)), pl.BlockSpec((d, tn), lambda i, j: (0, j)),
                  pl.BlockSpec((3, tn), lambda i, j: (0, j))],
        out_specs=pl.BlockSpec((tm, tn), lambda i, j: (i, j)),
        out_shape=jax.ShapeDtypeStruct((rows, width), BF16),
        compiler_params=_cparams("parallel", "parallel"), name="mlstm_qk",
    )(h, w_qk, conv_w)


def _ml_scan_kernel(*refs, zero_init):
    refs = list(refs)
    dirs = [tuple(refs[0:5]), tuple(refs[5:10])]
    refs = refs[10:]
    if not zero_init:
        c0_ref, n0_ref, m0_ref = refs[:3]
        refs = refs[3:]
    hf_ref, hb_ref, c_ref, n_ref, m_ref = refs
    h_out = (hf_ref, hb_ref)
    c = pl.program_id(1)
    last = pl.num_programs(1) - 1

    @pl.when(c == 0)
    def _():
        if zero_init:
            c_ref[...] = jnp.zeros_like(c_ref)
            n_ref[...] = jnp.zeros_like(n_ref)
            m_ref[...] = jnp.zeros_like(m_ref)
        else:
            c_ref[...] = c0_ref[...]
            n_ref[...] = n0_ref[...]
            m_ref[...] = m0_ref[...]

    length = hf_ref.shape[0]
    ti = lax.broadcasted_iota(jnp.int32, (length, length), 0)
    si = lax.broadcasted_iota(jnp.int32, (length, length), 1)
    for d in range(2):
        q_ref, k_ref, v_ref, g_ref, gt_ref = dirs[d]
        causal = (ti >= si) if d == 0 else (ti <= si)
        tri = jnp.where(causal, 1.0, 0.0).astype(F32)
        f_col = _log_sigmoid(g_ref[...])
        f_row = _log_sigmoid(gt_ref[...])
        b_col = jnp.dot(tri, f_col, precision=HI, preferred_element_type=F32)
        b_row = _dot_nt(f_row, tri, precision=HI)
        edge = length - 1 if d == 0 else 0
        for h in range(ML_HEADS):
            ji = d * 2 * ML_HEADS + h
            jf = ji + ML_HEADS
            bc = b_col[:, jf:jf + 1]
            br = b_row[jf:jf + 1, :]
            i_row = gt_ref[ji:ji + 1, :]
            i_col = g_ref[:, ji:ji + 1]
            m_prev = m_ref[0, d, h][:, 0:1]
            q = q_ref[:, h * ML_DK:(h + 1) * ML_DK]
            k = k_ref[:, h * ML_DK:(h + 1) * ML_DK]
            v = v_ref[:, h * ML_DV:(h + 1) * ML_DV].astype(BF16)
            cst = c_ref[0, d, h]
            nst = n_ref[0, d, h]
            a_row = i_row - br
            amat = jnp.where(causal, a_row, -jnp.inf)
            u = jnp.maximum(m_prev, jnp.max(amat, axis=1, keepdims=True))
            qk = (_dot_nt(q, k) * jnp.exp(amat - u)).astype(BF16)
            sc = jnp.exp(m_prev - u)
            state_ext = jnp.concatenate([cst, jnp.broadcast_to(nst, (LANES, ML_DK))], axis=0).astype(BF16)
            v_ext = jnp.concatenate([v, jnp.ones((length, LANES), BF16)], axis=1)
            tot = sc * _dot_nt(q, state_ext) + _dot(qk, v_ext)
            inv = 1.0 / jnp.maximum(jnp.abs(tot[:, ML_DV:]), jnp.exp(-(bc + u)))
            h_out[d][:, h * ML_DV:(h + 1) * ML_DV] = tot[:, :ML_DV] * jnp.concatenate([inv] * (ML_DV // LANES), axis=1)
            b_last = br[:, edge:edge + 1]
            wlog_row = b_last + a_row
            m_new = jnp.maximum(b_last + m_prev, jnp.max(wlog_row, axis=1, keepdims=True))
            decay = jnp.exp(b_last + m_prev - m_new)
            ws_row = jnp.exp(wlog_row - m_new)
            ws_col = jnp.exp(b_last - bc + i_col - m_new)
            kw = (ws_col * k.astype(F32)).astype(BF16)
            c_ref[0, d, h] = decay * cst + _dot_tn(v, kw)
            n_ref[0, d, h] = decay * nst + _dot(jnp.broadcast_to(ws_row, (8, length)).astype(BF16), k)[0:1]
            m_ref[0, d, h] = jnp.broadcast_to(m_new, (1, ML_DK))


def _ml_scan(qk, p, g, gt, state, n_seq, seq_len):
    rows = qk.shape[0]
    length = min(ML_CHUNK, seq_len)
    nc = seq_len // length
    qw = ML_HEADS * ML_DK
    vw = ML_HEADS * ML_DV
    ng = gt.shape[0]

    def fwd(b, c):
        return b * nc + c

    def bwd(b, c):
        return b * nc + nc - 1 - c

    args, specs = [], []
    for pos in (fwd, bwd):
        args += [qk, qk, p, g, gt]
        specs += [pl.BlockSpec((length, qw), lambda b, c, pos=pos: (pos(b, c), 0)),
                  pl.BlockSpec((length, qw), lambda b, c, pos=pos: (pos(b, c), 1)),
                  pl.BlockSpec((length, vw), lambda b, c, pos=pos: (pos(b, c), 0)),
                  pl.BlockSpec((length, LANES), lambda b, c, pos=pos: (pos(b, c), 0)),
                  pl.BlockSpec((ng, length), lambda b, c, pos=pos: (0, pos(b, c)))]
    c_spec = pl.BlockSpec((1, 2, ML_HEADS, ML_DV, ML_DK), lambda b, c: (b, 0, 0, 0, 0))
    n_spec = pl.BlockSpec((1, 2, ML_HEADS, 1, ML_DK), lambda b, c: (b, 0, 0, 0, 0))
    zero_init = state is None
    if not zero_init:
        c0, n0, m0 = state
        args += [c0, n0.reshape(n_seq, 2, ML_HEADS, 1, ML_DK),
                 jnp.broadcast_to(m0[..., None, None], (n_seq, 2, ML_HEADS, 1, ML_DK))]
        specs += [c_spec, n_spec, n_spec]
    hf, hb, c_fin, n_fin, m_fin = pl.pallas_call(
        functools.partial(_ml_scan_kernel, zero_init=zero_init), grid=(n_seq, nc), in_specs=specs,
        out_specs=[pl.BlockSpec((length, vw), lambda b, c: (fwd(b, c), 0)),
                   pl.BlockSpec((length, vw), lambda b, c: (bwd(b, c), 0)), c_spec, n_spec, n_spec],
        out_shape=[jax.ShapeDtypeStruct((rows, vw), F32), jax.ShapeDtypeStruct((rows, vw), F32),
                   jax.ShapeDtypeStruct((n_seq, 2, ML_HEADS, ML_DV, ML_DK), F32),
                   jax.ShapeDtypeStruct((n_seq, 2, ML_HEADS, 1, ML_DK), F32),
                   jax.ShapeDtypeStruct((n_seq, 2, ML_HEADS, 1, ML_DK), F32)],
        compiler_params=_cparams("parallel", "arbitrary"), name="mlstm_scan",
    )(*args)
    return hf, hb, (c_fin, n_fin[:, :, :, 0, :], m_fin[:, :, :, 0, 0])


def _gla_scan_kernel(*refs, zero_init):
    refs = list(refs)
    dirs = [tuple(refs[0:4]), tuple(refs[4:8])]
    w2_ref, ba_ref = refs[8:10]
    refs = refs[10:]
    if not zero_init:
        s0_ref = refs.pop(0)
    of_ref, ob_ref, s_ref, st_scr, la_scr = refs
    o_out = (of_ref, ob_ref)
    c = pl.program_id(1)
    last = pl.num_programs(1) - 1
    kw = GLA_HEADS * GLA_DK
    n_sub = of_ref.shape[0] // GLA_SUB

    @pl.when(c == 0)
    def _():
        for d in range(2):
            for h in range(GLA_HEADS):
                st_scr[d, h] = jnp.zeros((GLA_DV, GLA_DK), F32) if zero_init else s0_ref[0, d, h].T

    for d in range(2):
        u = dirs[d][3][...].astype(BF16)
        z = _dot(u, w2_ref[:, d * kw:(d + 1) * kw]) + ba_ref[:, d * kw:(d + 1) * kw]
        la_scr[d] = _log_sigmoid(z) / GLA_TAU

    ti = lax.broadcasted_iota(jnp.int32, (GLA_SUB, GLA_SUB), 0)
    si = lax.broadcasted_iota(jnp.int32, (GLA_SUB, GLA_SUB), 1)
    s_lane = lax.broadcasted_iota(jnp.int32, (GLA_SUB, GLA_SUB), 1)

    def sub_chunk(j, carry):
        for d in range(2):
            q_ref, k_ref, v_ref, _ = dirs[d]
            r0 = pl.multiple_of((j if d == 0 else n_sub - 1 - j) * GLA_SUB, GLA_SUB)
            causal = (ti >= si) if d == 0 else (ti <= si)
            tri = jnp.where(causal, 1.0, 0.0).astype(F32)
            bc_all = jnp.dot(tri, la_scr[d, pl.ds(r0, GLA_SUB), :], precision=HI, preferred_element_type=F32)
            edge = GLA_SUB - 1 if d == 0 else 0
            for h in range(GLA_HEADS):
                bc = bc_all[:, h * GLA_DK:(h + 1) * GLA_DK]
                q = q_ref[pl.ds(r0, GLA_SUB), h * GLA_DK:(h + 1) * GLA_DK] * GLA_DK ** -0.5
                k = k_ref[pl.ds(r0, GLA_SUB), h * GLA_DK:(h + 1) * GLA_DK]
                v = v_ref[pl.ds(r0, GLA_SUB), h * GLA_DV:(h + 1) * GLA_DV].astype(BF16)
                bc2 = bc * LOG2E
                a = jnp.zeros((GLA_SUB, GLA_SUB), F32)
                for s in range(GLA_SUB):
                    decay = jnp.exp2(bc2 - bc2[s:s + 1, :])
                    col = jnp.sum(q * (k[s:s + 1, :] * decay), axis=1, keepdims=True)
                    a = jnp.where(s_lane == s, col, a)
                a = jnp.where(causal, a, 0.0)
                st = st_scr[d, h]
                o = _dot(a.astype(BF16), v) + _dot_nt((q * jnp.exp2(bc2)).astype(BF16), st.astype(BF16))
                o_out[d][pl.ds(r0, GLA_SUB), h * GLA_DV:(h + 1) * GLA_DV] = o
                b_last = bc2[edge:edge + 1, :]
                k_dec = (k * jnp.exp2(b_last - bc2)).astype(BF16)
                st_scr[d, h] = jnp.exp2(b_last) * st + _dot_tn(v, k_dec)
        return carry

    lax.fori_loop(0, n_sub, sub_chunk, 0)

    @pl.when(c == last)
    def _():
        for d in range(2):
            for h in range(GLA_HEADS):
                s_ref[0, d, h] = st_scr[d, h].T


def _gla_scan(p, u, w2, b_a, state, n_seq, seq_len):
    rows = p.shape[0]
    length = min(GLA_BLOCK, seq_len)
    nc = seq_len // length
    kw = GLA_HEADS * GLA_DK
    vw = GLA_HEADS * GLA_DV

    def fwd(b, c):
        return b * nc + c

    def bwd(b, c):
        return b * nc + nc - 1 - c

    args, specs = [], []
    for pos in (fwd, bwd):
        args += [p, p, p, u]
        specs += [pl.BlockSpec((length, kw), lambda b, c, pos=pos: (pos(b, c), 0)),
                  pl.BlockSpec((length, kw), lambda b, c, pos=pos: (pos(b, c), 1)),
                  pl.BlockSpec((length, vw), lambda b, c, pos=pos: (pos(b, c), 2 * kw // vw)),
                  pl.BlockSpec((length, LANES), lambda b, c, pos=pos: (pos(b, c), 0))]
    args += [w2, b_a]
    specs += [pl.BlockSpec(w2.shape, lambda b, c: (0, 0)), pl.BlockSpec(b_a.shape, lambda b, c: (0, 0))]
    s_spec = pl.BlockSpec((1, 2, GLA_HEADS, GLA_DK, GLA_DV), lambda b, c: (b, 0, 0, 0, 0))
    zero_init = state is None
    if not zero_init:
        args.append(state)
        specs.append(s_spec)
    return pl.pallas_call(
        functools.partial(_gla_scan_kernel, zero_init=zero_init), grid=(n_seq, nc), in_specs=specs,
        out_specs=[pl.BlockSpec((length, vw), lambda b, c: (fwd(b, c), 0)),
                   pl.BlockSpec((length, vw), lambda b, c: (bwd(b, c), 0)), s_spec],
        out_shape=[jax.ShapeDtypeStruct((rows, vw), F32), jax.ShapeDtypeStruct((rows, vw), F32),
                   jax.ShapeDtypeStruct((n_seq, 2, GLA_HEADS, GLA_DK, GLA_DV), F32)],
        scratch_shapes=[pltpu.VMEM((2, GLA_HEADS, GLA_DV, GLA_DK), F32), pltpu.VMEM((2, length, kw), F32)],
        compiler_params=_cparams("parallel", "arbitrary"), name="gla_scan",
    )(*args)


def kernel(x_prompt, x_sample, cache_k_0, cache_v_0, state_mlstm_C_1, state_mlstm_n_1, state_mlstm_m_1, state_gla_S_2, cache_k_3, cache_v_3, c, c_ctx, w_mod, b_mod, norm1_g, norm2_g, final_g, router_w, router_b, moe_wg, moe_wu, moe_wd, attn0_w_qkv, attn0_sink, attn0_w_o, mlstm1_w_in, mlstm1_b_gates, mlstm1_conv, mlstm1_norm_g, mlstm1_w_out, gla2_w_in, gla2_w_a1, gla2_w_a2, gla2_b_a, gla2_norm_g, gla2_w_out, attn3_w_qkv, attn3_sink, attn3_w_o):
    n_ctx, ctx_len, d = x_prompt.shape
    n_lat, lat_len, _ = x_sample.shape
    depth = w_mod.shape[0]

    cvec = jnp.concatenate([c_ctx[None, :], c, jnp.zeros((8 - 1 - n_lat, d), F32)], axis=0)
    mod = _modulation(cvec, w_mod, b_mod).reshape(depth, 8, 6, 1, d)

    def mods(layer, kind, latent):
        return mod[layer, 1:1 + n_lat, kind] if latent else mod[layer, 0:1, kind]

    rw_hi = router_w.T.astype(BF16)
    rw_lo = (router_w.T - rw_hi.astype(F32)).astype(BF16)
    rwt = jnp.concatenate([rw_hi, rw_lo], axis=0)
    rb = router_b.reshape(-1, 1)
    attn_w = {0: (attn0_w_qkv.astype(BF16), attn0_sink, attn0_w_o.astype(BF16), cache_k_0, cache_v_0),
              3: (attn3_w_qkv.astype(BF16), attn3_sink, attn3_w_o.astype(BF16), cache_k_3, cache_v_3)}
    ml_qw = ML_HEADS * ML_DK
    ml_vw = ML_HEADS * ML_DV
    ml_main = 2 * ml_qw + 2 * ml_vw
    ml_w_qk = mlstm1_w_in[:, :2 * ml_qw].astype(BF16)
    ml_w_vo = mlstm1_w_in[:, 2 * ml_qw:ml_main].astype(BF16)
    ml_w_gates = mlstm1_w_in[:, ml_main:]
    ml_w_out = mlstm1_w_out.astype(BF16)
    gla_kw = GLA_HEADS * GLA_DK
    gla_w_in = gla2_w_in.astype(BF16)
    gla_w_a1 = jnp.pad(jnp.concatenate([gla2_w_a1[0], gla2_w_a1[1]], axis=1),
                       ((0, 0), (0, LANES - 2 * GLA_RANK))).astype(BF16)
    gla_w2 = jnp.zeros((LANES, 2 * gla_kw), F32)
    gla_w2 = gla_w2.at[:GLA_RANK, :gla_kw].set(gla2_w_a2[0]).at[GLA_RANK:2 * GLA_RANK, gla_kw:].set(gla2_w_a2[1])
    gla_w2 = gla_w2.astype(BF16)
    gla_ba = gla2_b_a.reshape(1, 2 * gla_kw)
    gla_w_out = gla2_w_out.astype(BF16)

    new_state = []

    def mixer(layer, s, count0):
        latent, n_seq, seq_len, x, h = s["latent"], s["n_seq"], s["seq_len"], s["x"], s["h"]
        tail = (mods(layer, 2, latent), norm2_g[layer], mods(layer, 4, latent), mods(layer, 3, latent), rwt, rb,
                count0)
        kind = layer % 3
        if kind == 0:
            w_qkv, sink, w_o, ck, cv = attn_w[layer]
            qkv = _matmul(h, w_qkv)
            if latent:
                att = _attn_lat(qkv, ck, cv, sink, n_seq, seq_len)
            else:
                att = _attn_ctx(qkv, sink, n_seq, seq_len)
                qw = ATT_HEADS * HEAD_DIM
                kw = ATT_KV * HEAD_DIM
                new_state.append(qkv[:, qw:qw + kw].reshape(n_seq, seq_len, ATT_KV, HEAD_DIM))
                new_state.append(qkv[:, qw + kw:].reshape(n_seq, seq_len, ATT_KV, HEAD_DIM))
            return _proj("plain", (att,), w_o, x, *tail)
        if kind == 1:
            p = _matmul(h, ml_w_vo)
            g, gt = _ml_gates(h, ml_w_gates, mlstm1_b_gates)
            qk = _ml_qk(h, ml_w_qk, mlstm1_conv, seq_len)
            st = (state_mlstm_C_1, state_mlstm_n_1, state_mlstm_m_1) if latent else None
            hf, hb, fin = _ml_scan(qk, p, g, gt, st, n_seq, seq_len)
            if not latent:
                new_state.extend(fin)
            return _proj("mlstm", (hf, hb, p, 1, mlstm1_norm_g), ml_w_out, x, *tail)
        p = _matmul(h, gla_w_in)
        u = _matmul(h, gla_w_a1)
        of, ob, s_fin = _gla_scan(p, u, gla_w2, gla_ba, state_gla_S_2 if latent else None, n_seq, seq_len)
        if not latent:
            new_state.append(s_fin)
        gla_vw = GLA_HEADS * GLA_DV
        return _proj("gla", (of, ob, p, (2 * gla_kw + gla_vw) // gla_vw, gla2_norm_g), gla_w_out, x, *tail)

    streams = [dict(latent=False, n_seq=n_ctx, seq_len=ctx_len, x=x_prompt.reshape(n_ctx * ctx_len, d)),
               dict(latent=True, n_seq=n_lat, seq_len=lat_len, x=x_sample.reshape(n_lat * lat_len, d))]
    for s in streams:
        s["h"] = _rownorm(s["x"], norm1_g[0], mods(0, 1, s["latent"]), mods(0, 0, s["latent"]))
    for layer in range(depth):
        counts = jnp.zeros((N_EXPERTS, LANES), jnp.int32)
        for s in streams:
            s["x"], s["h2"], s["meta"], s["wcol"], counts = mixer(layer, s, counts)
        cnt = counts[:, 0]
        for s in streams:
            s["slots"] = _slots(s["meta"], cnt)
        xs, info = _dispatch([s["h2"] for s in streams], jnp.concatenate([s["slots"] for s in streams], axis=1), cnt)
        ys = _ffn(xs, info, moe_wg, moe_wu, moe_wd, layer)
        for s in streams:
            latent = s["latent"]
            gate2 = mods(layer, 5, latent)
            if layer + 1 < depth:
                s["x"], s["h"] = _combine(s["x"], ys, s["slots"], s["wcol"], gate2, norm1_g[layer + 1],
                                          mod=(mods(layer + 1, 1, latent), mods(layer + 1, 0, latent)), out_x=True,
                                          h_dtype=F32 if (layer + 1) % 3 == 1 else BF16)
            else:
                s["out"] = _combine(s["x"], ys, s["slots"], s["wcol"], gate2, final_g, h_dtype=F32)
    y_prompt = streams[0]["out"].reshape(n_ctx, ctx_len, d)
    y_sample = streams[1]["out"].reshape(n_lat, lat_len, d)
    return (y_prompt, y_sample, *new_state)
```

```python
import functools

import jax
import jax.numpy as jnp
from jax import lax
from jax.experimental import pallas as pl
from jax.experimental.pallas import tpu as pltpu

F32 = jnp.float32
BF16 = jnp.bfloat16
HI = lax.Precision.HIGHEST

EPS = 1e-6
LOG2E = 1.4426950408889634
GRID_W = 64
ATT_HEADS = 16
ATT_KV = 4
ATT_GROUP = ATT_HEADS // ATT_KV
HEAD_DIM = 64
WINDOW = 128
Q_BLOCK = 128
ROPE_BASE = 10000.0
ML_HEADS = 8
ML_DK = 128
ML_DV = 256
ML_CHUNK = 128
GLA_HEADS = 4
GLA_DK = 128
GLA_DV = 256
GLA_RANK = 16
GLA_TAU = 16.0
GLA_SUB = 16
GLA_BLOCK = 256
N_EXPERTS = 16
N_GROUPS = 4
GROUP_SIZE = N_EXPERTS // N_GROUPS
LANES = 128
VMEM_LIMIT = 56 * 1024 * 1024


def _cparams(*sem):
    return pltpu.CompilerParams(dimension_semantics=sem, vmem_limit_bytes=VMEM_LIMIT)


def _dot(a, b):
    return jnp.dot(a, b, preferred_element_type=F32)


def _dot_nt(a, b, precision=None):
    return lax.dot_general(a, b, (((1,), (1,)), ((), ())), precision=precision, preferred_element_type=F32)


def _dot_tn(a, b):
    return lax.dot_general(a, b, (((0,), (0,)), ((), ())), preferred_element_type=F32)


def _sigmoid(x):
    return 1.0 / (1.0 + jnp.exp(-x))


def _silu(x):
    return x * _sigmoid(x)


def _log_sigmoid(x):
    return jnp.minimum(x, 0.0) - jnp.log(1.0 + jnp.exp(-jnp.abs(x)))


def _rms_rows(x, g):
    ms = jnp.mean(x * x, axis=-1, keepdims=True)
    return x * lax.rsqrt(ms + EPS) * g


def _mod_kernel(c_ref, w_ref, b_ref, o_ref):
    s = _silu(c_ref[...])
    o_ref[0] = _dot(s.astype(BF16), w_ref[0].astype(BF16)) + b_ref[0]


def _modulation(cvec, w_mod, b_mod):
    depth, d, n6 = w_mod.shape
    tn = 1536
    return pl.pallas_call(
        _mod_kernel,
        grid=(depth, n6 // tn),
        in_specs=[pl.BlockSpec((8, d), lambda l, j: (0, 0)),
                  pl.BlockSpec((1, d, tn), lambda l, j: (l, 0, j)),
                  pl.BlockSpec((1, 1, tn), lambda l, j: (l, 0, j))],
        out_specs=pl.BlockSpec((1, 8, tn), lambda l, j: (l, 0, j)),
        out_shape=jax.ShapeDtypeStruct((depth, 8, n6), F32),
        compiler_params=_cparams("parallel", "parallel"),
        name="adaln_modulation",
    )(cvec, w_mod, b_mod.reshape(depth, 1, n6))


def _route(h, rwt, rb, carry):
    tm = h.shape[0]
    h_hi = h.astype(BF16)
    h_lo = (h - h_hi.astype(F32)).astype(BF16)
    by_hi = _dot_nt(rwt, h_hi)
    logits = by_hi[:N_EXPERTS] + by_hi[N_EXPERTS:] + _dot_nt(rwt[:N_EXPERTS], h_lo)
    scores = _sigmoid(logits)
    sel = scores + rb
    expert = lax.broadcasted_iota(jnp.int32, sel.shape, 0)
    pos = expert % GROUP_SIZE
    grp = expert // GROUP_SIZE

    def mate(x, k):
        ahead = pltpu.roll(x, N_EXPERTS - k, axis=0)
        behind = pltpu.roll(x, GROUP_SIZE - k, axis=0)
        return jnp.where(pos + k < GROUP_SIZE, ahead, behind)

    beaten = jnp.zeros_like(sel)
    for k in range(1, GROUP_SIZE):
        other = mate(sel, k)
        other_first = (pos + k) % GROUP_SIZE < pos
        beaten = beaten + jnp.where(other_first, jnp.where(other >= sel, 1.0, 0.0), jnp.where(other > sel, 1.0, 0.0))
    top2 = jnp.where(beaten < 2.0, sel, 0.0)
    gscore = top2
    for k in range(1, GROUP_SIZE):
        gscore = gscore + mate(top2, k)
    lost = jnp.zeros_like(sel)
    for k in range(1, N_GROUPS):
        other = pltpu.roll(gscore, N_EXPERTS - GROUP_SIZE * k, axis=0)
        other_first = (grp + k) % N_GROUPS < grp
        lost = lost + jnp.where(other_first, jnp.where(other >= gscore, 1.0, 0.0),
                                jnp.where(other > gscore, 1.0, 0.0))
    picked = jnp.where(lost < 0.5, jnp.where(beaten < 2.0, 1.0, 0.0), 0.0)
    chosen = picked > 0.5
    weight = jnp.where(chosen, scores, 0.0)
    wsum = jnp.sum(weight, axis=0, keepdims=True)
    e_f = expert.astype(F32)
    e_a = jnp.min(jnp.where(chosen, e_f, float(N_EXPERTS)), axis=0, keepdims=True)
    e_b = jnp.max(jnp.where(chosen, e_f, -1.0), axis=0, keepdims=True)
    before = (lax.broadcasted_iota(jnp.int32, (tm, tm), 0) < lax.broadcasted_iota(jnp.int32, (tm, tm), 1))
    rank = _dot(picked.astype(BF16), jnp.where(before, 1.0, 0.0).astype(BF16)) + carry
    is_a = e_f == e_a
    is_b = e_f == e_b
    r_a = jnp.sum(jnp.where(is_a, rank, 0.0), axis=0, keepdims=True)
    r_b = jnp.sum(jnp.where(is_b, rank, 0.0), axis=0, keepdims=True)
    w_a = jnp.sum(jnp.where(is_a, weight, 0.0), axis=0, keepdims=True)
    w_b = jnp.sum(jnp.where(is_b, weight, 0.0), axis=0, keepdims=True)
    meta = jnp.concatenate([e_a, e_b, r_a, r_b, jnp.zeros((4, tm), F32)], axis=0).astype(jnp.int32)
    wcol = jnp.concatenate([w_a / wsum, w_b / wsum, jnp.zeros((LANES - 2, tm), F32)], axis=0).T
    return meta, wcol, carry + jnp.sum(picked, axis=1, keepdims=True)


def _norm_mod(x, g_ref, mod_refs):
    h = _rms_rows(x, g_ref[...])
    if mod_refs is not None:
        a_ref, s_ref = mod_refs
        h = h * (1.0 + a_ref[0]) + s_ref[0]
    return h


def _rownorm_kernel(x_ref, g_ref, a_ref, s_ref, h_ref):
    h_ref[...] = _norm_mod(x_ref[...], g_ref, (a_ref, s_ref)).astype(h_ref.dtype)


def _mod_spec(n_mod, rows, tm, d, n_prefetch=0):
    per = (rows // n_mod) // tm
    return pl.BlockSpec((1, 1, d), lambda i, *_: (i // per, 0, 0))


def _rownorm(x, g, scale, shift, *, tm=512):
    rows, d = x.shape
    row_spec = pl.BlockSpec((tm, d), lambda i: (i, 0))
    return pl.pallas_call(
        _rownorm_kernel, grid=(rows // tm,),
        in_specs=[row_spec, pl.BlockSpec((1, d), lambda i: (0, 0)), _mod_spec(scale.shape[0], rows, tm, d),
                  _mod_spec(shift.shape[0], rows, tm, d)],
        out_specs=row_spec, out_shape=jax.ShapeDtypeStruct((rows, d), BF16),
        compiler_params=_cparams("parallel"), name="rownorm",
    )(x, g.reshape(1, d), scale, shift)


def _mm_kernel(a_ref, w_ref, o_ref):
    o_ref[...] = _dot(a_ref[...].astype(BF16), w_ref[...]).astype(o_ref.dtype)


def _matmul(a, w, *, out_dtype=F32, tm=1024):
    m, k = a.shape
    n = w.shape[1]
    tn = next(t for t in (1024, 768, 512, LANES) if n % t == 0)
    return pl.pallas_call(
        _mm_kernel, grid=(m // tm, n // tn),
        in_specs=[pl.BlockSpec((tm, k), lambda i, j: (i, 0)), pl.BlockSpec((k, tn), lambda i, j: (0, j))],
        out_specs=pl.BlockSpec((tm, tn), lambda i, j: (i, j)),
        out_shape=jax.ShapeDtypeStruct((m, n), out_dtype),
        compiler_params=_cparams("parallel", "parallel"), name="matmul",
    )(a, w)


def _head_norm(x, g, n_heads, dv):
    outs = []
    for h in range(n_heads):
        xs = x[:, h * dv:(h + 1) * dv]
        ms = jnp.mean(xs * xs, axis=-1, keepdims=True)
        outs.append(xs * lax.rsqrt(ms + EPS) * g[:, h * dv:(h + 1) * dv])
    return jnp.concatenate(outs, axis=1)


def _proj_kernel(*refs, pre):
    refs = list(refs)
    if pre == "plain":
        a = refs.pop(0)[...]
    else:
        f_ref, b_ref, p_ref, hg_ref = refs.pop(0), refs.pop(0), refs.pop(0), refs.pop(0)
        hsum = f_ref[...] + b_ref[...]
        if pre == "mlstm":
            a = _sigmoid(p_ref[...]) * _head_norm(hsum, hg_ref[...], ML_HEADS, ML_DV)
        else:
            a = _head_norm(hsum, hg_ref[...], GLA_HEADS, GLA_DV) * _silu(p_ref[...])
        a = a.astype(BF16)
    w_ref, x_ref, gate_ref, g_ref, a_ref, s_ref, rwt_ref, rb_ref, count0_ref = refs[:9]
    xo_ref, h_ref, meta_ref, wcol_ref, count_ref, carry_ref = refs[9:]

    @pl.when(pl.program_id(0) == 0)
    def _():
        carry_ref[...] = count0_ref[...].astype(F32)

    x = x_ref[...] + gate_ref[0] * _dot(a, w_ref[...])
    xo_ref[...] = x
    h = _norm_mod(x, g_ref, (a_ref, s_ref))
    h_ref[...] = h
    meta, wcol, carry = _route(h, rwt_ref[...], rb_ref[...], carry_ref[:, 0:1])
    meta_ref[...] = meta
    wcol_ref[...] = wcol
    carry_ref[...] = jnp.broadcast_to(carry, carry_ref.shape)
    count_ref[...] = jnp.broadcast_to(carry, count_ref.shape).astype(jnp.int32)


def _proj(pre, pre_args, w_out, x, gate, g, scale, shift, rwt, rb, count0, *, tm=512):
    rows, d = x.shape
    k = w_out.shape[0]
    row_spec = pl.BlockSpec((tm, d), lambda i: (i, 0))
    if pre == "plain":
        args, specs = [pre_args[0]], [pl.BlockSpec((tm, k), lambda i: (i, 0))]
    else:
        hf, hb, p, col_block, hg = pre_args
        wide = pl.BlockSpec((tm, k), lambda i: (i, 0))
        args = [hf, hb, p, hg.reshape(1, k)]
        specs = [wide, wide, pl.BlockSpec((tm, k), lambda i: (i, col_block)), pl.BlockSpec((1, k), lambda i: (0, 0))]
    args += [w_out, x, gate, g.reshape(1, d), scale, shift, rwt, rb, count0]
    specs += [pl.BlockSpec((k, d), lambda i: (0, 0)), row_spec, _mod_spec(gate.shape[0], rows, tm, d),
              pl.BlockSpec((1, d), lambda i: (0, 0)), _mod_spec(scale.shape[0], rows, tm, d),
              _mod_spec(shift.shape[0], rows, tm, d), pl.BlockSpec(rwt.shape, lambda i: (0, 0)),
              pl.BlockSpec(rb.shape, lambda i: (0, 0)), pl.BlockSpec(count0.shape, lambda i: (0, 0))]
    return pl.pallas_call(
        functools.partial(_proj_kernel, pre=pre), grid=(rows // tm,), in_specs=specs,
        out_specs=[row_spec, row_spec, pl.BlockSpec((8, tm), lambda i: (0, i)),
                   pl.BlockSpec((tm, LANES), lambda i: (i, 0)), pl.BlockSpec((N_EXPERTS, LANES), lambda i: (0, 0))],
        out_shape=[jax.ShapeDtypeStruct((rows, d), F32), jax.ShapeDtypeStruct((rows, d), F32),
                   jax.ShapeDtypeStruct((8, rows), jnp.int32), jax.ShapeDtypeStruct((rows, LANES), F32),
                   jax.ShapeDtypeStruct((N_EXPERTS, LANES), jnp.int32)],
        scratch_shapes=[pltpu.VMEM((N_EXPERTS, LANES), F32)],
        compiler_params=_cparams("arbitrary"), name="proj_" + pre,
    )(*args)


MOE_TILE = 512
MOE_TILE_SHIFT = 9
MOE_TOKENS = 256
ROW_UNROLL = 8


def _slot_tiles(rows):
    return (2 * rows) // MOE_TILE + N_EXPERTS


def _expert_offsets(cnt_ref, off_ref):
    def per_expert(e, k):
        off_ref[e] = k * MOE_TILE
        return k + ((cnt_ref[e] + MOE_TILE - 1) >> MOE_TILE_SHIFT)
    return lax.fori_loop(0, N_EXPERTS, per_expert, 0)


def _slots_kernel(cnt_ref, meta_ref, slot_ref, off_ref):
    @pl.when(pl.program_id(0) == 0)
    def _():
        _expert_offsets(cnt_ref, off_ref)

    e_a, e_b = meta_ref[0:1, :], meta_ref[1:2, :]
    off_a = jnp.zeros_like(e_a)
    off_b = jnp.zeros_like(e_b)
    for e in range(N_EXPERTS):
        off_a = jnp.where(e_a == e, off_ref[e], off_a)
        off_b = jnp.where(e_b == e, off_ref[e], off_b)
    slot_ref[...] = jnp.concatenate([off_a + meta_ref[2:3, :], off_b + meta_ref[3:4, :],
                                     jnp.zeros((6, e_a.shape[1]), jnp.int32)], axis=0)


def _slots(meta, counts, *, tm=1024):
    rows = meta.shape[1]
    grid_spec = pltpu.PrefetchScalarGridSpec(
        num_scalar_prefetch=1, grid=(rows // tm,),
        in_specs=[pl.BlockSpec((8, tm), lambda i, cnt: (0, i))],
        out_specs=pl.BlockSpec((8, tm), lambda i, cnt: (0, i)),
        scratch_shapes=[pltpu.SMEM((N_EXPERTS,), jnp.int32)])
    return pl.pallas_call(
        _slots_kernel, grid_spec=grid_spec, out_shape=jax.ShapeDtypeStruct((8, rows), jnp.int32),
        compiler_params=_cparams("arbitrary"), name="moe_slots",
    )(counts, meta)


def _dispatch_kernel(*refs, steps):
    sa_ref, sb_ref, cnt_ref = refs[:3]
    h_refs = refs[3:3 + len(steps)]
    xs_ref, info_ref, off_ref, zero_ref, sem = refs[3 + len(steps):]
    i = pl.program_id(0)
    tm = h_refs[0].shape[0]
    n_tiles = info_ref.shape[0] - 1

    def tile_copy(tile):
        return pltpu.make_async_copy(zero_ref, xs_ref.at[pl.ds(tile * MOE_TILE, MOE_TILE), :], sem)

    @pl.when(i == 0)
    def _():
        zero_ref[...] = jnp.zeros_like(zero_ref)
        used = _expert_offsets(cnt_ref, off_ref)

        def per_expert(e, _):
            first = off_ref[e] >> MOE_TILE_SHIFT
            nt = (cnt_ref[e] + MOE_TILE - 1) >> MOE_TILE_SHIFT

            def fill(j, _):
                info_ref[first + j] = e
                return 0
            lax.fori_loop(0, nt, fill, 0)

            @pl.when(nt > 0)
            def _():
                tile_copy(first + nt - 1).start()
                tile_copy(first + nt - 1).wait()
            return 0
        lax.fori_loop(0, N_EXPERTS, per_expert, 0)
        info_ref[n_tiles] = used

        def tail(j, _):
            info_ref[j] = N_EXPERTS - 1
            tile_copy(j).start()
            tile_copy(j).wait()
            return 0
        lax.fori_loop(used, n_tiles, tail, 0)

    base = i * tm

    def copy_rows(h_ref):
        def row_copy(t, slot):
            return pltpu.make_async_copy(h_ref.at[pl.ds(t, 1), :], xs_ref.at[pl.ds(slot, 1), :], sem)

        for t in range(tm):
            row_copy(t, sa_ref[base + t]).start(priority=0)
            row_copy(t, sb_ref[base + t]).start(priority=1)
        for _ in range(2):
            pltpu.make_async_copy(h_ref, xs_ref.at[pl.ds(0, tm), :], sem).wait()

    first = 0
    for h_ref, n in zip(h_refs, steps):
        pl.when(jnp.logical_and(i >= first, i < first + n))(functools.partial(copy_rows, h_ref))
        first += n


def _dispatch(hs, slots, counts):
    d = hs[0].shape[1]
    tm = MOE_TOKENS
    steps = tuple(h.shape[0] // tm for h in hs)
    n_tiles = _slot_tiles(sum(h.shape[0] for h in hs))
    specs, first = [], 0
    for n in steps:
        specs.append(pl.BlockSpec((tm, d), lambda i, *_, first=first, n=n: (jnp.clip(i - first, 0, n - 1), 0)))
        first += n
    grid_spec = pltpu.PrefetchScalarGridSpec(
        num_scalar_prefetch=3, grid=(sum(steps),), in_specs=specs,
        out_specs=[pl.BlockSpec(memory_space=pl.ANY), pl.BlockSpec(memory_space=pltpu.SMEM)],
        scratch_shapes=[pltpu.SMEM((N_EXPERTS,), jnp.int32), pltpu.VMEM((MOE_TILE, d), F32),
                        pltpu.SemaphoreType.DMA(())])
    return pl.pallas_call(
        functools.partial(_dispatch_kernel, steps=steps), grid_spec=grid_spec,
        out_shape=[jax.ShapeDtypeStruct((n_tiles * MOE_TILE, d), F32),
                   jax.ShapeDtypeStruct((n_tiles + 1,), jnp.int32)],
        compiler_params=_cparams("arbitrary"), name="moe_dispatch",
    )(slots[0], slots[1], counts, *hs)


def _ffn_kernel(info_ref, xs_ref, wg_ref, wu_ref, wd_ref, ys_ref, wg_s, wu_s, wd_s):
    i = pl.program_id(0)
    used = info_ref[info_ref.shape[0] - 1]
    fresh = jnp.logical_or(i == 0, info_ref[i] != info_ref[jnp.maximum(i - 1, 0)])

    @pl.when(jnp.logical_and(i < used, fresh))
    def _():
        wg_s[...] = wg_ref[0, 0].astype(BF16)
        wu_s[...] = wu_ref[0, 0].astype(BF16)
        wd_s[...] = wd_ref[0, 0].astype(BF16)

    @pl.when(i < used)
    def _():
        x = xs_ref[...].astype(BF16)
        hid = _silu(_dot(x, wg_s[...])) * _dot(x, wu_s[...])
        ys_ref[...] = _dot(hid.astype(BF16), wd_s[...])

    @pl.when(i >= used)
    def _():
        ys_ref[...] = jnp.zeros_like(ys_ref)


def _ffn(xs, info, wg, wu, wd, layer):
    slots, d = xs.shape
    n_tiles = slots // MOE_TILE
    f = wg.shape[3]

    def w_map(i, info):
        return (layer, info[i], 0, 0)

    grid_spec = pltpu.PrefetchScalarGridSpec(
        num_scalar_prefetch=1, grid=(n_tiles,),
        in_specs=[pl.BlockSpec((MOE_TILE, d), lambda i, info: (jnp.minimum(i, info[n_tiles] - 1), 0)),
                  pl.BlockSpec((1, 1, d, f), w_map), pl.BlockSpec((1, 1, d, f), w_map),
                  pl.BlockSpec((1, 1, f, d), w_map)],
        out_specs=pl.BlockSpec((MOE_TILE, d), lambda i, info: (i, 0)),
        scratch_shapes=[pltpu.VMEM((d, f), BF16), pltpu.VMEM((d, f), BF16), pltpu.VMEM((f, d), BF16)])
    return pl.pallas_call(
        _ffn_kernel, grid_spec=grid_spec, out_shape=jax.ShapeDtypeStruct((slots, d), F32),
        compiler_params=_cparams("arbitrary"), name="moe_ffn",
    )(info, xs, wg, wu, wd)


def _combine_kernel(*refs, has_mod, out_x):
    refs = list(refs)
    sa_ref, sb_ref, x_ref, ys_ref, wcol_ref, gate_ref, g_ref = refs[:7]
    refs = refs[7:]
    mod_refs = (refs.pop(0), refs.pop(0)) if has_mod else None
    xo_ref = refs.pop(0) if out_x else None
    h_ref, buf_a, buf_b, sems = refs
    i = pl.program_id(0)
    tm = x_ref.shape[0]

    def issue(tile, slot):
        base = tile * tm
        for t in range(tm):
            pltpu.make_async_copy(ys_ref.at[pl.ds(sa_ref[base + t], 1), :], buf_a.at[slot, pl.ds(t, 1), :],
                                  sems.at[slot]).start(priority=0)
            pltpu.make_async_copy(ys_ref.at[pl.ds(sb_ref[base + t], 1), :], buf_b.at[slot, pl.ds(t, 1), :],
                                  sems.at[slot]).start(priority=1)

    @pl.when(i == 0)
    def _():
        issue(0, 0)

    @pl.when(i + 1 < pl.num_programs(0))
    def _():
        issue(i + 1, (i + 1) % 2)

    slot = i % 2
    for buf in (buf_a, buf_b):
        pltpu.make_async_copy(ys_ref.at[pl.ds(0, tm), :], buf.at[slot], sems.at[slot]).wait()
    y = wcol_ref[:, 0:1] * buf_a[slot] + wcol_ref[:, 1:2] * buf_b[slot]
    x = x_ref[...] + gate_ref[0] * y
    if out_x:
        xo_ref[...] = x
    h_ref[...] = _norm_mod(x, g_ref, mod_refs).astype(h_ref.dtype)


def _combine(x, ys, slots, wcol, gate, g, *, mod=None, out_x=False, h_dtype=BF16):
    rows, d = x.shape
    tm = MOE_TOKENS
    row_spec = pl.BlockSpec((tm, d), lambda i, *_: (i, 0))
    args = [x, ys, wcol, gate, g.reshape(1, d)]
    specs = [row_spec, pl.BlockSpec(memory_space=pl.ANY), pl.BlockSpec((tm, LANES), lambda i, *_: (i, 0)),
             _mod_spec(gate.shape[0], rows, tm, d), pl.BlockSpec((1, d), lambda i, *_: (0, 0))]
    if mod is not None:
        for m in mod:
            args.append(m)
            specs.append(_mod_spec(m.shape[0], rows, tm, d))
    out_shape, out_specs = [], []
    if out_x:
        out_shape.append(jax.ShapeDtypeStruct((rows, d), F32))
        out_specs.append(row_spec)
    out_shape.append(jax.ShapeDtypeStruct((rows, d), h_dtype))
    out_specs.append(row_spec)
    grid_spec = pltpu.PrefetchScalarGridSpec(
        num_scalar_prefetch=2, grid=(rows // tm,), in_specs=specs, out_specs=out_specs,
        scratch_shapes=[pltpu.VMEM((2, tm, d), F32), pltpu.VMEM((2, tm, d), F32), pltpu.SemaphoreType.DMA((2,))])
    outs = pl.pallas_call(
        functools.partial(_combine_kernel, has_mod=mod is not None, out_x=out_x), grid_spec=grid_spec,
        out_shape=out_shape, compiler_params=_cparams("arbitrary"), name="moe_combine",
    )(slots[0], slots[1], *args)
    return outs if out_x else outs[0]


def _softmax_av(scores, values, sink_col):
    m = sink_col
    for s in scores:
        m = jnp.maximum(m, jnp.max(s, axis=-1, keepdims=True))
    den = jnp.exp(sink_col - m)
    acc = None
    for s, v in zip(scores, values):
        p = jnp.exp(s - m)
        den = den + jnp.sum(p, axis=-1, keepdims=True)
        pv = _dot(p.astype(BF16), v)
        acc = pv if acc is None else acc + pv
    return acc / den


def _sink_column(sink_ref, kv, rows):
    return jnp.concatenate([jnp.full((rows, 1), sink_ref[kv * ATT_GROUP + g], F32) for g in range(ATT_GROUP)], axis=0)


def _attn_ctx_kernel(sink_ref, qkv_ref, o_ref):
    t = qkv_ref.shape[0]
    qw = ATT_HEADS * HEAD_DIM
    kw = ATT_KV * HEAD_DIM
    heads_out = []
    for kv in range(ATT_KV):
        q = jnp.concatenate(
            [qkv_ref[:, (kv * ATT_GROUP + g) * HEAD_DIM:(kv * ATT_GROUP + g + 1) * HEAD_DIM] for g in range(ATT_GROUP)],
            axis=0).astype(BF16)
        k = qkv_ref[:, qw + kv * HEAD_DIM:qw + (kv + 1) * HEAD_DIM].astype(BF16)
        v = qkv_ref[:, qw + kw + kv * HEAD_DIM:qw + kw + (kv + 1) * HEAD_DIM].astype(BF16)
        s = _dot_nt(q, k) * HEAD_DIM ** -0.5
        o = _softmax_av([s], [v], _sink_column(sink_ref, kv, t))
        heads_out += [o[g * t:(g + 1) * t] for g in range(ATT_GROUP)]
    o_ref[...] = jnp.concatenate(heads_out, axis=1).astype(o_ref.dtype)


def _attn_ctx(qkv, sink, n_seq, seq_len):
    rows, cols = qkv.shape
    return pl.pallas_call(
        _attn_ctx_kernel, grid=(n_seq,),
        in_specs=[pl.BlockSpec(memory_space=pltpu.SMEM), pl.BlockSpec((seq_len, cols), lambda b: (b, 0))],
        out_specs=pl.BlockSpec((seq_len, ATT_HEADS * HEAD_DIM), lambda b: (b, 0)),
        out_shape=jax.ShapeDtypeStruct((rows, ATT_HEADS * HEAD_DIM), BF16),
        compiler_params=_cparams("parallel"), name="attn_context",
    )(sink, qkv)


def _rope_block(x, cos, sin_signed):
    lane = lax.broadcasted_iota(jnp.int32, x.shape, 1)
    nf = HEAD_DIM // 4
    partner = jnp.where((lane % (2 * nf)) < nf, pltpu.roll(x, LANES - nf, axis=1), pltpu.roll(x, nf, axis=1))
    return x * cos + partner * sin_signed


def _attn_lat_kernel(sink_ref, qkv_ref, ck_ref, cv_ref, cos_ref, sin_ref, o_ref, k_scr):
    i = pl.program_id(1)
    t = qkv_ref.shape[0]
    qw = ATT_HEADS * HEAD_DIM
    kw = ATT_KV * HEAD_DIM
    span = Q_BLOCK + 2 * WINDOW

    @pl.when(i == 0)
    def _():
        for c in range(kw // LANES):
            blk = qkv_ref[:, qw + c * LANES:qw + (c + 1) * LANES]
            k_scr[:, c * LANES:(c + 1) * LANES] = _rope_block(blk, cos_ref[...], sin_ref[...]).astype(BF16)

    r0 = pl.multiple_of(i * Q_BLOCK, Q_BLOCK)
    ws = pl.multiple_of(jnp.clip(r0 - WINDOW, 0, t - span), Q_BLOCK)
    cos_q = cos_ref[pl.ds(r0, Q_BLOCK), :]
    sin_q = sin_ref[pl.ds(r0, Q_BLOCK), :]
    qpos = r0 + lax.broadcasted_iota(jnp.int32, (Q_BLOCK, span), 0)
    kpos = ws + lax.broadcasted_iota(jnp.int32, (Q_BLOCK, span), 1)
    band = jnp.abs(qpos - kpos) <= WINDOW
    band = jnp.concatenate([band] * ATT_GROUP, axis=0)
    heads_out = []
    for kv in range(ATT_KV):
        heads = []
        for g in range(ATT_GROUP):
            h = kv * ATT_GROUP + g
            c, half = divmod(h * HEAD_DIM, LANES)
            blk = _rope_block(qkv_ref[pl.ds(r0, Q_BLOCK), c * LANES:(c + 1) * LANES], cos_q, sin_q)
            heads.append(blk[:, half:half + HEAD_DIM])
        q = jnp.concatenate(heads, axis=0).astype(BF16)
        ck = ck_ref[0, :, kv * HEAD_DIM:(kv + 1) * HEAD_DIM].astype(BF16)
        cv = cv_ref[0, :, kv * HEAD_DIM:(kv + 1) * HEAD_DIM].astype(BF16)
        kwin = k_scr[pl.ds(ws, span), kv * HEAD_DIM:(kv + 1) * HEAD_DIM]
        vwin = qkv_ref[pl.ds(ws, span), qw + kw + kv * HEAD_DIM:qw + kw + (kv + 1) * HEAD_DIM].astype(BF16)
        s_ctx = _dot_nt(q, ck) * HEAD_DIM ** -0.5
        s_win = jnp.where(band, _dot_nt(q, kwin) * HEAD_DIM ** -0.5, -jnp.inf)
        o = _softmax_av([s_ctx, s_win], [cv, vwin], _sink_column(sink_ref, kv, Q_BLOCK))
        heads_out += [o[g * Q_BLOCK:(g + 1) * Q_BLOCK] for g in range(ATT_GROUP)]
    o_ref[...] = jnp.concatenate(heads_out, axis=1).astype(o_ref.dtype)


def _rope_tables(seq_len):
    pos = jnp.arange(seq_len, dtype=jnp.int32)
    row = (pos // GRID_W).astype(F32)
    col = (pos % GRID_W).astype(F32)
    nf = HEAD_DIM // 4
    inv = ROPE_BASE ** (-jnp.arange(nf, dtype=F32) / nf)
    ang_r = row[:, None] * inv[None, :]
    ang_c = col[:, None] * inv[None, :]
    cos_h = jnp.concatenate([jnp.cos(ang_r), jnp.cos(ang_r), jnp.cos(ang_c), jnp.cos(ang_c)], axis=1)
    sin_h = jnp.concatenate([-jnp.sin(ang_r), jnp.sin(ang_r), -jnp.sin(ang_c), jnp.sin(ang_c)], axis=1)
    reps = LANES // HEAD_DIM
    return jnp.tile(cos_h, (1, reps)), jnp.tile(sin_h, (1, reps))


def _attn_lat(qkv, cache_k, cache_v, sink, n_seq, seq_len):
    rows, cols = qkv.shape
    past = cache_k.shape[1]
    kw = ATT_KV * HEAD_DIM
    cos, sin = _rope_tables(seq_len)
    return pl.pallas_call(
        _attn_lat_kernel, grid=(n_seq, seq_len // Q_BLOCK),
        in_specs=[pl.BlockSpec(memory_space=pltpu.SMEM),
                  pl.BlockSpec((seq_len, cols), lambda b, i: (b, 0)),
                  pl.BlockSpec((1, past, kw), lambda b, i: (b, 0, 0)),
                  pl.BlockSpec((1, past, kw), lambda b, i: (b, 0, 0)),
                  pl.BlockSpec((seq_len, LANES), lambda b, i: (0, 0)),
                  pl.BlockSpec((seq_len, LANES), lambda b, i: (0, 0))],
        out_specs=pl.BlockSpec((Q_BLOCK, ATT_HEADS * HEAD_DIM), lambda b, i: (b * (seq_len // Q_BLOCK) + i, 0)),
        out_shape=jax.ShapeDtypeStruct((rows, ATT_HEADS * HEAD_DIM), BF16),
        scratch_shapes=[pltpu.VMEM((seq_len, kw), BF16)],
        compiler_params=_cparams("parallel", "arbitrary"), name="attn_latent",
    )(sink, qkv, cache_k.reshape(n_seq, past, kw), cache_v.reshape(n_seq, past, kw), cos, sin)


def _split_bf16(w):
    hi = w.astype(BF16)
    return hi, (w - hi.astype(F32)).astype(BF16)


def _ml_gates_kernel(h_ref, w_ref, wt_ref, b_ref, bt_ref, g_ref, gt_ref):
    h = h_ref[...]
    h_hi = h.astype(BF16)
    h_lo = (h - h_hi.astype(F32)).astype(BF16)
    ng = gt_ref.shape[0]
    by_hi = _dot(h_hi, w_ref[...])
    g_ref[...] = by_hi[:, :LANES] + by_hi[:, LANES:] + _dot(h_lo, w_ref[:, :LANES]) + b_ref[...]
    by_hi_t = _dot_nt(wt_ref[...], h_hi)
    gt_ref[...] = by_hi_t[:ng] + by_hi_t[ng:] + _dot_nt(wt_ref[:ng, :], h_lo) + bt_ref[...]


def _ml_gates(h, w_gates, b_gates, *, tm=512):
    rows, d = h.shape
    ng = w_gates.shape[1]
    w_hi, w_lo = _split_bf16(jnp.pad(w_gates, ((0, 0), (0, LANES - ng))))
    wt_hi, wt_lo = _split_bf16(w_gates.T)
    b_pad = jnp.pad(b_gates, (0, LANES - ng)).reshape(1, LANES)
    return pl.pallas_call(
        _ml_gates_kernel, grid=(rows // tm,),
        in_specs=[pl.BlockSpec((tm, d), lambda i: (i, 0)), pl.BlockSpec((d, 2 * LANES), lambda i: (0, 0)),
                  pl.BlockSpec((2 * ng, d), lambda i: (0, 0)), pl.BlockSpec((1, LANES), lambda i: (0, 0)),
                  pl.BlockSpec((ng, 1), lambda i: (0, 0))],
        out_specs=[pl.BlockSpec((tm, LANES), lambda i: (i, 0)), pl.BlockSpec((ng, tm), lambda i: (0, i))],
        out_shape=[jax.ShapeDtypeStruct((rows, LANES), F32), jax.ShapeDtypeStruct((ng, rows), F32)],
        compiler_params=_cparams("parallel"), name="mlstm_gates",
    )(h, jnp.concatenate([w_hi, w_lo], axis=1), jnp.concatenate([wt_hi, wt_lo], axis=0), b_pad,
      b_gates.reshape(ng, 1))


def _ml_qk_kernel(h_ref, w_ref, cw_ref, o_ref, *, seq_len, k_scale):
    j = pl.program_id(1)
    x = _dot(h_ref[...].astype(BF16), w_ref[...])
    t = x.shape[0]
    pos = lax.broadcasted_iota(jnp.int32, x.shape, 0) % seq_len
    prev = jnp.where(pos == 0, 0.0, pltpu.roll(x, 1, axis=0))
    nxt = jnp.where(pos == seq_len - 1, 0.0, pltpu.roll(x, t - 1, axis=0))
    y = prev * cw_ref[0:1, :] + x * cw_ref[1:2, :] + nxt * cw_ref[2:3, :]
    scale = jnp.where(j >= pl.num_programs(1) // 2, k_scale, 1.0).astype(F32)
    o_ref[...] = (_silu(y) * scale).astype(o_ref.dtype)


def _ml_qk(h, w_qk, conv_w, seq_len, *, tm=1024, tn=1024):
    rows, d = h.shape
    width = w_qk.shape[1]
    return pl.pallas_call(
        functools.partial(_ml_qk_kernel, seq_len=seq_len, k_scale=ML_DK ** -0.5), grid=(rows // tm, width // tn),
        in_specs=[pl.BlockSpec((tm, d), lambda i, j: (i, 0)), pl.BlockSpec((d, tn), lambda i, j: (0, j)),
                  pl.BlockSpec((3, tn), lambda i, j: (0, j))],
        out_specs=pl.BlockSpec((tm, tn), lambda i, j: (i, j)),
        out_shape=jax.ShapeDtypeStruct((rows, width), BF16),
        compiler_params=_cparams("parallel", "parallel"), name="mlstm_qk",
    )(h, w_qk, conv_w)


def _ml_scan_kernel(*refs, zero_init):
    refs = list(refs)
    dirs = [tuple(refs[0:5]), tuple(refs[5:10])]
    refs = refs[10:]
    if not zero_init:
        c0_ref, n0_ref, m0_ref = refs[:3]
        refs = refs[3:]
    hf_ref, hb_ref, c_ref, n_ref, m_ref = refs
    h_out = (hf_ref, hb_ref)
    c = pl.program_id(1)
    last = pl.num_programs(1) - 1

    @pl.when(c == 0)
    def _():
        if zero_init:
            c_ref[...] = jnp.zeros_like(c_ref)
            n_ref[...] = jnp.zeros_like(n_ref)
            m_ref[...] = jnp.zeros_like(m_ref)
        else:
            c_ref[...] = c0_ref[...]
            n_ref[...] = n0_ref[...]
            m_ref[...] = m0_ref[...]

    length = hf_ref.shape[0]
    ti = lax.broadcasted_iota(jnp.int32, (length, length), 0)
    si = lax.broadcasted_iota(jnp.int32, (length, length), 1)
    for d in range(2):
        q_ref, k_ref, v_ref, g_ref, gt_ref = dirs[d]
        causal = (ti >= si) if d == 0 else (ti <= si)
        tri = jnp.where(causal, 1.0, 0.0).astype(F32)
        f_col = _log_sigmoid(g_ref[...])
        f_row = _log_sigmoid(gt_ref[...])
        b_col = jnp.dot(tri, f_col, precision=HI, preferred_element_type=F32)
        b_row = _dot_nt(f_row, tri, precision=HI)
        edge = length - 1 if d == 0 else 0
        for h in range(ML_HEADS):
            ji = d * 2 * ML_HEADS + h
            jf = ji + ML_HEADS
            bc = b_col[:, jf:jf + 1]
            br = b_row[jf:jf + 1, :]
            i_row = gt_ref[ji:ji + 1, :]
            i_col = g_ref[:, ji:ji + 1]
            m_prev = m_ref[0, d, h][:, 0:1]
            q = q_ref[:, h * ML_DK:(h + 1) * ML_DK]
            k = k_ref[:, h * ML_DK:(h + 1) * ML_DK]
            v = v_ref[:, h * ML_DV:(h + 1) * ML_DV].astype(BF16)
            cst = c_ref[0, d, h]
            nst = n_ref[0, d, h]
            a_row = i_row - br
            amat = jnp.where(causal, a_row, -jnp.inf)
            u = jnp.maximum(m_prev, jnp.max(amat, axis=1, keepdims=True))
            qk = (_dot_nt(q, k) * jnp.exp(amat - u)).astype(BF16)
            sc = jnp.exp(m_prev - u)
            state_ext = jnp.concatenate([cst, jnp.broadcast_to(nst, (LANES, ML_DK))], axis=0).astype(BF16)
            v_ext = jnp.concatenate([v, jnp.ones((length, LANES), BF16)], axis=1)
            tot = sc * _dot_nt(q, state_ext) + _dot(qk, v_ext)
            inv = 1.0 / jnp.maximum(jnp.abs(tot[:, ML_DV:]), jnp.exp(-(bc + u)))
            h_out[d][:, h * ML_DV:(h + 1) * ML_DV] = tot[:, :ML_DV] * jnp.concatenate([inv] * (ML_DV // LANES), axis=1)
            b_last = br[:, edge:edge + 1]
            wlog_row = b_last + a_row
            m_new = jnp.maximum(b_last + m_prev, jnp.max(wlog_row, axis=1, keepdims=True))
            decay = jnp.exp(b_last + m_prev - m_new)
            ws_row = jnp.exp(wlog_row - m_new)
            ws_col = jnp.exp(b_last - bc + i_col - m_new)
            kw = (ws_col * k.astype(F32)).astype(BF16)
            c_ref[0, d, h] = decay * cst + _dot_tn(v, kw)
            n_ref[0, d, h] = decay * nst + _dot(jnp.broadcast_to(ws_row, (8, length)).astype(BF16), k)[0:1]
            m_ref[0, d, h] = jnp.broadcast_to(m_new, (1, ML_DK))


def _ml_scan(qk, p, g, gt, state, n_seq, seq_len):
    rows = qk.shape[0]
    length = min(ML_CHUNK, seq_len)
    nc = seq_len // length
    qw = ML_HEADS * ML_DK
    vw = ML_HEADS * ML_DV
    ng = gt.shape[0]

    def fwd(b, c):
        return b * nc + c

    def bwd(b, c):
        return b * nc + nc - 1 - c

    args, specs = [], []
    for pos in (fwd, bwd):
        args += [qk, qk, p, g, gt]
        specs += [pl.BlockSpec((length, qw), lambda b, c, pos=pos: (pos(b, c), 0)),
                  pl.BlockSpec((length, qw), lambda b, c, pos=pos: (pos(b, c), 1)),
                  pl.BlockSpec((length, vw), lambda b, c, pos=pos: (pos(b, c), 0)),
                  pl.BlockSpec((length, LANES), lambda b, c, pos=pos: (pos(b, c), 0)),
                  pl.BlockSpec((ng, length), lambda b, c, pos=pos: (0, pos(b, c)))]
    c_spec = pl.BlockSpec((1, 2, ML_HEADS, ML_DV, ML_DK), lambda b, c: (b, 0, 0, 0, 0))
    n_spec = pl.BlockSpec((1, 2, ML_HEADS, 1, ML_DK), lambda b, c: (b, 0, 0, 0, 0))
    zero_init = state is None
    if not zero_init:
        c0, n0, m0 = state
        args += [c0, n0.reshape(n_seq, 2, ML_HEADS, 1, ML_DK),
                 jnp.broadcast_to(m0[..., None, None], (n_seq, 2, ML_HEADS, 1, ML_DK))]
        specs += [c_spec, n_spec, n_spec]
    hf, hb, c_fin, n_fin, m_fin = pl.pallas_call(
        functools.partial(_ml_scan_kernel, zero_init=zero_init), grid=(n_seq, nc), in_specs=specs,
        out_specs=[pl.BlockSpec((length, vw), lambda b, c: (fwd(b, c), 0)),
                   pl.BlockSpec((length, vw), lambda b, c: (bwd(b, c), 0)), c_spec, n_spec, n_spec],
        out_shape=[jax.ShapeDtypeStruct((rows, vw), F32), jax.ShapeDtypeStruct((rows, vw), F32),
                   jax.ShapeDtypeStruct((n_seq, 2, ML_HEADS, ML_DV, ML_DK), F32),
                   jax.ShapeDtypeStruct((n_seq, 2, ML_HEADS, 1, ML_DK), F32),
                   jax.ShapeDtypeStruct((n_seq, 2, ML_HEADS, 1, ML_DK), F32)],
        compiler_params=_cparams("parallel", "arbitrary"), name="mlstm_scan",
    )(*args)
    return hf, hb, (c_fin, n_fin[:, :, :, 0, :], m_fin[:, :, :, 0, 0])


def _gla_scan_kernel(*refs, zero_init):
    refs = list(refs)
    dirs = [tuple(refs[0:4]), tuple(refs[4:8])]
    w2_ref, ba_ref = refs[8:10]
    refs = refs[10:]
    if not zero_init:
        s0_ref = refs.pop(0)
    of_ref, ob_ref, s_ref, st_scr, la_scr = refs
    o_out = (of_ref, ob_ref)
    c = pl.program_id(1)
    last = pl.num_programs(1) - 1
    kw = GLA_HEADS * GLA_DK
    n_sub = of_ref.shape[0] // GLA_SUB

    @pl.when(c == 0)
    def _():
        for d in range(2):
            for h in range(GLA_HEADS):
                st_scr[d, h] = jnp.zeros((GLA_DV, GLA_DK), F32) if zero_init else s0_ref[0, d, h].T

    for d in range(2):
        u = dirs[d][3][...].astype(BF16)
        z = _dot(u, w2_ref[:, d * kw:(d + 1) * kw]) + ba_ref[:, d * kw:(d + 1) * kw]
        la_scr[d] = _log_sigmoid(z) / GLA_TAU

    ti = lax.broadcasted_iota(jnp.int32, (GLA_SUB, GLA_SUB), 0)
    si = lax.broadcasted_iota(jnp.int32, (GLA_SUB, GLA_SUB), 1)
    s_lane = lax.broadcasted_iota(jnp.int32, (GLA_SUB, GLA_SUB), 1)

    def sub_chunk(j, carry):
        for d in range(2):
            q_ref, k_ref, v_ref, _ = dirs[d]
            r0 = pl.multiple_of((j if d == 0 else n_sub - 1 - j) * GLA_SUB, GLA_SUB)
            causal = (ti >= si) if d == 0 else (ti <= si)
            tri = jnp.where(causal, 1.0, 0.0).astype(F32)
            bc_all = jnp.dot(tri, la_scr[d, pl.ds(r0, GLA_SUB), :], precision=HI, preferred_element_type=F32)
            edge = GLA_SUB - 1 if d == 0 else 0
            for h in range(GLA_HEADS):
                bc = bc_all[:, h * GLA_DK:(h + 1) * GLA_DK]
                q = q_ref[pl.ds(r0, GLA_SUB), h * GLA_DK:(h + 1) * GLA_DK] * GLA_DK ** -0.5
                k = k_ref[pl.ds(r0, GLA_SUB), h * GLA_DK:(h + 1) * GLA_DK]
                v = v_ref[pl.ds(r0, GLA_SUB), h * GLA_DV:(h + 1) * GLA_DV].astype(BF16)
                bc2 = bc * LOG2E
                a = jnp.zeros((GLA_SUB, GLA_SUB), F32)
                for s in range(GLA_SUB):
                    decay = jnp.exp2(bc2 - bc2[s:s + 1, :])
                    col = jnp.sum(q * (k[s:s + 1, :] * decay), axis=1, keepdims=True)
                    a = jnp.where(s_lane == s, col, a)
                a = jnp.where(causal, a, 0.0)
                st = st_scr[d, h]
                o = _dot(a.astype(BF16), v) + _dot_nt((q * jnp.exp2(bc2)).astype(BF16), st.astype(BF16))
                o_out[d][pl.ds(r0, GLA_SUB), h * GLA_DV:(h + 1) * GLA_DV] = o
                b_last = bc2[edge:edge + 1, :]
                k_dec = (k * jnp.exp2(b_last - bc2)).astype(BF16)
                st_scr[d, h] = jnp.exp2(b_last) * st + _dot_tn(v, k_dec)
        return carry

    lax.fori_loop(0, n_sub, sub_chunk, 0)

    @pl.when(c == last)
    def _():
        for d in range(2):
            for h in range(GLA_HEADS):
                s_ref[0, d, h] = st_scr[d, h].T


def _gla_scan(p, u, w2, b_a, state, n_seq, seq_len):
    rows = p.shape[0]
    length = min(GLA_BLOCK, seq_len)
    nc = seq_len // length
    kw = GLA_HEADS * GLA_DK
    vw = GLA_HEADS * GLA_DV

    def fwd(b, c):
        return b * nc + c

    def bwd(b, c):
        return b * nc + nc - 1 - c

    args, specs = [], []
    for pos in (fwd, bwd):
        args += [p, p, p, u]
        specs += [pl.BlockSpec((length, kw), lambda b, c, pos=pos: (pos(b, c), 0)),
                  pl.BlockSpec((length, kw), lambda b, c, pos=pos: (pos(b, c), 1)),
                  pl.BlockSpec((length, vw), lambda b, c, pos=pos: (pos(b, c), 2 * kw // vw)),
                  pl.BlockSpec((length, LANES), lambda b, c, pos=pos: (pos(b, c), 0))]
    args += [w2, b_a]
    specs += [pl.BlockSpec(w2.shape, lambda b, c: (0, 0)), pl.BlockSpec(b_a.shape, lambda b, c: (0, 0))]
    s_spec = pl.BlockSpec((1, 2, GLA_HEADS, GLA_DK, GLA_DV), lambda b, c: (b, 0, 0, 0, 0))
    zero_init = state is None
    if not zero_init:
        args.append(state)
        specs.append(s_spec)
    return pl.pallas_call(
        functools.partial(_gla_scan_kernel, zero_init=zero_init), grid=(n_seq, nc), in_specs=specs,
        out_specs=[pl.BlockSpec((length, vw), lambda b, c: (fwd(b, c), 0)),
                   pl.BlockSpec((length, vw), lambda b, c: (bwd(b, c), 0)), s_spec],
        out_shape=[jax.ShapeDtypeStruct((rows, vw), F32), jax.ShapeDtypeStruct((rows, vw), F32),
                   jax.ShapeDtypeStruct((n_seq, 2, GLA_HEADS, GLA_DK, GLA_DV), F32)],
        scratch_shapes=[pltpu.VMEM((2, GLA_HEADS, GLA_DV, GLA_DK), F32), pltpu.VMEM((2, length, kw), F32)],
        compiler_params=_cparams("parallel", "arbitrary"), name="gla_scan",
    )(*args)


def kernel(x_prompt, x_sample, cache_k_0, cache_v_0, state_mlstm_C_1, state_mlstm_n_1, state_mlstm_m_1, state_gla_S_2, cache_k_3, cache_v_3, c, c_ctx, w_mod, b_mod, norm1_g, norm2_g, final_g, router_w, router_b, moe_wg, moe_wu, moe_wd, attn0_w_qkv, attn0_sink, attn0_w_o, mlstm1_w_in, mlstm1_b_gates, mlstm1_conv, mlstm1_norm_g, mlstm1_w_out, gla2_w_in, gla2_w_a1, gla2_w_a2, gla2_b_a, gla2_norm_g, gla2_w_out, attn3_w_qkv, attn3_sink, attn3_w_o):
    n_ctx, ctx_len, d = x_prompt.shape
    n_lat, lat_len, _ = x_sample.shape
    depth = w_mod.shape[0]

    cvec = jnp.concatenate([c_ctx[None, :], c, jnp.zeros((8 - 1 - n_lat, d), F32)], axis=0)
    mod = _modulation(cvec, w_mod, b_mod).reshape(depth, 8, 6, 1, d)

    def mods(layer, kind, latent):
        return mod[layer, 1:1 + n_lat, kind] if latent else mod[layer, 0:1, kind]

    rw_hi = router_w.T.astype(BF16)
    rw_lo = (router_w.T - rw_hi.astype(F32)).astype(BF16)
    rwt = jnp.concatenate([rw_hi, rw_lo], axis=0)
    rb = router_b.reshape(-1, 1)
    attn_w = {0: (attn0_w_qkv.astype(BF16), attn0_sink, attn0_w_o.astype(BF16), cache_k_0, cache_v_0),
              3: (attn3_w_qkv.astype(BF16), attn3_sink, attn3_w_o.astype(BF16), cache_k_3, cache_v_3)}
    ml_qw = ML_HEADS * ML_DK
    ml_vw = ML_HEADS * ML_DV
    ml_main = 2 * ml_qw + 2 * ml_vw
    ml_w_qk = mlstm1_w_in[:, :2 * ml_qw].astype(BF16)
    ml_w_vo = mlstm1_w_in[:, 2 * ml_qw:ml_main].astype(BF16)
    ml_w_gates = mlstm1_w_in[:, ml_main:]
    ml_w_out = mlstm1_w_out.astype(BF16)
    gla_kw = GLA_HEADS * GLA_DK
    gla_w_in = gla2_w_in.astype(BF16)
    gla_w_a1 = jnp.pad(jnp.concatenate([gla2_w_a1[0], gla2_w_a1[1]], axis=1),
                       ((0, 0), (0, LANES - 2 * GLA_RANK))).astype(BF16)
    gla_w2 = jnp.zeros((LANES, 2 * gla_kw), F32)
    gla_w2 = gla_w2.at[:GLA_RANK, :gla_kw].set(gla2_w_a2[0]).at[GLA_RANK:2 * GLA_RANK, gla_kw:].set(gla2_w_a2[1])
    gla_w2 = gla_w2.astype(BF16)
    gla_ba = gla2_b_a.reshape(1, 2 * gla_kw)
    gla_w_out = gla2_w_out.astype(BF16)

    new_state = []

    def mixer(layer, s, count0):
        latent, n_seq, seq_len, x, h = s["latent"], s["n_seq"], s["seq_len"], s["x"], s["h"]
        tail = (mods(layer, 2, latent), norm2_g[layer], mods(layer, 4, latent), mods(layer, 3, latent), rwt, rb,
                count0)
        kind = layer % 3
        if kind == 0:
            w_qkv, sink, w_o, ck, cv = attn_w[layer]
            qkv = _matmul(h, w_qkv)
            if latent:
                att = _attn_lat(qkv, ck, cv, sink, n_seq, seq_len)
            else:
                att = _attn_ctx(qkv, sink, n_seq, seq_len)
                qw = ATT_HEADS * HEAD_DIM
                kw = ATT_KV * HEAD_DIM
                new_state.append(qkv[:, qw:qw + kw].reshape(n_seq, seq_len, ATT_KV, HEAD_DIM))
                new_state.append(qkv[:, qw + kw:].reshape(n_seq, seq_len, ATT_KV, HEAD_DIM))
            return _proj("plain", (att,), w_o, x, *tail)
        if kind == 1:
            p = _matmul(h, ml_w_vo)
            g, gt = _ml_gates(h, ml_w_gates, mlstm1_b_gates)
            qk = _ml_qk(h, ml_w_qk, mlstm1_conv, seq_len)
            st = (state_mlstm_C_1, state_mlstm_n_1, state_mlstm_m_1) if latent else None
            hf, hb, fin = _ml_scan(qk, p, g, gt, st, n_seq, seq_len)
            if not latent:
                new_state.extend(fin)
            return _proj("mlstm", (hf, hb, p, 1, mlstm1_norm_g), ml_w_out, x, *tail)
        p = _matmul(h, gla_w_in)
        u = _matmul(h, gla_w_a1)
        of, ob, s_fin = _gla_scan(p, u, gla_w2, gla_ba, state_gla_S_2 if latent else None, n_seq, seq_len)
        if not latent:
            new_state.append(s_fin)
        gla_vw = GLA_HEADS * GLA_DV
        return _proj("gla", (of, ob, p, (2 * gla_kw + gla_vw) // gla_vw, gla2_norm_g), gla_w_out, x, *tail)

    streams = [dict(latent=False, n_seq=n_ctx, seq_len=ctx_len, x=x_prompt.reshape(n_ctx * ctx_len, d)),
               dict(latent=True, n_seq=n_lat, seq_len=lat_len, x=x_sample.reshape(n_lat * lat_len, d))]
    for s in streams:
        s["h"] = _rownorm(s["x"], norm1_g[0], mods(0, 1, s["latent"]), mods(0, 0, s["latent"]))
    for layer in range(depth):
        counts = jnp.zeros((N_EXPERTS, LANES), jnp.int32)
        for s in streams:
            s["x"], s["h2"], s["meta"], s["wcol"], counts = mixer(layer, s, counts)
        cnt = counts[:, 0]
        for s in streams:
            s["slots"] = _slots(s["meta"], cnt)
        xs, info = _dispatch([s["h2"] for s in streams], jnp.concatenate([s["slots"] for s in streams], axis=1), cnt)
        ys = _ffn(xs, info, moe_wg, moe_wu, moe_wd, layer)
        for s in streams:
            latent = s["latent"]
            gate2 = mods(layer, 5, latent)
            if layer + 1 < depth:
                s["x"], s["h"] = _combine(s["x"], ys, s["slots"], s["wcol"], gate2, norm1_g[layer + 1],
                                          mod=(mods(layer + 1, 1, latent), mods(layer + 1, 0, latent)), out_x=True,
                                          h_dtype=F32 if (layer + 1) % 3 == 1 else BF16)
            else:
                s["out"] = _combine(s["x"], ys, s["slots"], s["wcol"], gate2, final_g, h_dtype=F32)
    y_prompt = streams[0]["out"].reshape(n_ctx, ctx_len, d)
    y_sample = streams[1]["out"].reshape(n_lat, lat_len, d)
    return (y_prompt, y_sample, *new_state)
```

```python
import functools

import jax
import jax.numpy as jnp
from jax import lax
from jax.experimental import pallas as pl
from jax.experimental.pallas import tpu as pltpu

F32 = jnp.float32
BF16 = jnp.bfloat16
HI = lax.Precision.HIGHEST

EPS = 1e-6
LOG2E = 1.4426950408889634
GRID_W = 64
ATT_HEADS = 16
ATT_KV = 4
ATT_GROUP = ATT_HEADS // ATT_KV
HEAD_DIM = 64
WINDOW = 128
Q_BLOCK = 128
ROPE_BASE = 10000.0
ML_HEADS = 8
ML_DK = 128
ML_DV = 256
ML_CHUNK = 128
GLA_HEADS = 4
GLA_DK = 128
GLA_DV = 256
GLA_RANK = 16
GLA_TAU = 16.0
GLA_SUB = 16
GLA_BLOCK = 256
N_EXPERTS = 16
N_GROUPS = 4
GROUP_SIZE = N_EXPERTS // N_GROUPS
LANES = 128
VMEM_LIMIT = 56 * 1024 * 1024


def _cparams(*sem):
    return pltpu.CompilerParams(dimension_semantics=sem, vmem_limit_bytes=VMEM_LIMIT)


def _dot(a, b):
    return jnp.dot(a, b, preferred_element_type=F32)


def _dot_nt(a, b, precision=None):
    return lax.dot_general(a, b, (((1,), (1,)), ((), ())), precision=precision, preferred_element_type=F32)


def _dot_tn(a, b):
    return lax.dot_general(a, b, (((0,), (0,)), ((), ())), preferred_element_type=F32)


def _sigmoid(x):
    return 1.0 / (1.0 + jnp.exp(-x))


def _silu(x):
    return x * _sigmoid(x)


def _log_sigmoid(x):
    return jnp.minimum(x, 0.0) - jnp.log(1.0 + jnp.exp(-jnp.abs(x)))


def _rms_rows(x, g):
    ms = jnp.mean(x * x, axis=-1, keepdims=True)
    return x * lax.rsqrt(ms + EPS) * g


def _mod_kernel(c_ref, w_ref, b_ref, o_ref):
    s = _silu(c_ref[...])
    o_ref[0] = _dot(s.astype(BF16), w_ref[0].astype(BF16)) + b_ref[0]


def _modulation(cvec, w_mod, b_mod):
    depth, d, n6 = w_mod.shape
    tn = 1536
    return pl.pallas_call(
        _mod_kernel,
        grid=(depth, n6 // tn),
        in_specs=[pl.BlockSpec((8, d), lambda l, j: (0, 0)),
                  pl.BlockSpec((1, d, tn), lambda l, j: (l, 0, j)),
                  pl.BlockSpec((1, 1, tn), lambda l, j: (l, 0, j))],
        out_specs=pl.BlockSpec((1, 8, tn), lambda l, j: (l, 0, j)),
        out_shape=jax.ShapeDtypeStruct((depth, 8, n6), F32),
        compiler_params=_cparams("parallel", "parallel"),
        name="adaln_modulation",
    )(cvec, w_mod, b_mod.reshape(depth, 1, n6))


def _route(h, rwt, rb, carry):
    tm = h.shape[0]
    h_hi = h.astype(BF16)
    h_lo = (h - h_hi.astype(F32)).astype(BF16)
    by_hi = _dot_nt(rwt, h_hi)
    logits = by_hi[:N_EXPERTS] + by_hi[N_EXPERTS:] + _dot_nt(rwt[:N_EXPERTS], h_lo)
    scores = _sigmoid(logits)
    sel = scores + rb
    expert = lax.broadcasted_iota(jnp.int32, sel.shape, 0)
    pos = expert % GROUP_SIZE
    grp = expert // GROUP_SIZE

    def mate(x, k):
        ahead = pltpu.roll(x, N_EXPERTS - k, axis=0)
        behind = pltpu.roll(x, GROUP_SIZE - k, axis=0)
        return jnp.where(pos + k < GROUP_SIZE, ahead, behind)

    beaten = jnp.zeros_like(sel)
    for k in range(1, GROUP_SIZE):
        other = mate(sel, k)
        other_first = (pos + k) % GROUP_SIZE < pos
        beaten = beaten + jnp.where(other_first, jnp.where(other >= sel, 1.0, 0.0), jnp.where(other > sel, 1.0, 0.0))
    top2 = jnp.where(beaten < 2.0, sel, 0.0)
    gscore = top2
    for k in range(1, GROUP_SIZE):
        gscore = gscore + mate(top2, k)
    lost = jnp.zeros_like(sel)
    for k in range(1, N_GROUPS):
        other = pltpu.roll(gscore, N_EXPERTS - GROUP_SIZE * k, axis=0)
        other_first = (grp + k) % N_GROUPS < grp
        lost = lost + jnp.where(other_first, jnp.where(other >= gscore, 1.0, 0.0),
                                jnp.where(other > gscore, 1.0, 0.0))
    picked = jnp.where(lost < 0.5, jnp.where(beaten < 2.0, 1.0, 0.0), 0.0)
    chosen = picked > 0.5
    weight = jnp.where(chosen, scores, 0.0)
    wsum = jnp.sum(weight, axis=0, keepdims=True)
    e_f = expert.astype(F32)
    e_a = jnp.min(jnp.where(chosen, e_f, float(N_EXPERTS)), axis=0, keepdims=True)
    e_b = jnp.max(jnp.where(chosen, e_f, -1.0), axis=0, keepdims=True)
    before = (lax.broadcasted_iota(jnp.int32, (tm, tm), 0) < lax.broadcasted_iota(jnp.int32, (tm, tm), 1))
    rank = _dot(picked.astype(BF16), jnp.where(before, 1.0, 0.0).astype(BF16)) + carry
    is_a = e_f == e_a
    is_b = e_f == e_b
    r_a = jnp.sum(jnp.where(is_a, rank, 0.0), axis=0, keepdims=True)
    r_b = jnp.sum(jnp.where(is_b, rank, 0.0), axis=0, keepdims=True)
    w_a = jnp.sum(jnp.where(is_a, weight, 0.0), axis=0, keepdims=True)
    w_b = jnp.sum(jnp.where(is_b, weight, 0.0), axis=0, keepdims=True)
    meta = jnp.concatenate([e_a, e_b, r_a, r_b, jnp.zeros((4, tm), F32)], axis=0).astype(jnp.int32)
    wcol = jnp.concatenate([w_a / wsum, w_b / wsum, jnp.zeros((LANES - 2, tm), F32)], axis=0).T
    return meta, wcol, carry + jnp.sum(picked, axis=1, keepdims=True)


def _norm_mod(x, g_ref, mod_refs):
    h = _rms_rows(x, g_ref[...])
    if mod_refs is not None:
        a_ref, s_ref = mod_refs
        h = h * (1.0 + a_ref[0]) + s_ref[0]
    return h


def _rownorm_kernel(x_ref, g_ref, a_ref, s_ref, h_ref):
    h_ref[...] = _norm_mod(x_ref[...], g_ref, (a_ref, s_ref)).astype(h_ref.dtype)


def _mod_spec(n_mod, rows, tm, d, n_prefetch=0):
    per = (rows // n_mod) // tm
    return pl.BlockSpec((1, 1, d), lambda i, *_: (i // per, 0, 0))


def _rownorm(x, g, scale, shift, *, tm=512):
    rows, d = x.shape
    row_spec = pl.BlockSpec((tm, d), lambda i: (i, 0))
    return pl.pallas_call(
        _rownorm_kernel, grid=(rows // tm,),
        in_specs=[row_spec, pl.BlockSpec((1, d), lambda i: (0, 0)), _mod_spec(scale.shape[0], rows, tm, d),
                  _mod_spec(shift.shape[0], rows, tm, d)],
        out_specs=row_spec, out_shape=jax.ShapeDtypeStruct((rows, d), BF16),
        compiler_params=_cparams("parallel"), name="rownorm",
    )(x, g.reshape(1, d), scale, shift)


def _mm_kernel(a_ref, w_ref, o_ref):
    o_ref[...] = _dot(a_ref[...].astype(BF16), w_ref[...]).astype(o_ref.dtype)


def _matmul(a, w, *, out_dtype=F32, tm=1024):
    m, k = a.shape
    n = w.shape[1]
    tn = next(t for t in (1024, 768, 512, LANES) if n % t == 0)
    return pl.pallas_call(
        _mm_kernel, grid=(m // tm, n // tn),
        in_specs=[pl.BlockSpec((tm, k), lambda i, j: (i, 0)), pl.BlockSpec((k, tn), lambda i, j: (0, j))],
        out_specs=pl.BlockSpec((tm, tn), lambda i, j: (i, j)),
        out_shape=jax.ShapeDtypeStruct((m, n), out_dtype),
        compiler_params=_cparams("parallel", "parallel"), name="matmul",
    )(a, w)


def _head_norm(x, g, n_heads, dv):
    outs = []
    for h in range(n_heads):
        xs = x[:, h * dv:(h + 1) * dv]
        ms = jnp.mean(xs * xs, axis=-1, keepdims=True)
        outs.append(xs * lax.rsqrt(ms + EPS) * g[:, h * dv:(h + 1) * dv])
    return jnp.concatenate(outs, axis=1)


def _proj_kernel(*refs, pre):
    refs = list(refs)
    if pre == "plain":
        a = refs.pop(0)[...]
    else:
        f_ref, b_ref, p_ref, hg_ref = refs.pop(0), refs.pop(0), refs.pop(0), refs.pop(0)
        hsum = f_ref[...] + b_ref[...]
        if pre == "mlstm":
            a = _sigmoid(p_ref[...]) * _head_norm(hsum, hg_ref[...], ML_HEADS, ML_DV)
        else:
            a = _head_norm(hsum, hg_ref[...], GLA_HEADS, GLA_DV) * _silu(p_ref[...])
        a = a.astype(BF16)
    w_ref, x_ref, gate_ref, g_ref, a_ref, s_ref, rwt_ref, rb_ref, count0_ref = refs[:9]
    xo_ref, h_ref, meta_ref, wcol_ref, count_ref, carry_ref = refs[9:]

    @pl.when(pl.program_id(0) == 0)
    def _():
        carry_ref[...] = count0_ref[...].astype(F32)

    x = x_ref[...] + gate_ref[0] * _dot(a, w_ref[...])
    xo_ref[...] = x
    h = _norm_mod(x, g_ref, (a_ref, s_ref))
    _rows_to_tiles(h_ref, h)
    meta, wcol, carry = _route(h, rwt_ref[...], rb_ref[...], carry_ref[:, 0:1])
    meta_ref[...] = meta
    wcol_ref[...] = wcol
    carry_ref[...] = jnp.broadcast_to(carry, carry_ref.shape)
    count_ref[...] = jnp.broadcast_to(carry, count_ref.shape).astype(jnp.int32)


def _proj(pre, pre_args, w_out, x, gate, g, scale, shift, rwt, rb, count0, *, tm=512):
    rows, d = x.shape
    k = w_out.shape[0]
    row_spec = pl.BlockSpec((tm, d), lambda i: (i, 0))
    if pre == "plain":
        args, specs = [pre_args[0]], [pl.BlockSpec((tm, k), lambda i: (i, 0))]
    else:
        hf, hb, p, col_block, hg = pre_args
        wide = pl.BlockSpec((tm, k), lambda i: (i, 0))
        args = [hf, hb, p, hg.reshape(1, k)]
        specs = [wide, wide, pl.BlockSpec((tm, k), lambda i: (i, col_block)), pl.BlockSpec((1, k), lambda i: (0, 0))]
    args += [w_out, x, gate, g.reshape(1, d), scale, shift, rwt, rb, count0]
    specs += [pl.BlockSpec((k, d), lambda i: (0, 0)), row_spec, _mod_spec(gate.shape[0], rows, tm, d),
              pl.BlockSpec((1, d), lambda i: (0, 0)), _mod_spec(scale.shape[0], rows, tm, d),
              _mod_spec(shift.shape[0], rows, tm, d), pl.BlockSpec(rwt.shape, lambda i: (0, 0)),
              pl.BlockSpec(rb.shape, lambda i: (0, 0)), pl.BlockSpec(count0.shape, lambda i: (0, 0))]
    return pl.pallas_call(
        functools.partial(_proj_kernel, pre=pre), grid=(rows // tm,), in_specs=specs,
        out_specs=[row_spec, pl.BlockSpec((tm * SUBLANES, LANES), lambda i: (i, 0)),
                   pl.BlockSpec((8, tm), lambda i: (0, i)),
                   pl.BlockSpec((tm, LANES), lambda i: (i, 0)), pl.BlockSpec((N_EXPERTS, LANES), lambda i: (0, 0))],
        out_shape=[jax.ShapeDtypeStruct((rows, d), F32), jax.ShapeDtypeStruct((rows * SUBLANES, LANES), F32),
                   jax.ShapeDtypeStruct((8, rows), jnp.int32), jax.ShapeDtypeStruct((rows, LANES), F32),
                   jax.ShapeDtypeStruct((N_EXPERTS, LANES), jnp.int32)],
        scratch_shapes=[pltpu.VMEM((N_EXPERTS, LANES), F32)],
        compiler_params=_cparams("arbitrary"), name="proj_" + pre,
    )(*args)


MOE_TILE = 512
MOE_TILE_SHIFT = 9
MOE_TOKENS = 256
ROW_UNROLL = 8


SUBLANES = 8


def _rows_to_tiles(ref, x, lead=()):
    rows = x.shape[0]
    for c in range(SUBLANES):
        ref[(*lead, pl.ds(c, rows, stride=SUBLANES), slice(None))] = x[:, c * LANES:(c + 1) * LANES]


def _tiles_to_rows(ref, rows, lead=()):
    return jnp.concatenate([ref[(*lead, pl.ds(c, rows, stride=SUBLANES), slice(None))] for c in range(SUBLANES)],
                           axis=1)


def _slot_tiles(rows):
    return (2 * rows) // MOE_TILE + N_EXPERTS


def _expert_offsets(cnt_ref, off_ref):
    def per_expert(e, k):
        off_ref[e] = k * MOE_TILE
        return k + ((cnt_ref[e] + MOE_TILE - 1) >> MOE_TILE_SHIFT)
    return lax.fori_loop(0, N_EXPERTS, per_expert, 0)


def _slots_kernel(cnt_ref, meta_ref, slot_ref, off_ref):
    @pl.when(pl.program_id(0) == 0)
    def _():
        _expert_offsets(cnt_ref, off_ref)

    e_a, e_b = meta_ref[0:1, :], meta_ref[1:2, :]
    off_a = jnp.zeros_like(e_a)
    off_b = jnp.zeros_like(e_b)
    for e in range(N_EXPERTS):
        off_a = jnp.where(e_a == e, off_ref[e], off_a)
        off_b = jnp.where(e_b == e, off_ref[e], off_b)
    slot_ref[...] = jnp.concatenate([off_a + meta_ref[2:3, :], off_b + meta_ref[3:4, :],
                                     jnp.zeros((6, e_a.shape[1]), jnp.int32)], axis=0)


def _slots(meta, counts, *, tm=1024):
    rows = meta.shape[1]
    grid_spec = pltpu.PrefetchScalarGridSpec(
        num_scalar_prefetch=1, grid=(rows // tm,),
        in_specs=[pl.BlockSpec((8, tm), lambda i, cnt: (0, i))],
        out_specs=pl.BlockSpec((8, tm), lambda i, cnt: (0, i)),
        scratch_shapes=[pltpu.SMEM((N_EXPERTS,), jnp.int32)])
    return pl.pallas_call(
        _slots_kernel, grid_spec=grid_spec, out_shape=jax.ShapeDtypeStruct((8, rows), jnp.int32),
        compiler_params=_cparams("arbitrary"), name="moe_slots",
    )(counts, meta)


def _dispatch_kernel(*refs, steps):
    sa_ref, sb_ref, cnt_ref = refs[:3]
    h_refs = refs[3:3 + len(steps)]
    xs_ref, info_ref, off_ref, zero_ref, sem = refs[3 + len(steps):]
    i = pl.program_id(0)
    tm = h_refs[0].shape[0] // SUBLANES
    n_tiles = info_ref.shape[0] - 1
    tile_rows = MOE_TILE * SUBLANES

    def tile_copy(tile):
        return pltpu.make_async_copy(zero_ref, xs_ref.at[pl.ds(tile * tile_rows, tile_rows), :], sem)

    @pl.when(i == 0)
    def _():
        zero_ref[...] = jnp.zeros_like(zero_ref)
        used = _expert_offsets(cnt_ref, off_ref)

        def per_expert(e, _):
            first = off_ref[e] >> MOE_TILE_SHIFT
            nt = (cnt_ref[e] + MOE_TILE - 1) >> MOE_TILE_SHIFT

            def fill(j, _):
                info_ref[first + j] = e
                return 0
            lax.fori_loop(0, nt, fill, 0)

            @pl.when(nt > 0)
            def _():
                tile_copy(first + nt - 1).start()
                tile_copy(first + nt - 1).wait()
            return 0
        lax.fori_loop(0, N_EXPERTS, per_expert, 0)
        info_ref[n_tiles] = used

        def tail(j, _):
            info_ref[j] = N_EXPERTS - 1
            tile_copy(j).start()
            tile_copy(j).wait()
            return 0
        lax.fori_loop(used, n_tiles, tail, 0)

    base = i * tm

    def copy_rows(h_ref):
        def row_copy(t, slot):
            dst = pl.multiple_of(slot * SUBLANES, SUBLANES)
            return pltpu.make_async_copy(h_ref.at[pl.ds(t * SUBLANES, SUBLANES), :],
                                         xs_ref.at[pl.ds(dst, SUBLANES), :], sem)

        for t in range(tm):
            row_copy(t, sa_ref[base + t]).start(priority=0)
            row_copy(t, sb_ref[base + t]).start(priority=1)
        for _ in range(2):
            pltpu.make_async_copy(h_ref, xs_ref.at[pl.ds(0, tm * SUBLANES), :], sem).wait()

    first = 0
    for h_ref, n in zip(h_refs, steps):
        pl.when(jnp.logical_and(i >= first, i < first + n))(functools.partial(copy_rows, h_ref))
        first += n


def _dispatch(hs, slots, counts):
    tm = MOE_TOKENS
    steps = tuple(h.shape[0] // (tm * SUBLANES) for h in hs)
    n_tiles = _slot_tiles(sum(steps) * tm)
    specs, first = [], 0
    for n in steps:
        specs.append(pl.BlockSpec((tm * SUBLANES, LANES),
                                  lambda i, *_, first=first, n=n: (jnp.clip(i - first, 0, n - 1), 0)))
        first += n
    grid_spec = pltpu.PrefetchScalarGridSpec(
        num_scalar_prefetch=3, grid=(sum(steps),), in_specs=specs,
        out_specs=[pl.BlockSpec(memory_space=pl.ANY), pl.BlockSpec(memory_space=pltpu.SMEM)],
        scratch_shapes=[pltpu.SMEM((N_EXPERTS,), jnp.int32), pltpu.VMEM((MOE_TILE * SUBLANES, LANES), F32),
                        pltpu.SemaphoreType.DMA(())])
    return pl.pallas_call(
        functools.partial(_dispatch_kernel, steps=steps), grid_spec=grid_spec,
        out_shape=[jax.ShapeDtypeStruct((n_tiles * MOE_TILE * SUBLANES, LANES), F32),
                   jax.ShapeDtypeStruct((n_tiles + 1,), jnp.int32)],
        compiler_params=_cparams("arbitrary"), name="moe_dispatch",
    )(slots[0], slots[1], counts, *hs)


def _ffn_kernel(info_ref, xs_ref, wg_ref, wu_ref, wd_ref, ys_ref, wg_s, wu_s, wd_s):
    i = pl.program_id(0)
    used = info_ref[info_ref.shape[0] - 1]
    fresh = jnp.logical_or(i == 0, info_ref[i] != info_ref[jnp.maximum(i - 1, 0)])

    @pl.when(jnp.logical_and(i < used, fresh))
    def _():
        wg_s[...] = wg_ref[0, 0].astype(BF16)
        wu_s[...] = wu_ref[0, 0].astype(BF16)
        wd_s[...] = wd_ref[0, 0].astype(BF16)

    @pl.when(i < used)
    def _():
        x = _tiles_to_rows(xs_ref, MOE_TILE).astype(BF16)
        hid = _silu(_dot(x, wg_s[...])) * _dot(x, wu_s[...])
        _rows_to_tiles(ys_ref, _dot(hid.astype(BF16), wd_s[...]))

    @pl.when(i >= used)
    def _():
        ys_ref[...] = jnp.zeros_like(ys_ref)


def _ffn(xs, info, wg, wu, wd, layer):
    tile_rows = MOE_TILE * SUBLANES
    n_tiles = xs.shape[0] // tile_rows
    d, f = wg.shape[2:]

    def w_map(i, info):
        return (layer, info[i], 0, 0)

    grid_spec = pltpu.PrefetchScalarGridSpec(
        num_scalar_prefetch=1, grid=(n_tiles,),
        in_specs=[pl.BlockSpec((tile_rows, LANES), lambda i, info: (jnp.minimum(i, info[n_tiles] - 1), 0)),
                  pl.BlockSpec((1, 1, d, f), w_map), pl.BlockSpec((1, 1, d, f), w_map),
                  pl.BlockSpec((1, 1, f, d), w_map)],
        out_specs=pl.BlockSpec((tile_rows, LANES), lambda i, info: (i, 0)),
        scratch_shapes=[pltpu.VMEM((d, f), BF16), pltpu.VMEM((d, f), BF16), pltpu.VMEM((f, d), BF16)])
    return pl.pallas_call(
        _ffn_kernel, grid_spec=grid_spec, out_shape=jax.ShapeDtypeStruct(xs.shape, F32),
        compiler_params=_cparams("arbitrary"), name="moe_ffn",
    )(info, xs, wg, wu, wd)


def _combine_kernel(*refs, has_mod, out_x):
    refs = list(refs)
    sa_ref, sb_ref, x_ref, ys_ref, wcol_ref, gate_ref, g_ref = refs[:7]
    refs = refs[7:]
    mod_refs = (refs.pop(0), refs.pop(0)) if has_mod else None
    xo_ref = refs.pop(0) if out_x else None
    h_ref, buf_a, buf_b, sems = refs
    i = pl.program_id(0)
    tm = x_ref.shape[0]

    def issue(tile, slot):
        base = tile * tm
        for t in range(tm):
            dst = pl.ds(t * SUBLANES, SUBLANES)
            src_a = pl.multiple_of(sa_ref[base + t] * SUBLANES, SUBLANES)
            src_b = pl.multiple_of(sb_ref[base + t] * SUBLANES, SUBLANES)
            pltpu.make_async_copy(ys_ref.at[pl.ds(src_a, SUBLANES), :], buf_a.at[slot, dst, :],
                                  sems.at[slot]).start(priority=0)
            pltpu.make_async_copy(ys_ref.at[pl.ds(src_b, SUBLANES), :], buf_b.at[slot, dst, :],
                                  sems.at[slot]).start(priority=1)

    @pl.when(i == 0)
    def _():
        issue(0, 0)

    @pl.when(i + 1 < pl.num_programs(0))
    def _():
        issue(i + 1, (i + 1) % 2)

    slot = i % 2
    for buf in (buf_a, buf_b):
        pltpu.make_async_copy(ys_ref.at[pl.ds(0, tm * SUBLANES), :], buf.at[slot], sems.at[slot]).wait()
    y = (wcol_ref[:, 0:1] * _tiles_to_rows(buf_a, tm, lead=(slot,))
         + wcol_ref[:, 1:2] * _tiles_to_rows(buf_b, tm, lead=(slot,)))
    x = x_ref[...] + gate_ref[0] * y
    if out_x:
        xo_ref[...] = x
    h_ref[...] = _norm_mod(x, g_ref, mod_refs).astype(h_ref.dtype)


def _combine(x, ys, slots, wcol, gate, g, *, mod=None, out_x=False, h_dtype=BF16):
    rows, d = x.shape
    tm = MOE_TOKENS
    row_spec = pl.BlockSpec((tm, d), lambda i, *_: (i, 0))
    args = [x, ys, wcol, gate, g.reshape(1, d)]
    specs = [row_spec, pl.BlockSpec(memory_space=pl.ANY), pl.BlockSpec((tm, LANES), lambda i, *_: (i, 0)),
             _mod_spec(gate.shape[0], rows, tm, d), pl.BlockSpec((1, d), lambda i, *_: (0, 0))]
    if mod is not None:
        for m in mod:
            args.append(m)
            specs.append(_mod_spec(m.shape[0], rows, tm, d))
    out_shape, out_specs = [], []
    if out_x:
        out_shape.append(jax.ShapeDtypeStruct((rows, d), F32))
        out_specs.append(row_spec)
    out_shape.append(jax.ShapeDtypeStruct((rows, d), h_dtype))
    out_specs.append(row_spec)
    grid_spec = pltpu.PrefetchScalarGridSpec(
        num_scalar_prefetch=2, grid=(rows // tm,), in_specs=specs, out_specs=out_specs,
        scratch_shapes=[pltpu.VMEM((2, tm * SUBLANES, LANES), F32), pltpu.VMEM((2, tm * SUBLANES, LANES), F32),
                        pltpu.SemaphoreType.DMA((2,))])
    outs = pl.pallas_call(
        functools.partial(_combine_kernel, has_mod=mod is not None, out_x=out_x), grid_spec=grid_spec,
        out_shape=out_shape, compiler_params=_cparams("arbitrary"), name="moe_combine",
    )(slots[0], slots[1], *args)
    return outs if out_x else outs[0]


def _softmax_av(scores, values, sink_col):
    m = sink_col
    for s in scores:
        m = jnp.maximum(m, jnp.max(s, axis=-1, keepdims=True))
    den = jnp.exp(sink_col - m)
    acc = None
    for s, v in zip(scores, values):
        p = jnp.exp(s - m)
        den = den + jnp.sum(p, axis=-1, keepdims=True)
        pv = _dot(p.astype(BF16), v)
        acc = pv if acc is None else acc + pv
    return acc / den


def _sink_column(sink_ref, kv, rows):
    return jnp.concatenate([jnp.full((rows, 1), sink_ref[kv * ATT_GROUP + g], F32) for g in range(ATT_GROUP)], axis=0)


def _attn_ctx_kernel(sink_ref, qkv_ref, o_ref):
    t = qkv_ref.shape[0]
    qw = ATT_HEADS * HEAD_DIM
    kw = ATT_KV * HEAD_DIM
    heads_out = []
    for kv in range(ATT_KV):
        q = jnp.concatenate(
            [qkv_ref[:, (kv * ATT_GROUP + g) * HEAD_DIM:(kv * ATT_GROUP + g + 1) * HEAD_DIM] for g in range(ATT_GROUP)],
            axis=0).astype(BF16)
        k = qkv_ref[:, qw + kv * HEAD_DIM:qw + (kv + 1) * HEAD_DIM].astype(BF16)
        v = qkv_ref[:, qw + kw + kv * HEAD_DIM:qw + kw + (kv + 1) * HEAD_DIM].astype(BF16)
        s = _dot_nt(q, k) * HEAD_DIM ** -0.5
        o = _softmax_av([s], [v], _sink_column(sink_ref, kv, t))
        heads_out += [o[g * t:(g + 1) * t] for g in range(ATT_GROUP)]
    o_ref[...] = jnp.concatenate(heads_out, axis=1).astype(o_ref.dtype)


def _attn_ctx(qkv, sink, n_seq, seq_len):
    rows, cols = qkv.shape
    return pl.pallas_call(
        _attn_ctx_kernel, grid=(n_seq,),
        in_specs=[pl.BlockSpec(memory_space=pltpu.SMEM), pl.BlockSpec((seq_len, cols), lambda b: (b, 0))],
        out_specs=pl.BlockSpec((seq_len, ATT_HEADS * HEAD_DIM), lambda b: (b, 0)),
        out_shape=jax.ShapeDtypeStruct((rows, ATT_HEADS * HEAD_DIM), BF16),
        compiler_params=_cparams("parallel"), name="attn_context",
    )(sink, qkv)


def _rope_block(x, cos, sin_signed):
    lane = lax.broadcasted_iota(jnp.int32, x.shape, 1)
    nf = HEAD_DIM // 4
    partner = jnp.where((lane % (2 * nf)) < nf, pltpu.roll(x, LANES - nf, axis=1), pltpu.roll(x, nf, axis=1))
    return x * cos + partner * sin_signed


def _attn_lat_kernel(sink_ref, qkv_ref, ck_ref, cv_ref, cos_ref, sin_ref, o_ref, k_scr):
    i = pl.program_id(1)
    t = qkv_ref.shape[0]
    qw = ATT_HEADS * HEAD_DIM
    kw = ATT_KV * HEAD_DIM
    span = Q_BLOCK + 2 * WINDOW

    @pl.when(i == 0)
    def _():
        for c in range(kw // LANES):
            blk = qkv_ref[:, qw + c * LANES:qw + (c + 1) * LANES]
            k_scr[:, c * LANES:(c + 1) * LANES] = _rope_block(blk, cos_ref[...], sin_ref[...]).astype(BF16)

    r0 = pl.multiple_of(i * Q_BLOCK, Q_BLOCK)
    ws = pl.multiple_of(jnp.clip(r0 - WINDOW, 0, t - span), Q_BLOCK)
    cos_q = cos_ref[pl.ds(r0, Q_BLOCK), :]
    sin_q = sin_ref[pl.ds(r0, Q_BLOCK), :]
    qpos = r0 + lax.broadcasted_iota(jnp.int32, (Q_BLOCK, span), 0)
    kpos = ws + lax.broadcasted_iota(jnp.int32, (Q_BLOCK, span), 1)
    band = jnp.abs(qpos - kpos) <= WINDOW
    band = jnp.concatenate([band] * ATT_GROUP, axis=0)
    heads_out = []
    for kv in range(ATT_KV):
        heads = []
        for g in range(ATT_GROUP):
            h = kv * ATT_GROUP + g
            c, half = divmod(h * HEAD_DIM, LANES)
            blk = _rope_block(qkv_ref[pl.ds(r0, Q_BLOCK), c * LANES:(c + 1) * LANES], cos_q, sin_q)
            heads.append(blk[:, half:half + HEAD_DIM])
        q = jnp.concatenate(heads, axis=0).astype(BF16)
        ck = ck_ref[0, :, kv * HEAD_DIM:(kv + 1) * HEAD_DIM].astype(BF16)
        cv = cv_ref[0, :, kv * HEAD_DIM:(kv + 1) * HEAD_DIM].astype(BF16)
        kwin = k_scr[pl.ds(ws, span), kv * HEAD_DIM:(kv + 1) * HEAD_DIM]
        vwin = qkv_ref[pl.ds(ws, span), qw + kw + kv * HEAD_DIM:qw + kw + (kv + 1) * HEAD_DIM].astype(BF16)
        s_ctx = _dot_nt(q, ck) * HEAD_DIM ** -0.5
        s_win = jnp.where(band, _dot_nt(q, kwin) * HEAD_DIM ** -0.5, -jnp.inf)
        o = _softmax_av([s_ctx, s_win], [cv, vwin], _sink_column(sink_ref, kv, Q_BLOCK))
        heads_out += [o[g * Q_BLOCK:(g + 1) * Q_BLOCK] for g in range(ATT_GROUP)]
    o_ref[...] = jnp.concatenate(heads_out, axis=1).astype(o_ref.dtype)


def _rope_tables(seq_len):
    pos = jnp.arange(seq_len, dtype=jnp.int32)
    row = (pos // GRID_W).astype(F32)
    col = (pos % GRID_W).astype(F32)
    nf = HEAD_DIM // 4
    inv = ROPE_BASE ** (-jnp.arange(nf, dtype=F32) / nf)
    ang_r = row[:, None] * inv[None, :]
    ang_c = col[:, None] * inv[None, :]
    cos_h = jnp.concatenate([jnp.cos(ang_r), jnp.cos(ang_r), jnp.cos(ang_c), jnp.cos(ang_c)], axis=1)
    sin_h = jnp.concatenate([-jnp.sin(ang_r), jnp.sin(ang_r), -jnp.sin(ang_c), jnp.sin(ang_c)], axis=1)
    reps = LANES // HEAD_DIM
    return jnp.tile(cos_h, (1, reps)), jnp.tile(sin_h, (1, reps))


def _attn_lat(qkv, cache_k, cache_v, sink, n_seq, seq_len):
    rows, cols = qkv.shape
    past = cache_k.shape[1]
    kw = ATT_KV * HEAD_DIM
    cos, sin = _rope_tables(seq_len)
    return pl.pallas_call(
        _attn_lat_kernel, grid=(n_seq, seq_len // Q_BLOCK),
        in_specs=[pl.BlockSpec(memory_space=pltpu.SMEM),
                  pl.BlockSpec((seq_len, cols), lambda b, i: (b, 0)),
                  pl.BlockSpec((1, past, kw), lambda b, i: (b, 0, 0)),
                  pl.BlockSpec((1, past, kw), lambda b, i: (b, 0, 0)),
                  pl.BlockSpec((seq_len, LANES), lambda b, i: (0, 0)),
                  pl.BlockSpec((seq_len, LANES), lambda b, i: (0, 0))],
        out_specs=pl.BlockSpec((Q_BLOCK, ATT_HEADS * HEAD_DIM), lambda b, i: (b * (seq_len // Q_BLOCK) + i, 0)),
        out_shape=jax.ShapeDtypeStruct((rows, ATT_HEADS * HEAD_DIM), BF16),
        scratch_shapes=[pltpu.VMEM((seq_len, kw), BF16)],
        compiler_params=_cparams("parallel", "arbitrary"), name="attn_latent",
    )(sink, qkv, cache_k.reshape(n_seq, past, kw), cache_v.reshape(n_seq, past, kw), cos, sin)


def _split_bf16(w):
    hi = w.astype(BF16)
    return hi, (w - hi.astype(F32)).astype(BF16)


def _ml_gates_kernel(h_ref, w_ref, wt_ref, b_ref, bt_ref, g_ref, gt_ref):
    h = h_ref[...]
    h_hi = h.astype(BF16)
    h_lo = (h - h_hi.astype(F32)).astype(BF16)
    ng = gt_ref.shape[0]
    by_hi = _dot(h_hi, w_ref[...])
    g_ref[...] = by_hi[:, :LANES] + by_hi[:, LANES:] + _dot(h_lo, w_ref[:, :LANES]) + b_ref[...]
    by_hi_t = _dot_nt(wt_ref[...], h_hi)
    gt_ref[...] = by_hi_t[:ng] + by_hi_t[ng:] + _dot_nt(wt_ref[:ng, :], h_lo) + bt_ref[...]


def _ml_gates(h, w_gates, b_gates, *, tm=512):
    rows, d = h.shape
    ng = w_gates.shape[1]
    w_hi, w_lo = _split_bf16(jnp.pad(w_gates, ((0, 0), (0, LANES - ng))))
    wt_hi, wt_lo = _split_bf16(w_gates.T)
    b_pad = jnp.pad(b_gates, (0, LANES - ng)).reshape(1, LANES)
    return pl.pallas_call(
        _ml_gates_kernel, grid=(rows // tm,),
        in_specs=[pl.BlockSpec((tm, d), lambda i: (i, 0)), pl.BlockSpec((d, 2 * LANES), lambda i: (0, 0)),
                  pl.BlockSpec((2 * ng, d), lambda i: (0, 0)), pl.BlockSpec((1, LANES), lambda i: (0, 0)),
                  pl.BlockSpec((ng, 1), lambda i: (0, 0))],
        out_specs=[pl.BlockSpec((tm, LANES), lambda i: (i, 0)), pl.BlockSpec((ng, tm), lambda i: (0, i))],
        out_shape=[jax.ShapeDtypeStruct((rows, LANES), F32), jax.ShapeDtypeStruct((ng, rows), F32)],
        compiler_params=_cparams("parallel"), name="mlstm_gates",
    )(h, jnp.concatenate([w_hi, w_lo], axis=1), jnp.concatenate([wt_hi, wt_lo], axis=0), b_pad,
      b_gates.reshape(ng, 1))


def _ml_qk_kernel(h_ref, w_ref, cw_ref, o_ref, *, seq_len, k_scale):
    j = pl.program_id(1)
    x = _dot(h_ref[...].astype(BF16), w_ref[...])
    t = x.shape[0]
    pos = lax.broadcasted_iota(jnp.int32, x.shape, 0) % seq_len
    prev = jnp.where(pos == 0, 0.0, pltpu.roll(x, 1, axis=0))
    nxt = jnp.where(pos == seq_len - 1, 0.0, pltpu.roll(x, t - 1, axis=0))
    y = prev * cw_ref[0:1, :] + x * cw_ref[1:2, :] + nxt * cw_ref[2:3, :]
    scale = jnp.where(j >= pl.num_programs(1) // 2, k_scale, 1.0).astype(F32)
    o_ref[...] = (_silu(y) * scale).astype(o_ref.dtype)


def _ml_qk(h, w_qk, conv_w, seq_len, *, tm=1024, tn=1024):
    rows, d = h.shape
    width = w_qk.shape[1]
    return pl.pallas_call(
        functools.partial(_ml_qk_kernel, seq_len=seq_len, k_scale=ML_DK ** -0.5), grid=(rows // tm, width // tn),
        in_specs=[pl.BlockSpec((tm, d), lambda i, j: (i, 0)), pl.BlockSpec((d, tn), lambda i, j: (0, j)),
                  pl.BlockSpec((3, tn), lambda i, j: (0, j))],
        out_specs=pl.BlockSpec((tm, tn), lambda i, j: (i, j)),
        out_shape=jax.ShapeDtypeStruct((rows, width), BF16),
        compiler_params=_cparams("parallel", "parallel"), name="mlstm_qk",
    )(h, w_qk, conv_w)


def _ml_scan_kernel(*refs, zero_init):
    refs = list(refs)
    dirs = [tuple(refs[0:5]), tuple(refs[5:10])]
    refs = refs[10:]
    if not zero_init:
        c0_ref, n0_ref, m0_ref = refs[:3]
        refs = refs[3:]
    hf_ref, hb_ref, c_ref, n_ref, m_ref = refs
    h_out = (hf_ref, hb_ref)
    c = pl.program_id(1)
    last = pl.num_programs(1) - 1

    @pl.when(c == 0)
    def _():
        if zero_init:
            c_ref[...] = jnp.zeros_like(c_ref)
            n_ref[...] = jnp.zeros_like(n_ref)
            m_ref[...] = jnp.zeros_like(m_ref)
        else:
            c_ref[...] = c0_ref[...]
            n_ref[...] = n0_ref[...]
            m_ref[...] = m0_ref[...]

    length = hf_ref.shape[0]
    ti = lax.broadcasted_iota(jnp.int32, (length, length), 0)
    si = lax.broadcasted_iota(jnp.int32, (length, length), 1)
    for d in range(2):
        q_ref, k_ref, v_ref, g_ref, gt_ref = dirs[d]
        causal = (ti >= si) if d == 0 else (ti <= si)
        tri = jnp.where(causal, 1.0, 0.0).astype(F32)
        f_col = _log_sigmoid(g_ref[...])
        f_row = _log_sigmoid(gt_ref[...])
        b_col = jnp.dot(tri, f_col, precision=HI, preferred_element_type=F32)
        b_row = _dot_nt(f_row, tri, precision=HI)
        edge = length - 1 if d == 0 else 0
        for h in range(ML_HEADS):
            ji = d * 2 * ML_HEADS + h
            jf = ji + ML_HEADS
            bc = b_col[:, jf:jf + 1]
            br = b_row[jf:jf + 1, :]
            i_row = gt_ref[ji:ji + 1, :]
            i_col = g_ref[:, ji:ji + 1]
            m_prev = m_ref[0, d, h][:, 0:1]
            q = q_ref[:, h * ML_DK:(h + 1) * ML_DK]
            k = k_ref[:, h * ML_DK:(h + 1) * ML_DK]
            v = v_ref[:, h * ML_DV:(h + 1) * ML_DV].astype(BF16)
            cst = c_ref[0, d, h]
            nst = n_ref[0, d, h]
            a_row = i_row - br
            amat = jnp.where(causal, a_row, -jnp.inf)
            u = jnp.maximum(m_prev, jnp.max(amat, axis=1, keepdims=True))
            qk = (_dot_nt(q, k) * jnp.exp(amat - u)).astype(BF16)
            sc = jnp.exp(m_prev - u)
            state_ext = jnp.concatenate([cst, jnp.broadcast_to(nst, (LANES, ML_DK))], axis=0).astype(BF16)
            v_ext = jnp.concatenate([v, jnp.ones((length, LANES), BF16)], axis=1)
            tot = sc * _dot_nt(q, state_ext) + _dot(qk, v_ext)
            inv = 1.0 / jnp.maximum(jnp.abs(tot[:, ML_DV:]), jnp.exp(-(bc + u)))
            h_out[d][:, h * ML_DV:(h + 1) * ML_DV] = tot[:, :ML_DV] * jnp.concatenate([inv] * (ML_DV // LANES), axis=1)
            b_last = br[:, edge:edge + 1]
            wlog_row = b_last + a_row
            m_new = jnp.maximum(b_last + m_prev, jnp.max(wlog_row, axis=1, keepdims=True))
            decay = jnp.exp(b_last + m_prev - m_new)
            ws_row = jnp.exp(wlog_row - m_new)
            ws_col = jnp.exp(b_last - bc + i_col - m_new)
            kw = (ws_col * k.astype(F32)).astype(BF16)
            c_ref[0, d, h] = decay * cst + _dot_tn(v, kw)
            n_ref[0, d, h] = decay * nst + _dot(jnp.broadcast_to(ws_row, (8, length)).astype(BF16), k)[0:1]
            m_ref[0, d, h] = jnp.broadcast_to(m_new, (1, ML_DK))


def _ml_scan(qk, p, g, gt, state, n_seq, seq_len):
    rows = qk.shape[0]
    length = min(ML_CHUNK, seq_len)
    nc = seq_len // length
    qw = ML_HEADS * ML_DK
    vw = ML_HEADS * ML_DV
    ng = gt.shape[0]

    def fwd(b, c):
        return b * nc + c

    def bwd(b, c):
        return b * nc + nc - 1 - c

    args, specs = [], []
    for pos in (fwd, bwd):
        args += [qk, qk, p, g, gt]
        specs += [pl.BlockSpec((length, qw), lambda b, c, pos=pos: (pos(b, c), 0)),
                  pl.BlockSpec((length, qw), lambda b, c, pos=pos: (pos(b, c), 1)),
                  pl.BlockSpec((length, vw), lambda b, c, pos=pos: (pos(b, c), 0)),
                  pl.BlockSpec((length, LANES), lambda b, c, pos=pos: (pos(b, c), 0)),
                  pl.BlockSpec((ng, length), lambda b, c, pos=pos: (0, pos(b, c)))]
    c_spec = pl.BlockSpec((1, 2, ML_HEADS, ML_DV, ML_DK), lambda b, c: (b, 0, 0, 0, 0))
    n_spec = pl.BlockSpec((1, 2, ML_HEADS, 1, ML_DK), lambda b, c: (b, 0, 0, 0, 0))
    zero_init = state is None
    if not zero_init:
        c0, n0, m0 = state
        args += [c0, n0.reshape(n_seq, 2, ML_HEADS, 1, ML_DK),
                 jnp.broadcast_to(m0[..., None, None], (n_seq, 2, ML_HEADS, 1, ML_DK))]
        specs += [c_spec, n_spec, n_spec]
    hf, hb, c_fin, n_fin, m_fin = pl.pallas_call(
        functools.partial(_ml_scan_kernel, zero_init=zero_init), grid=(n_seq, nc), in_specs=specs,
        out_specs=[pl.BlockSpec((length, vw), lambda b, c: (fwd(b, c), 0)),
                   pl.BlockSpec((length, vw), lambda b, c: (bwd(b, c), 0)), c_spec, n_spec, n_spec],
        out_shape=[jax.ShapeDtypeStruct((rows, vw), F32), jax.ShapeDtypeStruct((rows, vw), F32),
                   jax.ShapeDtypeStruct((n_seq, 2, ML_HEADS, ML_DV, ML_DK), F32),
                   jax.ShapeDtypeStruct((n_seq, 2, ML_HEADS, 1, ML_DK), F32),
                   jax.ShapeDtypeStruct((n_seq, 2, ML_HEADS, 1, ML_DK), F32)],
        compiler_params=_cparams("parallel", "arbitrary"), name="mlstm_scan",
    )(*args)
    return hf, hb, (c_fin, n_fin[:, :, :, 0, :], m_fin[:, :, :, 0, 0])


def _gla_scan_kernel(*refs, zero_init):
    refs = list(refs)
    dirs = [tuple(refs[0:4]), tuple(refs[4:8])]
    w2_ref, ba_ref = refs[8:10]
    refs = refs[10:]
    if not zero_init:
        s0_ref = refs.pop(0)
    of_ref, ob_ref, s_ref, st_scr, la_scr = refs
    o_out = (of_ref, ob_ref)
    c = pl.program_id(1)
    last = pl.num_programs(1) - 1
    kw = GLA_HEADS * GLA_DK
    n_sub = of_ref.shape[0] // GLA_SUB

    @pl.when(c == 0)
    def _():
        for d in range(2):
            for h in range(GLA_HEADS):
                st_scr[d, h] = jnp.zeros((GLA_DV, GLA_DK), F32) if zero_init else s0_ref[0, d, h].T

    for d in range(2):
        u = dirs[d][3][...].astype(BF16)
        z = _dot(u, w2_ref[:, d * kw:(d + 1) * kw]) + ba_ref[:, d * kw:(d + 1) * kw]
        la_scr[d] = _log_sigmoid(z) / GLA_TAU

    ti = lax.broadcasted_iota(jnp.int32, (GLA_SUB, GLA_SUB), 0)
    si = lax.broadcasted_iota(jnp.int32, (GLA_SUB, GLA_SUB), 1)
    s_lane = lax.broadcasted_iota(jnp.int32, (GLA_SUB, GLA_SUB), 1)

    def sub_chunk(j, carry):
        for d in range(2):
            q_ref, k_ref, v_ref, _ = dirs[d]
            r0 = pl.multiple_of((j if d == 0 else n_sub - 1 - j) * GLA_SUB, GLA_SUB)
            causal = (ti >= si) if d == 0 else (ti <= si)
            tri = jnp.where(causal, 1.0, 0.0).astype(F32)
            bc_all = jnp.dot(tri, la_scr[d, pl.ds(r0, GLA_SUB), :], precision=HI, preferred_element_type=F32)
            edge = GLA_SUB - 1 if d == 0 else 0
            for h in range(GLA_HEADS):
                bc = bc_all[:, h * GLA_DK:(h + 1) * GLA_DK]
                q = q_ref[pl.ds(r0, GLA_SUB), h * GLA_DK:(h + 1) * GLA_DK] * GLA_DK ** -0.5
                k = k_ref[pl.ds(r0, GLA_SUB), h * GLA_DK:(h + 1) * GLA_DK]
                v = v_ref[pl.ds(r0, GLA_SUB), h * GLA_DV:(h + 1) * GLA_DV].astype(BF16)
                bc2 = bc * LOG2E
                a = jnp.zeros((GLA_SUB, GLA_SUB), F32)
                for s in range(GLA_SUB):
                    decay = jnp.exp2(bc2 - bc2[s:s + 1, :])
                    col = jnp.sum(q * (k[s:s + 1, :] * decay), axis=1, keepdims=True)
                    a = jnp.where(s_lane == s, col, a)
                a = jnp.where(causal, a, 0.0)
                st = st_scr[d, h]
                o = _dot(a.astype(BF16), v) + _dot_nt((q * jnp.exp2(bc2)).astype(BF16), st.astype(BF16))
                o_out[d][pl.ds(r0, GLA_SUB), h * GLA_DV:(h + 1) * GLA_DV] = o
                b_last = bc2[edge:edge + 1, :]
                k_dec = (k * jnp.exp2(b_last - bc2)).astype(BF16)
                st_scr[d, h] = jnp.exp2(b_last) * st + _dot_tn(v, k_dec)
        return carry

    lax.fori_loop(0, n_sub, sub_chunk, 0)

    @pl.when(c == last)
    def _():
        for d in range(2):
            for h in range(GLA_HEADS):
                s_ref[0, d, h] = st_scr[d, h].T


def _gla_scan(p, u, w2, b_a, state, n_seq, seq_len):
    rows = p.shape[0]
    length = min(GLA_BLOCK, seq_len)
    nc = seq_len // length
    kw = GLA_HEADS * GLA_DK
    vw = GLA_HEADS * GLA_DV

    def fwd(b, c):
        return b * nc + c

    def bwd(b, c):
        return b * nc + nc - 1 - c

    args, specs = [], []
    for pos in (fwd, bwd):
        args += [p, p, p, u]
        specs += [pl.BlockSpec((length, kw), lambda b, c, pos=pos: (pos(b, c), 0)),
                  pl.BlockSpec((length, kw), lambda b, c, pos=pos: (pos(b, c), 1)),
                  pl.BlockSpec((length, vw), lambda b, c, pos=pos: (pos(b, c), 2 * kw // vw)),
                  pl.BlockSpec((length, LANES), lambda b, c, pos=pos: (pos(b, c), 0))]
    args += [w2, b_a]
    specs += [pl.BlockSpec(w2.shape, lambda b, c: (0, 0)), pl.BlockSpec(b_a.shape, lambda b, c: (0, 0))]
    s_spec = pl.BlockSpec((1, 2, GLA_HEADS, GLA_DK, GLA_DV), lambda b, c: (b, 0, 0, 0, 0))
    zero_init = state is None
    if not zero_init:
        args.append(state)
        specs.append(s_spec)
    return pl.pallas_call(
        functools.partial(_gla_scan_kernel, zero_init=zero_init), grid=(n_seq, nc), in_specs=specs,
        out_specs=[pl.BlockSpec((length, vw), lambda b, c: (fwd(b, c), 0)),
                   pl.BlockSpec((length, vw), lambda b, c: (bwd(b, c), 0)), s_spec],
        out_shape=[jax.ShapeDtypeStruct((rows, vw), F32), jax.ShapeDtypeStruct((rows, vw), F32),
                   jax.ShapeDtypeStruct((n_seq, 2, GLA_HEADS, GLA_DK, GLA_DV), F32)],
        scratch_shapes=[pltpu.VMEM((2, GLA_HEADS, GLA_DV, GLA_DK), F32), pltpu.VMEM((2, length, kw), F32)],
        compiler_params=_cparams("parallel", "arbitrary"), name="gla_scan",
    )(*args)


def kernel(x_prompt, x_sample, cache_k_0, cache_v_0, state_mlstm_C_1, state_mlstm_n_1, state_mlstm_m_1, state_gla_S_2, cache_k_3, cache_v_3, c, c_ctx, w_mod, b_mod, norm1_g, norm2_g, final_g, router_w, router_b, moe_wg, moe_wu, moe_wd, attn0_w_qkv, attn0_sink, attn0_w_o, mlstm1_w_in, mlstm1_b_gates, mlstm1_conv, mlstm1_norm_g, mlstm1_w_out, gla2_w_in, gla2_w_a1, gla2_w_a2, gla2_b_a, gla2_norm_g, gla2_w_out, attn3_w_qkv, attn3_sink, attn3_w_o):
    n_ctx, ctx_len, d = x_prompt.shape
    n_lat, lat_len, _ = x_sample.shape
    depth = w_mod.shape[0]

    cvec = jnp.concatenate([c_ctx[None, :], c, jnp.zeros((8 - 1 - n_lat, d), F32)], axis=0)
    mod = _modulation(cvec, w_mod, b_mod).reshape(depth, 8, 6, 1, d)

    def mods(layer, kind, latent):
        return mod[layer, 1:1 + n_lat, kind] if latent else mod[layer, 0:1, kind]

    rw_hi = router_w.T.astype(BF16)
    rw_lo = (router_w.T - rw_hi.astype(F32)).astype(BF16)
    rwt = jnp.concatenate([rw_hi, rw_lo], axis=0)
    rb = router_b.reshape(-1, 1)
    attn_w = {0: (attn0_w_qkv.astype(BF16), attn0_sink, attn0_w_o.astype(BF16), cache_k_0, cache_v_0),
              3: (attn3_w_qkv.astype(BF16), attn3_sink, attn3_w_o.astype(BF16), cache_k_3, cache_v_3)}
    ml_qw = ML_HEADS * ML_DK
    ml_vw = ML_HEADS * ML_DV
    ml_main = 2 * ml_qw + 2 * ml_vw
    ml_w_qk = mlstm1_w_in[:, :2 * ml_qw].astype(BF16)
    ml_w_vo = mlstm1_w_in[:, 2 * ml_qw:ml_main].astype(BF16)
    ml_w_gates = mlstm1_w_in[:, ml_main:]
    ml_w_out = mlstm1_w_out.astype(BF16)
    gla_kw = GLA_HEADS * GLA_DK
    gla_w_in = gla2_w_in.astype(BF16)
    gla_w_a1 = jnp.pad(jnp.concatenate([gla2_w_a1[0], gla2_w_a1[1]], axis=1),
                       ((0, 0), (0, LANES - 2 * GLA_RANK))).astype(BF16)
    gla_w2 = jnp.zeros((LANES, 2 * gla_kw), F32)
    gla_w2 = gla_w2.at[:GLA_RANK, :gla_kw].set(gla2_w_a2[0]).at[GLA_RANK:2 * GLA_RANK, gla_kw:].set(gla2_w_a2[1])
    gla_w2 = gla_w2.astype(BF16)
    gla_ba = gla2_b_a.reshape(1, 2 * gla_kw)
    gla_w_out = gla2_w_out.astype(BF16)

    new_state = []

    def mixer(layer, s, count0):
        latent, n_seq, seq_len, x, h = s["latent"], s["n_seq"], s["seq_len"], s["x"], s["h"]
        tail = (mods(layer, 2, latent), norm2_g[layer], mods(layer, 4, latent), mods(layer, 3, latent), rwt, rb,
                count0)
        kind = layer % 3
        if kind == 0:
            w_qkv, sink, w_o, ck, cv = attn_w[layer]
            qkv = _matmul(h, w_qkv)
            if latent:
                att = _attn_lat(qkv, ck, cv, sink, n_seq, seq_len)
            else:
                att = _attn_ctx(qkv, sink, n_seq, seq_len)
                qw = ATT_HEADS * HEAD_DIM
                kw = ATT_KV * HEAD_DIM
                new_state.append(qkv[:, qw:qw + kw].reshape(n_seq, seq_len, ATT_KV, HEAD_DIM))
                new_state.append(qkv[:, qw + kw:].reshape(n_seq, seq_len, ATT_KV, HEAD_DIM))
            return _proj("plain", (att,), w_o, x, *tail)
        if kind == 1:
            p = _matmul(h, ml_w_vo)
            g, gt = _ml_gates(h, ml_w_gates, mlstm1_b_gates)
            qk = _ml_qk(h, ml_w_qk, mlstm1_conv, seq_len)
            st = (state_mlstm_C_1, state_mlstm_n_1, state_mlstm_m_1) if latent else None
            hf, hb, fin = _ml_scan(qk, p, g, gt, st, n_seq, seq_len)
            if not latent:
                new_state.extend(fin)
            return _proj("mlstm", (hf, hb, p, 1, mlstm1_norm_g), ml_w_out, x, *tail)
        p = _matmul(h, gla_w_in)
        u = _matmul(h, gla_w_a1)
        of, ob, s_fin = _gla_scan(p, u, gla_w2, gla_ba, state_gla_S_2 if latent else None, n_seq, seq_len)
        if not latent:
            new_state.append(s_fin)
        gla_vw = GLA_HEADS * GLA_DV
        return _proj("gla", (of, ob, p, (2 * gla_kw + gla_vw) // gla_vw, gla2_norm_g), gla_w_out, x, *tail)

    streams = [dict(latent=False, n_seq=n_ctx, seq_len=ctx_len, x=x_prompt.reshape(n_ctx * ctx_len, d)),
               dict(latent=True, n_seq=n_lat, seq_len=lat_len, x=x_sample.reshape(n_lat * lat_len, d))]
    for s in streams:
        s["h"] = _rownorm(s["x"], norm1_g[0], mods(0, 1, s["latent"]), mods(0, 0, s["latent"]))
    for layer in range(depth):
        counts = jnp.zeros((N_EXPERTS, LANES), jnp.int32)
        for s in streams:
            s["x"], s["h2"], s["meta"], s["wcol"], counts = mixer(layer, s, counts)
        cnt = counts[:, 0]
        for s in streams:
            s["slots"] = _slots(s["meta"], cnt)
        xs, info = _dispatch([s["h2"] for s in streams], jnp.concatenate([s["slots"] for s in streams], axis=1), cnt)
        ys = _ffn(xs, info, moe_wg, moe_wu, moe_wd, layer)
        for s in streams:
            latent = s["latent"]
            gate2 = mods(layer, 5, latent)
            if layer + 1 < depth:
                s["x"], s["h"] = _combine(s["x"], ys, s["slots"], s["wcol"], gate2, norm1_g[layer + 1],
                                          mod=(mods(layer + 1, 1, latent), mods(layer + 1, 0, latent)), out_x=True,
                                          h_dtype=F32 if (layer + 1) % 3 == 1 else BF16)
            else:
                s["out"] = _combine(s["x"], ys, s["slots"], s["wcol"], gate2, final_g, h_dtype=F32)
    y_prompt = streams[0]["out"].reshape(n_ctx, ctx_len, d)
    y_sample = streams[1]["out"].reshape(n_lat, lat_len, d)
    return (y_prompt, y_sample, *new_state)
```

```python
import functools

import jax
import jax.numpy as jnp
from jax import lax
from jax.experimental import pallas as pl
from jax.experimental.pallas import tpu as pltpu

F32 = jnp.float32
BF16 = jnp.bfloat16
HI = lax.Precision.HIGHEST

EPS = 1e-6
LOG2E = 1.4426950408889634
GRID_W = 64
ATT_HEADS = 16
ATT_KV = 4
ATT_GROUP = ATT_HEADS // ATT_KV
HEAD_DIM = 64
WINDOW = 128
Q_BLOCK = 128
ROPE_BASE = 10000.0
ML_HEADS = 8
ML_DK = 128
ML_DV = 256
ML_CHUNK = 128
GLA_HEADS = 4
GLA_DK = 128
GLA_DV = 256
GLA_RANK = 16
GLA_TAU = 16.0
GLA_SUB = 16
GLA_BLOCK = 256
GLA_SEQS = 2
N_EXPERTS = 16
N_GROUPS = 4
GROUP_SIZE = N_EXPERTS // N_GROUPS
LANES = 128
VMEM_LIMIT = 56 * 1024 * 1024


def _cparams(*sem):
    return pltpu.CompilerParams(dimension_semantics=sem, vmem_limit_bytes=VMEM_LIMIT)


def _dot(a, b):
    return jnp.dot(a, b, preferred_element_type=F32)


def _dot_nt(a, b, precision=None):
    return lax.dot_general(a, b, (((1,), (1,)), ((), ())), precision=precision, preferred_element_type=F32)


def _dot_tn(a, b):
    return lax.dot_general(a, b, (((0,), (0,)), ((), ())), preferred_element_type=F32)


def _sigmoid(x):
    return 1.0 / (1.0 + jnp.exp(-x))


def _silu(x):
    return x * _sigmoid(x)


def _log_sigmoid(x):
    return jnp.minimum(x, 0.0) - jnp.log(1.0 + jnp.exp(-jnp.abs(x)))


def _rms_rows(x, g):
    ms = jnp.mean(x * x, axis=-1, keepdims=True)
    return x * lax.rsqrt(ms + EPS) * g


def _mod_kernel(c_ref, w_ref, b_ref, o_ref):
    s = _silu(c_ref[...])
    o_ref[0] = _dot(s.astype(BF16), w_ref[0].astype(BF16)) + b_ref[0]


def _modulation(cvec, w_mod, b_mod):
    depth, d, n6 = w_mod.shape
    tn = 1536
    return pl.pallas_call(
        _mod_kernel,
        grid=(depth, n6 // tn),
        in_specs=[pl.BlockSpec((8, d), lambda l, j: (0, 0)),
                  pl.BlockSpec((1, d, tn), lambda l, j: (l, 0, j)),
                  pl.BlockSpec((1, 1, tn), lambda l, j: (l, 0, j))],
        out_specs=pl.BlockSpec((1, 8, tn), lambda l, j: (l, 0, j)),
        out_shape=jax.ShapeDtypeStruct((depth, 8, n6), F32),
        compiler_params=_cparams("parallel", "parallel"),
        name="adaln_modulation",
    )(cvec, w_mod, b_mod.reshape(depth, 1, n6))


def _route(h, rwt, rb, carry):
    tm = h.shape[0]
    h_hi = h.astype(BF16)
    h_lo = (h - h_hi.astype(F32)).astype(BF16)
    by_hi = _dot_nt(rwt, h_hi)
    logits = by_hi[:N_EXPERTS] + by_hi[N_EXPERTS:] + _dot_nt(rwt[:N_EXPERTS], h_lo)
    scores = _sigmoid(logits)
    sel = scores + rb
    expert = lax.broadcasted_iota(jnp.int32, sel.shape, 0)
    pos = expert % GROUP_SIZE
    grp = expert // GROUP_SIZE

    def mate(x, k):
        ahead = pltpu.roll(x, N_EXPERTS - k, axis=0)
        behind = pltpu.roll(x, GROUP_SIZE - k, axis=0)
        return jnp.where(pos + k < GROUP_SIZE, ahead, behind)

    beaten = jnp.zeros_like(sel)
    for k in range(1, GROUP_SIZE):
        other = mate(sel, k)
        other_first = (pos + k) % GROUP_SIZE < pos
        beaten = beaten + jnp.where(other_first, jnp.where(other >= sel, 1.0, 0.0), jnp.where(other > sel, 1.0, 0.0))
    top2 = jnp.where(beaten < 2.0, sel, 0.0)
    gscore = top2
    for k in range(1, GROUP_SIZE):
        gscore = gscore + mate(top2, k)
    lost = jnp.zeros_like(sel)
    for k in range(1, N_GROUPS):
        other = pltpu.roll(gscore, N_EXPERTS - GROUP_SIZE * k, axis=0)
        other_first = (grp + k) % N_GROUPS < grp
        lost = lost + jnp.where(other_first, jnp.where(other >= gscore, 1.0, 0.0),
                                jnp.where(other > gscore, 1.0, 0.0))
    picked = jnp.where(lost < 0.5, jnp.where(beaten < 2.0, 1.0, 0.0), 0.0)
    chosen = picked > 0.5
    weight = jnp.where(chosen, scores, 0.0)
    wsum = jnp.sum(weight, axis=0, keepdims=True)
    e_f = expert.astype(F32)
    e_a = jnp.min(jnp.where(chosen, e_f, float(N_EXPERTS)), axis=0, keepdims=True)
    e_b = jnp.max(jnp.where(chosen, e_f, -1.0), axis=0, keepdims=True)
    before = (lax.broadcasted_iota(jnp.int32, (tm, tm), 0) < lax.broadcasted_iota(jnp.int32, (tm, tm), 1))
    rank = _dot(picked.astype(BF16), jnp.where(before, 1.0, 0.0).astype(BF16)) + carry
    is_a = e_f == e_a
    is_b = e_f == e_b
    r_a = jnp.sum(jnp.where(is_a, rank, 0.0), axis=0, keepdims=True)
    r_b = jnp.sum(jnp.where(is_b, rank, 0.0), axis=0, keepdims=True)
    w_a = jnp.sum(jnp.where(is_a, weight, 0.0), axis=0, keepdims=True)
    w_b = jnp.sum(jnp.where(is_b, weight, 0.0), axis=0, keepdims=True)
    meta = jnp.concatenate([e_a, e_b, r_a, r_b, jnp.zeros((4, tm), F32)], axis=0).astype(jnp.int32)
    wcol = jnp.concatenate([w_a / wsum, w_b / wsum, jnp.zeros((LANES - 2, tm), F32)], axis=0).T
    return meta, wcol, carry + jnp.sum(picked, axis=1, keepdims=True)


def _norm_mod(x, g_ref, mod_refs):
    h = _rms_rows(x, g_ref[...])
    if mod_refs is not None:
        a_ref, s_ref = mod_refs
        h = h * (1.0 + a_ref[0]) + s_ref[0]
    return h


def _rownorm_kernel(x_ref, g_ref, a_ref, s_ref, h_ref):
    h_ref[...] = _norm_mod(x_ref[...], g_ref, (a_ref, s_ref)).astype(h_ref.dtype)


def _mod_spec(n_mod, rows, tm, d, n_prefetch=0):
    per = (rows // n_mod) // tm
    return pl.BlockSpec((1, 1, d), lambda i, *_: (i // per, 0, 0))


def _rownorm(x, g, scale, shift, *, tm=512):
    rows, d = x.shape
    row_spec = pl.BlockSpec((tm, d), lambda i: (i, 0))
    return pl.pallas_call(
        _rownorm_kernel, grid=(rows // tm,),
        in_specs=[row_spec, pl.BlockSpec((1, d), lambda i: (0, 0)), _mod_spec(scale.shape[0], rows, tm, d),
                  _mod_spec(shift.shape[0], rows, tm, d)],
        out_specs=row_spec, out_shape=jax.ShapeDtypeStruct((rows, d), BF16),
        compiler_params=_cparams("parallel"), name="rownorm",
    )(x, g.reshape(1, d), scale, shift)


def _mm_kernel(a_ref, w_ref, o_ref):
    o_ref[...] = _dot(a_ref[...].astype(BF16), w_ref[...]).astype(o_ref.dtype)


def _matmul(a, w, *, out_dtype=F32, tm=1024):
    m, k = a.shape
    n = w.shape[1]
    tn = next(t for t in (1024, 768, 512, LANES) if n % t == 0)
    return pl.pallas_call(
        _mm_kernel, grid=(m // tm, n // tn),
        in_specs=[pl.BlockSpec((tm, k), lambda i, j: (i, 0)), pl.BlockSpec((k, tn), lambda i, j: (0, j))],
        out_specs=pl.BlockSpec((tm, tn), lambda i, j: (i, j)),
        out_shape=jax.ShapeDtypeStruct((m, n), out_dtype),
        compiler_params=_cparams("parallel", "parallel"), name="matmul",
    )(a, w)


def _head_norm(x, g, n_heads, dv):
    outs = []
    for h in range(n_heads):
        xs = x[:, h * dv:(h + 1) * dv]
        ms = jnp.mean(xs * xs, axis=-1, keepdims=True)
        outs.append(xs * lax.rsqrt(ms + EPS) * g[:, h * dv:(h + 1) * dv])
    return jnp.concatenate(outs, axis=1)


def _proj_kernel(*refs, pre):
    refs = list(refs)
    if pre == "plain":
        a = refs.pop(0)[...]
    else:
        f_ref, b_ref, p_ref, hg_ref = refs.pop(0), refs.pop(0), refs.pop(0), refs.pop(0)
        hsum = f_ref[...] + b_ref[...]
        if pre == "mlstm":
            a = _sigmoid(p_ref[...]) * _head_norm(hsum, hg_ref[...], ML_HEADS, ML_DV)
        else:
            a = _head_norm(hsum, hg_ref[...], GLA_HEADS, GLA_DV) * _silu(p_ref[...])
        a = a.astype(BF16)
    w_ref, x_ref, gate_ref, g_ref, a_ref, s_ref, rwt_ref, rb_ref, count0_ref = refs[:9]
    xo_ref, h_ref, meta_ref, wcol_ref, count_ref, carry_ref = refs[9:]

    @pl.when(pl.program_id(0) == 0)
    def _():
        carry_ref[...] = count0_ref[...].astype(F32)

    x = x_ref[...] + gate_ref[0] * _dot(a, w_ref[...])
    xo_ref[...] = x
    h = _norm_mod(x, g_ref, (a_ref, s_ref))
    _rows_to_tiles(h_ref, h)
    meta, wcol, carry = _route(h, rwt_ref[...], rb_ref[...], carry_ref[:, 0:1])
    meta_ref[...] = meta
    wcol_ref[...] = wcol
    carry_ref[...] = jnp.broadcast_to(carry, carry_ref.shape)
    count_ref[...] = jnp.broadcast_to(carry, count_ref.shape).astype(jnp.int32)


def _proj(pre, pre_args, w_out, x, gate, g, scale, shift, rwt, rb, count0, *, tm=512):
    rows, d = x.shape
    k = w_out.shape[0]
    row_spec = pl.BlockSpec((tm, d), lambda i: (i, 0))
    if pre == "plain":
        args, specs = [pre_args[0]], [pl.BlockSpec((tm, k), lambda i: (i, 0))]
    else:
        hf, hb, p, col_block, hg = pre_args
        wide = pl.BlockSpec((tm, k), lambda i: (i, 0))
        args = [hf, hb, p, hg.reshape(1, k)]
        specs = [wide, wide, pl.BlockSpec((tm, k), lambda i: (i, col_block)), pl.BlockSpec((1, k), lambda i: (0, 0))]
    args += [w_out, x, gate, g.reshape(1, d), scale, shift, rwt, rb, count0]
    specs += [pl.BlockSpec((k, d), lambda i: (0, 0)), row_spec, _mod_spec(gate.shape[0], rows, tm, d),
              pl.BlockSpec((1, d), lambda i: (0, 0)), _mod_spec(scale.shape[0], rows, tm, d),
              _mod_spec(shift.shape[0], rows, tm, d), pl.BlockSpec(rwt.shape, lambda i: (0, 0)),
              pl.BlockSpec(rb.shape, lambda i: (0, 0)), pl.BlockSpec(count0.shape, lambda i: (0, 0))]
    return pl.pallas_call(
        functools.partial(_proj_kernel, pre=pre), grid=(rows // tm,), in_specs=specs,
        out_specs=[row_spec, pl.BlockSpec((tm * SUBLANES, LANES), lambda i: (i, 0)),
                   pl.BlockSpec((8, tm), lambda i: (0, i)),
                   pl.BlockSpec((tm, LANES), lambda i: (i, 0)), pl.BlockSpec((N_EXPERTS, LANES), lambda i: (0, 0))],
        out_shape=[jax.ShapeDtypeStruct((rows, d), F32), jax.ShapeDtypeStruct((rows * SUBLANES, LANES), F32),
                   jax.ShapeDtypeStruct((8, rows), jnp.int32), jax.ShapeDtypeStruct((rows, LANES), F32),
                   jax.ShapeDtypeStruct((N_EXPERTS, LANES), jnp.int32)],
        scratch_shapes=[pltpu.VMEM((N_EXPERTS, LANES), F32)],
        compiler_params=_cparams("arbitrary"), name="proj_" + pre,
    )(*args)


MOE_TILE = 512
MOE_TILE_SHIFT = 9
MOE_TOKENS = 256
ROW_UNROLL = 8


SUBLANES = 8


def _rows_to_tiles(ref, x, lead=()):
    rows = x.shape[0]
    for c in range(SUBLANES):
        ref[(*lead, pl.ds(c, rows, stride=SUBLANES), slice(None))] = x[:, c * LANES:(c + 1) * LANES]


def _tiles_to_rows(ref, rows, lead=()):
    return jnp.concatenate([ref[(*lead, pl.ds(c, rows, stride=SUBLANES), slice(None))] for c in range(SUBLANES)],
                           axis=1)


def _slot_tiles(rows):
    return (2 * rows) // MOE_TILE + N_EXPERTS


def _expert_offsets(cnt_ref, off_ref):
    def per_expert(e, k):
        off_ref[e] = k * MOE_TILE
        return k + ((cnt_ref[e] + MOE_TILE - 1) >> MOE_TILE_SHIFT)
    return lax.fori_loop(0, N_EXPERTS, per_expert, 0)


def _slots_kernel(cnt_ref, meta_ref, slot_ref, off_ref):
    @pl.when(pl.program_id(0) == 0)
    def _():
        _expert_offsets(cnt_ref, off_ref)

    e_a, e_b = meta_ref[0:1, :], meta_ref[1:2, :]
    off_a = jnp.zeros_like(e_a)
    off_b = jnp.zeros_like(e_b)
    for e in range(N_EXPERTS):
        off_a = jnp.where(e_a == e, off_ref[e], off_a)
        off_b = jnp.where(e_b == e, off_ref[e], off_b)
    slot_ref[...] = jnp.concatenate([off_a + meta_ref[2:3, :], off_b + meta_ref[3:4, :],
                                     jnp.zeros((6, e_a.shape[1]), jnp.int32)], axis=0)


def _slots(meta, counts, *, tm=1024):
    rows = meta.shape[1]
    grid_spec = pltpu.PrefetchScalarGridSpec(
        num_scalar_prefetch=1, grid=(rows // tm,),
        in_specs=[pl.BlockSpec((8, tm), lambda i, cnt: (0, i))],
        out_specs=pl.BlockSpec((8, tm), lambda i, cnt: (0, i)),
        scratch_shapes=[pltpu.SMEM((N_EXPERTS,), jnp.int32)])
    return pl.pallas_call(
        _slots_kernel, grid_spec=grid_spec, out_shape=jax.ShapeDtypeStruct((8, rows), jnp.int32),
        compiler_params=_cparams("arbitrary"), name="moe_slots",
    )(counts, meta)


def _dispatch_kernel(*refs, steps):
    sa_ref, sb_ref, cnt_ref = refs[:3]
    h_refs = refs[3:3 + len(steps)]
    xs_ref, info_ref, off_ref, zero_ref, sem = refs[3 + len(steps):]
    i = pl.program_id(0)
    tm = h_refs[0].shape[0] // SUBLANES
    n_tiles = info_ref.shape[0] - 1
    tile_rows = MOE_TILE * SUBLANES

    def tile_copy(tile):
        return pltpu.make_async_copy(zero_ref, xs_ref.at[pl.ds(tile * tile_rows, tile_rows), :], sem)

    @pl.when(i == 0)
    def _():
        zero_ref[...] = jnp.zeros_like(zero_ref)
        used = _expert_offsets(cnt_ref, off_ref)

        def per_expert(e, _):
            first = off_ref[e] >> MOE_TILE_SHIFT
            nt = (cnt_ref[e] + MOE_TILE - 1) >> MOE_TILE_SHIFT

            def fill(j, _):
                info_ref[first + j] = e
                return 0
            lax.fori_loop(0, nt, fill, 0)

            @pl.when(nt > 0)
            def _():
                tile_copy(first + nt - 1).start()
                tile_copy(first + nt - 1).wait()
            return 0
        lax.fori_loop(0, N_EXPERTS, per_expert, 0)
        info_ref[n_tiles] = used

        def tail(j, _):
            info_ref[j] = N_EXPERTS - 1
            tile_copy(j).start()
            tile_copy(j).wait()
            return 0
        lax.fori_loop(used, n_tiles, tail, 0)

    base = i * tm

    def copy_rows(h_ref):
        def row_copy(t, slot):
            dst = pl.multiple_of(slot * SUBLANES, SUBLANES)
            return pltpu.make_async_copy(h_ref.at[pl.ds(t * SUBLANES, SUBLANES), :],
                                         xs_ref.at[pl.ds(dst, SUBLANES), :], sem)

        for t in range(tm):
            row_copy(t, sa_ref[base + t]).start(priority=0)
            row_copy(t, sb_ref[base + t]).start(priority=1)
        for _ in range(2):
            pltpu.make_async_copy(h_ref, xs_ref.at[pl.ds(0, tm * SUBLANES), :], sem).wait()

    first = 0
    for h_ref, n in zip(h_refs, steps):
        pl.when(jnp.logical_and(i >= first, i < first + n))(functools.partial(copy_rows, h_ref))
        first += n


def _dispatch(hs, slots, counts):
    tm = MOE_TOKENS
    steps = tuple(h.shape[0] // (tm * SUBLANES) for h in hs)
    n_tiles = _slot_tiles(sum(steps) * tm)
    specs, first = [], 0
    for n in steps:
        specs.append(pl.BlockSpec((tm * SUBLANES, LANES),
                                  lambda i, *_, first=first, n=n: (jnp.clip(i - first, 0, n - 1), 0)))
        first += n
    grid_spec = pltpu.PrefetchScalarGridSpec(
        num_scalar_prefetch=3, grid=(sum(steps),), in_specs=specs,
        out_specs=[pl.BlockSpec(memory_space=pl.ANY), pl.BlockSpec(memory_space=pltpu.SMEM)],
        scratch_shapes=[pltpu.SMEM((N_EXPERTS,), jnp.int32), pltpu.VMEM((MOE_TILE * SUBLANES, LANES), F32),
                        pltpu.SemaphoreType.DMA(())])
    return pl.pallas_call(
        functools.partial(_dispatch_kernel, steps=steps), grid_spec=grid_spec,
        out_shape=[jax.ShapeDtypeStruct((n_tiles * MOE_TILE * SUBLANES, LANES), F32),
                   jax.ShapeDtypeStruct((n_tiles + 1,), jnp.int32)],
        compiler_params=_cparams("arbitrary"), name="moe_dispatch",
    )(slots[0], slots[1], counts, *hs)


def _ffn_kernel(info_ref, xs_ref, wg_ref, wu_ref, wd_ref, ys_ref, wg_s, wu_s, wd_s):
    i = pl.program_id(0)
    used = info_ref[info_ref.shape[0] - 1]
    fresh = jnp.logical_or(i == 0, info_ref[i] != info_ref[jnp.maximum(i - 1, 0)])

    @pl.when(jnp.logical_and(i < used, fresh))
    def _():
        wg_s[...] = wg_ref[0, 0].astype(BF16)
        wu_s[...] = wu_ref[0, 0].astype(BF16)
        wd_s[...] = wd_ref[0, 0].astype(BF16)

    @pl.when(i < used)
    def _():
        x = _tiles_to_rows(xs_ref, MOE_TILE).astype(BF16)
        hid = _silu(_dot(x, wg_s[...])) * _dot(x, wu_s[...])
        _rows_to_tiles(ys_ref, _dot(hid.astype(BF16), wd_s[...]))

    @pl.when(i >= used)
    def _():
        ys_ref[...] = jnp.zeros_like(ys_ref)


def _ffn(xs, info, wg, wu, wd, layer):
    tile_rows = MOE_TILE * SUBLANES
    n_tiles = xs.shape[0] // tile_rows
    d, f = wg.shape[2:]

    def w_map(i, info):
        return (layer, info[i], 0, 0)

    grid_spec = pltpu.PrefetchScalarGridSpec(
        num_scalar_prefetch=1, grid=(n_tiles,),
        in_specs=[pl.BlockSpec((tile_rows, LANES), lambda i, info: (jnp.minimum(i, info[n_tiles] - 1), 0)),
                  pl.BlockSpec((1, 1, d, f), w_map), pl.BlockSpec((1, 1, d, f), w_map),
                  pl.BlockSpec((1, 1, f, d), w_map)],
        out_specs=pl.BlockSpec((tile_rows, LANES), lambda i, info: (i, 0)),
        scratch_shapes=[pltpu.VMEM((d, f), BF16), pltpu.VMEM((d, f), BF16), pltpu.VMEM((f, d), BF16)])
    return pl.pallas_call(
        _ffn_kernel, grid_spec=grid_spec, out_shape=jax.ShapeDtypeStruct(xs.shape, F32),
        compiler_params=_cparams("arbitrary"), name="moe_ffn",
    )(info, xs, wg, wu, wd)


def _combine_kernel(*refs, has_mod, out_x):
    refs = list(refs)
    sa_ref, sb_ref, x_ref, ys_ref, wcol_ref, gate_ref, g_ref = refs[:7]
    refs = refs[7:]
    mod_refs = (refs.pop(0), refs.pop(0)) if has_mod else None
    xo_ref = refs.pop(0) if out_x else None
    h_ref, buf_a, buf_b, sems = refs
    i = pl.program_id(0)
    tm = x_ref.shape[0]

    def issue(tile, slot):
        base = tile * tm
        for t in range(tm):
            dst = pl.ds(t * SUBLANES, SUBLANES)
            src_a = pl.multiple_of(sa_ref[base + t] * SUBLANES, SUBLANES)
            src_b = pl.multiple_of(sb_ref[base + t] * SUBLANES, SUBLANES)
            pltpu.make_async_copy(ys_ref.at[pl.ds(src_a, SUBLANES), :], buf_a.at[slot, dst, :],
                                  sems.at[slot]).start(priority=0)
            pltpu.make_async_copy(ys_ref.at[pl.ds(src_b, SUBLANES), :], buf_b.at[slot, dst, :],
                                  sems.at[slot]).start(priority=1)

    @pl.when(i == 0)
    def _():
        issue(0, 0)

    @pl.when(i + 1 < pl.num_programs(0))
    def _():
        issue(i + 1, (i + 1) % 2)

    slot = i % 2
    for buf in (buf_a, buf_b):
        pltpu.make_async_copy(ys_ref.at[pl.ds(0, tm * SUBLANES), :], buf.at[slot], sems.at[slot]).wait()
    y = (wcol_ref[:, 0:1] * _tiles_to_rows(buf_a, tm, lead=(slot,))
         + wcol_ref[:, 1:2] * _tiles_to_rows(buf_b, tm, lead=(slot,)))
    x = x_ref[...] + gate_ref[0] * y
    if out_x:
        xo_ref[...] = x
    h_ref[...] = _norm_mod(x, g_ref, mod_refs).astype(h_ref.dtype)


def _combine(x, ys, slots, wcol, gate, g, *, mod=None, out_x=False, h_dtype=BF16):
    rows, d = x.shape
    tm = MOE_TOKENS
    row_spec = pl.BlockSpec((tm, d), lambda i, *_: (i, 0))
    args = [x, ys, wcol, gate, g.reshape(1, d)]
    specs = [row_spec, pl.BlockSpec(memory_space=pl.ANY), pl.BlockSpec((tm, LANES), lambda i, *_: (i, 0)),
             _mod_spec(gate.shape[0], rows, tm, d), pl.BlockSpec((1, d), lambda i, *_: (0, 0))]
    if mod is not None:
        for m in mod:
            args.append(m)
            specs.append(_mod_spec(m.shape[0], rows, tm, d))
    out_shape, out_specs = [], []
    if out_x:
        out_shape.append(jax.ShapeDtypeStruct((rows, d), F32))
        out_specs.append(row_spec)
    out_shape.append(jax.ShapeDtypeStruct((rows, d), h_dtype))
    out_specs.append(row_spec)
    grid_spec = pltpu.PrefetchScalarGridSpec(
        num_scalar_prefetch=2, grid=(rows // tm,), in_specs=specs, out_specs=out_specs,
        scratch_shapes=[pltpu.VMEM((2, tm * SUBLANES, LANES), F32), pltpu.VMEM((2, tm * SUBLANES, LANES), F32),
                        pltpu.SemaphoreType.DMA((2,))])
    outs = pl.pallas_call(
        functools.partial(_combine_kernel, has_mod=mod is not None, out_x=out_x), grid_spec=grid_spec,
        out_shape=out_shape, compiler_params=_cparams("arbitrary"), name="moe_combine",
    )(slots[0], slots[1], *args)
    return outs if out_x else outs[0]


def _softmax_av(scores, values, sink_col):
    m = sink_col
    for s in scores:
        m = jnp.maximum(m, jnp.max(s, axis=-1, keepdims=True))
    den = jnp.exp(sink_col - m)
    acc = None
    for s, v in zip(scores, values):
        p = jnp.exp(s - m)
        den = den + jnp.sum(p, axis=-1, keepdims=True)
        pv = _dot(p.astype(BF16), v)
        acc = pv if acc is None else acc + pv
    return acc / den


def _sink_column(sink_ref, kv, rows):
    return jnp.concatenate([jnp.full((rows, 1), sink_ref[kv * ATT_GROUP + g], F32) for g in range(ATT_GROUP)], axis=0)


def _attn_ctx_kernel(sink_ref, qkv_ref, o_ref):
    t = qkv_ref.shape[0]
    qw = ATT_HEADS * HEAD_DIM
    kw = ATT_KV * HEAD_DIM
    heads_out = []
    for kv in range(ATT_KV):
        q = jnp.concatenate(
            [qkv_ref[:, (kv * ATT_GROUP + g) * HEAD_DIM:(kv * ATT_GROUP + g + 1) * HEAD_DIM] for g in range(ATT_GROUP)],
            axis=0).astype(BF16)
        k = qkv_ref[:, qw + kv * HEAD_DIM:qw + (kv + 1) * HEAD_DIM].astype(BF16)
        v = qkv_ref[:, qw + kw + kv * HEAD_DIM:qw + kw + (kv + 1) * HEAD_DIM].astype(BF16)
        s = _dot_nt(q, k) * HEAD_DIM ** -0.5
        o = _softmax_av([s], [v], _sink_column(sink_ref, kv, t))
        heads_out += [o[g * t:(g + 1) * t] for g in range(ATT_GROUP)]
    o_ref[...] = jnp.concatenate(heads_out, axis=1).astype(o_ref.dtype)


def _attn_ctx(qkv, sink, n_seq, seq_len):
    rows, cols = qkv.shape
    return pl.pallas_call(
        _attn_ctx_kernel, grid=(n_seq,),
        in_specs=[pl.BlockSpec(memory_space=pltpu.SMEM), pl.BlockSpec((seq_len, cols), lambda b: (b, 0))],
        out_specs=pl.BlockSpec((seq_len, ATT_HEADS * HEAD_DIM), lambda b: (b, 0)),
        out_shape=jax.ShapeDtypeStruct((rows, ATT_HEADS * HEAD_DIM), BF16),
        compiler_params=_cparams("parallel"), name="attn_context",
    )(sink, qkv)


def _rope_block(x, cos, sin_signed):
    lane = lax.broadcasted_iota(jnp.int32, x.shape, 1)
    nf = HEAD_DIM // 4
    partner = jnp.where((lane % (2 * nf)) < nf, pltpu.roll(x, LANES - nf, axis=1), pltpu.roll(x, nf, axis=1))
    return x * cos + partner * sin_signed


def _attn_lat_kernel(sink_ref, qkv_ref, ck_ref, cv_ref, cos_ref, sin_ref, o_ref, k_scr):
    i = pl.program_id(1)
    t = qkv_ref.shape[0]
    qw = ATT_HEADS * HEAD_DIM
    kw = ATT_KV * HEAD_DIM
    span = Q_BLOCK + 2 * WINDOW

    @pl.when(i == 0)
    def _():
        for c in range(kw // LANES):
            blk = qkv_ref[:, qw + c * LANES:qw + (c + 1) * LANES]
            k_scr[:, c * LANES:(c + 1) * LANES] = _rope_block(blk, cos_ref[...], sin_ref[...]).astype(BF16)

    r0 = pl.multiple_of(i * Q_BLOCK, Q_BLOCK)
    ws = pl.multiple_of(jnp.clip(r0 - WINDOW, 0, t - span), Q_BLOCK)
    cos_q = cos_ref[pl.ds(r0, Q_BLOCK), :]
    sin_q = sin_ref[pl.ds(r0, Q_BLOCK), :]
    qpos = r0 + lax.broadcasted_iota(jnp.int32, (Q_BLOCK, span), 0)
    kpos = ws + lax.broadcasted_iota(jnp.int32, (Q_BLOCK, span), 1)
    band = jnp.abs(qpos - kpos) <= WINDOW
    band = jnp.concatenate([band] * ATT_GROUP, axis=0)
    heads_out = []
    for kv in range(ATT_KV):
        heads = []
        for g in range(ATT_GROUP):
            h = kv * ATT_GROUP + g
            c, half = divmod(h * HEAD_DIM, LANES)
            blk = _rope_block(qkv_ref[pl.ds(r0, Q_BLOCK), c * LANES:(c + 1) * LANES], cos_q, sin_q)
            heads.append(blk[:, half:half + HEAD_DIM])
        q = jnp.concatenate(heads, axis=0).astype(BF16)
        ck = ck_ref[0, :, kv * HEAD_DIM:(kv + 1) * HEAD_DIM].astype(BF16)
        cv = cv_ref[0, :, kv * HEAD_DIM:(kv + 1) * HEAD_DIM].astype(BF16)
        kwin = k_scr[pl.ds(ws, span), kv * HEAD_DIM:(kv + 1) * HEAD_DIM]
        vwin = qkv_ref[pl.ds(ws, span), qw + kw + kv * HEAD_DIM:qw + kw + (kv + 1) * HEAD_DIM].astype(BF16)
        s_ctx = _dot_nt(q, ck) * HEAD_DIM ** -0.5
        s_win = jnp.where(band, _dot_nt(q, kwin) * HEAD_DIM ** -0.5, -jnp.inf)
        o = _softmax_av([s_ctx, s_win], [cv, vwin], _sink_column(sink_ref, kv, Q_BLOCK))
        heads_out += [o[g * Q_BLOCK:(g + 1) * Q_BLOCK] for g in range(ATT_GROUP)]
    o_ref[...] = jnp.concatenate(heads_out, axis=1).astype(o_ref.dtype)


def _rope_tables(seq_len):
    pos = jnp.arange(seq_len, dtype=jnp.int32)
    row = (pos // GRID_W).astype(F32)
    col = (pos % GRID_W).astype(F32)
    nf = HEAD_DIM // 4
    inv = ROPE_BASE ** (-jnp.arange(nf, dtype=F32) / nf)
    ang_r = row[:, None] * inv[None, :]
    ang_c = col[:, None] * inv[None, :]
    cos_h = jnp.concatenate([jnp.cos(ang_r), jnp.cos(ang_r), jnp.cos(ang_c), jnp.cos(ang_c)], axis=1)
    sin_h = jnp.concatenate([-jnp.sin(ang_r), jnp.sin(ang_r), -jnp.sin(ang_c), jnp.sin(ang_c)], axis=1)
    reps = LANES // HEAD_DIM
    return jnp.tile(cos_h, (1, reps)), jnp.tile(sin_h, (1, reps))


def _attn_lat(qkv, cache_k, cache_v, sink, n_seq, seq_len):
    rows, cols = qkv.shape
    past = cache_k.shape[1]
    kw = ATT_KV * HEAD_DIM
    cos, sin = _rope_tables(seq_len)
    return pl.pallas_call(
        _attn_lat_kernel, grid=(n_seq, seq_len // Q_BLOCK),
        in_specs=[pl.BlockSpec(memory_space=pltpu.SMEM),
                  pl.BlockSpec((seq_len, cols), lambda b, i: (b, 0)),
                  pl.BlockSpec((1, past, kw), lambda b, i: (b, 0, 0)),
                  pl.BlockSpec((1, past, kw), lambda b, i: (b, 0, 0)),
                  pl.BlockSpec((seq_len, LANES), lambda b, i: (0, 0)),
                  pl.BlockSpec((seq_len, LANES), lambda b, i: (0, 0))],
        out_specs=pl.BlockSpec((Q_BLOCK, ATT_HEADS * HEAD_DIM), lambda b, i: (b * (seq_len // Q_BLOCK) + i, 0)),
        out_shape=jax.ShapeDtypeStruct((rows, ATT_HEADS * HEAD_DIM), BF16),
        scratch_shapes=[pltpu.VMEM((seq_len, kw), BF16)],
        compiler_params=_cparams("parallel", "arbitrary"), name="attn_latent",
    )(sink, qkv, cache_k.reshape(n_seq, past, kw), cache_v.reshape(n_seq, past, kw), cos, sin)


def _split_bf16(w):
    hi = w.astype(BF16)
    return hi, (w - hi.astype(F32)).astype(BF16)


def _ml_gates_kernel(h_ref, w_ref, wt_ref, b_ref, bt_ref, g_ref, gt_ref):
    h = h_ref[...]
    h_hi = h.astype(BF16)
    h_lo = (h - h_hi.astype(F32)).astype(BF16)
    ng = gt_ref.shape[0]
    by_hi = _dot(h_hi, w_ref[...])
    g_ref[...] = by_hi[:, :LANES] + by_hi[:, LANES:] + _dot(h_lo, w_ref[:, :LANES]) + b_ref[...]
    by_hi_t = _dot_nt(wt_ref[...], h_hi)
    gt_ref[...] = by_hi_t[:ng] + by_hi_t[ng:] + _dot_nt(wt_ref[:ng, :], h_lo) + bt_ref[...]


def _ml_gates(h, w_gates, b_gates, *, tm=512):
    rows, d = h.shape
    ng = w_gates.shape[1]
    w_hi, w_lo = _split_bf16(jnp.pad(w_gates, ((0, 0), (0, LANES - ng))))
    wt_hi, wt_lo = _split_bf16(w_gates.T)
    b_pad = jnp.pad(b_gates, (0, LANES - ng)).reshape(1, LANES)
    return pl.pallas_call(
        _ml_gates_kernel, grid=(rows // tm,),
        in_specs=[pl.BlockSpec((tm, d), lambda i: (i, 0)), pl.BlockSpec((d, 2 * LANES), lambda i: (0, 0)),
                  pl.BlockSpec((2 * ng, d), lambda i: (0, 0)), pl.BlockSpec((1, LANES), lambda i: (0, 0)),
                  pl.BlockSpec((ng, 1), lambda i: (0, 0))],
        out_specs=[pl.BlockSpec((tm, LANES), lambda i: (i, 0)), pl.BlockSpec((ng, tm), lambda i: (0, i))],
        out_shape=[jax.ShapeDtypeStruct((rows, LANES), F32), jax.ShapeDtypeStruct((ng, rows), F32)],
        compiler_params=_cparams("parallel"), name="mlstm_gates",
    )(h, jnp.concatenate([w_hi, w_lo], axis=1), jnp.concatenate([wt_hi, wt_lo], axis=0), b_pad,
      b_gates.reshape(ng, 1))


def _ml_qk_kernel(h_ref, w_ref, cw_ref, o_ref, *, seq_len, k_scale):
    j = pl.program_id(1)
    x = _dot(h_ref[...].astype(BF16), w_ref[...])
    t = x.shape[0]
    pos = lax.broadcasted_iota(jnp.int32, x.shape, 0) % seq_len
    prev = jnp.where(pos == 0, 0.0, pltpu.roll(x, 1, axis=0))
    nxt = jnp.where(pos == seq_len - 1, 0.0, pltpu.roll(x, t - 1, axis=0))
    y = prev * cw_ref[0:1, :] + x * cw_ref[1:2, :] + nxt * cw_ref[2:3, :]
    scale = jnp.where(j >= pl.num_programs(1) // 2, k_scale, 1.0).astype(F32)
    o_ref[...] = (_silu(y) * scale).astype(o_ref.dtype)


def _ml_qk(h, w_qk, conv_w, seq_len, *, tm=1024, tn=1024):
    rows, d = h.shape
    width = w_qk.shape[1]
    return pl.pallas_call(
        functools.partial(_ml_qk_kernel, seq_len=seq_len, k_scale=ML_DK ** -0.5), grid=(rows // tm, width // tn),
        in_specs=[pl.BlockSpec((tm, d), lambda i, j: (i, 0)), pl.BlockSpec((d, tn), lambda i, j: (0, j)),
                  pl.BlockSpec((3, tn), lambda i, j: (0, j))],
        out_specs=pl.BlockSpec((tm, tn), lambda i, j: (i, j)),
        out_shape=jax.ShapeDtypeStruct((rows, width), BF16),
        compiler_params=_cparams("parallel", "parallel"), name="mlstm_qk",
    )(h, w_qk, conv_w)


def _ml_scan_kernel(*refs, zero_init):
    refs = list(refs)
    dirs = [tuple(refs[0:5]), tuple(refs[5:10])]
    refs = refs[10:]
    if not zero_init:
        c0_ref, n0_ref, m0_ref = refs[:3]
        refs = refs[3:]
    hf_ref, hb_ref, c_ref, n_ref, m_ref = refs
    h_out = (hf_ref, hb_ref)
    c = pl.program_id(1)
    last = pl.num_programs(1) - 1

    @pl.when(c == 0)
    def _():
        if zero_init:
            c_ref[...] = jnp.zeros_like(c_ref)
            n_ref[...] = jnp.zeros_like(n_ref)
            m_ref[...] = jnp.zeros_like(m_ref)
        else:
            c_ref[...] = c0_ref[...]
            n_ref[...] = n0_ref[...]
            m_ref[...] = m0_ref[...]

    length = hf_ref.shape[0]
    ti = lax.broadcasted_iota(jnp.int32, (length, length), 0)
    si = lax.broadcasted_iota(jnp.int32, (length, length), 1)
    for d in range(2):
        q_ref, k_ref, v_ref, g_ref, gt_ref = dirs[d]
        causal = (ti >= si) if d == 0 else (ti <= si)
        tri = jnp.where(causal, 1.0, 0.0).astype(F32)
        f_col = _log_sigmoid(g_ref[...])
        f_row = _log_sigmoid(gt_ref[...])
        b_col = jnp.dot(tri, f_col, precision=HI, preferred_element_type=F32)
        b_row = _dot_nt(f_row, tri, precision=HI)
        edge = length - 1 if d == 0 else 0
        for h in range(ML_HEADS):
            ji = d * 2 * ML_HEADS + h
            jf = ji + ML_HEADS
            bc = b_col[:, jf:jf + 1]
            br = b_row[jf:jf + 1, :]
            i_row = gt_ref[ji:ji + 1, :]
            i_col = g_ref[:, ji:ji + 1]
            m_prev = m_ref[0, d, h][:, 0:1]
            q = q_ref[:, h * ML_DK:(h + 1) * ML_DK]
            k = k_ref[:, h * ML_DK:(h + 1) * ML_DK]
            v = v_ref[:, h * ML_DV:(h + 1) * ML_DV].astype(BF16)
            cst = c_ref[0, d, h]
            nst = n_ref[0, d, h]
            a_row = i_row - br
            amat = jnp.where(causal, a_row, -jnp.inf)
            u = jnp.maximum(m_prev, jnp.max(amat, axis=1, keepdims=True))
            qk = (_dot_nt(q, k) * jnp.exp(amat - u)).astype(BF16)
            sc = jnp.exp(m_prev - u)
            state_ext = jnp.concatenate([cst, jnp.broadcast_to(nst, (LANES, ML_DK))], axis=0).astype(BF16)
            v_ext = jnp.concatenate([v, jnp.ones((length, LANES), BF16)], axis=1)
            tot = sc * _dot_nt(q, state_ext) + _dot(qk, v_ext)
            inv = 1.0 / jnp.maximum(jnp.abs(tot[:, ML_DV:]), jnp.exp(-(bc + u)))
            h_out[d][:, h * ML_DV:(h + 1) * ML_DV] = tot[:, :ML_DV] * jnp.concatenate([inv] * (ML_DV // LANES), axis=1)
            b_last = br[:, edge:edge + 1]
            wlog_row = b_last + a_row
            m_new = jnp.maximum(b_last + m_prev, jnp.max(wlog_row, axis=1, keepdims=True))
            decay = jnp.exp(b_last + m_prev - m_new)
            ws_row = jnp.exp(wlog_row - m_new)
            ws_col = jnp.exp(b_last - bc + i_col - m_new)
            kw = (ws_col * k.astype(F32)).astype(BF16)
            c_ref[0, d, h] = decay * cst + _dot_tn(v, kw)
            n_ref[0, d, h] = decay * nst + _dot(jnp.broadcast_to(ws_row, (8, length)).astype(BF16), k)[0:1]
            m_ref[0, d, h] = jnp.broadcast_to(m_new, (1, ML_DK))


def _ml_scan(qk, p, g, gt, state, n_seq, seq_len):
    rows = qk.shape[0]
    length = min(ML_CHUNK, seq_len)
    nc = seq_len // length
    qw = ML_HEADS * ML_DK
    vw = ML_HEADS * ML_DV
    ng = gt.shape[0]

    def fwd(b, c):
        return b * nc + c

    def bwd(b, c):
        return b * nc + nc - 1 - c

    args, specs = [], []
    for pos in (fwd, bwd):
        args += [qk, qk, p, g, gt]
        specs += [pl.BlockSpec((length, qw), lambda b, c, pos=pos: (pos(b, c), 0)),
                  pl.BlockSpec((length, qw), lambda b, c, pos=pos: (pos(b, c), 1)),
                  pl.BlockSpec((length, vw), lambda b, c, pos=pos: (pos(b, c), 0)),
                  pl.BlockSpec((length, LANES), lambda b, c, pos=pos: (pos(b, c), 0)),
                  pl.BlockSpec((ng, length), lambda b, c, pos=pos: (0, pos(b, c)))]
    c_spec = pl.BlockSpec((1, 2, ML_HEADS, ML_DV, ML_DK), lambda b, c: (b, 0, 0, 0, 0))
    n_spec = pl.BlockSpec((1, 2, ML_HEADS, 1, ML_DK), lambda b, c: (b, 0, 0, 0, 0))
    zero_init = state is None
    if not zero_init:
        c0, n0, m0 = state
        args += [c0, n0.reshape(n_seq, 2, ML_HEADS, 1, ML_DK),
                 jnp.broadcast_to(m0[..., None, None], (n_seq, 2, ML_HEADS, 1, ML_DK))]
        specs += [c_spec, n_spec, n_spec]
    hf, hb, c_fin, n_fin, m_fin = pl.pallas_call(
        functools.partial(_ml_scan_kernel, zero_init=zero_init), grid=(n_seq, nc), in_specs=specs,
        out_specs=[pl.BlockSpec((length, vw), lambda b, c: (fwd(b, c), 0)),
                   pl.BlockSpec((length, vw), lambda b, c: (bwd(b, c), 0)), c_spec, n_spec, n_spec],
        out_shape=[jax.ShapeDtypeStruct((rows, vw), F32), jax.ShapeDtypeStruct((rows, vw), F32),
                   jax.ShapeDtypeStruct((n_seq, 2, ML_HEADS, ML_DV, ML_DK), F32),
                   jax.ShapeDtypeStruct((n_seq, 2, ML_HEADS, 1, ML_DK), F32),
                   jax.ShapeDtypeStruct((n_seq, 2, ML_HEADS, 1, ML_DK), F32)],
        compiler_params=_cparams("parallel", "arbitrary"), name="mlstm_scan",
    )(*args)
    return hf, hb, (c_fin, n_fin[:, :, :, 0, :], m_fin[:, :, :, 0, 0])


def _gla_scan_kernel(*refs, zero_init, seqs):
    refs = list(refs)
    dirs = [tuple(refs[0:4]), tuple(refs[4:8])]
    w2_ref, ba_ref = refs[8:10]
    refs = refs[10:]
    if not zero_init:
        s0_ref = refs.pop(0)
    of_ref, ob_ref, s_ref, st_scr, la_scr = refs
    o_out = (of_ref, ob_ref)
    c = pl.program_id(1)
    last = pl.num_programs(1) - 1
    kw = GLA_HEADS * GLA_DK
    length = of_ref.shape[0] // seqs
    n_sub = length // GLA_SUB

    @pl.when(c == 0)
    def _():
        for b in range(seqs):
            for d in range(2):
                for h in range(GLA_HEADS):
                    st_scr[b, d, h] = jnp.zeros((GLA_DV, GLA_DK), F32) if zero_init else s0_ref[b, d, h].T

    for d in range(2):
        u = dirs[d][3][...].astype(BF16)
        z = _dot(u, w2_ref[:, d * kw:(d + 1) * kw]) + ba_ref[:, d * kw:(d + 1) * kw]
        la = _log_sigmoid(z) / GLA_TAU
        for b in range(seqs):
            la_scr[2 * b + d] = la[b * length:(b + 1) * length]

    ti = lax.broadcasted_iota(jnp.int32, (GLA_SUB, GLA_SUB), 0)
    si = lax.broadcasted_iota(jnp.int32, (GLA_SUB, GLA_SUB), 1)
    s_lane = lax.broadcasted_iota(jnp.int32, (GLA_SUB, GLA_SUB), 1)

    def sub_chunk(j, carry):
        for b in range(seqs):
            for d in range(2):
                q_ref, k_ref, v_ref, _ = dirs[d]
                rl = pl.multiple_of((j if d == 0 else n_sub - 1 - j) * GLA_SUB, GLA_SUB)
                r0 = b * length + rl
                causal = (ti >= si) if d == 0 else (ti <= si)
                tri = jnp.where(causal, 1.0, 0.0).astype(F32)
                bc_all = jnp.dot(tri, la_scr[2 * b + d, pl.ds(rl, GLA_SUB), :], precision=HI,
                                 preferred_element_type=F32)
                edge = GLA_SUB - 1 if d == 0 else 0
                for h in range(GLA_HEADS):
                    bc = bc_all[:, h * GLA_DK:(h + 1) * GLA_DK]
                    q = q_ref[pl.ds(r0, GLA_SUB), h * GLA_DK:(h + 1) * GLA_DK] * GLA_DK ** -0.5
                    k = k_ref[pl.ds(r0, GLA_SUB), h * GLA_DK:(h + 1) * GLA_DK]
                    v = v_ref[pl.ds(r0, GLA_SUB), h * GLA_DV:(h + 1) * GLA_DV].astype(BF16)
                    bc2 = bc * LOG2E
                    a = jnp.zeros((GLA_SUB, GLA_SUB), F32)
                    for s in range(GLA_SUB):
                        decay = jnp.exp2(bc2 - bc2[s:s + 1, :])
                        col = jnp.sum(q * (k[s:s + 1, :] * decay), axis=1, keepdims=True)
                        a = jnp.where(s_lane == s, col, a)
                    a = jnp.where(causal, a, 0.0)
                    st = st_scr[b, d, h]
                    o = _dot(a.astype(BF16), v) + _dot_nt((q * jnp.exp2(bc2)).astype(BF16), st.astype(BF16))
                    o_out[d][pl.ds(r0, GLA_SUB), h * GLA_DV:(h + 1) * GLA_DV] = o
                    b_last = bc2[edge:edge + 1, :]
                    k_dec = (k * jnp.exp2(b_last - bc2)).astype(BF16)
                    st_scr[b, d, h] = jnp.exp2(b_last) * st + _dot_tn(v, k_dec)
        return carry

    lax.fori_loop(0, n_sub, sub_chunk, 0, unroll=max(1, GLA_SEQS // seqs))

    @pl.when(c == last)
    def _():
        for b in range(seqs):
            for d in range(2):
                for h in range(GLA_HEADS):
                    s_ref[b, d, h] = st_scr[b, d, h].T


def _gla_scan(p, u, w2, b_a, state, n_seq, seq_len):
    rows = p.shape[0]
    length = min(GLA_BLOCK, seq_len)
    nc = seq_len // length
    seqs = GLA_SEQS if (nc == 1 and n_seq % GLA_SEQS == 0) else 1
    kw = GLA_HEADS * GLA_DK
    vw = GLA_HEADS * GLA_DV
    blk = seqs * length

    def fwd(b, c):
        return b * nc + c

    def bwd(b, c):
        return b * nc + nc - 1 - c

    args, specs = [], []
    for pos in (fwd, bwd):
        args += [p, p, p, u]
        specs += [pl.BlockSpec((blk, kw), lambda b, c, pos=pos: (pos(b, c), 0)),
                  pl.BlockSpec((blk, kw), lambda b, c, pos=pos: (pos(b, c), 1)),
                  pl.BlockSpec((blk, vw), lambda b, c, pos=pos: (pos(b, c), 2 * kw // vw)),
                  pl.BlockSpec((blk, LANES), lambda b, c, pos=pos: (pos(b, c), 0))]
    args += [w2, b_a]
    specs += [pl.BlockSpec(w2.shape, lambda b, c: (0, 0)), pl.BlockSpec(b_a.shape, lambda b, c: (0, 0))]
    s_spec = pl.BlockSpec((seqs, 2, GLA_HEADS, GLA_DK, GLA_DV), lambda b, c: (b, 0, 0, 0, 0))
    zero_init = state is None
    if not zero_init:
        args.append(state)
        specs.append(s_spec)
    return pl.pallas_call(
        functools.partial(_gla_scan_kernel, zero_init=zero_init, seqs=seqs), grid=(n_seq // seqs, nc),
        in_specs=specs,
        out_specs=[pl.BlockSpec((blk, vw), lambda b, c: (fwd(b, c), 0)),
                   pl.BlockSpec((blk, vw), lambda b, c: (bwd(b, c), 0)), s_spec],
        out_shape=[jax.ShapeDtypeStruct((rows, vw), F32), jax.ShapeDtypeStruct((rows, vw), F32),
                   jax.ShapeDtypeStruct((n_seq, 2, GLA_HEADS, GLA_DK, GLA_DV), F32)],
        scratch_shapes=[pltpu.VMEM((seqs, 2, GLA_HEADS, GLA_DV, GLA_DK), F32),
                        pltpu.VMEM((2 * seqs, length, kw), F32)],
        compiler_params=_cparams("parallel", "arbitrary"), name="gla_scan",
    )(*args)


def kernel(x_prompt, x_sample, cache_k_0, cache_v_0, state_mlstm_C_1, state_mlstm_n_1, state_mlstm_m_1, state_gla_S_2, cache_k_3, cache_v_3, c, c_ctx, w_mod, b_mod, norm1_g, norm2_g, final_g, router_w, router_b, moe_wg, moe_wu, moe_wd, attn0_w_qkv, attn0_sink, attn0_w_o, mlstm1_w_in, mlstm1_b_gates, mlstm1_conv, mlstm1_norm_g, mlstm1_w_out, gla2_w_in, gla2_w_a1, gla2_w_a2, gla2_b_a, gla2_norm_g, gla2_w_out, attn3_w_qkv, attn3_sink, attn3_w_o):
    n_ctx, ctx_len, d = x_prompt.shape
    n_lat, lat_len, _ = x_sample.shape
    depth = w_mod.shape[0]

    cvec = jnp.concatenate([c_ctx[None, :], c, jnp.zeros((8 - 1 - n_lat, d), F32)], axis=0)
    mod = _modulation(cvec, w_mod, b_mod).reshape(depth, 8, 6, 1, d)

    def mods(layer, kind, latent):
        return mod[layer, 1:1 + n_lat, kind] if latent else mod[layer, 0:1, kind]

    rw_hi = router_w.T.astype(BF16)
    rw_lo = (router_w.T - rw_hi.astype(F32)).astype(BF16)
    rwt = jnp.concatenate([rw_hi, rw_lo], axis=0)
    rb = router_b.reshape(-1, 1)
    attn_w = {0: (attn0_w_qkv.astype(BF16), attn0_sink, attn0_w_o.astype(BF16), cache_k_0, cache_v_0),
              3: (attn3_w_qkv.astype(BF16), attn3_sink, attn3_w_o.astype(BF16), cache_k_3, cache_v_3)}
    ml_qw = ML_HEADS * ML_DK
    ml_vw = ML_HEADS * ML_DV
    ml_main = 2 * ml_qw + 2 * ml_vw
    ml_w_qk = mlstm1_w_in[:, :2 * ml_qw].astype(BF16)
    ml_w_vo = mlstm1_w_in[:, 2 * ml_qw:ml_main].astype(BF16)
    ml_w_gates = mlstm1_w_in[:, ml_main:]
    ml_w_out = mlstm1_w_out.astype(BF16)
    gla_kw = GLA_HEADS * GLA_DK
    gla_w_in = gla2_w_in.astype(BF16)
    gla_w_a1 = jnp.pad(jnp.concatenate([gla2_w_a1[0], gla2_w_a1[1]], axis=1),
                       ((0, 0), (0, LANES - 2 * GLA_RANK))).astype(BF16)
    gla_w2 = jnp.zeros((LANES, 2 * gla_kw), F32)
    gla_w2 = gla_w2.at[:GLA_RANK, :gla_kw].set(gla2_w_a2[0]).at[GLA_RANK:2 * GLA_RANK, gla_kw:].set(gla2_w_a2[1])
    gla_w2 = gla_w2.astype(BF16)
    gla_ba = gla2_b_a.reshape(1, 2 * gla_kw)
    gla_w_out = gla2_w_out.astype(BF16)

    new_state = []

    def mixer(layer, s, count0):
        latent, n_seq, seq_len, x, h = s["latent"], s["n_seq"], s["seq_len"], s["x"], s["h"]
        tail = (mods(layer, 2, latent), norm2_g[layer], mods(layer, 4, latent), mods(layer, 3, latent), rwt, rb,
                count0)
        kind = layer % 3
        if kind == 0:
            w_qkv, sink, w_o, ck, cv = attn_w[layer]
            qkv = _matmul(h, w_qkv)
            if latent:
                att = _attn_lat(qkv, ck, cv, sink, n_seq, seq_len)
            else:
                att = _attn_ctx(qkv, sink, n_seq, seq_len)
                qw = ATT_HEADS * HEAD_DIM
                kw = ATT_KV * HEAD_DIM
                new_state.append(qkv[:, qw:qw + kw].reshape(n_seq, seq_len, ATT_KV, HEAD_DIM))
                new_state.append(qkv[:, qw + kw:].reshape(n_seq, seq_len, ATT_KV, HEAD_DIM))
            return _proj("plain", (att,), w_o, x, *tail)
        if kind == 1:
            p = _matmul(h, ml_w_vo)
            g, gt = _ml_gates(h, ml_w_gates, mlstm1_b_gates)
            qk = _ml_qk(h, ml_w_qk, mlstm1_conv, seq_len)
            st = (state_mlstm_C_1, state_mlstm_n_1, state_mlstm_m_1) if latent else None
            hf, hb, fin = _ml_scan(qk, p, g, gt, st, n_seq, seq_len)
            if not latent:
                new_state.extend(fin)
            return _proj("mlstm", (hf, hb, p, 1, mlstm1_norm_g), ml_w_out, x, *tail)
        p = _matmul(h, gla_w_in)
        u = _matmul(h, gla_w_a1)
        of, ob, s_fin = _gla_scan(p, u, gla_w2, gla_ba, state_gla_S_2 if latent else None, n_seq, seq_len)
        if not latent:
            new_state.append(s_fin)
        gla_vw = GLA_HEADS * GLA_DV
        return _proj("gla", (of, ob, p, (2 * gla_kw + gla_vw) // gla_vw, gla2_norm_g), gla_w_out, x, *tail)

    streams = [dict(latent=False, n_seq=n_ctx, seq_len=ctx_len, x=x_prompt.reshape(n_ctx * ctx_len, d)),
               dict(latent=True, n_seq=n_lat, seq_len=lat_len, x=x_sample.reshape(n_lat * lat_len, d))]
    for s in streams:
        s["h"] = _rownorm(s["x"], norm1_g[0], mods(0, 1, s["latent"]), mods(0, 0, s["latent"]))
    for layer in range(depth):
        counts = jnp.zeros((N_EXPERTS, LANES), jnp.int32)
        for s in streams:
            s["x"], s["h2"], s["meta"], s["wcol"], counts = mixer(layer, s, counts)
        cnt = counts[:, 0]
        for s in streams:
            s["slots"] = _slots(s["meta"], cnt)
        xs, info = _dispatch([s["h2"] for s in streams], jnp.concatenate([s["slots"] for s in streams], axis=1), cnt)
        ys = _ffn(xs, info, moe_wg, moe_wu, moe_wd, layer)
        for s in streams:
            latent = s["latent"]
            gate2 = mods(layer, 5, latent)
            if layer + 1 < depth:
                s["x"], s["h"] = _combine(s["x"], ys, s["slots"], s["wcol"], gate2, norm1_g[layer + 1],
                                          mod=(mods(layer + 1, 1, latent), mods(layer + 1, 0, latent)), out_x=True,
                                          h_dtype=F32 if (layer + 1) % 3 == 1 else BF16)
            else:
                s["out"] = _combine(s["x"], ys, s["slots"], s["wcol"], gate2, final_g, h_dtype=F32)
    y_prompt = streams[0]["out"].reshape(n_ctx, ctx_len, d)
    y_sample = streams[1]["out"].reshape(n_lat, lat_len, d)
    return (y_prompt, y_sample, *new_state)
```

```python
import functools

import jax
import jax.numpy as jnp
from jax import lax
from jax.experimental import pallas as pl
from jax.experimental.pallas import tpu as pltpu

F32 = jnp.float32
BF16 = jnp.bfloat16
HI = lax.Precision.HIGHEST

EPS = 1e-6
LOG2E = 1.4426950408889634
GRID_W = 64
ATT_HEADS = 16
ATT_KV = 4
ATT_GROUP = ATT_HEADS // ATT_KV
HEAD_DIM = 64
WINDOW = 128
Q_BLOCK = 128
ROPE_BASE = 10000.0
ML_HEADS = 8
ML_DK = 128
ML_DV = 256
ML_CHUNK = 128
GLA_HEADS = 4
GLA_DK = 128
GLA_DV = 256
GLA_RANK = 16
GLA_TAU = 16.0
GLA_SUB = 16
GLA_BLOCK = 256
N_EXPERTS = 16
N_GROUPS = 4
GROUP_SIZE = N_EXPERTS // N_GROUPS
LANES = 128
VMEM_LIMIT = 56 * 1024 * 1024


def _cparams(*sem):
    return pltpu.CompilerParams(dimension_semantics=sem, vmem_limit_bytes=VMEM_LIMIT)


def _dot(a, b):
    return jnp.dot(a, b, preferred_element_type=F32)


def _dot_nt(a, b, precision=None):
    return lax.dot_general(a, b, (((1,), (1,)), ((), ())), precision=precision, preferred_element_type=F32)


def _dot_tn(a, b):
    return lax.dot_general(a, b, (((0,), (0,)), ((), ())), preferred_element_type=F32)


def _sigmoid(x):
    return 1.0 / (1.0 + jnp.exp(-x))


def _silu(x):
    return x * _sigmoid(x)


def _log_sigmoid(x):
    return jnp.minimum(x, 0.0) - jnp.log(1.0 + jnp.exp(-jnp.abs(x)))


def _rms_rows(x, g):
    ms = jnp.mean(x * x, axis=-1, keepdims=True)
    return x * lax.rsqrt(ms + EPS) * g


def _mod_kernel(c_ref, w_ref, b_ref, o_ref):
    s = _silu(c_ref[...])
    o_ref[0] = _dot(s.astype(BF16), w_ref[0].astype(BF16)) + b_ref[0]


def _modulation(cvec, w_mod, b_mod):
    depth, d, n6 = w_mod.shape
    tn = 1536
    return pl.pallas_call(
        _mod_kernel,
        grid=(depth, n6 // tn),
        in_specs=[pl.BlockSpec((8, d), lambda l, j: (0, 0)),
                  pl.BlockSpec((1, d, tn), lambda l, j: (l, 0, j)),
                  pl.BlockSpec((1, 1, tn), lambda l, j: (l, 0, j))],
        out_specs=pl.BlockSpec((1, 8, tn), lambda l, j: (l, 0, j)),
        out_shape=jax.ShapeDtypeStruct((depth, 8, n6), F32),
        compiler_params=_cparams("parallel", "parallel"),
        name="adaln_modulation",
    )(cvec, w_mod, b_mod.reshape(depth, 1, n6))


def _route(h, rwt, rb, carry):
    tm = h.shape[0]
    h_hi = h.astype(BF16)
    h_lo = (h - h_hi.astype(F32)).astype(BF16)
    by_hi = _dot_nt(rwt, h_hi)
    logits = by_hi[:N_EXPERTS] + by_hi[N_EXPERTS:] + _dot_nt(rwt[:N_EXPERTS], h_lo)
    scores = _sigmoid(logits)
    sel = scores + rb
    expert = lax.broadcasted_iota(jnp.int32, sel.shape, 0)
    pos = expert % GROUP_SIZE
    grp = expert // GROUP_SIZE

    def mate(x, k):
        ahead = pltpu.roll(x, N_EXPERTS - k, axis=0)
        behind = pltpu.roll(x, GROUP_SIZE - k, axis=0)
        return jnp.where(pos + k < GROUP_SIZE, ahead, behind)

    beaten = jnp.zeros_like(sel)
    for k in range(1, GROUP_SIZE):
        other = mate(sel, k)
        other_first = (pos + k) % GROUP_SIZE < pos
        beaten = beaten + jnp.where(other_first, jnp.where(other >= sel, 1.0, 0.0), jnp.where(other > sel, 1.0, 0.0))
    top2 = jnp.where(beaten < 2.0, sel, 0.0)
    gscore = top2
    for k in range(1, GROUP_SIZE):
        gscore = gscore + mate(top2, k)
    lost = jnp.zeros_like(sel)
    for k in range(1, N_GROUPS):
        other = pltpu.roll(gscore, N_EXPERTS - GROUP_SIZE * k, axis=0)
        other_first = (grp + k) % N_GROUPS < grp
        lost = lost + jnp.where(other_first, jnp.where(other >= gscore, 1.0, 0.0),
                                jnp.where(other > gscore, 1.0, 0.0))
    picked = jnp.where(lost < 0.5, jnp.where(beaten < 2.0, 1.0, 0.0), 0.0)
    chosen = picked > 0.5
    weight = jnp.where(chosen, scores, 0.0)
    wsum = jnp.sum(weight, axis=0, keepdims=True)
    e_f = expert.astype(F32)
    e_a = jnp.min(jnp.where(chosen, e_f, float(N_EXPERTS)), axis=0, keepdims=True)
    e_b = jnp.max(jnp.where(chosen, e_f, -1.0), axis=0, keepdims=True)
    before = (lax.broadcasted_iota(jnp.int32, (tm, tm), 0) < lax.broadcasted_iota(jnp.int32, (tm, tm), 1))
    rank = _dot(picked.astype(BF16), jnp.where(before, 1.0, 0.0).astype(BF16)) + carry
    is_a = e_f == e_a
    is_b = e_f == e_b
    r_a = jnp.sum(jnp.where(is_a, rank, 0.0), axis=0, keepdims=True)
    r_b = jnp.sum(jnp.where(is_b, rank, 0.0), axis=0, keepdims=True)
    w_a = jnp.sum(jnp.where(is_a, weight, 0.0), axis=0, keepdims=True)
    w_b = jnp.sum(jnp.where(is_b, weight, 0.0), axis=0, keepdims=True)
    meta = jnp.concatenate([e_a, e_b, r_a, r_b, jnp.zeros((4, tm), F32)], axis=0).astype(jnp.int32)
    wcol = jnp.concatenate([w_a / wsum, w_b / wsum, jnp.zeros((LANES - 2, tm), F32)], axis=0).T
    return meta, wcol, carry + jnp.sum(picked, axis=1, keepdims=True)


def _norm_mod(x, g_ref, mod_refs):
    h = _rms_rows(x, g_ref[...])
    if mod_refs is not None:
        a_ref, s_ref = mod_refs
        h = h * (1.0 + a_ref[0]) + s_ref[0]
    return h


def _rownorm_kernel(x_ref, g_ref, a_ref, s_ref, h_ref):
    h_ref[...] = _norm_mod(x_ref[...], g_ref, (a_ref, s_ref)).astype(h_ref.dtype)


def _mod_spec(n_mod, rows, tm, d, n_prefetch=0):
    per = (rows // n_mod) // tm
    return pl.BlockSpec((1, 1, d), lambda i, *_: (i // per, 0, 0))


def _rownorm(x, g, scale, shift, *, tm=512):
    rows, d = x.shape
    row_spec = pl.BlockSpec((tm, d), lambda i: (i, 0))
    return pl.pallas_call(
        _rownorm_kernel, grid=(rows // tm,),
        in_specs=[row_spec, pl.BlockSpec((1, d), lambda i: (0, 0)), _mod_spec(scale.shape[0], rows, tm, d),
                  _mod_spec(shift.shape[0], rows, tm, d)],
        out_specs=row_spec, out_shape=jax.ShapeDtypeStruct((rows, d), BF16),
        compiler_params=_cparams("parallel"), name="rownorm",
    )(x, g.reshape(1, d), scale, shift)


def _mm_kernel(a_ref, w_ref, o_ref):
    o_ref[...] = _dot(a_ref[...].astype(BF16), w_ref[...]).astype(o_ref.dtype)


def _matmul(a, w, *, out_dtype=F32, tm=1024):
    m, k = a.shape
    n = w.shape[1]
    tn = next(t for t in (1024, 768, 512, LANES) if n % t == 0)
    return pl.pallas_call(
        _mm_kernel, grid=(m // tm, n // tn),
        in_specs=[pl.BlockSpec((tm, k), lambda i, j: (i, 0)), pl.BlockSpec((k, tn), lambda i, j: (0, j))],
        out_specs=pl.BlockSpec((tm, tn), lambda i, j: (i, j)),
        out_shape=jax.ShapeDtypeStruct((m, n), out_dtype),
        compiler_params=_cparams("parallel", "parallel"), name="matmul",
    )(a, w)


def _head_norm(x, g, n_heads, dv):
    outs = []
    for h in range(n_heads):
        xs = x[:, h * dv:(h + 1) * dv]
        ms = jnp.mean(xs * xs, axis=-1, keepdims=True)
        outs.append(xs * lax.rsqrt(ms + EPS) * g[:, h * dv:(h + 1) * dv])
    return jnp.concatenate(outs, axis=1)


def _proj_kernel(*refs, pre):
    refs = list(refs)
    if pre == "plain":
        a = refs.pop(0)[...]
    else:
        f_ref, b_ref, p_ref, hg_ref = refs.pop(0), refs.pop(0), refs.pop(0), refs.pop(0)
        hsum = f_ref[...] + b_ref[...]
        if pre == "mlstm":
            a = _sigmoid(p_ref[...]) * _head_norm(hsum, hg_ref[...], ML_HEADS, ML_DV)
        else:
            a = _head_norm(hsum, hg_ref[...], GLA_HEADS, GLA_DV) * _silu(p_ref[...])
        a = a.astype(BF16)
    w_ref, x_ref, gate_ref, g_ref, a_ref, s_ref, rwt_ref, rb_ref, count0_ref = refs[:9]
    xo_ref, h_ref, meta_ref, wcol_ref, count_ref, carry_ref = refs[9:]

    @pl.when(pl.program_id(0) == 0)
    def _():
        carry_ref[...] = count0_ref[...].astype(F32)

    x = x_ref[...] + gate_ref[0] * _dot(a, w_ref[...])
    xo_ref[...] = x
    h = _norm_mod(x, g_ref, (a_ref, s_ref))
    _rows_to_tiles(h_ref, h)
    meta, wcol, carry = _route(h, rwt_ref[...], rb_ref[...], carry_ref[:, 0:1])
    meta_ref[...] = meta
    wcol_ref[...] = wcol
    carry_ref[...] = jnp.broadcast_to(carry, carry_ref.shape)
    count_ref[...] = jnp.broadcast_to(carry, count_ref.shape).astype(jnp.int32)


def _proj(pre, pre_args, w_out, x, gate, g, scale, shift, rwt, rb, count0, *, tm=512):
    rows, d = x.shape
    k = w_out.shape[0]
    row_spec = pl.BlockSpec((tm, d), lambda i: (i, 0))
    if pre == "plain":
        args, specs = [pre_args[0]], [pl.BlockSpec((tm, k), lambda i: (i, 0))]
    else:
        hf, hb, p, col_block, hg = pre_args
        wide = pl.BlockSpec((tm, k), lambda i: (i, 0))
        args = [hf, hb, p, hg.reshape(1, k)]
        specs = [wide, wide, pl.BlockSpec((tm, k), lambda i: (i, col_block)), pl.BlockSpec((1, k), lambda i: (0, 0))]
    args += [w_out, x, gate, g.reshape(1, d), scale, shift, rwt, rb, count0]
    specs += [pl.BlockSpec((k, d), lambda i: (0, 0)), row_spec, _mod_spec(gate.shape[0], rows, tm, d),
              pl.BlockSpec((1, d), lambda i: (0, 0)), _mod_spec(scale.shape[0], rows, tm, d),
              _mod_spec(shift.shape[0], rows, tm, d), pl.BlockSpec(rwt.shape, lambda i: (0, 0)),
              pl.BlockSpec(rb.shape, lambda i: (0, 0)), pl.BlockSpec(count0.shape, lambda i: (0, 0))]
    return pl.pallas_call(
        functools.partial(_proj_kernel, pre=pre), grid=(rows // tm,), in_specs=specs,
        out_specs=[row_spec, pl.BlockSpec((tm * SUBLANES, LANES), lambda i: (i, 0)),
                   pl.BlockSpec((8, tm), lambda i: (0, i)),
                   pl.BlockSpec((tm, LANES), lambda i: (i, 0)), pl.BlockSpec((N_EXPERTS, LANES), lambda i: (0, 0))],
        out_shape=[jax.ShapeDtypeStruct((rows, d), F32), jax.ShapeDtypeStruct((rows * SUBLANES, LANES), F32),
                   jax.ShapeDtypeStruct((8, rows), jnp.int32), jax.ShapeDtypeStruct((rows, LANES), F32),
                   jax.ShapeDtypeStruct((N_EXPERTS, LANES), jnp.int32)],
        scratch_shapes=[pltpu.VMEM((N_EXPERTS, LANES), F32)],
        compiler_params=_cparams("arbitrary"), name="proj_" + pre,
    )(*args)


MOE_TILE = 512
MOE_TILE_SHIFT = 9
MOE_TOKENS = 256
ROW_UNROLL = 8


SUBLANES = 8


def _rows_to_tiles(ref, x, lead=()):
    rows = x.shape[0]
    for c in range(SUBLANES):
        ref[(*lead, pl.ds(c, rows, stride=SUBLANES), slice(None))] = x[:, c * LANES:(c + 1) * LANES]


def _tiles_to_rows(ref, rows, lead=()):
    return jnp.concatenate([ref[(*lead, pl.ds(c, rows, stride=SUBLANES), slice(None))] for c in range(SUBLANES)],
                           axis=1)


def _slot_tiles(rows):
    return (2 * rows) // MOE_TILE + N_EXPERTS


def _expert_offsets(cnt_ref, off_ref):
    def per_expert(e, k):
        off_ref[e] = k * MOE_TILE
        return k + ((cnt_ref[e] + MOE_TILE - 1) >> MOE_TILE_SHIFT)
    return lax.fori_loop(0, N_EXPERTS, per_expert, 0)


def _slots_kernel(cnt_ref, meta_ref, slot_ref, off_ref):
    @pl.when(pl.program_id(0) == 0)
    def _():
        _expert_offsets(cnt_ref, off_ref)

    e_a, e_b = meta_ref[0:1, :], meta_ref[1:2, :]
    off_a = jnp.zeros_like(e_a)
    off_b = jnp.zeros_like(e_b)
    for e in range(N_EXPERTS):
        off_a = jnp.where(e_a == e, off_ref[e], off_a)
        off_b = jnp.where(e_b == e, off_ref[e], off_b)
    slot_ref[...] = jnp.concatenate([off_a + meta_ref[2:3, :], off_b + meta_ref[3:4, :],
                                     jnp.zeros((6, e_a.shape[1]), jnp.int32)], axis=0)


def _slots(meta, counts, *, tm=1024):
    rows = meta.shape[1]
    grid_spec = pltpu.PrefetchScalarGridSpec(
        num_scalar_prefetch=1, grid=(rows // tm,),
        in_specs=[pl.BlockSpec((8, tm), lambda i, cnt: (0, i))],
        out_specs=pl.BlockSpec((8, tm), lambda i, cnt: (0, i)),
        scratch_shapes=[pltpu.SMEM((N_EXPERTS,), jnp.int32)])
    return pl.pallas_call(
        _slots_kernel, grid_spec=grid_spec, out_shape=jax.ShapeDtypeStruct((8, rows), jnp.int32),
        compiler_params=_cparams("arbitrary"), name="moe_slots",
    )(counts, meta)


def _dispatch_kernel(*refs, steps):
    sa_ref, sb_ref, cnt_ref = refs[:3]
    h_refs = refs[3:3 + len(steps)]
    xs_ref, info_ref, off_ref, zero_ref, sem = refs[3 + len(steps):]
    i = pl.program_id(0)
    tm = h_refs[0].shape[0] // SUBLANES
    n_tiles = info_ref.shape[0] - 1
    tile_rows = MOE_TILE * SUBLANES

    def tile_copy(tile):
        return pltpu.make_async_copy(zero_ref, xs_ref.at[pl.ds(tile * tile_rows, tile_rows), :], sem)

    @pl.when(i == 0)
    def _():
        zero_ref[...] = jnp.zeros_like(zero_ref)
        used = _expert_offsets(cnt_ref, off_ref)

        def per_expert(e, _):
            first = off_ref[e] >> MOE_TILE_SHIFT
            nt = (cnt_ref[e] + MOE_TILE - 1) >> MOE_TILE_SHIFT

            def fill(j, _):
                info_ref[first + j] = e
                return 0
            lax.fori_loop(0, nt, fill, 0)

            @pl.when(nt > 0)
            def _():
                tile_copy(first + nt - 1).start()
                tile_copy(first + nt - 1).wait()
            return 0
        lax.fori_loop(0, N_EXPERTS, per_expert, 0)
        info_ref[n_tiles] = used

        def tail(j, _):
            info_ref[j] = N_EXPERTS - 1
            tile_copy(j).start()
            tile_copy(j).wait()
            return 0
        lax.fori_loop(used, n_tiles, tail, 0)

    base = i * tm

    def copy_rows(h_ref):
        def row_copy(t, slot):
            dst = pl.multiple_of(slot * SUBLANES, SUBLANES)
            return pltpu.make_async_copy(h_ref.at[pl.ds(t * SUBLANES, SUBLANES), :],
                                         xs_ref.at[pl.ds(dst, SUBLANES), :], sem)

        for t in range(tm):
            row_copy(t, sa_ref[base + t]).start(priority=0)
            row_copy(t, sb_ref[base + t]).start(priority=1)
        for _ in range(2):
            pltpu.make_async_copy(h_ref, xs_ref.at[pl.ds(0, tm * SUBLANES), :], sem).wait()

    first = 0
    for h_ref, n in zip(h_refs, steps):
        pl.when(jnp.logical_and(i >= first, i < first + n))(functools.partial(copy_rows, h_ref))
        first += n


def _dispatch(hs, slots, counts):
    tm = MOE_TOKENS
    steps = tuple(h.shape[0] // (tm * SUBLANES) for h in hs)
    n_tiles = _slot_tiles(sum(steps) * tm)
    specs, first = [], 0
    for n in steps:
        specs.append(pl.BlockSpec((tm * SUBLANES, LANES),
                                  lambda i, *_, first=first, n=n: (jnp.clip(i - first, 0, n - 1), 0)))
        first += n
    grid_spec = pltpu.PrefetchScalarGridSpec(
        num_scalar_prefetch=3, grid=(sum(steps),), in_specs=specs,
        out_specs=[pl.BlockSpec(memory_space=pl.ANY), pl.BlockSpec(memory_space=pltpu.SMEM)],
        scratch_shapes=[pltpu.SMEM((N_EXPERTS,), jnp.int32), pltpu.VMEM((MOE_TILE * SUBLANES, LANES), F32),
                        pltpu.SemaphoreType.DMA(())])
    return pl.pallas_call(
        functools.partial(_dispatch_kernel, steps=steps), grid_spec=grid_spec,
        out_shape=[jax.ShapeDtypeStruct((n_tiles * MOE_TILE * SUBLANES, LANES), F32),
                   jax.ShapeDtypeStruct((n_tiles + 1,), jnp.int32)],
        compiler_params=_cparams("arbitrary"), name="moe_dispatch",
    )(slots[0], slots[1], counts, *hs)


def _ffn_kernel(info_ref, xs_ref, wg_ref, wu_ref, wd_ref, ys_ref, wg_s, wu_s, wd_s):
    i = pl.program_id(0)
    used = info_ref[info_ref.shape[0] - 1]
    fresh = jnp.logical_or(i == 0, info_ref[i] != info_ref[jnp.maximum(i - 1, 0)])

    @pl.when(jnp.logical_and(i < used, fresh))
    def _():
        wg_s[...] = wg_ref[0, 0].astype(BF16)
        wu_s[...] = wu_ref[0, 0].astype(BF16)
        wd_s[...] = wd_ref[0, 0].astype(BF16)

    @pl.when(i < used)
    def _():
        x = _tiles_to_rows(xs_ref, MOE_TILE).astype(BF16)
        hid = _silu(_dot(x, wg_s[...])) * _dot(x, wu_s[...])
        _rows_to_tiles(ys_ref, _dot(hid.astype(BF16), wd_s[...]))

    @pl.when(i >= used)
    def _():
        ys_ref[...] = jnp.zeros_like(ys_ref)


def _ffn(xs, info, wg, wu, wd, layer):
    tile_rows = MOE_TILE * SUBLANES
    n_tiles = xs.shape[0] // tile_rows
    d, f = wg.shape[2:]

    def w_map(i, info):
        return (layer, info[i], 0, 0)

    grid_spec = pltpu.PrefetchScalarGridSpec(
        num_scalar_prefetch=1, grid=(n_tiles,),
        in_specs=[pl.BlockSpec((tile_rows, LANES), lambda i, info: (jnp.minimum(i, info[n_tiles] - 1), 0)),
                  pl.BlockSpec((1, 1, d, f), w_map), pl.BlockSpec((1, 1, d, f), w_map),
                  pl.BlockSpec((1, 1, f, d), w_map)],
        out_specs=pl.BlockSpec((tile_rows, LANES), lambda i, info: (i, 0)),
        scratch_shapes=[pltpu.VMEM((d, f), BF16), pltpu.VMEM((d, f), BF16), pltpu.VMEM((f, d), BF16)])
    return pl.pallas_call(
        _ffn_kernel, grid_spec=grid_spec, out_shape=jax.ShapeDtypeStruct(xs.shape, F32),
        compiler_params=_cparams("arbitrary"), name="moe_ffn",
    )(info, xs, wg, wu, wd)


def _combine_kernel(*refs, has_mod, out_x):
    refs = list(refs)
    sa_ref, sb_ref, x_ref, ys_ref, wcol_ref, gate_ref, g_ref = refs[:7]
    refs = refs[7:]
    mod_refs = (refs.pop(0), refs.pop(0)) if has_mod else None
    xo_ref = refs.pop(0) if out_x else None
    h_ref, buf_a, buf_b, sems = refs
    i = pl.program_id(0)
    tm = x_ref.shape[0]

    def issue(tile, slot):
        base = tile * tm
        for t in range(tm):
            dst = pl.ds(t * SUBLANES, SUBLANES)
            src_a = pl.multiple_of(sa_ref[base + t] * SUBLANES, SUBLANES)
            src_b = pl.multiple_of(sb_ref[base + t] * SUBLANES, SUBLANES)
            pltpu.make_async_copy(ys_ref.at[pl.ds(src_a, SUBLANES), :], buf_a.at[slot, dst, :],
                                  sems.at[slot]).start(priority=0)
            pltpu.make_async_copy(ys_ref.at[pl.ds(src_b, SUBLANES), :], buf_b.at[slot, dst, :],
                                  sems.at[slot]).start(priority=1)

    @pl.when(i == 0)
    def _():
        issue(0, 0)

    @pl.when(i + 1 < pl.num_programs(0))
    def _():
        issue(i + 1, (i + 1) % 2)

    slot = i % 2
    for buf in (buf_a, buf_b):
        pltpu.make_async_copy(ys_ref.at[pl.ds(0, tm * SUBLANES), :], buf.at[slot], sems.at[slot]).wait()
    y = (wcol_ref[:, 0:1] * _tiles_to_rows(buf_a, tm, lead=(slot,))
         + wcol_ref[:, 1:2] * _tiles_to_rows(buf_b, tm, lead=(slot,)))
    x = x_ref[...] + gate_ref[0] * y
    if out_x:
        xo_ref[...] = x
    h_ref[...] = _norm_mod(x, g_ref, mod_refs).astype(h_ref.dtype)


def _combine(x, ys, slots, wcol, gate, g, *, mod=None, out_x=False, h_dtype=BF16):
    rows, d = x.shape
    tm = MOE_TOKENS
    row_spec = pl.BlockSpec((tm, d), lambda i, *_: (i, 0))
    args = [x, ys, wcol, gate, g.reshape(1, d)]
    specs = [row_spec, pl.BlockSpec(memory_space=pl.ANY), pl.BlockSpec((tm, LANES), lambda i, *_: (i, 0)),
             _mod_spec(gate.shape[0], rows, tm, d), pl.BlockSpec((1, d), lambda i, *_: (0, 0))]
    if mod is not None:
        for m in mod:
            args.append(m)
            specs.append(_mod_spec(m.shape[0], rows, tm, d))
    out_shape, out_specs = [], []
    if out_x:
        out_shape.append(jax.ShapeDtypeStruct((rows, d), F32))
        out_specs.append(row_spec)
    out_shape.append(jax.ShapeDtypeStruct((rows, d), h_dtype))
    out_specs.append(row_spec)
    grid_spec = pltpu.PrefetchScalarGridSpec(
        num_scalar_prefetch=2, grid=(rows // tm,), in_specs=specs, out_specs=out_specs,
        scratch_shapes=[pltpu.VMEM((2, tm * SUBLANES, LANES), F32), pltpu.VMEM((2, tm * SUBLANES, LANES), F32),
                        pltpu.SemaphoreType.DMA((2,))])
    outs = pl.pallas_call(
        functools.partial(_combine_kernel, has_mod=mod is not None, out_x=out_x), grid_spec=grid_spec,
        out_shape=out_shape, compiler_params=_cparams("arbitrary"), name="moe_combine",
    )(slots[0], slots[1], *args)
    return outs if out_x else outs[0]


def _softmax_av(scores, values, sink_col):
    m = sink_col
    for s in scores:
        m = jnp.maximum(m, jnp.max(s, axis=-1, keepdims=True))
    den = jnp.exp(sink_col - m)
    acc = None
    for s, v in zip(scores, values):
        p = jnp.exp(s - m)
        den = den + jnp.sum(p, axis=-1, keepdims=True)
        pv = _dot(p.astype(BF16), v)
        acc = pv if acc is None else acc + pv
    return acc / den


def _sink_column(sink_ref, kv, rows):
    return jnp.concatenate([jnp.full((rows, 1), sink_ref[kv * ATT_GROUP + g], F32) for g in range(ATT_GROUP)], axis=0)


def _attn_ctx_kernel(sink_ref, qkv_ref, o_ref):
    t = qkv_ref.shape[0]
    qw = ATT_HEADS * HEAD_DIM
    kw = ATT_KV * HEAD_DIM
    heads_out = []
    for kv in range(ATT_KV):
        q = jnp.concatenate(
            [qkv_ref[:, (kv * ATT_GROUP + g) * HEAD_DIM:(kv * ATT_GROUP + g + 1) * HEAD_DIM] for g in range(ATT_GROUP)],
            axis=0).astype(BF16)
        k = qkv_ref[:, qw + kv * HEAD_DIM:qw + (kv + 1) * HEAD_DIM].astype(BF16)
        v = qkv_ref[:, qw + kw + kv * HEAD_DIM:qw + kw + (kv + 1) * HEAD_DIM].astype(BF16)
        s = _dot_nt(q, k) * HEAD_DIM ** -0.5
        o = _softmax_av([s], [v], _sink_column(sink_ref, kv, t))
        heads_out += [o[g * t:(g + 1) * t] for g in range(ATT_GROUP)]
    o_ref[...] = jnp.concatenate(heads_out, axis=1).astype(o_ref.dtype)


def _attn_ctx(qkv, sink, n_seq, seq_len):
    rows, cols = qkv.shape
    return pl.pallas_call(
        _attn_ctx_kernel, grid=(n_seq,),
        in_specs=[pl.BlockSpec(memory_space=pltpu.SMEM), pl.BlockSpec((seq_len, cols), lambda b: (b, 0))],
        out_specs=pl.BlockSpec((seq_len, ATT_HEADS * HEAD_DIM), lambda b: (b, 0)),
        out_shape=jax.ShapeDtypeStruct((rows, ATT_HEADS * HEAD_DIM), BF16),
        compiler_params=_cparams("parallel"), name="attn_context",
    )(sink, qkv)


def _rope_block(x, cos, sin_signed):
    lane = lax.broadcasted_iota(jnp.int32, x.shape, 1)
    nf = HEAD_DIM // 4
    partner = jnp.where((lane % (2 * nf)) < nf, pltpu.roll(x, LANES - nf, axis=1), pltpu.roll(x, nf, axis=1))
    return x * cos + partner * sin_signed


def _attn_lat_kernel(sink_ref, qkv_ref, ck_ref, cv_ref, cos_ref, sin_ref, o_ref, k_scr):
    i = pl.program_id(1)
    t = qkv_ref.shape[0]
    qw = ATT_HEADS * HEAD_DIM
    kw = ATT_KV * HEAD_DIM
    span = Q_BLOCK + 2 * WINDOW

    @pl.when(i == 0)
    def _():
        for c in range(kw // LANES):
            blk = qkv_ref[:, qw + c * LANES:qw + (c + 1) * LANES]
            k_scr[:, c * LANES:(c + 1) * LANES] = _rope_block(blk, cos_ref[...], sin_ref[...]).astype(BF16)

    r0 = pl.multiple_of(i * Q_BLOCK, Q_BLOCK)
    ws = pl.multiple_of(jnp.clip(r0 - WINDOW, 0, t - span), Q_BLOCK)
    cos_q = cos_ref[pl.ds(r0, Q_BLOCK), :]
    sin_q = sin_ref[pl.ds(r0, Q_BLOCK), :]
    qpos = r0 + lax.broadcasted_iota(jnp.int32, (Q_BLOCK, span), 0)
    kpos = ws + lax.broadcasted_iota(jnp.int32, (Q_BLOCK, span), 1)
    band = jnp.abs(qpos - kpos) <= WINDOW
    band = jnp.concatenate([band] * ATT_GROUP, axis=0)
    heads_out = []
    for kv in range(ATT_KV):
        heads = []
        for g in range(ATT_GROUP):
            h = kv * ATT_GROUP + g
            c, half = divmod(h * HEAD_DIM, LANES)
            blk = _rope_block(qkv_ref[pl.ds(r0, Q_BLOCK), c * LANES:(c + 1) * LANES], cos_q, sin_q)
            heads.append(blk[:, half:half + HEAD_DIM])
        q = jnp.concatenate(heads, axis=0).astype(BF16)
        ck = ck_ref[0, :, kv * HEAD_DIM:(kv + 1) * HEAD_DIM].astype(BF16)
        cv = cv_ref[0, :, kv * HEAD_DIM:(kv + 1) * HEAD_DIM].astype(BF16)
        kwin = k_scr[pl.ds(ws, span), kv * HEAD_DIM:(kv + 1) * HEAD_DIM]
        vwin = qkv_ref[pl.ds(ws, span), qw + kw + kv * HEAD_DIM:qw + kw + (kv + 1) * HEAD_DIM].astype(BF16)
        s_ctx = _dot_nt(q, ck) * HEAD_DIM ** -0.5
        s_win = jnp.where(band, _dot_nt(q, kwin) * HEAD_DIM ** -0.5, -jnp.inf)
        o = _softmax_av([s_ctx, s_win], [cv, vwin], _sink_column(sink_ref, kv, Q_BLOCK))
        heads_out += [o[g * Q_BLOCK:(g + 1) * Q_BLOCK] for g in range(ATT_GROUP)]
    o_ref[...] = jnp.concatenate(heads_out, axis=1).astype(o_ref.dtype)


def _rope_tables(seq_len):
    pos = jnp.arange(seq_len, dtype=jnp.int32)
    row = (pos // GRID_W).astype(F32)
    col = (pos % GRID_W).astype(F32)
    nf = HEAD_DIM // 4
    inv = ROPE_BASE ** (-jnp.arange(nf, dtype=F32) / nf)
    ang_r = row[:, None] * inv[None, :]
    ang_c = col[:, None] * inv[None, :]
    cos_h = jnp.concatenate([jnp.cos(ang_r), jnp.cos(ang_r), jnp.cos(ang_c), jnp.cos(ang_c)], axis=1)
    sin_h = jnp.concatenate([-jnp.sin(ang_r), jnp.sin(ang_r), -jnp.sin(ang_c), jnp.sin(ang_c)], axis=1)
    reps = LANES // HEAD_DIM
    return jnp.tile(cos_h, (1, reps)), jnp.tile(sin_h, (1, reps))


def _attn_lat(qkv, cache_k, cache_v, sink, n_seq, seq_len):
    rows, cols = qkv.shape
    past = cache_k.shape[1]
    kw = ATT_KV * HEAD_DIM
    cos, sin = _rope_tables(seq_len)
    return pl.pallas_call(
        _attn_lat_kernel, grid=(n_seq, seq_len // Q_BLOCK),
        in_specs=[pl.BlockSpec(memory_space=pltpu.SMEM),
                  pl.BlockSpec((seq_len, cols), lambda b, i: (b, 0)),
                  pl.BlockSpec((1, past, kw), lambda b, i: (b, 0, 0)),
                  pl.BlockSpec((1, past, kw), lambda b, i: (b, 0, 0)),
                  pl.BlockSpec((seq_len, LANES), lambda b, i: (0, 0)),
                  pl.BlockSpec((seq_len, LANES), lambda b, i: (0, 0))],
        out_specs=pl.BlockSpec((Q_BLOCK, ATT_HEADS * HEAD_DIM), lambda b, i: (b * (seq_len // Q_BLOCK) + i, 0)),
        out_shape=jax.ShapeDtypeStruct((rows, ATT_HEADS * HEAD_DIM), BF16),
        scratch_shapes=[pltpu.VMEM((seq_len, kw), BF16)],
        compiler_params=_cparams("parallel", "arbitrary"), name="attn_latent",
    )(sink, qkv, cache_k.reshape(n_seq, past, kw), cache_v.reshape(n_seq, past, kw), cos, sin)


def _split_bf16(w):
    hi = w.astype(BF16)
    return hi, (w - hi.astype(F32)).astype(BF16)


def _select_lanes(x, sel):
    x1 = x.astype(BF16)
    rest = x - x1.astype(F32)
    x2 = rest.astype(BF16)
    x3 = (rest - x2.astype(F32)).astype(BF16)
    return _dot(x1, sel) + _dot(x2, sel) + _dot(x3, sel)


def _ml_gates_kernel(h_ref, w_ref, wt_ref, b_ref, bt_ref, g_ref, gt_ref):
    h = h_ref[...]
    h_hi = h.astype(BF16)
    h_lo = (h - h_hi.astype(F32)).astype(BF16)
    ng = gt_ref.shape[0]
    by_hi = _dot(h_hi, w_ref[...])
    g_ref[...] = by_hi[:, :LANES] + by_hi[:, LANES:] + _dot(h_lo, w_ref[:, :LANES]) + b_ref[...]
    by_hi_t = _dot_nt(wt_ref[...], h_hi)
    gt_ref[...] = by_hi_t[:ng] + by_hi_t[ng:] + _dot_nt(wt_ref[:ng, :], h_lo) + bt_ref[...]


def _ml_gates(h, w_gates, b_gates, *, tm=512):
    rows, d = h.shape
    ng = w_gates.shape[1]
    w_hi, w_lo = _split_bf16(jnp.pad(w_gates, ((0, 0), (0, LANES - ng))))
    wt_hi, wt_lo = _split_bf16(w_gates.T)
    b_pad = jnp.pad(b_gates, (0, LANES - ng)).reshape(1, LANES)
    return pl.pallas_call(
        _ml_gates_kernel, grid=(rows // tm,),
        in_specs=[pl.BlockSpec((tm, d), lambda i: (i, 0)), pl.BlockSpec((d, 2 * LANES), lambda i: (0, 0)),
                  pl.BlockSpec((2 * ng, d), lambda i: (0, 0)), pl.BlockSpec((1, LANES), lambda i: (0, 0)),
                  pl.BlockSpec((ng, 1), lambda i: (0, 0))],
        out_specs=[pl.BlockSpec((tm, LANES), lambda i: (i, 0)), pl.BlockSpec((ng, tm), lambda i: (0, i))],
        out_shape=[jax.ShapeDtypeStruct((rows, LANES), F32), jax.ShapeDtypeStruct((ng, rows), F32)],
        compiler_params=_cparams("parallel"), name="mlstm_gates",
    )(h, jnp.concatenate([w_hi, w_lo], axis=1), jnp.concatenate([wt_hi, wt_lo], axis=0), b_pad,
      b_gates.reshape(ng, 1))


def _ml_qk_kernel(h_ref, w_ref, cw_ref, o_ref, *, seq_len, k_scale):
    j = pl.program_id(1)
    x = _dot(h_ref[...].astype(BF16), w_ref[...])
    t = x.shape[0]
    pos = lax.broadcasted_iota(jnp.int32, x.shape, 0) % seq_len
    prev = jnp.where(pos == 0, 0.0, pltpu.roll(x, 1, axis=0))
    nxt = jnp.where(pos == seq_len - 1, 0.0, pltpu.roll(x, t - 1, axis=0))
    y = prev * cw_ref[0:1, :] + x * cw_ref[1:2, :] + nxt * cw_ref[2:3, :]
    scale = jnp.where(j >= pl.num_programs(1) // 2, k_scale, 1.0).astype(F32)
    o_ref[...] = (_silu(y) * scale).astype(o_ref.dtype)


def _ml_qk(h, w_qk, conv_w, seq_len, *, tm=1024, tn=1024):
    rows, d = h.shape
    width = w_qk.shape[1]
    return pl.pallas_call(
        functools.partial(_ml_qk_kernel, seq_len=seq_len, k_scale=ML_DK ** -0.5), grid=(rows // tm, width // tn),
        in_specs=[pl.BlockSpec((tm, d), lambda i, j: (i, 0)), pl.BlockSpec((d, tn), lambda i, j: (0, j)),
                  pl.BlockSpec((3, tn), lambda i, j: (0, j))],
        out_specs=pl.BlockSpec((tm, tn), lambda i, j: (i, j)),
        out_shape=jax.ShapeDtypeStruct((rows, width), BF16),
        compiler_params=_cparams("parallel", "parallel"), name="mlstm_qk",
    )(h, w_qk, conv_w)


def _ml_scan_kernel(*refs, zero_init):
    refs = list(refs)
    dirs = [tuple(refs[0:5]), tuple(refs[5:10])]
    sel_ref = refs[10]
    refs = refs[11:]
    if not zero_init:
        c0_ref, n0_ref, m0_ref = refs[:3]
        refs = refs[3:]
    hf_ref, hb_ref, c_ref, n_ref, m_ref = refs
    h_out = (hf_ref, hb_ref)
    c = pl.program_id(1)
    last = pl.num_programs(1) - 1

    @pl.when(c == 0)
    def _():
        if zero_init:
            c_ref[...] = jnp.zeros_like(c_ref)
            n_ref[...] = jnp.zeros_like(n_ref)
            m_ref[...] = jnp.zeros_like(m_ref)
        else:
            c_ref[...] = c0_ref[...]
            n_ref[...] = n0_ref[...]
            m_ref[...] = m0_ref[...]

    length = hf_ref.shape[0]
    ti = lax.broadcasted_iota(jnp.int32, (length, length), 0)
    si = lax.broadcasted_iota(jnp.int32, (length, length), 1)
    for d in range(2):
        q_ref, k_ref, v_ref, g_ref, gt_ref = dirs[d]
        causal = (ti >= si) if d == 0 else (ti <= si)
        tri = jnp.where(causal, 1.0, 0.0).astype(F32)
        g_col = g_ref[...]
        f_row = _log_sigmoid(gt_ref[...])
        b_col = jnp.dot(tri, _log_sigmoid(g_col), precision=HI, preferred_element_type=F32)
        b_row = _dot_nt(f_row, tri, precision=HI)
        i_rep = _select_lanes(g_col, sel_ref[d, 0])
        b_rep = _select_lanes(b_col, sel_ref[d, 1])
        edge = length - 1 if d == 0 else 0
        for h in range(ML_HEADS):
            ji = d * 2 * ML_HEADS + h
            jf = ji + ML_HEADS
            bc = b_rep[:, h * LANES:(h + 1) * LANES]
            i_col = i_rep[:, h * LANES:(h + 1) * LANES]
            br = b_row[jf:jf + 1, :]
            i_row = gt_ref[ji:ji + 1, :]
            m_rep = m_ref[0, d, h]
            m_prev = m_rep[:, 0:1]
            q = q_ref[:, h * ML_DK:(h + 1) * ML_DK]
            k = k_ref[:, h * ML_DK:(h + 1) * ML_DK]
            v = v_ref[:, h * ML_DV:(h + 1) * ML_DV].astype(BF16)
            cst = c_ref[0, d, h]
            nst = n_ref[0, d, h]
            a_row = i_row - br
            amat = jnp.where(causal, a_row, -jnp.inf)
            u = jnp.maximum(m_prev, jnp.max(amat, axis=1, keepdims=True))
            qk = (_dot_nt(q, k) * jnp.exp(amat - u)).astype(BF16)
            sc = jnp.exp(m_prev - u)
            state_ext = jnp.concatenate([cst, jnp.broadcast_to(nst, (LANES, ML_DK))], axis=0).astype(BF16)
            v_ext = jnp.concatenate([v, jnp.ones((length, LANES), BF16)], axis=1)
            tot = sc * _dot_nt(q, state_ext) + _dot(qk, v_ext)
            inv = 1.0 / jnp.maximum(jnp.abs(tot[:, ML_DV:]), jnp.exp(-(bc + u)))
            h_out[d][:, h * ML_DV:(h + 1) * ML_DV] = tot[:, :ML_DV] * jnp.concatenate([inv] * (ML_DV // LANES), axis=1)
            b_last = bc[edge:edge + 1, :]
            wlog_row = b_last + a_row
            m_new = jnp.maximum(b_last + m_rep, jnp.max(wlog_row, axis=1, keepdims=True))
            decay = jnp.exp(b_last + m_rep - m_new)
            ws_row = jnp.exp(wlog_row - m_new)
            ws_col = jnp.exp(b_last - bc + i_col - m_new)
            kw = (ws_col * k.astype(F32)).astype(BF16)
            c_ref[0, d, h] = decay * cst + _dot_tn(v, kw)
            n_ref[0, d, h] = decay * nst + _dot(jnp.broadcast_to(ws_row, (8, length)).astype(BF16), k)[0:1]
            m_ref[0, d, h] = m_new


def _ml_scan(qk, p, g, gt, state, n_seq, seq_len):
    rows = qk.shape[0]
    length = min(ML_CHUNK, seq_len)
    nc = seq_len // length
    qw = ML_HEADS * ML_DK
    vw = ML_HEADS * ML_DV
    ng = gt.shape[0]

    def fwd(b, c):
        return b * nc + c

    def bwd(b, c):
        return b * nc + nc - 1 - c

    args, specs = [], []
    for pos in (fwd, bwd):
        args += [qk, qk, p, g, gt]
        specs += [pl.BlockSpec((length, qw), lambda b, c, pos=pos: (pos(b, c), 0)),
                  pl.BlockSpec((length, qw), lambda b, c, pos=pos: (pos(b, c), 1)),
                  pl.BlockSpec((length, vw), lambda b, c, pos=pos: (pos(b, c), 0)),
                  pl.BlockSpec((length, LANES), lambda b, c, pos=pos: (pos(b, c), 0)),
                  pl.BlockSpec((ng, length), lambda b, c, pos=pos: (0, pos(b, c)))]
    assert length == LANES and ML_DK == LANES
    gate_lane = jnp.arange(LANES)[:, None]
    head = (jnp.arange(ML_HEADS * LANES) // LANES)[None, :]
    sel = jnp.stack([jnp.stack([gate_lane == (2 * d + kind) * ML_HEADS + head for kind in range(2)])
                     for d in range(2)]).astype(BF16)
    args.append(sel)
    specs.append(pl.BlockSpec(sel.shape, lambda b, c: (0, 0, 0, 0)))
    c_spec = pl.BlockSpec((1, 2, ML_HEADS, ML_DV, ML_DK), lambda b, c: (b, 0, 0, 0, 0))
    n_spec = pl.BlockSpec((1, 2, ML_HEADS, 1, ML_DK), lambda b, c: (b, 0, 0, 0, 0))
    zero_init = state is None
    if not zero_init:
        c0, n0, m0 = state
        args += [c0, n0.reshape(n_seq, 2, ML_HEADS, 1, ML_DK),
                 jnp.broadcast_to(m0[..., None, None], (n_seq, 2, ML_HEADS, 1, ML_DK))]
        specs += [c_spec, n_spec, n_spec]
    hf, hb, c_fin, n_fin, m_fin = pl.pallas_call(
        functools.partial(_ml_scan_kernel, zero_init=zero_init), grid=(n_seq, nc), in_specs=specs,
        out_specs=[pl.BlockSpec((length, vw), lambda b, c: (fwd(b, c), 0)),
                   pl.BlockSpec((length, vw), lambda b, c: (bwd(b, c), 0)), c_spec, n_spec, n_spec],
        out_shape=[jax.ShapeDtypeStruct((rows, vw), F32), jax.ShapeDtypeStruct((rows, vw), F32),
                   jax.ShapeDtypeStruct((n_seq, 2, ML_HEADS, ML_DV, ML_DK), F32),
                   jax.ShapeDtypeStruct((n_seq, 2, ML_HEADS, 1, ML_DK), F32),
                   jax.ShapeDtypeStruct((n_seq, 2, ML_HEADS, 1, ML_DK), F32)],
        compiler_params=_cparams("parallel", "arbitrary"), name="mlstm_scan",
    )(*args)
    return hf, hb, (c_fin, n_fin[:, :, :, 0, :], m_fin[:, :, :, 0, 0])


def _gla_scan_kernel(*refs, zero_init):
    refs = list(refs)
    dirs = [tuple(refs[0:4]), tuple(refs[4:8])]
    w2_ref, ba_ref = refs[8:10]
    refs = refs[10:]
    if not zero_init:
        s0_ref = refs.pop(0)
    of_ref, ob_ref, s_ref, st_scr, la_scr = refs
    o_out = (of_ref, ob_ref)
    c = pl.program_id(1)
    last = pl.num_programs(1) - 1
    kw = GLA_HEADS * GLA_DK
    n_sub = of_ref.shape[0] // GLA_SUB

    @pl.when(c == 0)
    def _():
        for d in range(2):
            for h in range(GLA_HEADS):
                st_scr[d, h] = jnp.zeros((GLA_DV, GLA_DK), F32) if zero_init else s0_ref[0, d, h].T

    for d in range(2):
        u = dirs[d][3][...].astype(BF16)
        z = _dot(u, w2_ref[:, d * kw:(d + 1) * kw]) + ba_ref[:, d * kw:(d + 1) * kw]
        la_scr[d] = _log_sigmoid(z) / GLA_TAU

    ti = lax.broadcasted_iota(jnp.int32, (GLA_SUB, GLA_SUB), 0)
    si = lax.broadcasted_iota(jnp.int32, (GLA_SUB, GLA_SUB), 1)
    s_lane = lax.broadcasted_iota(jnp.int32, (GLA_SUB, GLA_SUB), 1)

    def sub_chunk(j, carry):
        for d in range(2):
            q_ref, k_ref, v_ref, _ = dirs[d]
            r0 = pl.multiple_of((j if d == 0 else n_sub - 1 - j) * GLA_SUB, GLA_SUB)
            causal = (ti >= si) if d == 0 else (ti <= si)
            tri = jnp.where(causal, 1.0, 0.0).astype(F32)
            bc_all = jnp.dot(tri, la_scr[d, pl.ds(r0, GLA_SUB), :], precision=HI, preferred_element_type=F32)
            edge = GLA_SUB - 1 if d == 0 else 0
            for h in range(GLA_HEADS):
                bc = bc_all[:, h * GLA_DK:(h + 1) * GLA_DK]
                q = q_ref[pl.ds(r0, GLA_SUB), h * GLA_DK:(h + 1) * GLA_DK] * GLA_DK ** -0.5
                k = k_ref[pl.ds(r0, GLA_SUB), h * GLA_DK:(h + 1) * GLA_DK]
                v = v_ref[pl.ds(r0, GLA_SUB), h * GLA_DV:(h + 1) * GLA_DV].astype(BF16)
                bc2 = bc * LOG2E
                a = jnp.zeros((GLA_SUB, GLA_SUB), F32)
                for s in range(GLA_SUB):
                    decay = jnp.exp2(bc2 - bc2[s:s + 1, :])
                    col = jnp.sum(q * (k[s:s + 1, :] * decay), axis=1, keepdims=True)
                    a = jnp.where(s_lane == s, col, a)
                a = jnp.where(causal, a, 0.0)
                st = st_scr[d, h]
                o = _dot(a.astype(BF16), v) + _dot_nt((q * jnp.exp2(bc2)).astype(BF16), st.astype(BF16))
                o_out[d][pl.ds(r0, GLA_SUB), h * GLA_DV:(h + 1) * GLA_DV] = o
                b_last = bc2[edge:edge + 1, :]
                k_dec = (k * jnp.exp2(b_last - bc2)).astype(BF16)
                st_scr[d, h] = jnp.exp2(b_last) * st + _dot_tn(v, k_dec)
        return carry

    lax.fori_loop(0, n_sub, sub_chunk, 0)

    @pl.when(c == last)
    def _():
        for d in range(2):
            for h in range(GLA_HEADS):
                s_ref[0, d, h] = st_scr[d, h].T


def _gla_scan(p, u, w2, b_a, state, n_seq, seq_len):
    rows = p.shape[0]
    length = min(GLA_BLOCK, seq_len)
    nc = seq_len // length
    kw = GLA_HEADS * GLA_DK
    vw = GLA_HEADS * GLA_DV

    def fwd(b, c):
        return b * nc + c

    def bwd(b, c):
        return b * nc + nc - 1 - c

    args, specs = [], []
    for pos in (fwd, bwd):
        args += [p, p, p, u]
        specs += [pl.BlockSpec((length, kw), lambda b, c, pos=pos: (pos(b, c), 0)),
                  pl.BlockSpec((length, kw), lambda b, c, pos=pos: (pos(b, c), 1)),
                  pl.BlockSpec((length, vw), lambda b, c, pos=pos: (pos(b, c), 2 * kw // vw)),
                  pl.BlockSpec((length, LANES), lambda b, c, pos=pos: (pos(b, c), 0))]
    args += [w2, b_a]
    specs += [pl.BlockSpec(w2.shape, lambda b, c: (0, 0)), pl.BlockSpec(b_a.shape, lambda b, c: (0, 0))]
    s_spec = pl.BlockSpec((1, 2, GLA_HEADS, GLA_DK, GLA_DV), lambda b, c: (b, 0, 0, 0, 0))
    zero_init = state is None
    if not zero_init:
        args.append(state)
        specs.append(s_spec)
    return pl.pallas_call(
        functools.partial(_gla_scan_kernel, zero_init=zero_init), grid=(n_seq, nc), in_specs=specs,
        out_specs=[pl.BlockSpec((length, vw), lambda b, c: (fwd(b, c), 0)),
                   pl.BlockSpec((length, vw), lambda b, c: (bwd(b, c), 0)), s_spec],
        out_shape=[jax.ShapeDtypeStruct((rows, vw), F32), jax.ShapeDtypeStruct((rows, vw), F32),
                   jax.ShapeDtypeStruct((n_seq, 2, GLA_HEADS, GLA_DK, GLA_DV), F32)],
        scratch_shapes=[pltpu.VMEM((2, GLA_HEADS, GLA_DV, GLA_DK), F32), pltpu.VMEM((2, length, kw), F32)],
        compiler_params=_cparams("parallel", "arbitrary"), name="gla_scan",
    )(*args)


def kernel(x_prompt, x_sample, cache_k_0, cache_v_0, state_mlstm_C_1, state_mlstm_n_1, state_mlstm_m_1, state_gla_S_2, cache_k_3, cache_v_3, c, c_ctx, w_mod, b_mod, norm1_g, norm2_g, final_g, router_w, router_b, moe_wg, moe_wu, moe_wd, attn0_w_qkv, attn0_sink, attn0_w_o, mlstm1_w_in, mlstm1_b_gates, mlstm1_conv, mlstm1_norm_g, mlstm1_w_out, gla2_w_in, gla2_w_a1, gla2_w_a2, gla2_b_a, gla2_norm_g, gla2_w_out, attn3_w_qkv, attn3_sink, attn3_w_o):
    n_ctx, ctx_len, d = x_prompt.shape
    n_lat, lat_len, _ = x_sample.shape
    depth = w_mod.shape[0]

    cvec = jnp.concatenate([c_ctx[None, :], c, jnp.zeros((8 - 1 - n_lat, d), F32)], axis=0)
    mod = _modulation(cvec, w_mod, b_mod).reshape(depth, 8, 6, 1, d)

    def mods(layer, kind, latent):
        return mod[layer, 1:1 + n_lat, kind] if latent else mod[layer, 0:1, kind]

    rw_hi = router_w.T.astype(BF16)
    rw_lo = (router_w.T - rw_hi.astype(F32)).astype(BF16)
    rwt = jnp.concatenate([rw_hi, rw_lo], axis=0)
    rb = router_b.reshape(-1, 1)
    attn_w = {0: (attn0_w_qkv.astype(BF16), attn0_sink, attn0_w_o.astype(BF16), cache_k_0, cache_v_0),
              3: (attn3_w_qkv.astype(BF16), attn3_sink, attn3_w_o.astype(BF16), cache_k_3, cache_v_3)}
    ml_qw = ML_HEADS * ML_DK
    ml_vw = ML_HEADS * ML_DV
    ml_main = 2 * ml_qw + 2 * ml_vw
    ml_w_qk = mlstm1_w_in[:, :2 * ml_qw].astype(BF16)
    ml_w_vo = mlstm1_w_in[:, 2 * ml_qw:ml_main].astype(BF16)
    ml_w_gates = mlstm1_w_in[:, ml_main:]
    ml_w_out = mlstm1_w_out.astype(BF16)
    gla_kw = GLA_HEADS * GLA_DK
    gla_w_in = gla2_w_in.astype(BF16)
    gla_w_a1 = jnp.pad(jnp.concatenate([gla2_w_a1[0], gla2_w_a1[1]], axis=1),
                       ((0, 0), (0, LANES - 2 * GLA_RANK))).astype(BF16)
    gla_w2 = jnp.zeros((LANES, 2 * gla_kw), F32)
    gla_w2 = gla_w2.at[:GLA_RANK, :gla_kw].set(gla2_w_a2[0]).at[GLA_RANK:2 * GLA_RANK, gla_kw:].set(gla2_w_a2[1])
    gla_w2 = gla_w2.astype(BF16)
    gla_ba = gla2_b_a.reshape(1, 2 * gla_kw)
    gla_w_out = gla2_w_out.astype(BF16)

    new_state = []

    def mixer(layer, s, count0):
        latent, n_seq, seq_len, x, h = s["latent"], s["n_seq"], s["seq_len"], s["x"], s["h"]
        tail = (mods(layer, 2, latent), norm2_g[layer], mods(layer, 4, latent), mods(layer, 3, latent), rwt, rb,
                count0)
        kind = layer % 3
        if kind == 0:
            w_qkv, sink, w_o, ck, cv = attn_w[layer]
            qkv = _matmul(h, w_qkv)
            if latent:
                att = _attn_lat(qkv, ck, cv, sink, n_seq, seq_len)
            else:
                att = _attn_ctx(qkv, sink, n_seq, seq_len)
                qw = ATT_HEADS * HEAD_DIM
                kw = ATT_KV * HEAD_DIM
                new_state.append(qkv[:, qw:qw + kw].reshape(n_seq, seq_len, ATT_KV, HEAD_DIM))
                new_state.append(qkv[:, qw + kw:].reshape(n_seq, seq_len, ATT_KV, HEAD_DIM))
            return _proj("plain", (att,), w_o, x, *tail)
        if kind == 1:
            p = _matmul(h, ml_w_vo)
            g, gt = _ml_gates(h, ml_w_gates, mlstm1_b_gates)
            qk = _ml_qk(h, ml_w_qk, mlstm1_conv, seq_len)
            st = (state_mlstm_C_1, state_mlstm_n_1, state_mlstm_m_1) if latent else None
            hf, hb, fin = _ml_scan(qk, p, g, gt, st, n_seq, seq_len)
            if not latent:
                new_state.extend(fin)
            return _proj("mlstm", (hf, hb, p, 1, mlstm1_norm_g), ml_w_out, x, *tail)
        p = _matmul(h, gla_w_in)
        u = _matmul(h, gla_w_a1)
        of, ob, s_fin = _gla_scan(p, u, gla_w2, gla_ba, state_gla_S_2 if latent else None, n_seq, seq_len)
        if not latent:
            new_state.append(s_fin)
        gla_vw = GLA_HEADS * GLA_DV
        return _proj("gla", (of, ob, p, (2 * gla_kw + gla_vw) // gla_vw, gla2_norm_g), gla_w_out, x, *tail)

    streams = [dict(latent=False, n_seq=n_ctx, seq_len=ctx_len, x=x_prompt.reshape(n_ctx * ctx_len, d)),
               dict(latent=True, n_seq=n_lat, seq_len=lat_len, x=x_sample.reshape(n_lat * lat_len, d))]
    for s in streams:
        s["h"] = _rownorm(s["x"], norm1_g[0], mods(0, 1, s["latent"]), mods(0, 0, s["latent"]))
    for layer in range(depth):
        counts = jnp.zeros((N_EXPERTS, LANES), jnp.int32)
        for s in streams:
            s["x"], s["h2"], s["meta"], s["wcol"], counts = mixer(layer, s, counts)
        cnt = counts[:, 0]
        for s in streams:
            s["slots"] = _slots(s["meta"], cnt)
        xs, info = _dispatch([s["h2"] for s in streams], jnp.concatenate([s["slots"] for s in streams], axis=1), cnt)
        ys = _ffn(xs, info, moe_wg, moe_wu, moe_wd, layer)
        for s in streams:
            latent = s["latent"]
            gate2 = mods(layer, 5, latent)
            if layer + 1 < depth:
                s["x"], s["h"] = _combine(s["x"], ys, s["slots"], s["wcol"], gate2, norm1_g[layer + 1],
                                          mod=(mods(layer + 1, 1, latent), mods(layer + 1, 0, latent)), out_x=True,
                                          h_dtype=F32 if (layer + 1) % 3 == 1 else BF16)
            else:
                s["out"] = _combine(s["x"], ys, s["slots"], s["wcol"], gate2, final_g, h_dtype=F32)
    y_prompt = streams[0]["out"].reshape(n_ctx, ctx_len, d)
    y_sample = streams[1]["out"].reshape(n_lat, lat_len, d)
    return (y_prompt, y_sample, *new_state)
```

```python
import functools

import jax
import jax.numpy as jnp
from jax import lax
from jax.experimental import pallas as pl
from jax.experimental.pallas import tpu as pltpu

F32 = jnp.float32
BF16 = jnp.bfloat16
HI = lax.Precision.HIGHEST

EPS = 1e-6
LOG2E = 1.4426950408889634
GRID_W = 64
ATT_HEADS = 16
ATT_KV = 4
ATT_GROUP = ATT_HEADS // ATT_KV
HEAD_DIM = 64
WINDOW = 128
Q_BLOCK = 128
ROPE_BASE = 10000.0
ML_HEADS = 8
ML_DK = 128
ML_DV = 256
ML_CHUNK = 128
GLA_HEADS = 4
GLA_DK = 128
GLA_DV = 256
GLA_RANK = 16
GLA_TAU = 16.0
GLA_SUB = 16
GLA_BLOCK = 256
N_EXPERTS = 16
N_GROUPS = 4
GROUP_SIZE = N_EXPERTS // N_GROUPS
LANES = 128
VMEM_LIMIT = 56 * 1024 * 1024


def _cparams(*sem):
    return pltpu.CompilerParams(dimension_semantics=sem, vmem_limit_bytes=VMEM_LIMIT)


def _dot(a, b):
    return jnp.dot(a, b, preferred_element_type=F32)


def _dot_nt(a, b, precision=None):
    return lax.dot_general(a, b, (((1,), (1,)), ((), ())), precision=precision, preferred_element_type=F32)


def _dot_tn(a, b):
    return lax.dot_general(a, b, (((0,), (0,)), ((), ())), preferred_element_type=F32)


def _sigmoid(x):
    return 1.0 / (1.0 + jnp.exp(-x))


def _silu(x):
    return x * _sigmoid(x)


def _log_sigmoid(x):
    return jnp.minimum(x, 0.0) - jnp.log(1.0 + jnp.exp(-jnp.abs(x)))


def _rms_rows(x, g):
    ms = jnp.mean(x * x, axis=-1, keepdims=True)
    return x * lax.rsqrt(ms + EPS) * g


def _mod_kernel(c_ref, w_ref, b_ref, o_ref):
    s = _silu(c_ref[...])
    o_ref[0] = _dot(s.astype(BF16), w_ref[0].astype(BF16)) + b_ref[0]


def _modulation(cvec, w_mod, b_mod):
    depth, d, n6 = w_mod.shape
    tn = 1536
    return pl.pallas_call(
        _mod_kernel,
        grid=(depth, n6 // tn),
        in_specs=[pl.BlockSpec((8, d), lambda l, j: (0, 0)),
                  pl.BlockSpec((1, d, tn), lambda l, j: (l, 0, j)),
                  pl.BlockSpec((1, 1, tn), lambda l, j: (l, 0, j))],
        out_specs=pl.BlockSpec((1, 8, tn), lambda l, j: (l, 0, j)),
        out_shape=jax.ShapeDtypeStruct((depth, 8, n6), F32),
        compiler_params=_cparams("parallel", "parallel"),
        name="adaln_modulation",
    )(cvec, w_mod, b_mod.reshape(depth, 1, n6))


def _route(h, rwt, rb, carry):
    tm = h.shape[0]
    h_hi = h.astype(BF16)
    h_lo = (h - h_hi.astype(F32)).astype(BF16)
    by_hi = _dot_nt(rwt, h_hi)
    logits = by_hi[:N_EXPERTS] + by_hi[N_EXPERTS:] + _dot_nt(rwt[:N_EXPERTS], h_lo)
    scores = _sigmoid(logits)
    sel = scores + rb
    expert = lax.broadcasted_iota(jnp.int32, sel.shape, 0)
    pos = expert % GROUP_SIZE
    grp = expert // GROUP_SIZE

    def mate(x, k):
        ahead = pltpu.roll(x, N_EXPERTS - k, axis=0)
        behind = pltpu.roll(x, GROUP_SIZE - k, axis=0)
        return jnp.where(pos + k < GROUP_SIZE, ahead, behind)

    beaten = jnp.zeros_like(sel)
    for k in range(1, GROUP_SIZE):
        other = mate(sel, k)
        other_first = (pos + k) % GROUP_SIZE < pos
        beaten = beaten + jnp.where(other_first, jnp.where(other >= sel, 1.0, 0.0), jnp.where(other > sel, 1.0, 0.0))
    top2 = jnp.where(beaten < 2.0, sel, 0.0)
    gscore = top2
    for k in range(1, GROUP_SIZE):
        gscore = gscore + mate(top2, k)
    lost = jnp.zeros_like(sel)
    for k in range(1, N_GROUPS):
        other = pltpu.roll(gscore, N_EXPERTS - GROUP_SIZE * k, axis=0)
        other_first = (grp + k) % N_GROUPS < grp
        lost = lost + jnp.where(other_first, jnp.where(other >= gscore, 1.0, 0.0),
                                jnp.where(other > gscore, 1.0, 0.0))
    picked = jnp.where(lost < 0.5, jnp.where(beaten < 2.0, 1.0, 0.0), 0.0)
    chosen = picked > 0.5
    weight = jnp.where(chosen, scores, 0.0)
    wsum = jnp.sum(weight, axis=0, keepdims=True)
    e_f = expert.astype(F32)
    e_a = jnp.min(jnp.where(chosen, e_f, float(N_EXPERTS)), axis=0, keepdims=True)
    e_b = jnp.max(jnp.where(chosen, e_f, -1.0), axis=0, keepdims=True)
    before = (lax.broadcasted_iota(jnp.int32, (tm, tm), 0) < lax.broadcasted_iota(jnp.int32, (tm, tm), 1))
    rank = _dot(picked.astype(BF16), jnp.where(before, 1.0, 0.0).astype(BF16)) + carry
    is_a = e_f == e_a
    is_b = e_f == e_b
    r_a = jnp.sum(jnp.where(is_a, rank, 0.0), axis=0, keepdims=True)
    r_b = jnp.sum(jnp.where(is_b, rank, 0.0), axis=0, keepdims=True)
    w_a = jnp.sum(jnp.where(is_a, weight, 0.0), axis=0, keepdims=True)
    w_b = jnp.sum(jnp.where(is_b, weight, 0.0), axis=0, keepdims=True)
    meta = jnp.concatenate([e_a, e_b, r_a, r_b, jnp.zeros((4, tm), F32)], axis=0).astype(jnp.int32)
    wcol = jnp.concatenate([w_a / wsum, w_b / wsum, jnp.zeros((LANES - 2, tm), F32)], axis=0).T
    return meta, wcol, carry + jnp.sum(picked, axis=1, keepdims=True)


def _norm_mod(x, g_ref, mod_refs):
    h = _rms_rows(x, g_ref[...])
    if mod_refs is not None:
        a_ref, s_ref = mod_refs
        h = h * (1.0 + a_ref[0]) + s_ref[0]
    return h


def _rownorm_kernel(x_ref, g_ref, a_ref, s_ref, h_ref):
    h_ref[...] = _norm_mod(x_ref[...], g_ref, (a_ref, s_ref)).astype(h_ref.dtype)


def _mod_spec(n_mod, rows, tm, d, n_prefetch=0):
    per = (rows // n_mod) // tm
    return pl.BlockSpec((1, 1, d), lambda i, *_: (i // per, 0, 0))


def _rownorm(x, g, scale, shift, *, tm=512):
    rows, d = x.shape
    row_spec = pl.BlockSpec((tm, d), lambda i: (i, 0))
    return pl.pallas_call(
        _rownorm_kernel, grid=(rows // tm,),
        in_specs=[row_spec, pl.BlockSpec((1, d), lambda i: (0, 0)), _mod_spec(scale.shape[0], rows, tm, d),
                  _mod_spec(shift.shape[0], rows, tm, d)],
        out_specs=row_spec, out_shape=jax.ShapeDtypeStruct((rows, d), BF16),
        compiler_params=_cparams("parallel"), name="rownorm",
    )(x, g.reshape(1, d), scale, shift)


def _mm_kernel(a_ref, w_ref, o_ref):
    o_ref[...] = _dot(a_ref[...].astype(BF16), w_ref[...]).astype(o_ref.dtype)


def _matmul(a, w, *, out_dtype=F32, tm=1024):
    m, k = a.shape
    n = w.shape[1]
    tn = next(t for t in (1024, 768, 512, LANES) if n % t == 0)
    return pl.pallas_call(
        _mm_kernel, grid=(m // tm, n // tn),
        in_specs=[pl.BlockSpec((tm, k), lambda i, j: (i, 0)), pl.BlockSpec((k, tn), lambda i, j: (0, j))],
        out_specs=pl.BlockSpec((tm, tn), lambda i, j: (i, j)),
        out_shape=jax.ShapeDtypeStruct((m, n), out_dtype),
        compiler_params=_cparams("parallel", "parallel"), name="matmul",
    )(a, w)


def _head_norm(x, g, n_heads, dv):
    outs = []
    for h in range(n_heads):
        xs = x[:, h * dv:(h + 1) * dv]
        ms = jnp.mean(xs * xs, axis=-1, keepdims=True)
        outs.append(xs * lax.rsqrt(ms + EPS) * g[:, h * dv:(h + 1) * dv])
    return jnp.concatenate(outs, axis=1)


def _proj_kernel(*refs, pre):
    refs = list(refs)
    if pre == "plain":
        a = refs.pop(0)[...]
    else:
        f_ref, b_ref, p_ref, hg_ref = refs.pop(0), refs.pop(0), refs.pop(0), refs.pop(0)
        hsum = f_ref[...] + b_ref[...]
        if pre == "mlstm":
            a = _sigmoid(p_ref[...]) * _head_norm(hsum, hg_ref[...], ML_HEADS, ML_DV)
        else:
            a = _head_norm(hsum, hg_ref[...], GLA_HEADS, GLA_DV) * _silu(p_ref[...])
        a = a.astype(BF16)
    w_ref, x_ref, gate_ref, g_ref, a_ref, s_ref, rwt_ref, rb_ref, count0_ref = refs[:9]
    xo_ref, h_ref, meta_ref, wcol_ref, count_ref, carry_ref = refs[9:]

    @pl.when(pl.program_id(0) == 0)
    def _():
        carry_ref[...] = count0_ref[...].astype(F32)

    x = x_ref[...] + gate_ref[0] * _dot(a, w_ref[...])
    xo_ref[...] = x
    h = _norm_mod(x, g_ref, (a_ref, s_ref))
    _rows_to_tiles(h_ref, h)
    meta, wcol, carry = _route(h, rwt_ref[...], rb_ref[...], carry_ref[:, 0:1])
    meta_ref[...] = meta
    wcol_ref[...] = wcol
    carry_ref[...] = jnp.broadcast_to(carry, carry_ref.shape)
    count_ref[...] = jnp.broadcast_to(carry, count_ref.shape).astype(jnp.int32)


def _proj(pre, pre_args, w_out, x, gate, g, scale, shift, rwt, rb, count0, *, tm=512):
    rows, d = x.shape
    k = w_out.shape[0]
    row_spec = pl.BlockSpec((tm, d), lambda i: (i, 0))
    if pre == "plain":
        args, specs = [pre_args[0]], [pl.BlockSpec((tm, k), lambda i: (i, 0))]
    else:
        hf, hb, p, col_block, hg = pre_args
        wide = pl.BlockSpec((tm, k), lambda i: (i, 0))
        args = [hf, hb, p, hg.reshape(1, k)]
        specs = [wide, wide, pl.BlockSpec((tm, k), lambda i: (i, col_block)), pl.BlockSpec((1, k), lambda i: (0, 0))]
    args += [w_out, x, gate, g.reshape(1, d), scale, shift, rwt, rb, count0]
    specs += [pl.BlockSpec((k, d), lambda i: (0, 0)), row_spec, _mod_spec(gate.shape[0], rows, tm, d),
              pl.BlockSpec((1, d), lambda i: (0, 0)), _mod_spec(scale.shape[0], rows, tm, d),
              _mod_spec(shift.shape[0], rows, tm, d), pl.BlockSpec(rwt.shape, lambda i: (0, 0)),
              pl.BlockSpec(rb.shape, lambda i: (0, 0)), pl.BlockSpec(count0.shape, lambda i: (0, 0))]
    return pl.pallas_call(
        functools.partial(_proj_kernel, pre=pre), grid=(rows // tm,), in_specs=specs,
        out_specs=[row_spec, pl.BlockSpec((tm * SUBLANES, LANES), lambda i: (i, 0)),
                   pl.BlockSpec((8, tm), lambda i: (0, i)),
                   pl.BlockSpec((tm, LANES), lambda i: (i, 0)), pl.BlockSpec((N_EXPERTS, LANES), lambda i: (0, 0))],
        out_shape=[jax.ShapeDtypeStruct((rows, d), F32), jax.ShapeDtypeStruct((rows * SUBLANES, LANES), F32),
                   jax.ShapeDtypeStruct((8, rows), jnp.int32), jax.ShapeDtypeStruct((rows, LANES), F32),
                   jax.ShapeDtypeStruct((N_EXPERTS, LANES), jnp.int32)],
        scratch_shapes=[pltpu.VMEM((N_EXPERTS, LANES), F32)],
        compiler_params=_cparams("arbitrary"), name="proj_" + pre,
    )(*args)


MOE_TILE = 512
MOE_TILE_SHIFT = 9
MOE_TOKENS = 256
ROW_UNROLL = 8


SUBLANES = 8


def _rows_to_tiles(ref, x, lead=()):
    rows = x.shape[0]
    for c in range(SUBLANES):
        ref[(*lead, pl.ds(c, rows, stride=SUBLANES), slice(None))] = x[:, c * LANES:(c + 1) * LANES]


def _tiles_to_rows(ref, rows, lead=()):
    return jnp.concatenate([ref[(*lead, pl.ds(c, rows, stride=SUBLANES), slice(None))] for c in range(SUBLANES)],
                           axis=1)


def _slot_tiles(rows):
    return (2 * rows) // MOE_TILE + N_EXPERTS


def _expert_offsets(cnt_ref, off_ref):
    def per_expert(e, k):
        off_ref[e] = k * MOE_TILE
        return k + ((cnt_ref[e] + MOE_TILE - 1) >> MOE_TILE_SHIFT)
    return lax.fori_loop(0, N_EXPERTS, per_expert, 0)


def _slots_kernel(cnt_ref, meta_ref, slot_ref, off_ref):
    @pl.when(pl.program_id(0) == 0)
    def _():
        _expert_offsets(cnt_ref, off_ref)

    e_a, e_b = meta_ref[0:1, :], meta_ref[1:2, :]
    off_a = jnp.zeros_like(e_a)
    off_b = jnp.zeros_like(e_b)
    for e in range(N_EXPERTS):
        off_a = jnp.where(e_a == e, off_ref[e], off_a)
        off_b = jnp.where(e_b == e, off_ref[e], off_b)
    slot_ref[...] = jnp.concatenate([off_a + meta_ref[2:3, :], off_b + meta_ref[3:4, :],
                                     jnp.zeros((6, e_a.shape[1]), jnp.int32)], axis=0)


def _slots(meta, counts, *, tm=1024):
    rows = meta.shape[1]
    grid_spec = pltpu.PrefetchScalarGridSpec(
        num_scalar_prefetch=1, grid=(rows // tm,),
        in_specs=[pl.BlockSpec((8, tm), lambda i, cnt: (0, i))],
        out_specs=pl.BlockSpec((8, tm), lambda i, cnt: (0, i)),
        scratch_shapes=[pltpu.SMEM((N_EXPERTS,), jnp.int32)])
    return pl.pallas_call(
        _slots_kernel, grid_spec=grid_spec, out_shape=jax.ShapeDtypeStruct((8, rows), jnp.int32),
        compiler_params=_cparams("arbitrary"), name="moe_slots",
    )(counts, meta)


def _dispatch_kernel(*refs, steps):
    sa_ref, sb_ref, cnt_ref = refs[:3]
    h_refs = refs[3:3 + len(steps)]
    xs_ref, info_ref, off_ref, zero_ref, sem = refs[3 + len(steps):]
    i = pl.program_id(0)
    tm = h_refs[0].shape[0] // SUBLANES
    n_tiles = info_ref.shape[0] - 1
    tile_rows = MOE_TILE * SUBLANES

    def tile_copy(tile):
        return pltpu.make_async_copy(zero_ref, xs_ref.at[pl.ds(tile * tile_rows, tile_rows), :], sem)

    @pl.when(i == 0)
    def _():
        zero_ref[...] = jnp.zeros_like(zero_ref)
        used = _expert_offsets(cnt_ref, off_ref)

        def per_expert(e, _):
            first = off_ref[e] >> MOE_TILE_SHIFT
            nt = (cnt_ref[e] + MOE_TILE - 1) >> MOE_TILE_SHIFT

            def fill(j, _):
                info_ref[first + j] = e
                return 0
            lax.fori_loop(0, nt, fill, 0)

            @pl.when(nt > 0)
            def _():
                tile_copy(first + nt - 1).start()
                tile_copy(first + nt - 1).wait()
            return 0
        lax.fori_loop(0, N_EXPERTS, per_expert, 0)
        info_ref[n_tiles] = used

        def tail(j, _):
            info_ref[j] = N_EXPERTS - 1
            tile_copy(j).start()
            tile_copy(j).wait()
            return 0
        lax.fori_loop(used, n_tiles, tail, 0)

    base = i * tm

    def copy_rows(h_ref):
        def row_copy(t, slot):
            dst = pl.multiple_of(slot * SUBLANES, SUBLANES)
            return pltpu.make_async_copy(h_ref.at[pl.ds(t * SUBLANES, SUBLANES), :],
                                         xs_ref.at[pl.ds(dst, SUBLANES), :], sem)

        for t in range(tm):
            row_copy(t, sa_ref[base + t]).start(priority=0)
            row_copy(t, sb_ref[base + t]).start(priority=1)
        for _ in range(2):
            pltpu.make_async_copy(h_ref, xs_ref.at[pl.ds(0, tm * SUBLANES), :], sem).wait()

    first = 0
    for h_ref, n in zip(h_refs, steps):
        pl.when(jnp.logical_and(i >= first, i < first + n))(functools.partial(copy_rows, h_ref))
        first += n


def _dispatch(hs, slots, counts):
    tm = MOE_TOKENS
    steps = tuple(h.shape[0] // (tm * SUBLANES) for h in hs)
    n_tiles = _slot_tiles(sum(steps) * tm)
    specs, first = [], 0
    for n in steps:
        specs.append(pl.BlockSpec((tm * SUBLANES, LANES),
                                  lambda i, *_, first=first, n=n: (jnp.clip(i - first, 0, n - 1), 0)))
        first += n
    grid_spec = pltpu.PrefetchScalarGridSpec(
        num_scalar_prefetch=3, grid=(sum(steps),), in_specs=specs,
        out_specs=[pl.BlockSpec(memory_space=pl.ANY), pl.BlockSpec(memory_space=pltpu.SMEM)],
        scratch_shapes=[pltpu.SMEM((N_EXPERTS,), jnp.int32), pltpu.VMEM((MOE_TILE * SUBLANES, LANES), F32),
                        pltpu.SemaphoreType.DMA(())])
    return pl.pallas_call(
        functools.partial(_dispatch_kernel, steps=steps), grid_spec=grid_spec,
        out_shape=[jax.ShapeDtypeStruct((n_tiles * MOE_TILE * SUBLANES, LANES), F32),
                   jax.ShapeDtypeStruct((n_tiles + 1,), jnp.int32)],
        compiler_params=_cparams("arbitrary"), name="moe_dispatch",
    )(slots[0], slots[1], counts, *hs)


def _ffn_kernel(info_ref, xs_ref, wg_ref, wu_ref, wd_ref, ys_ref, wg_s, wu_s, wd_s):
    i = pl.program_id(0)
    used = info_ref[info_ref.shape[0] - 1]
    fresh = jnp.logical_or(i == 0, info_ref[i] != info_ref[jnp.maximum(i - 1, 0)])

    @pl.when(jnp.logical_and(i < used, fresh))
    def _():
        wg_s[...] = wg_ref[0, 0].astype(BF16)
        wu_s[...] = wu_ref[0, 0].astype(BF16)
        wd_s[...] = wd_ref[0, 0].astype(BF16)

    @pl.when(i < used)
    def _():
        x = _tiles_to_rows(xs_ref, MOE_TILE).astype(BF16)
        hid = _silu(_dot(x, wg_s[...])) * _dot(x, wu_s[...])
        _rows_to_tiles(ys_ref, _dot(hid.astype(BF16), wd_s[...]))

    @pl.when(i >= used)
    def _():
        ys_ref[...] = jnp.zeros_like(ys_ref)


def _ffn(xs, info, wg, wu, wd, layer):
    tile_rows = MOE_TILE * SUBLANES
    n_tiles = xs.shape[0] // tile_rows
    d, f = wg.shape[2:]

    def w_map(i, info):
        return (layer, info[i], 0, 0)

    grid_spec = pltpu.PrefetchScalarGridSpec(
        num_scalar_prefetch=1, grid=(n_tiles,),
        in_specs=[pl.BlockSpec((tile_rows, LANES), lambda i, info: (jnp.minimum(i, info[n_tiles] - 1), 0)),
                  pl.BlockSpec((1, 1, d, f), w_map), pl.BlockSpec((1, 1, d, f), w_map),
                  pl.BlockSpec((1, 1, f, d), w_map)],
        out_specs=pl.BlockSpec((tile_rows, LANES), lambda i, info: (i, 0)),
        scratch_shapes=[pltpu.VMEM((d, f), BF16), pltpu.VMEM((d, f), BF16), pltpu.VMEM((f, d), BF16)])
    return pl.pallas_call(
        _ffn_kernel, grid_spec=grid_spec, out_shape=jax.ShapeDtypeStruct(xs.shape, F32),
        compiler_params=_cparams("arbitrary"), name="moe_ffn",
    )(info, xs, wg, wu, wd)


def _combine_kernel(*refs, has_mod, out_x):
    refs = list(refs)
    sa_ref, sb_ref, x_ref, ys_ref, wcol_ref, gate_ref, g_ref = refs[:7]
    refs = refs[7:]
    mod_refs = (refs.pop(0), refs.pop(0)) if has_mod else None
    xo_ref = refs.pop(0) if out_x else None
    h_ref, buf_a, buf_b, sems = refs
    i = pl.program_id(0)
    tm = x_ref.shape[0]

    def issue(tile, slot):
        base = tile * tm
        for t in range(tm):
            dst = pl.ds(t * SUBLANES, SUBLANES)
            src_a = pl.multiple_of(sa_ref[base + t] * SUBLANES, SUBLANES)
            src_b = pl.multiple_of(sb_ref[base + t] * SUBLANES, SUBLANES)
            pltpu.make_async_copy(ys_ref.at[pl.ds(src_a, SUBLANES), :], buf_a.at[slot, dst, :],
                                  sems.at[slot]).start(priority=0)
            pltpu.make_async_copy(ys_ref.at[pl.ds(src_b, SUBLANES), :], buf_b.at[slot, dst, :],
                                  sems.at[slot]).start(priority=1)

    @pl.when(i == 0)
    def _():
        issue(0, 0)

    @pl.when(i + 1 < pl.num_programs(0))
    def _():
        issue(i + 1, (i + 1) % 2)

    slot = i % 2
    for buf in (buf_a, buf_b):
        pltpu.make_async_copy(ys_ref.at[pl.ds(0, tm * SUBLANES), :], buf.at[slot], sems.at[slot]).wait()
    y = (wcol_ref[:, 0:1] * _tiles_to_rows(buf_a, tm, lead=(slot,))
         + wcol_ref[:, 1:2] * _tiles_to_rows(buf_b, tm, lead=(slot,)))
    x = x_ref[...] + gate_ref[0] * y
    if out_x:
        xo_ref[...] = x
    h_ref[...] = _norm_mod(x, g_ref, mod_refs).astype(h_ref.dtype)


def _combine(x, ys, slots, wcol, gate, g, *, mod=None, out_x=False, h_dtype=BF16):
    rows, d = x.shape
    tm = MOE_TOKENS
    row_spec = pl.BlockSpec((tm, d), lambda i, *_: (i, 0))
    args = [x, ys, wcol, gate, g.reshape(1, d)]
    specs = [row_spec, pl.BlockSpec(memory_space=pl.ANY), pl.BlockSpec((tm, LANES), lambda i, *_: (i, 0)),
             _mod_spec(gate.shape[0], rows, tm, d), pl.BlockSpec((1, d), lambda i, *_: (0, 0))]
    if mod is not None:
        for m in mod:
            args.append(m)
            specs.append(_mod_spec(m.shape[0], rows, tm, d))
    out_shape, out_specs = [], []
    if out_x:
        out_shape.append(jax.ShapeDtypeStruct((rows, d), F32))
        out_specs.append(row_spec)
    out_shape.append(jax.ShapeDtypeStruct((rows, d), h_dtype))
    out_specs.append(row_spec)
    grid_spec = pltpu.PrefetchScalarGridSpec(
        num_scalar_prefetch=2, grid=(rows // tm,), in_specs=specs, out_specs=out_specs,
        scratch_shapes=[pltpu.VMEM((2, tm * SUBLANES, LANES), F32), pltpu.VMEM((2, tm * SUBLANES, LANES), F32),
                        pltpu.SemaphoreType.DMA((2,))])
    outs = pl.pallas_call(
        functools.partial(_combine_kernel, has_mod=mod is not None, out_x=out_x), grid_spec=grid_spec,
        out_shape=out_shape, compiler_params=_cparams("arbitrary"), name="moe_combine",
    )(slots[0], slots[1], *args)
    return outs if out_x else outs[0]


def _softmax_av(scores, values, sink):
    rows, hd = scores[0].shape[0], values[0].shape[1]
    m = sink
    for s in scores:
        m = jnp.maximum(m, jnp.broadcast_to(jnp.max(s, axis=-1, keepdims=True), (rows, LANES)))
    tot = None
    for s, v in zip(scores, values):
        n = s.shape[1]
        p = jnp.exp(s - jnp.concatenate([m] * (n // LANES), axis=1)).astype(BF16)
        v_ext = jnp.concatenate([v, jnp.zeros((n, LANES - hd), BF16), jnp.ones((n, LANES), BF16)], axis=1)
        pv = _dot(p, v_ext)
        tot = pv if tot is None else tot + pv
    den = tot[:, LANES:] + jnp.exp(sink - m)
    return tot[:, :hd] / den[:, :hd]


def _sink_column(sink_ref, kv, rows):
    return jnp.concatenate([jnp.full((rows, LANES), sink_ref[kv * ATT_GROUP + g], F32) for g in range(ATT_GROUP)],
                           axis=0)


def _attn_ctx_kernel(sink_ref, qkv_ref, o_ref):
    t = qkv_ref.shape[0]
    qw = ATT_HEADS * HEAD_DIM
    kw = ATT_KV * HEAD_DIM
    heads_out = []
    for kv in range(ATT_KV):
        q = jnp.concatenate(
            [qkv_ref[:, (kv * ATT_GROUP + g) * HEAD_DIM:(kv * ATT_GROUP + g + 1) * HEAD_DIM] for g in range(ATT_GROUP)],
            axis=0).astype(BF16)
        k = qkv_ref[:, qw + kv * HEAD_DIM:qw + (kv + 1) * HEAD_DIM].astype(BF16)
        v = qkv_ref[:, qw + kw + kv * HEAD_DIM:qw + kw + (kv + 1) * HEAD_DIM].astype(BF16)
        s = _dot_nt(q, k) * HEAD_DIM ** -0.5
        o = _softmax_av([s], [v], _sink_column(sink_ref, kv, t))
        heads_out += [o[g * t:(g + 1) * t] for g in range(ATT_GROUP)]
    o_ref[...] = jnp.concatenate(heads_out, axis=1).astype(o_ref.dtype)


def _attn_ctx(qkv, sink, n_seq, seq_len):
    rows, cols = qkv.shape
    return pl.pallas_call(
        _attn_ctx_kernel, grid=(n_seq,),
        in_specs=[pl.BlockSpec(memory_space=pltpu.SMEM), pl.BlockSpec((seq_len, cols), lambda b: (b, 0))],
        out_specs=pl.BlockSpec((seq_len, ATT_HEADS * HEAD_DIM), lambda b: (b, 0)),
        out_shape=jax.ShapeDtypeStruct((rows, ATT_HEADS * HEAD_DIM), BF16),
        compiler_params=_cparams("parallel"), name="attn_context",
    )(sink, qkv)


def _rope_block(x, cos, sin_signed):
    lane = lax.broadcasted_iota(jnp.int32, x.shape, 1)
    nf = HEAD_DIM // 4
    partner = jnp.where((lane % (2 * nf)) < nf, pltpu.roll(x, LANES - nf, axis=1), pltpu.roll(x, nf, axis=1))
    return x * cos + partner * sin_signed


def _attn_lat_kernel(sink_ref, qkv_ref, ck_ref, cv_ref, cos_ref, sin_ref, o_ref, k_scr):
    i = pl.program_id(1)
    t = qkv_ref.shape[0]
    qw = ATT_HEADS * HEAD_DIM
    kw = ATT_KV * HEAD_DIM
    span = Q_BLOCK + 2 * WINDOW

    @pl.when(i == 0)
    def _():
        for c in range(kw // LANES):
            blk = qkv_ref[:, qw + c * LANES:qw + (c + 1) * LANES]
            k_scr[:, c * LANES:(c + 1) * LANES] = _rope_block(blk, cos_ref[...], sin_ref[...]).astype(BF16)

    r0 = pl.multiple_of(i * Q_BLOCK, Q_BLOCK)
    ws = pl.multiple_of(jnp.clip(r0 - WINDOW, 0, t - span), Q_BLOCK)
    cos_q = cos_ref[pl.ds(r0, Q_BLOCK), :]
    sin_q = sin_ref[pl.ds(r0, Q_BLOCK), :]
    qpos = r0 + lax.broadcasted_iota(jnp.int32, (Q_BLOCK, span), 0)
    kpos = ws + lax.broadcasted_iota(jnp.int32, (Q_BLOCK, span), 1)
    band = jnp.abs(qpos - kpos) <= WINDOW
    band = jnp.concatenate([band] * ATT_GROUP, axis=0)
    heads_out = []
    for kv in range(ATT_KV):
        heads = []
        for g in range(ATT_GROUP):
            h = kv * ATT_GROUP + g
            c, half = divmod(h * HEAD_DIM, LANES)
            blk = _rope_block(qkv_ref[pl.ds(r0, Q_BLOCK), c * LANES:(c + 1) * LANES], cos_q, sin_q)
            heads.append(blk[:, half:half + HEAD_DIM])
        q = jnp.concatenate(heads, axis=0).astype(BF16)
        ck = ck_ref[0, :, kv * HEAD_DIM:(kv + 1) * HEAD_DIM].astype(BF16)
        cv = cv_ref[0, :, kv * HEAD_DIM:(kv + 1) * HEAD_DIM].astype(BF16)
        kwin = k_scr[pl.ds(ws, span), kv * HEAD_DIM:(kv + 1) * HEAD_DIM]
        vwin = qkv_ref[pl.ds(ws, span), qw + kw + kv * HEAD_DIM:qw + kw + (kv + 1) * HEAD_DIM].astype(BF16)
        s_ctx = _dot_nt(q, ck) * HEAD_DIM ** -0.5
        s_win = jnp.where(band, _dot_nt(q, kwin) * HEAD_DIM ** -0.5, -jnp.inf)
        o = _softmax_av([s_ctx, s_win], [cv, vwin], _sink_column(sink_ref, kv, Q_BLOCK))
        heads_out += [o[g * Q_BLOCK:(g + 1) * Q_BLOCK] for g in range(ATT_GROUP)]
    o_ref[...] = jnp.concatenate(heads_out, axis=1).astype(o_ref.dtype)


def _rope_tables(seq_len):
    pos = jnp.arange(seq_len, dtype=jnp.int32)
    row = (pos // GRID_W).astype(F32)
    col = (pos % GRID_W).astype(F32)
    nf = HEAD_DIM // 4
    inv = ROPE_BASE ** (-jnp.arange(nf, dtype=F32) / nf)
    ang_r = row[:, None] * inv[None, :]
    ang_c = col[:, None] * inv[None, :]
    cos_h = jnp.concatenate([jnp.cos(ang_r), jnp.cos(ang_r), jnp.cos(ang_c), jnp.cos(ang_c)], axis=1)
    sin_h = jnp.concatenate([-jnp.sin(ang_r), jnp.sin(ang_r), -jnp.sin(ang_c), jnp.sin(ang_c)], axis=1)
    reps = LANES // HEAD_DIM
    return jnp.tile(cos_h, (1, reps)), jnp.tile(sin_h, (1, reps))


def _attn_lat(qkv, cache_k, cache_v, sink, n_seq, seq_len):
    rows, cols = qkv.shape
    past = cache_k.shape[1]
    kw = ATT_KV * HEAD_DIM
    cos, sin = _rope_tables(seq_len)
    return pl.pallas_call(
        _attn_lat_kernel, grid=(n_seq, seq_len // Q_BLOCK),
        in_specs=[pl.BlockSpec(memory_space=pltpu.SMEM),
                  pl.BlockSpec((seq_len, cols), lambda b, i: (b, 0)),
                  pl.BlockSpec((1, past, kw), lambda b, i: (b, 0, 0)),
                  pl.BlockSpec((1, past, kw), lambda b, i: (b, 0, 0)),
                  pl.BlockSpec((seq_len, LANES), lambda b, i: (0, 0)),
                  pl.BlockSpec((seq_len, LANES), lambda b, i: (0, 0))],
        out_specs=pl.BlockSpec((Q_BLOCK, ATT_HEADS * HEAD_DIM), lambda b, i: (b * (seq_len // Q_BLOCK) + i, 0)),
        out_shape=jax.ShapeDtypeStruct((rows, ATT_HEADS * HEAD_DIM), BF16),
        scratch_shapes=[pltpu.VMEM((seq_len, kw), BF16)],
        compiler_params=_cparams("parallel", "arbitrary"), name="attn_latent",
    )(sink, qkv, cache_k.reshape(n_seq, past, kw), cache_v.reshape(n_seq, past, kw), cos, sin)


def _split_bf16(w):
    hi = w.astype(BF16)
    return hi, (w - hi.astype(F32)).astype(BF16)


def _select_lanes(x, sel):
    x1 = x.astype(BF16)
    rest = x - x1.astype(F32)
    x2 = rest.astype(BF16)
    x3 = (rest - x2.astype(F32)).astype(BF16)
    return _dot(x1, sel) + _dot(x2, sel) + _dot(x3, sel)


def _ml_gates_kernel(h_ref, w_ref, wt_ref, b_ref, bt_ref, g_ref, gt_ref):
    h = h_ref[...]
    h_hi = h.astype(BF16)
    h_lo = (h - h_hi.astype(F32)).astype(BF16)
    ng = gt_ref.shape[0]
    by_hi = _dot(h_hi, w_ref[...])
    g_ref[...] = by_hi[:, :LANES] + by_hi[:, LANES:] + _dot(h_lo, w_ref[:, :LANES]) + b_ref[...]
    by_hi_t = _dot_nt(wt_ref[...], h_hi)
    gt_ref[...] = by_hi_t[:ng] + by_hi_t[ng:] + _dot_nt(wt_ref[:ng, :], h_lo) + bt_ref[...]


def _ml_gates(h, w_gates, b_gates, *, tm=512):
    rows, d = h.shape
    ng = w_gates.shape[1]
    w_hi, w_lo = _split_bf16(jnp.pad(w_gates, ((0, 0), (0, LANES - ng))))
    wt_hi, wt_lo = _split_bf16(w_gates.T)
    b_pad = jnp.pad(b_gates, (0, LANES - ng)).reshape(1, LANES)
    return pl.pallas_call(
        _ml_gates_kernel, grid=(rows // tm,),
        in_specs=[pl.BlockSpec((tm, d), lambda i: (i, 0)), pl.BlockSpec((d, 2 * LANES), lambda i: (0, 0)),
                  pl.BlockSpec((2 * ng, d), lambda i: (0, 0)), pl.BlockSpec((1, LANES), lambda i: (0, 0)),
                  pl.BlockSpec((ng, 1), lambda i: (0, 0))],
        out_specs=[pl.BlockSpec((tm, LANES), lambda i: (i, 0)), pl.BlockSpec((ng, tm), lambda i: (0, i))],
        out_shape=[jax.ShapeDtypeStruct((rows, LANES), F32), jax.ShapeDtypeStruct((ng, rows), F32)],
        compiler_params=_cparams("parallel"), name="mlstm_gates",
    )(h, jnp.concatenate([w_hi, w_lo], axis=1), jnp.concatenate([wt_hi, wt_lo], axis=0), b_pad,
      b_gates.reshape(ng, 1))


def _ml_qk_kernel(h_ref, w_ref, cw_ref, o_ref, *, seq_len, k_scale):
    j = pl.program_id(1)
    x = _dot(h_ref[...].astype(BF16), w_ref[...])
    t = x.shape[0]
    pos = lax.broadcasted_iota(jnp.int32, x.shape, 0) % seq_len
    prev = jnp.where(pos == 0, 0.0, pltpu.roll(x, 1, axis=0))
    nxt = jnp.where(pos == seq_len - 1, 0.0, pltpu.roll(x, t - 1, axis=0))
    y = prev * cw_ref[0:1, :] + x * cw_ref[1:2, :] + nxt * cw_ref[2:3, :]
    scale = jnp.where(j >= pl.num_programs(1) // 2, k_scale, 1.0).astype(F32)
    o_ref[...] = (_silu(y) * scale).astype(o_ref.dtype)


def _ml_qk(h, w_qk, conv_w, seq_len, *, tm=1024, tn=1024):
    rows, d = h.shape
    width = w_qk.shape[1]
    return pl.pallas_call(
        functools.partial(_ml_qk_kernel, seq_len=seq_len, k_scale=ML_DK ** -0.5), grid=(rows // tm, width // tn),
        in_specs=[pl.BlockSpec((tm, d), lambda i, j: (i, 0)), pl.BlockSpec((d, tn), lambda i, j: (0, j)),
                  pl.BlockSpec((3, tn), lambda i, j: (0, j))],
        out_specs=pl.BlockSpec((tm, tn), lambda i, j: (i, j)),
        out_shape=jax.ShapeDtypeStruct((rows, width), BF16),
        compiler_params=_cparams("parallel", "parallel"), name="mlstm_qk",
    )(h, w_qk, conv_w)


def _ml_scan_kernel(*refs, zero_init):
    refs = list(refs)
    dirs = [tuple(refs[0:5]), tuple(refs[5:10])]
    sel_ref = refs[10]
    refs = refs[11:]
    if not zero_init:
        c0_ref, n0_ref, m0_ref = refs[:3]
        refs = refs[3:]
    hf_ref, hb_ref, c_ref, n_ref, m_ref = refs
    h_out = (hf_ref, hb_ref)
    c = pl.program_id(1)
    last = pl.num_programs(1) - 1

    @pl.when(c == 0)
    def _():
        if zero_init:
            c_ref[...] = jnp.zeros_like(c_ref)
            n_ref[...] = jnp.zeros_like(n_ref)
            m_ref[...] = jnp.zeros_like(m_ref)
        else:
            c_ref[...] = c0_ref[...]
            n_ref[...] = n0_ref[...]
            m_ref[...] = m0_ref[...]

    length = hf_ref.shape[0]
    ti = lax.broadcasted_iota(jnp.int32, (length, length), 0)
    si = lax.broadcasted_iota(jnp.int32, (length, length), 1)
    for d in range(2):
        q_ref, k_ref, v_ref, g_ref, gt_ref = dirs[d]
        causal = (ti >= si) if d == 0 else (ti <= si)
        tri = jnp.where(causal, 1.0, 0.0).astype(F32)
        g_col = g_ref[...]
        f_row = _log_sigmoid(gt_ref[...])
        b_col = jnp.dot(tri, _log_sigmoid(g_col), precision=HI, preferred_element_type=F32)
        b_row = _dot_nt(f_row, tri, precision=HI)
        i_rep = _select_lanes(g_col, sel_ref[d, 0])
        b_rep = _select_lanes(b_col, sel_ref[d, 1])
        edge = length - 1 if d == 0 else 0
        for h in range(ML_HEADS):
            ji = d * 2 * ML_HEADS + h
            jf = ji + ML_HEADS
            bc = b_rep[:, h * LANES:(h + 1) * LANES]
            i_col = i_rep[:, h * LANES:(h + 1) * LANES]
            br = b_row[jf:jf + 1, :]
            i_row = gt_ref[ji:ji + 1, :]
            m_rep = m_ref[0, d, h]
            q = q_ref[:, h * ML_DK:(h + 1) * ML_DK]
            k = k_ref[:, h * ML_DK:(h + 1) * ML_DK]
            v = v_ref[:, h * ML_DV:(h + 1) * ML_DV].astype(BF16)
            cst = c_ref[0, d, h]
            nst = n_ref[0, d, h]
            a_row = i_row - br
            amat = jnp.where(causal, a_row, -jnp.inf)
            u = jnp.maximum(m_rep, jnp.broadcast_to(jnp.max(amat, axis=1, keepdims=True), (length, LANES)))
            qk = (_dot_nt(q, k) * jnp.exp(amat - u)).astype(BF16)
            sc = jnp.exp(m_rep - u)
            sc = jnp.concatenate([sc] * (ML_DV // LANES + 1), axis=1)
            state_ext = jnp.concatenate([cst, jnp.broadcast_to(nst, (LANES, ML_DK))], axis=0).astype(BF16)
            v_ext = jnp.concatenate([v, jnp.ones((length, LANES), BF16)], axis=1)
            tot = sc * _dot_nt(q, state_ext) + _dot(qk, v_ext)
            inv = 1.0 / jnp.maximum(jnp.abs(tot[:, ML_DV:]), jnp.exp(-(bc + u)))
            h_out[d][:, h * ML_DV:(h + 1) * ML_DV] = tot[:, :ML_DV] * jnp.concatenate([inv] * (ML_DV // LANES), axis=1)
            b_last = bc[edge:edge + 1, :]
            wlog_row = b_last + a_row
            m_new = jnp.maximum(b_last + m_rep, jnp.max(wlog_row, axis=1, keepdims=True))
            decay = jnp.exp(b_last + m_rep - m_new)
            ws_row = jnp.exp(wlog_row - m_new)
            ws_col = jnp.exp(b_last - bc + i_col - m_new)
            kw = (ws_col * k.astype(F32)).astype(BF16)
            c_ref[0, d, h] = decay * cst + _dot_tn(v, kw)
            n_ref[0, d, h] = decay * nst + _dot(jnp.broadcast_to(ws_row, (8, length)).astype(BF16), k)[0:1]
            m_ref[0, d, h] = m_new


def _ml_scan(qk, p, g, gt, state, n_seq, seq_len):
    rows = qk.shape[0]
    length = min(ML_CHUNK, seq_len)
    nc = seq_len // length
    qw = ML_HEADS * ML_DK
    vw = ML_HEADS * ML_DV
    ng = gt.shape[0]

    def fwd(b, c):
        return b * nc + c

    def bwd(b, c):
        return b * nc + nc - 1 - c

    args, specs = [], []
    for pos in (fwd, bwd):
        args += [qk, qk, p, g, gt]
        specs += [pl.BlockSpec((length, qw), lambda b, c, pos=pos: (pos(b, c), 0)),
                  pl.BlockSpec((length, qw), lambda b, c, pos=pos: (pos(b, c), 1)),
                  pl.BlockSpec((length, vw), lambda b, c, pos=pos: (pos(b, c), 0)),
                  pl.BlockSpec((length, LANES), lambda b, c, pos=pos: (pos(b, c), 0)),
                  pl.BlockSpec((ng, length), lambda b, c, pos=pos: (0, pos(b, c)))]
    assert length == LANES and ML_DK == LANES
    gate_lane = jnp.arange(LANES)[:, None]
    head = (jnp.arange(ML_HEADS * LANES) // LANES)[None, :]
    sel = jnp.stack([jnp.stack([gate_lane == (2 * d + kind) * ML_HEADS + head for kind in range(2)])
                     for d in range(2)]).astype(BF16)
    args.append(sel)
    specs.append(pl.BlockSpec(sel.shape, lambda b, c: (0, 0, 0, 0)))
    c_spec = pl.BlockSpec((1, 2, ML_HEADS, ML_DV, ML_DK), lambda b, c: (b, 0, 0, 0, 0))
    n_spec = pl.BlockSpec((1, 2, ML_HEADS, 1, ML_DK), lambda b, c: (b, 0, 0, 0, 0))
    zero_init = state is None
    if not zero_init:
        c0, n0, m0 = state
        args += [c0, n0.reshape(n_seq, 2, ML_HEADS, 1, ML_DK),
                 jnp.broadcast_to(m0[..., None, None], (n_seq, 2, ML_HEADS, 1, ML_DK))]
        specs += [c_spec, n_spec, n_spec]
    hf, hb, c_fin, n_fin, m_fin = pl.pallas_call(
        functools.partial(_ml_scan_kernel, zero_init=zero_init), grid=(n_seq, nc), in_specs=specs,
        out_specs=[pl.BlockSpec((length, vw), lambda b, c: (fwd(b, c), 0)),
                   pl.BlockSpec((length, vw), lambda b, c: (bwd(b, c), 0)), c_spec, n_spec, n_spec],
        out_shape=[jax.ShapeDtypeStruct((rows, vw), F32), jax.ShapeDtypeStruct((rows, vw), F32),
                   jax.ShapeDtypeStruct((n_seq, 2, ML_HEADS, ML_DV, ML_DK), F32),
                   jax.ShapeDtypeStruct((n_seq, 2, ML_HEADS, 1, ML_DK), F32),
                   jax.ShapeDtypeStruct((n_seq, 2, ML_HEADS, 1, ML_DK), F32)],
        compiler_params=_cparams("parallel", "arbitrary"), name="mlstm_scan",
    )(*args)
    return hf, hb, (c_fin, n_fin[:, :, :, 0, :], m_fin[:, :, :, 0, 0])


def _gla_scan_kernel(*refs, zero_init):
    refs = list(refs)
    dirs = [tuple(refs[0:4]), tuple(refs[4:8])]
    w2_ref, ba_ref = refs[8:10]
    refs = refs[10:]
    if not zero_init:
        s0_ref = refs.pop(0)
    of_ref, ob_ref, s_ref, st_scr, la_scr = refs
    o_out = (of_ref, ob_ref)
    c = pl.program_id(1)
    last = pl.num_programs(1) - 1
    kw = GLA_HEADS * GLA_DK
    n_sub = of_ref.shape[0] // GLA_SUB

    @pl.when(c == 0)
    def _():
        for d in range(2):
            for h in range(GLA_HEADS):
                st_scr[d, h] = jnp.zeros((GLA_DV, GLA_DK), F32) if zero_init else s0_ref[0, d, h].T

    for d in range(2):
        u = dirs[d][3][...].astype(BF16)
        z = _dot(u, w2_ref[:, d * kw:(d + 1) * kw]) + ba_ref[:, d * kw:(d + 1) * kw]
        la_scr[d] = _log_sigmoid(z) / GLA_TAU

    ti = lax.broadcasted_iota(jnp.int32, (GLA_SUB, GLA_SUB), 0)
    si = lax.broadcasted_iota(jnp.int32, (GLA_SUB, GLA_SUB), 1)
    s_lane = lax.broadcasted_iota(jnp.int32, (GLA_SUB, GLA_SUB), 1)

    def sub_chunk(j, carry):
        for d in range(2):
            q_ref, k_ref, v_ref, _ = dirs[d]
            r0 = pl.multiple_of((j if d == 0 else n_sub - 1 - j) * GLA_SUB, GLA_SUB)
            causal = (ti >= si) if d == 0 else (ti <= si)
            tri = jnp.where(causal, 1.0, 0.0).astype(F32)
            bc_all = jnp.dot(tri, la_scr[d, pl.ds(r0, GLA_SUB), :], precision=HI, preferred_element_type=F32)
            edge = GLA_SUB - 1 if d == 0 else 0
            for h in range(GLA_HEADS):
                bc = bc_all[:, h * GLA_DK:(h + 1) * GLA_DK]
                q = q_ref[pl.ds(r0, GLA_SUB), h * GLA_DK:(h + 1) * GLA_DK] * GLA_DK ** -0.5
                k = k_ref[pl.ds(r0, GLA_SUB), h * GLA_DK:(h + 1) * GLA_DK]
                v = v_ref[pl.ds(r0, GLA_SUB), h * GLA_DV:(h + 1) * GLA_DV].astype(BF16)
                bc2 = bc * LOG2E
                a = jnp.zeros((GLA_SUB, GLA_SUB), F32)
                for s in range(GLA_SUB):
                    decay = jnp.exp2(bc2 - bc2[s:s + 1, :])
                    col = jnp.sum(q * (k[s:s + 1, :] * decay), axis=1, keepdims=True)
                    a = jnp.where(s_lane == s, col, a)
                a = jnp.where(causal, a, 0.0)
                st = st_scr[d, h]
                o = _dot(a.astype(BF16), v) + _dot_nt((q * jnp.exp2(bc2)).astype(BF16), st.astype(BF16))
                o_out[d][pl.ds(r0, GLA_SUB), h * GLA_DV:(h + 1) * GLA_DV] = o
                b_last = bc2[edge:edge + 1, :]
                k_dec = (k * jnp.exp2(b_last - bc2)).astype(BF16)
                st_scr[d, h] = jnp.exp2(b_last) * st + _dot_tn(v, k_dec)
        return carry

    lax.fori_loop(0, n_sub, sub_chunk, 0)

    @pl.when(c == last)
    def _():
        for d in range(2):
            for h in range(GLA_HEADS):
                s_ref[0, d, h] = st_scr[d, h].T


def _gla_scan(p, u, w2, b_a, state, n_seq, seq_len):
    rows = p.shape[0]
    length = min(GLA_BLOCK, seq_len)
    nc = seq_len // length
    kw = GLA_HEADS * GLA_DK
    vw = GLA_HEADS * GLA_DV

    def fwd(b, c):
        return b * nc + c

    def bwd(b, c):
        return b * nc + nc - 1 - c

    args, specs = [], []
    for pos in (fwd, bwd):
        args += [p, p, p, u]
        specs += [pl.BlockSpec((length, kw), lambda b, c, pos=pos: (pos(b, c), 0)),
                  pl.BlockSpec((length, kw), lambda b, c, pos=pos: (pos(b, c), 1)),
                  pl.BlockSpec((length, vw), lambda b, c, pos=pos: (pos(b, c), 2 * kw // vw)),
                  pl.BlockSpec((length, LANES), lambda b, c, pos=pos: (pos(b, c), 0))]
    args += [w2, b_a]
    specs += [pl.BlockSpec(w2.shape, lambda b, c: (0, 0)), pl.BlockSpec(b_a.shape, lambda b, c: (0, 0))]
    s_spec = pl.BlockSpec((1, 2, GLA_HEADS, GLA_DK, GLA_DV), lambda b, c: (b, 0, 0, 0, 0))
    zero_init = state is None
    if not zero_init:
        args.append(state)
        specs.append(s_spec)
    return pl.pallas_call(
        functools.partial(_gla_scan_kernel, zero_init=zero_init), grid=(n_seq, nc), in_specs=specs,
        out_specs=[pl.BlockSpec((length, vw), lambda b, c: (fwd(b, c), 0)),
                   pl.BlockSpec((length, vw), lambda b, c: (bwd(b, c), 0)), s_spec],
        out_shape=[jax.ShapeDtypeStruct((rows, vw), F32), jax.ShapeDtypeStruct((rows, vw), F32),
                   jax.ShapeDtypeStruct((n_seq, 2, GLA_HEADS, GLA_DK, GLA_DV), F32)],
        scratch_shapes=[pltpu.VMEM((2, GLA_HEADS, GLA_DV, GLA_DK), F32), pltpu.VMEM((2, length, kw), F32)],
        compiler_params=_cparams("parallel", "arbitrary"), name="gla_scan",
    )(*args)


def kernel(x_prompt, x_sample, cache_k_0, cache_v_0, state_mlstm_C_1, state_mlstm_n_1, state_mlstm_m_1, state_gla_S_2, cache_k_3, cache_v_3, c, c_ctx, w_mod, b_mod, norm1_g, norm2_g, final_g, router_w, router_b, moe_wg, moe_wu, moe_wd, attn0_w_qkv, attn0_sink, attn0_w_o, mlstm1_w_in, mlstm1_b_gates, mlstm1_conv, mlstm1_norm_g, mlstm1_w_out, gla2_w_in, gla2_w_a1, gla2_w_a2, gla2_b_a, gla2_norm_g, gla2_w_out, attn3_w_qkv, attn3_sink, attn3_w_o):
    n_ctx, ctx_len, d = x_prompt.shape
    n_lat, lat_len, _ = x_sample.shape
    depth = w_mod.shape[0]

    cvec = jnp.concatenate([c_ctx[None, :], c, jnp.zeros((8 - 1 - n_lat, d), F32)], axis=0)
    mod = _modulation(cvec, w_mod, b_mod).reshape(depth, 8, 6, 1, d)

    def mods(layer, kind, latent):
        return mod[layer, 1:1 + n_lat, kind] if latent else mod[layer, 0:1, kind]

    rw_hi = router_w.T.astype(BF16)
    rw_lo = (router_w.T - rw_hi.astype(F32)).astype(BF16)
    rwt = jnp.concatenate([rw_hi, rw_lo], axis=0)
    rb = router_b.reshape(-1, 1)
    attn_w = {0: (attn0_w_qkv.astype(BF16), attn0_sink, attn0_w_o.astype(BF16), cache_k_0, cache_v_0),
              3: (attn3_w_qkv.astype(BF16), attn3_sink, attn3_w_o.astype(BF16), cache_k_3, cache_v_3)}
    ml_qw = ML_HEADS * ML_DK
    ml_vw = ML_HEADS * ML_DV
    ml_main = 2 * ml_qw + 2 * ml_vw
    ml_w_qk = mlstm1_w_in[:, :2 * ml_qw].astype(BF16)
    ml_w_vo = mlstm1_w_in[:, 2 * ml_qw:ml_main].astype(BF16)
    ml_w_gates = mlstm1_w_in[:, ml_main:]
    ml_w_out = mlstm1_w_out.astype(BF16)
    gla_kw = GLA_HEADS * GLA_DK
    gla_w_in = gla2_w_in.astype(BF16)
    gla_w_a1 = jnp.pad(jnp.concatenate([gla2_w_a1[0], gla2_w_a1[1]], axis=1),
                       ((0, 0), (0, LANES - 2 * GLA_RANK))).astype(BF16)
    gla_w2 = jnp.zeros((LANES, 2 * gla_kw), F32)
    gla_w2 = gla_w2.at[:GLA_RANK, :gla_kw].set(gla2_w_a2[0]).at[GLA_RANK:2 * GLA_RANK, gla_kw:].set(gla2_w_a2[1])
    gla_w2 = gla_w2.astype(BF16)
    gla_ba = gla2_b_a.reshape(1, 2 * gla_kw)
    gla_w_out = gla2_w_out.astype(BF16)

    new_state = []

    def mixer(layer, s, count0):
        latent, n_seq, seq_len, x, h = s["latent"], s["n_seq"], s["seq_len"], s["x"], s["h"]
        tail = (mods(layer, 2, latent), norm2_g[layer], mods(layer, 4, latent), mods(layer, 3, latent), rwt, rb,
                count0)
        kind = layer % 3
        if kind == 0:
            w_qkv, sink, w_o, ck, cv = attn_w[layer]
            qkv = _matmul(h, w_qkv)
            if latent:
                att = _attn_lat(qkv, ck, cv, sink, n_seq, seq_len)
            else:
                att = _attn_ctx(qkv, sink, n_seq, seq_len)
                qw = ATT_HEADS * HEAD_DIM
                kw = ATT_KV * HEAD_DIM
                new_state.append(qkv[:, qw:qw + kw].reshape(n_seq, seq_len, ATT_KV, HEAD_DIM))
                new_state.append(qkv[:, qw + kw:].reshape(n_seq, seq_len, ATT_KV, HEAD_DIM))
            return _proj("plain", (att,), w_o, x, *tail)
        if kind == 1:
            p = _matmul(h, ml_w_vo)
            g, gt = _ml_gates(h, ml_w_gates, mlstm1_b_gates)
            qk = _ml_qk(h, ml_w_qk, mlstm1_conv, seq_len)
            st = (state_mlstm_C_1, state_mlstm_n_1, state_mlstm_m_1) if latent else None
            hf, hb, fin = _ml_scan(qk, p, g, gt, st, n_seq, seq_len)
            if not latent:
                new_state.extend(fin)
            return _proj("mlstm", (hf, hb, p, 1, mlstm1_norm_g), ml_w_out, x, *tail)
        p = _matmul(h, gla_w_in)
        u = _matmul(h, gla_w_a1)
        of, ob, s_fin = _gla_scan(p, u, gla_w2, gla_ba, state_gla_S_2 if latent else None, n_seq, seq_len)
        if not latent:
            new_state.append(s_fin)
        gla_vw = GLA_HEADS * GLA_DV
        return _proj("gla", (of, ob, p, (2 * gla_kw + gla_vw) // gla_vw, gla2_norm_g), gla_w_out, x, *tail)

    streams = [dict(latent=False, n_seq=n_ctx, seq_len=ctx_len, x=x_prompt.reshape(n_ctx * ctx_len, d)),
               dict(latent=True, n_seq=n_lat, seq_len=lat_len, x=x_sample.reshape(n_lat * lat_len, d))]
    for s in streams:
        s["h"] = _rownorm(s["x"], norm1_g[0], mods(0, 1, s["latent"]), mods(0, 0, s["latent"]))
    for layer in range(depth):
        counts = jnp.zeros((N_EXPERTS, LANES), jnp.int32)
        for s in streams:
            s["x"], s["h2"], s["meta"], s["wcol"], counts = mixer(layer, s, counts)
        cnt = counts[:, 0]
        for s in streams:
            s["slots"] = _slots(s["meta"], cnt)
        xs, info = _dispatch([s["h2"] for s in streams], jnp.concatenate([s["slots"] for s in streams], axis=1), cnt)
        ys = _ffn(xs, info, moe_wg, moe_wu, moe_wd, layer)
        for s in streams:
            latent = s["latent"]
            gate2 = mods(layer, 5, latent)
            if layer + 1 < depth:
                s["x"], s["h"] = _combine(s["x"], ys, s["slots"], s["wcol"], gate2, norm1_g[layer + 1],
                                          mod=(mods(layer + 1, 1, latent), mods(layer + 1, 0, latent)), out_x=True,
                                          h_dtype=F32 if (layer + 1) % 3 == 1 else BF16)
            else:
                s["out"] = _combine(s["x"], ys, s["slots"], s["wcol"], gate2, final_g, h_dtype=F32)
    y_prompt = streams[0]["out"].reshape(n_ctx, ctx_len, d)
    y_sample = streams[1]["out"].reshape(n_lat, lat_len, d)
    return (y_prompt, y_sample, *new_state)
```

```python
import functools

import jax
import jax.numpy as jnp
from jax import lax
from jax.experimental import pallas as pl
from jax.experimental.pallas import tpu as pltpu

F32 = jnp.float32
BF16 = jnp.bfloat16
HI = lax.Precision.HIGHEST

EPS = 1e-6
LOG2E = 1.4426950408889634
GRID_W = 64
ATT_HEADS = 16
ATT_KV = 4
ATT_GROUP = ATT_HEADS // ATT_KV
HEAD_DIM = 64
WINDOW = 128
Q_BLOCK = 128
ROPE_BASE = 10000.0
ML_HEADS = 8
ML_DK = 128
ML_DV = 256
ML_CHUNK = 128
GLA_HEADS = 4
GLA_DK = 128
GLA_DV = 256
GLA_RANK = 16
GLA_TAU = 16.0
GLA_SUB = 16
GLA_BLOCK = 256
GLA_MAX_EXP2 = 80.0
N_EXPERTS = 16
N_GROUPS = 4
GROUP_SIZE = N_EXPERTS // N_GROUPS
LANES = 128
VMEM_LIMIT = 56 * 1024 * 1024


def _cparams(*sem):
    return pltpu.CompilerParams(dimension_semantics=sem, vmem_limit_bytes=VMEM_LIMIT)


def _dot(a, b):
    return jnp.dot(a, b, preferred_element_type=F32)


def _dot_nt(a, b, precision=None):
    return lax.dot_general(a, b, (((1,), (1,)), ((), ())), precision=precision, preferred_element_type=F32)


def _dot_tn(a, b):
    return lax.dot_general(a, b, (((0,), (0,)), ((), ())), preferred_element_type=F32)


def _sigmoid(x):
    return 1.0 / (1.0 + jnp.exp(-x))


def _silu(x):
    return x * _sigmoid(x)


def _log_sigmoid(x):
    return jnp.minimum(x, 0.0) - jnp.log(1.0 + jnp.exp(-jnp.abs(x)))


def _rms_rows(x, g):
    ms = jnp.mean(x * x, axis=-1, keepdims=True)
    return x * lax.rsqrt(ms + EPS) * g


def _mod_kernel(c_ref, w_ref, b_ref, o_ref):
    s = _silu(c_ref[...])
    o_ref[0] = _dot(s.astype(BF16), w_ref[0].astype(BF16)) + b_ref[0]


def _modulation(cvec, w_mod, b_mod):
    depth, d, n6 = w_mod.shape
    tn = 1536
    return pl.pallas_call(
        _mod_kernel,
        grid=(depth, n6 // tn),
        in_specs=[pl.BlockSpec((8, d), lambda l, j: (0, 0)),
                  pl.BlockSpec((1, d, tn), lambda l, j: (l, 0, j)),
                  pl.BlockSpec((1, 1, tn), lambda l, j: (l, 0, j))],
        out_specs=pl.BlockSpec((1, 8, tn), lambda l, j: (l, 0, j)),
        out_shape=jax.ShapeDtypeStruct((depth, 8, n6), F32),
        compiler_params=_cparams("parallel", "parallel"),
        name="adaln_modulation",
    )(cvec, w_mod, b_mod.reshape(depth, 1, n6))


def _route(h, rwt, rb, carry):
    tm = h.shape[0]
    h_hi = h.astype(BF16)
    h_lo = (h - h_hi.astype(F32)).astype(BF16)
    by_hi = _dot_nt(rwt, h_hi)
    logits = by_hi[:N_EXPERTS] + by_hi[N_EXPERTS:] + _dot_nt(rwt[:N_EXPERTS], h_lo)
    scores = _sigmoid(logits)
    sel = scores + rb
    expert = lax.broadcasted_iota(jnp.int32, sel.shape, 0)
    pos = expert % GROUP_SIZE
    grp = expert // GROUP_SIZE

    def mate(x, k):
        ahead = pltpu.roll(x, N_EXPERTS - k, axis=0)
        behind = pltpu.roll(x, GROUP_SIZE - k, axis=0)
        return jnp.where(pos + k < GROUP_SIZE, ahead, behind)

    beaten = jnp.zeros_like(sel)
    for k in range(1, GROUP_SIZE):
        other = mate(sel, k)
        other_first = (pos + k) % GROUP_SIZE < pos
        beaten = beaten + jnp.where(other_first, jnp.where(other >= sel, 1.0, 0.0), jnp.where(other > sel, 1.0, 0.0))
    top2 = jnp.where(beaten < 2.0, sel, 0.0)
    gscore = top2
    for k in range(1, GROUP_SIZE):
        gscore = gscore + mate(top2, k)
    lost = jnp.zeros_like(sel)
    for k in range(1, N_GROUPS):
        other = pltpu.roll(gscore, N_EXPERTS - GROUP_SIZE * k, axis=0)
        other_first = (grp + k) % N_GROUPS < grp
        lost = lost + jnp.where(other_first, jnp.where(other >= gscore, 1.0, 0.0),
                                jnp.where(other > gscore, 1.0, 0.0))
    picked = jnp.where(lost < 0.5, jnp.where(beaten < 2.0, 1.0, 0.0), 0.0)
    chosen = picked > 0.5
    weight = jnp.where(chosen, scores, 0.0)
    wsum = jnp.sum(weight, axis=0, keepdims=True)
    e_f = expert.astype(F32)
    e_a = jnp.min(jnp.where(chosen, e_f, float(N_EXPERTS)), axis=0, keepdims=True)
    e_b = jnp.max(jnp.where(chosen, e_f, -1.0), axis=0, keepdims=True)
    before = (lax.broadcasted_iota(jnp.int32, (tm, tm), 0) < lax.broadcasted_iota(jnp.int32, (tm, tm), 1))
    rank = _dot(picked.astype(BF16), jnp.where(before, 1.0, 0.0).astype(BF16)) + carry
    is_a = e_f == e_a
    is_b = e_f == e_b
    r_a = jnp.sum(jnp.where(is_a, rank, 0.0), axis=0, keepdims=True)
    r_b = jnp.sum(jnp.where(is_b, rank, 0.0), axis=0, keepdims=True)
    w_a = jnp.sum(jnp.where(is_a, weight, 0.0), axis=0, keepdims=True)
    w_b = jnp.sum(jnp.where(is_b, weight, 0.0), axis=0, keepdims=True)
    meta = jnp.concatenate([e_a, e_b, r_a, r_b, jnp.zeros((4, tm), F32)], axis=0).astype(jnp.int32)
    wcol = jnp.concatenate([w_a / wsum, w_b / wsum, jnp.zeros((LANES - 2, tm), F32)], axis=0).T
    return meta, wcol, carry + jnp.sum(picked, axis=1, keepdims=True)


def _norm_mod(x, g_ref, mod_refs):
    h = _rms_rows(x, g_ref[...])
    if mod_refs is not None:
        a_ref, s_ref = mod_refs
        h = h * (1.0 + a_ref[0]) + s_ref[0]
    return h


def _rownorm_kernel(x_ref, g_ref, a_ref, s_ref, h_ref):
    h_ref[...] = _norm_mod(x_ref[...], g_ref, (a_ref, s_ref)).astype(h_ref.dtype)


def _mod_spec(n_mod, rows, tm, d, n_prefetch=0):
    per = (rows // n_mod) // tm
    return pl.BlockSpec((1, 1, d), lambda i, *_: (i // per, 0, 0))


def _rownorm(x, g, scale, shift, *, tm=512):
    rows, d = x.shape
    row_spec = pl.BlockSpec((tm, d), lambda i: (i, 0))
    return pl.pallas_call(
        _rownorm_kernel, grid=(rows // tm,),
        in_specs=[row_spec, pl.BlockSpec((1, d), lambda i: (0, 0)), _mod_spec(scale.shape[0], rows, tm, d),
                  _mod_spec(shift.shape[0], rows, tm, d)],
        out_specs=row_spec, out_shape=jax.ShapeDtypeStruct((rows, d), BF16),
        compiler_params=_cparams("parallel"), name="rownorm",
    )(x, g.reshape(1, d), scale, shift)


def _mm_kernel(a_ref, w_ref, o_ref):
    o_ref[...] = _dot(a_ref[...].astype(BF16), w_ref[...]).astype(o_ref.dtype)


def _matmul(a, w, *, out_dtype=F32, tm=1024):
    m, k = a.shape
    n = w.shape[1]
    tn = next(t for t in (1024, 768, 512, LANES) if n % t == 0)
    return pl.pallas_call(
        _mm_kernel, grid=(m // tm, n // tn),
        in_specs=[pl.BlockSpec((tm, k), lambda i, j: (i, 0)), pl.BlockSpec((k, tn), lambda i, j: (0, j))],
        out_specs=pl.BlockSpec((tm, tn), lambda i, j: (i, j)),
        out_shape=jax.ShapeDtypeStruct((m, n), out_dtype),
        compiler_params=_cparams("parallel", "parallel"), name="matmul",
    )(a, w)


def _head_norm(x, g, n_heads, dv):
    outs = []
    for h in range(n_heads):
        xs = x[:, h * dv:(h + 1) * dv]
        ms = jnp.mean(xs * xs, axis=-1, keepdims=True)
        outs.append(xs * lax.rsqrt(ms + EPS) * g[:, h * dv:(h + 1) * dv])
    return jnp.concatenate(outs, axis=1)


def _proj_kernel(*refs, pre):
    refs = list(refs)
    if pre == "plain":
        a = refs.pop(0)[...]
    else:
        f_ref, b_ref, p_ref, hg_ref = refs.pop(0), refs.pop(0), refs.pop(0), refs.pop(0)
        hsum = f_ref[...] + b_ref[...]
        if pre == "mlstm":
            a = _sigmoid(p_ref[...]) * _head_norm(hsum, hg_ref[...], ML_HEADS, ML_DV)
        else:
            a = _head_norm(hsum, hg_ref[...], GLA_HEADS, GLA_DV) * _silu(p_ref[...])
        a = a.astype(BF16)
    w_ref, x_ref, gate_ref, g_ref, a_ref, s_ref, rwt_ref, rb_ref, count0_ref = refs[:9]
    xo_ref, h_ref, meta_ref, wcol_ref, count_ref, carry_ref = refs[9:]

    @pl.when(pl.program_id(0) == 0)
    def _():
        carry_ref[...] = count0_ref[...].astype(F32)

    x = x_ref[...] + gate_ref[0] * _dot(a, w_ref[...])
    xo_ref[...] = x
    h = _norm_mod(x, g_ref, (a_ref, s_ref))
    _rows_to_tiles(h_ref, h)
    meta, wcol, carry = _route(h, rwt_ref[...], rb_ref[...], carry_ref[:, 0:1])
    meta_ref[...] = meta
    wcol_ref[...] = wcol
    carry_ref[...] = jnp.broadcast_to(carry, carry_ref.shape)
    count_ref[...] = jnp.broadcast_to(carry, count_ref.shape).astype(jnp.int32)


def _proj(pre, pre_args, w_out, x, gate, g, scale, shift, rwt, rb, count0, *, tm=512):
    rows, d = x.shape
    k = w_out.shape[0]
    row_spec = pl.BlockSpec((tm, d), lambda i: (i, 0))
    if pre == "plain":
        args, specs = [pre_args[0]], [pl.BlockSpec((tm, k), lambda i: (i, 0))]
    else:
        hf, hb, p, col_block, hg = pre_args
        wide = pl.BlockSpec((tm, k), lambda i: (i, 0))
        args = [hf, hb, p, hg.reshape(1, k)]
        specs = [wide, wide, pl.BlockSpec((tm, k), lambda i: (i, col_block)), pl.BlockSpec((1, k), lambda i: (0, 0))]
    args += [w_out, x, gate, g.reshape(1, d), scale, shift, rwt, rb, count0]
    specs += [pl.BlockSpec((k, d), lambda i: (0, 0)), row_spec, _mod_spec(gate.shape[0], rows, tm, d),
              pl.BlockSpec((1, d), lambda i: (0, 0)), _mod_spec(scale.shape[0], rows, tm, d),
              _mod_spec(shift.shape[0], rows, tm, d), pl.BlockSpec(rwt.shape, lambda i: (0, 0)),
              pl.BlockSpec(rb.shape, lambda i: (0, 0)), pl.BlockSpec(count0.shape, lambda i: (0, 0))]
    return pl.pallas_call(
        functools.partial(_proj_kernel, pre=pre), grid=(rows // tm,), in_specs=specs,
        out_specs=[row_spec, pl.BlockSpec((tm * SUBLANES, LANES), lambda i: (i, 0)),
                   pl.BlockSpec((8, tm), lambda i: (0, i)),
                   pl.BlockSpec((tm, LANES), lambda i: (i, 0)), pl.BlockSpec((N_EXPERTS, LANES), lambda i: (0, 0))],
        out_shape=[jax.ShapeDtypeStruct((rows, d), F32), jax.ShapeDtypeStruct((rows * SUBLANES, LANES), F32),
                   jax.ShapeDtypeStruct((8, rows), jnp.int32), jax.ShapeDtypeStruct((rows, LANES), F32),
                   jax.ShapeDtypeStruct((N_EXPERTS, LANES), jnp.int32)],
        scratch_shapes=[pltpu.VMEM((N_EXPERTS, LANES), F32)],
        compiler_params=_cparams("arbitrary"), name="proj_" + pre,
    )(*args)


MOE_TILE = 512
MOE_TILE_SHIFT = 9
MOE_TOKENS = 256
ROW_UNROLL = 8


SUBLANES = 8


def _rows_to_tiles(ref, x, lead=()):
    rows = x.shape[0]
    for c in range(SUBLANES):
        ref[(*lead, pl.ds(c, rows, stride=SUBLANES), slice(None))] = x[:, c * LANES:(c + 1) * LANES]


def _tiles_to_rows(ref, rows, lead=()):
    return jnp.concatenate([ref[(*lead, pl.ds(c, rows, stride=SUBLANES), slice(None))] for c in range(SUBLANES)],
                           axis=1)


def _slot_tiles(rows):
    return (2 * rows) // MOE_TILE + N_EXPERTS


def _expert_offsets(cnt_ref, off_ref):
    def per_expert(e, k):
        off_ref[e] = k * MOE_TILE
        return k + ((cnt_ref[e] + MOE_TILE - 1) >> MOE_TILE_SHIFT)
    return lax.fori_loop(0, N_EXPERTS, per_expert, 0)


def _slots_kernel(cnt_ref, meta_ref, slot_ref, off_ref):
    @pl.when(pl.program_id(0) == 0)
    def _():
        _expert_offsets(cnt_ref, off_ref)

    e_a, e_b = meta_ref[0:1, :], meta_ref[1:2, :]
    off_a = jnp.zeros_like(e_a)
    off_b = jnp.zeros_like(e_b)
    for e in range(N_EXPERTS):
        off_a = jnp.where(e_a == e, off_ref[e], off_a)
        off_b = jnp.where(e_b == e, off_ref[e], off_b)
    slot_ref[...] = jnp.concatenate([off_a + meta_ref[2:3, :], off_b + meta_ref[3:4, :],
                                     jnp.zeros((6, e_a.shape[1]), jnp.int32)], axis=0)


def _slots(meta, counts, *, tm=1024):
    rows = meta.shape[1]
    grid_spec = pltpu.PrefetchScalarGridSpec(
        num_scalar_prefetch=1, grid=(rows // tm,),
        in_specs=[pl.BlockSpec((8, tm), lambda i, cnt: (0, i))],
        out_specs=pl.BlockSpec((8, tm), lambda i, cnt: (0, i)),
        scratch_shapes=[pltpu.SMEM((N_EXPERTS,), jnp.int32)])
    return pl.pallas_call(
        _slots_kernel, grid_spec=grid_spec, out_shape=jax.ShapeDtypeStruct((8, rows), jnp.int32),
        compiler_params=_cparams("arbitrary"), name="moe_slots",
    )(counts, meta)


def _dispatch_kernel(*refs, steps):
    sa_ref, sb_ref, cnt_ref = refs[:3]
    h_refs = refs[3:3 + len(steps)]
    xs_ref, info_ref, off_ref, zero_ref, sem = refs[3 + len(steps):]
    i = pl.program_id(0)
    tm = h_refs[0].shape[0] // SUBLANES
    n_tiles = info_ref.shape[0] - 1
    tile_rows = MOE_TILE * SUBLANES

    def tile_copy(tile):
        return pltpu.make_async_copy(zero_ref, xs_ref.at[pl.ds(tile * tile_rows, tile_rows), :], sem)

    @pl.when(i == 0)
    def _():
        zero_ref[...] = jnp.zeros_like(zero_ref)
        used = _expert_offsets(cnt_ref, off_ref)

        def per_expert(e, _):
            first = off_ref[e] >> MOE_TILE_SHIFT
            nt = (cnt_ref[e] + MOE_TILE - 1) >> MOE_TILE_SHIFT

            def fill(j, _):
                info_ref[first + j] = e
                return 0
            lax.fori_loop(0, nt, fill, 0)

            @pl.when(nt > 0)
            def _():
                tile_copy(first + nt - 1).start()
                tile_copy(first + nt - 1).wait()
            return 0
        lax.fori_loop(0, N_EXPERTS, per_expert, 0)
        info_ref[n_tiles] = used

        def tail(j, _):
            info_ref[j] = N_EXPERTS - 1
            tile_copy(j).start()
            tile_copy(j).wait()
            return 0
        lax.fori_loop(used, n_tiles, tail, 0)

    base = i * tm

    def copy_rows(h_ref):
        def row_copy(t, slot):
            dst = pl.multiple_of(slot * SUBLANES, SUBLANES)
            return pltpu.make_async_copy(h_ref.at[pl.ds(t * SUBLANES, SUBLANES), :],
                                         xs_ref.at[pl.ds(dst, SUBLANES), :], sem)

        for t in range(tm):
            row_copy(t, sa_ref[base + t]).start(priority=0)
            row_copy(t, sb_ref[base + t]).start(priority=1)
        for _ in range(2):
            pltpu.make_async_copy(h_ref, xs_ref.at[pl.ds(0, tm * SUBLANES), :], sem).wait()

    first = 0
    for h_ref, n in zip(h_refs, steps):
        pl.when(jnp.logical_and(i >= first, i < first + n))(functools.partial(copy_rows, h_ref))
        first += n


def _dispatch(hs, slots, counts):
    tm = MOE_TOKENS
    steps = tuple(h.shape[0] // (tm * SUBLANES) for h in hs)
    n_tiles = _slot_tiles(sum(steps) * tm)
    specs, first = [], 0
    for n in steps:
        specs.append(pl.BlockSpec((tm * SUBLANES, LANES),
                                  lambda i, *_, first=first, n=n: (jnp.clip(i - first, 0, n - 1), 0)))
        first += n
    grid_spec = pltpu.PrefetchScalarGridSpec(
        num_scalar_prefetch=3, grid=(sum(steps),), in_specs=specs,
        out_specs=[pl.BlockSpec(memory_space=pl.ANY), pl.BlockSpec(memory_space=pltpu.SMEM)],
        scratch_shapes=[pltpu.SMEM((N_EXPERTS,), jnp.int32), pltpu.VMEM((MOE_TILE * SUBLANES, LANES), F32),
                        pltpu.SemaphoreType.DMA(())])
    return pl.pallas_call(
        functools.partial(_dispatch_kernel, steps=steps), grid_spec=grid_spec,
        out_shape=[jax.ShapeDtypeStruct((n_tiles * MOE_TILE * SUBLANES, LANES), F32),
                   jax.ShapeDtypeStruct((n_tiles + 1,), jnp.int32)],
        compiler_params=_cparams("arbitrary"), name="moe_dispatch",
    )(slots[0], slots[1], counts, *hs)


def _ffn_kernel(info_ref, xs_ref, wg_ref, wu_ref, wd_ref, ys_ref, wg_s, wu_s, wd_s):
    i = pl.program_id(0)
    used = info_ref[info_ref.shape[0] - 1]
    fresh = jnp.logical_or(i == 0, info_ref[i] != info_ref[jnp.maximum(i - 1, 0)])

    @pl.when(jnp.logical_and(i < used, fresh))
    def _():
        wg_s[...] = wg_ref[0, 0].astype(BF16)
        wu_s[...] = wu_ref[0, 0].astype(BF16)
        wd_s[...] = wd_ref[0, 0].astype(BF16)

    @pl.when(i < used)
    def _():
        x = _tiles_to_rows(xs_ref, MOE_TILE).astype(BF16)
        hid = _silu(_dot(x, wg_s[...])) * _dot(x, wu_s[...])
        _rows_to_tiles(ys_ref, _dot(hid.astype(BF16), wd_s[...]))

    @pl.when(i >= used)
    def _():
        ys_ref[...] = jnp.zeros_like(ys_ref)


def _ffn(xs, info, wg, wu, wd, layer):
    tile_rows = MOE_TILE * SUBLANES
    n_tiles = xs.shape[0] // tile_rows
    d, f = wg.shape[2:]

    def w_map(i, info):
        return (layer, info[i], 0, 0)

    grid_spec = pltpu.PrefetchScalarGridSpec(
        num_scalar_prefetch=1, grid=(n_tiles,),
        in_specs=[pl.BlockSpec((tile_rows, LANES), lambda i, info: (jnp.minimum(i, info[n_tiles] - 1), 0)),
                  pl.BlockSpec((1, 1, d, f), w_map), pl.BlockSpec((1, 1, d, f), w_map),
                  pl.BlockSpec((1, 1, f, d), w_map)],
        out_specs=pl.BlockSpec((tile_rows, LANES), lambda i, info: (i, 0)),
        scratch_shapes=[pltpu.VMEM((d, f), BF16), pltpu.VMEM((d, f), BF16), pltpu.VMEM((f, d), BF16)])
    return pl.pallas_call(
        _ffn_kernel, grid_spec=grid_spec, out_shape=jax.ShapeDtypeStruct(xs.shape, F32),
        compiler_params=_cparams("arbitrary"), name="moe_ffn",
    )(info, xs, wg, wu, wd)


def _combine_kernel(*refs, has_mod, out_x):
    refs = list(refs)
    sa_ref, sb_ref, x_ref, ys_ref, wcol_ref, gate_ref, g_ref = refs[:7]
    refs = refs[7:]
    mod_refs = (refs.pop(0), refs.pop(0)) if has_mod else None
    xo_ref = refs.pop(0) if out_x else None
    h_ref, buf_a, buf_b, sems = refs
    i = pl.program_id(0)
    tm = x_ref.shape[0]

    def issue(tile, slot):
        base = tile * tm
        for t in range(tm):
            dst = pl.ds(t * SUBLANES, SUBLANES)
            src_a = pl.multiple_of(sa_ref[base + t] * SUBLANES, SUBLANES)
            src_b = pl.multiple_of(sb_ref[base + t] * SUBLANES, SUBLANES)
            pltpu.make_async_copy(ys_ref.at[pl.ds(src_a, SUBLANES), :], buf_a.at[slot, dst, :],
                                  sems.at[slot]).start(priority=0)
            pltpu.make_async_copy(ys_ref.at[pl.ds(src_b, SUBLANES), :], buf_b.at[slot, dst, :],
                                  sems.at[slot]).start(priority=1)

    @pl.when(i == 0)
    def _():
        issue(0, 0)

    @pl.when(i + 1 < pl.num_programs(0))
    def _():
        issue(i + 1, (i + 1) % 2)

    slot = i % 2
    for buf in (buf_a, buf_b):
        pltpu.make_async_copy(ys_ref.at[pl.ds(0, tm * SUBLANES), :], buf.at[slot], sems.at[slot]).wait()
    y = (wcol_ref[:, 0:1] * _tiles_to_rows(buf_a, tm, lead=(slot,))
         + wcol_ref[:, 1:2] * _tiles_to_rows(buf_b, tm, lead=(slot,)))
    x = x_ref[...] + gate_ref[0] * y
    if out_x:
        xo_ref[...] = x
    h_ref[...] = _norm_mod(x, g_ref, mod_refs).astype(h_ref.dtype)


def _combine(x, ys, slots, wcol, gate, g, *, mod=None, out_x=False, h_dtype=BF16):
    rows, d = x.shape
    tm = MOE_TOKENS
    row_spec = pl.BlockSpec((tm, d), lambda i, *_: (i, 0))
    args = [x, ys, wcol, gate, g.reshape(1, d)]
    specs = [row_spec, pl.BlockSpec(memory_space=pl.ANY), pl.BlockSpec((tm, LANES), lambda i, *_: (i, 0)),
             _mod_spec(gate.shape[0], rows, tm, d), pl.BlockSpec((1, d), lambda i, *_: (0, 0))]
    if mod is not None:
        for m in mod:
            args.append(m)
            specs.append(_mod_spec(m.shape[0], rows, tm, d))
    out_shape, out_specs = [], []
    if out_x:
        out_shape.append(jax.ShapeDtypeStruct((rows, d), F32))
        out_specs.append(row_spec)
    out_shape.append(jax.ShapeDtypeStruct((rows, d), h_dtype))
    out_specs.append(row_spec)
    grid_spec = pltpu.PrefetchScalarGridSpec(
        num_scalar_prefetch=2, grid=(rows // tm,), in_specs=specs, out_specs=out_specs,
        scratch_shapes=[pltpu.VMEM((2, tm * SUBLANES, LANES), F32), pltpu.VMEM((2, tm * SUBLANES, LANES), F32),
                        pltpu.SemaphoreType.DMA((2,))])
    outs = pl.pallas_call(
        functools.partial(_combine_kernel, has_mod=mod is not None, out_x=out_x), grid_spec=grid_spec,
        out_shape=out_shape, compiler_params=_cparams("arbitrary"), name="moe_combine",
    )(slots[0], slots[1], *args)
    return outs if out_x else outs[0]


def _softmax_av(scores, values, sink):
    rows, hd = scores[0].shape[0], values[0].shape[1]
    m = sink
    for s in scores:
        m = jnp.maximum(m, jnp.broadcast_to(jnp.max(s, axis=-1, keepdims=True), (rows, LANES)))
    tot = None
    for s, v in zip(scores, values):
        n = s.shape[1]
        p = jnp.exp(s - jnp.concatenate([m] * (n // LANES), axis=1)).astype(BF16)
        v_ext = jnp.concatenate([v, jnp.zeros((n, LANES - hd), BF16), jnp.ones((n, LANES), BF16)], axis=1)
        pv = _dot(p, v_ext)
        tot = pv if tot is None else tot + pv
    den = tot[:, LANES:] + jnp.exp(sink - m)
    return tot[:, :hd] / den[:, :hd]


def _sink_column(sink_ref, kv, rows):
    return jnp.concatenate([jnp.full((rows, LANES), sink_ref[kv * ATT_GROUP + g], F32) for g in range(ATT_GROUP)],
                           axis=0)


def _attn_ctx_kernel(sink_ref, qkv_ref, o_ref):
    t = qkv_ref.shape[0]
    qw = ATT_HEADS * HEAD_DIM
    kw = ATT_KV * HEAD_DIM
    heads_out = []
    for kv in range(ATT_KV):
        q = jnp.concatenate(
            [qkv_ref[:, (kv * ATT_GROUP + g) * HEAD_DIM:(kv * ATT_GROUP + g + 1) * HEAD_DIM] for g in range(ATT_GROUP)],
            axis=0).astype(BF16)
        k = qkv_ref[:, qw + kv * HEAD_DIM:qw + (kv + 1) * HEAD_DIM].astype(BF16)
        v = qkv_ref[:, qw + kw + kv * HEAD_DIM:qw + kw + (kv + 1) * HEAD_DIM].astype(BF16)
        s = _dot_nt(q, k) * HEAD_DIM ** -0.5
        o = _softmax_av([s], [v], _sink_column(sink_ref, kv, t))
        heads_out += [o[g * t:(g + 1) * t] for g in range(ATT_GROUP)]
    o_ref[...] = jnp.concatenate(heads_out, axis=1).astype(o_ref.dtype)


def _attn_ctx(qkv, sink, n_seq, seq_len):
    rows, cols = qkv.shape
    return pl.pallas_call(
        _attn_ctx_kernel, grid=(n_seq,),
        in_specs=[pl.BlockSpec(memory_space=pltpu.SMEM), pl.BlockSpec((seq_len, cols), lambda b: (b, 0))],
        out_specs=pl.BlockSpec((seq_len, ATT_HEADS * HEAD_DIM), lambda b: (b, 0)),
        out_shape=jax.ShapeDtypeStruct((rows, ATT_HEADS * HEAD_DIM), BF16),
        compiler_params=_cparams("parallel"), name="attn_context",
    )(sink, qkv)


def _rope_block(x, cos, sin_signed):
    lane = lax.broadcasted_iota(jnp.int32, x.shape, 1)
    nf = HEAD_DIM // 4
    partner = jnp.where((lane % (2 * nf)) < nf, pltpu.roll(x, LANES - nf, axis=1), pltpu.roll(x, nf, axis=1))
    return x * cos + partner * sin_signed


def _attn_lat_kernel(sink_ref, qkv_ref, ck_ref, cv_ref, cos_ref, sin_ref, o_ref, k_scr):
    i = pl.program_id(1)
    t = qkv_ref.shape[0]
    qw = ATT_HEADS * HEAD_DIM
    kw = ATT_KV * HEAD_DIM
    span = Q_BLOCK + 2 * WINDOW

    @pl.when(i == 0)
    def _():
        for c in range(kw // LANES):
            blk = qkv_ref[:, qw + c * LANES:qw + (c + 1) * LANES]
            k_scr[:, c * LANES:(c + 1) * LANES] = _rope_block(blk, cos_ref[...], sin_ref[...]).astype(BF16)

    r0 = pl.multiple_of(i * Q_BLOCK, Q_BLOCK)
    ws = pl.multiple_of(jnp.clip(r0 - WINDOW, 0, t - span), Q_BLOCK)
    cos_q = cos_ref[pl.ds(r0, Q_BLOCK), :]
    sin_q = sin_ref[pl.ds(r0, Q_BLOCK), :]
    qpos = r0 + lax.broadcasted_iota(jnp.int32, (Q_BLOCK, span), 0)
    kpos = ws + lax.broadcasted_iota(jnp.int32, (Q_BLOCK, span), 1)
    band = jnp.abs(qpos - kpos) <= WINDOW
    band = jnp.concatenate([band] * ATT_GROUP, axis=0)
    heads_out = []
    for kv in range(ATT_KV):
        heads = []
        for g in range(ATT_GROUP):
            h = kv * ATT_GROUP + g
            c, half = divmod(h * HEAD_DIM, LANES)
            blk = _rope_block(qkv_ref[pl.ds(r0, Q_BLOCK), c * LANES:(c + 1) * LANES], cos_q, sin_q)
            heads.append(blk[:, half:half + HEAD_DIM])
        q = jnp.concatenate(heads, axis=0).astype(BF16)
        ck = ck_ref[0, :, kv * HEAD_DIM:(kv + 1) * HEAD_DIM].astype(BF16)
        cv = cv_ref[0, :, kv * HEAD_DIM:(kv + 1) * HEAD_DIM].astype(BF16)
        kwin = k_scr[pl.ds(ws, span), kv * HEAD_DIM:(kv + 1) * HEAD_DIM]
        vwin = qkv_ref[pl.ds(ws, span), qw + kw + kv * HEAD_DIM:qw + kw + (kv + 1) * HEAD_DIM].astype(BF16)
        s_ctx = _dot_nt(q, ck) * HEAD_DIM ** -0.5
        s_win = jnp.where(band, _dot_nt(q, kwin) * HEAD_DIM ** -0.5, -jnp.inf)
        o = _softmax_av([s_ctx, s_win], [cv, vwin], _sink_column(sink_ref, kv, Q_BLOCK))
        heads_out += [o[g * Q_BLOCK:(g + 1) * Q_BLOCK] for g in range(ATT_GROUP)]
    o_ref[...] = jnp.concatenate(heads_out, axis=1).astype(o_ref.dtype)


def _rope_tables(seq_len):
    pos = jnp.arange(seq_len, dtype=jnp.int32)
    row = (pos // GRID_W).astype(F32)
    col = (pos % GRID_W).astype(F32)
    nf = HEAD_DIM // 4
    inv = ROPE_BASE ** (-jnp.arange(nf, dtype=F32) / nf)
    ang_r = row[:, None] * inv[None, :]
    ang_c = col[:, None] * inv[None, :]
    cos_h = jnp.concatenate([jnp.cos(ang_r), jnp.cos(ang_r), jnp.cos(ang_c), jnp.cos(ang_c)], axis=1)
    sin_h = jnp.concatenate([-jnp.sin(ang_r), jnp.sin(ang_r), -jnp.sin(ang_c), jnp.sin(ang_c)], axis=1)
    reps = LANES // HEAD_DIM
    return jnp.tile(cos_h, (1, reps)), jnp.tile(sin_h, (1, reps))


def _attn_lat(qkv, cache_k, cache_v, sink, n_seq, seq_len):
    rows, cols = qkv.shape
    past = cache_k.shape[1]
    kw = ATT_KV * HEAD_DIM
    cos, sin = _rope_tables(seq_len)
    return pl.pallas_call(
        _attn_lat_kernel, grid=(n_seq, seq_len // Q_BLOCK),
        in_specs=[pl.BlockSpec(memory_space=pltpu.SMEM),
                  pl.BlockSpec((seq_len, cols), lambda b, i: (b, 0)),
                  pl.BlockSpec((1, past, kw), lambda b, i: (b, 0, 0)),
                  pl.BlockSpec((1, past, kw), lambda b, i: (b, 0, 0)),
                  pl.BlockSpec((seq_len, LANES), lambda b, i: (0, 0)),
                  pl.BlockSpec((seq_len, LANES), lambda b, i: (0, 0))],
        out_specs=pl.BlockSpec((Q_BLOCK, ATT_HEADS * HEAD_DIM), lambda b, i: (b * (seq_len // Q_BLOCK) + i, 0)),
        out_shape=jax.ShapeDtypeStruct((rows, ATT_HEADS * HEAD_DIM), BF16),
        scratch_shapes=[pltpu.VMEM((seq_len, kw), BF16)],
        compiler_params=_cparams("parallel", "arbitrary"), name="attn_latent",
    )(sink, qkv, cache_k.reshape(n_seq, past, kw), cache_v.reshape(n_seq, past, kw), cos, sin)


def _split_bf16(w):
    hi = w.astype(BF16)
    return hi, (w - hi.astype(F32)).astype(BF16)


def _select_lanes(x, sel):
    x1 = x.astype(BF16)
    rest = x - x1.astype(F32)
    x2 = rest.astype(BF16)
    x3 = (rest - x2.astype(F32)).astype(BF16)
    return _dot(x1, sel) + _dot(x2, sel) + _dot(x3, sel)


def _ml_gates_kernel(h_ref, w_ref, wt_ref, b_ref, bt_ref, g_ref, gt_ref):
    h = h_ref[...]
    h_hi = h.astype(BF16)
    h_lo = (h - h_hi.astype(F32)).astype(BF16)
    ng = gt_ref.shape[0]
    by_hi = _dot(h_hi, w_ref[...])
    g_ref[...] = by_hi[:, :LANES] + by_hi[:, LANES:] + _dot(h_lo, w_ref[:, :LANES]) + b_ref[...]
    by_hi_t = _dot_nt(wt_ref[...], h_hi)
    gt_ref[...] = by_hi_t[:ng] + by_hi_t[ng:] + _dot_nt(wt_ref[:ng, :], h_lo) + bt_ref[...]


def _ml_gates(h, w_gates, b_gates, *, tm=512):
    rows, d = h.shape
    ng = w_gates.shape[1]
    w_hi, w_lo = _split_bf16(jnp.pad(w_gates, ((0, 0), (0, LANES - ng))))
    wt_hi, wt_lo = _split_bf16(w_gates.T)
    b_pad = jnp.pad(b_gates, (0, LANES - ng)).reshape(1, LANES)
    return pl.pallas_call(
        _ml_gates_kernel, grid=(rows // tm,),
        in_specs=[pl.BlockSpec((tm, d), lambda i: (i, 0)), pl.BlockSpec((d, 2 * LANES), lambda i: (0, 0)),
                  pl.BlockSpec((2 * ng, d), lambda i: (0, 0)), pl.BlockSpec((1, LANES), lambda i: (0, 0)),
                  pl.BlockSpec((ng, 1), lambda i: (0, 0))],
        out_specs=[pl.BlockSpec((tm, LANES), lambda i: (i, 0)), pl.BlockSpec((ng, tm), lambda i: (0, i))],
        out_shape=[jax.ShapeDtypeStruct((rows, LANES), F32), jax.ShapeDtypeStruct((ng, rows), F32)],
        compiler_params=_cparams("parallel"), name="mlstm_gates",
    )(h, jnp.concatenate([w_hi, w_lo], axis=1), jnp.concatenate([wt_hi, wt_lo], axis=0), b_pad,
      b_gates.reshape(ng, 1))


def _ml_qk_kernel(h_ref, w_ref, cw_ref, o_ref, *, seq_len, k_scale):
    j = pl.program_id(1)
    x = _dot(h_ref[...].astype(BF16), w_ref[...])
    t = x.shape[0]
    pos = lax.broadcasted_iota(jnp.int32, x.shape, 0) % seq_len
    prev = jnp.where(pos == 0, 0.0, pltpu.roll(x, 1, axis=0))
    nxt = jnp.where(pos == seq_len - 1, 0.0, pltpu.roll(x, t - 1, axis=0))
    y = prev * cw_ref[0:1, :] + x * cw_ref[1:2, :] + nxt * cw_ref[2:3, :]
    scale = jnp.where(j >= pl.num_programs(1) // 2, k_scale, 1.0).astype(F32)
    o_ref[...] = (_silu(y) * scale).astype(o_ref.dtype)


def _ml_qk(h, w_qk, conv_w, seq_len, *, tm=1024, tn=1024):
    rows, d = h.shape
    width = w_qk.shape[1]
    return pl.pallas_call(
        functools.partial(_ml_qk_kernel, seq_len=seq_len, k_scale=ML_DK ** -0.5), grid=(rows // tm, width // tn),
        in_specs=[pl.BlockSpec((tm, d), lambda i, j: (i, 0)), pl.BlockSpec((d, tn), lambda i, j: (0, j)),
                  pl.BlockSpec((3, tn), lambda i, j: (0, j))],
        out_specs=pl.BlockSpec((tm, tn), lambda i, j: (i, j)),
        out_shape=jax.ShapeDtypeStruct((rows, width), BF16),
        compiler_params=_cparams("parallel", "parallel"), name="mlstm_qk",
    )(h, w_qk, conv_w)


def _ml_scan_kernel(*refs, zero_init):
    refs = list(refs)
    dirs = [tuple(refs[0:5]), tuple(refs[5:10])]
    sel_ref = refs[10]
    refs = refs[11:]
    if not zero_init:
        c0_ref, n0_ref, m0_ref = refs[:3]
        refs = refs[3:]
    hf_ref, hb_ref, c_ref, n_ref, m_ref = refs
    h_out = (hf_ref, hb_ref)
    c = pl.program_id(1)
    last = pl.num_programs(1) - 1

    @pl.when(c == 0)
    def _():
        if zero_init:
            c_ref[...] = jnp.zeros_like(c_ref)
            n_ref[...] = jnp.zeros_like(n_ref)
            m_ref[...] = jnp.zeros_like(m_ref)
        else:
            c_ref[...] = c0_ref[...]
            n_ref[...] = n0_ref[...]
            m_ref[...] = m0_ref[...]

    length = hf_ref.shape[0]
    ti = lax.broadcasted_iota(jnp.int32, (length, length), 0)
    si = lax.broadcasted_iota(jnp.int32, (length, length), 1)
    for d in range(2):
        q_ref, k_ref, v_ref, g_ref, gt_ref = dirs[d]
        causal = (ti >= si) if d == 0 else (ti <= si)
        tri = jnp.where(causal, 1.0, 0.0).astype(F32)
        g_col = g_ref[...]
        f_row = _log_sigmoid(gt_ref[...])
        b_col = jnp.dot(tri, _log_sigmoid(g_col), precision=HI, preferred_element_type=F32)
        b_row = _dot_nt(f_row, tri, precision=HI)
        i_rep = _select_lanes(g_col, sel_ref[d, 0])
        b_rep = _select_lanes(b_col, sel_ref[d, 1])
        edge = length - 1 if d == 0 else 0
        for h in range(ML_HEADS):
            ji = d * 2 * ML_HEADS + h
            jf = ji + ML_HEADS
            bc = b_rep[:, h * LANES:(h + 1) * LANES]
            i_col = i_rep[:, h * LANES:(h + 1) * LANES]
            br = b_row[jf:jf + 1, :]
            i_row = gt_ref[ji:ji + 1, :]
            m_rep = m_ref[0, d, h]
            q = q_ref[:, h * ML_DK:(h + 1) * ML_DK]
            k = k_ref[:, h * ML_DK:(h + 1) * ML_DK]
            v = v_ref[:, h * ML_DV:(h + 1) * ML_DV].astype(BF16)
            cst = c_ref[0, d, h]
            nst = n_ref[0, d, h]
            a_row = i_row - br
            amat = jnp.where(causal, a_row, -jnp.inf)
            u = jnp.maximum(m_rep, jnp.broadcast_to(jnp.max(amat, axis=1, keepdims=True), (length, LANES)))
            qk = (_dot_nt(q, k) * jnp.exp(amat - u)).astype(BF16)
            sc = jnp.exp(m_rep - u)
            sc = jnp.concatenate([sc] * (ML_DV // LANES + 1), axis=1)
            state_ext = jnp.concatenate([cst, jnp.broadcast_to(nst, (LANES, ML_DK))], axis=0).astype(BF16)
            v_ext = jnp.concatenate([v, jnp.ones((length, LANES), BF16)], axis=1)
            tot = sc * _dot_nt(q, state_ext) + _dot(qk, v_ext)
            inv = 1.0 / jnp.maximum(jnp.abs(tot[:, ML_DV:]), jnp.exp(-(bc + u)))
            h_out[d][:, h * ML_DV:(h + 1) * ML_DV] = tot[:, :ML_DV] * jnp.concatenate([inv] * (ML_DV // LANES), axis=1)
            b_last = bc[edge:edge + 1, :]
            wlog_row = b_last + a_row
            m_new = jnp.maximum(b_last + m_rep, jnp.max(wlog_row, axis=1, keepdims=True))
            decay = jnp.exp(b_last + m_rep - m_new)
            ws_row = jnp.exp(wlog_row - m_new)
            ws_col = jnp.exp(b_last - bc + i_col - m_new)
            kw = (ws_col * k.astype(F32)).astype(BF16)
            c_ref[0, d, h] = decay * cst + _dot_tn(v, kw)
            n_ref[0, d, h] = decay * nst + _dot(jnp.broadcast_to(ws_row, (8, length)).astype(BF16), k)[0:1]
            m_ref[0, d, h] = m_new


def _ml_scan(qk, p, g, gt, state, n_seq, seq_len):
    rows = qk.shape[0]
    length = min(ML_CHUNK, seq_len)
    nc = seq_len // length
    qw = ML_HEADS * ML_DK
    vw = ML_HEADS * ML_DV
    ng = gt.shape[0]

    def fwd(b, c):
        return b * nc + c

    def bwd(b, c):
        return b * nc + nc - 1 - c

    args, specs = [], []
    for pos in (fwd, bwd):
        args += [qk, qk, p, g, gt]
        specs += [pl.BlockSpec((length, qw), lambda b, c, pos=pos: (pos(b, c), 0)),
                  pl.BlockSpec((length, qw), lambda b, c, pos=pos: (pos(b, c), 1)),
                  pl.BlockSpec((length, vw), lambda b, c, pos=pos: (pos(b, c), 0)),
                  pl.BlockSpec((length, LANES), lambda b, c, pos=pos: (pos(b, c), 0)),
                  pl.BlockSpec((ng, length), lambda b, c, pos=pos: (0, pos(b, c)))]
    assert length == LANES and ML_DK == LANES
    gate_lane = jnp.arange(LANES)[:, None]
    head = (jnp.arange(ML_HEADS * LANES) // LANES)[None, :]
    sel = jnp.stack([jnp.stack([gate_lane == (2 * d + kind) * ML_HEADS + head for kind in range(2)])
                     for d in range(2)]).astype(BF16)
    args.append(sel)
    specs.append(pl.BlockSpec(sel.shape, lambda b, c: (0, 0, 0, 0)))
    c_spec = pl.BlockSpec((1, 2, ML_HEADS, ML_DV, ML_DK), lambda b, c: (b, 0, 0, 0, 0))
    n_spec = pl.BlockSpec((1, 2, ML_HEADS, 1, ML_DK), lambda b, c: (b, 0, 0, 0, 0))
    zero_init = state is None
    if not zero_init:
        c0, n0, m0 = state
        args += [c0, n0.reshape(n_seq, 2, ML_HEADS, 1, ML_DK),
                 jnp.broadcast_to(m0[..., None, None], (n_seq, 2, ML_HEADS, 1, ML_DK))]
        specs += [c_spec, n_spec, n_spec]
    hf, hb, c_fin, n_fin, m_fin = pl.pallas_call(
        functools.partial(_ml_scan_kernel, zero_init=zero_init), grid=(n_seq, nc), in_specs=specs,
        out_specs=[pl.BlockSpec((length, vw), lambda b, c: (fwd(b, c), 0)),
                   pl.BlockSpec((length, vw), lambda b, c: (bwd(b, c), 0)), c_spec, n_spec, n_spec],
        out_shape=[jax.ShapeDtypeStruct((rows, vw), F32), jax.ShapeDtypeStruct((rows, vw), F32),
                   jax.ShapeDtypeStruct((n_seq, 2, ML_HEADS, ML_DV, ML_DK), F32),
                   jax.ShapeDtypeStruct((n_seq, 2, ML_HEADS, 1, ML_DK), F32),
                   jax.ShapeDtypeStruct((n_seq, 2, ML_HEADS, 1, ML_DK), F32)],
        compiler_params=_cparams("parallel", "arbitrary"), name="mlstm_scan",
    )(*args)
    return hf, hb, (c_fin, n_fin[:, :, :, 0, :], m_fin[:, :, :, 0, 0])


def _gla_scan_kernel(*refs, zero_init):
    refs = list(refs)
    dirs = [tuple(refs[0:4]), tuple(refs[4:8])]
    w2_ref, ba_ref = refs[8:10]
    refs = refs[10:]
    if not zero_init:
        s0_ref = refs.pop(0)
    of_ref, ob_ref, s_ref, st_scr, la_scr = refs
    o_out = (of_ref, ob_ref)
    c = pl.program_id(1)
    last = pl.num_programs(1) - 1
    kw = GLA_HEADS * GLA_DK
    n_sub = of_ref.shape[0] // GLA_SUB

    @pl.when(c == 0)
    def _():
        for d in range(2):
            for h in range(GLA_HEADS):
                st_scr[d, h] = jnp.zeros((GLA_DV, GLA_DK), F32) if zero_init else s0_ref[0, d, h].T

    for d in range(2):
        u = dirs[d][3][...].astype(BF16)
        z = _dot(u, w2_ref[:, d * kw:(d + 1) * kw]) + ba_ref[:, d * kw:(d + 1) * kw]
        la = _log_sigmoid(z) / GLA_TAU
        la_scr[d] = la
        totals = jnp.sum(la.reshape(n_sub, GLA_SUB, kw), axis=1)
        worst = jnp.min(totals) if d == 0 else jnp.minimum(worst, jnp.min(totals))
    decay_bounded = worst * LOG2E > -GLA_MAX_EXP2

    ti = lax.broadcasted_iota(jnp.int32, (GLA_SUB, GLA_SUB), 0)
    si = lax.broadcasted_iota(jnp.int32, (GLA_SUB, GLA_SUB), 1)
    s_lane = lax.broadcasted_iota(jnp.int32, (GLA_SUB, GLA_SUB), 1)

    def sub_chunk(j, carry, bounded):
        chains = [(d, h) for d in range(2) for h in range(GLA_HEADS)]
        rows = [pl.ds(pl.multiple_of((j if d == 0 else n_sub - 1 - j) * GLA_SUB, GLA_SUB), GLA_SUB)
                for d in range(2)]
        causal = [ti >= si, ti <= si]
        edge = [GLA_SUB - 1, 0]
        bc_all = [jnp.dot(jnp.where(causal[d], 1.0, 0.0).astype(F32), la_scr[d, rows[d], :], precision=HI,
                          preferred_element_type=F32) * LOG2E for d in range(2)]
        q, k, v, bc2, b_last, k_dec, q_dec, a = {}, {}, {}, {}, {}, {}, {}, {}
        for c in chains:
            d, h = c
            q_ref, k_ref, v_ref, _ = dirs[d]
            bc2[c] = bc_all[d][:, h * GLA_DK:(h + 1) * GLA_DK]
            q[c] = q_ref[rows[d], h * GLA_DK:(h + 1) * GLA_DK] * GLA_DK ** -0.5
            k[c] = k_ref[rows[d], h * GLA_DK:(h + 1) * GLA_DK]
            v[c] = v_ref[rows[d], h * GLA_DV:(h + 1) * GLA_DV].astype(BF16)
            b_last[c] = bc2[c][edge[d]:edge[d] + 1, :]
            k_dec[c] = (k[c] * jnp.exp2(b_last[c] - bc2[c])).astype(BF16)
            q_dec[c] = q[c] * jnp.exp2(bc2[c])
        for c in chains:
            if bounded:
                a[c] = _dot_nt((q_dec[c] * jnp.exp2(-b_last[c])).astype(BF16), k_dec[c])
            else:
                acc = jnp.zeros((GLA_SUB, GLA_SUB), F32)
                for s in range(GLA_SUB):
                    decay = jnp.exp2(bc2[c] - bc2[c][s:s + 1, :])
                    col = jnp.sum(q[c] * (k[c][s:s + 1, :] * decay), axis=1, keepdims=True)
                    acc = jnp.where(s_lane == s, col, acc)
                a[c] = acc
        inter = {c: _dot_nt(q_dec[c].astype(BF16), st_scr[c[0], c[1]].astype(BF16)) for c in chains}
        kv = {c: _dot_tn(v[c], k_dec[c]) for c in chains}
        intra = {c: _dot(jnp.where(causal[c[0]], a[c], 0.0).astype(BF16), v[c]) for c in chains}
        for c in chains:
            d, h = c
            o_out[d][rows[d], h * GLA_DV:(h + 1) * GLA_DV] = intra[c] + inter[c]
            st_scr[d, h] = jnp.exp2(b_last[c]) * st_scr[d, h] + kv[c]
        return carry

    @pl.when(decay_bounded)
    def _():
        lax.fori_loop(0, n_sub, functools.partial(sub_chunk, bounded=True), 0)

    @pl.when(jnp.logical_not(decay_bounded))
    def _():
        lax.fori_loop(0, n_sub, functools.partial(sub_chunk, bounded=False), 0)

    @pl.when(c == last)
    def _():
        for d in range(2):
            for h in range(GLA_HEADS):
                s_ref[0, d, h] = st_scr[d, h].T


def _gla_scan(p, u, w2, b_a, state, n_seq, seq_len):
    rows = p.shape[0]
    length = min(GLA_BLOCK, seq_len)
    nc = seq_len // length
    kw = GLA_HEADS * GLA_DK
    vw = GLA_HEADS * GLA_DV

    def fwd(b, c):
        return b * nc + c

    def bwd(b, c):
        return b * nc + nc - 1 - c

    args, specs = [], []
    for pos in (fwd, bwd):
        args += [p, p, p, u]
        specs += [pl.BlockSpec((length, kw), lambda b, c, pos=pos: (pos(b, c), 0)),
                  pl.BlockSpec((length, kw), lambda b, c, pos=pos: (pos(b, c), 1)),
                  pl.BlockSpec((length, vw), lambda b, c, pos=pos: (pos(b, c), 2 * kw // vw)),
                  pl.BlockSpec((length, LANES), lambda b, c, pos=pos: (pos(b, c), 0))]
    args += [w2, b_a]
    specs += [pl.BlockSpec(w2.shape, lambda b, c: (0, 0)), pl.BlockSpec(b_a.shape, lambda b, c: (0, 0))]
    s_spec = pl.BlockSpec((1, 2, GLA_HEADS, GLA_DK, GLA_DV), lambda b, c: (b, 0, 0, 0, 0))
    zero_init = state is None
    if not zero_init:
        args.append(state)
        specs.append(s_spec)
    return pl.pallas_call(
        functools.partial(_gla_scan_kernel, zero_init=zero_init), grid=(n_seq, nc), in_specs=specs,
        out_specs=[pl.BlockSpec((length, vw), lambda b, c: (fwd(b, c), 0)),
                   pl.BlockSpec((length, vw), lambda b, c: (bwd(b, c), 0)), s_spec],
        out_shape=[jax.ShapeDtypeStruct((rows, vw), F32), jax.ShapeDtypeStruct((rows, vw), F32),
                   jax.ShapeDtypeStruct((n_seq, 2, GLA_HEADS, GLA_DK, GLA_DV), F32)],
        scratch_shapes=[pltpu.VMEM((2, GLA_HEADS, GLA_DV, GLA_DK), F32), pltpu.VMEM((2, length, kw), F32)],
        compiler_params=_cparams("parallel", "arbitrary"), name="gla_scan",
    )(*args)


def kernel(x_prompt, x_sample, cache_k_0, cache_v_0, state_mlstm_C_1, state_mlstm_n_1, state_mlstm_m_1, state_gla_S_2, cache_k_3, cache_v_3, c, c_ctx, w_mod, b_mod, norm1_g, norm2_g, final_g, router_w, router_b, moe_wg, moe_wu, moe_wd, attn0_w_qkv, attn0_sink, attn0_w_o, mlstm1_w_in, mlstm1_b_gates, mlstm1_conv, mlstm1_norm_g, mlstm1_w_out, gla2_w_in, gla2_w_a1, gla2_w_a2, gla2_b_a, gla2_norm_g, gla2_w_out, attn3_w_qkv, attn3_sink, attn3_w_o):
    n_ctx, ctx_len, d = x_prompt.shape
    n_lat, lat_len, _ = x_sample.shape
    depth = w_mod.shape[0]

    cvec = jnp.concatenate([c_ctx[None, :], c, jnp.zeros((8 - 1 - n_lat, d), F32)], axis=0)
    mod = _modulation(cvec, w_mod, b_mod).reshape(depth, 8, 6, 1, d)

    def mods(layer, kind, latent):
        return mod[layer, 1:1 + n_lat, kind] if latent else mod[layer, 0:1, kind]

    rw_hi = router_w.T.astype(BF16)
    rw_lo = (router_w.T - rw_hi.astype(F32)).astype(BF16)
    rwt = jnp.concatenate([rw_hi, rw_lo], axis=0)
    rb = router_b.reshape(-1, 1)
    attn_w = {0: (attn0_w_qkv.astype(BF16), attn0_sink, attn0_w_o.astype(BF16), cache_k_0, cache_v_0),
              3: (attn3_w_qkv.astype(BF16), attn3_sink, attn3_w_o.astype(BF16), cache_k_3, cache_v_3)}
    ml_qw = ML_HEADS * ML_DK
    ml_vw = ML_HEADS * ML_DV
    ml_main = 2 * ml_qw + 2 * ml_vw
    ml_w_qk = mlstm1_w_in[:, :2 * ml_qw].astype(BF16)
    ml_w_vo = mlstm1_w_in[:, 2 * ml_qw:ml_main].astype(BF16)
    ml_w_gates = mlstm1_w_in[:, ml_main:]
    ml_w_out = mlstm1_w_out.astype(BF16)
    gla_kw = GLA_HEADS * GLA_DK
    gla_w_in = gla2_w_in.astype(BF16)
    gla_w_a1 = jnp.pad(jnp.concatenate([gla2_w_a1[0], gla2_w_a1[1]], axis=1),
                       ((0, 0), (0, LANES - 2 * GLA_RANK))).astype(BF16)
    gla_w2 = jnp.zeros((LANES, 2 * gla_kw), F32)
    gla_w2 = gla_w2.at[:GLA_RANK, :gla_kw].set(gla2_w_a2[0]).at[GLA_RANK:2 * GLA_RANK, gla_kw:].set(gla2_w_a2[1])
    gla_w2 = gla_w2.astype(BF16)
    gla_ba = gla2_b_a.reshape(1, 2 * gla_kw)
    gla_w_out = gla2_w_out.astype(BF16)

    new_state = []

    def mixer(layer, s, count0):
        latent, n_seq, seq_len, x, h = s["latent"], s["n_seq"], s["seq_len"], s["x"], s["h"]
        tail = (mods(layer, 2, latent), norm2_g[layer], mods(layer, 4, latent), mods(layer, 3, latent), rwt, rb,
                count0)
        kind = layer % 3
        if kind == 0:
            w_qkv, sink, w_o, ck, cv = attn_w[layer]
            qkv = _matmul(h, w_qkv)
            if latent:
                att = _attn_lat(qkv, ck, cv, sink, n_seq, seq_len)
            else:
                att = _attn_ctx(qkv, sink, n_seq, seq_len)
                qw = ATT_HEADS * HEAD_DIM
                kw = ATT_KV * HEAD_DIM
                new_state.append(qkv[:, qw:qw + kw].reshape(n_seq, seq_len, ATT_KV, HEAD_DIM))
                new_state.append(qkv[:, qw + kw:].reshape(n_seq, seq_len, ATT_KV, HEAD_DIM))
            return _proj("plain", (att,), w_o, x, *tail)
        if kind == 1:
            p = _matmul(h, ml_w_vo)
            g, gt = _ml_gates(h, ml_w_gates, mlstm1_b_gates)
            qk = _ml_qk(h, ml_w_qk, mlstm1_conv, seq_len)
            st = (state_mlstm_C_1, state_mlstm_n_1, state_mlstm_m_1) if latent else None
            hf, hb, fin = _ml_scan(qk, p, g, gt, st, n_seq, seq_len)
            if not latent:
                new_state.extend(fin)
            return _proj("mlstm", (hf, hb, p, 1, mlstm1_norm_g), ml_w_out, x, *tail)
        p = _matmul(h, gla_w_in)
        u = _matmul(h, gla_w_a1)
        of, ob, s_fin = _gla_scan(p, u, gla_w2, gla_ba, state_gla_S_2 if latent else None, n_seq, seq_len)
        if not latent:
            new_state.append(s_fin)
        gla_vw = GLA_HEADS * GLA_DV
        return _proj("gla", (of, ob, p, (2 * gla_kw + gla_vw) // gla_vw, gla2_norm_g), gla_w_out, x, *tail)

    streams = [dict(latent=False, n_seq=n_ctx, seq_len=ctx_len, x=x_prompt.reshape(n_ctx * ctx_len, d)),
               dict(latent=True, n_seq=n_lat, seq_len=lat_len, x=x_sample.reshape(n_lat * lat_len, d))]
    for s in streams:
        s["h"] = _rownorm(s["x"], norm1_g[0], mods(0, 1, s["latent"]), mods(0, 0, s["latent"]))
    for layer in range(depth):
        counts = jnp.zeros((N_EXPERTS, LANES), jnp.int32)
        for s in streams:
            s["x"], s["h2"], s["meta"], s["wcol"], counts = mixer(layer, s, counts)
        cnt = counts[:, 0]
        for s in streams:
            s["slots"] = _slots(s["meta"], cnt)
        xs, info = _dispatch([s["h2"] for s in streams], jnp.concatenate([s["slots"] for s in streams], axis=1), cnt)
        ys = _ffn(xs, info, moe_wg, moe_wu, moe_wd, layer)
        for s in streams:
            latent = s["latent"]
            gate2 = mods(layer, 5, latent)
            if layer + 1 < depth:
                s["x"], s["h"] = _combine(s["x"], ys, s["slots"], s["wcol"], gate2, norm1_g[layer + 1],
                                          mod=(mods(layer + 1, 1, latent), mods(layer + 1, 0, latent)), out_x=True,
                                          h_dtype=F32 if (layer + 1) % 3 == 1 else BF16)
            else:
                s["out"] = _combine(s["x"], ys, s["slots"], s["wcol"], gate2, final_g, h_dtype=F32)
    y_prompt = streams[0]["out"].reshape(n_ctx, ctx_len, d)
    y_sample = streams[1]["out"].reshape(n_lat, lat_len, d)
    return (y_prompt, y_sample, *new_state)
```

```python
import functools

import jax
import jax.numpy as jnp
from jax import lax
from jax.experimental import pallas as pl
from jax.experimental.pallas import tpu as pltpu

F32 = jnp.float32
BF16 = jnp.bfloat16
HI = lax.Precision.HIGHEST

EPS = 1e-6
LOG2E = 1.4426950408889634
GRID_W = 64
ATT_HEADS = 16
ATT_KV = 4
ATT_GROUP = ATT_HEADS // ATT_KV
HEAD_DIM = 64
WINDOW = 128
Q_BLOCK = 128
ROPE_BASE = 10000.0
ML_HEADS = 8
ML_DK = 128
ML_DV = 256
ML_CHUNK = 128
GLA_HEADS = 4
GLA_DK = 128
GLA_DV = 256
GLA_RANK = 16
GLA_TAU = 16.0
GLA_SUB = 16
GLA_BLOCK = 256
GLA_MAX_EXP2 = 80.0
N_EXPERTS = 16
N_GROUPS = 4
GROUP_SIZE = N_EXPERTS // N_GROUPS
LANES = 128
VMEM_LIMIT = 56 * 1024 * 1024


def _cparams(*sem):
    return pltpu.CompilerParams(dimension_semantics=sem, vmem_limit_bytes=VMEM_LIMIT)


def _dot(a, b):
    return jnp.dot(a, b, preferred_element_type=F32)


def _dot_nt(a, b, precision=None):
    return lax.dot_general(a, b, (((1,), (1,)), ((), ())), precision=precision, preferred_element_type=F32)


def _dot_tn(a, b):
    return lax.dot_general(a, b, (((0,), (0,)), ((), ())), preferred_element_type=F32)


def _sigmoid(x):
    return 1.0 / (1.0 + jnp.exp(-x))


def _silu(x):
    return x * _sigmoid(x)


def _log_sigmoid(x):
    return jnp.minimum(x, 0.0) - jnp.log(1.0 + jnp.exp(-jnp.abs(x)))


def _rms_rows(x, g):
    ms = jnp.mean(x * x, axis=-1, keepdims=True)
    return x * lax.rsqrt(ms + EPS) * g


def _mod_kernel(c_ref, w_ref, b_ref, o_ref):
    s = _silu(c_ref[...])
    o_ref[0] = _dot(s.astype(BF16), w_ref[0].astype(BF16)) + b_ref[0]


def _modulation(cvec, w_mod, b_mod):
    depth, d, n6 = w_mod.shape
    tn = 1536
    return pl.pallas_call(
        _mod_kernel,
        grid=(depth, n6 // tn),
        in_specs=[pl.BlockSpec((8, d), lambda l, j: (0, 0)),
                  pl.BlockSpec((1, d, tn), lambda l, j: (l, 0, j)),
                  pl.BlockSpec((1, 1, tn), lambda l, j: (l, 0, j))],
        out_specs=pl.BlockSpec((1, 8, tn), lambda l, j: (l, 0, j)),
        out_shape=jax.ShapeDtypeStruct((depth, 8, n6), F32),
        compiler_params=_cparams("parallel", "parallel"),
        name="adaln_modulation",
    )(cvec, w_mod, b_mod.reshape(depth, 1, n6))


def _route(h, rwt, rb, carry):
    tm = h.shape[0]
    h_hi = h.astype(BF16)
    h_lo = (h - h_hi.astype(F32)).astype(BF16)
    by_hi = _dot_nt(rwt, h_hi)
    logits = by_hi[:N_EXPERTS] + by_hi[N_EXPERTS:] + _dot_nt(rwt[:N_EXPERTS], h_lo)
    scores = _sigmoid(logits)
    sel = scores + rb
    expert = lax.broadcasted_iota(jnp.int32, sel.shape, 0)
    pos = expert % GROUP_SIZE
    grp = expert // GROUP_SIZE

    def mate(x, k):
        ahead = pltpu.roll(x, N_EXPERTS - k, axis=0)
        behind = pltpu.roll(x, GROUP_SIZE - k, axis=0)
        return jnp.where(pos + k < GROUP_SIZE, ahead, behind)

    beaten = jnp.zeros_like(sel)
    for k in range(1, GROUP_SIZE):
        other = mate(sel, k)
        other_first = (pos + k) % GROUP_SIZE < pos
        beaten = beaten + jnp.where(other_first, jnp.where(other >= sel, 1.0, 0.0), jnp.where(other > sel, 1.0, 0.0))
    top2 = jnp.where(beaten < 2.0, sel, 0.0)
    gscore = top2
    for k in range(1, GROUP_SIZE):
        gscore = gscore + mate(top2, k)
    lost = jnp.zeros_like(sel)
    for k in range(1, N_GROUPS):
        other = pltpu.roll(gscore, N_EXPERTS - GROUP_SIZE * k, axis=0)
        other_first = (grp + k) % N_GROUPS < grp
        lost = lost + jnp.where(other_first, jnp.where(other >= gscore, 1.0, 0.0),
                                jnp.where(other > gscore, 1.0, 0.0))
    picked = jnp.where(lost < 0.5, jnp.where(beaten < 2.0, 1.0, 0.0), 0.0)
    chosen = picked > 0.5
    weight = jnp.where(chosen, scores, 0.0)
    wsum = jnp.sum(weight, axis=0, keepdims=True)
    e_f = expert.astype(F32)
    e_a = jnp.min(jnp.where(chosen, e_f, float(N_EXPERTS)), axis=0, keepdims=True)
    e_b = jnp.max(jnp.where(chosen, e_f, -1.0), axis=0, keepdims=True)
    before = (lax.broadcasted_iota(jnp.int32, (tm, tm), 0) < lax.broadcasted_iota(jnp.int32, (tm, tm), 1))
    rank = _dot(picked.astype(BF16), jnp.where(before, 1.0, 0.0).astype(BF16)) + carry
    is_a = e_f == e_a
    is_b = e_f == e_b
    r_a = jnp.sum(jnp.where(is_a, rank, 0.0), axis=0, keepdims=True)
    r_b = jnp.sum(jnp.where(is_b, rank, 0.0), axis=0, keepdims=True)
    w_a = jnp.sum(jnp.where(is_a, weight, 0.0), axis=0, keepdims=True)
    w_b = jnp.sum(jnp.where(is_b, weight, 0.0), axis=0, keepdims=True)
    meta = jnp.concatenate([e_a, e_b, r_a, r_b, jnp.zeros((4, tm), F32)], axis=0).astype(jnp.int32)
    wcol = jnp.concatenate([w_a / wsum, w_b / wsum, jnp.zeros((LANES - 2, tm), F32)], axis=0).T
    return meta, wcol, carry + jnp.sum(picked, axis=1, keepdims=True)


def _norm_mod(x, g_ref, mod_refs):
    h = _rms_rows(x, g_ref[...])
    if mod_refs is not None:
        a_ref, s_ref = mod_refs
        h = h * (1.0 + a_ref[0]) + s_ref[0]
    return h


def _rownorm_kernel(x_ref, g_ref, a_ref, s_ref, h_ref):
    h_ref[...] = _norm_mod(x_ref[...], g_ref, (a_ref, s_ref)).astype(h_ref.dtype)


def _mod_spec(n_mod, rows, tm, d, n_prefetch=0):
    per = (rows // n_mod) // tm
    return pl.BlockSpec((1, 1, d), lambda i, *_: (i // per, 0, 0))


def _rownorm(x, g, scale, shift, *, tm=512):
    rows, d = x.shape
    row_spec = pl.BlockSpec((tm, d), lambda i: (i, 0))
    return pl.pallas_call(
        _rownorm_kernel, grid=(rows // tm,),
        in_specs=[row_spec, pl.BlockSpec((1, d), lambda i: (0, 0)), _mod_spec(scale.shape[0], rows, tm, d),
                  _mod_spec(shift.shape[0], rows, tm, d)],
        out_specs=row_spec, out_shape=jax.ShapeDtypeStruct((rows, d), BF16),
        compiler_params=_cparams("parallel"), name="rownorm",
    )(x, g.reshape(1, d), scale, shift)


def _mm_kernel(a_ref, w_ref, o_ref):
    o_ref[...] = _dot(a_ref[...].astype(BF16), w_ref[...]).astype(o_ref.dtype)


def _matmul(a, w, *, out_dtype=F32, tm=1024):
    m, k = a.shape
    n = w.shape[1]
    tn = next(t for t in (1024, 768, 512, LANES) if n % t == 0)
    return pl.pallas_call(
        _mm_kernel, grid=(m // tm, n // tn),
        in_specs=[pl.BlockSpec((tm, k), lambda i, j: (i, 0)), pl.BlockSpec((k, tn), lambda i, j: (0, j))],
        out_specs=pl.BlockSpec((tm, tn), lambda i, j: (i, j)),
        out_shape=jax.ShapeDtypeStruct((m, n), out_dtype),
        compiler_params=_cparams("parallel", "parallel"), name="matmul",
    )(a, w)


def _head_norm(x, g, n_heads, dv):
    outs = []
    for h in range(n_heads):
        xs = x[:, h * dv:(h + 1) * dv]
        ms = jnp.mean(xs * xs, axis=-1, keepdims=True)
        outs.append(xs * lax.rsqrt(ms + EPS) * g[:, h * dv:(h + 1) * dv])
    return jnp.concatenate(outs, axis=1)


def _proj_kernel(*refs, pre):
    refs = list(refs)
    if pre == "plain":
        a = refs.pop(0)[...]
    else:
        f_ref, b_ref, p_ref, hg_ref = refs.pop(0), refs.pop(0), refs.pop(0), refs.pop(0)
        hsum = f_ref[...] + b_ref[...]
        if pre == "mlstm":
            a = _sigmoid(p_ref[...]) * _head_norm(hsum, hg_ref[...], ML_HEADS, ML_DV)
        else:
            a = _head_norm(hsum, hg_ref[...], GLA_HEADS, GLA_DV) * _silu(p_ref[...])
        a = a.astype(BF16)
    w_ref, x_ref, gate_ref, g_ref, a_ref, s_ref, rwt_ref, rb_ref, count0_ref = refs[:9]
    xo_ref, h_ref, meta_ref, wcol_ref, count_ref, carry_ref = refs[9:]

    @pl.when(pl.program_id(0) == 0)
    def _():
        carry_ref[...] = count0_ref[...].astype(F32)

    x = x_ref[...] + gate_ref[0] * _dot(a, w_ref[...])
    xo_ref[...] = x
    h = _norm_mod(x, g_ref, (a_ref, s_ref))
    _rows_to_tiles(h_ref, h)
    meta, wcol, carry = _route(h, rwt_ref[...], rb_ref[...], carry_ref[:, 0:1])
    meta_ref[...] = meta
    wcol_ref[...] = wcol
    carry_ref[...] = jnp.broadcast_to(carry, carry_ref.shape)
    count_ref[...] = jnp.broadcast_to(carry, count_ref.shape).astype(jnp.int32)


def _proj(pre, pre_args, w_out, x, gate, g, scale, shift, rwt, rb, count0, *, tm=512):
    rows, d = x.shape
    k = w_out.shape[0]
    row_spec = pl.BlockSpec((tm, d), lambda i: (i, 0))
    if pre == "plain":
        args, specs = [pre_args[0]], [pl.BlockSpec((tm, k), lambda i: (i, 0))]
    else:
        hf, hb, p, col_block, hg = pre_args
        wide = pl.BlockSpec((tm, k), lambda i: (i, 0))
        args = [hf, hb, p, hg.reshape(1, k)]
        specs = [wide, wide, pl.BlockSpec((tm, k), lambda i: (i, col_block)), pl.BlockSpec((1, k), lambda i: (0, 0))]
    args += [w_out, x, gate, g.reshape(1, d), scale, shift, rwt, rb, count0]
    specs += [pl.BlockSpec((k, d), lambda i: (0, 0)), row_spec, _mod_spec(gate.shape[0], rows, tm, d),
              pl.BlockSpec((1, d), lambda i: (0, 0)), _mod_spec(scale.shape[0], rows, tm, d),
              _mod_spec(shift.shape[0], rows, tm, d), pl.BlockSpec(rwt.shape, lambda i: (0, 0)),
              pl.BlockSpec(rb.shape, lambda i: (0, 0)), pl.BlockSpec(count0.shape, lambda i: (0, 0))]
    return pl.pallas_call(
        functools.partial(_proj_kernel, pre=pre), grid=(rows // tm,), in_specs=specs,
        out_specs=[row_spec, pl.BlockSpec((tm * SUBLANES, LANES), lambda i: (i, 0)),
                   pl.BlockSpec((8, tm), lambda i: (0, i)),
                   pl.BlockSpec((tm, LANES), lambda i: (i, 0)), pl.BlockSpec((N_EXPERTS, LANES), lambda i: (0, 0))],
        out_shape=[jax.ShapeDtypeStruct((rows, d), F32), jax.ShapeDtypeStruct((rows * SUBLANES, LANES), F32),
                   jax.ShapeDtypeStruct((8, rows), jnp.int32), jax.ShapeDtypeStruct((rows, LANES), F32),
                   jax.ShapeDtypeStruct((N_EXPERTS, LANES), jnp.int32)],
        scratch_shapes=[pltpu.VMEM((N_EXPERTS, LANES), F32)],
        compiler_params=_cparams("arbitrary"), name="proj_" + pre,
    )(*args)


MOE_TILE = 512
MOE_TILE_SHIFT = 9
MOE_TOKENS = 256
ROW_UNROLL = 8


SUBLANES = 8


def _rows_to_tiles(ref, x, lead=()):
    rows = x.shape[0]
    for c in range(SUBLANES):
        ref[(*lead, pl.ds(c, rows, stride=SUBLANES), slice(None))] = x[:, c * LANES:(c + 1) * LANES]


def _tiles_to_rows(ref, rows, lead=()):
    return jnp.concatenate([ref[(*lead, pl.ds(c, rows, stride=SUBLANES), slice(None))] for c in range(SUBLANES)],
                           axis=1)


def _slot_tiles(rows):
    return (2 * rows) // MOE_TILE + N_EXPERTS


def _expert_offsets(cnt_ref, off_ref):
    def per_expert(e, k):
        off_ref[e] = k * MOE_TILE
        return k + ((cnt_ref[e] + MOE_TILE - 1) >> MOE_TILE_SHIFT)
    return lax.fori_loop(0, N_EXPERTS, per_expert, 0)


def _slots_kernel(cnt_ref, meta_ref, slot_ref, off_ref):
    @pl.when(pl.program_id(0) == 0)
    def _():
        _expert_offsets(cnt_ref, off_ref)

    e_a, e_b = meta_ref[0:1, :], meta_ref[1:2, :]
    off_a = jnp.zeros_like(e_a)
    off_b = jnp.zeros_like(e_b)
    for e in range(N_EXPERTS):
        off_a = jnp.where(e_a == e, off_ref[e], off_a)
        off_b = jnp.where(e_b == e, off_ref[e], off_b)
    slot_ref[...] = jnp.concatenate([off_a + meta_ref[2:3, :], off_b + meta_ref[3:4, :],
                                     jnp.zeros((6, e_a.shape[1]), jnp.int32)], axis=0)


def _slots(meta, counts, *, tm=1024):
    rows = meta.shape[1]
    grid_spec = pltpu.PrefetchScalarGridSpec(
        num_scalar_prefetch=1, grid=(rows // tm,),
        in_specs=[pl.BlockSpec((8, tm), lambda i, cnt: (0, i))],
        out_specs=pl.BlockSpec((8, tm), lambda i, cnt: (0, i)),
        scratch_shapes=[pltpu.SMEM((N_EXPERTS,), jnp.int32)])
    return pl.pallas_call(
        _slots_kernel, grid_spec=grid_spec, out_shape=jax.ShapeDtypeStruct((8, rows), jnp.int32),
        compiler_params=_cparams("arbitrary"), name="moe_slots",
    )(counts, meta)


def _dispatch_kernel(*refs, steps):
    sa_ref, sb_ref, cnt_ref = refs[:3]
    h_refs = refs[3:3 + len(steps)]
    xs_ref, info_ref, off_ref, zero_ref, sem = refs[3 + len(steps):]
    i = pl.program_id(0)
    tm = h_refs[0].shape[0] // SUBLANES
    n_tiles = info_ref.shape[0] - 1
    tile_rows = MOE_TILE * SUBLANES

    def tile_copy(tile):
        return pltpu.make_async_copy(zero_ref, xs_ref.at[pl.ds(tile * tile_rows, tile_rows), :], sem)

    @pl.when(i == 0)
    def _():
        zero_ref[...] = jnp.zeros_like(zero_ref)
        used = _expert_offsets(cnt_ref, off_ref)

        def per_expert(e, _):
            first = off_ref[e] >> MOE_TILE_SHIFT
            nt = (cnt_ref[e] + MOE_TILE - 1) >> MOE_TILE_SHIFT

            def fill(j, _):
                info_ref[first + j] = e
                return 0
            lax.fori_loop(0, nt, fill, 0)

            @pl.when(nt > 0)
            def _():
                tile_copy(first + nt - 1).start()
                tile_copy(first + nt - 1).wait()
            return 0
        lax.fori_loop(0, N_EXPERTS, per_expert, 0)
        info_ref[n_tiles] = used

        def tail(j, _):
            info_ref[j] = N_EXPERTS - 1
            tile_copy(j).start()
            tile_copy(j).wait()
            return 0
        lax.fori_loop(used, n_tiles, tail, 0)

    base = i * tm

    def copy_rows(h_ref):
        def row_copy(t, slot):
            dst = pl.multiple_of(slot * SUBLANES, SUBLANES)
            return pltpu.make_async_copy(h_ref.at[pl.ds(t * SUBLANES, SUBLANES), :],
                                         xs_ref.at[pl.ds(dst, SUBLANES), :], sem)

        for t in range(tm):
            row_copy(t, sa_ref[base + t]).start(priority=0)
            row_copy(t, sb_ref[base + t]).start(priority=1)
        for _ in range(2):
            pltpu.make_async_copy(h_ref, xs_ref.at[pl.ds(0, tm * SUBLANES), :], sem).wait()

    first = 0
    for h_ref, n in zip(h_refs, steps):
        pl.when(jnp.logical_and(i >= first, i < first + n))(functools.partial(copy_rows, h_ref))
        first += n


def _dispatch(hs, slots, counts):
    tm = MOE_TOKENS
    steps = tuple(h.shape[0] // (tm * SUBLANES) for h in hs)
    n_tiles = _slot_tiles(sum(steps) * tm)
    specs, first = [], 0
    for n in steps:
        specs.append(pl.BlockSpec((tm * SUBLANES, LANES),
                                  lambda i, *_, first=first, n=n: (jnp.clip(i - first, 0, n - 1), 0)))
        first += n
    grid_spec = pltpu.PrefetchScalarGridSpec(
        num_scalar_prefetch=3, grid=(sum(steps),), in_specs=specs,
        out_specs=[pl.BlockSpec(memory_space=pl.ANY), pl.BlockSpec(memory_space=pltpu.SMEM)],
        scratch_shapes=[pltpu.SMEM((N_EXPERTS,), jnp.int32), pltpu.VMEM((MOE_TILE * SUBLANES, LANES), F32),
                        pltpu.SemaphoreType.DMA(())])
    return pl.pallas_call(
        functools.partial(_dispatch_kernel, steps=steps), grid_spec=grid_spec,
        out_shape=[jax.ShapeDtypeStruct((n_tiles * MOE_TILE * SUBLANES, LANES), F32),
                   jax.ShapeDtypeStruct((n_tiles + 1,), jnp.int32)],
        compiler_params=_cparams("arbitrary"), name="moe_dispatch",
    )(slots[0], slots[1], counts, *hs)


def _ffn_kernel(info_ref, xs_ref, wg_ref, wu_ref, wd_ref, ys_ref, wg_s, wu_s, wd_s):
    i = pl.program_id(0)
    used = info_ref[info_ref.shape[0] - 1]
    fresh = jnp.logical_or(i == 0, info_ref[i] != info_ref[jnp.maximum(i - 1, 0)])

    @pl.when(jnp.logical_and(i < used, fresh))
    def _():
        wg_s[...] = wg_ref[0, 0].astype(BF16)
        wu_s[...] = wu_ref[0, 0].astype(BF16)
        wd_s[...] = wd_ref[0, 0].astype(BF16)

    @pl.when(i < used)
    def _():
        x = _tiles_to_rows(xs_ref, MOE_TILE).astype(BF16)
        hid = _silu(_dot(x, wg_s[...])) * _dot(x, wu_s[...])
        _rows_to_tiles(ys_ref, _dot(hid.astype(BF16), wd_s[...]))

    @pl.when(i >= used)
    def _():
        ys_ref[...] = jnp.zeros_like(ys_ref)


def _ffn(xs, info, wg, wu, wd, layer):
    tile_rows = MOE_TILE * SUBLANES
    n_tiles = xs.shape[0] // tile_rows
    d, f = wg.shape[2:]

    def w_map(i, info):
        return (layer, info[i], 0, 0)

    grid_spec = pltpu.PrefetchScalarGridSpec(
        num_scalar_prefetch=1, grid=(n_tiles,),
        in_specs=[pl.BlockSpec((tile_rows, LANES), lambda i, info: (jnp.minimum(i, info[n_tiles] - 1), 0)),
                  pl.BlockSpec((1, 1, d, f), w_map), pl.BlockSpec((1, 1, d, f), w_map),
                  pl.BlockSpec((1, 1, f, d), w_map)],
        out_specs=pl.BlockSpec((tile_rows, LANES), lambda i, info: (i, 0)),
        scratch_shapes=[pltpu.VMEM((d, f), BF16), pltpu.VMEM((d, f), BF16), pltpu.VMEM((f, d), BF16)])
    return pl.pallas_call(
        _ffn_kernel, grid_spec=grid_spec, out_shape=jax.ShapeDtypeStruct(xs.shape, F32),
        compiler_params=_cparams("arbitrary"), name="moe_ffn",
    )(info, xs, wg, wu, wd)


def _combine_kernel(*refs, has_mod, out_x):
    refs = list(refs)
    sa_ref, sb_ref, x_ref, ys_ref, wcol_ref, gate_ref, g_ref = refs[:7]
    refs = refs[7:]
    mod_refs = (refs.pop(0), refs.pop(0)) if has_mod else None
    xo_ref = refs.pop(0) if out_x else None
    h_ref, buf_a, buf_b, sems = refs
    i = pl.program_id(0)
    tm = x_ref.shape[0]

    def issue(tile, slot):
        base = tile * tm
        for t in range(tm):
            dst = pl.ds(t * SUBLANES, SUBLANES)
            src_a = pl.multiple_of(sa_ref[base + t] * SUBLANES, SUBLANES)
            src_b = pl.multiple_of(sb_ref[base + t] * SUBLANES, SUBLANES)
            pltpu.make_async_copy(ys_ref.at[pl.ds(src_a, SUBLANES), :], buf_a.at[slot, dst, :],
                                  sems.at[slot]).start(priority=0)
            pltpu.make_async_copy(ys_ref.at[pl.ds(src_b, SUBLANES), :], buf_b.at[slot, dst, :],
                                  sems.at[slot]).start(priority=1)

    @pl.when(i == 0)
    def _():
        issue(0, 0)

    @pl.when(i + 1 < pl.num_programs(0))
    def _():
        issue(i + 1, (i + 1) % 2)

    slot = i % 2
    for buf in (buf_a, buf_b):
        pltpu.make_async_copy(ys_ref.at[pl.ds(0, tm * SUBLANES), :], buf.at[slot], sems.at[slot]).wait()
    y = (wcol_ref[:, 0:1] * _tiles_to_rows(buf_a, tm, lead=(slot,))
         + wcol_ref[:, 1:2] * _tiles_to_rows(buf_b, tm, lead=(slot,)))
    x = x_ref[...] + gate_ref[0] * y
    if out_x:
        xo_ref[...] = x
    h_ref[...] = _norm_mod(x, g_ref, mod_refs).astype(h_ref.dtype)


def _combine(x, ys, slots, wcol, gate, g, *, mod=None, out_x=False, h_dtype=BF16):
    rows, d = x.shape
    tm = MOE_TOKENS
    row_spec = pl.BlockSpec((tm, d), lambda i, *_: (i, 0))
    args = [x, ys, wcol, gate, g.reshape(1, d)]
    specs = [row_spec, pl.BlockSpec(memory_space=pl.ANY), pl.BlockSpec((tm, LANES), lambda i, *_: (i, 0)),
             _mod_spec(gate.shape[0], rows, tm, d), pl.BlockSpec((1, d), lambda i, *_: (0, 0))]
    if mod is not None:
        for m in mod:
            args.append(m)
            specs.append(_mod_spec(m.shape[0], rows, tm, d))
    out_shape, out_specs = [], []
    if out_x:
        out_shape.append(jax.ShapeDtypeStruct((rows, d), F32))
        out_specs.append(row_spec)
    out_shape.append(jax.ShapeDtypeStruct((rows, d), h_dtype))
    out_specs.append(row_spec)
    grid_spec = pltpu.PrefetchScalarGridSpec(
        num_scalar_prefetch=2, grid=(rows // tm,), in_specs=specs, out_specs=out_specs,
        scratch_shapes=[pltpu.VMEM((2, tm * SUBLANES, LANES), F32), pltpu.VMEM((2, tm * SUBLANES, LANES), F32),
                        pltpu.SemaphoreType.DMA((2,))])
    outs = pl.pallas_call(
        functools.partial(_combine_kernel, has_mod=mod is not None, out_x=out_x), grid_spec=grid_spec,
        out_shape=out_shape, compiler_params=_cparams("arbitrary"), name="moe_combine",
    )(slots[0], slots[1], *args)
    return outs if out_x else outs[0]


def _softmax_av(groups):
    maxes = []
    for scores, _, sink in groups:
        m = sink
        for s in scores:
            m = jnp.maximum(m, jnp.broadcast_to(jnp.max(s, axis=-1, keepdims=True), sink.shape))
        maxes.append(m)
    probs = [[jnp.exp(s - jnp.concatenate([m] * (s.shape[1] // LANES), axis=1)).astype(BF16) for s in scores]
             for (scores, _, _), m in zip(groups, maxes)]
    outs = []
    for (_, values, sink), m, ps in zip(groups, maxes, probs):
        hd = values[0].shape[1]
        tot = None
        for p, v in zip(ps, values):
            n = v.shape[0]
            v_ext = jnp.concatenate([v, jnp.zeros((n, LANES - hd), BF16), jnp.ones((n, LANES), BF16)], axis=1)
            pv = _dot(p, v_ext)
            tot = pv if tot is None else tot + pv
        outs.append((tot, sink, m, hd))
    return [tot[:, :hd] / (tot[:, LANES:] + jnp.exp(sink - m))[:, :hd] for tot, sink, m, hd in outs]


def _sink_column(sink_ref, kv, rows):
    return jnp.concatenate([jnp.full((rows, LANES), sink_ref[kv * ATT_GROUP + g], F32) for g in range(ATT_GROUP)],
                           axis=0)


def _attn_ctx_kernel(sink_ref, qkv_ref, o_ref):
    t = qkv_ref.shape[0]
    qw = ATT_HEADS * HEAD_DIM
    kw = ATT_KV * HEAD_DIM
    groups = []
    for kv in range(ATT_KV):
        q = jnp.concatenate(
            [qkv_ref[:, (kv * ATT_GROUP + g) * HEAD_DIM:(kv * ATT_GROUP + g + 1) * HEAD_DIM] for g in range(ATT_GROUP)],
            axis=0).astype(BF16)
        k = qkv_ref[:, qw + kv * HEAD_DIM:qw + (kv + 1) * HEAD_DIM].astype(BF16)
        v = qkv_ref[:, qw + kw + kv * HEAD_DIM:qw + kw + (kv + 1) * HEAD_DIM].astype(BF16)
        groups.append(([_dot_nt(q, k) * HEAD_DIM ** -0.5], [v], _sink_column(sink_ref, kv, t)))
    heads_out = [o[g * t:(g + 1) * t] for o in _softmax_av(groups) for g in range(ATT_GROUP)]
    o_ref[...] = jnp.concatenate(heads_out, axis=1).astype(o_ref.dtype)


def _attn_ctx(qkv, sink, n_seq, seq_len):
    rows, cols = qkv.shape
    return pl.pallas_call(
        _attn_ctx_kernel, grid=(n_seq,),
        in_specs=[pl.BlockSpec(memory_space=pltpu.SMEM), pl.BlockSpec((seq_len, cols), lambda b: (b, 0))],
        out_specs=pl.BlockSpec((seq_len, ATT_HEADS * HEAD_DIM), lambda b: (b, 0)),
        out_shape=jax.ShapeDtypeStruct((rows, ATT_HEADS * HEAD_DIM), BF16),
        compiler_params=_cparams("parallel"), name="attn_context",
    )(sink, qkv)


def _rope_block(x, cos, sin_signed):
    lane = lax.broadcasted_iota(jnp.int32, x.shape, 1)
    nf = HEAD_DIM // 4
    partner = jnp.where((lane % (2 * nf)) < nf, pltpu.roll(x, LANES - nf, axis=1), pltpu.roll(x, nf, axis=1))
    return x * cos + partner * sin_signed


def _attn_lat_kernel(sink_ref, qkv_ref, ck_ref, cv_ref, cos_ref, sin_ref, o_ref, k_scr):
    i = pl.program_id(1)
    t = qkv_ref.shape[0]
    qw = ATT_HEADS * HEAD_DIM
    kw = ATT_KV * HEAD_DIM
    span = Q_BLOCK + 2 * WINDOW

    @pl.when(i == 0)
    def _():
        for c in range(kw // LANES):
            blk = qkv_ref[:, qw + c * LANES:qw + (c + 1) * LANES]
            k_scr[:, c * LANES:(c + 1) * LANES] = _rope_block(blk, cos_ref[...], sin_ref[...]).astype(BF16)

    r0 = pl.multiple_of(i * Q_BLOCK, Q_BLOCK)
    ws = pl.multiple_of(jnp.clip(r0 - WINDOW, 0, t - span), Q_BLOCK)
    cos_q = cos_ref[pl.ds(r0, Q_BLOCK), :]
    sin_q = sin_ref[pl.ds(r0, Q_BLOCK), :]
    qpos = r0 + lax.broadcasted_iota(jnp.int32, (Q_BLOCK, span), 0)
    kpos = ws + lax.broadcasted_iota(jnp.int32, (Q_BLOCK, span), 1)
    band = jnp.abs(qpos - kpos) <= WINDOW
    band = jnp.concatenate([band] * ATT_GROUP, axis=0)
    groups = []
    for kv in range(ATT_KV):
        heads = []
        for g in range(ATT_GROUP):
            h = kv * ATT_GROUP + g
            c, half = divmod(h * HEAD_DIM, LANES)
            blk = _rope_block(qkv_ref[pl.ds(r0, Q_BLOCK), c * LANES:(c + 1) * LANES], cos_q, sin_q)
            heads.append(blk[:, half:half + HEAD_DIM])
        q = jnp.concatenate(heads, axis=0).astype(BF16)
        ck = ck_ref[0, :, kv * HEAD_DIM:(kv + 1) * HEAD_DIM].astype(BF16)
        cv = cv_ref[0, :, kv * HEAD_DIM:(kv + 1) * HEAD_DIM].astype(BF16)
        kwin = k_scr[pl.ds(ws, span), kv * HEAD_DIM:(kv + 1) * HEAD_DIM]
        vwin = qkv_ref[pl.ds(ws, span), qw + kw + kv * HEAD_DIM:qw + kw + (kv + 1) * HEAD_DIM].astype(BF16)
        s_ctx = _dot_nt(q, ck) * HEAD_DIM ** -0.5
        s_win = jnp.where(band, _dot_nt(q, kwin) * HEAD_DIM ** -0.5, -jnp.inf)
        groups.append(([s_ctx, s_win], [cv, vwin], _sink_column(sink_ref, kv, Q_BLOCK)))
    heads_out = [o[g * Q_BLOCK:(g + 1) * Q_BLOCK] for o in _softmax_av(groups) for g in range(ATT_GROUP)]
    o_ref[...] = jnp.concatenate(heads_out, axis=1).astype(o_ref.dtype)


def _rope_tables(seq_len):
    pos = jnp.arange(seq_len, dtype=jnp.int32)
    row = (pos // GRID_W).astype(F32)
    col = (pos % GRID_W).astype(F32)
    nf = HEAD_DIM // 4
    inv = ROPE_BASE ** (-jnp.arange(nf, dtype=F32) / nf)
    ang_r = row[:, None] * inv[None, :]
    ang_c = col[:, None] * inv[None, :]
    cos_h = jnp.concatenate([jnp.cos(ang_r), jnp.cos(ang_r), jnp.cos(ang_c), jnp.cos(ang_c)], axis=1)
    sin_h = jnp.concatenate([-jnp.sin(ang_r), jnp.sin(ang_r), -jnp.sin(ang_c), jnp.sin(ang_c)], axis=1)
    reps = LANES // HEAD_DIM
    return jnp.tile(cos_h, (1, reps)), jnp.tile(sin_h, (1, reps))


def _attn_lat(qkv, cache_k, cache_v, sink, n_seq, seq_len):
    rows, cols = qkv.shape
    past = cache_k.shape[1]
    kw = ATT_KV * HEAD_DIM
    cos, sin = _rope_tables(seq_len)
    return pl.pallas_call(
        _attn_lat_kernel, grid=(n_seq, seq_len // Q_BLOCK),
        in_specs=[pl.BlockSpec(memory_space=pltpu.SMEM),
                  pl.BlockSpec((seq_len, cols), lambda b, i: (b, 0)),
                  pl.BlockSpec((1, past, kw), lambda b, i: (b, 0, 0)),
                  pl.BlockSpec((1, past, kw), lambda b, i: (b, 0, 0)),
                  pl.BlockSpec((seq_len, LANES), lambda b, i: (0, 0)),
                  pl.BlockSpec((seq_len, LANES), lambda b, i: (0, 0))],
        out_specs=pl.BlockSpec((Q_BLOCK, ATT_HEADS * HEAD_DIM), lambda b, i: (b * (seq_len // Q_BLOCK) + i, 0)),
        out_shape=jax.ShapeDtypeStruct((rows, ATT_HEADS * HEAD_DIM), BF16),
        scratch_shapes=[pltpu.VMEM((seq_len, kw), BF16)],
        compiler_params=_cparams("parallel", "arbitrary"), name="attn_latent",
    )(sink, qkv, cache_k.reshape(n_seq, past, kw), cache_v.reshape(n_seq, past, kw), cos, sin)


def _split_bf16(w):
    hi = w.astype(BF16)
    return hi, (w - hi.astype(F32)).astype(BF16)


def _select_lanes(x, sel):
    x1 = x.astype(BF16)
    rest = x - x1.astype(F32)
    x2 = rest.astype(BF16)
    x3 = (rest - x2.astype(F32)).astype(BF16)
    return _dot(x1, sel) + _dot(x2, sel) + _dot(x3, sel)


def _ml_gates_kernel(h_ref, w_ref, wt_ref, b_ref, bt_ref, g_ref, gt_ref):
    h = h_ref[...]
    h_hi = h.astype(BF16)
    h_lo = (h - h_hi.astype(F32)).astype(BF16)
    ng = gt_ref.shape[0]
    by_hi = _dot(h_hi, w_ref[...])
    g_ref[...] = by_hi[:, :LANES] + by_hi[:, LANES:] + _dot(h_lo, w_ref[:, :LANES]) + b_ref[...]
    by_hi_t = _dot_nt(wt_ref[...], h_hi)
    gt_ref[...] = by_hi_t[:ng] + by_hi_t[ng:] + _dot_nt(wt_ref[:ng, :], h_lo) + bt_ref[...]


def _ml_gates(h, w_gates, b_gates, *, tm=512):
    rows, d = h.shape
    ng = w_gates.shape[1]
    w_hi, w_lo = _split_bf16(jnp.pad(w_gates, ((0, 0), (0, LANES - ng))))
    wt_hi, wt_lo = _split_bf16(w_gates.T)
    b_pad = jnp.pad(b_gates, (0, LANES - ng)).reshape(1, LANES)
    return pl.pallas_call(
        _ml_gates_kernel, grid=(rows // tm,),
        in_specs=[pl.BlockSpec((tm, d), lambda i: (i, 0)), pl.BlockSpec((d, 2 * LANES), lambda i: (0, 0)),
                  pl.BlockSpec((2 * ng, d), lambda i: (0, 0)), pl.BlockSpec((1, LANES), lambda i: (0, 0)),
                  pl.BlockSpec((ng, 1), lambda i: (0, 0))],
        out_specs=[pl.BlockSpec((tm, LANES), lambda i: (i, 0)), pl.BlockSpec((ng, tm), lambda i: (0, i))],
        out_shape=[jax.ShapeDtypeStruct((rows, LANES), F32), jax.ShapeDtypeStruct((ng, rows), F32)],
        compiler_params=_cparams("parallel"), name="mlstm_gates",
    )(h, jnp.concatenate([w_hi, w_lo], axis=1), jnp.concatenate([wt_hi, wt_lo], axis=0), b_pad,
      b_gates.reshape(ng, 1))


def _ml_qk_kernel(h_ref, w_ref, cw_ref, o_ref, *, seq_len, k_scale):
    j = pl.program_id(1)
    x = _dot(h_ref[...].astype(BF16), w_ref[...])
    t = x.shape[0]
    pos = lax.broadcasted_iota(jnp.int32, x.shape, 0) % seq_len
    prev = jnp.where(pos == 0, 0.0, pltpu.roll(x, 1, axis=0))
    nxt = jnp.where(pos == seq_len - 1, 0.0, pltpu.roll(x, t - 1, axis=0))
    y = prev * cw_ref[0:1, :] + x * cw_ref[1:2, :] + nxt * cw_ref[2:3, :]
    scale = jnp.where(j >= pl.num_programs(1) // 2, k_scale, 1.0).astype(F32)
    o_ref[...] = (_silu(y) * scale).astype(o_ref.dtype)


def _ml_qk(h, w_qk, conv_w, seq_len, *, tm=1024, tn=1024):
    rows, d = h.shape
    width = w_qk.shape[1]
    return pl.pallas_call(
        functools.partial(_ml_qk_kernel, seq_len=seq_len, k_scale=ML_DK ** -0.5), grid=(rows // tm, width // tn),
        in_specs=[pl.BlockSpec((tm, d), lambda i, j: (i, 0)), pl.BlockSpec((d, tn), lambda i, j: (0, j)),
                  pl.BlockSpec((3, tn), lambda i, j: (0, j))],
        out_specs=pl.BlockSpec((tm, tn), lambda i, j: (i, j)),
        out_shape=jax.ShapeDtypeStruct((rows, width), BF16),
        compiler_params=_cparams("parallel", "parallel"), name="mlstm_qk",
    )(h, w_qk, conv_w)


def _ml_scan_kernel(*refs, zero_init):
    refs = list(refs)
    dirs = [tuple(refs[0:5]), tuple(refs[5:10])]
    sel_ref = refs[10]
    refs = refs[11:]
    if not zero_init:
        c0_ref, n0_ref, m0_ref = refs[:3]
        refs = refs[3:]
    hf_ref, hb_ref, c_ref, n_ref, m_ref = refs
    h_out = (hf_ref, hb_ref)
    c = pl.program_id(1)
    last = pl.num_programs(1) - 1

    @pl.when(c == 0)
    def _():
        if zero_init:
            c_ref[...] = jnp.zeros_like(c_ref)
            n_ref[...] = jnp.zeros_like(n_ref)
            m_ref[...] = jnp.zeros_like(m_ref)
        else:
            c_ref[...] = c0_ref[...]
            n_ref[...] = n0_ref[...]
            m_ref[...] = m0_ref[...]

    length = hf_ref.shape[0]
    ti = lax.broadcasted_iota(jnp.int32, (length, length), 0)
    si = lax.broadcasted_iota(jnp.int32, (length, length), 1)
    for d in range(2):
        q_ref, k_ref, v_ref, g_ref, gt_ref = dirs[d]
        causal = (ti >= si) if d == 0 else (ti <= si)
        tri = jnp.where(causal, 1.0, 0.0).astype(F32)
        g_col = g_ref[...]
        f_row = _log_sigmoid(gt_ref[...])
        b_col = jnp.dot(tri, _log_sigmoid(g_col), precision=HI, preferred_element_type=F32)
        b_row = _dot_nt(f_row, tri, precision=HI)
        i_rep = _select_lanes(g_col, sel_ref[d, 0])
        b_rep = _select_lanes(b_col, sel_ref[d, 1])
        edge = length - 1 if d == 0 else 0
        heads = range(ML_HEADS)
        q = [q_ref[:, h * ML_DK:(h + 1) * ML_DK] for h in heads]
        k = [k_ref[:, h * ML_DK:(h + 1) * ML_DK] for h in heads]
        v = [v_ref[:, h * ML_DV:(h + 1) * ML_DV].astype(BF16) for h in heads]
        cst = [c_ref[0, d, h] for h in heads]
        nst = [n_ref[0, d, h] for h in heads]
        m_rep = [m_ref[0, d, h] for h in heads]
        bc = [b_rep[:, h * LANES:(h + 1) * LANES] for h in heads]
        i_col = [i_rep[:, h * LANES:(h + 1) * LANES] for h in heads]
        qk_raw = [_dot_nt(q[h], k[h]) for h in heads]
        inter = [_dot_nt(q[h], jnp.concatenate([cst[h], jnp.broadcast_to(nst[h], (LANES, ML_DK))],
                                               axis=0).astype(BF16)) for h in heads]
        ji = d * 2 * ML_HEADS
        a_row = [gt_ref[ji + h:ji + h + 1, :] - b_row[ji + ML_HEADS + h:ji + ML_HEADS + h + 1, :] for h in heads]
        amat = [jnp.where(causal, a_row[h], -jnp.inf) for h in heads]
        u = [jnp.maximum(m_rep[h], jnp.broadcast_to(jnp.max(amat[h], axis=1, keepdims=True), (length, LANES)))
             for h in heads]
        qk = [(qk_raw[h] * jnp.exp(amat[h] - u[h])).astype(BF16) for h in heads]
        intra = [_dot(qk[h], jnp.concatenate([v[h], jnp.ones((length, LANES), BF16)], axis=1)) for h in heads]
        b_last = [bc[h][edge:edge + 1, :] for h in heads]
        wlog_row = [b_last[h] + a_row[h] for h in heads]
        m_new = [jnp.maximum(b_last[h] + m_rep[h], jnp.max(wlog_row[h], axis=1, keepdims=True)) for h in heads]
        ws_row = [jnp.exp(wlog_row[h] - m_new[h]) for h in heads]
        kw = [(jnp.exp(b_last[h] - bc[h] + i_col[h] - m_new[h]) * k[h].astype(F32)).astype(BF16) for h in heads]
        c_upd = [_dot_tn(v[h], kw[h]) for h in heads]
        n_upd = [_dot(jnp.broadcast_to(ws_row[h], (8, length)).astype(BF16), k[h])[0:1] for h in heads]
        for h in heads:
            sc = jnp.exp(m_rep[h] - u[h])
            tot = jnp.concatenate([sc] * (ML_DV // LANES + 1), axis=1) * inter[h] + intra[h]
            inv = 1.0 / jnp.maximum(jnp.abs(tot[:, ML_DV:]), jnp.exp(-(bc[h] + u[h])))
            h_out[d][:, h * ML_DV:(h + 1) * ML_DV] = tot[:, :ML_DV] * jnp.concatenate([inv] * (ML_DV // LANES), axis=1)
            decay = jnp.exp(b_last[h] + m_rep[h] - m_new[h])
            c_ref[0, d, h] = decay * cst[h] + c_upd[h]
            n_ref[0, d, h] = decay * nst[h] + n_upd[h]
            m_ref[0, d, h] = m_new[h]


def _ml_scan(qk, p, g, gt, state, n_seq, seq_len):
    rows = qk.shape[0]
    length = min(ML_CHUNK, seq_len)
    nc = seq_len // length
    qw = ML_HEADS * ML_DK
    vw = ML_HEADS * ML_DV
    ng = gt.shape[0]

    def fwd(b, c):
        return b * nc + c

    def bwd(b, c):
        return b * nc + nc - 1 - c

    args, specs = [], []
    for pos in (fwd, bwd):
        args += [qk, qk, p, g, gt]
        specs += [pl.BlockSpec((length, qw), lambda b, c, pos=pos: (pos(b, c), 0)),
                  pl.BlockSpec((length, qw), lambda b, c, pos=pos: (pos(b, c), 1)),
                  pl.BlockSpec((length, vw), lambda b, c, pos=pos: (pos(b, c), 0)),
                  pl.BlockSpec((length, LANES), lambda b, c, pos=pos: (pos(b, c), 0)),
                  pl.BlockSpec((ng, length), lambda b, c, pos=pos: (0, pos(b, c)))]
    assert length == LANES and ML_DK == LANES
    gate_lane = jnp.arange(LANES)[:, None]
    head = (jnp.arange(ML_HEADS * LANES) // LANES)[None, :]
    sel = jnp.stack([jnp.stack([gate_lane == (2 * d + kind) * ML_HEADS + head for kind in range(2)])
                     for d in range(2)]).astype(BF16)
    args.append(sel)
    specs.append(pl.BlockSpec(sel.shape, lambda b, c: (0, 0, 0, 0)))
    c_spec = pl.BlockSpec((1, 2, ML_HEADS, ML_DV, ML_DK), lambda b, c: (b, 0, 0, 0, 0))
    n_spec = pl.BlockSpec((1, 2, ML_HEADS, 1, ML_DK), lambda b, c: (b, 0, 0, 0, 0))
    zero_init = state is None
    if not zero_init:
        c0, n0, m0 = state
        args += [c0, n0.reshape(n_seq, 2, ML_HEADS, 1, ML_DK),
                 jnp.broadcast_to(m0[..., None, None], (n_seq, 2, ML_HEADS, 1, ML_DK))]
        specs += [c_spec, n_spec, n_spec]
    hf, hb, c_fin, n_fin, m_fin = pl.pallas_call(
        functools.partial(_ml_scan_kernel, zero_init=zero_init), grid=(n_seq, nc), in_specs=specs,
        out_specs=[pl.BlockSpec((length, vw), lambda b, c: (fwd(b, c), 0)),
                   pl.BlockSpec((length, vw), lambda b, c: (bwd(b, c), 0)), c_spec, n_spec, n_spec],
        out_shape=[jax.ShapeDtypeStruct((rows, vw), F32), jax.ShapeDtypeStruct((rows, vw), F32),
                   jax.ShapeDtypeStruct((n_seq, 2, ML_HEADS, ML_DV, ML_DK), F32),
                   jax.ShapeDtypeStruct((n_seq, 2, ML_HEADS, 1, ML_DK), F32),
                   jax.ShapeDtypeStruct((n_seq, 2, ML_HEADS, 1, ML_DK), F32)],
        compiler_params=_cparams("parallel", "arbitrary"), name="mlstm_scan",
    )(*args)
    return hf, hb, (c_fin, n_fin[:, :, :, 0, :], m_fin[:, :, :, 0, 0])


def _gla_scan_kernel(*refs, zero_init):
    refs = list(refs)
    dirs = [tuple(refs[0:4]), tuple(refs[4:8])]
    w2_ref, ba_ref = refs[8:10]
    refs = refs[10:]
    if not zero_init:
        s0_ref = refs.pop(0)
    of_ref, ob_ref, s_ref, st_scr, la_scr = refs
    o_out = (of_ref, ob_ref)
    c = pl.program_id(1)
    last = pl.num_programs(1) - 1
    kw = GLA_HEADS * GLA_DK
    n_sub = of_ref.shape[0] // GLA_SUB

    @pl.when(c == 0)
    def _():
        for d in range(2):
            for h in range(GLA_HEADS):
                st_scr[d, h] = jnp.zeros((GLA_DV, GLA_DK), F32) if zero_init else s0_ref[0, d, h].T

    for d in range(2):
        u = dirs[d][3][...].astype(BF16)
        z = _dot(u, w2_ref[:, d * kw:(d + 1) * kw]) + ba_ref[:, d * kw:(d + 1) * kw]
        la = _log_sigmoid(z) / GLA_TAU
        la_scr[d] = la
        totals = jnp.sum(la.reshape(n_sub, GLA_SUB, kw), axis=1)
        worst = jnp.min(totals) if d == 0 else jnp.minimum(worst, jnp.min(totals))
    decay_bounded = worst * LOG2E > -GLA_MAX_EXP2

    ti = lax.broadcasted_iota(jnp.int32, (GLA_SUB, GLA_SUB), 0)
    si = lax.broadcasted_iota(jnp.int32, (GLA_SUB, GLA_SUB), 1)
    s_lane = lax.broadcasted_iota(jnp.int32, (GLA_SUB, GLA_SUB), 1)

    def sub_chunk(j, carry, bounded):
        chains = [(d, h) for d in range(2) for h in range(GLA_HEADS)]
        rows = [pl.ds(pl.multiple_of((j if d == 0 else n_sub - 1 - j) * GLA_SUB, GLA_SUB), GLA_SUB)
                for d in range(2)]
        causal = [ti >= si, ti <= si]
        edge = [GLA_SUB - 1, 0]
        bc_all = [jnp.dot(jnp.where(causal[d], 1.0, 0.0).astype(F32), la_scr[d, rows[d], :], precision=HI,
                          preferred_element_type=F32) * LOG2E for d in range(2)]
        q, k, v, bc2, b_last, k_dec, q_dec, a = {}, {}, {}, {}, {}, {}, {}, {}
        for c in chains:
            d, h = c
            q_ref, k_ref, v_ref, _ = dirs[d]
            bc2[c] = bc_all[d][:, h * GLA_DK:(h + 1) * GLA_DK]
            q[c] = q_ref[rows[d], h * GLA_DK:(h + 1) * GLA_DK] * GLA_DK ** -0.5
            k[c] = k_ref[rows[d], h * GLA_DK:(h + 1) * GLA_DK]
            v[c] = v_ref[rows[d], h * GLA_DV:(h + 1) * GLA_DV].astype(BF16)
            b_last[c] = bc2[c][edge[d]:edge[d] + 1, :]
            k_dec[c] = (k[c] * jnp.exp2(b_last[c] - bc2[c])).astype(BF16)
            q_dec[c] = q[c] * jnp.exp2(bc2[c])
        for c in chains:
            if bounded:
                a[c] = _dot_nt((q_dec[c] * jnp.exp2(-b_last[c])).astype(BF16), k_dec[c])
            else:
                acc = jnp.zeros((GLA_SUB, GLA_SUB), F32)
                for s in range(GLA_SUB):
                    decay = jnp.exp2(bc2[c] - bc2[c][s:s + 1, :])
                    col = jnp.sum(q[c] * (k[c][s:s + 1, :] * decay), axis=1, keepdims=True)
                    acc = jnp.where(s_lane == s, col, acc)
                a[c] = acc
        inter = {c: _dot_nt(q_dec[c].astype(BF16), st_scr[c[0], c[1]].astype(BF16)) for c in chains}
        kv = {c: _dot_tn(v[c], k_dec[c]) for c in chains}
        intra = {c: _dot(jnp.where(causal[c[0]], a[c], 0.0).astype(BF16), v[c]) for c in chains}
        for c in chains:
            d, h = c
            o_out[d][rows[d], h * GLA_DV:(h + 1) * GLA_DV] = intra[c] + inter[c]
            st_scr[d, h] = jnp.exp2(b_last[c]) * st_scr[d, h] + kv[c]
        return carry

    @pl.when(decay_bounded)
    def _():
        lax.fori_loop(0, n_sub, functools.partial(sub_chunk, bounded=True), 0)

    @pl.when(jnp.logical_not(decay_bounded))
    def _():
        lax.fori_loop(0, n_sub, functools.partial(sub_chunk, bounded=False), 0)

    @pl.when(c == last)
    def _():
        for d in range(2):
            for h in range(GLA_HEADS):
                s_ref[0, d, h] = st_scr[d, h].T


def _gla_scan(p, u, w2, b_a, state, n_seq, seq_len):
    rows = p.shape[0]
    length = min(GLA_BLOCK, seq_len)
    nc = seq_len // length
    kw = GLA_HEADS * GLA_DK
    vw = GLA_HEADS * GLA_DV

    def fwd(b, c):
        return b * nc + c

    def bwd(b, c):
        return b * nc + nc - 1 - c

    args, specs = [], []
    for pos in (fwd, bwd):
        args += [p, p, p, u]
        specs += [pl.BlockSpec((length, kw), lambda b, c, pos=pos: (pos(b, c), 0)),
                  pl.BlockSpec((length, kw), lambda b, c, pos=pos: (pos(b, c), 1)),
                  pl.BlockSpec((length, vw), lambda b, c, pos=pos: (pos(b, c), 2 * kw // vw)),
                  pl.BlockSpec((length, LANES), lambda b, c, pos=pos: (pos(b, c), 0))]
    args += [w2, b_a]
    specs += [pl.BlockSpec(w2.shape, lambda b, c: (0, 0)), pl.BlockSpec(b_a.shape, lambda b, c: (0, 0))]
    s_spec = pl.BlockSpec((1, 2, GLA_HEADS, GLA_DK, GLA_DV), lambda b, c: (b, 0, 0, 0, 0))
    zero_init = state is None
    if not zero_init:
        args.append(state)
        specs.append(s_spec)
    return pl.pallas_call(
        functools.partial(_gla_scan_kernel, zero_init=zero_init), grid=(n_seq, nc), in_specs=specs,
        out_specs=[pl.BlockSpec((length, vw), lambda b, c: (fwd(b, c), 0)),
                   pl.BlockSpec((length, vw), lambda b, c: (bwd(b, c), 0)), s_spec],
        out_shape=[jax.ShapeDtypeStruct((rows, vw), F32), jax.ShapeDtypeStruct((rows, vw), F32),
                   jax.ShapeDtypeStruct((n_seq, 2, GLA_HEADS, GLA_DK, GLA_DV), F32)],
        scratch_shapes=[pltpu.VMEM((2, GLA_HEADS, GLA_DV, GLA_DK), F32), pltpu.VMEM((2, length, kw), F32)],
        compiler_params=_cparams("parallel", "arbitrary"), name="gla_scan",
    )(*args)


def kernel(x_prompt, x_sample, cache_k_0, cache_v_0, state_mlstm_C_1, state_mlstm_n_1, state_mlstm_m_1, state_gla_S_2, cache_k_3, cache_v_3, c, c_ctx, w_mod, b_mod, norm1_g, norm2_g, final_g, router_w, router_b, moe_wg, moe_wu, moe_wd, attn0_w_qkv, attn0_sink, attn0_w_o, mlstm1_w_in, mlstm1_b_gates, mlstm1_conv, mlstm1_norm_g, mlstm1_w_out, gla2_w_in, gla2_w_a1, gla2_w_a2, gla2_b_a, gla2_norm_g, gla2_w_out, attn3_w_qkv, attn3_sink, attn3_w_o):
    n_ctx, ctx_len, d = x_prompt.shape
    n_lat, lat_len, _ = x_sample.shape
    depth = w_mod.shape[0]

    cvec = jnp.concatenate([c_ctx[None, :], c, jnp.zeros((8 - 1 - n_lat, d), F32)], axis=0)
    mod = _modulation(cvec, w_mod, b_mod).reshape(depth, 8, 6, 1, d)

    def mods(layer, kind, latent):
        return mod[layer, 1:1 + n_lat, kind] if latent else mod[layer, 0:1, kind]

    rw_hi = router_w.T.astype(BF16)
    rw_lo = (router_w.T - rw_hi.astype(F32)).astype(BF16)
    rwt = jnp.concatenate([rw_hi, rw_lo], axis=0)
    rb = router_b.reshape(-1, 1)
    attn_w = {0: (attn0_w_qkv.astype(BF16), attn0_sink, attn0_w_o.astype(BF16), cache_k_0, cache_v_0),
              3: (attn3_w_qkv.astype(BF16), attn3_sink, attn3_w_o.astype(BF16), cache_k_3, cache_v_3)}
    ml_qw = ML_HEADS * ML_DK
    ml_vw = ML_HEADS * ML_DV
    ml_main = 2 * ml_qw + 2 * ml_vw
    ml_w_qk = mlstm1_w_in[:, :2 * ml_qw].astype(BF16)
    ml_w_vo = mlstm1_w_in[:, 2 * ml_qw:ml_main].astype(BF16)
    ml_w_gates = mlstm1_w_in[:, ml_main:]
    ml_w_out = mlstm1_w_out.astype(BF16)
    gla_kw = GLA_HEADS * GLA_DK
    gla_w_in = gla2_w_in.astype(BF16)
    gla_w_a1 = jnp.pad(jnp.concatenate([gla2_w_a1[0], gla2_w_a1[1]], axis=1),
                       ((0, 0), (0, LANES - 2 * GLA_RANK))).astype(BF16)
    gla_w2 = jnp.zeros((LANES, 2 * gla_kw), F32)
    gla_w2 = gla_w2.at[:GLA_RANK, :gla_kw].set(gla2_w_a2[0]).at[GLA_RANK:2 * GLA_RANK, gla_kw:].set(gla2_w_a2[1])
    gla_w2 = gla_w2.astype(BF16)
    gla_ba = gla2_b_a.reshape(1, 2 * gla_kw)
    gla_w_out = gla2_w_out.astype(BF16)

    new_state = []

    def mixer(layer, s, count0):
        latent, n_seq, seq_len, x, h = s["latent"], s["n_seq"], s["seq_len"], s["x"], s["h"]
        tail = (mods(layer, 2, latent), norm2_g[layer], mods(layer, 4, latent), mods(layer, 3, latent), rwt, rb,
                count0)
        kind = layer % 3
        if kind == 0:
            w_qkv, sink, w_o, ck, cv = attn_w[layer]
            qkv = _matmul(h, w_qkv)
            if latent:
                att = _attn_lat(qkv, ck, cv, sink, n_seq, seq_len)
            else:
                att = _attn_ctx(qkv, sink, n_seq, seq_len)
                qw = ATT_HEADS * HEAD_DIM
                kw = ATT_KV * HEAD_DIM
                new_state.append(qkv[:, qw:qw + kw].reshape(n_seq, seq_len, ATT_KV, HEAD_DIM))
                new_state.append(qkv[:, qw + kw:].reshape(n_seq, seq_len, ATT_KV, HEAD_DIM))
            return _proj("plain", (att,), w_o, x, *tail)
        if kind == 1:
            p = _matmul(h, ml_w_vo)
            g, gt = _ml_gates(h, ml_w_gates, mlstm1_b_gates)
            qk = _ml_qk(h, ml_w_qk, mlstm1_conv, seq_len)
            st = (state_mlstm_C_1, state_mlstm_n_1, state_mlstm_m_1) if latent else None
            hf, hb, fin = _ml_scan(qk, p, g, gt, st, n_seq, seq_len)
            if not latent:
                new_state.extend(fin)
            return _proj("mlstm", (hf, hb, p, 1, mlstm1_norm_g), ml_w_out, x, *tail)
        p = _matmul(h, gla_w_in)
        u = _matmul(h, gla_w_a1)
        of, ob, s_fin = _gla_scan(p, u, gla_w2, gla_ba, state_gla_S_2 if latent else None, n_seq, seq_len)
        if not latent:
            new_state.append(s_fin)
        gla_vw = GLA_HEADS * GLA_DV
        return _proj("gla", (of, ob, p, (2 * gla_kw + gla_vw) // gla_vw, gla2_norm_g), gla_w_out, x, *tail)

    streams = [dict(latent=False, n_seq=n_ctx, seq_len=ctx_len, x=x_prompt.reshape(n_ctx * ctx_len, d)),
               dict(latent=True, n_seq=n_lat, seq_len=lat_len, x=x_sample.reshape(n_lat * lat_len, d))]
    for s in streams:
        s["h"] = _rownorm(s["x"], norm1_g[0], mods(0, 1, s["latent"]), mods(0, 0, s["latent"]))
    for layer in range(depth):
        counts = jnp.zeros((N_EXPERTS, LANES), jnp.int32)
        for s in streams:
            s["x"], s["h2"], s["meta"], s["wcol"], counts = mixer(layer, s, counts)
        cnt = counts[:, 0]
        for s in streams:
            s["slots"] = _slots(s["meta"], cnt)
        xs, info = _dispatch([s["h2"] for s in streams], jnp.concatenate([s["slots"] for s in streams], axis=1), cnt)
        ys = _ffn(xs, info, moe_wg, moe_wu, moe_wd, layer)
        for s in streams:
            latent = s["latent"]
            gate2 = mods(layer, 5, latent)
            if layer + 1 < depth:
                s["x"], s["h"] = _combine(s["x"], ys, s["slots"], s["wcol"], gate2, norm1_g[layer + 1],
                                          mod=(mods(layer + 1, 1, latent), mods(layer + 1, 0, latent)), out_x=True,
                                          h_dtype=F32 if (layer + 1) % 3 == 1 else BF16)
            else:
                s["out"] = _combine(s["x"], ys, s["slots"], s["wcol"], gate2, final_g, h_dtype=F32)
    y_prompt = streams[0]["out"].reshape(n_ctx, ctx_len, d)
    y_sample = streams[1]["out"].reshape(n_lat, lat_len, d)
    return (y_prompt, y_sample, *new_state)
```

```python
import functools

import jax
import jax.numpy as jnp
from jax import lax
from jax.experimental import pallas as pl
from jax.experimental.pallas import tpu as pltpu

F32 = jnp.float32
BF16 = jnp.bfloat16
HI = lax.Precision.HIGHEST

EPS = 1e-6
LOG2E = 1.4426950408889634
GRID_W = 64
ATT_HEADS = 16
ATT_KV = 4
ATT_GROUP = ATT_HEADS // ATT_KV
HEAD_DIM = 64
WINDOW = 128
Q_BLOCK = 128
ROPE_BASE = 10000.0
ML_HEADS = 8
ML_DK = 128
ML_DV = 256
ML_CHUNK = 128
GLA_HEADS = 4
GLA_DK = 128
GLA_DV = 256
GLA_RANK = 16
GLA_TAU = 16.0
GLA_SUB = 16
GLA_BLOCK = 256
GLA_STEPS = 4
GLA_MAX_EXP2 = 80.0
N_EXPERTS = 16
N_GROUPS = 4
GROUP_SIZE = N_EXPERTS // N_GROUPS
LANES = 128
VMEM_LIMIT = 56 * 1024 * 1024


def _cparams(*sem):
    return pltpu.CompilerParams(dimension_semantics=sem, vmem_limit_bytes=VMEM_LIMIT)


def _dot(a, b):
    return jnp.dot(a, b, preferred_element_type=F32)


def _dot_nt(a, b, precision=None):
    return lax.dot_general(a, b, (((1,), (1,)), ((), ())), precision=precision, preferred_element_type=F32)


def _dot_tn(a, b):
    return lax.dot_general(a, b, (((0,), (0,)), ((), ())), preferred_element_type=F32)


def _sigmoid(x):
    return 1.0 / (1.0 + jnp.exp(-x))


def _silu(x):
    return x * _sigmoid(x)


def _log_sigmoid(x):
    return jnp.minimum(x, 0.0) - jnp.log(1.0 + jnp.exp(-jnp.abs(x)))


def _rms_rows(x, g):
    ms = jnp.mean(x * x, axis=-1, keepdims=True)
    return x * lax.rsqrt(ms + EPS) * g


def _mod_kernel(c_ref, w_ref, b_ref, o_ref):
    s = _silu(c_ref[...])
    o_ref[0] = _dot(s.astype(BF16), w_ref[0].astype(BF16)) + b_ref[0]


def _modulation(cvec, w_mod, b_mod):
    depth, d, n6 = w_mod.shape
    tn = 1536
    return pl.pallas_call(
        _mod_kernel,
        grid=(depth, n6 // tn),
        in_specs=[pl.BlockSpec((8, d), lambda l, j: (0, 0)),
                  pl.BlockSpec((1, d, tn), lambda l, j: (l, 0, j)),
                  pl.BlockSpec((1, 1, tn), lambda l, j: (l, 0, j))],
        out_specs=pl.BlockSpec((1, 8, tn), lambda l, j: (l, 0, j)),
        out_shape=jax.ShapeDtypeStruct((depth, 8, n6), F32),
        compiler_params=_cparams("parallel", "parallel"),
        name="adaln_modulation",
    )(cvec, w_mod, b_mod.reshape(depth, 1, n6))


def _route(h, rwt, rb, carry):
    tm = h.shape[0]
    h_hi = h.astype(BF16)
    h_lo = (h - h_hi.astype(F32)).astype(BF16)
    by_hi = _dot_nt(rwt, h_hi)
    logits = by_hi[:N_EXPERTS] + by_hi[N_EXPERTS:] + _dot_nt(rwt[:N_EXPERTS], h_lo)
    scores = _sigmoid(logits)
    sel = scores + rb
    expert = lax.broadcasted_iota(jnp.int32, sel.shape, 0)
    pos = expert % GROUP_SIZE
    grp = expert // GROUP_SIZE

    def mate(x, k):
        ahead = pltpu.roll(x, N_EXPERTS - k, axis=0)
        behind = pltpu.roll(x, GROUP_SIZE - k, axis=0)
        return jnp.where(pos + k < GROUP_SIZE, ahead, behind)

    beaten = jnp.zeros_like(sel)
    for k in range(1, GROUP_SIZE):
        other = mate(sel, k)
        other_first = (pos + k) % GROUP_SIZE < pos
        beaten = beaten + jnp.where(other_first, jnp.where(other >= sel, 1.0, 0.0), jnp.where(other > sel, 1.0, 0.0))
    top2 = jnp.where(beaten < 2.0, sel, 0.0)
    gscore = top2
    for k in range(1, GROUP_SIZE):
        gscore = gscore + mate(top2, k)
    lost = jnp.zeros_like(sel)
    for k in range(1, N_GROUPS):
        other = pltpu.roll(gscore, N_EXPERTS - GROUP_SIZE * k, axis=0)
        other_first = (grp + k) % N_GROUPS < grp
        lost = lost + jnp.where(other_first, jnp.where(other >= gscore, 1.0, 0.0),
                                jnp.where(other > gscore, 1.0, 0.0))
    picked = jnp.where(lost < 0.5, jnp.where(beaten < 2.0, 1.0, 0.0), 0.0)
    chosen = picked > 0.5
    weight = jnp.where(chosen, scores, 0.0)
    wsum = jnp.sum(weight, axis=0, keepdims=True)
    e_f = expert.astype(F32)
    e_a = jnp.min(jnp.where(chosen, e_f, float(N_EXPERTS)), axis=0, keepdims=True)
    e_b = jnp.max(jnp.where(chosen, e_f, -1.0), axis=0, keepdims=True)
    before = (lax.broadcasted_iota(jnp.int32, (tm, tm), 0) < lax.broadcasted_iota(jnp.int32, (tm, tm), 1))
    rank = _dot(picked.astype(BF16), jnp.where(before, 1.0, 0.0).astype(BF16)) + carry
    is_a = e_f == e_a
    is_b = e_f == e_b
    r_a = jnp.sum(jnp.where(is_a, rank, 0.0), axis=0, keepdims=True)
    r_b = jnp.sum(jnp.where(is_b, rank, 0.0), axis=0, keepdims=True)
    w_a = jnp.sum(jnp.where(is_a, weight, 0.0), axis=0, keepdims=True)
    w_b = jnp.sum(jnp.where(is_b, weight, 0.0), axis=0, keepdims=True)
    meta = jnp.concatenate([e_a, e_b, r_a, r_b, jnp.zeros((4, tm), F32)], axis=0).astype(jnp.int32)
    wcol = jnp.concatenate([w_a / wsum, w_b / wsum, jnp.zeros((LANES - 2, tm), F32)], axis=0).T
    return meta, wcol, carry + jnp.sum(picked, axis=1, keepdims=True)


def _norm_mod(x, g_ref, mod_refs):
    h = _rms_rows(x, g_ref[...])
    if mod_refs is not None:
        a_ref, s_ref = mod_refs
        h = h * (1.0 + a_ref[0]) + s_ref[0]
    return h


def _rownorm_kernel(x_ref, g_ref, a_ref, s_ref, h_ref):
    h_ref[...] = _norm_mod(x_ref[...], g_ref, (a_ref, s_ref)).astype(h_ref.dtype)


def _mod_spec(n_mod, rows, tm, d, n_prefetch=0):
    per = (rows // n_mod) // tm
    return pl.BlockSpec((1, 1, d), lambda i, *_: (i // per, 0, 0))


def _rownorm(x, g, scale, shift, *, tm=512):
    rows, d = x.shape
    row_spec = pl.BlockSpec((tm, d), lambda i: (i, 0))
    return pl.pallas_call(
        _rownorm_kernel, grid=(rows // tm,),
        in_specs=[row_spec, pl.BlockSpec((1, d), lambda i: (0, 0)), _mod_spec(scale.shape[0], rows, tm, d),
                  _mod_spec(shift.shape[0], rows, tm, d)],
        out_specs=row_spec, out_shape=jax.ShapeDtypeStruct((rows, d), BF16),
        compiler_params=_cparams("parallel"), name="rownorm",
    )(x, g.reshape(1, d), scale, shift)


def _mm_kernel(a_ref, w_ref, o_ref):
    o_ref[...] = _dot(a_ref[...].astype(BF16), w_ref[...]).astype(o_ref.dtype)


def _matmul(a, w, *, out_dtype=F32, tm=1024):
    m, k = a.shape
    n = w.shape[1]
    tn = next(t for t in (1024, 768, 512, LANES) if n % t == 0)
    return pl.pallas_call(
        _mm_kernel, grid=(m // tm, n // tn),
        in_specs=[pl.BlockSpec((tm, k), lambda i, j: (i, 0)), pl.BlockSpec((k, tn), lambda i, j: (0, j))],
        out_specs=pl.BlockSpec((tm, tn), lambda i, j: (i, j)),
        out_shape=jax.ShapeDtypeStruct((m, n), out_dtype),
        compiler_params=_cparams("parallel", "parallel"), name="matmul",
    )(a, w)


def _head_norm(x, g, n_heads, dv):
    outs = []
    for h in range(n_heads):
        xs = x[:, h * dv:(h + 1) * dv]
        ms = jnp.mean(xs * xs, axis=-1, keepdims=True)
        outs.append(xs * lax.rsqrt(ms + EPS) * g[:, h * dv:(h + 1) * dv])
    return jnp.concatenate(outs, axis=1)


def _proj_kernel(*refs, pre):
    refs = list(refs)
    if pre == "plain":
        a = refs.pop(0)[...]
    else:
        f_ref, b_ref, p_ref, hg_ref = refs.pop(0), refs.pop(0), refs.pop(0), refs.pop(0)
        hsum = f_ref[...] + b_ref[...]
        if pre == "mlstm":
            a = _sigmoid(p_ref[...]) * _head_norm(hsum, hg_ref[...], ML_HEADS, ML_DV)
        else:
            a = _head_norm(hsum, hg_ref[...], GLA_HEADS, GLA_DV) * _silu(p_ref[...])
        a = a.astype(BF16)
    w_ref, x_ref, gate_ref, g_ref, a_ref, s_ref, rwt_ref, rb_ref, count0_ref = refs[:9]
    xo_ref, h_ref, meta_ref, wcol_ref, count_ref, carry_ref = refs[9:]

    @pl.when(pl.program_id(0) == 0)
    def _():
        carry_ref[...] = count0_ref[...].astype(F32)

    x = x_ref[...] + gate_ref[0] * _dot(a, w_ref[...])
    xo_ref[...] = x
    h = _norm_mod(x, g_ref, (a_ref, s_ref))
    _rows_to_tiles(h_ref, h)
    meta, wcol, carry = _route(h, rwt_ref[...], rb_ref[...], carry_ref[:, 0:1])
    meta_ref[...] = meta
    wcol_ref[...] = wcol
    carry_ref[...] = jnp.broadcast_to(carry, carry_ref.shape)
    count_ref[...] = jnp.broadcast_to(carry, count_ref.shape).astype(jnp.int32)


def _proj(pre, pre_args, w_out, x, gate, g, scale, shift, rwt, rb, count0, *, tm=512):
    rows, d = x.shape
    k = w_out.shape[0]
    row_spec = pl.BlockSpec((tm, d), lambda i: (i, 0))
    if pre == "plain":
        args, specs = [pre_args[0]], [pl.BlockSpec((tm, k), lambda i: (i, 0))]
    else:
        hf, hb, p, col_block, hg = pre_args
        wide = pl.BlockSpec((tm, k), lambda i: (i, 0))
        args = [hf, hb, p, hg.reshape(1, k)]
        specs = [wide, wide, pl.BlockSpec((tm, k), lambda i: (i, col_block)), pl.BlockSpec((1, k), lambda i: (0, 0))]
    args += [w_out, x, gate, g.reshape(1, d), scale, shift, rwt, rb, count0]
    specs += [pl.BlockSpec((k, d), lambda i: (0, 0)), row_spec, _mod_spec(gate.shape[0], rows, tm, d),
              pl.BlockSpec((1, d), lambda i: (0, 0)), _mod_spec(scale.shape[0], rows, tm, d),
              _mod_spec(shift.shape[0], rows, tm, d), pl.BlockSpec(rwt.shape, lambda i: (0, 0)),
              pl.BlockSpec(rb.shape, lambda i: (0, 0)), pl.BlockSpec(count0.shape, lambda i: (0, 0))]
    return pl.pallas_call(
        functools.partial(_proj_kernel, pre=pre), grid=(rows // tm,), in_specs=specs,
        out_specs=[row_spec, pl.BlockSpec((tm * SUBLANES, LANES), lambda i: (i, 0)),
                   pl.BlockSpec((8, tm), lambda i: (0, i)),
                   pl.BlockSpec((tm, LANES), lambda i: (i, 0)), pl.BlockSpec((N_EXPERTS, LANES), lambda i: (0, 0))],
        out_shape=[jax.ShapeDtypeStruct((rows, d), F32), jax.ShapeDtypeStruct((rows * SUBLANES, LANES), F32),
                   jax.ShapeDtypeStruct((8, rows), jnp.int32), jax.ShapeDtypeStruct((rows, LANES), F32),
                   jax.ShapeDtypeStruct((N_EXPERTS, LANES), jnp.int32)],
        scratch_shapes=[pltpu.VMEM((N_EXPERTS, LANES), F32)],
        compiler_params=_cparams("arbitrary"), name="proj_" + pre,
    )(*args)


MOE_TILE = 512
MOE_TILE_SHIFT = 9
MOE_TOKENS = 256
FFN_PARTS = 2
ROW_UNROLL = 8


SUBLANES = 8


def _rows_to_tiles(ref, x, lead=(), first=0):
    rows = x.shape[0]
    for c in range(SUBLANES):
        ref[(*lead, pl.ds(first * SUBLANES + c, rows, stride=SUBLANES), slice(None))] = x[:, c * LANES:(c + 1) * LANES]


def _tiles_to_rows(ref, rows, lead=(), first=0):
    return jnp.concatenate([ref[(*lead, pl.ds(first * SUBLANES + c, rows, stride=SUBLANES), slice(None))]
                            for c in range(SUBLANES)], axis=1)


def _slot_tiles(rows):
    return (2 * rows) // MOE_TILE + N_EXPERTS


def _expert_offsets(cnt_ref, off_ref):
    def per_expert(e, k):
        off_ref[e] = k * MOE_TILE
        return k + ((cnt_ref[e] + MOE_TILE - 1) >> MOE_TILE_SHIFT)
    return lax.fori_loop(0, N_EXPERTS, per_expert, 0)


def _slots_kernel(cnt_ref, meta_ref, slot_ref, off_ref):
    @pl.when(pl.program_id(0) == 0)
    def _():
        _expert_offsets(cnt_ref, off_ref)

    e_a, e_b = meta_ref[0:1, :], meta_ref[1:2, :]
    off_a = jnp.zeros_like(e_a)
    off_b = jnp.zeros_like(e_b)
    for e in range(N_EXPERTS):
        off_a = jnp.where(e_a == e, off_ref[e], off_a)
        off_b = jnp.where(e_b == e, off_ref[e], off_b)
    slot_ref[...] = jnp.concatenate([off_a + meta_ref[2:3, :], off_b + meta_ref[3:4, :],
                                     jnp.zeros((6, e_a.shape[1]), jnp.int32)], axis=0)


def _slots(meta, counts, *, tm=1024):
    rows = meta.shape[1]
    grid_spec = pltpu.PrefetchScalarGridSpec(
        num_scalar_prefetch=1, grid=(rows // tm,),
        in_specs=[pl.BlockSpec((8, tm), lambda i, cnt: (0, i))],
        out_specs=pl.BlockSpec((8, tm), lambda i, cnt: (0, i)),
        scratch_shapes=[pltpu.SMEM((N_EXPERTS,), jnp.int32)])
    return pl.pallas_call(
        _slots_kernel, grid_spec=grid_spec, out_shape=jax.ShapeDtypeStruct((8, rows), jnp.int32),
        compiler_params=_cparams("arbitrary"), name="moe_slots",
    )(counts, meta)


def _dispatch_kernel(*refs, steps):
    sa_ref, sb_ref, cnt_ref = refs[:3]
    h_refs = refs[3:3 + len(steps)]
    xs_ref, info_ref, off_ref, zero_ref, sem = refs[3 + len(steps):]
    i = pl.program_id(0)
    tm = h_refs[0].shape[0] // SUBLANES
    n_tiles = info_ref.shape[0] - 1
    tile_rows = MOE_TILE * SUBLANES

    def tile_copy(tile):
        return pltpu.make_async_copy(zero_ref, xs_ref.at[pl.ds(tile * tile_rows, tile_rows), :], sem)

    @pl.when(i == 0)
    def _():
        zero_ref[...] = jnp.zeros_like(zero_ref)
        used = _expert_offsets(cnt_ref, off_ref)

        def per_expert(e, _):
            first = off_ref[e] >> MOE_TILE_SHIFT
            nt = (cnt_ref[e] + MOE_TILE - 1) >> MOE_TILE_SHIFT

            def fill(j, _):
                info_ref[first + j] = e
                return 0
            lax.fori_loop(0, nt, fill, 0)

            @pl.when(nt > 0)
            def _():
                tile_copy(first + nt - 1).start()
                tile_copy(first + nt - 1).wait()
            return 0
        lax.fori_loop(0, N_EXPERTS, per_expert, 0)
        info_ref[n_tiles] = used

        def tail(j, _):
            info_ref[j] = N_EXPERTS - 1
            tile_copy(j).start()
            tile_copy(j).wait()
            return 0
        lax.fori_loop(used, n_tiles, tail, 0)

    base = i * tm

    def copy_rows(h_ref):
        def row_copy(t, slot):
            dst = pl.multiple_of(slot * SUBLANES, SUBLANES)
            return pltpu.make_async_copy(h_ref.at[pl.ds(t * SUBLANES, SUBLANES), :],
                                         xs_ref.at[pl.ds(dst, SUBLANES), :], sem)

        for t in range(tm):
            row_copy(t, sa_ref[base + t]).start(priority=0)
            row_copy(t, sb_ref[base + t]).start(priority=1)
        for _ in range(2):
            pltpu.make_async_copy(h_ref, xs_ref.at[pl.ds(0, tm * SUBLANES), :], sem).wait()

    first = 0
    for h_ref, n in zip(h_refs, steps):
        pl.when(jnp.logical_and(i >= first, i < first + n))(functools.partial(copy_rows, h_ref))
        first += n


def _dispatch(hs, slots, counts):
    tm = MOE_TOKENS
    steps = tuple(h.shape[0] // (tm * SUBLANES) for h in hs)
    n_tiles = _slot_tiles(sum(steps) * tm)
    specs, first = [], 0
    for n in steps:
        specs.append(pl.BlockSpec((tm * SUBLANES, LANES),
                                  lambda i, *_, first=first, n=n: (jnp.clip(i - first, 0, n - 1), 0)))
        first += n
    grid_spec = pltpu.PrefetchScalarGridSpec(
        num_scalar_prefetch=3, grid=(sum(steps),), in_specs=specs,
        out_specs=[pl.BlockSpec(memory_space=pl.ANY), pl.BlockSpec(memory_space=pltpu.SMEM)],
        scratch_shapes=[pltpu.SMEM((N_EXPERTS,), jnp.int32), pltpu.VMEM((MOE_TILE * SUBLANES, LANES), F32),
                        pltpu.SemaphoreType.DMA(())])
    return pl.pallas_call(
        functools.partial(_dispatch_kernel, steps=steps), grid_spec=grid_spec,
        out_shape=[jax.ShapeDtypeStruct((n_tiles * MOE_TILE * SUBLANES, LANES), F32),
                   jax.ShapeDtypeStruct((n_tiles + 1,), jnp.int32)],
        compiler_params=_cparams("arbitrary"), name="moe_dispatch",
    )(slots[0], slots[1], counts, *hs)


def _ffn_kernel(info_ref, xs_ref, wg_ref, wu_ref, wd_ref, ys_ref, wg_s, wu_s, wd_s):
    i = pl.program_id(0)
    used = info_ref[info_ref.shape[0] - 1]
    fresh = jnp.logical_or(i == 0, info_ref[i] != info_ref[jnp.maximum(i - 1, 0)])

    @pl.when(jnp.logical_and(i < used, fresh))
    def _():
        wg_s[...] = wg_ref[0, 0].astype(BF16)
        wu_s[...] = wu_ref[0, 0].astype(BF16)
        wd_s[...] = wd_ref[0, 0].astype(BF16)

    @pl.when(i < used)
    def _():
        part = MOE_TILE // FFN_PARTS
        x = [_tiles_to_rows(xs_ref, part, first=p * part).astype(BF16) for p in range(FFN_PARTS)]
        gate = [_dot(xp, wg_s[...]) for xp in x]
        up = [_dot(xp, wu_s[...]) for xp in x]
        hid = [(_silu(g) * u).astype(BF16) for g, u in zip(gate, up)]
        y = [_dot(hp, wd_s[...]) for hp in hid]
        for p in range(FFN_PARTS):
            _rows_to_tiles(ys_ref, y[p], first=p * part)

    @pl.when(i >= used)
    def _():
        ys_ref[...] = jnp.zeros_like(ys_ref)


def _ffn(xs, info, wg, wu, wd, layer):
    tile_rows = MOE_TILE * SUBLANES
    n_tiles = xs.shape[0] // tile_rows
    d, f = wg.shape[2:]

    def w_map(i, info):
        return (layer, info[i], 0, 0)

    grid_spec = pltpu.PrefetchScalarGridSpec(
        num_scalar_prefetch=1, grid=(n_tiles,),
        in_specs=[pl.BlockSpec((tile_rows, LANES), lambda i, info: (jnp.minimum(i, info[n_tiles] - 1), 0)),
                  pl.BlockSpec((1, 1, d, f), w_map), pl.BlockSpec((1, 1, d, f), w_map),
                  pl.BlockSpec((1, 1, f, d), w_map)],
        out_specs=pl.BlockSpec((tile_rows, LANES), lambda i, info: (i, 0)),
        scratch_shapes=[pltpu.VMEM((d, f), BF16), pltpu.VMEM((d, f), BF16), pltpu.VMEM((f, d), BF16)])
    return pl.pallas_call(
        _ffn_kernel, grid_spec=grid_spec, out_shape=jax.ShapeDtypeStruct(xs.shape, F32),
        compiler_params=_cparams("arbitrary"), name="moe_ffn",
    )(info, xs, wg, wu, wd)


def _combine_kernel(*refs, has_mod, out_x):
    refs = list(refs)
    sa_ref, sb_ref, x_ref, ys_ref, wcol_ref, gate_ref, g_ref = refs[:7]
    refs = refs[7:]
    mod_refs = (refs.pop(0), refs.pop(0)) if has_mod else None
    xo_ref = refs.pop(0) if out_x else None
    h_ref, buf_a, buf_b, sems = refs
    i = pl.program_id(0)
    tm = x_ref.shape[0]

    def issue(tile, slot):
        base = tile * tm
        for t in range(tm):
            dst = pl.ds(t * SUBLANES, SUBLANES)
            src_a = pl.multiple_of(sa_ref[base + t] * SUBLANES, SUBLANES)
            src_b = pl.multiple_of(sb_ref[base + t] * SUBLANES, SUBLANES)
            pltpu.make_async_copy(ys_ref.at[pl.ds(src_a, SUBLANES), :], buf_a.at[slot, dst, :],
                                  sems.at[slot]).start(priority=0)
            pltpu.make_async_copy(ys_ref.at[pl.ds(src_b, SUBLANES), :], buf_b.at[slot, dst, :],
                                  sems.at[slot]).start(priority=1)

    @pl.when(i == 0)
    def _():
        issue(0, 0)

    @pl.when(i + 1 < pl.num_programs(0))
    def _():
        issue(i + 1, (i + 1) % 2)

    slot = i % 2
    for buf in (buf_a, buf_b):
        pltpu.make_async_copy(ys_ref.at[pl.ds(0, tm * SUBLANES), :], buf.at[slot], sems.at[slot]).wait()
    y = (wcol_ref[:, 0:1] * _tiles_to_rows(buf_a, tm, lead=(slot,))
         + wcol_ref[:, 1:2] * _tiles_to_rows(buf_b, tm, lead=(slot,)))
    x = x_ref[...] + gate_ref[0] * y
    if out_x:
        xo_ref[...] = x
    h_ref[...] = _norm_mod(x, g_ref, mod_refs).astype(h_ref.dtype)


def _combine(x, ys, slots, wcol, gate, g, *, mod=None, out_x=False, h_dtype=BF16):
    rows, d = x.shape
    tm = MOE_TOKENS
    row_spec = pl.BlockSpec((tm, d), lambda i, *_: (i, 0))
    args = [x, ys, wcol, gate, g.reshape(1, d)]
    specs = [row_spec, pl.BlockSpec(memory_space=pl.ANY), pl.BlockSpec((tm, LANES), lambda i, *_: (i, 0)),
             _mod_spec(gate.shape[0], rows, tm, d), pl.BlockSpec((1, d), lambda i, *_: (0, 0))]
    if mod is not None:
        for m in mod:
            args.append(m)
            specs.append(_mod_spec(m.shape[0], rows, tm, d))
    out_shape, out_specs = [], []
    if out_x:
        out_shape.append(jax.ShapeDtypeStruct((rows, d), F32))
        out_specs.append(row_spec)
    out_shape.append(jax.ShapeDtypeStruct((rows, d), h_dtype))
    out_specs.append(row_spec)
    grid_spec = pltpu.PrefetchScalarGridSpec(
        num_scalar_prefetch=2, grid=(rows // tm,), in_specs=specs, out_specs=out_specs,
        scratch_shapes=[pltpu.VMEM((2, tm * SUBLANES, LANES), F32), pltpu.VMEM((2, tm * SUBLANES, LANES), F32),
                        pltpu.SemaphoreType.DMA((2,))])
    outs = pl.pallas_call(
        functools.partial(_combine_kernel, has_mod=mod is not None, out_x=out_x), grid_spec=grid_spec,
        out_shape=out_shape, compiler_params=_cparams("arbitrary"), name="moe_combine",
    )(slots[0], slots[1], *args)
    return outs if out_x else outs[0]


def _softmax_av(groups):
    maxes = []
    for scores, _, sink in groups:
        m = sink
        for s in scores:
            m = jnp.maximum(m, jnp.broadcast_to(jnp.max(s, axis=-1, keepdims=True), sink.shape))
        maxes.append(m)
    probs = [[jnp.exp(s - jnp.concatenate([m] * (s.shape[1] // LANES), axis=1)).astype(BF16) for s in scores]
             for (scores, _, _), m in zip(groups, maxes)]
    outs = []
    for (_, values, sink), m, ps in zip(groups, maxes, probs):
        hd = values[0].shape[1]
        tot = None
        for p, v in zip(ps, values):
            n = v.shape[0]
            v_ext = jnp.concatenate([v, jnp.zeros((n, LANES - hd), BF16), jnp.ones((n, LANES), BF16)], axis=1)
            pv = _dot(p, v_ext)
            tot = pv if tot is None else tot + pv
        outs.append((tot, sink, m, hd))
    return [tot[:, :hd] / (tot[:, LANES:] + jnp.exp(sink - m))[:, :hd] for tot, sink, m, hd in outs]


def _sink_column(sink_ref, kv, rows):
    return jnp.concatenate([jnp.full((rows, LANES), sink_ref[kv * ATT_GROUP + g], F32) for g in range(ATT_GROUP)],
                           axis=0)


def _attn_ctx_kernel(sink_ref, qkv_ref, o_ref):
    t = qkv_ref.shape[0]
    qw = ATT_HEADS * HEAD_DIM
    kw = ATT_KV * HEAD_DIM
    groups = []
    for kv in range(ATT_KV):
        q = jnp.concatenate(
            [qkv_ref[:, (kv * ATT_GROUP + g) * HEAD_DIM:(kv * ATT_GROUP + g + 1) * HEAD_DIM] for g in range(ATT_GROUP)],
            axis=0).astype(BF16)
        k = qkv_ref[:, qw + kv * HEAD_DIM:qw + (kv + 1) * HEAD_DIM].astype(BF16)
        v = qkv_ref[:, qw + kw + kv * HEAD_DIM:qw + kw + (kv + 1) * HEAD_DIM].astype(BF16)
        groups.append(([_dot_nt(q, k) * HEAD_DIM ** -0.5], [v], _sink_column(sink_ref, kv, t)))
    heads_out = [o[g * t:(g + 1) * t] for o in _softmax_av(groups) for g in range(ATT_GROUP)]
    o_ref[...] = jnp.concatenate(heads_out, axis=1).astype(o_ref.dtype)


def _attn_ctx(qkv, sink, n_seq, seq_len):
    rows, cols = qkv.shape
    return pl.pallas_call(
        _attn_ctx_kernel, grid=(n_seq,),
        in_specs=[pl.BlockSpec(memory_space=pltpu.SMEM), pl.BlockSpec((seq_len, cols), lambda b: (b, 0))],
        out_specs=pl.BlockSpec((seq_len, ATT_HEADS * HEAD_DIM), lambda b: (b, 0)),
        out_shape=jax.ShapeDtypeStruct((rows, ATT_HEADS * HEAD_DIM), BF16),
        compiler_params=_cparams("parallel"), name="attn_context",
    )(sink, qkv)


def _rope_block(x, cos, sin_signed):
    lane = lax.broadcasted_iota(jnp.int32, x.shape, 1)
    nf = HEAD_DIM // 4
    partner = jnp.where((lane % (2 * nf)) < nf, pltpu.roll(x, LANES - nf, axis=1), pltpu.roll(x, nf, axis=1))
    return x * cos + partner * sin_signed


def _attn_lat_kernel(sink_ref, qkv_ref, ck_ref, cv_ref, cos_ref, sin_ref, o_ref, k_scr):
    i = pl.program_id(1)
    t = qkv_ref.shape[0]
    qw = ATT_HEADS * HEAD_DIM
    kw = ATT_KV * HEAD_DIM
    span = Q_BLOCK + 2 * WINDOW

    @pl.when(i == 0)
    def _():
        for c in range(kw // LANES):
            blk = qkv_ref[:, qw + c * LANES:qw + (c + 1) * LANES]
            k_scr[:, c * LANES:(c + 1) * LANES] = _rope_block(blk, cos_ref[...], sin_ref[...]).astype(BF16)

    r0 = pl.multiple_of(i * Q_BLOCK, Q_BLOCK)
    ws = pl.multiple_of(jnp.clip(r0 - WINDOW, 0, t - span), Q_BLOCK)
    cos_q = cos_ref[pl.ds(r0, Q_BLOCK), :]
    sin_q = sin_ref[pl.ds(r0, Q_BLOCK), :]
    qpos = r0 + lax.broadcasted_iota(jnp.int32, (Q_BLOCK, span), 0)
    kpos = ws + lax.broadcasted_iota(jnp.int32, (Q_BLOCK, span), 1)
    band = jnp.abs(qpos - kpos) <= WINDOW
    band = jnp.concatenate([band] * ATT_GROUP, axis=0)
    groups = []
    for kv in range(ATT_KV):
        heads = []
        for g in range(ATT_GROUP):
            h = kv * ATT_GROUP + g
            c, half = divmod(h * HEAD_DIM, LANES)
            blk = _rope_block(qkv_ref[pl.ds(r0, Q_BLOCK), c * LANES:(c + 1) * LANES], cos_q, sin_q)
            heads.append(blk[:, half:half + HEAD_DIM])
        q = jnp.concatenate(heads, axis=0).astype(BF16)
        ck = ck_ref[0, :, kv * HEAD_DIM:(kv + 1) * HEAD_DIM].astype(BF16)
        cv = cv_ref[0, :, kv * HEAD_DIM:(kv + 1) * HEAD_DIM].astype(BF16)
        kwin = k_scr[pl.ds(ws, span), kv * HEAD_DIM:(kv + 1) * HEAD_DIM]
        vwin = qkv_ref[pl.ds(ws, span), qw + kw + kv * HEAD_DIM:qw + kw + (kv + 1) * HEAD_DIM].astype(BF16)
        s_ctx = _dot_nt(q, ck) * HEAD_DIM ** -0.5
        s_win = jnp.where(band, _dot_nt(q, kwin) * HEAD_DIM ** -0.5, -jnp.inf)
        groups.append(([s_ctx, s_win], [cv, vwin], _sink_column(sink_ref, kv, Q_BLOCK)))
    heads_out = [o[g * Q_BLOCK:(g + 1) * Q_BLOCK] for o in _softmax_av(groups) for g in range(ATT_GROUP)]
    o_ref[...] = jnp.concatenate(heads_out, axis=1).astype(o_ref.dtype)


def _rope_tables(seq_len):
    pos = jnp.arange(seq_len, dtype=jnp.int32)
    row = (pos // GRID_W).astype(F32)
    col = (pos % GRID_W).astype(F32)
    nf = HEAD_DIM // 4
    inv = ROPE_BASE ** (-jnp.arange(nf, dtype=F32) / nf)
    ang_r = row[:, None] * inv[None, :]
    ang_c = col[:, None] * inv[None, :]
    cos_h = jnp.concatenate([jnp.cos(ang_r), jnp.cos(ang_r), jnp.cos(ang_c), jnp.cos(ang_c)], axis=1)
    sin_h = jnp.concatenate([-jnp.sin(ang_r), jnp.sin(ang_r), -jnp.sin(ang_c), jnp.sin(ang_c)], axis=1)
    reps = LANES // HEAD_DIM
    return jnp.tile(cos_h, (1, reps)), jnp.tile(sin_h, (1, reps))


def _attn_lat(qkv, cache_k, cache_v, sink, n_seq, seq_len):
    rows, cols = qkv.shape
    past = cache_k.shape[1]
    kw = ATT_KV * HEAD_DIM
    cos, sin = _rope_tables(seq_len)
    return pl.pallas_call(
        _attn_lat_kernel, grid=(n_seq, seq_len // Q_BLOCK),
        in_specs=[pl.BlockSpec(memory_space=pltpu.SMEM),
                  pl.BlockSpec((seq_len, cols), lambda b, i: (b, 0)),
                  pl.BlockSpec((1, past, kw), lambda b, i: (b, 0, 0)),
                  pl.BlockSpec((1, past, kw), lambda b, i: (b, 0, 0)),
                  pl.BlockSpec((seq_len, LANES), lambda b, i: (0, 0)),
                  pl.BlockSpec((seq_len, LANES), lambda b, i: (0, 0))],
        out_specs=pl.BlockSpec((Q_BLOCK, ATT_HEADS * HEAD_DIM), lambda b, i: (b * (seq_len // Q_BLOCK) + i, 0)),
        out_shape=jax.ShapeDtypeStruct((rows, ATT_HEADS * HEAD_DIM), BF16),
        scratch_shapes=[pltpu.VMEM((seq_len, kw), BF16)],
        compiler_params=_cparams("parallel", "arbitrary"), name="attn_latent",
    )(sink, qkv, cache_k.reshape(n_seq, past, kw), cache_v.reshape(n_seq, past, kw), cos, sin)


def _split_bf16(w):
    hi = w.astype(BF16)
    return hi, (w - hi.astype(F32)).astype(BF16)


def _select_lanes(x, sel):
    x1 = x.astype(BF16)
    rest = x - x1.astype(F32)
    x2 = rest.astype(BF16)
    x3 = (rest - x2.astype(F32)).astype(BF16)
    return _dot(x1, sel) + _dot(x2, sel) + _dot(x3, sel)


def _ml_gates_kernel(h_ref, w_ref, wt_ref, b_ref, bt_ref, g_ref, gt_ref):
    h = h_ref[...]
    h_hi = h.astype(BF16)
    h_lo = (h - h_hi.astype(F32)).astype(BF16)
    ng = gt_ref.shape[0]
    by_hi = _dot(h_hi, w_ref[...])
    g_ref[...] = by_hi[:, :LANES] + by_hi[:, LANES:] + _dot(h_lo, w_ref[:, :LANES]) + b_ref[...]
    by_hi_t = _dot_nt(wt_ref[...], h_hi)
    gt_ref[...] = by_hi_t[:ng] + by_hi_t[ng:] + _dot_nt(wt_ref[:ng, :], h_lo) + bt_ref[...]


def _ml_gates(h, w_gates, b_gates, *, tm=512):
    rows, d = h.shape
    ng = w_gates.shape[1]
    w_hi, w_lo = _split_bf16(jnp.pad(w_gates, ((0, 0), (0, LANES - ng))))
    wt_hi, wt_lo = _split_bf16(w_gates.T)
    b_pad = jnp.pad(b_gates, (0, LANES - ng)).reshape(1, LANES)
    return pl.pallas_call(
        _ml_gates_kernel, grid=(rows // tm,),
        in_specs=[pl.BlockSpec((tm, d), lambda i: (i, 0)), pl.BlockSpec((d, 2 * LANES), lambda i: (0, 0)),
                  pl.BlockSpec((2 * ng, d), lambda i: (0, 0)), pl.BlockSpec((1, LANES), lambda i: (0, 0)),
                  pl.BlockSpec((ng, 1), lambda i: (0, 0))],
        out_specs=[pl.BlockSpec((tm, LANES), lambda i: (i, 0)), pl.BlockSpec((ng, tm), lambda i: (0, i))],
        out_shape=[jax.ShapeDtypeStruct((rows, LANES), F32), jax.ShapeDtypeStruct((ng, rows), F32)],
        compiler_params=_cparams("parallel"), name="mlstm_gates",
    )(h, jnp.concatenate([w_hi, w_lo], axis=1), jnp.concatenate([wt_hi, wt_lo], axis=0), b_pad,
      b_gates.reshape(ng, 1))


def _ml_qk_kernel(h_ref, w_ref, cw_ref, o_ref, *, seq_len, k_scale):
    j = pl.program_id(1)
    x = _dot(h_ref[...].astype(BF16), w_ref[...])
    t = x.shape[0]
    pos = lax.broadcasted_iota(jnp.int32, x.shape, 0) % seq_len
    prev = jnp.where(pos == 0, 0.0, pltpu.roll(x, 1, axis=0))
    nxt = jnp.where(pos == seq_len - 1, 0.0, pltpu.roll(x, t - 1, axis=0))
    y = prev * cw_ref[0:1, :] + x * cw_ref[1:2, :] + nxt * cw_ref[2:3, :]
    scale = jnp.where(j >= pl.num_programs(1) // 2, k_scale, 1.0).astype(F32)
    o_ref[...] = (_silu(y) * scale).astype(o_ref.dtype)


def _ml_qk(h, w_qk, conv_w, seq_len, *, tm=1024, tn=1024):
    rows, d = h.shape
    width = w_qk.shape[1]
    return pl.pallas_call(
        functools.partial(_ml_qk_kernel, seq_len=seq_len, k_scale=ML_DK ** -0.5), grid=(rows // tm, width // tn),
        in_specs=[pl.BlockSpec((tm, d), lambda i, j: (i, 0)), pl.BlockSpec((d, tn), lambda i, j: (0, j)),
                  pl.BlockSpec((3, tn), lambda i, j: (0, j))],
        out_specs=pl.BlockSpec((tm, tn), lambda i, j: (i, j)),
        out_shape=jax.ShapeDtypeStruct((rows, width), BF16),
        compiler_params=_cparams("parallel", "parallel"), name="mlstm_qk",
    )(h, w_qk, conv_w)


def _ml_scan_kernel(*refs, zero_init):
    refs = list(refs)
    dirs = [tuple(refs[0:5]), tuple(refs[5:10])]
    sel_ref = refs[10]
    refs = refs[11:]
    if not zero_init:
        c0_ref, n0_ref, m0_ref = refs[:3]
        refs = refs[3:]
    hf_ref, hb_ref, c_ref, n_ref, m_ref = refs
    h_out = (hf_ref, hb_ref)
    c = pl.program_id(1)
    last = pl.num_programs(1) - 1

    @pl.when(c == 0)
    def _():
        if zero_init:
            c_ref[...] = jnp.zeros_like(c_ref)
            n_ref[...] = jnp.zeros_like(n_ref)
            m_ref[...] = jnp.zeros_like(m_ref)
        else:
            c_ref[...] = c0_ref[...]
            n_ref[...] = n0_ref[...]
            m_ref[...] = m0_ref[...]

    length = hf_ref.shape[0]
    ti = lax.broadcasted_iota(jnp.int32, (length, length), 0)
    si = lax.broadcasted_iota(jnp.int32, (length, length), 1)
    for d in range(2):
        q_ref, k_ref, v_ref, g_ref, gt_ref = dirs[d]
        causal = (ti >= si) if d == 0 else (ti <= si)
        tri = jnp.where(causal, 1.0, 0.0).astype(F32)
        g_col = g_ref[...]
        f_row = _log_sigmoid(gt_ref[...])
        b_col = jnp.dot(tri, _log_sigmoid(g_col), precision=HI, preferred_element_type=F32)
        b_row = _dot_nt(f_row, tri, precision=HI)
        i_rep = _select_lanes(g_col, sel_ref[d, 0])
        b_rep = _select_lanes(b_col, sel_ref[d, 1])
        edge = length - 1 if d == 0 else 0
        heads = range(ML_HEADS)
        q = [q_ref[:, h * ML_DK:(h + 1) * ML_DK] for h in heads]
        k = [k_ref[:, h * ML_DK:(h + 1) * ML_DK] for h in heads]
        v = [v_ref[:, h * ML_DV:(h + 1) * ML_DV].astype(BF16) for h in heads]
        cst = [c_ref[0, d, h] for h in heads]
        nst = [n_ref[0, d, h] for h in heads]
        m_rep = [m_ref[0, d, h] for h in heads]
        bc = [b_rep[:, h * LANES:(h + 1) * LANES] for h in heads]
        i_col = [i_rep[:, h * LANES:(h + 1) * LANES] for h in heads]
        qk_raw = [_dot_nt(q[h], k[h]) for h in heads]
        inter = [_dot_nt(q[h], jnp.concatenate([cst[h], jnp.broadcast_to(nst[h], (LANES, ML_DK))],
                                               axis=0).astype(BF16)) for h in heads]
        ji = d * 2 * ML_HEADS
        a_row = [gt_ref[ji + h:ji + h + 1, :] - b_row[ji + ML_HEADS + h:ji + ML_HEADS + h + 1, :] for h in heads]
        amat = [jnp.where(causal, a_row[h], -jnp.inf) for h in heads]
        u = [jnp.maximum(m_rep[h], jnp.broadcast_to(jnp.max(amat[h], axis=1, keepdims=True), (length, LANES)))
             for h in heads]
        qk = [(qk_raw[h] * jnp.exp(amat[h] - u[h])).astype(BF16) for h in heads]
        intra = [_dot(qk[h], jnp.concatenate([v[h], jnp.ones((length, LANES), BF16)], axis=1)) for h in heads]
        b_last = [bc[h][edge:edge + 1, :] for h in heads]
        wlog_row = [b_last[h] + a_row[h] for h in heads]
        m_new = [jnp.maximum(b_last[h] + m_rep[h], jnp.max(wlog_row[h], axis=1, keepdims=True)) for h in heads]
        ws_row = [jnp.exp(wlog_row[h] - m_new[h]) for h in heads]
        kw = [(jnp.exp(b_last[h] - bc[h] + i_col[h] - m_new[h]) * k[h].astype(F32)).astype(BF16) for h in heads]
        c_upd = [_dot_tn(v[h], kw[h]) for h in heads]
        n_upd = [_dot(jnp.broadcast_to(ws_row[h], (8, length)).astype(BF16), k[h])[0:1] for h in heads]
        for h in heads:
            sc = jnp.exp(m_rep[h] - u[h])
            tot = jnp.concatenate([sc] * (ML_DV // LANES + 1), axis=1) * inter[h] + intra[h]
            inv = 1.0 / jnp.maximum(jnp.abs(tot[:, ML_DV:]), jnp.exp(-(bc[h] + u[h])))
            h_out[d][:, h * ML_DV:(h + 1) * ML_DV] = tot[:, :ML_DV] * jnp.concatenate([inv] * (ML_DV // LANES), axis=1)
            decay = jnp.exp(b_last[h] + m_rep[h] - m_new[h])
            c_ref[0, d, h] = decay * cst[h] + c_upd[h]
            n_ref[0, d, h] = decay * nst[h] + n_upd[h]
            m_ref[0, d, h] = m_new[h]


def _ml_scan(qk, p, g, gt, state, n_seq, seq_len):
    rows = qk.shape[0]
    length = min(ML_CHUNK, seq_len)
    nc = seq_len // length
    qw = ML_HEADS * ML_DK
    vw = ML_HEADS * ML_DV
    ng = gt.shape[0]

    def fwd(b, c):
        return b * nc + c

    def bwd(b, c):
        return b * nc + nc - 1 - c

    args, specs = [], []
    for pos in (fwd, bwd):
        args += [qk, qk, p, g, gt]
        specs += [pl.BlockSpec((length, qw), lambda b, c, pos=pos: (pos(b, c), 0)),
                  pl.BlockSpec((length, qw), lambda b, c, pos=pos: (pos(b, c), 1)),
                  pl.BlockSpec((length, vw), lambda b, c, pos=pos: (pos(b, c), 0)),
                  pl.BlockSpec((length, LANES), lambda b, c, pos=pos: (pos(b, c), 0)),
                  pl.BlockSpec((ng, length), lambda b, c, pos=pos: (0, pos(b, c)))]
    assert length == LANES and ML_DK == LANES
    gate_lane = jnp.arange(LANES)[:, None]
    head = (jnp.arange(ML_HEADS * LANES) // LANES)[None, :]
    sel = jnp.stack([jnp.stack([gate_lane == (2 * d + kind) * ML_HEADS + head for kind in range(2)])
                     for d in range(2)]).astype(BF16)
    args.append(sel)
    specs.append(pl.BlockSpec(sel.shape, lambda b, c: (0, 0, 0, 0)))
    c_spec = pl.BlockSpec((1, 2, ML_HEADS, ML_DV, ML_DK), lambda b, c: (b, 0, 0, 0, 0))
    n_spec = pl.BlockSpec((1, 2, ML_HEADS, 1, ML_DK), lambda b, c: (b, 0, 0, 0, 0))
    zero_init = state is None
    if not zero_init:
        c0, n0, m0 = state
        args += [c0, n0.reshape(n_seq, 2, ML_HEADS, 1, ML_DK),
                 jnp.broadcast_to(m0[..., None, None], (n_seq, 2, ML_HEADS, 1, ML_DK))]
        specs += [c_spec, n_spec, n_spec]
    hf, hb, c_fin, n_fin, m_fin = pl.pallas_call(
        functools.partial(_ml_scan_kernel, zero_init=zero_init), grid=(n_seq, nc), in_specs=specs,
        out_specs=[pl.BlockSpec((length, vw), lambda b, c: (fwd(b, c), 0)),
                   pl.BlockSpec((length, vw), lambda b, c: (bwd(b, c), 0)), c_spec, n_spec, n_spec],
        out_shape=[jax.ShapeDtypeStruct((rows, vw), F32), jax.ShapeDtypeStruct((rows, vw), F32),
                   jax.ShapeDtypeStruct((n_seq, 2, ML_HEADS, ML_DV, ML_DK), F32),
                   jax.ShapeDtypeStruct((n_seq, 2, ML_HEADS, 1, ML_DK), F32),
                   jax.ShapeDtypeStruct((n_seq, 2, ML_HEADS, 1, ML_DK), F32)],
        compiler_params=_cparams("parallel", "arbitrary"), name="mlstm_scan",
    )(*args)
    return hf, hb, (c_fin, n_fin[:, :, :, 0, :], m_fin[:, :, :, 0, 0])


def _gla_scan_kernel(*refs, zero_init):
    refs = list(refs)
    dirs = [tuple(refs[0:4]), tuple(refs[4:8])]
    w2_ref, ba_ref = refs[8:10]
    refs = refs[10:]
    if not zero_init:
        s0_ref = refs.pop(0)
    of_ref, ob_ref, s_ref, st_scr, la_scr = refs
    o_out = (of_ref, ob_ref)
    c = pl.program_id(1)
    last = pl.num_programs(1) - 1
    kw = GLA_HEADS * GLA_DK
    n_sub = of_ref.shape[0] // GLA_SUB

    @pl.when(c == 0)
    def _():
        for d in range(2):
            for h in range(GLA_HEADS):
                st_scr[d, h] = jnp.zeros((GLA_DV, GLA_DK), F32) if zero_init else s0_ref[0, d, h].T

    for d in range(2):
        u = dirs[d][3][...].astype(BF16)
        z = _dot(u, w2_ref[:, d * kw:(d + 1) * kw]) + ba_ref[:, d * kw:(d + 1) * kw]
        la = _log_sigmoid(z) / GLA_TAU
        la_scr[d] = la
        totals = jnp.sum(la.reshape(n_sub, GLA_SUB, kw), axis=1)
        worst = jnp.min(totals) if d == 0 else jnp.minimum(worst, jnp.min(totals))
    decay_bounded = worst * LOG2E > -GLA_MAX_EXP2

    ti = lax.broadcasted_iota(jnp.int32, (GLA_SUB, GLA_SUB), 0)
    si = lax.broadcasted_iota(jnp.int32, (GLA_SUB, GLA_SUB), 1)
    s_lane = lax.broadcasted_iota(jnp.int32, (GLA_SUB, GLA_SUB), 1)

    def sub_chunk(j, carry, bounded):
        steps = [(d, t) for d in range(2) for t in range(GLA_STEPS)]
        chains = [(d, h, t) for d, t in steps for h in range(GLA_HEADS)]
        rows = {(d, t): pl.ds(pl.multiple_of(
            ((j * GLA_STEPS + t) if d == 0 else n_sub - 1 - (j * GLA_STEPS + t)) * GLA_SUB, GLA_SUB), GLA_SUB)
            for d, t in steps}
        causal = [ti >= si, ti <= si]
        edge = [GLA_SUB - 1, 0]
        bc_all = {(d, t): jnp.dot(jnp.where(causal[d], 1.0, 0.0).astype(F32), la_scr[d, rows[d, t], :], precision=HI,
                                  preferred_element_type=F32) * LOG2E for d, t in steps}
        q, k, v, bc2, b_last, k_dec, q_dec, a = {}, {}, {}, {}, {}, {}, {}, {}
        for c in chains:
            d, h, t = c
            q_ref, k_ref, v_ref, _ = dirs[d]
            bc2[c] = bc_all[d, t][:, h * GLA_DK:(h + 1) * GLA_DK]
            q[c] = q_ref[rows[d, t], h * GLA_DK:(h + 1) * GLA_DK] * GLA_DK ** -0.5
            k[c] = k_ref[rows[d, t], h * GLA_DK:(h + 1) * GLA_DK]
            v[c] = v_ref[rows[d, t], h * GLA_DV:(h + 1) * GLA_DV].astype(BF16)
            b_last[c] = bc2[c][edge[d]:edge[d] + 1, :]
            k_dec[c] = (k[c] * jnp.exp2(b_last[c] - bc2[c])).astype(BF16)
            q_dec[c] = q[c] * jnp.exp2(bc2[c])
        for c in chains:
            if bounded:
                a[c] = _dot_nt((q_dec[c] * jnp.exp2(-b_last[c])).astype(BF16), k_dec[c])
            else:
                acc = jnp.zeros((GLA_SUB, GLA_SUB), F32)
                for s in range(GLA_SUB):
                    decay = jnp.exp2(bc2[c] - bc2[c][s:s + 1, :])
                    col = jnp.sum(q[c] * (k[c][s:s + 1, :] * decay), axis=1, keepdims=True)
                    acc = jnp.where(s_lane == s, col, acc)
                a[c] = acc
        kv = {c: _dot_tn(v[c], k_dec[c]) for c in chains}
        intra = {c: _dot(jnp.where(causal[c[0]], a[c], 0.0).astype(BF16), v[c]) for c in chains}
        state = {(d, h): st_scr[d, h] for d in range(2) for h in range(GLA_HEADS)}
        for t in range(GLA_STEPS):
            now = [c for c in chains if c[2] == t]
            inter = {c: _dot_nt(q_dec[c].astype(BF16), state[c[0], c[1]].astype(BF16)) for c in now}
            for c in now:
                d, h, _ = c
                o_out[d][rows[d, t], h * GLA_DV:(h + 1) * GLA_DV] = intra[c] + inter[c]
                state[d, h] = jnp.exp2(b_last[c]) * state[d, h] + kv[c]
        for (d, h), st in state.items():
            st_scr[d, h] = st
        return carry

    @pl.when(decay_bounded)
    def _():
        lax.fori_loop(0, n_sub // GLA_STEPS, functools.partial(sub_chunk, bounded=True), 0)

    @pl.when(jnp.logical_not(decay_bounded))
    def _():
        lax.fori_loop(0, n_sub // GLA_STEPS, functools.partial(sub_chunk, bounded=False), 0)

    @pl.when(c == last)
    def _():
        for d in range(2):
            for h in range(GLA_HEADS):
                s_ref[0, d, h] = st_scr[d, h].T


def _gla_scan(p, u, w2, b_a, state, n_seq, seq_len):
    rows = p.shape[0]
    length = min(GLA_BLOCK, seq_len)
    nc = seq_len // length
    kw = GLA_HEADS * GLA_DK
    vw = GLA_HEADS * GLA_DV

    def fwd(b, c):
        return b * nc + c

    def bwd(b, c):
        return b * nc + nc - 1 - c

    args, specs = [], []
    for pos in (fwd, bwd):
        args += [p, p, p, u]
        specs += [pl.BlockSpec((length, kw), lambda b, c, pos=pos: (pos(b, c), 0)),
                  pl.BlockSpec((length, kw), lambda b, c, pos=pos: (pos(b, c), 1)),
                  pl.BlockSpec((length, vw), lambda b, c, pos=pos: (pos(b, c), 2 * kw // vw)),
                  pl.BlockSpec((length, LANES), lambda b, c, pos=pos: (pos(b, c), 0))]
    args += [w2, b_a]
    specs += [pl.BlockSpec(w2.shape, lambda b, c: (0, 0)), pl.BlockSpec(b_a.shape, lambda b, c: (0, 0))]
    s_spec = pl.BlockSpec((1, 2, GLA_HEADS, GLA_DK, GLA_DV), lambda b, c: (b, 0, 0, 0, 0))
    zero_init = state is None
    if not zero_init:
        args.append(state)
        specs.append(s_spec)
    return pl.pallas_call(
        functools.partial(_gla_scan_kernel, zero_init=zero_init), grid=(n_seq, nc), in_specs=specs,
        out_specs=[pl.BlockSpec((length, vw), lambda b, c: (fwd(b, c), 0)),
                   pl.BlockSpec((length, vw), lambda b, c: (bwd(b, c), 0)), s_spec],
        out_shape=[jax.ShapeDtypeStruct((rows, vw), F32), jax.ShapeDtypeStruct((rows, vw), F32),
                   jax.ShapeDtypeStruct((n_seq, 2, GLA_HEADS, GLA_DK, GLA_DV), F32)],
        scratch_shapes=[pltpu.VMEM((2, GLA_HEADS, GLA_DV, GLA_DK), F32), pltpu.VMEM((2, length, kw), F32)],
        compiler_params=_cparams("parallel", "arbitrary"), name="gla_scan",
    )(*args)


def kernel(x_prompt, x_sample, cache_k_0, cache_v_0, state_mlstm_C_1, state_mlstm_n_1, state_mlstm_m_1, state_gla_S_2, cache_k_3, cache_v_3, c, c_ctx, w_mod, b_mod, norm1_g, norm2_g, final_g, router_w, router_b, moe_wg, moe_wu, moe_wd, attn0_w_qkv, attn0_sink, attn0_w_o, mlstm1_w_in, mlstm1_b_gates, mlstm1_conv, mlstm1_norm_g, mlstm1_w_out, gla2_w_in, gla2_w_a1, gla2_w_a2, gla2_b_a, gla2_norm_g, gla2_w_out, attn3_w_qkv, attn3_sink, attn3_w_o):
    n_ctx, ctx_len, d = x_prompt.shape
    n_lat, lat_len, _ = x_sample.shape
    depth = w_mod.shape[0]

    cvec = jnp.concatenate([c_ctx[None, :], c, jnp.zeros((8 - 1 - n_lat, d), F32)], axis=0)
    mod = _modulation(cvec, w_mod, b_mod).reshape(depth, 8, 6, 1, d)

    def mods(layer, kind, latent):
        return mod[layer, 1:1 + n_lat, kind] if latent else mod[layer, 0:1, kind]

    rw_hi = router_w.T.astype(BF16)
    rw_lo = (router_w.T - rw_hi.astype(F32)).astype(BF16)
    rwt = jnp.concatenate([rw_hi, rw_lo], axis=0)
    rb = router_b.reshape(-1, 1)
    attn_w = {0: (attn0_w_qkv.astype(BF16), attn0_sink, attn0_w_o.astype(BF16), cache_k_0, cache_v_0),
              3: (attn3_w_qkv.astype(BF16), attn3_sink, attn3_w_o.astype(BF16), cache_k_3, cache_v_3)}
    ml_qw = ML_HEADS * ML_DK
    ml_vw = ML_HEADS * ML_DV
    ml_main = 2 * ml_qw + 2 * ml_vw
    ml_w_qk = mlstm1_w_in[:, :2 * ml_qw].astype(BF16)
    ml_w_vo = mlstm1_w_in[:, 2 * ml_qw:ml_main].astype(BF16)
    ml_w_gates = mlstm1_w_in[:, ml_main:]
    ml_w_out = mlstm1_w_out.astype(BF16)
    gla_kw = GLA_HEADS * GLA_DK
    gla_w_in = gla2_w_in.astype(BF16)
    gla_w_a1 = jnp.pad(jnp.concatenate([gla2_w_a1[0], gla2_w_a1[1]], axis=1),
                       ((0, 0), (0, LANES - 2 * GLA_RANK))).astype(BF16)
    gla_w2 = jnp.zeros((LANES, 2 * gla_kw), F32)
    gla_w2 = gla_w2.at[:GLA_RANK, :gla_kw].set(gla2_w_a2[0]).at[GLA_RANK:2 * GLA_RANK, gla_kw:].set(gla2_w_a2[1])
    gla_w2 = gla_w2.astype(BF16)
    gla_ba = gla2_b_a.reshape(1, 2 * gla_kw)
    gla_w_out = gla2_w_out.astype(BF16)

    new_state = []

    def mixer(layer, s, count0):
        latent, n_seq, seq_len, x, h = s["latent"], s["n_seq"], s["seq_len"], s["x"], s["h"]
        tail = (mods(layer, 2, latent), norm2_g[layer], mods(layer, 4, latent), mods(layer, 3, latent), rwt, rb,
                count0)
        kind = layer % 3
        if kind == 0:
            w_qkv, sink, w_o, ck, cv = attn_w[layer]
            qkv = _matmul(h, w_qkv)
            if latent:
                att = _attn_lat(qkv, ck, cv, sink, n_seq, seq_len)
            else:
                att = _attn_ctx(qkv, sink, n_seq, seq_len)
                qw = ATT_HEADS * HEAD_DIM
                kw = ATT_KV * HEAD_DIM
                new_state.append(qkv[:, qw:qw + kw].reshape(n_seq, seq_len, ATT_KV, HEAD_DIM))
                new_state.append(qkv[:, qw + kw:].reshape(n_seq, seq_len, ATT_KV, HEAD_DIM))
            return _proj("plain", (att,), w_o, x, *tail)
        if kind == 1:
            p = _matmul(h, ml_w_vo)
            g, gt = _ml_gates(h, ml_w_gates, mlstm1_b_gates)
            qk = _ml_qk(h, ml_w_qk, mlstm1_conv, seq_len)
            st = (state_mlstm_C_1, state_mlstm_n_1, state_mlstm_m_1) if latent else None
            hf, hb, fin = _ml_scan(qk, p, g, gt, st, n_seq, seq_len)
            if not latent:
                new_state.extend(fin)
            return _proj("mlstm", (hf, hb, p, 1, mlstm1_norm_g), ml_w_out, x, *tail)
        p = _matmul(h, gla_w_in)
        u = _matmul(h, gla_w_a1)
        of, ob, s_fin = _gla_scan(p, u, gla_w2, gla_ba, state_gla_S_2 if latent else None, n_seq, seq_len)
        if not latent:
            new_state.append(s_fin)
        gla_vw = GLA_HEADS * GLA_DV
        return _proj("gla", (of, ob, p, (2 * gla_kw + gla_vw) // gla_vw, gla2_norm_g), gla_w_out, x, *tail)

    streams = [dict(latent=False, n_seq=n_ctx, seq_len=ctx_len, x=x_prompt.reshape(n_ctx * ctx_len, d)),
               dict(latent=True, n_seq=n_lat, seq_len=lat_len, x=x_sample.reshape(n_lat * lat_len, d))]
    for s in streams:
        s["h"] = _rownorm(s["x"], norm1_g[0], mods(0, 1, s["latent"]), mods(0, 0, s["latent"]))
    for layer in range(depth):
        counts = jnp.zeros((N_EXPERTS, LANES), jnp.int32)
        for s in streams:
            s["x"], s["h2"], s["meta"], s["wcol"], counts = mixer(layer, s, counts)
        cnt = counts[:, 0]
        for s in streams:
            s["slots"] = _slots(s["meta"], cnt)
        xs, info = _dispatch([s["h2"] for s in streams], jnp.concatenate([s["slots"] for s in streams], axis=1), cnt)
        ys = _ffn(xs, info, moe_wg, moe_wu, moe_wd, layer)
        for s in streams:
            latent = s["latent"]
            gate2 = mods(layer, 5, latent)
            if layer + 1 < depth:
                s["x"], s["h"] = _combine(s["x"], ys, s["slots"], s["wcol"], gate2, norm1_g[layer + 1],
                                          mod=(mods(layer + 1, 1, latent), mods(layer + 1, 0, latent)), out_x=True,
                                          h_dtype=F32 if (layer + 1) % 3 == 1 else BF16)
            else:
                s["out"] = _combine(s["x"], ys, s["slots"], s["wcol"], gate2, final_g, h_dtype=F32)
    y_prompt = streams[0]["out"].reshape(n_ctx, ctx_len, d)
    y_sample = streams[1]["out"].reshape(n_lat, lat_len, d)
    return (y_prompt, y_sample, *new_state)
```

```python
import functools

import jax
import jax.numpy as jnp
from jax import lax
from jax.experimental import pallas as pl
from jax.experimental.pallas import tpu as pltpu

F32 = jnp.float32
BF16 = jnp.bfloat16
HI = lax.Precision.HIGHEST

EPS = 1e-6
LOG2E = 1.4426950408889634
GRID_W = 64
ATT_HEADS = 16
ATT_KV = 4
ATT_GROUP = ATT_HEADS // ATT_KV
HEAD_DIM = 64
WINDOW = 128
Q_BLOCK = 128
ROPE_BASE = 10000.0
ML_HEADS = 8
ML_DK = 128
ML_DV = 256
ML_CHUNK = 128
GLA_HEADS = 4
GLA_DK = 128
GLA_DV = 256
GLA_RANK = 16
GLA_TAU = 16.0
GLA_SUB = 16
GLA_BLOCK = 256
GLA_STEPS = 8
GLA_MAX_EXP2 = 80.0
N_EXPERTS = 16
N_GROUPS = 4
GROUP_SIZE = N_EXPERTS // N_GROUPS
LANES = 128
VMEM_LIMIT = 56 * 1024 * 1024


def _cparams(*sem):
    return pltpu.CompilerParams(dimension_semantics=sem, vmem_limit_bytes=VMEM_LIMIT)


def _dot(a, b):
    return jnp.dot(a, b, preferred_element_type=F32)


def _dot_nt(a, b, precision=None):
    return lax.dot_general(a, b, (((1,), (1,)), ((), ())), precision=precision, preferred_element_type=F32)


def _dot_tn(a, b):
    return lax.dot_general(a, b, (((0,), (0,)), ((), ())), preferred_element_type=F32)


def _sigmoid(x):
    return 1.0 / (1.0 + jnp.exp(-x))


def _silu(x):
    return x * _sigmoid(x)


def _log_sigmoid(x):
    return jnp.minimum(x, 0.0) - jnp.log(1.0 + jnp.exp(-jnp.abs(x)))


def _rms_rows(x, g):
    ms = jnp.mean(x * x, axis=-1, keepdims=True)
    return x * lax.rsqrt(ms + EPS) * g


def _mod_kernel(c_ref, w_ref, b_ref, o_ref):
    s = _silu(c_ref[...])
    o_ref[0] = _dot(s.astype(BF16), w_ref[0].astype(BF16)) + b_ref[0]


def _modulation(cvec, w_mod, b_mod):
    depth, d, n6 = w_mod.shape
    tn = 1536
    return pl.pallas_call(
        _mod_kernel,
        grid=(depth, n6 // tn),
        in_specs=[pl.BlockSpec((8, d), lambda l, j: (0, 0)),
                  pl.BlockSpec((1, d, tn), lambda l, j: (l, 0, j)),
                  pl.BlockSpec((1, 1, tn), lambda l, j: (l, 0, j))],
        out_specs=pl.BlockSpec((1, 8, tn), lambda l, j: (l, 0, j)),
        out_shape=jax.ShapeDtypeStruct((depth, 8, n6), F32),
        compiler_params=_cparams("parallel", "parallel"),
        name="adaln_modulation",
    )(cvec, w_mod, b_mod.reshape(depth, 1, n6))


def _route(h, rwt, rb, carry):
    tm = h.shape[0]
    h_hi = h.astype(BF16)
    h_lo = (h - h_hi.astype(F32)).astype(BF16)
    by_hi = _dot_nt(rwt, h_hi)
    logits = by_hi[:N_EXPERTS] + by_hi[N_EXPERTS:] + _dot_nt(rwt[:N_EXPERTS], h_lo)
    scores = _sigmoid(logits)
    sel = scores + rb
    expert = lax.broadcasted_iota(jnp.int32, sel.shape, 0)
    pos = expert % GROUP_SIZE
    grp = expert // GROUP_SIZE

    def mate(x, k):
        ahead = pltpu.roll(x, N_EXPERTS - k, axis=0)
        behind = pltpu.roll(x, GROUP_SIZE - k, axis=0)
        return jnp.where(pos + k < GROUP_SIZE, ahead, behind)

    beaten = jnp.zeros_like(sel)
    for k in range(1, GROUP_SIZE):
        other = mate(sel, k)
        other_first = (pos + k) % GROUP_SIZE < pos
        beaten = beaten + jnp.where(other_first, jnp.where(other >= sel, 1.0, 0.0), jnp.where(other > sel, 1.0, 0.0))
    top2 = jnp.where(beaten < 2.0, sel, 0.0)
    gscore = top2
    for k in range(1, GROUP_SIZE):
        gscore = gscore + mate(top2, k)
    lost = jnp.zeros_like(sel)
    for k in range(1, N_GROUPS):
        other = pltpu.roll(gscore, N_EXPERTS - GROUP_SIZE * k, axis=0)
        other_first = (grp + k) % N_GROUPS < grp
        lost = lost + jnp.where(other_first, jnp.where(other >= gscore, 1.0, 0.0),
                                jnp.where(other > gscore, 1.0, 0.0))
    picked = jnp.where(lost < 0.5, jnp.where(beaten < 2.0, 1.0, 0.0), 0.0)
    chosen = picked > 0.5
    weight = jnp.where(chosen, scores, 0.0)
    wsum = jnp.sum(weight, axis=0, keepdims=True)
    e_f = expert.astype(F32)
    e_a = jnp.min(jnp.where(chosen, e_f, float(N_EXPERTS)), axis=0, keepdims=True)
    e_b = jnp.max(jnp.where(chosen, e_f, -1.0), axis=0, keepdims=True)
    before = (lax.broadcasted_iota(jnp.int32, (tm, tm), 0) < lax.broadcasted_iota(jnp.int32, (tm, tm), 1))
    rank = _dot(picked.astype(BF16), jnp.where(before, 1.0, 0.0).astype(BF16)) + carry
    is_a = e_f == e_a
    is_b = e_f == e_b
    r_a = jnp.sum(jnp.where(is_a, rank, 0.0), axis=0, keepdims=True)
    r_b = jnp.sum(jnp.where(is_b, rank, 0.0), axis=0, keepdims=True)
    w_a = jnp.sum(jnp.where(is_a, weight, 0.0), axis=0, keepdims=True)
    w_b = jnp.sum(jnp.where(is_b, weight, 0.0), axis=0, keepdims=True)
    meta = jnp.concatenate([e_a, e_b, r_a, r_b, jnp.zeros((4, tm), F32)], axis=0).astype(jnp.int32)
    wcol = jnp.concatenate([w_a / wsum, w_b / wsum, jnp.zeros((LANES - 2, tm), F32)], axis=0).T
    return meta, wcol, carry + jnp.sum(picked, axis=1, keepdims=True)


def _norm_mod(x, g_ref, mod_refs):
    h = _rms_rows(x, g_ref[...])
    if mod_refs is not None:
        a_ref, s_ref = mod_refs
        h = h * (1.0 + a_ref[0]) + s_ref[0]
    return h


def _rownorm_kernel(x_ref, g_ref, a_ref, s_ref, h_ref):
    h_ref[...] = _norm_mod(x_ref[...], g_ref, (a_ref, s_ref)).astype(h_ref.dtype)


def _mod_spec(n_mod, rows, tm, d, n_prefetch=0):
    per = (rows // n_mod) // tm
    return pl.BlockSpec((1, 1, d), lambda i, *_: (i // per, 0, 0))


def _rownorm(x, g, scale, shift, *, tm=512):
    rows, d = x.shape
    row_spec = pl.BlockSpec((tm, d), lambda i: (i, 0))
    return pl.pallas_call(
        _rownorm_kernel, grid=(rows // tm,),
        in_specs=[row_spec, pl.BlockSpec((1, d), lambda i: (0, 0)), _mod_spec(scale.shape[0], rows, tm, d),
                  _mod_spec(shift.shape[0], rows, tm, d)],
        out_specs=row_spec, out_shape=jax.ShapeDtypeStruct((rows, d), BF16),
        compiler_params=_cparams("parallel"), name="rownorm",
    )(x, g.reshape(1, d), scale, shift)


def _mm_kernel(a_ref, w_ref, o_ref):
    o_ref[...] = _dot(a_ref[...].astype(BF16), w_ref[...]).astype(o_ref.dtype)


def _matmul(a, w, *, out_dtype=F32, tm=1024):
    m, k = a.shape
    n = w.shape[1]
    tn = next(t for t in (1024, 768, 512, LANES) if n % t == 0)
    return pl.pallas_call(
        _mm_kernel, grid=(m // tm, n // tn),
        in_specs=[pl.BlockSpec((tm, k), lambda i, j: (i, 0)), pl.BlockSpec((k, tn), lambda i, j: (0, j))],
        out_specs=pl.BlockSpec((tm, tn), lambda i, j: (i, j)),
        out_shape=jax.ShapeDtypeStruct((m, n), out_dtype),
        compiler_params=_cparams("parallel", "parallel"), name="matmul",
    )(a, w)


def _head_norm(x, g, n_heads, dv):
    outs = []
    for h in range(n_heads):
        xs = x[:, h * dv:(h + 1) * dv]
        ms = jnp.mean(xs * xs, axis=-1, keepdims=True)
        outs.append(xs * lax.rsqrt(ms + EPS) * g[:, h * dv:(h + 1) * dv])
    return jnp.concatenate(outs, axis=1)


def _proj_kernel(*refs, pre):
    refs = list(refs)
    if pre == "plain":
        a = refs.pop(0)[...]
    else:
        f_ref, b_ref, p_ref, hg_ref = refs.pop(0), refs.pop(0), refs.pop(0), refs.pop(0)
        hsum = f_ref[...] + b_ref[...]
        if pre == "mlstm":
            a = _sigmoid(p_ref[...]) * _head_norm(hsum, hg_ref[...], ML_HEADS, ML_DV)
        else:
            a = _head_norm(hsum, hg_ref[...], GLA_HEADS, GLA_DV) * _silu(p_ref[...])
        a = a.astype(BF16)
    w_ref, x_ref, gate_ref, g_ref, a_ref, s_ref, rwt_ref, rb_ref, count0_ref = refs[:9]
    xo_ref, h_ref, meta_ref, wcol_ref, count_ref, carry_ref = refs[9:]

    @pl.when(pl.program_id(0) == 0)
    def _():
        carry_ref[...] = count0_ref[...].astype(F32)

    x = x_ref[...] + gate_ref[0] * _dot(a, w_ref[...])
    xo_ref[...] = x
    h = _norm_mod(x, g_ref, (a_ref, s_ref))
    _rows_to_tiles(h_ref, h)
    meta, wcol, carry = _route(h, rwt_ref[...], rb_ref[...], carry_ref[:, 0:1])
    meta_ref[...] = meta
    wcol_ref[...] = wcol
    carry_ref[...] = jnp.broadcast_to(carry, carry_ref.shape)
    count_ref[...] = jnp.broadcast_to(carry, count_ref.shape).astype(jnp.int32)


def _proj(pre, pre_args, w_out, x, gate, g, scale, shift, rwt, rb, count0, *, tm=512):
    rows, d = x.shape
    k = w_out.shape[0]
    row_spec = pl.BlockSpec((tm, d), lambda i: (i, 0))
    if pre == "plain":
        args, specs = [pre_args[0]], [pl.BlockSpec((tm, k), lambda i: (i, 0))]
    else:
        hf, hb, p, col_block, hg = pre_args
        wide = pl.BlockSpec((tm, k), lambda i: (i, 0))
        args = [hf, hb, p, hg.reshape(1, k)]
        specs = [wide, wide, pl.BlockSpec((tm, k), lambda i: (i, col_block)), pl.BlockSpec((1, k), lambda i: (0, 0))]
    args += [w_out, x, gate, g.reshape(1, d), scale, shift, rwt, rb, count0]
    specs += [pl.BlockSpec((k, d), lambda i: (0, 0)), row_spec, _mod_spec(gate.shape[0], rows, tm, d),
              pl.BlockSpec((1, d), lambda i: (0, 0)), _mod_spec(scale.shape[0], rows, tm, d),
              _mod_spec(shift.shape[0], rows, tm, d), pl.BlockSpec(rwt.shape, lambda i: (0, 0)),
              pl.BlockSpec(rb.shape, lambda i: (0, 0)), pl.BlockSpec(count0.shape, lambda i: (0, 0))]
    return pl.pallas_call(
        functools.partial(_proj_kernel, pre=pre), grid=(rows // tm,), in_specs=specs,
        out_specs=[row_spec, pl.BlockSpec((tm * SUBLANES, LANES), lambda i: (i, 0)),
                   pl.BlockSpec((8, tm), lambda i: (0, i)),
                   pl.BlockSpec((tm, LANES), lambda i: (i, 0)), pl.BlockSpec((N_EXPERTS, LANES), lambda i: (0, 0))],
        out_shape=[jax.ShapeDtypeStruct((rows, d), F32), jax.ShapeDtypeStruct((rows * SUBLANES, LANES), F32),
                   jax.ShapeDtypeStruct((8, rows), jnp.int32), jax.ShapeDtypeStruct((rows, LANES), F32),
                   jax.ShapeDtypeStruct((N_EXPERTS, LANES), jnp.int32)],
        scratch_shapes=[pltpu.VMEM((N_EXPERTS, LANES), F32)],
        compiler_params=_cparams("arbitrary"), name="proj_" + pre,
    )(*args)


MOE_TILE = 512
MOE_TILE_SHIFT = 9
MOE_TOKENS = 256
FFN_PARTS = 2
ROW_UNROLL = 8


SUBLANES = 8


def _rows_to_tiles(ref, x, lead=(), first=0):
    rows = x.shape[0]
    for c in range(SUBLANES):
        ref[(*lead, pl.ds(first * SUBLANES + c, rows, stride=SUBLANES), slice(None))] = x[:, c * LANES:(c + 1) * LANES]


def _tiles_to_rows(ref, rows, lead=(), first=0):
    return jnp.concatenate([ref[(*lead, pl.ds(first * SUBLANES + c, rows, stride=SUBLANES), slice(None))]
                            for c in range(SUBLANES)], axis=1)


def _slot_tiles(rows):
    return (2 * rows) // MOE_TILE + N_EXPERTS


def _expert_offsets(cnt_ref, off_ref):
    def per_expert(e, k):
        off_ref[e] = k * MOE_TILE
        return k + ((cnt_ref[e] + MOE_TILE - 1) >> MOE_TILE_SHIFT)
    return lax.fori_loop(0, N_EXPERTS, per_expert, 0)


def _slots_kernel(cnt_ref, meta_ref, slot_ref, off_ref):
    @pl.when(pl.program_id(0) == 0)
    def _():
        _expert_offsets(cnt_ref, off_ref)

    e_a, e_b = meta_ref[0:1, :], meta_ref[1:2, :]
    off_a = jnp.zeros_like(e_a)
    off_b = jnp.zeros_like(e_b)
    for e in range(N_EXPERTS):
        off_a = jnp.where(e_a == e, off_ref[e], off_a)
        off_b = jnp.where(e_b == e, off_ref[e], off_b)
    slot_ref[...] = jnp.concatenate([off_a + meta_ref[2:3, :], off_b + meta_ref[3:4, :],
                                     jnp.zeros((6, e_a.shape[1]), jnp.int32)], axis=0)


def _slots(meta, counts, *, tm=1024):
    rows = meta.shape[1]
    grid_spec = pltpu.PrefetchScalarGridSpec(
        num_scalar_prefetch=1, grid=(rows // tm,),
        in_specs=[pl.BlockSpec((8, tm), lambda i, cnt: (0, i))],
        out_specs=pl.BlockSpec((8, tm), lambda i, cnt: (0, i)),
        scratch_shapes=[pltpu.SMEM((N_EXPERTS,), jnp.int32)])
    return pl.pallas_call(
        _slots_kernel, grid_spec=grid_spec, out_shape=jax.ShapeDtypeStruct((8, rows), jnp.int32),
        compiler_params=_cparams("arbitrary"), name="moe_slots",
    )(counts, meta)


def _dispatch_kernel(*refs, steps):
    sa_ref, sb_ref, cnt_ref = refs[:3]
    h_refs = refs[3:3 + len(steps)]
    xs_ref, info_ref, off_ref, zero_ref, sem = refs[3 + len(steps):]
    i = pl.program_id(0)
    tm = h_refs[0].shape[0] // SUBLANES
    n_tiles = info_ref.shape[0] - 1
    tile_rows = MOE_TILE * SUBLANES

    def tile_copy(tile):
        return pltpu.make_async_copy(zero_ref, xs_ref.at[pl.ds(tile * tile_rows, tile_rows), :], sem)

    @pl.when(i == 0)
    def _():
        zero_ref[...] = jnp.zeros_like(zero_ref)
        used = _expert_offsets(cnt_ref, off_ref)

        def per_expert(e, _):
            first = off_ref[e] >> MOE_TILE_SHIFT
            nt = (cnt_ref[e] + MOE_TILE - 1) >> MOE_TILE_SHIFT

            def fill(j, _):
                info_ref[first + j] = e
                return 0
            lax.fori_loop(0, nt, fill, 0)

            @pl.when(nt > 0)
            def _():
                tile_copy(first + nt - 1).start()
                tile_copy(first + nt - 1).wait()
            return 0
        lax.fori_loop(0, N_EXPERTS, per_expert, 0)
        info_ref[n_tiles] = used

        def tail(j, _):
            info_ref[j] = N_EXPERTS - 1
            tile_copy(j).start()
            tile_copy(j).wait()
            return 0
        lax.fori_loop(used, n_tiles, tail, 0)

    base = i * tm

    def copy_rows(h_ref):
        def row_copy(t, slot):
            dst = pl.multiple_of(slot * SUBLANES, SUBLANES)
            return pltpu.make_async_copy(h_ref.at[pl.ds(t * SUBLANES, SUBLANES), :],
                                         xs_ref.at[pl.ds(dst, SUBLANES), :], sem)

        for t in range(tm):
            row_copy(t, sa_ref[base + t]).start(priority=0)
            row_copy(t, sb_ref[base + t]).start(priority=1)
        for _ in range(2):
            pltpu.make_async_copy(h_ref, xs_ref.at[pl.ds(0, tm * SUBLANES), :], sem).wait()

    first = 0
    for h_ref, n in zip(h_refs, steps):
        pl.when(jnp.logical_and(i >= first, i < first + n))(functools.partial(copy_rows, h_ref))
        first += n


def _dispatch(hs, slots, counts):
    tm = MOE_TOKENS
    steps = tuple(h.shape[0] // (tm * SUBLANES) for h in hs)
    n_tiles = _slot_tiles(sum(steps) * tm)
    specs, first = [], 0
    for n in steps:
        specs.append(pl.BlockSpec((tm * SUBLANES, LANES),
                                  lambda i, *_, first=first, n=n: (jnp.clip(i - first, 0, n - 1), 0)))
        first += n
    grid_spec = pltpu.PrefetchScalarGridSpec(
        num_scalar_prefetch=3, grid=(sum(steps),), in_specs=specs,
        out_specs=[pl.BlockSpec(memory_space=pl.ANY), pl.BlockSpec(memory_space=pltpu.SMEM)],
        scratch_shapes=[pltpu.SMEM((N_EXPERTS,), jnp.int32), pltpu.VMEM((MOE_TILE * SUBLANES, LANES), F32),
                        pltpu.SemaphoreType.DMA(())])
    return pl.pallas_call(
        functools.partial(_dispatch_kernel, steps=steps), grid_spec=grid_spec,
        out_shape=[jax.ShapeDtypeStruct((n_tiles * MOE_TILE * SUBLANES, LANES), F32),
                   jax.ShapeDtypeStruct((n_tiles + 1,), jnp.int32)],
        compiler_params=_cparams("arbitrary"), name="moe_dispatch",
    )(slots[0], slots[1], counts, *hs)


def _ffn_kernel(info_ref, xs_ref, wg_ref, wu_ref, wd_ref, ys_ref, wg_s, wu_s, wd_s):
    i = pl.program_id(0)
    used = info_ref[info_ref.shape[0] - 1]
    fresh = jnp.logical_or(i == 0, info_ref[i] != info_ref[jnp.maximum(i - 1, 0)])

    @pl.when(jnp.logical_and(i < used, fresh))
    def _():
        wg_s[...] = wg_ref[0, 0].astype(BF16)
        wu_s[...] = wu_ref[0, 0].astype(BF16)
        wd_s[...] = wd_ref[0, 0].astype(BF16)

    @pl.when(i < used)
    def _():
        part = MOE_TILE // FFN_PARTS
        x = [_tiles_to_rows(xs_ref, part, first=p * part).astype(BF16) for p in range(FFN_PARTS)]
        gate = [_dot(xp, wg_s[...]) for xp in x]
        up = [_dot(xp, wu_s[...]) for xp in x]
        hid = [(_silu(g) * u).astype(BF16) for g, u in zip(gate, up)]
        y = [_dot(hp, wd_s[...]) for hp in hid]
        for p in range(FFN_PARTS):
            _rows_to_tiles(ys_ref, y[p], first=p * part)

    @pl.when(i >= used)
    def _():
        ys_ref[...] = jnp.zeros_like(ys_ref)


def _ffn(xs, info, wg, wu, wd, layer):
    tile_rows = MOE_TILE * SUBLANES
    n_tiles = xs.shape[0] // tile_rows
    d, f = wg.shape[2:]

    def w_map(i, info):
        return (layer, info[i], 0, 0)

    grid_spec = pltpu.PrefetchScalarGridSpec(
        num_scalar_prefetch=1, grid=(n_tiles,),
        in_specs=[pl.BlockSpec((tile_rows, LANES), lambda i, info: (jnp.minimum(i, info[n_tiles] - 1), 0)),
                  pl.BlockSpec((1, 1, d, f), w_map), pl.BlockSpec((1, 1, d, f), w_map),
                  pl.BlockSpec((1, 1, f, d), w_map)],
        out_specs=pl.BlockSpec((tile_rows, LANES), lambda i, info: (i, 0)),
        scratch_shapes=[pltpu.VMEM((d, f), BF16), pltpu.VMEM((d, f), BF16), pltpu.VMEM((f, d), BF16)])
    return pl.pallas_call(
        _ffn_kernel, grid_spec=grid_spec, out_shape=jax.ShapeDtypeStruct(xs.shape, F32),
        compiler_params=_cparams("arbitrary"), name="moe_ffn",
    )(info, xs, wg, wu, wd)


def _combine_kernel(*refs, has_mod, out_x):
    refs = list(refs)
    sa_ref, sb_ref, x_ref, ys_ref, wcol_ref, gate_ref, g_ref = refs[:7]
    refs = refs[7:]
    mod_refs = (refs.pop(0), refs.pop(0)) if has_mod else None
    xo_ref = refs.pop(0) if out_x else None
    h_ref, buf_a, buf_b, sems = refs
    i = pl.program_id(0)
    tm = x_ref.shape[0]

    def issue(tile, slot):
        base = tile * tm
        for t in range(tm):
            dst = pl.ds(t * SUBLANES, SUBLANES)
            src_a = pl.multiple_of(sa_ref[base + t] * SUBLANES, SUBLANES)
            src_b = pl.multiple_of(sb_ref[base + t] * SUBLANES, SUBLANES)
            pltpu.make_async_copy(ys_ref.at[pl.ds(src_a, SUBLANES), :], buf_a.at[slot, dst, :],
                                  sems.at[slot]).start(priority=0)
            pltpu.make_async_copy(ys_ref.at[pl.ds(src_b, SUBLANES), :], buf_b.at[slot, dst, :],
                                  sems.at[slot]).start(priority=1)

    @pl.when(i == 0)
    def _():
        issue(0, 0)

    @pl.when(i + 1 < pl.num_programs(0))
    def _():
        issue(i + 1, (i + 1) % 2)

    slot = i % 2
    for buf in (buf_a, buf_b):
        pltpu.make_async_copy(ys_ref.at[pl.ds(0, tm * SUBLANES), :], buf.at[slot], sems.at[slot]).wait()
    y = (wcol_ref[:, 0:1] * _tiles_to_rows(buf_a, tm, lead=(slot,))
         + wcol_ref[:, 1:2] * _tiles_to_rows(buf_b, tm, lead=(slot,)))
    x = x_ref[...] + gate_ref[0] * y
    if out_x:
        xo_ref[...] = x
    h_ref[...] = _norm_mod(x, g_ref, mod_refs).astype(h_ref.dtype)


def _combine(x, ys, slots, wcol, gate, g, *, mod=None, out_x=False, h_dtype=BF16):
    rows, d = x.shape
    tm = MOE_TOKENS
    row_spec = pl.BlockSpec((tm, d), lambda i, *_: (i, 0))
    args = [x, ys, wcol, gate, g.reshape(1, d)]
    specs = [row_spec, pl.BlockSpec(memory_space=pl.ANY), pl.BlockSpec((tm, LANES), lambda i, *_: (i, 0)),
             _mod_spec(gate.shape[0], rows, tm, d), pl.BlockSpec((1, d), lambda i, *_: (0, 0))]
    if mod is not None:
        for m in mod:
            args.append(m)
            specs.append(_mod_spec(m.shape[0], rows, tm, d))
    out_shape, out_specs = [], []
    if out_x:
        out_shape.append(jax.ShapeDtypeStruct((rows, d), F32))
        out_specs.append(row_spec)
    out_shape.append(jax.ShapeDtypeStruct((rows, d), h_dtype))
    out_specs.append(row_spec)
    grid_spec = pltpu.PrefetchScalarGridSpec(
        num_scalar_prefetch=2, grid=(rows // tm,), in_specs=specs, out_specs=out_specs,
        scratch_shapes=[pltpu.VMEM((2, tm * SUBLANES, LANES), F32), pltpu.VMEM((2, tm * SUBLANES, LANES), F32),
                        pltpu.SemaphoreType.DMA((2,))])
    outs = pl.pallas_call(
        functools.partial(_combine_kernel, has_mod=mod is not None, out_x=out_x), grid_spec=grid_spec,
        out_shape=out_shape, compiler_params=_cparams("arbitrary"), name="moe_combine",
    )(slots[0], slots[1], *args)
    return outs if out_x else outs[0]


def _softmax_av(groups):
    maxes = []
    for scores, _, sink in groups:
        m = sink
        for s in scores:
            m = jnp.maximum(m, jnp.broadcast_to(jnp.max(s, axis=-1, keepdims=True), sink.shape))
        maxes.append(m)
    probs = [[jnp.exp(s - jnp.concatenate([m] * (s.shape[1] // LANES), axis=1)).astype(BF16) for s in scores]
             for (scores, _, _), m in zip(groups, maxes)]
    outs = []
    for (_, values, sink), m, ps in zip(groups, maxes, probs):
        hd = values[0].shape[1]
        tot = None
        for p, v in zip(ps, values):
            n = v.shape[0]
            v_ext = jnp.concatenate([v, jnp.zeros((n, LANES - hd), BF16), jnp.ones((n, LANES), BF16)], axis=1)
            pv = _dot(p, v_ext)
            tot = pv if tot is None else tot + pv
        outs.append((tot, sink, m, hd))
    return [tot[:, :hd] / (tot[:, LANES:] + jnp.exp(sink - m))[:, :hd] for tot, sink, m, hd in outs]


def _sink_column(sink_ref, kv, rows):
    return jnp.concatenate([jnp.full((rows, LANES), sink_ref[kv * ATT_GROUP + g], F32) for g in range(ATT_GROUP)],
                           axis=0)


def _attn_ctx_kernel(sink_ref, qkv_ref, o_ref):
    t = qkv_ref.shape[0]
    qw = ATT_HEADS * HEAD_DIM
    kw = ATT_KV * HEAD_DIM
    groups = []
    for kv in range(ATT_KV):
        q = jnp.concatenate(
            [qkv_ref[:, (kv * ATT_GROUP + g) * HEAD_DIM:(kv * ATT_GROUP + g + 1) * HEAD_DIM] for g in range(ATT_GROUP)],
            axis=0).astype(BF16)
        k = qkv_ref[:, qw + kv * HEAD_DIM:qw + (kv + 1) * HEAD_DIM].astype(BF16)
        v = qkv_ref[:, qw + kw + kv * HEAD_DIM:qw + kw + (kv + 1) * HEAD_DIM].astype(BF16)
        groups.append(([_dot_nt(q, k) * HEAD_DIM ** -0.5], [v], _sink_column(sink_ref, kv, t)))
    heads_out = [o[g * t:(g + 1) * t] for o in _softmax_av(groups) for g in range(ATT_GROUP)]
    o_ref[...] = jnp.concatenate(heads_out, axis=1).astype(o_ref.dtype)


def _attn_ctx(qkv, sink, n_seq, seq_len):
    rows, cols = qkv.shape
    return pl.pallas_call(
        _attn_ctx_kernel, grid=(n_seq,),
        in_specs=[pl.BlockSpec(memory_space=pltpu.SMEM), pl.BlockSpec((seq_len, cols), lambda b: (b, 0))],
        out_specs=pl.BlockSpec((seq_len, ATT_HEADS * HEAD_DIM), lambda b: (b, 0)),
        out_shape=jax.ShapeDtypeStruct((rows, ATT_HEADS * HEAD_DIM), BF16),
        compiler_params=_cparams("parallel"), name="attn_context",
    )(sink, qkv)


def _rope_block(x, cos, sin_signed):
    lane = lax.broadcasted_iota(jnp.int32, x.shape, 1)
    nf = HEAD_DIM // 4
    partner = jnp.where((lane % (2 * nf)) < nf, pltpu.roll(x, LANES - nf, axis=1), pltpu.roll(x, nf, axis=1))
    return x * cos + partner * sin_signed


def _attn_lat_kernel(sink_ref, qkv_ref, ck_ref, cv_ref, cos_ref, sin_ref, o_ref, k_scr, v_scr, ck_scr, cv_scr):
    i = pl.program_id(1)
    t = qkv_ref.shape[0]
    qw = ATT_HEADS * HEAD_DIM
    kw = ATT_KV * HEAD_DIM
    span = Q_BLOCK + 2 * WINDOW

    @pl.when(i == 0)
    def _():
        for c in range(kw // LANES):
            blk = qkv_ref[:, qw + c * LANES:qw + (c + 1) * LANES]
            k_scr[:, c * LANES:(c + 1) * LANES] = _rope_block(blk, cos_ref[...], sin_ref[...]).astype(BF16)
        v_scr[...] = qkv_ref[:, qw + kw:qw + 2 * kw].astype(BF16)
        for kv in range(ATT_KV):
            ck_scr[:, kv * HEAD_DIM:(kv + 1) * HEAD_DIM] = ck_ref[0, :, kv, :].astype(BF16)
            cv_scr[:, kv * HEAD_DIM:(kv + 1) * HEAD_DIM] = cv_ref[0, :, kv, :].astype(BF16)

    r0 = pl.multiple_of(i * Q_BLOCK, Q_BLOCK)
    ws = pl.multiple_of(jnp.clip(r0 - WINDOW, 0, t - span), Q_BLOCK)
    cos_q = cos_ref[pl.ds(r0, Q_BLOCK), :]
    sin_q = sin_ref[pl.ds(r0, Q_BLOCK), :]
    qpos = r0 + lax.broadcasted_iota(jnp.int32, (Q_BLOCK, span), 0)
    kpos = ws + lax.broadcasted_iota(jnp.int32, (Q_BLOCK, span), 1)
    band = jnp.abs(qpos - kpos) <= WINDOW
    band = jnp.concatenate([band] * ATT_GROUP, axis=0)
    roped = [_rope_block(qkv_ref[pl.ds(r0, Q_BLOCK), c * LANES:(c + 1) * LANES], cos_q, sin_q)
             for c in range(qw // LANES)]
    groups = []
    for kv in range(ATT_KV):
        heads = []
        for g in range(ATT_GROUP):
            c, half = divmod((kv * ATT_GROUP + g) * HEAD_DIM, LANES)
            heads.append(roped[c][:, half:half + HEAD_DIM])
        q = jnp.concatenate(heads, axis=0).astype(BF16)
        ck = ck_scr[:, kv * HEAD_DIM:(kv + 1) * HEAD_DIM]
        cv = cv_scr[:, kv * HEAD_DIM:(kv + 1) * HEAD_DIM]
        kwin = k_scr[pl.ds(ws, span), kv * HEAD_DIM:(kv + 1) * HEAD_DIM]
        vwin = v_scr[pl.ds(ws, span), kv * HEAD_DIM:(kv + 1) * HEAD_DIM]
        s_ctx = _dot_nt(q, ck) * HEAD_DIM ** -0.5
        s_win = jnp.where(band, _dot_nt(q, kwin) * HEAD_DIM ** -0.5, -jnp.inf)
        groups.append(([s_ctx, s_win], [cv, vwin], _sink_column(sink_ref, kv, Q_BLOCK)))
    heads_out = [o[g * Q_BLOCK:(g + 1) * Q_BLOCK] for o in _softmax_av(groups) for g in range(ATT_GROUP)]
    o_ref[...] = jnp.concatenate(heads_out, axis=1).astype(o_ref.dtype)


def _rope_tables(seq_len):
    pos = jnp.arange(seq_len, dtype=jnp.int32)
    row = (pos // GRID_W).astype(F32)
    col = (pos % GRID_W).astype(F32)
    nf = HEAD_DIM // 4
    inv = ROPE_BASE ** (-jnp.arange(nf, dtype=F32) / nf)
    ang_r = row[:, None] * inv[None, :]
    ang_c = col[:, None] * inv[None, :]
    cos_h = jnp.concatenate([jnp.cos(ang_r), jnp.cos(ang_r), jnp.cos(ang_c), jnp.cos(ang_c)], axis=1)
    sin_h = jnp.concatenate([-jnp.sin(ang_r), jnp.sin(ang_r), -jnp.sin(ang_c), jnp.sin(ang_c)], axis=1)
    reps = LANES // HEAD_DIM
    return jnp.tile(cos_h, (1, reps)), jnp.tile(sin_h, (1, reps))


def _attn_lat(qkv, cache_k, cache_v, sink, n_seq, seq_len):
    rows, cols = qkv.shape
    past = cache_k.shape[1]
    kw = ATT_KV * HEAD_DIM
    cos, sin = _rope_tables(seq_len)
    return pl.pallas_call(
        _attn_lat_kernel, grid=(n_seq, seq_len // Q_BLOCK),
        in_specs=[pl.BlockSpec(memory_space=pltpu.SMEM),
                  pl.BlockSpec((seq_len, cols), lambda b, i: (b, 0)),
                  pl.BlockSpec((1, past, ATT_KV, HEAD_DIM), lambda b, i: (b, 0, 0, 0)),
                  pl.BlockSpec((1, past, ATT_KV, HEAD_DIM), lambda b, i: (b, 0, 0, 0)),
                  pl.BlockSpec((seq_len, LANES), lambda b, i: (0, 0)),
                  pl.BlockSpec((seq_len, LANES), lambda b, i: (0, 0))],
        out_specs=pl.BlockSpec((Q_BLOCK, ATT_HEADS * HEAD_DIM), lambda b, i: (b * (seq_len // Q_BLOCK) + i, 0)),
        out_shape=jax.ShapeDtypeStruct((rows, ATT_HEADS * HEAD_DIM), BF16),
        scratch_shapes=[pltpu.VMEM((seq_len, kw), BF16), pltpu.VMEM((seq_len, kw), BF16),
                        pltpu.VMEM((past, kw), BF16), pltpu.VMEM((past, kw), BF16)],
        compiler_params=_cparams("parallel", "arbitrary"), name="attn_latent",
    )(sink, qkv, cache_k, cache_v, cos, sin)


def _split_bf16(w):
    hi = w.astype(BF16)
    return hi, (w - hi.astype(F32)).astype(BF16)


def _select_lanes(x, sel):
    x1 = x.astype(BF16)
    rest = x - x1.astype(F32)
    x2 = rest.astype(BF16)
    x3 = (rest - x2.astype(F32)).astype(BF16)
    return _dot(x1, sel) + _dot(x2, sel) + _dot(x3, sel)


def _ml_gates_kernel(h_ref, w_ref, wt_ref, b_ref, bt_ref, g_ref, gt_ref):
    h = h_ref[...]
    h_hi = h.astype(BF16)
    h_lo = (h - h_hi.astype(F32)).astype(BF16)
    ng = gt_ref.shape[0]
    by_hi = _dot(h_hi, w_ref[...])
    g_ref[...] = by_hi[:, :LANES] + by_hi[:, LANES:] + _dot(h_lo, w_ref[:, :LANES]) + b_ref[...]
    by_hi_t = _dot_nt(wt_ref[...], h_hi)
    gt_ref[...] = by_hi_t[:ng] + by_hi_t[ng:] + _dot_nt(wt_ref[:ng, :], h_lo) + bt_ref[...]


def _ml_gates(h, w_gates, b_gates, *, tm=512):
    rows, d = h.shape
    ng = w_gates.shape[1]
    w_hi, w_lo = _split_bf16(jnp.pad(w_gates, ((0, 0), (0, LANES - ng))))
    wt_hi, wt_lo = _split_bf16(w_gates.T)
    b_pad = jnp.pad(b_gates, (0, LANES - ng)).reshape(1, LANES)
    return pl.pallas_call(
        _ml_gates_kernel, grid=(rows // tm,),
        in_specs=[pl.BlockSpec((tm, d), lambda i: (i, 0)), pl.BlockSpec((d, 2 * LANES), lambda i: (0, 0)),
                  pl.BlockSpec((2 * ng, d), lambda i: (0, 0)), pl.BlockSpec((1, LANES), lambda i: (0, 0)),
                  pl.BlockSpec((ng, 1), lambda i: (0, 0))],
        out_specs=[pl.BlockSpec((tm, LANES), lambda i: (i, 0)), pl.BlockSpec((ng, tm), lambda i: (0, i))],
        out_shape=[jax.ShapeDtypeStruct((rows, LANES), F32), jax.ShapeDtypeStruct((ng, rows), F32)],
        compiler_params=_cparams("parallel"), name="mlstm_gates",
    )(h, jnp.concatenate([w_hi, w_lo], axis=1), jnp.concatenate([wt_hi, wt_lo], axis=0), b_pad,
      b_gates.reshape(ng, 1))


def _ml_qk_kernel(h_ref, w_ref, cw_ref, o_ref, *, seq_len, k_scale):
    j = pl.program_id(1)
    x = _dot(h_ref[...].astype(BF16), w_ref[...])
    t = x.shape[0]
    pos = lax.broadcasted_iota(jnp.int32, x.shape, 0) % seq_len
    prev = jnp.where(pos == 0, 0.0, pltpu.roll(x, 1, axis=0))
    nxt = jnp.where(pos == seq_len - 1, 0.0, pltpu.roll(x, t - 1, axis=0))
    y = prev * cw_ref[0:1, :] + x * cw_ref[1:2, :] + nxt * cw_ref[2:3, :]
    scale = jnp.where(j >= pl.num_programs(1) // 2, k_scale, 1.0).astype(F32)
    o_ref[...] = (_silu(y) * scale).astype(o_ref.dtype)


def _ml_qk(h, w_qk, conv_w, seq_len, *, tm=1024, tn=1024):
    rows, d = h.shape
    width = w_qk.shape[1]
    return pl.pallas_call(
        functools.partial(_ml_qk_kernel, seq_len=seq_len, k_scale=ML_DK ** -0.5), grid=(rows // tm, width // tn),
        in_specs=[pl.BlockSpec((tm, d), lambda i, j: (i, 0)), pl.BlockSpec((d, tn), lambda i, j: (0, j)),
                  pl.BlockSpec((3, tn), lambda i, j: (0, j))],
        out_specs=pl.BlockSpec((tm, tn), lambda i, j: (i, j)),
        out_shape=jax.ShapeDtypeStruct((rows, width), BF16),
        compiler_params=_cparams("parallel", "parallel"), name="mlstm_qk",
    )(h, w_qk, conv_w)


def _ml_scan_kernel(*refs, zero_init):
    refs = list(refs)
    dirs = [tuple(refs[0:5]), tuple(refs[5:10])]
    sel_ref = refs[10]
    refs = refs[11:]
    if not zero_init:
        c0_ref, n0_ref, m0_ref = refs[:3]
        refs = refs[3:]
    hf_ref, hb_ref, c_ref, n_ref, m_ref = refs
    h_out = (hf_ref, hb_ref)
    c = pl.program_id(1)
    last = pl.num_programs(1) - 1

    @pl.when(c == 0)
    def _():
        if zero_init:
            c_ref[...] = jnp.zeros_like(c_ref)
            n_ref[...] = jnp.zeros_like(n_ref)
            m_ref[...] = jnp.zeros_like(m_ref)
        else:
            c_ref[...] = c0_ref[...]
            n_ref[...] = n0_ref[...]
            m_ref[...] = m0_ref[...]

    length = hf_ref.shape[0]
    ti = lax.broadcasted_iota(jnp.int32, (length, length), 0)
    si = lax.broadcasted_iota(jnp.int32, (length, length), 1)
    for d in range(2):
        q_ref, k_ref, v_ref, g_ref, gt_ref = dirs[d]
        causal = (ti >= si) if d == 0 else (ti <= si)
        tri = jnp.where(causal, 1.0, 0.0).astype(F32)
        g_col = g_ref[...]
        f_row = _log_sigmoid(gt_ref[...])
        b_col = jnp.dot(tri, _log_sigmoid(g_col), precision=HI, preferred_element_type=F32)
        b_row = _dot_nt(f_row, tri, precision=HI)
        i_rep = _select_lanes(g_col, sel_ref[d, 0])
        b_rep = _select_lanes(b_col, sel_ref[d, 1])
        edge = length - 1 if d == 0 else 0
        heads = range(ML_HEADS)
        q = [q_ref[:, h * ML_DK:(h + 1) * ML_DK] for h in heads]
        k = [k_ref[:, h * ML_DK:(h + 1) * ML_DK] for h in heads]
        v = [v_ref[:, h * ML_DV:(h + 1) * ML_DV].astype(BF16) for h in heads]
        cst = [c_ref[0, d, h] for h in heads]
        nst = [n_ref[0, d, h] for h in heads]
        m_rep = [m_ref[0, d, h] for h in heads]
        bc = [b_rep[:, h * LANES:(h + 1) * LANES] for h in heads]
        i_col = [i_rep[:, h * LANES:(h + 1) * LANES] for h in heads]
        qk_raw = [_dot_nt(q[h], k[h]) for h in heads]
        inter = [_dot_nt(q[h], jnp.concatenate([cst[h], jnp.broadcast_to(nst[h], (LANES, ML_DK))],
                                               axis=0).astype(BF16)) for h in heads]
        ji = d * 2 * ML_HEADS
        a_row = [gt_ref[ji + h:ji + h + 1, :] - b_row[ji + ML_HEADS + h:ji + ML_HEADS + h + 1, :] for h in heads]
        amat = [jnp.where(causal, a_row[h], -jnp.inf) for h in heads]
        u = [jnp.maximum(m_rep[h], jnp.broadcast_to(jnp.max(amat[h], axis=1, keepdims=True), (length, LANES)))
             for h in heads]
        qk = [(qk_raw[h] * jnp.exp(amat[h] - u[h])).astype(BF16) for h in heads]
        intra = [_dot(qk[h], jnp.concatenate([v[h], jnp.ones((length, LANES), BF16)], axis=1)) for h in heads]
        b_last = [bc[h][edge:edge + 1, :] for h in heads]
        wlog_row = [b_last[h] + a_row[h] for h in heads]
        m_new = [jnp.maximum(b_last[h] + m_rep[h], jnp.max(wlog_row[h], axis=1, keepdims=True)) for h in heads]
        ws_row = [jnp.exp(wlog_row[h] - m_new[h]) for h in heads]
        kw = [(jnp.exp(b_last[h] - bc[h] + i_col[h] - m_new[h]) * k[h].astype(F32)).astype(BF16) for h in heads]
        c_upd = [_dot_tn(v[h], kw[h]) for h in heads]
        n_upd = [_dot(jnp.broadcast_to(ws_row[h], (8, length)).astype(BF16), k[h])[0:1] for h in heads]
        for h in heads:
            sc = jnp.exp(m_rep[h] - u[h])
            tot = jnp.concatenate([sc] * (ML_DV // LANES + 1), axis=1) * inter[h] + intra[h]
            inv = 1.0 / jnp.maximum(jnp.abs(tot[:, ML_DV:]), jnp.exp(-(bc[h] + u[h])))
            h_out[d][:, h * ML_DV:(h + 1) * ML_DV] = tot[:, :ML_DV] * jnp.concatenate([inv] * (ML_DV // LANES), axis=1)
            decay = jnp.exp(b_last[h] + m_rep[h] - m_new[h])
            c_ref[0, d, h] = decay * cst[h] + c_upd[h]
            n_ref[0, d, h] = decay * nst[h] + n_upd[h]
            m_ref[0, d, h] = m_new[h]


def _ml_scan(qk, p, g, gt, state, n_seq, seq_len):
    rows = qk.shape[0]
    length = min(ML_CHUNK, seq_len)
    nc = seq_len // length
    qw = ML_HEADS * ML_DK
    vw = ML_HEADS * ML_DV
    ng = gt.shape[0]

    def fwd(b, c):
        return b * nc + c

    def bwd(b, c):
        return b * nc + nc - 1 - c

    args, specs = [], []
    for pos in (fwd, bwd):
        args += [qk, qk, p, g, gt]
        specs += [pl.BlockSpec((length, qw), lambda b, c, pos=pos: (pos(b, c), 0)),
                  pl.BlockSpec((length, qw), lambda b, c, pos=pos: (pos(b, c), 1)),
                  pl.BlockSpec((length, vw), lambda b, c, pos=pos: (pos(b, c), 0)),
                  pl.BlockSpec((length, LANES), lambda b, c, pos=pos: (pos(b, c), 0)),
                  pl.BlockSpec((ng, length), lambda b, c, pos=pos: (0, pos(b, c)))]
    assert length == LANES and ML_DK == LANES
    gate_lane = jnp.arange(LANES)[:, None]
    head = (jnp.arange(ML_HEADS * LANES) // LANES)[None, :]
    sel = jnp.stack([jnp.stack([gate_lane == (2 * d + kind) * ML_HEADS + head for kind in range(2)])
                     for d in range(2)]).astype(BF16)
    args.append(sel)
    specs.append(pl.BlockSpec(sel.shape, lambda b, c: (0, 0, 0, 0)))
    c_spec = pl.BlockSpec((1, 2, ML_HEADS, ML_DV, ML_DK), lambda b, c: (b, 0, 0, 0, 0))
    n_spec = pl.BlockSpec((1, 2, ML_HEADS, 1, ML_DK), lambda b, c: (b, 0, 0, 0, 0))
    zero_init = state is None
    if not zero_init:
        c0, n0, m0 = state
        args += [c0, n0.reshape(n_seq, 2, ML_HEADS, 1, ML_DK),
                 jnp.broadcast_to(m0[..., None, None], (n_seq, 2, ML_HEADS, 1, ML_DK))]
        specs += [c_spec, n_spec, n_spec]
    hf, hb, c_fin, n_fin, m_fin = pl.pallas_call(
        functools.partial(_ml_scan_kernel, zero_init=zero_init), grid=(n_seq, nc), in_specs=specs,
        out_specs=[pl.BlockSpec((length, vw), lambda b, c: (fwd(b, c), 0)),
                   pl.BlockSpec((length, vw), lambda b, c: (bwd(b, c), 0)), c_spec, n_spec, n_spec],
        out_shape=[jax.ShapeDtypeStruct((rows, vw), F32), jax.ShapeDtypeStruct((rows, vw), F32),
                   jax.ShapeDtypeStruct((n_seq, 2, ML_HEADS, ML_DV, ML_DK), F32),
                   jax.ShapeDtypeStruct((n_seq, 2, ML_HEADS, 1, ML_DK), F32),
                   jax.ShapeDtypeStruct((n_seq, 2, ML_HEADS, 1, ML_DK), F32)],
        compiler_params=_cparams("parallel", "arbitrary"), name="mlstm_scan",
    )(*args)
    return hf, hb, (c_fin, n_fin[:, :, :, 0, :], m_fin[:, :, :, 0, 0])


def _gla_scan_kernel(*refs, zero_init):
    refs = list(refs)
    dirs = [tuple(refs[0:4]), tuple(refs[4:8])]
    w2_ref, ba_ref = refs[8:10]
    refs = refs[10:]
    if not zero_init:
        s0_ref = refs.pop(0)
    of_ref, ob_ref, s_ref, st_scr, la_scr = refs
    o_out = (of_ref, ob_ref)
    c = pl.program_id(1)
    last = pl.num_programs(1) - 1
    kw = GLA_HEADS * GLA_DK
    n_sub = of_ref.shape[0] // GLA_SUB

    @pl.when(c == 0)
    def _():
        for d in range(2):
            for h in range(GLA_HEADS):
                st_scr[d, h] = jnp.zeros((GLA_DV, GLA_DK), F32) if zero_init else s0_ref[0, d, h].T

    for d in range(2):
        u = dirs[d][3][...].astype(BF16)
        z = _dot(u, w2_ref[:, d * kw:(d + 1) * kw]) + ba_ref[:, d * kw:(d + 1) * kw]
        la = _log_sigmoid(z) / GLA_TAU
        la_scr[d] = la
        totals = jnp.sum(la.reshape(n_sub, GLA_SUB, kw), axis=1)
        worst = jnp.min(totals) if d == 0 else jnp.minimum(worst, jnp.min(totals))
    decay_bounded = worst * LOG2E > -GLA_MAX_EXP2

    ti = lax.broadcasted_iota(jnp.int32, (GLA_SUB, GLA_SUB), 0)
    si = lax.broadcasted_iota(jnp.int32, (GLA_SUB, GLA_SUB), 1)
    s_lane = lax.broadcasted_iota(jnp.int32, (GLA_SUB, GLA_SUB), 1)

    def sub_chunk(j, carry, bounded):
        steps = [(d, t) for d in range(2) for t in range(GLA_STEPS)]
        chains = [(d, h, t) for d, t in steps for h in range(GLA_HEADS)]
        rows = {(d, t): pl.ds(pl.multiple_of(
            ((j * GLA_STEPS + t) if d == 0 else n_sub - 1 - (j * GLA_STEPS + t)) * GLA_SUB, GLA_SUB), GLA_SUB)
            for d, t in steps}
        causal = [ti >= si, ti <= si]
        edge = [GLA_SUB - 1, 0]
        bc_all = {(d, t): jnp.dot(jnp.where(causal[d], 1.0, 0.0).astype(F32), la_scr[d, rows[d, t], :], precision=HI,
                                  preferred_element_type=F32) * LOG2E for d, t in steps}
        q, k, v, bc2, b_last, k_dec, q_dec, a = {}, {}, {}, {}, {}, {}, {}, {}
        for c in chains:
            d, h, t = c
            q_ref, k_ref, v_ref, _ = dirs[d]
            bc2[c] = bc_all[d, t][:, h * GLA_DK:(h + 1) * GLA_DK]
            q[c] = q_ref[rows[d, t], h * GLA_DK:(h + 1) * GLA_DK] * GLA_DK ** -0.5
            k[c] = k_ref[rows[d, t], h * GLA_DK:(h + 1) * GLA_DK]
            v[c] = v_ref[rows[d, t], h * GLA_DV:(h + 1) * GLA_DV].astype(BF16)
            b_last[c] = bc2[c][edge[d]:edge[d] + 1, :]
            k_dec[c] = (k[c] * jnp.exp2(b_last[c] - bc2[c])).astype(BF16)
            q_dec[c] = q[c] * jnp.exp2(bc2[c])
        for c in chains:
            if bounded:
                a[c] = _dot_nt((q_dec[c] * jnp.exp2(-b_last[c])).astype(BF16), k_dec[c])
            else:
                acc = jnp.zeros((GLA_SUB, GLA_SUB), F32)
                for s in range(GLA_SUB):
                    decay = jnp.exp2(bc2[c] - bc2[c][s:s + 1, :])
                    col = jnp.sum(q[c] * (k[c][s:s + 1, :] * decay), axis=1, keepdims=True)
                    acc = jnp.where(s_lane == s, col, acc)
                a[c] = acc
        kv = {c: _dot_tn(v[c], k_dec[c]) for c in chains}
        intra = {c: _dot(jnp.where(causal[c[0]], a[c], 0.0).astype(BF16), v[c]) for c in chains}
        state = {(d, h): st_scr[d, h] for d in range(2) for h in range(GLA_HEADS)}
        for t in range(GLA_STEPS):
            now = [c for c in chains if c[2] == t]
            inter = {c: _dot_nt(q_dec[c].astype(BF16), state[c[0], c[1]].astype(BF16)) for c in now}
            for c in now:
                d, h, _ = c
                o_out[d][rows[d, t], h * GLA_DV:(h + 1) * GLA_DV] = intra[c] + inter[c]
                state[d, h] = jnp.exp2(b_last[c]) * state[d, h] + kv[c]
        for (d, h), st in state.items():
            st_scr[d, h] = st
        return carry

    @pl.when(decay_bounded)
    def _():
        lax.fori_loop(0, n_sub // GLA_STEPS, functools.partial(sub_chunk, bounded=True), 0)

    @pl.when(jnp.logical_not(decay_bounded))
    def _():
        lax.fori_loop(0, n_sub // GLA_STEPS, functools.partial(sub_chunk, bounded=False), 0)

    @pl.when(c == last)
    def _():
        for d in range(2):
            for h in range(GLA_HEADS):
                s_ref[0, d, h] = st_scr[d, h].T


def _gla_scan(p, u, w2, b_a, state, n_seq, seq_len):
    rows = p.shape[0]
    length = min(GLA_BLOCK, seq_len)
    nc = seq_len // length
    kw = GLA_HEADS * GLA_DK
    vw = GLA_HEADS * GLA_DV

    def fwd(b, c):
        return b * nc + c

    def bwd(b, c):
        return b * nc + nc - 1 - c

    args, specs = [], []
    for pos in (fwd, bwd):
        args += [p, p, p, u]
        specs += [pl.BlockSpec((length, kw), lambda b, c, pos=pos: (pos(b, c), 0)),
                  pl.BlockSpec((length, kw), lambda b, c, pos=pos: (pos(b, c), 1)),
                  pl.BlockSpec((length, vw), lambda b, c, pos=pos: (pos(b, c), 2 * kw // vw)),
                  pl.BlockSpec((length, LANES), lambda b, c, pos=pos: (pos(b, c), 0))]
    args += [w2, b_a]
    specs += [pl.BlockSpec(w2.shape, lambda b, c: (0, 0)), pl.BlockSpec(b_a.shape, lambda b, c: (0, 0))]
    s_spec = pl.BlockSpec((1, 2, GLA_HEADS, GLA_DK, GLA_DV), lambda b, c: (b, 0, 0, 0, 0))
    zero_init = state is None
    if not zero_init:
        args.append(state)
        specs.append(s_spec)
    return pl.pallas_call(
        functools.partial(_gla_scan_kernel, zero_init=zero_init), grid=(n_seq, nc), in_specs=specs,
        out_specs=[pl.BlockSpec((length, vw), lambda b, c: (fwd(b, c), 0)),
                   pl.BlockSpec((length, vw), lambda b, c: (bwd(b, c), 0)), s_spec],
        out_shape=[jax.ShapeDtypeStruct((rows, vw), F32), jax.ShapeDtypeStruct((rows, vw), F32),
                   jax.ShapeDtypeStruct((n_seq, 2, GLA_HEADS, GLA_DK, GLA_DV), F32)],
        scratch_shapes=[pltpu.VMEM((2, GLA_HEADS, GLA_DV, GLA_DK), F32), pltpu.VMEM((2, length, kw), F32)],
        compiler_params=_cparams("parallel", "arbitrary"), name="gla_scan",
    )(*args)


def kernel(x_prompt, x_sample, cache_k_0, cache_v_0, state_mlstm_C_1, state_mlstm_n_1, state_mlstm_m_1, state_gla_S_2, cache_k_3, cache_v_3, c, c_ctx, w_mod, b_mod, norm1_g, norm2_g, final_g, router_w, router_b, moe_wg, moe_wu, moe_wd, attn0_w_qkv, attn0_sink, attn0_w_o, mlstm1_w_in, mlstm1_b_gates, mlstm1_conv, mlstm1_norm_g, mlstm1_w_out, gla2_w_in, gla2_w_a1, gla2_w_a2, gla2_b_a, gla2_norm_g, gla2_w_out, attn3_w_qkv, attn3_sink, attn3_w_o):
    n_ctx, ctx_len, d = x_prompt.shape
    n_lat, lat_len, _ = x_sample.shape
    depth = w_mod.shape[0]

    cvec = jnp.concatenate([c_ctx[None, :], c, jnp.zeros((8 - 1 - n_lat, d), F32)], axis=0)
    mod = _modulation(cvec, w_mod, b_mod).reshape(depth, 8, 6, 1, d)

    def mods(layer, kind, latent):
        return mod[layer, 1:1 + n_lat, kind] if latent else mod[layer, 0:1, kind]

    rw_hi = router_w.T.astype(BF16)
    rw_lo = (router_w.T - rw_hi.astype(F32)).astype(BF16)
    rwt = jnp.concatenate([rw_hi, rw_lo], axis=0)
    rb = router_b.reshape(-1, 1)
    attn_w = {0: (attn0_w_qkv.astype(BF16), attn0_sink, attn0_w_o.astype(BF16), cache_k_0, cache_v_0),
              3: (attn3_w_qkv.astype(BF16), attn3_sink, attn3_w_o.astype(BF16), cache_k_3, cache_v_3)}
    ml_qw = ML_HEADS * ML_DK
    ml_vw = ML_HEADS * ML_DV
    ml_main = 2 * ml_qw + 2 * ml_vw
    ml_w_qk = mlstm1_w_in[:, :2 * ml_qw].astype(BF16)
    ml_w_vo = mlstm1_w_in[:, 2 * ml_qw:ml_main].astype(BF16)
    ml_w_gates = mlstm1_w_in[:, ml_main:]
    ml_w_out = mlstm1_w_out.astype(BF16)
    gla_kw = GLA_HEADS * GLA_DK
    gla_w_in = gla2_w_in.astype(BF16)
    gla_w_a1 = jnp.pad(jnp.concatenate([gla2_w_a1[0], gla2_w_a1[1]], axis=1),
                       ((0, 0), (0, LANES - 2 * GLA_RANK))).astype(BF16)
    gla_w2 = jnp.zeros((LANES, 2 * gla_kw), F32)
    gla_w2 = gla_w2.at[:GLA_RANK, :gla_kw].set(gla2_w_a2[0]).at[GLA_RANK:2 * GLA_RANK, gla_kw:].set(gla2_w_a2[1])
    gla_w2 = gla_w2.astype(BF16)
    gla_ba = gla2_b_a.reshape(1, 2 * gla_kw)
    gla_w_out = gla2_w_out.astype(BF16)

    new_state = []

    def mixer(layer, s, count0):
        latent, n_seq, seq_len, x, h = s["latent"], s["n_seq"], s["seq_len"], s["x"], s["h"]
        tail = (mods(layer, 2, latent), norm2_g[layer], mods(layer, 4, latent), mods(layer, 3, latent), rwt, rb,
                count0)
        kind = layer % 3
        if kind == 0:
            w_qkv, sink, w_o, ck, cv = attn_w[layer]
            qkv = _matmul(h, w_qkv)
            if latent:
                att = _attn_lat(qkv, ck, cv, sink, n_seq, seq_len)
            else:
                att = _attn_ctx(qkv, sink, n_seq, seq_len)
                qw = ATT_HEADS * HEAD_DIM
                kw = ATT_KV * HEAD_DIM
                new_state.append(qkv[:, qw:qw + kw].reshape(n_seq, seq_len, ATT_KV, HEAD_DIM))
                new_state.append(qkv[:, qw + kw:].reshape(n_seq, seq_len, ATT_KV, HEAD_DIM))
            return _proj("plain", (att,), w_o, x, *tail)
        if kind == 1:
            p = _matmul(h, ml_w_vo)
            g, gt = _ml_gates(h, ml_w_gates, mlstm1_b_gates)
            qk = _ml_qk(h, ml_w_qk, mlstm1_conv, seq_len)
            st = (state_mlstm_C_1, state_mlstm_n_1, state_mlstm_m_1) if latent else None
            hf, hb, fin = _ml_scan(qk, p, g, gt, st, n_seq, seq_len)
            if not latent:
                new_state.extend(fin)
            return _proj("mlstm", (hf, hb, p, 1, mlstm1_norm_g), ml_w_out, x, *tail)
        p = _matmul(h, gla_w_in)
        u = _matmul(h, gla_w_a1)
        of, ob, s_fin = _gla_scan(p, u, gla_w2, gla_ba, state_gla_S_2 if latent else None, n_seq, seq_len)
        if not latent:
            new_state.append(s_fin)
        gla_vw = GLA_HEADS * GLA_DV
        return _proj("gla", (of, ob, p, (2 * gla_kw + gla_vw) // gla_vw, gla2_norm_g), gla_w_out, x, *tail)

    streams = [dict(latent=False, n_seq=n_ctx, seq_len=ctx_len, x=x_prompt.reshape(n_ctx * ctx_len, d)),
               dict(latent=True, n_seq=n_lat, seq_len=lat_len, x=x_sample.reshape(n_lat * lat_len, d))]
    for s in streams:
        s["h"] = _rownorm(s["x"], norm1_g[0], mods(0, 1, s["latent"]), mods(0, 0, s["latent"]))
    for layer in range(depth):
        counts = jnp.zeros((N_EXPERTS, LANES), jnp.int32)
        for s in streams:
            s["x"], s["h2"], s["meta"], s["wcol"], counts = mixer(layer, s, counts)
        cnt = counts[:, 0]
        for s in streams:
            s["slots"] = _slots(s["meta"], cnt)
        xs, info = _dispatch([s["h2"] for s in streams], jnp.concatenate([s["slots"] for s in streams], axis=1), cnt)
        ys = _ffn(xs, info, moe_wg, moe_wu, moe_wd, layer)
        for s in streams:
            latent = s["latent"]
            gate2 = mods(layer, 5, latent)
            if layer + 1 < depth:
                s["x"], s["h"] = _combine(s["x"], ys, s["slots"], s["wcol"], gate2, norm1_g[layer + 1],
                                          mod=(mods(layer + 1, 1, latent), mods(layer + 1, 0, latent)), out_x=True,
                                          h_dtype=F32 if (layer + 1) % 3 == 1 else BF16)
            else:
                s["out"] = _combine(s["x"], ys, s["slots"], s["wcol"], gate2, final_g, h_dtype=F32)
    y_prompt = streams[0]["out"].reshape(n_ctx, ctx_len, d)
    y_sample = streams[1]["out"].reshape(n_lat, lat_len, d)
    return (y_prompt, y_sample, *new_state)
```

```python
import functools

import jax
import jax.numpy as jnp
from jax import lax
from jax.experimental import pallas as pl
from jax.experimental.pallas import tpu as pltpu

F32 = jnp.float32
BF16 = jnp.bfloat16
HI = lax.Precision.HIGHEST

EPS = 1e-6
LOG2E = 1.4426950408889634
GRID_W = 64
ATT_HEADS = 16
ATT_KV = 4
ATT_GROUP = ATT_HEADS // ATT_KV
HEAD_DIM = 64
WINDOW = 128
Q_BLOCK = 128
ROPE_BASE = 10000.0
ML_HEADS = 8
ML_DK = 128
ML_DV = 256
ML_CHUNK = 128
GLA_HEADS = 4
GLA_DK = 128
GLA_DV = 256
GLA_RANK = 16
GLA_TAU = 16.0
GLA_SUB = 16
GLA_BLOCK = 256
GLA_STEPS = 8
GLA_MAX_EXP2 = 80.0
N_EXPERTS = 16
N_GROUPS = 4
GROUP_SIZE = N_EXPERTS // N_GROUPS
LANES = 128
VMEM_LIMIT = 56 * 1024 * 1024


def _cparams(*sem):
    return pltpu.CompilerParams(dimension_semantics=sem, vmem_limit_bytes=VMEM_LIMIT)


def _dot(a, b):
    return jnp.dot(a, b, preferred_element_type=F32)


def _dot_nt(a, b, precision=None):
    return lax.dot_general(a, b, (((1,), (1,)), ((), ())), precision=precision, preferred_element_type=F32)


def _dot_tn(a, b):
    return lax.dot_general(a, b, (((0,), (0,)), ((), ())), preferred_element_type=F32)


def _sigmoid(x):
    return 1.0 / (1.0 + jnp.exp(-x))


def _silu(x):
    return x * _sigmoid(x)


def _log_sigmoid(x):
    return jnp.minimum(x, 0.0) - jnp.log(1.0 + jnp.exp(-jnp.abs(x)))


def _rms_rows(x, g):
    ms = jnp.mean(x * x, axis=-1, keepdims=True)
    return x * lax.rsqrt(ms + EPS) * g


def _mod_kernel(c_ref, w_ref, b_ref, o_ref):
    s = _silu(c_ref[...])
    o_ref[0] = _dot(s.astype(BF16), w_ref[0].astype(BF16)) + b_ref[0]


def _modulation(cvec, w_mod, b_mod):
    depth, d, n6 = w_mod.shape
    tn = 1536
    return pl.pallas_call(
        _mod_kernel,
        grid=(depth, n6 // tn),
        in_specs=[pl.BlockSpec((8, d), lambda l, j: (0, 0)),
                  pl.BlockSpec((1, d, tn), lambda l, j: (l, 0, j)),
                  pl.BlockSpec((1, 1, tn), lambda l, j: (l, 0, j))],
        out_specs=pl.BlockSpec((1, 8, tn), lambda l, j: (l, 0, j)),
        out_shape=jax.ShapeDtypeStruct((depth, 8, n6), F32),
        compiler_params=_cparams("parallel", "parallel"),
        name="adaln_modulation",
    )(cvec, w_mod, b_mod.reshape(depth, 1, n6))


def _route(h, rwt, rb, carry):
    tm = h.shape[0]
    h_hi = h.astype(BF16)
    h_lo = (h - h_hi.astype(F32)).astype(BF16)
    by_hi = _dot_nt(rwt, h_hi)
    logits = by_hi[:N_EXPERTS] + by_hi[N_EXPERTS:] + _dot_nt(rwt[:N_EXPERTS], h_lo)
    scores = _sigmoid(logits)
    sel = scores + rb
    expert = lax.broadcasted_iota(jnp.int32, sel.shape, 0)
    pos = expert % GROUP_SIZE
    grp = expert // GROUP_SIZE

    def mate(x, k):
        ahead = pltpu.roll(x, N_EXPERTS - k, axis=0)
        behind = pltpu.roll(x, GROUP_SIZE - k, axis=0)
        return jnp.where(pos + k < GROUP_SIZE, ahead, behind)

    beaten = jnp.zeros_like(sel)
    for k in range(1, GROUP_SIZE):
        other = mate(sel, k)
        other_first = (pos + k) % GROUP_SIZE < pos
        beaten = beaten + jnp.where(other_first, jnp.where(other >= sel, 1.0, 0.0), jnp.where(other > sel, 1.0, 0.0))
    top2 = jnp.where(beaten < 2.0, sel, 0.0)
    gscore = top2
    for k in range(1, GROUP_SIZE):
        gscore = gscore + mate(top2, k)
    lost = jnp.zeros_like(sel)
    for k in range(1, N_GROUPS):
        other = pltpu.roll(gscore, N_EXPERTS - GROUP_SIZE * k, axis=0)
        other_first = (grp + k) % N_GROUPS < grp
        lost = lost + jnp.where(other_first, jnp.where(other >= gscore, 1.0, 0.0),
                                jnp.where(other > gscore, 1.0, 0.0))
    picked = jnp.where(lost < 0.5, jnp.where(beaten < 2.0, 1.0, 0.0), 0.0)
    chosen = picked > 0.5
    weight = jnp.where(chosen, scores, 0.0)
    wsum = jnp.sum(weight, axis=0, keepdims=True)
    e_f = expert.astype(F32)
    e_a = jnp.min(jnp.where(chosen, e_f, float(N_EXPERTS)), axis=0, keepdims=True)
    e_b = jnp.max(jnp.where(chosen, e_f, -1.0), axis=0, keepdims=True)
    before = (lax.broadcasted_iota(jnp.int32, (tm, tm), 0) < lax.broadcasted_iota(jnp.int32, (tm, tm), 1))
    rank = _dot(picked.astype(BF16), jnp.where(before, 1.0, 0.0).astype(BF16)) + carry
    is_a = e_f == e_a
    is_b = e_f == e_b
    r_a = jnp.sum(jnp.where(is_a, rank, 0.0), axis=0, keepdims=True)
    r_b = jnp.sum(jnp.where(is_b, rank, 0.0), axis=0, keepdims=True)
    w_a = jnp.sum(jnp.where(is_a, weight, 0.0), axis=0, keepdims=True)
    w_b = jnp.sum(jnp.where(is_b, weight, 0.0), axis=0, keepdims=True)
    meta = jnp.concatenate([e_a, e_b, r_a, r_b, jnp.zeros((4, tm), F32)], axis=0).astype(jnp.int32)
    wcol = jnp.concatenate([w_a / wsum, w_b / wsum, jnp.zeros((LANES - 2, tm), F32)], axis=0).T
    return meta, wcol, carry + jnp.sum(picked, axis=1, keepdims=True)


def _norm_mod(x, g_ref, mod_refs):
    h = _rms_rows(x, g_ref[...])
    if mod_refs is not None:
        a_ref, s_ref = mod_refs
        h = h * (1.0 + a_ref[0]) + s_ref[0]
    return h


def _rownorm_kernel(x_ref, g_ref, a_ref, s_ref, h_ref):
    h_ref[...] = _norm_mod(x_ref[...], g_ref, (a_ref, s_ref)).astype(h_ref.dtype)


def _mod_spec(n_mod, rows, tm, d, n_prefetch=0):
    per = (rows // n_mod) // tm
    return pl.BlockSpec((1, 1, d), lambda i, *_: (i // per, 0, 0))


def _rownorm(x, g, scale, shift, *, tm=512):
    rows, d = x.shape
    row_spec = pl.BlockSpec((tm, d), lambda i: (i, 0))
    return pl.pallas_call(
        _rownorm_kernel, grid=(rows // tm,),
        in_specs=[row_spec, pl.BlockSpec((1, d), lambda i: (0, 0)), _mod_spec(scale.shape[0], rows, tm, d),
                  _mod_spec(shift.shape[0], rows, tm, d)],
        out_specs=row_spec, out_shape=jax.ShapeDtypeStruct((rows, d), BF16),
        compiler_params=_cparams("parallel"), name="rownorm",
    )(x, g.reshape(1, d), scale, shift)


def _mm_kernel(a_ref, w_ref, o_ref):
    o_ref[...] = _dot(a_ref[...].astype(BF16), w_ref[...]).astype(o_ref.dtype)


def _matmul(a, w, *, out_dtype=F32, tm=1024):
    m, k = a.shape
    n = w.shape[1]
    tn = next(t for t in (1024, 768, 512, LANES) if n % t == 0)
    return pl.pallas_call(
        _mm_kernel, grid=(m // tm, n // tn),
        in_specs=[pl.BlockSpec((tm, k), lambda i, j: (i, 0)), pl.BlockSpec((k, tn), lambda i, j: (0, j))],
        out_specs=pl.BlockSpec((tm, tn), lambda i, j: (i, j)),
        out_shape=jax.ShapeDtypeStruct((m, n), out_dtype),
        compiler_params=_cparams("parallel", "parallel"), name="matmul",
    )(a, w)


def _head_norm(x, g, n_heads, dv):
    outs = []
    for h in range(n_heads):
        xs = x[:, h * dv:(h + 1) * dv]
        ms = jnp.mean(xs * xs, axis=-1, keepdims=True)
        outs.append(xs * lax.rsqrt(ms + EPS) * g[:, h * dv:(h + 1) * dv])
    return jnp.concatenate(outs, axis=1)


def _proj_kernel(*refs, pre):
    refs = list(refs)
    if pre == "plain":
        a = refs.pop(0)[...]
    else:
        f_ref, b_ref, p_ref, hg_ref = refs.pop(0), refs.pop(0), refs.pop(0), refs.pop(0)
        hsum = f_ref[...] + b_ref[...]
        if pre == "mlstm":
            a = _sigmoid(p_ref[...]) * _head_norm(hsum, hg_ref[...], ML_HEADS, ML_DV)
        else:
            a = _head_norm(hsum, hg_ref[...], GLA_HEADS, GLA_DV) * _silu(p_ref[...])
        a = a.astype(BF16)
    w_ref, x_ref, gate_ref, g_ref, a_ref, s_ref, rwt_ref, rb_ref, count0_ref = refs[:9]
    xo_ref, h_ref, meta_ref, wcol_ref, count_ref, carry_ref = refs[9:]

    @pl.when(pl.program_id(0) == 0)
    def _():
        carry_ref[...] = count0_ref[...].astype(F32)

    x = x_ref[...] + gate_ref[0] * _dot(a, w_ref[...])
    xo_ref[...] = x
    h = _norm_mod(x, g_ref, (a_ref, s_ref))
    _rows_to_tiles(h_ref, h)
    meta, wcol, carry = _route(h, rwt_ref[...], rb_ref[...], carry_ref[:, 0:1])
    meta_ref[...] = meta
    wcol_ref[...] = wcol
    carry_ref[...] = jnp.broadcast_to(carry, carry_ref.shape)
    count_ref[...] = jnp.broadcast_to(carry, count_ref.shape).astype(jnp.int32)


def _proj(pre, pre_args, w_out, x, gate, g, scale, shift, rwt, rb, count0, *, tm=512):
    rows, d = x.shape
    k = w_out.shape[0]
    row_spec = pl.BlockSpec((tm, d), lambda i: (i, 0))
    if pre == "plain":
        args, specs = [pre_args[0]], [pl.BlockSpec((tm, k), lambda i: (i, 0))]
    else:
        hf, hb, p, col_block, hg = pre_args
        wide = pl.BlockSpec((tm, k), lambda i: (i, 0))
        args = [hf, hb, p, hg.reshape(1, k)]
        specs = [wide, wide, pl.BlockSpec((tm, k), lambda i: (i, col_block)), pl.BlockSpec((1, k), lambda i: (0, 0))]
    args += [w_out, x, gate, g.reshape(1, d), scale, shift, rwt, rb, count0]
    specs += [pl.BlockSpec((k, d), lambda i: (0, 0)), row_spec, _mod_spec(gate.shape[0], rows, tm, d),
              pl.BlockSpec((1, d), lambda i: (0, 0)), _mod_spec(scale.shape[0], rows, tm, d),
              _mod_spec(shift.shape[0], rows, tm, d), pl.BlockSpec(rwt.shape, lambda i: (0, 0)),
              pl.BlockSpec(rb.shape, lambda i: (0, 0)), pl.BlockSpec(count0.shape, lambda i: (0, 0))]
    return pl.pallas_call(
        functools.partial(_proj_kernel, pre=pre), grid=(rows // tm,), in_specs=specs,
        out_specs=[row_spec, pl.BlockSpec((tm * SUBLANES, LANES), lambda i: (i, 0)),
                   pl.BlockSpec((8, tm), lambda i: (0, i)),
                   pl.BlockSpec((tm, LANES), lambda i: (i, 0)), pl.BlockSpec((N_EXPERTS, LANES), lambda i: (0, 0))],
        out_shape=[jax.ShapeDtypeStruct((rows, d), F32), jax.ShapeDtypeStruct((rows * SUBLANES, LANES), F32),
                   jax.ShapeDtypeStruct((8, rows), jnp.int32), jax.ShapeDtypeStruct((rows, LANES), F32),
                   jax.ShapeDtypeStruct((N_EXPERTS, LANES), jnp.int32)],
        scratch_shapes=[pltpu.VMEM((N_EXPERTS, LANES), F32)],
        compiler_params=_cparams("arbitrary"), name="proj_" + pre,
    )(*args)


MOE_TILE = 512
MOE_TILE_SHIFT = 9
MOE_TOKENS = 256
FFN_PARTS = 2
ROW_UNROLL = 8


SUBLANES = 8


def _rows_to_tiles(ref, x, lead=(), first=0):
    rows = x.shape[0]
    for c in range(SUBLANES):
        ref[(*lead, pl.ds(first * SUBLANES + c, rows, stride=SUBLANES), slice(None))] = x[:, c * LANES:(c + 1) * LANES]


def _tiles_to_rows(ref, rows, lead=(), first=0):
    return jnp.concatenate([ref[(*lead, pl.ds(first * SUBLANES + c, rows, stride=SUBLANES), slice(None))]
                            for c in range(SUBLANES)], axis=1)


def _slot_tiles(rows):
    return (2 * rows) // MOE_TILE + N_EXPERTS


def _expert_offsets(cnt_ref, off_ref):
    def per_expert(e, k):
        off_ref[e] = k * MOE_TILE
        return k + ((cnt_ref[e] + MOE_TILE - 1) >> MOE_TILE_SHIFT)
    return lax.fori_loop(0, N_EXPERTS, per_expert, 0)


def _slots_kernel(cnt_ref, meta_ref, slot_ref, off_ref):
    @pl.when(pl.program_id(0) == 0)
    def _():
        _expert_offsets(cnt_ref, off_ref)

    e_a, e_b = meta_ref[0:1, :], meta_ref[1:2, :]
    off_a = jnp.zeros_like(e_a)
    off_b = jnp.zeros_like(e_b)
    for e in range(N_EXPERTS):
        off_a = jnp.where(e_a == e, off_ref[e], off_a)
        off_b = jnp.where(e_b == e, off_ref[e], off_b)
    slot_ref[...] = jnp.concatenate([off_a + meta_ref[2:3, :], off_b + meta_ref[3:4, :],
                                     jnp.zeros((6, e_a.shape[1]), jnp.int32)], axis=0)


def _slots(meta, counts, *, tm=1024):
    rows = meta.shape[1]
    grid_spec = pltpu.PrefetchScalarGridSpec(
        num_scalar_prefetch=1, grid=(rows // tm,),
        in_specs=[pl.BlockSpec((8, tm), lambda i, cnt: (0, i))],
        out_specs=pl.BlockSpec((8, tm), lambda i, cnt: (0, i)),
        scratch_shapes=[pltpu.SMEM((N_EXPERTS,), jnp.int32)])
    return pl.pallas_call(
        _slots_kernel, grid_spec=grid_spec, out_shape=jax.ShapeDtypeStruct((8, rows), jnp.int32),
        compiler_params=_cparams("arbitrary"), name="moe_slots",
    )(counts, meta)


def _dispatch_kernel(*refs, steps):
    sa_ref, sb_ref, cnt_ref = refs[:3]
    h_refs = refs[3:3 + len(steps)]
    xs_ref, info_ref, off_ref, zero_ref, sem = refs[3 + len(steps):]
    i = pl.program_id(0)
    tm = h_refs[0].shape[0] // SUBLANES
    n_tiles = info_ref.shape[0] - 1
    tile_rows = MOE_TILE * SUBLANES

    def tile_copy(tile):
        return pltpu.make_async_copy(zero_ref, xs_ref.at[pl.ds(tile * tile_rows, tile_rows), :], sem)

    @pl.when(i == 0)
    def _():
        zero_ref[...] = jnp.zeros_like(zero_ref)
        used = _expert_offsets(cnt_ref, off_ref)

        def per_expert(e, _):
            first = off_ref[e] >> MOE_TILE_SHIFT
            nt = (cnt_ref[e] + MOE_TILE - 1) >> MOE_TILE_SHIFT

            def fill(j, _):
                info_ref[first + j] = e
                return 0
            lax.fori_loop(0, nt, fill, 0)

            @pl.when(nt > 0)
            def _():
                tile_copy(first + nt - 1).start()
                tile_copy(first + nt - 1).wait()
            return 0
        lax.fori_loop(0, N_EXPERTS, per_expert, 0)
        info_ref[n_tiles] = used

        def tail(j, _):
            info_ref[j] = N_EXPERTS - 1
            tile_copy(j).start()
            tile_copy(j).wait()
            return 0
        lax.fori_loop(used, n_tiles, tail, 0)

    base = i * tm

    def copy_rows(h_ref):
        def row_copy(t, slot):
            dst = pl.multiple_of(slot * SUBLANES, SUBLANES)
            return pltpu.make_async_copy(h_ref.at[pl.ds(t * SUBLANES, SUBLANES), :],
                                         xs_ref.at[pl.ds(dst, SUBLANES), :], sem)

        for t in range(tm):
            row_copy(t, sa_ref[base + t]).start(priority=0)
            row_copy(t, sb_ref[base + t]).start(priority=1)
        for _ in range(2):
            pltpu.make_async_copy(h_ref, xs_ref.at[pl.ds(0, tm * SUBLANES), :], sem).wait()

    first = 0
    for h_ref, n in zip(h_refs, steps):
        pl.when(jnp.logical_and(i >= first, i < first + n))(functools.partial(copy_rows, h_ref))
        first += n


def _dispatch(hs, slots, counts):
    tm = MOE_TOKENS
    steps = tuple(h.shape[0] // (tm * SUBLANES) for h in hs)
    n_tiles = _slot_tiles(sum(steps) * tm)
    specs, first = [], 0
    for n in steps:
        specs.append(pl.BlockSpec((tm * SUBLANES, LANES),
                                  lambda i, *_, first=first, n=n: (jnp.clip(i - first, 0, n - 1), 0)))
        first += n
    grid_spec = pltpu.PrefetchScalarGridSpec(
        num_scalar_prefetch=3, grid=(sum(steps),), in_specs=specs,
        out_specs=[pl.BlockSpec(memory_space=pl.ANY), pl.BlockSpec(memory_space=pltpu.SMEM)],
        scratch_shapes=[pltpu.SMEM((N_EXPERTS,), jnp.int32), pltpu.VMEM((MOE_TILE * SUBLANES, LANES), F32),
                        pltpu.SemaphoreType.DMA(())])
    return pl.pallas_call(
        functools.partial(_dispatch_kernel, steps=steps), grid_spec=grid_spec,
        out_shape=[jax.ShapeDtypeStruct((n_tiles * MOE_TILE * SUBLANES, LANES), F32),
                   jax.ShapeDtypeStruct((n_tiles + 1,), jnp.int32)],
        compiler_params=_cparams("arbitrary"), name="moe_dispatch",
    )(slots[0], slots[1], counts, *hs)


def _ffn_kernel(info_ref, xs_ref, wg_ref, wu_ref, wd_ref, ys_ref, wg_s, wu_s, wd_s):
    i = pl.program_id(0)
    used = info_ref[info_ref.shape[0] - 1]
    fresh = jnp.logical_or(i == 0, info_ref[i] != info_ref[jnp.maximum(i - 1, 0)])

    @pl.when(jnp.logical_and(i < used, fresh))
    def _():
        wg_s[...] = wg_ref[0, 0].astype(BF16)
        wu_s[...] = wu_ref[0, 0].astype(BF16)
        wd_s[...] = wd_ref[0, 0].astype(BF16)

    @pl.when(i < used)
    def _():
        part = MOE_TILE // FFN_PARTS
        x = [_tiles_to_rows(xs_ref, part, first=p * part).astype(BF16) for p in range(FFN_PARTS)]
        gate = [_dot(xp, wg_s[...]) for xp in x]
        up = [_dot(xp, wu_s[...]) for xp in x]
        hid = [(_silu(g) * u).astype(BF16) for g, u in zip(gate, up)]
        y = [_dot(hp, wd_s[...]) for hp in hid]
        for p in range(FFN_PARTS):
            _rows_to_tiles(ys_ref, y[p], first=p * part)

    @pl.when(i >= used)
    def _():
        ys_ref[...] = jnp.zeros_like(ys_ref)


def _ffn(xs, info, wg, wu, wd, layer):
    tile_rows = MOE_TILE * SUBLANES
    n_tiles = xs.shape[0] // tile_rows
    d, f = wg.shape[2:]

    def w_map(i, info):
        return (layer, info[i], 0, 0)

    grid_spec = pltpu.PrefetchScalarGridSpec(
        num_scalar_prefetch=1, grid=(n_tiles,),
        in_specs=[pl.BlockSpec((tile_rows, LANES), lambda i, info: (jnp.minimum(i, info[n_tiles] - 1), 0)),
                  pl.BlockSpec((1, 1, d, f), w_map), pl.BlockSpec((1, 1, d, f), w_map),
                  pl.BlockSpec((1, 1, f, d), w_map)],
        out_specs=pl.BlockSpec((tile_rows, LANES), lambda i, info: (i, 0)),
        scratch_shapes=[pltpu.VMEM((d, f), BF16), pltpu.VMEM((d, f), BF16), pltpu.VMEM((f, d), BF16)])
    return pl.pallas_call(
        _ffn_kernel, grid_spec=grid_spec, out_shape=jax.ShapeDtypeStruct(xs.shape, F32),
        compiler_params=_cparams("arbitrary"), name="moe_ffn",
    )(info, xs, wg, wu, wd)


def _combine_kernel(*refs, has_mod, out_x):
    refs = list(refs)
    sa_ref, sb_ref, x_ref, ys_ref, wcol_ref, gate_ref, g_ref = refs[:7]
    refs = refs[7:]
    mod_refs = (refs.pop(0), refs.pop(0)) if has_mod else None
    xo_ref = refs.pop(0) if out_x else None
    h_ref, buf_a, buf_b, sems = refs
    i = pl.program_id(0)
    tm = x_ref.shape[0]

    def issue(tile, slot):
        base = tile * tm
        for t in range(tm):
            dst = pl.ds(t * SUBLANES, SUBLANES)
            src_a = pl.multiple_of(sa_ref[base + t] * SUBLANES, SUBLANES)
            src_b = pl.multiple_of(sb_ref[base + t] * SUBLANES, SUBLANES)
            pltpu.make_async_copy(ys_ref.at[pl.ds(src_a, SUBLANES), :], buf_a.at[slot, dst, :],
                                  sems.at[slot]).start(priority=0)
            pltpu.make_async_copy(ys_ref.at[pl.ds(src_b, SUBLANES), :], buf_b.at[slot, dst, :],
                                  sems.at[slot]).start(priority=1)

    @pl.when(i == 0)
    def _():
        issue(0, 0)

    @pl.when(i + 1 < pl.num_programs(0))
    def _():
        issue(i + 1, (i + 1) % 2)

    slot = i % 2
    for buf in (buf_a, buf_b):
        pltpu.make_async_copy(ys_ref.at[pl.ds(0, tm * SUBLANES), :], buf.at[slot], sems.at[slot]).wait()
    y = (wcol_ref[:, 0:1] * _tiles_to_rows(buf_a, tm, lead=(slot,))
         + wcol_ref[:, 1:2] * _tiles_to_rows(buf_b, tm, lead=(slot,)))
    x = x_ref[...] + gate_ref[0] * y
    if out_x:
        xo_ref[...] = x
    h_ref[...] = _norm_mod(x, g_ref, mod_refs).astype(h_ref.dtype)


def _combine(x, ys, slots, wcol, gate, g, *, mod=None, out_x=False, h_dtype=BF16):
    rows, d = x.shape
    tm = MOE_TOKENS
    row_spec = pl.BlockSpec((tm, d), lambda i, *_: (i, 0))
    args = [x, ys, wcol, gate, g.reshape(1, d)]
    specs = [row_spec, pl.BlockSpec(memory_space=pl.ANY), pl.BlockSpec((tm, LANES), lambda i, *_: (i, 0)),
             _mod_spec(gate.shape[0], rows, tm, d), pl.BlockSpec((1, d), lambda i, *_: (0, 0))]
    if mod is not None:
        for m in mod:
            args.append(m)
            specs.append(_mod_spec(m.shape[0], rows, tm, d))
    out_shape, out_specs = [], []
    if out_x:
        out_shape.append(jax.ShapeDtypeStruct((rows, d), F32))
        out_specs.append(row_spec)
    out_shape.append(jax.ShapeDtypeStruct((rows, d), h_dtype))
    out_specs.append(row_spec)
    grid_spec = pltpu.PrefetchScalarGridSpec(
        num_scalar_prefetch=2, grid=(rows // tm,), in_specs=specs, out_specs=out_specs,
        scratch_shapes=[pltpu.VMEM((2, tm * SUBLANES, LANES), F32), pltpu.VMEM((2, tm * SUBLANES, LANES), F32),
                        pltpu.SemaphoreType.DMA((2,))])
    outs = pl.pallas_call(
        functools.partial(_combine_kernel, has_mod=mod is not None, out_x=out_x), grid_spec=grid_spec,
        out_shape=out_shape, compiler_params=_cparams("arbitrary"), name="moe_combine",
    )(slots[0], slots[1], *args)
    return outs if out_x else outs[0]


def _softmax_av(groups):
    maxes = []
    for scores, _, sink in groups:
        m = sink
        for s in scores:
            m = jnp.maximum(m, jnp.broadcast_to(jnp.max(s, axis=-1, keepdims=True), sink.shape))
        maxes.append(m)
    probs = [[jnp.exp(s - jnp.concatenate([m] * (s.shape[1] // LANES), axis=1)).astype(BF16) for s in scores]
             for (scores, _, _), m in zip(groups, maxes)]
    outs = []
    for (_, values, sink), m, ps in zip(groups, maxes, probs):
        hd = values[0].shape[1]
        tot = None
        for p, v in zip(ps, values):
            n = v.shape[0]
            v_ext = jnp.concatenate([v, jnp.zeros((n, LANES - hd), BF16), jnp.ones((n, LANES), BF16)], axis=1)
            pv = _dot(p, v_ext)
            tot = pv if tot is None else tot + pv
        outs.append((tot, sink, m, hd))
    return [tot[:, :hd] / (tot[:, LANES:] + jnp.exp(sink - m))[:, :hd] for tot, sink, m, hd in outs]


def _sink_column(sink_ref, kv, rows):
    return jnp.concatenate([jnp.full((rows, LANES), sink_ref[kv * ATT_GROUP + g], F32) for g in range(ATT_GROUP)],
                           axis=0)


def _attn_ctx_kernel(sink_ref, qkv_ref, o_ref):
    t = qkv_ref.shape[0]
    qw = ATT_HEADS * HEAD_DIM
    kw = ATT_KV * HEAD_DIM
    groups = []
    for kv in range(ATT_KV):
        q = jnp.concatenate(
            [qkv_ref[:, (kv * ATT_GROUP + g) * HEAD_DIM:(kv * ATT_GROUP + g + 1) * HEAD_DIM] for g in range(ATT_GROUP)],
            axis=0).astype(BF16)
        k = qkv_ref[:, qw + kv * HEAD_DIM:qw + (kv + 1) * HEAD_DIM].astype(BF16)
        v = qkv_ref[:, qw + kw + kv * HEAD_DIM:qw + kw + (kv + 1) * HEAD_DIM].astype(BF16)
        groups.append(([_dot_nt(q, k) * HEAD_DIM ** -0.5], [v], _sink_column(sink_ref, kv, t)))
    heads_out = [o[g * t:(g + 1) * t] for o in _softmax_av(groups) for g in range(ATT_GROUP)]
    o_ref[...] = jnp.concatenate(heads_out, axis=1).astype(o_ref.dtype)


def _attn_ctx(qkv, sink, n_seq, seq_len):
    rows, cols = qkv.shape
    return pl.pallas_call(
        _attn_ctx_kernel, grid=(n_seq,),
        in_specs=[pl.BlockSpec(memory_space=pltpu.SMEM), pl.BlockSpec((seq_len, cols), lambda b: (b, 0))],
        out_specs=pl.BlockSpec((seq_len, ATT_HEADS * HEAD_DIM), lambda b: (b, 0)),
        out_shape=jax.ShapeDtypeStruct((rows, ATT_HEADS * HEAD_DIM), BF16),
        compiler_params=_cparams("parallel"), name="attn_context",
    )(sink, qkv)


def _rope_block(x, cos, sin_signed):
    lane = lax.broadcasted_iota(jnp.int32, x.shape, 1)
    nf = HEAD_DIM // 4
    partner = jnp.where((lane % (2 * nf)) < nf, pltpu.roll(x, LANES - nf, axis=1), pltpu.roll(x, nf, axis=1))
    return x * cos + partner * sin_signed


def _attn_lat_kernel(sink_ref, qkv_ref, ck_ref, cv_ref, cos_ref, sin_ref, o_ref, k_scr):
    i = pl.program_id(1)
    t = qkv_ref.shape[0]
    qw = ATT_HEADS * HEAD_DIM
    kw = ATT_KV * HEAD_DIM
    span = Q_BLOCK + 2 * WINDOW

    @pl.when(i == 0)
    def _():
        for c in range(kw // LANES):
            blk = qkv_ref[:, qw + c * LANES:qw + (c + 1) * LANES]
            k_scr[:, c * LANES:(c + 1) * LANES] = _rope_block(blk, cos_ref[...], sin_ref[...]).astype(BF16)

    r0 = pl.multiple_of(i * Q_BLOCK, Q_BLOCK)
    ws = pl.multiple_of(jnp.clip(r0 - WINDOW, 0, t - span), Q_BLOCK)
    cos_q = cos_ref[pl.ds(r0, Q_BLOCK), :]
    sin_q = sin_ref[pl.ds(r0, Q_BLOCK), :]
    qpos = r0 + lax.broadcasted_iota(jnp.int32, (Q_BLOCK, span), 0)
    kpos = ws + lax.broadcasted_iota(jnp.int32, (Q_BLOCK, span), 1)
    band = jnp.abs(qpos - kpos) <= WINDOW
    band = jnp.concatenate([band] * ATT_GROUP, axis=0)
    groups = []
    for kv in range(ATT_KV):
        heads = []
        for g in range(ATT_GROUP):
            h = kv * ATT_GROUP + g
            c, half = divmod(h * HEAD_DIM, LANES)
            blk = _rope_block(qkv_ref[pl.ds(r0, Q_BLOCK), c * LANES:(c + 1) * LANES], cos_q, sin_q)
            heads.append(blk[:, half:half + HEAD_DIM])
        q = jnp.concatenate(heads, axis=0).astype(BF16)
        ck = ck_ref[0, :, kv * HEAD_DIM:(kv + 1) * HEAD_DIM].astype(BF16)
        cv = cv_ref[0, :, kv * HEAD_DIM:(kv + 1) * HEAD_DIM].astype(BF16)
        kwin = k_scr[pl.ds(ws, span), kv * HEAD_DIM:(kv + 1) * HEAD_DIM]
        vwin = qkv_ref[pl.ds(ws, span), qw + kw + kv * HEAD_DIM:qw + kw + (kv + 1) * HEAD_DIM].astype(BF16)
        s_ctx = _dot_nt(q, ck) * HEAD_DIM ** -0.5
        s_win = jnp.where(band, _dot_nt(q, kwin) * HEAD_DIM ** -0.5, -jnp.inf)
        groups.append(([s_ctx, s_win], [cv, vwin], _sink_column(sink_ref, kv, Q_BLOCK)))
    heads_out = [o[g * Q_BLOCK:(g + 1) * Q_BLOCK] for o in _softmax_av(groups) for g in range(ATT_GROUP)]
    o_ref[...] = jnp.concatenate(heads_out, axis=1).astype(o_ref.dtype)


def _rope_tables(seq_len):
    pos = jnp.arange(seq_len, dtype=jnp.int32)
    row = (pos // GRID_W).astype(F32)
    col = (pos % GRID_W).astype(F32)
    nf = HEAD_DIM // 4
    inv = ROPE_BASE ** (-jnp.arange(nf, dtype=F32) / nf)
    ang_r = row[:, None] * inv[None, :]
    ang_c = col[:, None] * inv[None, :]
    cos_h = jnp.concatenate([jnp.cos(ang_r), jnp.cos(ang_r), jnp.cos(ang_c), jnp.cos(ang_c)], axis=1)
    sin_h = jnp.concatenate([-jnp.sin(ang_r), jnp.sin(ang_r), -jnp.sin(ang_c), jnp.sin(ang_c)], axis=1)
    reps = LANES // HEAD_DIM
    return jnp.tile(cos_h, (1, reps)), jnp.tile(sin_h, (1, reps))


def _attn_lat(qkv, cache_k, cache_v, sink, n_seq, seq_len):
    rows, cols = qkv.shape
    past = cache_k.shape[1]
    kw = ATT_KV * HEAD_DIM
    cos, sin = _rope_tables(seq_len)
    return pl.pallas_call(
        _attn_lat_kernel, grid=(n_seq, seq_len // Q_BLOCK),
        in_specs=[pl.BlockSpec(memory_space=pltpu.SMEM),
                  pl.BlockSpec((seq_len, cols), lambda b, i: (b, 0)),
                  pl.BlockSpec((1, past, kw), lambda b, i: (b, 0, 0)),
                  pl.BlockSpec((1, past, kw), lambda b, i: (b, 0, 0)),
                  pl.BlockSpec((seq_len, LANES), lambda b, i: (0, 0)),
                  pl.BlockSpec((seq_len, LANES), lambda b, i: (0, 0))],
        out_specs=pl.BlockSpec((Q_BLOCK, ATT_HEADS * HEAD_DIM), lambda b, i: (b * (seq_len // Q_BLOCK) + i, 0)),
        out_shape=jax.ShapeDtypeStruct((rows, ATT_HEADS * HEAD_DIM), BF16),
        scratch_shapes=[pltpu.VMEM((seq_len, kw), BF16)],
        compiler_params=_cparams("parallel", "arbitrary"), name="attn_latent",
    )(sink, qkv, cache_k.reshape(n_seq, past, kw), cache_v.reshape(n_seq, past, kw), cos, sin)


def _split_bf16(w):
    hi = w.astype(BF16)
    return hi, (w - hi.astype(F32)).astype(BF16)


def _select_lanes(x, sel):
    x1 = x.astype(BF16)
    rest = x - x1.astype(F32)
    x2 = rest.astype(BF16)
    x3 = (rest - x2.astype(F32)).astype(BF16)
    return _dot(x1, sel) + _dot(x2, sel) + _dot(x3, sel)


def _ml_gates_kernel(h_ref, w_ref, wt_ref, b_ref, bt_ref, g_ref, gt_ref):
    h = h_ref[...]
    h_hi = h.astype(BF16)
    h_lo = (h - h_hi.astype(F32)).astype(BF16)
    ng = gt_ref.shape[0]
    by_hi = _dot(h_hi, w_ref[...])
    g_ref[...] = by_hi[:, :LANES] + by_hi[:, LANES:] + _dot(h_lo, w_ref[:, :LANES]) + b_ref[...]
    by_hi_t = _dot_nt(wt_ref[...], h_hi)
    gt_ref[...] = by_hi_t[:ng] + by_hi_t[ng:] + _dot_nt(wt_ref[:ng, :], h_lo) + bt_ref[...]


def _ml_gates(h, w_gates, b_gates, *, tm=512):
    rows, d = h.shape
    ng = w_gates.shape[1]
    w_hi, w_lo = _split_bf16(jnp.pad(w_gates, ((0, 0), (0, LANES - ng))))
    wt_hi, wt_lo = _split_bf16(w_gates.T)
    b_pad = jnp.pad(b_gates, (0, LANES - ng)).reshape(1, LANES)
    return pl.pallas_call(
        _ml_gates_kernel, grid=(rows // tm,),
        in_specs=[pl.BlockSpec((tm, d), lambda i: (i, 0)), pl.BlockSpec((d, 2 * LANES), lambda i: (0, 0)),
                  pl.BlockSpec((2 * ng, d), lambda i: (0, 0)), pl.BlockSpec((1, LANES), lambda i: (0, 0)),
                  pl.BlockSpec((ng, 1), lambda i: (0, 0))],
        out_specs=[pl.BlockSpec((tm, LANES), lambda i: (i, 0)), pl.BlockSpec((ng, tm), lambda i: (0, i))],
        out_shape=[jax.ShapeDtypeStruct((rows, LANES), F32), jax.ShapeDtypeStruct((ng, rows), F32)],
        compiler_params=_cparams("parallel"), name="mlstm_gates",
    )(h, jnp.concatenate([w_hi, w_lo], axis=1), jnp.concatenate([wt_hi, wt_lo], axis=0), b_pad,
      b_gates.reshape(ng, 1))


def _ml_qk_kernel(h_ref, w_ref, cw_ref, o_ref, *, seq_len, k_scale):
    j = pl.program_id(1)
    x = _dot(h_ref[...].astype(BF16), w_ref[...])
    t = x.shape[0]
    pos = lax.broadcasted_iota(jnp.int32, x.shape, 0) % seq_len
    prev = jnp.where(pos == 0, 0.0, pltpu.roll(x, 1, axis=0))
    nxt = jnp.where(pos == seq_len - 1, 0.0, pltpu.roll(x, t - 1, axis=0))
    y = prev * cw_ref[0:1, :] + x * cw_ref[1:2, :] + nxt * cw_ref[2:3, :]
    scale = jnp.where(j >= pl.num_programs(1) // 2, k_scale, 1.0).astype(F32)
    o_ref[...] = (_silu(y) * scale).astype(o_ref.dtype)


def _ml_qk(h, w_qk, conv_w, seq_len, *, tm=1024, tn=1024):
    rows, d = h.shape
    width = w_qk.shape[1]
    return pl.pallas_call(
        functools.partial(_ml_qk_kernel, seq_len=seq_len, k_scale=ML_DK ** -0.5), grid=(rows // tm, width // tn),
        in_specs=[pl.BlockSpec((tm, d), lambda i, j: (i, 0)), pl.BlockSpec((d, tn), lambda i, j: (0, j)),
                  pl.BlockSpec((3, tn), lambda i, j: (0, j))],
        out_specs=pl.BlockSpec((tm, tn), lambda i, j: (i, j)),
        out_shape=jax.ShapeDtypeStruct((rows, width), BF16),
        compiler_params=_cparams("parallel", "parallel"), name="mlstm_qk",
    )(h, w_qk, conv_w)


def _ml_scan_kernel(*refs, zero_init):
    refs = list(refs)
    dirs = [tuple(refs[0:5]), tuple(refs[5:10])]
    sel_ref = refs[10]
    refs = refs[11:]
    if not zero_init:
        c0_ref, n0_ref, m0_ref = refs[:3]
        refs = refs[3:]
    hf_ref, hb_ref, c_ref, n_ref, m_ref = refs
    h_out = (hf_ref, hb_ref)
    c = pl.program_id(1)
    last = pl.num_programs(1) - 1

    @pl.when(c == 0)
    def _():
        if zero_init:
            c_ref[...] = jnp.zeros_like(c_ref)
            n_ref[...] = jnp.zeros_like(n_ref)
            m_ref[...] = jnp.zeros_like(m_ref)
        else:
            c_ref[...] = c0_ref[...]
            n_ref[...] = n0_ref[...]
            m_ref[...] = m0_ref[...]

    length = hf_ref.shape[0]
    ti = lax.broadcasted_iota(jnp.int32, (length, length), 0)
    si = lax.broadcasted_iota(jnp.int32, (length, length), 1)
    causal = [ti >= si, ti <= si]
    edge = [length - 1, 0]
    b_row, i_rep, b_rep = [], [], []
    for d in range(2):
        g_ref, gt_ref = dirs[d][3], dirs[d][4]
        tri = jnp.where(causal[d], 1.0, 0.0).astype(F32)
        g_col = g_ref[...]
        f_row = _log_sigmoid(gt_ref[...])
        b_col = jnp.dot(tri, _log_sigmoid(g_col), precision=HI, preferred_element_type=F32)
        b_row.append(_dot_nt(f_row, tri, precision=HI))
        i_rep.append(_select_lanes(g_col, sel_ref[d, 0]))
        b_rep.append(_select_lanes(b_col, sel_ref[d, 1]))
    chains = [(d, h) for d in range(2) for h in range(ML_HEADS)]
    q = {(d, h): dirs[d][0][:, h * ML_DK:(h + 1) * ML_DK] for d, h in chains}
    k = {(d, h): dirs[d][1][:, h * ML_DK:(h + 1) * ML_DK] for d, h in chains}
    v = {(d, h): dirs[d][2][:, h * ML_DV:(h + 1) * ML_DV].astype(BF16) for d, h in chains}
    cst = {c: c_ref[0, c[0], c[1]] for c in chains}
    nst = {c: n_ref[0, c[0], c[1]] for c in chains}
    m_rep = {c: m_ref[0, c[0], c[1]] for c in chains}
    bc = {(d, h): b_rep[d][:, h * LANES:(h + 1) * LANES] for d, h in chains}
    i_col = {(d, h): i_rep[d][:, h * LANES:(h + 1) * LANES] for d, h in chains}
    qk_raw = {c: _dot_nt(q[c], k[c]) for c in chains}
    inter = {c: _dot_nt(q[c], jnp.concatenate([cst[c], jnp.broadcast_to(nst[c], (LANES, ML_DK))],
                                              axis=0).astype(BF16)) for c in chains}
    a_row = {}
    for d, h in chains:
        ji = d * 2 * ML_HEADS + h
        a_row[d, h] = dirs[d][4][ji:ji + 1, :] - b_row[d][ji + ML_HEADS:ji + ML_HEADS + 1, :]
    amat = {c: jnp.where(causal[c[0]], a_row[c], -jnp.inf) for c in chains}
    u = {c: jnp.maximum(m_rep[c], jnp.broadcast_to(jnp.max(amat[c], axis=1, keepdims=True), (length, LANES)))
         for c in chains}
    qk = {c: (qk_raw[c] * jnp.exp(amat[c] - u[c])).astype(BF16) for c in chains}
    intra = {c: _dot(qk[c], jnp.concatenate([v[c], jnp.ones((length, LANES), BF16)], axis=1)) for c in chains}
    b_last = {c: bc[c][edge[c[0]]:edge[c[0]] + 1, :] for c in chains}
    wlog_row = {c: b_last[c] + a_row[c] for c in chains}
    m_new = {c: jnp.maximum(b_last[c] + m_rep[c], jnp.max(wlog_row[c], axis=1, keepdims=True)) for c in chains}
    ws_row = {c: jnp.exp(wlog_row[c] - m_new[c]) for c in chains}
    kw = {c: (jnp.exp(b_last[c] - bc[c] + i_col[c] - m_new[c]) * k[c].astype(F32)).astype(BF16) for c in chains}
    c_upd = {c: _dot_tn(v[c], kw[c]) for c in chains}
    n_upd = {c: _dot(jnp.broadcast_to(ws_row[c], (8, length)).astype(BF16), k[c])[0:1] for c in chains}
    for c in chains:
        d, h = c
        sc = jnp.exp(m_rep[c] - u[c])
        tot = jnp.concatenate([sc] * (ML_DV // LANES + 1), axis=1) * inter[c] + intra[c]
        inv = 1.0 / jnp.maximum(jnp.abs(tot[:, ML_DV:]), jnp.exp(-(bc[c] + u[c])))
        h_out[d][:, h * ML_DV:(h + 1) * ML_DV] = tot[:, :ML_DV] * jnp.concatenate([inv] * (ML_DV // LANES), axis=1)
        decay = jnp.exp(b_last[c] + m_rep[c] - m_new[c])
        c_ref[0, d, h] = decay * cst[c] + c_upd[c]
        n_ref[0, d, h] = decay * nst[c] + n_upd[c]
        m_ref[0, d, h] = m_new[c]


def _ml_scan(qk, p, g, gt, state, n_seq, seq_len):
    rows = qk.shape[0]
    length = min(ML_CHUNK, seq_len)
    nc = seq_len // length
    qw = ML_HEADS * ML_DK
    vw = ML_HEADS * ML_DV
    ng = gt.shape[0]

    def fwd(b, c):
        return b * nc + c

    def bwd(b, c):
        return b * nc + nc - 1 - c

    args, specs = [], []
    for pos in (fwd, bwd):
        args += [qk, qk, p, g, gt]
        specs += [pl.BlockSpec((length, qw), lambda b, c, pos=pos: (pos(b, c), 0)),
                  pl.BlockSpec((length, qw), lambda b, c, pos=pos: (pos(b, c), 1)),
                  pl.BlockSpec((length, vw), lambda b, c, pos=pos: (pos(b, c), 0)),
                  pl.BlockSpec((length, LANES), lambda b, c, pos=pos: (pos(b, c), 0)),
                  pl.BlockSpec((ng, length), lambda b, c, pos=pos: (0, pos(b, c)))]
    assert length == LANES and ML_DK == LANES
    gate_lane = jnp.arange(LANES)[:, None]
    head = (jnp.arange(ML_HEADS * LANES) // LANES)[None, :]
    sel = jnp.stack([jnp.stack([gate_lane == (2 * d + kind) * ML_HEADS + head for kind in range(2)])
                     for d in range(2)]).astype(BF16)
    args.append(sel)
    specs.append(pl.BlockSpec(sel.shape, lambda b, c: (0, 0, 0, 0)))
    c_spec = pl.BlockSpec((1, 2, ML_HEADS, ML_DV, ML_DK), lambda b, c: (b, 0, 0, 0, 0))
    n_spec = pl.BlockSpec((1, 2, ML_HEADS, 1, ML_DK), lambda b, c: (b, 0, 0, 0, 0))
    zero_init = state is None
    if not zero_init:
        c0, n0, m0 = state
        args += [c0, n0.reshape(n_seq, 2, ML_HEADS, 1, ML_DK),
                 jnp.broadcast_to(m0[..., None, None], (n_seq, 2, ML_HEADS, 1, ML_DK))]
        specs += [c_spec, n_spec, n_spec]
    hf, hb, c_fin, n_fin, m_fin = pl.pallas_call(
        functools.partial(_ml_scan_kernel, zero_init=zero_init), grid=(n_seq, nc), in_specs=specs,
        out_specs=[pl.BlockSpec((length, vw), lambda b, c: (fwd(b, c), 0)),
                   pl.BlockSpec((length, vw), lambda b, c: (bwd(b, c), 0)), c_spec, n_spec, n_spec],
        out_shape=[jax.ShapeDtypeStruct((rows, vw), F32), jax.ShapeDtypeStruct((rows, vw), F32),
                   jax.ShapeDtypeStruct((n_seq, 2, ML_HEADS, ML_DV, ML_DK), F32),
                   jax.ShapeDtypeStruct((n_seq, 2, ML_HEADS, 1, ML_DK), F32),
                   jax.ShapeDtypeStruct((n_seq, 2, ML_HEADS, 1, ML_DK), F32)],
        compiler_params=_cparams("parallel", "arbitrary"), name="mlstm_scan",
    )(*args)
    return hf, hb, (c_fin, n_fin[:, :, :, 0, :], m_fin[:, :, :, 0, 0])


def _gla_scan_kernel(*refs, zero_init):
    refs = list(refs)
    dirs = [tuple(refs[0:4]), tuple(refs[4:8])]
    w2_ref, ba_ref = refs[8:10]
    refs = refs[10:]
    if not zero_init:
        s0_ref = refs.pop(0)
    of_ref, ob_ref, s_ref, st_scr, la_scr = refs
    o_out = (of_ref, ob_ref)
    c = pl.program_id(1)
    last = pl.num_programs(1) - 1
    kw = GLA_HEADS * GLA_DK
    n_sub = of_ref.shape[0] // GLA_SUB

    @pl.when(c == 0)
    def _():
        for d in range(2):
            for h in range(GLA_HEADS):
                st_scr[d, h] = jnp.zeros((GLA_DV, GLA_DK), F32) if zero_init else s0_ref[0, d, h].T

    for d in range(2):
        u = dirs[d][3][...].astype(BF16)
        z = _dot(u, w2_ref[:, d * kw:(d + 1) * kw]) + ba_ref[:, d * kw:(d + 1) * kw]
        la = _log_sigmoid(z) / GLA_TAU
        la_scr[d] = la
        totals = jnp.sum(la.reshape(n_sub, GLA_SUB, kw), axis=1)
        worst = jnp.min(totals) if d == 0 else jnp.minimum(worst, jnp.min(totals))
    decay_bounded = worst * LOG2E > -GLA_MAX_EXP2

    ti = lax.broadcasted_iota(jnp.int32, (GLA_SUB, GLA_SUB), 0)
    si = lax.broadcasted_iota(jnp.int32, (GLA_SUB, GLA_SUB), 1)
    s_lane = lax.broadcasted_iota(jnp.int32, (GLA_SUB, GLA_SUB), 1)

    def sub_chunk(j, carry, bounded):
        steps = [(d, t) for d in range(2) for t in range(GLA_STEPS)]
        chains = [(d, h, t) for d, t in steps for h in range(GLA_HEADS)]
        rows = {(d, t): pl.ds(pl.multiple_of(
            ((j * GLA_STEPS + t) if d == 0 else n_sub - 1 - (j * GLA_STEPS + t)) * GLA_SUB, GLA_SUB), GLA_SUB)
            for d, t in steps}
        causal = [ti >= si, ti <= si]
        edge = [GLA_SUB - 1, 0]
        bc_all = {(d, t): jnp.dot(jnp.where(causal[d], 1.0, 0.0).astype(F32), la_scr[d, rows[d, t], :], precision=HI,
                                  preferred_element_type=F32) * LOG2E for d, t in steps}
        q, k, v, bc2, b_last, k_dec, q_dec, a = {}, {}, {}, {}, {}, {}, {}, {}
        for c in chains:
            d, h, t = c
            q_ref, k_ref, v_ref, _ = dirs[d]
            bc2[c] = bc_all[d, t][:, h * GLA_DK:(h + 1) * GLA_DK]
            q[c] = q_ref[rows[d, t], h * GLA_DK:(h + 1) * GLA_DK] * GLA_DK ** -0.5
            k[c] = k_ref[rows[d, t], h * GLA_DK:(h + 1) * GLA_DK]
            v[c] = v_ref[rows[d, t], h * GLA_DV:(h + 1) * GLA_DV].astype(BF16)
            b_last[c] = bc2[c][edge[d]:edge[d] + 1, :]
            k_dec[c] = (k[c] * jnp.exp2(b_last[c] - bc2[c])).astype(BF16)
            q_dec[c] = q[c] * jnp.exp2(bc2[c])
        for c in chains:
            if bounded:
                a[c] = _dot_nt((q_dec[c] * jnp.exp2(-b_last[c])).astype(BF16), k_dec[c])
            else:
                acc = jnp.zeros((GLA_SUB, GLA_SUB), F32)
                for s in range(GLA_SUB):
                    decay = jnp.exp2(bc2[c] - bc2[c][s:s + 1, :])
                    col = jnp.sum(q[c] * (k[c][s:s + 1, :] * decay), axis=1, keepdims=True)
                    acc = jnp.where(s_lane == s, col, acc)
                a[c] = acc
        kv = {c: _dot_tn(v[c], k_dec[c]) for c in chains}
        intra = {c: _dot(jnp.where(causal[c[0]], a[c], 0.0).astype(BF16), v[c]) for c in chains}
        state = {(d, h): st_scr[d, h] for d in range(2) for h in range(GLA_HEADS)}
        for t in range(GLA_STEPS):
            now = [c for c in chains if c[2] == t]
            inter = {c: _dot_nt(q_dec[c].astype(BF16), state[c[0], c[1]].astype(BF16)) for c in now}
            for c in now:
                d, h, _ = c
                o_out[d][rows[d, t], h * GLA_DV:(h + 1) * GLA_DV] = intra[c] + inter[c]
                state[d, h] = jnp.exp2(b_last[c]) * state[d, h] + kv[c]
        for (d, h), st in state.items():
            st_scr[d, h] = st
        return carry

    @pl.when(decay_bounded)
    def _():
        lax.fori_loop(0, n_sub // GLA_STEPS, functools.partial(sub_chunk, bounded=True), 0)

    @pl.when(jnp.logical_not(decay_bounded))
    def _():
        lax.fori_loop(0, n_sub // GLA_STEPS, functools.partial(sub_chunk, bounded=False), 0)

    @pl.when(c == last)
    def _():
        for d in range(2):
            for h in range(GLA_HEADS):
                s_ref[0, d, h] = st_scr[d, h].T


def _gla_scan(p, u, w2, b_a, state, n_seq, seq_len):
    rows = p.shape[0]
    length = min(GLA_BLOCK, seq_len)
    nc = seq_len // length
    kw = GLA_HEADS * GLA_DK
    vw = GLA_HEADS * GLA_DV

    def fwd(b, c):
        return b * nc + c

    def bwd(b, c):
        return b * nc + nc - 1 - c

    args, specs = [], []
    for pos in (fwd, bwd):
        args += [p, p, p, u]
        specs += [pl.BlockSpec((length, kw), lambda b, c, pos=pos: (pos(b, c), 0)),
                  pl.BlockSpec((length, kw), lambda b, c, pos=pos: (pos(b, c), 1)),
                  pl.BlockSpec((length, vw), lambda b, c, pos=pos: (pos(b, c), 2 * kw // vw)),
                  pl.BlockSpec((length, LANES), lambda b, c, pos=pos: (pos(b, c), 0))]
    args += [w2, b_a]
    specs += [pl.BlockSpec(w2.shape, lambda b, c: (0, 0)), pl.BlockSpec(b_a.shape, lambda b, c: (0, 0))]
    s_spec = pl.BlockSpec((1, 2, GLA_HEADS, GLA_DK, GLA_DV), lambda b, c: (b, 0, 0, 0, 0))
    zero_init = state is None
    if not zero_init:
        args.append(state)
        specs.append(s_spec)
    return pl.pallas_call(
        functools.partial(_gla_scan_kernel, zero_init=zero_init), grid=(n_seq, nc), in_specs=specs,
        out_specs=[pl.BlockSpec((length, vw), lambda b, c: (fwd(b, c), 0)),
                   pl.BlockSpec((length, vw), lambda b, c: (bwd(b, c), 0)), s_spec],
        out_shape=[jax.ShapeDtypeStruct((rows, vw), F32), jax.ShapeDtypeStruct((rows, vw), F32),
                   jax.ShapeDtypeStruct((n_seq, 2, GLA_HEADS, GLA_DK, GLA_DV), F32)],
        scratch_shapes=[pltpu.VMEM((2, GLA_HEADS, GLA_DV, GLA_DK), F32), pltpu.VMEM((2, length, kw), F32)],
        compiler_params=_cparams("parallel", "arbitrary"), name="gla_scan",
    )(*args)


def kernel(x_prompt, x_sample, cache_k_0, cache_v_0, state_mlstm_C_1, state_mlstm_n_1, state_mlstm_m_1, state_gla_S_2, cache_k_3, cache_v_3, c, c_ctx, w_mod, b_mod, norm1_g, norm2_g, final_g, router_w, router_b, moe_wg, moe_wu, moe_wd, attn0_w_qkv, attn0_sink, attn0_w_o, mlstm1_w_in, mlstm1_b_gates, mlstm1_conv, mlstm1_norm_g, mlstm1_w_out, gla2_w_in, gla2_w_a1, gla2_w_a2, gla2_b_a, gla2_norm_g, gla2_w_out, attn3_w_qkv, attn3_sink, attn3_w_o):
    n_ctx, ctx_len, d = x_prompt.shape
    n_lat, lat_len, _ = x_sample.shape
    depth = w_mod.shape[0]

    cvec = jnp.concatenate([c_ctx[None, :], c, jnp.zeros((8 - 1 - n_lat, d), F32)], axis=0)
    mod = _modulation(cvec, w_mod, b_mod).reshape(depth, 8, 6, 1, d)

    def mods(layer, kind, latent):
        return mod[layer, 1:1 + n_lat, kind] if latent else mod[layer, 0:1, kind]

    rw_hi = router_w.T.astype(BF16)
    rw_lo = (router_w.T - rw_hi.astype(F32)).astype(BF16)
    rwt = jnp.concatenate([rw_hi, rw_lo], axis=0)
    rb = router_b.reshape(-1, 1)
    attn_w = {0: (attn0_w_qkv.astype(BF16), attn0_sink, attn0_w_o.astype(BF16), cache_k_0, cache_v_0),
              3: (attn3_w_qkv.astype(BF16), attn3_sink, attn3_w_o.astype(BF16), cache_k_3, cache_v_3)}
    ml_qw = ML_HEADS * ML_DK
    ml_vw = ML_HEADS * ML_DV
    ml_main = 2 * ml_qw + 2 * ml_vw
    ml_w_qk = mlstm1_w_in[:, :2 * ml_qw].astype(BF16)
    ml_w_vo = mlstm1_w_in[:, 2 * ml_qw:ml_main].astype(BF16)
    ml_w_gates = mlstm1_w_in[:, ml_main:]
    ml_w_out = mlstm1_w_out.astype(BF16)
    gla_kw = GLA_HEADS * GLA_DK
    gla_w_in = gla2_w_in.astype(BF16)
    gla_w_a1 = jnp.pad(jnp.concatenate([gla2_w_a1[0], gla2_w_a1[1]], axis=1),
                       ((0, 0), (0, LANES - 2 * GLA_RANK))).astype(BF16)
    gla_w2 = jnp.zeros((LANES, 2 * gla_kw), F32)
    gla_w2 = gla_w2.at[:GLA_RANK, :gla_kw].set(gla2_w_a2[0]).at[GLA_RANK:2 * GLA_RANK, gla_kw:].set(gla2_w_a2[1])
    gla_w2 = gla_w2.astype(BF16)
    gla_ba = gla2_b_a.reshape(1, 2 * gla_kw)
    gla_w_out = gla2_w_out.astype(BF16)

    new_state = []

    def mixer(layer, s, count0):
        latent, n_seq, seq_len, x, h = s["latent"], s["n_seq"], s["seq_len"], s["x"], s["h"]
        tail = (mods(layer, 2, latent), norm2_g[layer], mods(layer, 4, latent), mods(layer, 3, latent), rwt, rb,
                count0)
        kind = layer % 3
        if kind == 0:
            w_qkv, sink, w_o, ck, cv = attn_w[layer]
            qkv = _matmul(h, w_qkv)
            if latent:
                att = _attn_lat(qkv, ck, cv, sink, n_seq, seq_len)
            else:
                att = _attn_ctx(qkv, sink, n_seq, seq_len)
                qw = ATT_HEADS * HEAD_DIM
                kw = ATT_KV * HEAD_DIM
                new_state.append(qkv[:, qw:qw + kw].reshape(n_seq, seq_len, ATT_KV, HEAD_DIM))
                new_state.append(qkv[:, qw + kw:].reshape(n_seq, seq_len, ATT_KV, HEAD_DIM))
            return _proj("plain", (att,), w_o, x, *tail)
        if kind == 1:
            p = _matmul(h, ml_w_vo)
            g, gt = _ml_gates(h, ml_w_gates, mlstm1_b_gates)
            qk = _ml_qk(h, ml_w_qk, mlstm1_conv, seq_len)
            st = (state_mlstm_C_1, state_mlstm_n_1, state_mlstm_m_1) if latent else None
            hf, hb, fin = _ml_scan(qk, p, g, gt, st, n_seq, seq_len)
            if not latent:
                new_state.extend(fin)
            return _proj("mlstm", (hf, hb, p, 1, mlstm1_norm_g), ml_w_out, x, *tail)
        p = _matmul(h, gla_w_in)
        u = _matmul(h, gla_w_a1)
        of, ob, s_fin = _gla_scan(p, u, gla_w2, gla_ba, state_gla_S_2 if latent else None, n_seq, seq_len)
        if not latent:
            new_state.append(s_fin)
        gla_vw = GLA_HEADS * GLA_DV
        return _proj("gla", (of, ob, p, (2 * gla_kw + gla_vw) // gla_vw, gla2_norm_g), gla_w_out, x, *tail)

    streams = [dict(latent=False, n_seq=n_ctx, seq_len=ctx_len, x=x_prompt.reshape(n_ctx * ctx_len, d)),
               dict(latent=True, n_seq=n_lat, seq_len=lat_len, x=x_sample.reshape(n_lat * lat_len, d))]
    for s in streams:
        s["h"] = _rownorm(s["x"], norm1_g[0], mods(0, 1, s["latent"]), mods(0, 0, s["latent"]))
    for layer in range(depth):
        counts = jnp.zeros((N_EXPERTS, LANES), jnp.int32)
        for s in streams:
            s["x"], s["h2"], s["meta"], s["wcol"], counts = mixer(layer, s, counts)
        cnt = counts[:, 0]
        for s in streams:
            s["slots"] = _slots(s["meta"], cnt)
        xs, info = _dispatch([s["h2"] for s in streams], jnp.concatenate([s["slots"] for s in streams], axis=1), cnt)
        ys = _ffn(xs, info, moe_wg, moe_wu, moe_wd, layer)
        for s in streams:
            latent = s["latent"]
            gate2 = mods(layer, 5, latent)
            if layer + 1 < depth:
                s["x"], s["h"] = _combine(s["x"], ys, s["slots"], s["wcol"], gate2, norm1_g[layer + 1],
                                          mod=(mods(layer + 1, 1, latent), mods(layer + 1, 0, latent)), out_x=True,
                                          h_dtype=F32 if (layer + 1) % 3 == 1 else BF16)
            else:
                s["out"] = _combine(s["x"], ys, s["slots"], s["wcol"], gate2, final_g, h_dtype=F32)
    y_prompt = streams[0]["out"].reshape(n_ctx, ctx_len, d)
    y_sample = streams[1]["out"].reshape(n_lat, lat_len, d)
    return (y_prompt, y_sample, *new_state)
```

```python
import functools

import jax
import jax.numpy as jnp
from jax import lax
from jax.experimental import pallas as pl
from jax.experimental.pallas import tpu as pltpu

F32 = jnp.float32
BF16 = jnp.bfloat16
HI = lax.Precision.HIGHEST

EPS = 1e-6
LOG2E = 1.4426950408889634
GRID_W = 64
ATT_HEADS = 16
ATT_KV = 4
ATT_GROUP = ATT_HEADS // ATT_KV
HEAD_DIM = 64
WINDOW = 128
Q_BLOCK = 128
ROPE_BASE = 10000.0
ML_HEADS = 8
ML_DK = 128
ML_DV = 256
ML_CHUNK = 128
GLA_HEADS = 4
GLA_DK = 128
GLA_DV = 256
GLA_RANK = 16
GLA_TAU = 16.0
GLA_SUB = 16
GLA_BLOCK = 256
GLA_STEPS = 8
GLA_MAX_EXP2 = 80.0
N_EXPERTS = 16
N_GROUPS = 4
GROUP_SIZE = N_EXPERTS // N_GROUPS
LANES = 128
VMEM_LIMIT = 56 * 1024 * 1024


def _cparams(*sem):
    return pltpu.CompilerParams(dimension_semantics=sem, vmem_limit_bytes=VMEM_LIMIT)


def _dot(a, b):
    return jnp.dot(a, b, preferred_element_type=F32)


def _dot_nt(a, b, precision=None):
    return lax.dot_general(a, b, (((1,), (1,)), ((), ())), precision=precision, preferred_element_type=F32)


def _dot_tn(a, b):
    return lax.dot_general(a, b, (((0,), (0,)), ((), ())), preferred_element_type=F32)


def _sigmoid(x):
    return 1.0 / (1.0 + jnp.exp(-x))


def _silu(x):
    return x * _sigmoid(x)


def _log_sigmoid(x):
    return jnp.minimum(x, 0.0) - jnp.log(1.0 + jnp.exp(-jnp.abs(x)))


def _rms_rows(x, g):
    ms = jnp.mean(x * x, axis=-1, keepdims=True)
    return x * lax.rsqrt(ms + EPS) * g


def _mod_kernel(c_ref, w_ref, b_ref, o_ref):
    s = _silu(c_ref[...])
    o_ref[0] = _dot(s.astype(BF16), w_ref[0].astype(BF16)) + b_ref[0]


def _modulation(cvec, w_mod, b_mod):
    depth, d, n6 = w_mod.shape
    tn = 1536
    return pl.pallas_call(
        _mod_kernel,
        grid=(depth, n6 // tn),
        in_specs=[pl.BlockSpec((8, d), lambda l, j: (0, 0)),
                  pl.BlockSpec((1, d, tn), lambda l, j: (l, 0, j)),
                  pl.BlockSpec((1, 1, tn), lambda l, j: (l, 0, j))],
        out_specs=pl.BlockSpec((1, 8, tn), lambda l, j: (l, 0, j)),
        out_shape=jax.ShapeDtypeStruct((depth, 8, n6), F32),
        compiler_params=_cparams("parallel", "parallel"),
        name="adaln_modulation",
    )(cvec, w_mod, b_mod.reshape(depth, 1, n6))


def _route(h, rwt, rb, carry):
    tm = h.shape[0]
    h_hi = h.astype(BF16)
    h_lo = (h - h_hi.astype(F32)).astype(BF16)
    by_hi = _dot_nt(rwt, h_hi)
    logits = by_hi[:N_EXPERTS] + by_hi[N_EXPERTS:] + _dot_nt(rwt[:N_EXPERTS], h_lo)
    scores = _sigmoid(logits)
    sel = scores + rb
    expert = lax.broadcasted_iota(jnp.int32, sel.shape, 0)
    pos = expert % GROUP_SIZE
    grp = expert // GROUP_SIZE

    def mate(x, k):
        ahead = pltpu.roll(x, N_EXPERTS - k, axis=0)
        behind = pltpu.roll(x, GROUP_SIZE - k, axis=0)
        return jnp.where(pos + k < GROUP_SIZE, ahead, behind)

    beaten = jnp.zeros_like(sel)
    for k in range(1, GROUP_SIZE):
        other = mate(sel, k)
        other_first = (pos + k) % GROUP_SIZE < pos
        beaten = beaten + jnp.where(other_first, jnp.where(other >= sel, 1.0, 0.0), jnp.where(other > sel, 1.0, 0.0))
    top2 = jnp.where(beaten < 2.0, sel, 0.0)
    gscore = top2
    for k in range(1, GROUP_SIZE):
        gscore = gscore + mate(top2, k)
    lost = jnp.zeros_like(sel)
    for k in range(1, N_GROUPS):
        other = pltpu.roll(gscore, N_EXPERTS - GROUP_SIZE * k, axis=0)
        other_first = (grp + k) % N_GROUPS < grp
        lost = lost + jnp.where(other_first, jnp.where(other >= gscore, 1.0, 0.0),
                                jnp.where(other > gscore, 1.0, 0.0))
    picked = jnp.where(lost < 0.5, jnp.where(beaten < 2.0, 1.0, 0.0), 0.0)
    chosen = picked > 0.5
    weight = jnp.where(chosen, scores, 0.0)
    wsum = jnp.sum(weight, axis=0, keepdims=True)
    e_f = expert.astype(F32)
    e_a = jnp.min(jnp.where(chosen, e_f, float(N_EXPERTS)), axis=0, keepdims=True)
    e_b = jnp.max(jnp.where(chosen, e_f, -1.0), axis=0, keepdims=True)
    before = (lax.broadcasted_iota(jnp.int32, (tm, tm), 0) < lax.broadcasted_iota(jnp.int32, (tm, tm), 1))
    rank = _dot(picked.astype(BF16), jnp.where(before, 1.0, 0.0).astype(BF16)) + carry
    is_a = e_f == e_a
    is_b = e_f == e_b
    r_a = jnp.sum(jnp.where(is_a, rank, 0.0), axis=0, keepdims=True)
    r_b = jnp.sum(jnp.where(is_b, rank, 0.0), axis=0, keepdims=True)
    w_a = jnp.sum(jnp.where(is_a, weight, 0.0), axis=0, keepdims=True)
    w_b = jnp.sum(jnp.where(is_b, weight, 0.0), axis=0, keepdims=True)
    meta = jnp.concatenate([e_a, e_b, r_a, r_b, jnp.zeros((4, tm), F32)], axis=0).astype(jnp.int32)
    wcol = jnp.concatenate([w_a / wsum, w_b / wsum, jnp.zeros((LANES - 2, tm), F32)], axis=0).T
    return meta, wcol, carry + jnp.sum(picked, axis=1, keepdims=True)


def _norm_mod(x, g_ref, mod_refs):
    h = _rms_rows(x, g_ref[...])
    if mod_refs is not None:
        a_ref, s_ref = mod_refs
        h = h * (1.0 + a_ref[0]) + s_ref[0]
    return h


def _rownorm_kernel(x_ref, g_ref, a_ref, s_ref, h_ref):
    h_ref[...] = _norm_mod(x_ref[...], g_ref, (a_ref, s_ref)).astype(h_ref.dtype)


def _mod_spec(n_mod, rows, tm, d):
    per = (rows // n_mod) // tm
    return pl.BlockSpec((1, 1, d), lambda i, *_: (i // per, 0, 0))


def _rownorm(x, g, scale, shift, *, tm=512):
    rows, d = x.shape
    row_spec = pl.BlockSpec((tm, d), lambda i: (i, 0))
    return pl.pallas_call(
        _rownorm_kernel, grid=(rows // tm,),
        in_specs=[row_spec, pl.BlockSpec((1, d), lambda i: (0, 0)), _mod_spec(scale.shape[0], rows, tm, d),
                  _mod_spec(shift.shape[0], rows, tm, d)],
        out_specs=row_spec, out_shape=jax.ShapeDtypeStruct((rows, d), BF16),
        compiler_params=_cparams("parallel"), name="rownorm",
    )(x, g.reshape(1, d), scale, shift)


def _mm_kernel(a_ref, w_ref, o_ref):
    o_ref[...] = _dot(a_ref[...].astype(BF16), w_ref[...]).astype(o_ref.dtype)


def _matmul(a, w, *, out_dtype=F32, tm=1024):
    m, k = a.shape
    n = w.shape[1]
    tn = next(t for t in (1024, 768, 512, LANES) if n % t == 0)
    return pl.pallas_call(
        _mm_kernel, grid=(m // tm, n // tn),
        in_specs=[pl.BlockSpec((tm, k), lambda i, j: (i, 0)), pl.BlockSpec((k, tn), lambda i, j: (0, j))],
        out_specs=pl.BlockSpec((tm, tn), lambda i, j: (i, j)),
        out_shape=jax.ShapeDtypeStruct((m, n), out_dtype),
        compiler_params=_cparams("parallel", "parallel"), name="matmul",
    )(a, w)


def _head_norm(x, g, n_heads, dv):
    outs = []
    for h in range(n_heads):
        xs = x[:, h * dv:(h + 1) * dv]
        ms = jnp.mean(xs * xs, axis=-1, keepdims=True)
        outs.append(xs * lax.rsqrt(ms + EPS) * g[:, h * dv:(h + 1) * dv])
    return jnp.concatenate(outs, axis=1)


def _proj_kernel(*refs, pre):
    refs = list(refs)
    if pre == "plain":
        a = refs.pop(0)[...]
    else:
        f_ref, b_ref, p_ref, hg_ref = refs.pop(0), refs.pop(0), refs.pop(0), refs.pop(0)
        hsum = f_ref[...] + b_ref[...]
        if pre == "mlstm":
            a = _sigmoid(p_ref[...]) * _head_norm(hsum, hg_ref[...], ML_HEADS, ML_DV)
        else:
            a = _head_norm(hsum, hg_ref[...], GLA_HEADS, GLA_DV) * _silu(p_ref[...])
        a = a.astype(BF16)
    w_ref, x_ref, gate_ref, g_ref, a_ref, s_ref, rwt_ref, rb_ref, count0_ref = refs[:9]
    xo_ref, h_ref, meta_ref, wcol_ref, count_ref, carry_ref = refs[9:]

    @pl.when(pl.program_id(0) == 0)
    def _():
        carry_ref[...] = count0_ref[...].astype(F32)

    x = x_ref[...] + gate_ref[0] * _dot(a, w_ref[...])
    xo_ref[...] = x
    h = _norm_mod(x, g_ref, (a_ref, s_ref))
    _rows_to_tiles(h_ref, h)
    meta, wcol, carry = _route(h, rwt_ref[...], rb_ref[...], carry_ref[:, 0:1])
    meta_ref[...] = meta
    wcol_ref[...] = wcol
    carry_ref[...] = jnp.broadcast_to(carry, carry_ref.shape)
    count_ref[...] = jnp.broadcast_to(carry, count_ref.shape).astype(jnp.int32)


def _proj(pre, pre_args, w_out, x, gate, g, scale, shift, rwt, rb, count0, *, tm=512):
    rows, d = x.shape
    k = w_out.shape[0]
    row_spec = pl.BlockSpec((tm, d), lambda i: (i, 0))
    if pre == "plain":
        args, specs = [pre_args[0]], [pl.BlockSpec((tm, k), lambda i: (i, 0))]
    else:
        hf, hb, p, col_block, hg = pre_args
        wide = pl.BlockSpec((tm, k), lambda i: (i, 0))
        args = [hf, hb, p, hg.reshape(1, k)]
        specs = [wide, wide, pl.BlockSpec((tm, k), lambda i: (i, col_block)), pl.BlockSpec((1, k), lambda i: (0, 0))]
    args += [w_out, x, gate, g.reshape(1, d), scale, shift, rwt, rb, count0]
    specs += [pl.BlockSpec((k, d), lambda i: (0, 0)), row_spec, _mod_spec(gate.shape[0], rows, tm, d),
              pl.BlockSpec((1, d), lambda i: (0, 0)), _mod_spec(scale.shape[0], rows, tm, d),
              _mod_spec(shift.shape[0], rows, tm, d), pl.BlockSpec(rwt.shape, lambda i: (0, 0)),
              pl.BlockSpec(rb.shape, lambda i: (0, 0)), pl.BlockSpec(count0.shape, lambda i: (0, 0))]
    return pl.pallas_call(
        functools.partial(_proj_kernel, pre=pre), grid=(rows // tm,), in_specs=specs,
        out_specs=[row_spec, pl.BlockSpec((tm * SUBLANES, LANES), lambda i: (i, 0)),
                   pl.BlockSpec((8, tm), lambda i: (0, i)),
                   pl.BlockSpec((tm, LANES), lambda i: (i, 0)), pl.BlockSpec((N_EXPERTS, LANES), lambda i: (0, 0))],
        out_shape=[jax.ShapeDtypeStruct((rows, d), F32), jax.ShapeDtypeStruct((rows * SUBLANES, LANES), F32),
                   jax.ShapeDtypeStruct((8, rows), jnp.int32), jax.ShapeDtypeStruct((rows, LANES), F32),
                   jax.ShapeDtypeStruct((N_EXPERTS, LANES), jnp.int32)],
        scratch_shapes=[pltpu.VMEM((N_EXPERTS, LANES), F32)],
        compiler_params=_cparams("arbitrary"), name="proj_" + pre,
    )(*args)


MOE_TILE = 512
MOE_TILE_SHIFT = 9
MOE_TOKENS = 256
FFN_PARTS = 2


SUBLANES = 8


def _rows_to_tiles(ref, x, lead=(), first=0):
    rows = x.shape[0]
    for c in range(SUBLANES):
        ref[(*lead, pl.ds(first * SUBLANES + c, rows, stride=SUBLANES), slice(None))] = x[:, c * LANES:(c + 1) * LANES]


def _tiles_to_rows(ref, rows, lead=(), first=0):
    return jnp.concatenate([ref[(*lead, pl.ds(first * SUBLANES + c, rows, stride=SUBLANES), slice(None))]
                            for c in range(SUBLANES)], axis=1)


def _slot_tiles(rows):
    return (2 * rows) // MOE_TILE + N_EXPERTS


def _expert_offsets(cnt_ref, off_ref):
    def per_expert(e, k):
        off_ref[e] = k * MOE_TILE
        return k + ((cnt_ref[e] + MOE_TILE - 1) >> MOE_TILE_SHIFT)
    return lax.fori_loop(0, N_EXPERTS, per_expert, 0)


def _slots_kernel(cnt_ref, meta_ref, slot_ref, off_ref):
    @pl.when(pl.program_id(0) == 0)
    def _():
        _expert_offsets(cnt_ref, off_ref)

    e_a, e_b = meta_ref[0:1, :], meta_ref[1:2, :]
    off_a = jnp.zeros_like(e_a)
    off_b = jnp.zeros_like(e_b)
    for e in range(N_EXPERTS):
        off_a = jnp.where(e_a == e, off_ref[e], off_a)
        off_b = jnp.where(e_b == e, off_ref[e], off_b)
    slot_ref[...] = jnp.concatenate([off_a + meta_ref[2:3, :], off_b + meta_ref[3:4, :],
                                     jnp.zeros((6, e_a.shape[1]), jnp.int32)], axis=0)


def _slots(meta, counts, *, tm=1024):
    rows = meta.shape[1]
    grid_spec = pltpu.PrefetchScalarGridSpec(
        num_scalar_prefetch=1, grid=(rows // tm,),
        in_specs=[pl.BlockSpec((8, tm), lambda i, cnt: (0, i))],
        out_specs=pl.BlockSpec((8, tm), lambda i, cnt: (0, i)),
        scratch_shapes=[pltpu.SMEM((N_EXPERTS,), jnp.int32)])
    return pl.pallas_call(
        _slots_kernel, grid_spec=grid_spec, out_shape=jax.ShapeDtypeStruct((8, rows), jnp.int32),
        compiler_params=_cparams("arbitrary"), name="moe_slots",
    )(counts, meta)


def _dispatch_kernel(*refs, steps):
    sa_ref, sb_ref, cnt_ref = refs[:3]
    h_refs = refs[3:3 + len(steps)]
    xs_ref, info_ref, off_ref, zero_ref, sem = refs[3 + len(steps):]
    i = pl.program_id(0)
    tm = h_refs[0].shape[0] // SUBLANES
    n_tiles = info_ref.shape[0] - 1
    tile_rows = MOE_TILE * SUBLANES

    def tile_copy(tile):
        return pltpu.make_async_copy(zero_ref, xs_ref.at[pl.ds(tile * tile_rows, tile_rows), :], sem)

    @pl.when(i == 0)
    def _():
        zero_ref[...] = jnp.zeros_like(zero_ref)
        used = _expert_offsets(cnt_ref, off_ref)

        def per_expert(e, _):
            first = off_ref[e] >> MOE_TILE_SHIFT
            nt = (cnt_ref[e] + MOE_TILE - 1) >> MOE_TILE_SHIFT

            def fill(j, _):
                info_ref[first + j] = e
                return 0
            lax.fori_loop(0, nt, fill, 0)

            @pl.when(nt > 0)
            def _():
                tile_copy(first + nt - 1).start()
                tile_copy(first + nt - 1).wait()
            return 0
        lax.fori_loop(0, N_EXPERTS, per_expert, 0)
        info_ref[n_tiles] = used

        def tail(j, _):
            info_ref[j] = N_EXPERTS - 1
            tile_copy(j).start()
            tile_copy(j).wait()
            return 0
        lax.fori_loop(used, n_tiles, tail, 0)

    base = i * tm

    def copy_rows(h_ref):
        def row_copy(t, slot):
            dst = pl.multiple_of(slot * SUBLANES, SUBLANES)
            return pltpu.make_async_copy(h_ref.at[pl.ds(t * SUBLANES, SUBLANES), :],
                                         xs_ref.at[pl.ds(dst, SUBLANES), :], sem)

        for t in range(tm):
            row_copy(t, sa_ref[base + t]).start(priority=0)
            row_copy(t, sb_ref[base + t]).start(priority=1)
        for _ in range(2):
            pltpu.make_async_copy(h_ref, xs_ref.at[pl.ds(0, tm * SUBLANES), :], sem).wait()

    first = 0
    for h_ref, n in zip(h_refs, steps):
        pl.when(jnp.logical_and(i >= first, i < first + n))(functools.partial(copy_rows, h_ref))
        first += n


def _dispatch(hs, slots, counts):
    tm = MOE_TOKENS
    steps = tuple(h.shape[0] // (tm * SUBLANES) for h in hs)
    n_tiles = _slot_tiles(sum(steps) * tm)
    specs, first = [], 0
    for n in steps:
        specs.append(pl.BlockSpec((tm * SUBLANES, LANES),
                                  lambda i, *_, first=first, n=n: (jnp.clip(i - first, 0, n - 1), 0)))
        first += n
    grid_spec = pltpu.PrefetchScalarGridSpec(
        num_scalar_prefetch=3, grid=(sum(steps),), in_specs=specs,
        out_specs=[pl.BlockSpec(memory_space=pl.ANY), pl.BlockSpec(memory_space=pltpu.SMEM)],
        scratch_shapes=[pltpu.SMEM((N_EXPERTS,), jnp.int32), pltpu.VMEM((MOE_TILE * SUBLANES, LANES), F32),
                        pltpu.SemaphoreType.DMA(())])
    return pl.pallas_call(
        functools.partial(_dispatch_kernel, steps=steps), grid_spec=grid_spec,
        out_shape=[jax.ShapeDtypeStruct((n_tiles * MOE_TILE * SUBLANES, LANES), F32),
                   jax.ShapeDtypeStruct((n_tiles + 1,), jnp.int32)],
        compiler_params=_cparams("arbitrary"), name="moe_dispatch",
    )(slots[0], slots[1], counts, *hs)


def _ffn_kernel(info_ref, xs_ref, wg_ref, wu_ref, wd_ref, ys_ref, wg_s, wu_s, wd_s):
    i = pl.program_id(0)
    used = info_ref[info_ref.shape[0] - 1]
    fresh = jnp.logical_or(i == 0, info_ref[i] != info_ref[jnp.maximum(i - 1, 0)])

    @pl.when(jnp.logical_and(i < used, fresh))
    def _():
        wg_s[...] = wg_ref[0, 0].astype(BF16)
        wu_s[...] = wu_ref[0, 0].astype(BF16)
        wd_s[...] = wd_ref[0, 0].astype(BF16)

    @pl.when(i < used)
    def _():
        part = MOE_TILE // FFN_PARTS
        x = [_tiles_to_rows(xs_ref, part, first=p * part).astype(BF16) for p in range(FFN_PARTS)]
        gate = [_dot(xp, wg_s[...]) for xp in x]
        up = [_dot(xp, wu_s[...]) for xp in x]
        hid = [(_silu(g) * u).astype(BF16) for g, u in zip(gate, up)]
        y = [_dot(hp, wd_s[...]) for hp in hid]
        for p in range(FFN_PARTS):
            _rows_to_tiles(ys_ref, y[p], first=p * part)

    @pl.when(i >= used)
    def _():
        ys_ref[...] = jnp.zeros_like(ys_ref)


def _ffn(xs, info, wg, wu, wd, layer):
    tile_rows = MOE_TILE * SUBLANES
    n_tiles = xs.shape[0] // tile_rows
    d, f = wg.shape[2:]

    def w_map(i, info):
        return (layer, info[i], 0, 0)

    grid_spec = pltpu.PrefetchScalarGridSpec(
        num_scalar_prefetch=1, grid=(n_tiles,),
        in_specs=[pl.BlockSpec((tile_rows, LANES), lambda i, info: (jnp.minimum(i, info[n_tiles] - 1), 0)),
                  pl.BlockSpec((1, 1, d, f), w_map), pl.BlockSpec((1, 1, d, f), w_map),
                  pl.BlockSpec((1, 1, f, d), w_map)],
        out_specs=pl.BlockSpec((tile_rows, LANES), lambda i, info: (i, 0)),
        scratch_shapes=[pltpu.VMEM((d, f), BF16), pltpu.VMEM((d, f), BF16), pltpu.VMEM((f, d), BF16)])
    return pl.pallas_call(
        _ffn_kernel, grid_spec=grid_spec, out_shape=jax.ShapeDtypeStruct(xs.shape, F32),
        compiler_params=_cparams("arbitrary"), name="moe_ffn",
    )(info, xs, wg, wu, wd)


def _combine_kernel(*refs, has_mod, out_x):
    refs = list(refs)
    sa_ref, sb_ref, x_ref, ys_ref, wcol_ref, gate_ref, g_ref = refs[:7]
    refs = refs[7:]
    mod_refs = (refs.pop(0), refs.pop(0)) if has_mod else None
    xo_ref = refs.pop(0) if out_x else None
    h_ref, buf_a, buf_b, sems = refs
    i = pl.program_id(0)
    tm = x_ref.shape[0]

    def issue(tile, slot):
        base = tile * tm
        for t in range(tm):
            dst = pl.ds(t * SUBLANES, SUBLANES)
            src_a = pl.multiple_of(sa_ref[base + t] * SUBLANES, SUBLANES)
            src_b = pl.multiple_of(sb_ref[base + t] * SUBLANES, SUBLANES)
            pltpu.make_async_copy(ys_ref.at[pl.ds(src_a, SUBLANES), :], buf_a.at[slot, dst, :],
                                  sems.at[slot]).start(priority=0)
            pltpu.make_async_copy(ys_ref.at[pl.ds(src_b, SUBLANES), :], buf_b.at[slot, dst, :],
                                  sems.at[slot]).start(priority=1)

    @pl.when(i == 0)
    def _():
        issue(0, 0)

    @pl.when(i + 1 < pl.num_programs(0))
    def _():
        issue(i + 1, (i + 1) % 2)

    slot = i % 2
    for buf in (buf_a, buf_b):
        pltpu.make_async_copy(ys_ref.at[pl.ds(0, tm * SUBLANES), :], buf.at[slot], sems.at[slot]).wait()
    y = (wcol_ref[:, 0:1] * _tiles_to_rows(buf_a, tm, lead=(slot,))
         + wcol_ref[:, 1:2] * _tiles_to_rows(buf_b, tm, lead=(slot,)))
    x = x_ref[...] + gate_ref[0] * y
    if out_x:
        xo_ref[...] = x
    h_ref[...] = _norm_mod(x, g_ref, mod_refs).astype(h_ref.dtype)


def _combine(x, ys, slots, wcol, gate, g, *, mod=None, out_x=False, h_dtype=BF16):
    rows, d = x.shape
    tm = MOE_TOKENS
    row_spec = pl.BlockSpec((tm, d), lambda i, *_: (i, 0))
    args = [x, ys, wcol, gate, g.reshape(1, d)]
    specs = [row_spec, pl.BlockSpec(memory_space=pl.ANY), pl.BlockSpec((tm, LANES), lambda i, *_: (i, 0)),
             _mod_spec(gate.shape[0], rows, tm, d), pl.BlockSpec((1, d), lambda i, *_: (0, 0))]
    if mod is not None:
        for m in mod:
            args.append(m)
            specs.append(_mod_spec(m.shape[0], rows, tm, d))
    out_shape, out_specs = [], []
    if out_x:
        out_shape.append(jax.ShapeDtypeStruct((rows, d), F32))
        out_specs.append(row_spec)
    out_shape.append(jax.ShapeDtypeStruct((rows, d), h_dtype))
    out_specs.append(row_spec)
    grid_spec = pltpu.PrefetchScalarGridSpec(
        num_scalar_prefetch=2, grid=(rows // tm,), in_specs=specs, out_specs=out_specs,
        scratch_shapes=[pltpu.VMEM((2, tm * SUBLANES, LANES), F32), pltpu.VMEM((2, tm * SUBLANES, LANES), F32),
                        pltpu.SemaphoreType.DMA((2,))])
    outs = pl.pallas_call(
        functools.partial(_combine_kernel, has_mod=mod is not None, out_x=out_x), grid_spec=grid_spec,
        out_shape=out_shape, compiler_params=_cparams("arbitrary"), name="moe_combine",
    )(slots[0], slots[1], *args)
    return outs if out_x else outs[0]


def _softmax_av(groups):
    maxes = []
    for scores, _, sink in groups:
        m = sink
        for s in scores:
            m = jnp.maximum(m, jnp.broadcast_to(jnp.max(s, axis=-1, keepdims=True), sink.shape))
        maxes.append(m)
    probs = [[jnp.exp(s - jnp.concatenate([m] * (s.shape[1] // LANES), axis=1)).astype(BF16) for s in scores]
             for (scores, _, _), m in zip(groups, maxes)]
    outs = []
    for (_, values, sink), m, ps in zip(groups, maxes, probs):
        hd = values[0].shape[1]
        tot = None
        for p, v in zip(ps, values):
            n = v.shape[0]
            v_ext = jnp.concatenate([v, jnp.zeros((n, LANES - hd), BF16), jnp.ones((n, LANES), BF16)], axis=1)
            pv = _dot(p, v_ext)
            tot = pv if tot is None else tot + pv
        outs.append((tot, sink, m, hd))
    return [tot[:, :hd] / (tot[:, LANES:] + jnp.exp(sink - m))[:, :hd] for tot, sink, m, hd in outs]


def _sink_column(sink_ref, kv, rows):
    return jnp.concatenate([jnp.full((rows, LANES), sink_ref[kv * ATT_GROUP + g], F32) for g in range(ATT_GROUP)],
                           axis=0)


def _attn_ctx_kernel(sink_ref, qkv_ref, o_ref):
    t = qkv_ref.shape[0]
    qw = ATT_HEADS * HEAD_DIM
    kw = ATT_KV * HEAD_DIM
    groups = []
    for kv in range(ATT_KV):
        q = jnp.concatenate(
            [qkv_ref[:, (kv * ATT_GROUP + g) * HEAD_DIM:(kv * ATT_GROUP + g + 1) * HEAD_DIM] for g in range(ATT_GROUP)],
            axis=0).astype(BF16)
        k = qkv_ref[:, qw + kv * HEAD_DIM:qw + (kv + 1) * HEAD_DIM].astype(BF16)
        v = qkv_ref[:, qw + kw + kv * HEAD_DIM:qw + kw + (kv + 1) * HEAD_DIM].astype(BF16)
        groups.append(([_dot_nt(q, k) * HEAD_DIM ** -0.5], [v], _sink_column(sink_ref, kv, t)))
    heads_out = [o[g * t:(g + 1) * t] for o in _softmax_av(groups) for g in range(ATT_GROUP)]
    o_ref[...] = jnp.concatenate(heads_out, axis=1).astype(o_ref.dtype)


def _attn_ctx(qkv, sink, n_seq, seq_len):
    rows, cols = qkv.shape
    return pl.pallas_call(
        _attn_ctx_kernel, grid=(n_seq,),
        in_specs=[pl.BlockSpec(memory_space=pltpu.SMEM), pl.BlockSpec((seq_len, cols), lambda b: (b, 0))],
        out_specs=pl.BlockSpec((seq_len, ATT_HEADS * HEAD_DIM), lambda b: (b, 0)),
        out_shape=jax.ShapeDtypeStruct((rows, ATT_HEADS * HEAD_DIM), BF16),
        compiler_params=_cparams("parallel"), name="attn_context",
    )(sink, qkv)


def _rope_block(x, cos, sin_signed):
    lane = lax.broadcasted_iota(jnp.int32, x.shape, 1)
    nf = HEAD_DIM // 4
    partner = jnp.where((lane % (2 * nf)) < nf, pltpu.roll(x, LANES - nf, axis=1), pltpu.roll(x, nf, axis=1))
    return x * cos + partner * sin_signed


def _attn_lat_kernel(sink_ref, qkv_ref, ck_ref, cv_ref, cos_ref, sin_ref, o_ref, k_scr):
    i = pl.program_id(1)
    t = qkv_ref.shape[0]
    qw = ATT_HEADS * HEAD_DIM
    kw = ATT_KV * HEAD_DIM
    span = Q_BLOCK + 2 * WINDOW

    @pl.when(i == 0)
    def _():
        for c in range(kw // LANES):
            blk = qkv_ref[:, qw + c * LANES:qw + (c + 1) * LANES]
            k_scr[:, c * LANES:(c + 1) * LANES] = _rope_block(blk, cos_ref[...], sin_ref[...]).astype(BF16)

    r0 = pl.multiple_of(i * Q_BLOCK, Q_BLOCK)
    ws = pl.multiple_of(jnp.clip(r0 - WINDOW, 0, t - span), Q_BLOCK)
    cos_q = cos_ref[pl.ds(r0, Q_BLOCK), :]
    sin_q = sin_ref[pl.ds(r0, Q_BLOCK), :]
    qpos = r0 + lax.broadcasted_iota(jnp.int32, (Q_BLOCK, span), 0)
    kpos = ws + lax.broadcasted_iota(jnp.int32, (Q_BLOCK, span), 1)
    band = jnp.abs(qpos - kpos) <= WINDOW
    band = jnp.concatenate([band] * ATT_GROUP, axis=0)
    roped = [_rope_block(qkv_ref[pl.ds(r0, Q_BLOCK), c * LANES:(c + 1) * LANES], cos_q, sin_q)
             for c in range(qw // LANES)]
    groups = []
    for kv in range(ATT_KV):
        heads = []
        for g in range(ATT_GROUP):
            c, half = divmod((kv * ATT_GROUP + g) * HEAD_DIM, LANES)
            heads.append(roped[c][:, half:half + HEAD_DIM])
        q = jnp.concatenate(heads, axis=0).astype(BF16)
        ck = ck_ref[0, :, kv * HEAD_DIM:(kv + 1) * HEAD_DIM].astype(BF16)
        cv = cv_ref[0, :, kv * HEAD_DIM:(kv + 1) * HEAD_DIM].astype(BF16)
        kwin = k_scr[pl.ds(ws, span), kv * HEAD_DIM:(kv + 1) * HEAD_DIM]
        vwin = qkv_ref[pl.ds(ws, span), qw + kw + kv * HEAD_DIM:qw + kw + (kv + 1) * HEAD_DIM].astype(BF16)
        s_ctx = _dot_nt(q, ck) * HEAD_DIM ** -0.5
        s_win = jnp.where(band, _dot_nt(q, kwin) * HEAD_DIM ** -0.5, -jnp.inf)
        groups.append(([s_ctx, s_win], [cv, vwin], _sink_column(sink_ref, kv, Q_BLOCK)))
    heads_out = [o[g * Q_BLOCK:(g + 1) * Q_BLOCK] for o in _softmax_av(groups) for g in range(ATT_GROUP)]
    o_ref[...] = jnp.concatenate(heads_out, axis=1).astype(o_ref.dtype)


def _rope_tables(seq_len):
    pos = jnp.arange(seq_len, dtype=jnp.int32)
    row = (pos // GRID_W).astype(F32)
    col = (pos % GRID_W).astype(F32)
    nf = HEAD_DIM // 4
    inv = ROPE_BASE ** (-jnp.arange(nf, dtype=F32) / nf)
    ang_r = row[:, None] * inv[None, :]
    ang_c = col[:, None] * inv[None, :]
    cos_h = jnp.concatenate([jnp.cos(ang_r), jnp.cos(ang_r), jnp.cos(ang_c), jnp.cos(ang_c)], axis=1)
    sin_h = jnp.concatenate([-jnp.sin(ang_r), jnp.sin(ang_r), -jnp.sin(ang_c), jnp.sin(ang_c)], axis=1)
    reps = LANES // HEAD_DIM
    return jnp.tile(cos_h, (1, reps)), jnp.tile(sin_h, (1, reps))


def _attn_lat(qkv, cache_k, cache_v, sink, n_seq, seq_len):
    rows, cols = qkv.shape
    past = cache_k.shape[1]
    kw = ATT_KV * HEAD_DIM
    cos, sin = _rope_tables(seq_len)
    return pl.pallas_call(
        _attn_lat_kernel, grid=(n_seq, seq_len // Q_BLOCK),
        in_specs=[pl.BlockSpec(memory_space=pltpu.SMEM),
                  pl.BlockSpec((seq_len, cols), lambda b, i: (b, 0)),
                  pl.BlockSpec((1, past, kw), lambda b, i: (b, 0, 0)),
                  pl.BlockSpec((1, past, kw), lambda b, i: (b, 0, 0)),
                  pl.BlockSpec((seq_len, LANES), lambda b, i: (0, 0)),
                  pl.BlockSpec((seq_len, LANES), lambda b, i: (0, 0))],
        out_specs=pl.BlockSpec((Q_BLOCK, ATT_HEADS * HEAD_DIM), lambda b, i: (b * (seq_len // Q_BLOCK) + i, 0)),
        out_shape=jax.ShapeDtypeStruct((rows, ATT_HEADS * HEAD_DIM), BF16),
        scratch_shapes=[pltpu.VMEM((seq_len, kw), BF16)],
        compiler_params=_cparams("parallel", "arbitrary"), name="attn_latent",
    )(sink, qkv, cache_k.reshape(n_seq, past, kw), cache_v.reshape(n_seq, past, kw), cos, sin)


def _split_bf16(w):
    hi = w.astype(BF16)
    return hi, (w - hi.astype(F32)).astype(BF16)


def _select_lanes(x, sel):
    x1 = x.astype(BF16)
    rest = x - x1.astype(F32)
    x2 = rest.astype(BF16)
    x3 = (rest - x2.astype(F32)).astype(BF16)
    return _dot(x1, sel) + _dot(x2, sel) + _dot(x3, sel)


def _ml_gates_kernel(h_ref, w_ref, wt_ref, b_ref, bt_ref, g_ref, gt_ref):
    h = h_ref[...]
    h_hi = h.astype(BF16)
    h_lo = (h - h_hi.astype(F32)).astype(BF16)
    ng = gt_ref.shape[0]
    by_hi = _dot(h_hi, w_ref[...])
    g_ref[...] = by_hi[:, :LANES] + by_hi[:, LANES:] + _dot(h_lo, w_ref[:, :LANES]) + b_ref[...]
    by_hi_t = _dot_nt(wt_ref[...], h_hi)
    gt_ref[...] = by_hi_t[:ng] + by_hi_t[ng:] + _dot_nt(wt_ref[:ng, :], h_lo) + bt_ref[...]


def _ml_gates(h, w_gates, b_gates, *, tm=512):
    rows, d = h.shape
    ng = w_gates.shape[1]
    w_hi, w_lo = _split_bf16(jnp.pad(w_gates, ((0, 0), (0, LANES - ng))))
    wt_hi, wt_lo = _split_bf16(w_gates.T)
    b_pad = jnp.pad(b_gates, (0, LANES - ng)).reshape(1, LANES)
    return pl.pallas_call(
        _ml_gates_kernel, grid=(rows // tm,),
        in_specs=[pl.BlockSpec((tm, d), lambda i: (i, 0)), pl.BlockSpec((d, 2 * LANES), lambda i: (0, 0)),
                  pl.BlockSpec((2 * ng, d), lambda i: (0, 0)), pl.BlockSpec((1, LANES), lambda i: (0, 0)),
                  pl.BlockSpec((ng, 1), lambda i: (0, 0))],
        out_specs=[pl.BlockSpec((tm, LANES), lambda i: (i, 0)), pl.BlockSpec((ng, tm), lambda i: (0, i))],
        out_shape=[jax.ShapeDtypeStruct((rows, LANES), F32), jax.ShapeDtypeStruct((ng, rows), F32)],
        compiler_params=_cparams("parallel"), name="mlstm_gates",
    )(h, jnp.concatenate([w_hi, w_lo], axis=1), jnp.concatenate([wt_hi, wt_lo], axis=0), b_pad,
      b_gates.reshape(ng, 1))


def _ml_qk_kernel(h_ref, w_ref, cw_ref, o_ref, *, seq_len, k_scale):
    j = pl.program_id(1)
    x = _dot(h_ref[...].astype(BF16), w_ref[...])
    t = x.shape[0]
    pos = lax.broadcasted_iota(jnp.int32, x.shape, 0) % seq_len
    prev = jnp.where(pos == 0, 0.0, pltpu.roll(x, 1, axis=0))
    nxt = jnp.where(pos == seq_len - 1, 0.0, pltpu.roll(x, t - 1, axis=0))
    y = prev * cw_ref[0:1, :] + x * cw_ref[1:2, :] + nxt * cw_ref[2:3, :]
    scale = jnp.where(j >= pl.num_programs(1) // 2, k_scale, 1.0).astype(F32)
    o_ref[...] = (_silu(y) * scale).astype(o_ref.dtype)


def _ml_qk(h, w_qk, conv_w, seq_len, *, tm=1024, tn=1024):
    rows, d = h.shape
    width = w_qk.shape[1]
    return pl.pallas_call(
        functools.partial(_ml_qk_kernel, seq_len=seq_len, k_scale=ML_DK ** -0.5), grid=(rows // tm, width // tn),
        in_specs=[pl.BlockSpec((tm, d), lambda i, j: (i, 0)), pl.BlockSpec((d, tn), lambda i, j: (0, j)),
                  pl.BlockSpec((3, tn), lambda i, j: (0, j))],
        out_specs=pl.BlockSpec((tm, tn), lambda i, j: (i, j)),
        out_shape=jax.ShapeDtypeStruct((rows, width), BF16),
        compiler_params=_cparams("parallel", "parallel"), name="mlstm_qk",
    )(h, w_qk, conv_w)


def _ml_scan_kernel(*refs, zero_init):
    refs = list(refs)
    dirs = [tuple(refs[0:5]), tuple(refs[5:10])]
    sel_ref = refs[10]
    refs = refs[11:]
    if not zero_init:
        c0_ref, n0_ref, m0_ref = refs[:3]
        refs = refs[3:]
    hf_ref, hb_ref, c_ref, n_ref, m_ref = refs
    h_out = (hf_ref, hb_ref)
    c = pl.program_id(1)
    last = pl.num_programs(1) - 1

    @pl.when(c == 0)
    def _():
        if zero_init:
            c_ref[...] = jnp.zeros_like(c_ref)
            n_ref[...] = jnp.zeros_like(n_ref)
            m_ref[...] = jnp.zeros_like(m_ref)
        else:
            c_ref[...] = c0_ref[...]
            n_ref[...] = n0_ref[...]
            m_ref[...] = m0_ref[...]

    length = hf_ref.shape[0]
    ti = lax.broadcasted_iota(jnp.int32, (length, length), 0)
    si = lax.broadcasted_iota(jnp.int32, (length, length), 1)
    causal = [ti >= si, ti <= si]
    edge = [length - 1, 0]
    b_row, i_rep, b_rep = [], [], []
    for d in range(2):
        g_ref, gt_ref = dirs[d][3], dirs[d][4]
        tri = jnp.where(causal[d], 1.0, 0.0).astype(F32)
        g_col = g_ref[...]
        f_row = _log_sigmoid(gt_ref[...])
        b_col = jnp.dot(tri, _log_sigmoid(g_col), precision=HI, preferred_element_type=F32)
        b_row.append(_dot_nt(f_row, tri, precision=HI))
        i_rep.append(_select_lanes(g_col, sel_ref[d, 0]))
        b_rep.append(_select_lanes(b_col, sel_ref[d, 1]))
    chains = [(d, h) for d in range(2) for h in range(ML_HEADS)]
    q = {(d, h): dirs[d][0][:, h * ML_DK:(h + 1) * ML_DK] for d, h in chains}
    k = {(d, h): dirs[d][1][:, h * ML_DK:(h + 1) * ML_DK] for d, h in chains}
    v = {(d, h): dirs[d][2][:, h * ML_DV:(h + 1) * ML_DV].astype(BF16) for d, h in chains}
    cst = {c: c_ref[0, c[0], c[1]] for c in chains}
    nst = {c: n_ref[0, c[0], c[1]] for c in chains}
    m_rep = {c: m_ref[0, c[0], c[1]] for c in chains}
    bc = {(d, h): b_rep[d][:, h * LANES:(h + 1) * LANES] for d, h in chains}
    i_col = {(d, h): i_rep[d][:, h * LANES:(h + 1) * LANES] for d, h in chains}
    qk_raw = {c: _dot_nt(q[c], k[c]) for c in chains}
    inter = {c: _dot_nt(q[c], jnp.concatenate([cst[c], jnp.broadcast_to(nst[c], (LANES, ML_DK))],
                                              axis=0).astype(BF16)) for c in chains}
    a_row = {}
    for d, h in chains:
        ji = d * 2 * ML_HEADS + h
        a_row[d, h] = dirs[d][4][ji:ji + 1, :] - b_row[d][ji + ML_HEADS:ji + ML_HEADS + 1, :]
    amat = {c: jnp.where(causal[c[0]], a_row[c], -jnp.inf) for c in chains}
    u = {c: jnp.maximum(m_rep[c], jnp.broadcast_to(jnp.max(amat[c], axis=1, keepdims=True), (length, LANES)))
         for c in chains}
    qk = {c: (qk_raw[c] * jnp.exp(amat[c] - u[c])).astype(BF16) for c in chains}
    intra = {c: _dot(qk[c], jnp.concatenate([v[c], jnp.ones((length, LANES), BF16)], axis=1)) for c in chains}
    b_last = {c: bc[c][edge[c[0]]:edge[c[0]] + 1, :] for c in chains}
    wlog_row = {c: b_last[c] + a_row[c] for c in chains}
    m_new = {c: jnp.maximum(b_last[c] + m_rep[c], jnp.max(wlog_row[c], axis=1, keepdims=True)) for c in chains}
    ws_row = {c: jnp.exp(wlog_row[c] - m_new[c]) for c in chains}
    kw = {c: (jnp.exp(b_last[c] - bc[c] + i_col[c] - m_new[c]) * k[c].astype(F32)).astype(BF16) for c in chains}
    c_upd = {c: _dot_tn(v[c], kw[c]) for c in chains}
    n_upd = {c: _dot(jnp.broadcast_to(ws_row[c], (8, length)).astype(BF16), k[c])[0:1] for c in chains}
    for c in chains:
        d, h = c
        sc = jnp.exp(m_rep[c] - u[c])
        tot = jnp.concatenate([sc] * (ML_DV // LANES + 1), axis=1) * inter[c] + intra[c]
        inv = 1.0 / jnp.maximum(jnp.abs(tot[:, ML_DV:]), jnp.exp(-(bc[c] + u[c])))
        h_out[d][:, h * ML_DV:(h + 1) * ML_DV] = tot[:, :ML_DV] * jnp.concatenate([inv] * (ML_DV // LANES), axis=1)
        decay = jnp.exp(b_last[c] + m_rep[c] - m_new[c])
        c_ref[0, d, h] = decay * cst[c] + c_upd[c]
        n_ref[0, d, h] = decay * nst[c] + n_upd[c]
        m_ref[0, d, h] = m_new[c]


def _ml_scan(qk, p, g, gt, state, n_seq, seq_len):
    rows = qk.shape[0]
    length = min(ML_CHUNK, seq_len)
    nc = seq_len // length
    qw = ML_HEADS * ML_DK
    vw = ML_HEADS * ML_DV
    ng = gt.shape[0]

    def fwd(b, c):
        return b * nc + c

    def bwd(b, c):
        return b * nc + nc - 1 - c

    args, specs = [], []
    for pos in (fwd, bwd):
        args += [qk, qk, p, g, gt]
        specs += [pl.BlockSpec((length, qw), lambda b, c, pos=pos: (pos(b, c), 0)),
                  pl.BlockSpec((length, qw), lambda b, c, pos=pos: (pos(b, c), 1)),
                  pl.BlockSpec((length, vw), lambda b, c, pos=pos: (pos(b, c), 0)),
                  pl.BlockSpec((length, LANES), lambda b, c, pos=pos: (pos(b, c), 0)),
                  pl.BlockSpec((ng, length), lambda b, c, pos=pos: (0, pos(b, c)))]
    assert length == LANES and ML_DK == LANES
    gate_lane = jnp.arange(LANES)[:, None]
    head = (jnp.arange(ML_HEADS * LANES) // LANES)[None, :]
    sel = jnp.stack([jnp.stack([gate_lane == (2 * d + kind) * ML_HEADS + head for kind in range(2)])
                     for d in range(2)]).astype(BF16)
    args.append(sel)
    specs.append(pl.BlockSpec(sel.shape, lambda b, c: (0, 0, 0, 0)))
    c_spec = pl.BlockSpec((1, 2, ML_HEADS, ML_DV, ML_DK), lambda b, c: (b, 0, 0, 0, 0))
    n_spec = pl.BlockSpec((1, 2, ML_HEADS, 1, ML_DK), lambda b, c: (b, 0, 0, 0, 0))
    zero_init = state is None
    if not zero_init:
        c0, n0, m0 = state
        args += [c0, n0.reshape(n_seq, 2, ML_HEADS, 1, ML_DK),
                 jnp.broadcast_to(m0[..., None, None], (n_seq, 2, ML_HEADS, 1, ML_DK))]
        specs += [c_spec, n_spec, n_spec]
    hf, hb, c_fin, n_fin, m_fin = pl.pallas_call(
        functools.partial(_ml_scan_kernel, zero_init=zero_init), grid=(n_seq, nc), in_specs=specs,
        out_specs=[pl.BlockSpec((length, vw), lambda b, c: (fwd(b, c), 0)),
                   pl.BlockSpec((length, vw), lambda b, c: (bwd(b, c), 0)), c_spec, n_spec, n_spec],
        out_shape=[jax.ShapeDtypeStruct((rows, vw), F32), jax.ShapeDtypeStruct((rows, vw), F32),
                   jax.ShapeDtypeStruct((n_seq, 2, ML_HEADS, ML_DV, ML_DK), F32),
                   jax.ShapeDtypeStruct((n_seq, 2, ML_HEADS, 1, ML_DK), F32),
                   jax.ShapeDtypeStruct((n_seq, 2, ML_HEADS, 1, ML_DK), F32)],
        compiler_params=_cparams("parallel", "arbitrary"), name="mlstm_scan",
    )(*args)
    return hf, hb, (c_fin, n_fin[:, :, :, 0, :], m_fin[:, :, :, 0, 0])


def _gla_scan_kernel(*refs, zero_init):
    refs = list(refs)
    dirs = [tuple(refs[0:4]), tuple(refs[4:8])]
    w2_ref, ba_ref = refs[8:10]
    refs = refs[10:]
    if not zero_init:
        s0_ref = refs.pop(0)
    of_ref, ob_ref, s_ref, st_scr, la_scr = refs
    o_out = (of_ref, ob_ref)
    c = pl.program_id(1)
    last = pl.num_programs(1) - 1
    kw = GLA_HEADS * GLA_DK
    n_sub = of_ref.shape[0] // GLA_SUB

    @pl.when(c == 0)
    def _():
        for d in range(2):
            for h in range(GLA_HEADS):
                st_scr[d, h] = jnp.zeros((GLA_DV, GLA_DK), F32) if zero_init else s0_ref[0, d, h].T

    for d in range(2):
        u = dirs[d][3][...].astype(BF16)
        z = _dot(u, w2_ref[:, d * kw:(d + 1) * kw]) + ba_ref[:, d * kw:(d + 1) * kw]
        la = _log_sigmoid(z) / GLA_TAU
        la_scr[d] = la
        totals = jnp.sum(la.reshape(n_sub, GLA_SUB, kw), axis=1)
        worst = jnp.min(totals) if d == 0 else jnp.minimum(worst, jnp.min(totals))
    decay_bounded = worst * LOG2E > -GLA_MAX_EXP2

    ti = lax.broadcasted_iota(jnp.int32, (GLA_SUB, GLA_SUB), 0)
    si = lax.broadcasted_iota(jnp.int32, (GLA_SUB, GLA_SUB), 1)
    s_lane = lax.broadcasted_iota(jnp.int32, (GLA_SUB, GLA_SUB), 1)

    def sub_chunk(j, carry, bounded):
        steps = [(d, t) for d in range(2) for t in range(GLA_STEPS)]
        chains = [(d, h, t) for d, t in steps for h in range(GLA_HEADS)]
        rows = {(d, t): pl.ds(pl.multiple_of(
            ((j * GLA_STEPS + t) if d == 0 else n_sub - 1 - (j * GLA_STEPS + t)) * GLA_SUB, GLA_SUB), GLA_SUB)
            for d, t in steps}
        causal = [ti >= si, ti <= si]
        edge = [GLA_SUB - 1, 0]
        bc_all = {(d, t): jnp.dot(jnp.where(causal[d], 1.0, 0.0).astype(F32), la_scr[d, rows[d, t], :], precision=HI,
                                  preferred_element_type=F32) * LOG2E for d, t in steps}
        q, k, v, bc2, b_last, k_dec, q_dec, a = {}, {}, {}, {}, {}, {}, {}, {}
        for c in chains:
            d, h, t = c
            q_ref, k_ref, v_ref, _ = dirs[d]
            bc2[c] = bc_all[d, t][:, h * GLA_DK:(h + 1) * GLA_DK]
            q[c] = q_ref[rows[d, t], h * GLA_DK:(h + 1) * GLA_DK] * GLA_DK ** -0.5
            k[c] = k_ref[rows[d, t], h * GLA_DK:(h + 1) * GLA_DK]
            v[c] = v_ref[rows[d, t], h * GLA_DV:(h + 1) * GLA_DV].astype(BF16)
            b_last[c] = bc2[c][edge[d]:edge[d] + 1, :]
            k_dec[c] = (k[c] * jnp.exp2(b_last[c] - bc2[c])).astype(BF16)
            q_dec[c] = q[c] * jnp.exp2(bc2[c])
        for c in chains:
            if bounded:
                a[c] = _dot_nt((q_dec[c] * jnp.exp2(-b_last[c])).astype(BF16), k_dec[c])
            else:
                acc = jnp.zeros((GLA_SUB, GLA_SUB), F32)
                for s in range(GLA_SUB):
                    decay = jnp.exp2(bc2[c] - bc2[c][s:s + 1, :])
                    col = jnp.sum(q[c] * (k[c][s:s + 1, :] * decay), axis=1, keepdims=True)
                    acc = jnp.where(s_lane == s, col, acc)
                a[c] = acc
        kv = {c: _dot_tn(v[c], k_dec[c]) for c in chains}
        intra = {c: _dot(jnp.where(causal[c[0]], a[c], 0.0).astype(BF16), v[c]) for c in chains}
        state = {(d, h): st_scr[d, h] for d in range(2) for h in range(GLA_HEADS)}
        for t in range(GLA_STEPS):
            now = [c for c in chains if c[2] == t]
            inter = {c: _dot_nt(q_dec[c].astype(BF16), state[c[0], c[1]].astype(BF16)) for c in now}
            for c in now:
                d, h, _ = c
                o_out[d][rows[d, t], h * GLA_DV:(h + 1) * GLA_DV] = intra[c] + inter[c]
                state[d, h] = jnp.exp2(b_last[c]) * state[d, h] + kv[c]
        for (d, h), st in state.items():
            st_scr[d, h] = st
        return carry

    @pl.when(decay_bounded)
    def _():
        lax.fori_loop(0, n_sub // GLA_STEPS, functools.partial(sub_chunk, bounded=True), 0)

    @pl.when(jnp.logical_not(decay_bounded))
    def _():
        lax.fori_loop(0, n_sub // GLA_STEPS, functools.partial(sub_chunk, bounded=False), 0)

    @pl.when(c == last)
    def _():
        for d in range(2):
            for h in range(GLA_HEADS):
                s_ref[0, d, h] = st_scr[d, h].T


def _gla_scan(p, u, w2, b_a, state, n_seq, seq_len):
    rows = p.shape[0]
    length = min(GLA_BLOCK, seq_len)
    nc = seq_len // length
    kw = GLA_HEADS * GLA_DK
    vw = GLA_HEADS * GLA_DV

    def fwd(b, c):
        return b * nc + c

    def bwd(b, c):
        return b * nc + nc - 1 - c

    args, specs = [], []
    for pos in (fwd, bwd):
        args += [p, p, p, u]
        specs += [pl.BlockSpec((length, kw), lambda b, c, pos=pos: (pos(b, c), 0)),
                  pl.BlockSpec((length, kw), lambda b, c, pos=pos: (pos(b, c), 1)),
                  pl.BlockSpec((length, vw), lambda b, c, pos=pos: (pos(b, c), 2 * kw // vw)),
                  pl.BlockSpec((length, LANES), lambda b, c, pos=pos: (pos(b, c), 0))]
    args += [w2, b_a]
    specs += [pl.BlockSpec(w2.shape, lambda b, c: (0, 0)), pl.BlockSpec(b_a.shape, lambda b, c: (0, 0))]
    s_spec = pl.BlockSpec((1, 2, GLA_HEADS, GLA_DK, GLA_DV), lambda b, c: (b, 0, 0, 0, 0))
    zero_init = state is None
    if not zero_init:
        args.append(state)
        specs.append(s_spec)
    return pl.pallas_call(
        functools.partial(_gla_scan_kernel, zero_init=zero_init), grid=(n_seq, nc), in_specs=specs,
        out_specs=[pl.BlockSpec((length, vw), lambda b, c: (fwd(b, c), 0)),
                   pl.BlockSpec((length, vw), lambda b, c: (bwd(b, c), 0)), s_spec],
        out_shape=[jax.ShapeDtypeStruct((rows, vw), F32), jax.ShapeDtypeStruct((rows, vw), F32),
                   jax.ShapeDtypeStruct((n_seq, 2, GLA_HEADS, GLA_DK, GLA_DV), F32)],
        scratch_shapes=[pltpu.VMEM((2, GLA_HEADS, GLA_DV, GLA_DK), F32), pltpu.VMEM((2, length, kw), F32)],
        compiler_params=_cparams("parallel", "arbitrary"), name="gla_scan",
    )(*args)


def kernel(x_prompt, x_sample, cache_k_0, cache_v_0, state_mlstm_C_1, state_mlstm_n_1, state_mlstm_m_1, state_gla_S_2, cache_k_3, cache_v_3, c, c_ctx, w_mod, b_mod, norm1_g, norm2_g, final_g, router_w, router_b, moe_wg, moe_wu, moe_wd, attn0_w_qkv, attn0_sink, attn0_w_o, mlstm1_w_in, mlstm1_b_gates, mlstm1_conv, mlstm1_norm_g, mlstm1_w_out, gla2_w_in, gla2_w_a1, gla2_w_a2, gla2_b_a, gla2_norm_g, gla2_w_out, attn3_w_qkv, attn3_sink, attn3_w_o):
    n_ctx, ctx_len, d = x_prompt.shape
    n_lat, lat_len, _ = x_sample.shape
    depth = w_mod.shape[0]

    cvec = jnp.concatenate([c_ctx[None, :], c, jnp.zeros((8 - 1 - n_lat, d), F32)], axis=0)
    mod = _modulation(cvec, w_mod, b_mod).reshape(depth, 8, 6, 1, d)

    def mods(layer, kind, latent):
        return mod[layer, 1:1 + n_lat, kind] if latent else mod[layer, 0:1, kind]

    rw_hi = router_w.T.astype(BF16)
    rw_lo = (router_w.T - rw_hi.astype(F32)).astype(BF16)
    rwt = jnp.concatenate([rw_hi, rw_lo], axis=0)
    rb = router_b.reshape(-1, 1)
    attn_w = {0: (attn0_w_qkv.astype(BF16), attn0_sink, attn0_w_o.astype(BF16), cache_k_0, cache_v_0),
              3: (attn3_w_qkv.astype(BF16), attn3_sink, attn3_w_o.astype(BF16), cache_k_3, cache_v_3)}
    ml_qw = ML_HEADS * ML_DK
    ml_vw = ML_HEADS * ML_DV
    ml_main = 2 * ml_qw + 2 * ml_vw
    ml_w_qk = mlstm1_w_in[:, :2 * ml_qw].astype(BF16)
    ml_w_vo = mlstm1_w_in[:, 2 * ml_qw:ml_main].astype(BF16)
    ml_w_gates = mlstm1_w_in[:, ml_main:]
    ml_w_out = mlstm1_w_out.astype(BF16)
    gla_kw = GLA_HEADS * GLA_DK
    gla_w_in = gla2_w_in.astype(BF16)
    gla_w_a1 = jnp.pad(jnp.concatenate([gla2_w_a1[0], gla2_w_a1[1]], axis=1),
                       ((0, 0), (0, LANES - 2 * GLA_RANK))).astype(BF16)
    gla_w2 = jnp.zeros((LANES, 2 * gla_kw), F32)
    gla_w2 = gla_w2.at[:GLA_RANK, :gla_kw].set(gla2_w_a2[0]).at[GLA_RANK:2 * GLA_RANK, gla_kw:].set(gla2_w_a2[1])
    gla_w2 = gla_w2.astype(BF16)
    gla_ba = gla2_b_a.reshape(1, 2 * gla_kw)
    gla_w_out = gla2_w_out.astype(BF16)

    new_state = []

    def mixer(layer, s, count0):
        latent, n_seq, seq_len, x, h = s["latent"], s["n_seq"], s["seq_len"], s["x"], s["h"]
        tail = (mods(layer, 2, latent), norm2_g[layer], mods(layer, 4, latent), mods(layer, 3, latent), rwt, rb,
                count0)
        kind = layer % 3
        if kind == 0:
            w_qkv, sink, w_o, ck, cv = attn_w[layer]
            qkv = _matmul(h, w_qkv)
            if latent:
                att = _attn_lat(qkv, ck, cv, sink, n_seq, seq_len)
            else:
                att = _attn_ctx(qkv, sink, n_seq, seq_len)
                qw = ATT_HEADS * HEAD_DIM
                kw = ATT_KV * HEAD_DIM
                new_state.append(qkv[:, qw:qw + kw].reshape(n_seq, seq_len, ATT_KV, HEAD_DIM))
                new_state.append(qkv[:, qw + kw:].reshape(n_seq, seq_len, ATT_KV, HEAD_DIM))
            return _proj("plain", (att,), w_o, x, *tail)
        if kind == 1:
            p = _matmul(h, ml_w_vo)
            g, gt = _ml_gates(h, ml_w_gates, mlstm1_b_gates)
            qk = _ml_qk(h, ml_w_qk, mlstm1_conv, seq_len)
            st = (state_mlstm_C_1, state_mlstm_n_1, state_mlstm_m_1) if latent else None
            hf, hb, fin = _ml_scan(qk, p, g, gt, st, n_seq, seq_len)
            if not latent:
                new_state.extend(fin)
            return _proj("mlstm", (hf, hb, p, 1, mlstm1_norm_g), ml_w_out, x, *tail)
        p = _matmul(h, gla_w_in)
        u = _matmul(h, gla_w_a1)
        of, ob, s_fin = _gla_scan(p, u, gla_w2, gla_ba, state_gla_S_2 if latent else None, n_seq, seq_len)
        if not latent:
            new_state.append(s_fin)
        gla_vw = GLA_HEADS * GLA_DV
        return _proj("gla", (of, ob, p, (2 * gla_kw + gla_vw) // gla_vw, gla2_norm_g), gla_w_out, x, *tail)

    streams = [dict(latent=False, n_seq=n_ctx, seq_len=ctx_len, x=x_prompt.reshape(n_ctx * ctx_len, d)),
               dict(latent=True, n_seq=n_lat, seq_len=lat_len, x=x_sample.reshape(n_lat * lat_len, d))]
    for s in streams:
        s["h"] = _rownorm(s["x"], norm1_g[0], mods(0, 1, s["latent"]), mods(0, 0, s["latent"]))
    for layer in range(depth):
        counts = jnp.zeros((N_EXPERTS, LANES), jnp.int32)
        for s in streams:
            s["x"], s["h2"], s["meta"], s["wcol"], counts = mixer(layer, s, counts)
        cnt = counts[:, 0]
        for s in streams:
            s["slots"] = _slots(s["meta"], cnt)
        xs, info = _dispatch([s["h2"] for s in streams], jnp.concatenate([s["slots"] for s in streams], axis=1), cnt)
        ys = _ffn(xs, info, moe_wg, moe_wu, moe_wd, layer)
        for s in streams:
            latent = s["latent"]
            gate2 = mods(layer, 5, latent)
            if layer + 1 < depth:
                s["x"], s["h"] = _combine(s["x"], ys, s["slots"], s["wcol"], gate2, norm1_g[layer + 1],
                                          mod=(mods(layer + 1, 1, latent), mods(layer + 1, 0, latent)), out_x=True,
                                          h_dtype=F32 if (layer + 1) % 3 == 1 else BF16)
            else:
                s["out"] = _combine(s["x"], ys, s["slots"], s["wcol"], gate2, final_g, h_dtype=F32)
    y_prompt = streams[0]["out"].reshape(n_ctx, ctx_len, d)
    y_sample = streams[1]["out"].reshape(n_lat, lat_len, d)
    return (y_prompt, y_sample, *new_state)
```

```python
import functools

import jax
import jax.numpy as jnp
from jax import lax
from jax.experimental import pallas as pl
from jax.experimental.pallas import tpu as pltpu

F32 = jnp.float32
BF16 = jnp.bfloat16
HI = lax.Precision.HIGHEST

EPS = 1e-6
LOG2E = 1.4426950408889634
GRID_W = 64
ATT_HEADS = 16
ATT_KV = 4
ATT_GROUP = ATT_HEADS // ATT_KV
HEAD_DIM = 64
WINDOW = 128
Q_BLOCK = 128
ROPE_BASE = 10000.0
ML_HEADS = 8
ML_DK = 128
ML_DV = 256
ML_CHUNK = 128
GLA_HEADS = 4
GLA_DK = 128
GLA_DV = 256
GLA_RANK = 16
GLA_TAU = 16.0
GLA_SUB = 16
GLA_BLOCK = 256
GLA_STEPS = 16
GLA_MAX_EXP2 = 80.0
N_EXPERTS = 16
N_GROUPS = 4
GROUP_SIZE = N_EXPERTS // N_GROUPS
LANES = 128
VMEM_LIMIT = 56 * 1024 * 1024


def _cparams(*sem):
    return pltpu.CompilerParams(dimension_semantics=sem, vmem_limit_bytes=VMEM_LIMIT)


def _dot(a, b):
    return jnp.dot(a, b, preferred_element_type=F32)


def _dot_nt(a, b, precision=None):
    return lax.dot_general(a, b, (((1,), (1,)), ((), ())), precision=precision, preferred_element_type=F32)


def _dot_tn(a, b):
    return lax.dot_general(a, b, (((0,), (0,)), ((), ())), preferred_element_type=F32)


def _sigmoid(x):
    return 1.0 / (1.0 + jnp.exp(-x))


def _silu(x):
    return x * _sigmoid(x)


def _log_sigmoid(x):
    return jnp.minimum(x, 0.0) - jnp.log(1.0 + jnp.exp(-jnp.abs(x)))


def _rms_rows(x, g):
    ms = jnp.mean(x * x, axis=-1, keepdims=True)
    return x * lax.rsqrt(ms + EPS) * g


def _mod_kernel(c_ref, w_ref, b_ref, o_ref):
    s = _silu(c_ref[...])
    o_ref[0] = _dot(s.astype(BF16), w_ref[0].astype(BF16)) + b_ref[0]


def _modulation(cvec, w_mod, b_mod):
    depth, d, n6 = w_mod.shape
    tn = 1536
    return pl.pallas_call(
        _mod_kernel,
        grid=(depth, n6 // tn),
        in_specs=[pl.BlockSpec((8, d), lambda l, j: (0, 0)),
                  pl.BlockSpec((1, d, tn), lambda l, j: (l, 0, j)),
                  pl.BlockSpec((1, 1, tn), lambda l, j: (l, 0, j))],
        out_specs=pl.BlockSpec((1, 8, tn), lambda l, j: (l, 0, j)),
        out_shape=jax.ShapeDtypeStruct((depth, 8, n6), F32),
        compiler_params=_cparams("parallel", "parallel"),
        name="adaln_modulation",
    )(cvec, w_mod, b_mod.reshape(depth, 1, n6))


def _route(h, rwt, rb, carry):
    tm = h.shape[0]
    h_hi = h.astype(BF16)
    h_lo = (h - h_hi.astype(F32)).astype(BF16)
    by_hi = _dot_nt(rwt, h_hi)
    logits = by_hi[:N_EXPERTS] + by_hi[N_EXPERTS:] + _dot_nt(rwt[:N_EXPERTS], h_lo)
    scores = _sigmoid(logits)
    sel = scores + rb
    expert = lax.broadcasted_iota(jnp.int32, sel.shape, 0)
    pos = expert % GROUP_SIZE
    grp = expert // GROUP_SIZE

    def mate(x, k):
        ahead = pltpu.roll(x, N_EXPERTS - k, axis=0)
        behind = pltpu.roll(x, GROUP_SIZE - k, axis=0)
        return jnp.where(pos + k < GROUP_SIZE, ahead, behind)

    beaten = jnp.zeros_like(sel)
    for k in range(1, GROUP_SIZE):
        other = mate(sel, k)
        other_first = (pos + k) % GROUP_SIZE < pos
        beaten = beaten + jnp.where(other_first, jnp.where(other >= sel, 1.0, 0.0), jnp.where(other > sel, 1.0, 0.0))
    top2 = jnp.where(beaten < 2.0, sel, 0.0)
    gscore = top2
    for k in range(1, GROUP_SIZE):
        gscore = gscore + mate(top2, k)
    lost = jnp.zeros_like(sel)
    for k in range(1, N_GROUPS):
        other = pltpu.roll(gscore, N_EXPERTS - GROUP_SIZE * k, axis=0)
        other_first = (grp + k) % N_GROUPS < grp
        lost = lost + jnp.where(other_first, jnp.where(other >= gscore, 1.0, 0.0),
                                jnp.where(other > gscore, 1.0, 0.0))
    picked = jnp.where(lost < 0.5, jnp.where(beaten < 2.0, 1.0, 0.0), 0.0)
    chosen = picked > 0.5
    weight = jnp.where(chosen, scores, 0.0)
    wsum = jnp.sum(weight, axis=0, keepdims=True)
    e_f = expert.astype(F32)
    e_a = jnp.min(jnp.where(chosen, e_f, float(N_EXPERTS)), axis=0, keepdims=True)
    e_b = jnp.max(jnp.where(chosen, e_f, -1.0), axis=0, keepdims=True)
    before = (lax.broadcasted_iota(jnp.int32, (tm, tm), 0) < lax.broadcasted_iota(jnp.int32, (tm, tm), 1))
    rank = _dot(picked.astype(BF16), jnp.where(before, 1.0, 0.0).astype(BF16)) + carry
    is_a = e_f == e_a
    is_b = e_f == e_b
    r_a = jnp.sum(jnp.where(is_a, rank, 0.0), axis=0, keepdims=True)
    r_b = jnp.sum(jnp.where(is_b, rank, 0.0), axis=0, keepdims=True)
    w_a = jnp.sum(jnp.where(is_a, weight, 0.0), axis=0, keepdims=True)
    w_b = jnp.sum(jnp.where(is_b, weight, 0.0), axis=0, keepdims=True)
    meta = jnp.concatenate([e_a, e_b, r_a, r_b, jnp.zeros((4, tm), F32)], axis=0).astype(jnp.int32)
    wcol = jnp.concatenate([w_a / wsum, w_b / wsum, jnp.zeros((LANES - 2, tm), F32)], axis=0).T
    return meta, wcol, carry + jnp.sum(picked, axis=1, keepdims=True)


def _norm_mod(x, g_ref, mod_refs):
    h = _rms_rows(x, g_ref[...])
    if mod_refs is not None:
        a_ref, s_ref = mod_refs
        h = h * (1.0 + a_ref[0]) + s_ref[0]
    return h


def _rownorm_kernel(x_ref, g_ref, a_ref, s_ref, h_ref):
    h_ref[...] = _norm_mod(x_ref[...], g_ref, (a_ref, s_ref)).astype(h_ref.dtype)


def _mod_spec(n_mod, rows, tm, d):
    per = (rows // n_mod) // tm
    return pl.BlockSpec((1, 1, d), lambda i, *_: (i // per, 0, 0))


def _rownorm(x, g, scale, shift, *, tm=512):
    rows, d = x.shape
    row_spec = pl.BlockSpec((tm, d), lambda i: (i, 0))
    return pl.pallas_call(
        _rownorm_kernel, grid=(rows // tm,),
        in_specs=[row_spec, pl.BlockSpec((1, d), lambda i: (0, 0)), _mod_spec(scale.shape[0], rows, tm, d),
                  _mod_spec(shift.shape[0], rows, tm, d)],
        out_specs=row_spec, out_shape=jax.ShapeDtypeStruct((rows, d), BF16),
        compiler_params=_cparams("parallel"), name="rownorm",
    )(x, g.reshape(1, d), scale, shift)


def _mm_kernel(a_ref, w_ref, o_ref):
    o_ref[...] = _dot(a_ref[...].astype(BF16), w_ref[...]).astype(o_ref.dtype)


def _matmul(a, w, *, out_dtype=F32, tm=1024):
    m, k = a.shape
    n = w.shape[1]
    tn = next(t for t in (1024, 768, 512, LANES) if n % t == 0)
    return pl.pallas_call(
        _mm_kernel, grid=(m // tm, n // tn),
        in_specs=[pl.BlockSpec((tm, k), lambda i, j: (i, 0)), pl.BlockSpec((k, tn), lambda i, j: (0, j))],
        out_specs=pl.BlockSpec((tm, tn), lambda i, j: (i, j)),
        out_shape=jax.ShapeDtypeStruct((m, n), out_dtype),
        compiler_params=_cparams("parallel", "parallel"), name="matmul",
    )(a, w)


def _head_norm(x, g, n_heads, dv):
    outs = []
    for h in range(n_heads):
        xs = x[:, h * dv:(h + 1) * dv]
        ms = jnp.mean(xs * xs, axis=-1, keepdims=True)
        outs.append(xs * lax.rsqrt(ms + EPS) * g[:, h * dv:(h + 1) * dv])
    return jnp.concatenate(outs, axis=1)


def _proj_kernel(*refs, pre):
    refs = list(refs)
    if pre == "plain":
        a = refs.pop(0)[...]
    else:
        f_ref, b_ref, p_ref, hg_ref = refs.pop(0), refs.pop(0), refs.pop(0), refs.pop(0)
        hsum = f_ref[...] + b_ref[...]
        if pre == "mlstm":
            a = _sigmoid(p_ref[...]) * _head_norm(hsum, hg_ref[...], ML_HEADS, ML_DV)
        else:
            a = _head_norm(hsum, hg_ref[...], GLA_HEADS, GLA_DV) * _silu(p_ref[...])
        a = a.astype(BF16)
    w_ref, x_ref, gate_ref, g_ref, a_ref, s_ref, rwt_ref, rb_ref, count0_ref = refs[:9]
    xo_ref, h_ref, meta_ref, wcol_ref, count_ref, carry_ref = refs[9:]

    @pl.when(pl.program_id(0) == 0)
    def _():
        carry_ref[...] = count0_ref[...].astype(F32)

    x = x_ref[...] + gate_ref[0] * _dot(a, w_ref[...])
    xo_ref[...] = x
    h = _norm_mod(x, g_ref, (a_ref, s_ref))
    _rows_to_tiles(h_ref, h)
    meta, wcol, carry = _route(h, rwt_ref[...], rb_ref[...], carry_ref[:, 0:1])
    meta_ref[...] = meta
    wcol_ref[...] = wcol
    carry_ref[...] = jnp.broadcast_to(carry, carry_ref.shape)
    count_ref[...] = jnp.broadcast_to(carry, count_ref.shape).astype(jnp.int32)


def _proj(pre, pre_args, w_out, x, gate, g, scale, shift, rwt, rb, count0, *, tm=512):
    rows, d = x.shape
    k = w_out.shape[0]
    row_spec = pl.BlockSpec((tm, d), lambda i: (i, 0))
    if pre == "plain":
        args, specs = [pre_args[0]], [pl.BlockSpec((tm, k), lambda i: (i, 0))]
    else:
        hf, hb, p, col_block, hg = pre_args
        wide = pl.BlockSpec((tm, k), lambda i: (i, 0))
        args = [hf, hb, p, hg.reshape(1, k)]
        specs = [wide, wide, pl.BlockSpec((tm, k), lambda i: (i, col_block)), pl.BlockSpec((1, k), lambda i: (0, 0))]
    args += [w_out, x, gate, g.reshape(1, d), scale, shift, rwt, rb, count0]
    specs += [pl.BlockSpec((k, d), lambda i: (0, 0)), row_spec, _mod_spec(gate.shape[0], rows, tm, d),
              pl.BlockSpec((1, d), lambda i: (0, 0)), _mod_spec(scale.shape[0], rows, tm, d),
              _mod_spec(shift.shape[0], rows, tm, d), pl.BlockSpec(rwt.shape, lambda i: (0, 0)),
              pl.BlockSpec(rb.shape, lambda i: (0, 0)), pl.BlockSpec(count0.shape, lambda i: (0, 0))]
    return pl.pallas_call(
        functools.partial(_proj_kernel, pre=pre), grid=(rows // tm,), in_specs=specs,
        out_specs=[row_spec, pl.BlockSpec((tm * SUBLANES, LANES), lambda i: (i, 0)),
                   pl.BlockSpec((8, tm), lambda i: (0, i)),
                   pl.BlockSpec((tm, LANES), lambda i: (i, 0)), pl.BlockSpec((N_EXPERTS, LANES), lambda i: (0, 0))],
        out_shape=[jax.ShapeDtypeStruct((rows, d), F32), jax.ShapeDtypeStruct((rows * SUBLANES, LANES), F32),
                   jax.ShapeDtypeStruct((8, rows), jnp.int32), jax.ShapeDtypeStruct((rows, LANES), F32),
                   jax.ShapeDtypeStruct((N_EXPERTS, LANES), jnp.int32)],
        scratch_shapes=[pltpu.VMEM((N_EXPERTS, LANES), F32)],
        compiler_params=_cparams("arbitrary"), name="proj_" + pre,
    )(*args)


MOE_TILE = 512
MOE_TILE_SHIFT = 9
MOE_TOKENS = 256
FFN_PARTS = 2


SUBLANES = 8


def _rows_to_tiles(ref, x, lead=(), first=0):
    rows = x.shape[0]
    for c in range(SUBLANES):
        ref[(*lead, pl.ds(first * SUBLANES + c, rows, stride=SUBLANES), slice(None))] = x[:, c * LANES:(c + 1) * LANES]


def _tiles_to_rows(ref, rows, lead=(), first=0):
    return jnp.concatenate([ref[(*lead, pl.ds(first * SUBLANES + c, rows, stride=SUBLANES), slice(None))]
                            for c in range(SUBLANES)], axis=1)


def _slot_tiles(rows):
    return (2 * rows) // MOE_TILE + N_EXPERTS


def _expert_offsets(cnt_ref, off_ref):
    def per_expert(e, k):
        off_ref[e] = k * MOE_TILE
        return k + ((cnt_ref[e] + MOE_TILE - 1) >> MOE_TILE_SHIFT)
    return lax.fori_loop(0, N_EXPERTS, per_expert, 0)


def _slots_kernel(cnt_ref, meta_ref, slot_ref, off_ref):
    @pl.when(pl.program_id(0) == 0)
    def _():
        _expert_offsets(cnt_ref, off_ref)

    e_a, e_b = meta_ref[0:1, :], meta_ref[1:2, :]
    off_a = jnp.zeros_like(e_a)
    off_b = jnp.zeros_like(e_b)
    for e in range(N_EXPERTS):
        off_a = jnp.where(e_a == e, off_ref[e], off_a)
        off_b = jnp.where(e_b == e, off_ref[e], off_b)
    slot_ref[...] = jnp.concatenate([off_a + meta_ref[2:3, :], off_b + meta_ref[3:4, :],
                                     jnp.zeros((6, e_a.shape[1]), jnp.int32)], axis=0)


def _slots(meta, counts, *, tm=1024):
    rows = meta.shape[1]
    grid_spec = pltpu.PrefetchScalarGridSpec(
        num_scalar_prefetch=1, grid=(rows // tm,),
        in_specs=[pl.BlockSpec((8, tm), lambda i, cnt: (0, i))],
        out_specs=pl.BlockSpec((8, tm), lambda i, cnt: (0, i)),
        scratch_shapes=[pltpu.SMEM((N_EXPERTS,), jnp.int32)])
    return pl.pallas_call(
        _slots_kernel, grid_spec=grid_spec, out_shape=jax.ShapeDtypeStruct((8, rows), jnp.int32),
        compiler_params=_cparams("arbitrary"), name="moe_slots",
    )(counts, meta)


def _dispatch_kernel(*refs, steps):
    sa_ref, sb_ref, cnt_ref = refs[:3]
    h_refs = refs[3:3 + len(steps)]
    xs_ref, info_ref, off_ref, zero_ref, sem = refs[3 + len(steps):]
    i = pl.program_id(0)
    tm = h_refs[0].shape[0] // SUBLANES
    n_tiles = info_ref.shape[0] - 1
    tile_rows = MOE_TILE * SUBLANES

    def tile_copy(tile):
        return pltpu.make_async_copy(zero_ref, xs_ref.at[pl.ds(tile * tile_rows, tile_rows), :], sem)

    @pl.when(i == 0)
    def _():
        zero_ref[...] = jnp.zeros_like(zero_ref)
        used = _expert_offsets(cnt_ref, off_ref)

        def per_expert(e, _):
            first = off_ref[e] >> MOE_TILE_SHIFT
            nt = (cnt_ref[e] + MOE_TILE - 1) >> MOE_TILE_SHIFT

            def fill(j, _):
                info_ref[first + j] = e
                return 0
            lax.fori_loop(0, nt, fill, 0)

            @pl.when(nt > 0)
            def _():
                tile_copy(first + nt - 1).start()
                tile_copy(first + nt - 1).wait()
            return 0
        lax.fori_loop(0, N_EXPERTS, per_expert, 0)
        info_ref[n_tiles] = used

        def tail(j, _):
            info_ref[j] = N_EXPERTS - 1
            tile_copy(j).start()
            tile_copy(j).wait()
            return 0
        lax.fori_loop(used, n_tiles, tail, 0)

    base = i * tm

    def copy_rows(h_ref):
        def row_copy(t, slot):
            dst = pl.multiple_of(slot * SUBLANES, SUBLANES)
            return pltpu.make_async_copy(h_ref.at[pl.ds(t * SUBLANES, SUBLANES), :],
                                         xs_ref.at[pl.ds(dst, SUBLANES), :], sem)

        for t in range(tm):
            row_copy(t, sa_ref[base + t]).start(priority=0)
            row_copy(t, sb_ref[base + t]).start(priority=1)
        for _ in range(2):
            pltpu.make_async_copy(h_ref, xs_ref.at[pl.ds(0, tm * SUBLANES), :], sem).wait()

    first = 0
    for h_ref, n in zip(h_refs, steps):
        pl.when(jnp.logical_and(i >= first, i < first + n))(functools.partial(copy_rows, h_ref))
        first += n


def _dispatch(hs, slots, counts):
    tm = MOE_TOKENS
    steps = tuple(h.shape[0] // (tm * SUBLANES) for h in hs)
    n_tiles = _slot_tiles(sum(steps) * tm)
    specs, first = [], 0
    for n in steps:
        specs.append(pl.BlockSpec((tm * SUBLANES, LANES),
                                  lambda i, *_, first=first, n=n: (jnp.clip(i - first, 0, n - 1), 0)))
        first += n
    grid_spec = pltpu.PrefetchScalarGridSpec(
        num_scalar_prefetch=3, grid=(sum(steps),), in_specs=specs,
        out_specs=[pl.BlockSpec(memory_space=pl.ANY), pl.BlockSpec(memory_space=pltpu.SMEM)],
        scratch_shapes=[pltpu.SMEM((N_EXPERTS,), jnp.int32), pltpu.VMEM((MOE_TILE * SUBLANES, LANES), F32),
                        pltpu.SemaphoreType.DMA(())])
    return pl.pallas_call(
        functools.partial(_dispatch_kernel, steps=steps), grid_spec=grid_spec,
        out_shape=[jax.ShapeDtypeStruct((n_tiles * MOE_TILE * SUBLANES, LANES), F32),
                   jax.ShapeDtypeStruct((n_tiles + 1,), jnp.int32)],
        compiler_params=_cparams("arbitrary"), name="moe_dispatch",
    )(slots[0], slots[1], counts, *hs)


def _ffn_kernel(info_ref, xs_ref, wg_ref, wu_ref, wd_ref, ys_ref, wg_s, wu_s, wd_s):
    i = pl.program_id(0)
    used = info_ref[info_ref.shape[0] - 1]
    fresh = jnp.logical_or(i == 0, info_ref[i] != info_ref[jnp.maximum(i - 1, 0)])

    @pl.when(jnp.logical_and(i < used, fresh))
    def _():
        wg_s[...] = wg_ref[0, 0].astype(BF16)
        wu_s[...] = wu_ref[0, 0].astype(BF16)
        wd_s[...] = wd_ref[0, 0].astype(BF16)

    @pl.when(i < used)
    def _():
        part = MOE_TILE // FFN_PARTS
        x = [_tiles_to_rows(xs_ref, part, first=p * part).astype(BF16) for p in range(FFN_PARTS)]
        gate = [_dot(xp, wg_s[...]) for xp in x]
        up = [_dot(xp, wu_s[...]) for xp in x]
        hid = [(_silu(g) * u).astype(BF16) for g, u in zip(gate, up)]
        y = [_dot(hp, wd_s[...]) for hp in hid]
        for p in range(FFN_PARTS):
            _rows_to_tiles(ys_ref, y[p], first=p * part)

    @pl.when(i >= used)
    def _():
        ys_ref[...] = jnp.zeros_like(ys_ref)


def _ffn(xs, info, wg, wu, wd, layer):
    tile_rows = MOE_TILE * SUBLANES
    n_tiles = xs.shape[0] // tile_rows
    d, f = wg.shape[2:]

    def w_map(i, info):
        return (layer, info[i], 0, 0)

    grid_spec = pltpu.PrefetchScalarGridSpec(
        num_scalar_prefetch=1, grid=(n_tiles,),
        in_specs=[pl.BlockSpec((tile_rows, LANES), lambda i, info: (jnp.minimum(i, info[n_tiles] - 1), 0)),
                  pl.BlockSpec((1, 1, d, f), w_map), pl.BlockSpec((1, 1, d, f), w_map),
                  pl.BlockSpec((1, 1, f, d), w_map)],
        out_specs=pl.BlockSpec((tile_rows, LANES), lambda i, info: (i, 0)),
        scratch_shapes=[pltpu.VMEM((d, f), BF16), pltpu.VMEM((d, f), BF16), pltpu.VMEM((f, d), BF16)])
    return pl.pallas_call(
        _ffn_kernel, grid_spec=grid_spec, out_shape=jax.ShapeDtypeStruct(xs.shape, F32),
        compiler_params=_cparams("arbitrary"), name="moe_ffn",
    )(info, xs, wg, wu, wd)


def _combine_kernel(*refs, has_mod, out_x):
    refs = list(refs)
    sa_ref, sb_ref, x_ref, ys_ref, wcol_ref, gate_ref, g_ref = refs[:7]
    refs = refs[7:]
    mod_refs = (refs.pop(0), refs.pop(0)) if has_mod else None
    xo_ref = refs.pop(0) if out_x else None
    h_ref, buf_a, buf_b, sems = refs
    i = pl.program_id(0)
    tm = x_ref.shape[0]

    def issue(tile, slot):
        base = tile * tm
        for t in range(tm):
            dst = pl.ds(t * SUBLANES, SUBLANES)
            src_a = pl.multiple_of(sa_ref[base + t] * SUBLANES, SUBLANES)
            src_b = pl.multiple_of(sb_ref[base + t] * SUBLANES, SUBLANES)
            pltpu.make_async_copy(ys_ref.at[pl.ds(src_a, SUBLANES), :], buf_a.at[slot, dst, :],
                                  sems.at[slot]).start(priority=0)
            pltpu.make_async_copy(ys_ref.at[pl.ds(src_b, SUBLANES), :], buf_b.at[slot, dst, :],
                                  sems.at[slot]).start(priority=1)

    @pl.when(i == 0)
    def _():
        issue(0, 0)

    @pl.when(i + 1 < pl.num_programs(0))
    def _():
        issue(i + 1, (i + 1) % 2)

    slot = i % 2
    for buf in (buf_a, buf_b):
        pltpu.make_async_copy(ys_ref.at[pl.ds(0, tm * SUBLANES), :], buf.at[slot], sems.at[slot]).wait()
    y = (wcol_ref[:, 0:1] * _tiles_to_rows(buf_a, tm, lead=(slot,))
         + wcol_ref[:, 1:2] * _tiles_to_rows(buf_b, tm, lead=(slot,)))
    x = x_ref[...] + gate_ref[0] * y
    if out_x:
        xo_ref[...] = x
    h_ref[...] = _norm_mod(x, g_ref, mod_refs).astype(h_ref.dtype)


def _combine(x, ys, slots, wcol, gate, g, *, mod=None, out_x=False, h_dtype=BF16):
    rows, d = x.shape
    tm = MOE_TOKENS
    row_spec = pl.BlockSpec((tm, d), lambda i, *_: (i, 0))
    args = [x, ys, wcol, gate, g.reshape(1, d)]
    specs = [row_spec, pl.BlockSpec(memory_space=pl.ANY), pl.BlockSpec((tm, LANES), lambda i, *_: (i, 0)),
             _mod_spec(gate.shape[0], rows, tm, d), pl.BlockSpec((1, d), lambda i, *_: (0, 0))]
    if mod is not None:
        for m in mod:
            args.append(m)
            specs.append(_mod_spec(m.shape[0], rows, tm, d))
    out_shape, out_specs = [], []
    if out_x:
        out_shape.append(jax.ShapeDtypeStruct((rows, d), F32))
        out_specs.append(row_spec)
    out_shape.append(jax.ShapeDtypeStruct((rows, d), h_dtype))
    out_specs.append(row_spec)
    grid_spec = pltpu.PrefetchScalarGridSpec(
        num_scalar_prefetch=2, grid=(rows // tm,), in_specs=specs, out_specs=out_specs,
        scratch_shapes=[pltpu.VMEM((2, tm * SUBLANES, LANES), F32), pltpu.VMEM((2, tm * SUBLANES, LANES), F32),
                        pltpu.SemaphoreType.DMA((2,))])
    outs = pl.pallas_call(
        functools.partial(_combine_kernel, has_mod=mod is not None, out_x=out_x), grid_spec=grid_spec,
        out_shape=out_shape, compiler_params=_cparams("arbitrary"), name="moe_combine",
    )(slots[0], slots[1], *args)
    return outs if out_x else outs[0]


def _softmax_av(groups):
    maxes = []
    for scores, _, sink in groups:
        m = sink
        for s in scores:
            m = jnp.maximum(m, jnp.broadcast_to(jnp.max(s, axis=-1, keepdims=True), sink.shape))
        maxes.append(m)
    probs = [[jnp.exp(s - jnp.concatenate([m] * (s.shape[1] // LANES), axis=1)).astype(BF16) for s in scores]
             for (scores, _, _), m in zip(groups, maxes)]
    outs = []
    for (_, values, sink), m, ps in zip(groups, maxes, probs):
        hd = values[0].shape[1]
        tot = None
        for p, v in zip(ps, values):
            n = v.shape[0]
            v_ext = jnp.concatenate([v, jnp.zeros((n, LANES - hd), BF16), jnp.ones((n, LANES), BF16)], axis=1)
            pv = _dot(p, v_ext)
            tot = pv if tot is None else tot + pv
        outs.append((tot, sink, m, hd))
    return [tot[:, :hd] / (tot[:, LANES:] + jnp.exp(sink - m))[:, :hd] for tot, sink, m, hd in outs]


def _sink_column(sink_ref, kv, rows):
    return jnp.concatenate([jnp.full((rows, LANES), sink_ref[kv * ATT_GROUP + g], F32) for g in range(ATT_GROUP)],
                           axis=0)


def _attn_ctx_kernel(sink_ref, qkv_ref, o_ref):
    t = qkv_ref.shape[0]
    qw = ATT_HEADS * HEAD_DIM
    kw = ATT_KV * HEAD_DIM
    groups = []
    for kv in range(ATT_KV):
        q = jnp.concatenate(
            [qkv_ref[:, (kv * ATT_GROUP + g) * HEAD_DIM:(kv * ATT_GROUP + g + 1) * HEAD_DIM] for g in range(ATT_GROUP)],
            axis=0).astype(BF16)
        k = qkv_ref[:, qw + kv * HEAD_DIM:qw + (kv + 1) * HEAD_DIM].astype(BF16)
        v = qkv_ref[:, qw + kw + kv * HEAD_DIM:qw + kw + (kv + 1) * HEAD_DIM].astype(BF16)
        groups.append(([_dot_nt(q, k) * HEAD_DIM ** -0.5], [v], _sink_column(sink_ref, kv, t)))
    heads_out = [o[g * t:(g + 1) * t] for o in _softmax_av(groups) for g in range(ATT_GROUP)]
    o_ref[...] = jnp.concatenate(heads_out, axis=1).astype(o_ref.dtype)


def _attn_ctx(qkv, sink, n_seq, seq_len):
    rows, cols = qkv.shape
    return pl.pallas_call(
        _attn_ctx_kernel, grid=(n_seq,),
        in_specs=[pl.BlockSpec(memory_space=pltpu.SMEM), pl.BlockSpec((seq_len, cols), lambda b: (b, 0))],
        out_specs=pl.BlockSpec((seq_len, ATT_HEADS * HEAD_DIM), lambda b: (b, 0)),
        out_shape=jax.ShapeDtypeStruct((rows, ATT_HEADS * HEAD_DIM), BF16),
        compiler_params=_cparams("parallel"), name="attn_context",
    )(sink, qkv)


def _rope_block(x, cos, sin_signed):
    lane = lax.broadcasted_iota(jnp.int32, x.shape, 1)
    nf = HEAD_DIM // 4
    partner = jnp.where((lane % (2 * nf)) < nf, pltpu.roll(x, LANES - nf, axis=1), pltpu.roll(x, nf, axis=1))
    return x * cos + partner * sin_signed


def _attn_lat_kernel(sink_ref, qkv_ref, ck_ref, cv_ref, cos_ref, sin_ref, o_ref, k_scr):
    i = pl.program_id(1)
    t = qkv_ref.shape[0]
    qw = ATT_HEADS * HEAD_DIM
    kw = ATT_KV * HEAD_DIM
    span = Q_BLOCK + 2 * WINDOW

    @pl.when(i == 0)
    def _():
        for c in range(kw // LANES):
            blk = qkv_ref[:, qw + c * LANES:qw + (c + 1) * LANES]
            k_scr[:, c * LANES:(c + 1) * LANES] = _rope_block(blk, cos_ref[...], sin_ref[...]).astype(BF16)

    r0 = pl.multiple_of(i * Q_BLOCK, Q_BLOCK)
    ws = pl.multiple_of(jnp.clip(r0 - WINDOW, 0, t - span), Q_BLOCK)
    cos_q = cos_ref[pl.ds(r0, Q_BLOCK), :]
    sin_q = sin_ref[pl.ds(r0, Q_BLOCK), :]
    qpos = r0 + lax.broadcasted_iota(jnp.int32, (Q_BLOCK, span), 0)
    kpos = ws + lax.broadcasted_iota(jnp.int32, (Q_BLOCK, span), 1)
    band = jnp.abs(qpos - kpos) <= WINDOW
    band = jnp.concatenate([band] * ATT_GROUP, axis=0)
    roped = [_rope_block(qkv_ref[pl.ds(r0, Q_BLOCK), c * LANES:(c + 1) * LANES], cos_q, sin_q)
             for c in range(qw // LANES)]
    groups = []
    for kv in range(ATT_KV):
        heads = []
        for g in range(ATT_GROUP):
            c, half = divmod((kv * ATT_GROUP + g) * HEAD_DIM, LANES)
            heads.append(roped[c][:, half:half + HEAD_DIM])
        q = jnp.concatenate(heads, axis=0).astype(BF16)
        ck = ck_ref[0, :, kv * HEAD_DIM:(kv + 1) * HEAD_DIM].astype(BF16)
        cv = cv_ref[0, :, kv * HEAD_DIM:(kv + 1) * HEAD_DIM].astype(BF16)
        kwin = k_scr[pl.ds(ws, span), kv * HEAD_DIM:(kv + 1) * HEAD_DIM]
        vwin = qkv_ref[pl.ds(ws, span), qw + kw + kv * HEAD_DIM:qw + kw + (kv + 1) * HEAD_DIM].astype(BF16)
        s_ctx = _dot_nt(q, ck) * HEAD_DIM ** -0.5
        s_win = jnp.where(band, _dot_nt(q, kwin) * HEAD_DIM ** -0.5, -jnp.inf)
        groups.append(([s_ctx, s_win], [cv, vwin], _sink_column(sink_ref, kv, Q_BLOCK)))
    heads_out = [o[g * Q_BLOCK:(g + 1) * Q_BLOCK] for o in _softmax_av(groups) for g in range(ATT_GROUP)]
    o_ref[...] = jnp.concatenate(heads_out, axis=1).astype(o_ref.dtype)


def _rope_tables(seq_len):
    pos = jnp.arange(seq_len, dtype=jnp.int32)
    row = (pos // GRID_W).astype(F32)
    col = (pos % GRID_W).astype(F32)
    nf = HEAD_DIM // 4
    inv = ROPE_BASE ** (-jnp.arange(nf, dtype=F32) / nf)
    ang_r = row[:, None] * inv[None, :]
    ang_c = col[:, None] * inv[None, :]
    cos_h = jnp.concatenate([jnp.cos(ang_r), jnp.cos(ang_r), jnp.cos(ang_c), jnp.cos(ang_c)], axis=1)
    sin_h = jnp.concatenate([-jnp.sin(ang_r), jnp.sin(ang_r), -jnp.sin(ang_c), jnp.sin(ang_c)], axis=1)
    reps = LANES // HEAD_DIM
    return jnp.tile(cos_h, (1, reps)), jnp.tile(sin_h, (1, reps))


def _attn_lat(qkv, cache_k, cache_v, sink, n_seq, seq_len):
    rows, cols = qkv.shape
    past = cache_k.shape[1]
    kw = ATT_KV * HEAD_DIM
    cos, sin = _rope_tables(seq_len)
    return pl.pallas_call(
        _attn_lat_kernel, grid=(n_seq, seq_len // Q_BLOCK),
        in_specs=[pl.BlockSpec(memory_space=pltpu.SMEM),
                  pl.BlockSpec((seq_len, cols), lambda b, i: (b, 0)),
                  pl.BlockSpec((1, past, kw), lambda b, i: (b, 0, 0)),
                  pl.BlockSpec((1, past, kw), lambda b, i: (b, 0, 0)),
                  pl.BlockSpec((seq_len, LANES), lambda b, i: (0, 0)),
                  pl.BlockSpec((seq_len, LANES), lambda b, i: (0, 0))],
        out_specs=pl.BlockSpec((Q_BLOCK, ATT_HEADS * HEAD_DIM), lambda b, i: (b * (seq_len // Q_BLOCK) + i, 0)),
        out_shape=jax.ShapeDtypeStruct((rows, ATT_HEADS * HEAD_DIM), BF16),
        scratch_shapes=[pltpu.VMEM((seq_len, kw), BF16)],
        compiler_params=_cparams("parallel", "arbitrary"), name="attn_latent",
    )(sink, qkv, cache_k.reshape(n_seq, past, kw), cache_v.reshape(n_seq, past, kw), cos, sin)


def _split_bf16(w):
    hi = w.astype(BF16)
    return hi, (w - hi.astype(F32)).astype(BF16)


def _select_lanes(x, sel):
    x1 = x.astype(BF16)
    rest = x - x1.astype(F32)
    x2 = rest.astype(BF16)
    x3 = (rest - x2.astype(F32)).astype(BF16)
    return _dot(x1, sel) + _dot(x2, sel) + _dot(x3, sel)


def _ml_gates_kernel(h_ref, w_ref, wt_ref, b_ref, bt_ref, g_ref, gt_ref):
    h = h_ref[...]
    h_hi = h.astype(BF16)
    h_lo = (h - h_hi.astype(F32)).astype(BF16)
    ng = gt_ref.shape[0]
    by_hi = _dot(h_hi, w_ref[...])
    g_ref[...] = by_hi[:, :LANES] + by_hi[:, LANES:] + _dot(h_lo, w_ref[:, :LANES]) + b_ref[...]
    by_hi_t = _dot_nt(wt_ref[...], h_hi)
    gt_ref[...] = by_hi_t[:ng] + by_hi_t[ng:] + _dot_nt(wt_ref[:ng, :], h_lo) + bt_ref[...]


def _ml_gates(h, w_gates, b_gates, *, tm=512):
    rows, d = h.shape
    ng = w_gates.shape[1]
    w_hi, w_lo = _split_bf16(jnp.pad(w_gates, ((0, 0), (0, LANES - ng))))
    wt_hi, wt_lo = _split_bf16(w_gates.T)
    b_pad = jnp.pad(b_gates, (0, LANES - ng)).reshape(1, LANES)
    return pl.pallas_call(
        _ml_gates_kernel, grid=(rows // tm,),
        in_specs=[pl.BlockSpec((tm, d), lambda i: (i, 0)), pl.BlockSpec((d, 2 * LANES), lambda i: (0, 0)),
                  pl.BlockSpec((2 * ng, d), lambda i: (0, 0)), pl.BlockSpec((1, LANES), lambda i: (0, 0)),
                  pl.BlockSpec((ng, 1), lambda i: (0, 0))],
        out_specs=[pl.BlockSpec((tm, LANES), lambda i: (i, 0)), pl.BlockSpec((ng, tm), lambda i: (0, i))],
        out_shape=[jax.ShapeDtypeStruct((rows, LANES), F32), jax.ShapeDtypeStruct((ng, rows), F32)],
        compiler_params=_cparams("parallel"), name="mlstm_gates",
    )(h, jnp.concatenate([w_hi, w_lo], axis=1), jnp.concatenate([wt_hi, wt_lo], axis=0), b_pad,
      b_gates.reshape(ng, 1))


def _ml_qk_kernel(h_ref, w_ref, cw_ref, o_ref, *, seq_len, k_scale):
    j = pl.program_id(1)
    x = _dot(h_ref[...].astype(BF16), w_ref[...])
    t = x.shape[0]
    pos = lax.broadcasted_iota(jnp.int32, x.shape, 0) % seq_len
    prev = jnp.where(pos == 0, 0.0, pltpu.roll(x, 1, axis=0))
    nxt = jnp.where(pos == seq_len - 1, 0.0, pltpu.roll(x, t - 1, axis=0))
    y = prev * cw_ref[0:1, :] + x * cw_ref[1:2, :] + nxt * cw_ref[2:3, :]
    scale = jnp.where(j >= pl.num_programs(1) // 2, k_scale, 1.0).astype(F32)
    o_ref[...] = (_silu(y) * scale).astype(o_ref.dtype)


def _ml_qk(h, w_qk, conv_w, seq_len, *, tm=1024, tn=1024):
    rows, d = h.shape
    width = w_qk.shape[1]
    return pl.pallas_call(
        functools.partial(_ml_qk_kernel, seq_len=seq_len, k_scale=ML_DK ** -0.5), grid=(rows // tm, width // tn),
        in_specs=[pl.BlockSpec((tm, d), lambda i, j: (i, 0)), pl.BlockSpec((d, tn), lambda i, j: (0, j)),
                  pl.BlockSpec((3, tn), lambda i, j: (0, j))],
        out_specs=pl.BlockSpec((tm, tn), lambda i, j: (i, j)),
        out_shape=jax.ShapeDtypeStruct((rows, width), BF16),
        compiler_params=_cparams("parallel", "parallel"), name="mlstm_qk",
    )(h, w_qk, conv_w)


def _ml_scan_kernel(*refs, zero_init):
    refs = list(refs)
    dirs = [tuple(refs[0:5]), tuple(refs[5:10])]
    sel_ref = refs[10]
    refs = refs[11:]
    if not zero_init:
        c0_ref, n0_ref, m0_ref = refs[:3]
        refs = refs[3:]
    hf_ref, hb_ref, c_ref, n_ref, m_ref = refs
    h_out = (hf_ref, hb_ref)
    c = pl.program_id(1)
    last = pl.num_programs(1) - 1

    @pl.when(c == 0)
    def _():
        if zero_init:
            c_ref[...] = jnp.zeros_like(c_ref)
            n_ref[...] = jnp.zeros_like(n_ref)
            m_ref[...] = jnp.zeros_like(m_ref)
        else:
            c_ref[...] = c0_ref[...]
            n_ref[...] = n0_ref[...]
            m_ref[...] = m0_ref[...]

    length = hf_ref.shape[0]
    ti = lax.broadcasted_iota(jnp.int32, (length, length), 0)
    si = lax.broadcasted_iota(jnp.int32, (length, length), 1)
    causal = [ti >= si, ti <= si]
    edge = [length - 1, 0]
    b_row, i_rep, b_rep = [], [], []
    for d in range(2):
        g_ref, gt_ref = dirs[d][3], dirs[d][4]
        tri = jnp.where(causal[d], 1.0, 0.0).astype(F32)
        g_col = g_ref[...]
        f_row = _log_sigmoid(gt_ref[...])
        b_col = jnp.dot(tri, _log_sigmoid(g_col), precision=HI, preferred_element_type=F32)
        b_row.append(_dot_nt(f_row, tri, precision=HI))
        i_rep.append(_select_lanes(g_col, sel_ref[d, 0]))
        b_rep.append(_select_lanes(b_col, sel_ref[d, 1]))
    chains = [(d, h) for d in range(2) for h in range(ML_HEADS)]
    q = {(d, h): dirs[d][0][:, h * ML_DK:(h + 1) * ML_DK] for d, h in chains}
    k = {(d, h): dirs[d][1][:, h * ML_DK:(h + 1) * ML_DK] for d, h in chains}
    v = {(d, h): dirs[d][2][:, h * ML_DV:(h + 1) * ML_DV].astype(BF16) for d, h in chains}
    cst = {c: c_ref[0, c[0], c[1]] for c in chains}
    nst = {c: n_ref[0, c[0], c[1]] for c in chains}
    m_rep = {c: m_ref[0, c[0], c[1]] for c in chains}
    bc = {(d, h): b_rep[d][:, h * LANES:(h + 1) * LANES] for d, h in chains}
    i_col = {(d, h): i_rep[d][:, h * LANES:(h + 1) * LANES] for d, h in chains}
    qk_raw = {c: _dot_nt(q[c], k[c]) for c in chains}
    inter = {c: _dot_nt(q[c], jnp.concatenate([cst[c], jnp.broadcast_to(nst[c], (LANES, ML_DK))],
                                              axis=0).astype(BF16)) for c in chains}
    a_row = {}
    for d, h in chains:
        ji = d * 2 * ML_HEADS + h
        a_row[d, h] = dirs[d][4][ji:ji + 1, :] - b_row[d][ji + ML_HEADS:ji + ML_HEADS + 1, :]
    amat = {c: jnp.where(causal[c[0]], a_row[c], -jnp.inf) for c in chains}
    u = {c: jnp.maximum(m_rep[c], jnp.broadcast_to(jnp.max(amat[c], axis=1, keepdims=True), (length, LANES)))
         for c in chains}
    qk = {c: (qk_raw[c] * jnp.exp(amat[c] - u[c])).astype(BF16) for c in chains}
    intra = {c: _dot(qk[c], jnp.concatenate([v[c], jnp.ones((length, LANES), BF16)], axis=1)) for c in chains}
    b_last = {c: bc[c][edge[c[0]]:edge[c[0]] + 1, :] for c in chains}
    wlog_row = {c: b_last[c] + a_row[c] for c in chains}
    m_new = {c: jnp.maximum(b_last[c] + m_rep[c], jnp.max(wlog_row[c], axis=1, keepdims=True)) for c in chains}
    ws_row = {c: jnp.exp(wlog_row[c] - m_new[c]) for c in chains}
    kw = {c: (jnp.exp(b_last[c] - bc[c] + i_col[c] - m_new[c]) * k[c].astype(F32)).astype(BF16) for c in chains}
    c_upd = {c: _dot_tn(v[c], kw[c]) for c in chains}
    n_upd = {c: _dot(jnp.broadcast_to(ws_row[c], (8, length)).astype(BF16), k[c])[0:1] for c in chains}
    for c in chains:
        d, h = c
        sc = jnp.exp(m_rep[c] - u[c])
        tot = jnp.concatenate([sc] * (ML_DV // LANES + 1), axis=1) * inter[c] + intra[c]
        inv = 1.0 / jnp.maximum(jnp.abs(tot[:, ML_DV:]), jnp.exp(-(bc[c] + u[c])))
        h_out[d][:, h * ML_DV:(h + 1) * ML_DV] = tot[:, :ML_DV] * jnp.concatenate([inv] * (ML_DV // LANES), axis=1)
        decay = jnp.exp(b_last[c] + m_rep[c] - m_new[c])
        c_ref[0, d, h] = decay * cst[c] + c_upd[c]
        n_ref[0, d, h] = decay * nst[c] + n_upd[c]
        m_ref[0, d, h] = m_new[c]


def _ml_scan(qk, p, g, gt, state, n_seq, seq_len):
    rows = qk.shape[0]
    length = min(ML_CHUNK, seq_len)
    nc = seq_len // length
    qw = ML_HEADS * ML_DK
    vw = ML_HEADS * ML_DV
    ng = gt.shape[0]

    def fwd(b, c):
        return b * nc + c

    def bwd(b, c):
        return b * nc + nc - 1 - c

    args, specs = [], []
    for pos in (fwd, bwd):
        args += [qk, qk, p, g, gt]
        specs += [pl.BlockSpec((length, qw), lambda b, c, pos=pos: (pos(b, c), 0)),
                  pl.BlockSpec((length, qw), lambda b, c, pos=pos: (pos(b, c), 1)),
                  pl.BlockSpec((length, vw), lambda b, c, pos=pos: (pos(b, c), 0)),
                  pl.BlockSpec((length, LANES), lambda b, c, pos=pos: (pos(b, c), 0)),
                  pl.BlockSpec((ng, length), lambda b, c, pos=pos: (0, pos(b, c)))]
    assert length == LANES and ML_DK == LANES
    gate_lane = jnp.arange(LANES)[:, None]
    head = (jnp.arange(ML_HEADS * LANES) // LANES)[None, :]
    sel = jnp.stack([jnp.stack([gate_lane == (2 * d + kind) * ML_HEADS + head for kind in range(2)])
                     for d in range(2)]).astype(BF16)
    args.append(sel)
    specs.append(pl.BlockSpec(sel.shape, lambda b, c: (0, 0, 0, 0)))
    c_spec = pl.BlockSpec((1, 2, ML_HEADS, ML_DV, ML_DK), lambda b, c: (b, 0, 0, 0, 0))
    n_spec = pl.BlockSpec((1, 2, ML_HEADS, 1, ML_DK), lambda b, c: (b, 0, 0, 0, 0))
    zero_init = state is None
    if not zero_init:
        c0, n0, m0 = state
        args += [c0, n0.reshape(n_seq, 2, ML_HEADS, 1, ML_DK),
                 jnp.broadcast_to(m0[..., None, None], (n_seq, 2, ML_HEADS, 1, ML_DK))]
        specs += [c_spec, n_spec, n_spec]
    hf, hb, c_fin, n_fin, m_fin = pl.pallas_call(
        functools.partial(_ml_scan_kernel, zero_init=zero_init), grid=(n_seq, nc), in_specs=specs,
        out_specs=[pl.BlockSpec((length, vw), lambda b, c: (fwd(b, c), 0)),
                   pl.BlockSpec((length, vw), lambda b, c: (bwd(b, c), 0)), c_spec, n_spec, n_spec],
        out_shape=[jax.ShapeDtypeStruct((rows, vw), F32), jax.ShapeDtypeStruct((rows, vw), F32),
                   jax.ShapeDtypeStruct((n_seq, 2, ML_HEADS, ML_DV, ML_DK), F32),
                   jax.ShapeDtypeStruct((n_seq, 2, ML_HEADS, 1, ML_DK), F32),
                   jax.ShapeDtypeStruct((n_seq, 2, ML_HEADS, 1, ML_DK), F32)],
        compiler_params=_cparams("parallel", "arbitrary"), name="mlstm_scan",
    )(*args)
    return hf, hb, (c_fin, n_fin[:, :, :, 0, :], m_fin[:, :, :, 0, 0])


def _gla_scan_kernel(*refs, zero_init):
    refs = list(refs)
    dirs = [tuple(refs[0:4]), tuple(refs[4:8])]
    w2_ref, ba_ref = refs[8:10]
    refs = refs[10:]
    if not zero_init:
        s0_ref = refs.pop(0)
    of_ref, ob_ref, s_ref, st_scr, la_scr = refs
    o_out = (of_ref, ob_ref)
    c = pl.program_id(1)
    last = pl.num_programs(1) - 1
    kw = GLA_HEADS * GLA_DK
    n_sub = of_ref.shape[0] // GLA_SUB

    @pl.when(c == 0)
    def _():
        for d in range(2):
            for h in range(GLA_HEADS):
                st_scr[d, h] = jnp.zeros((GLA_DV, GLA_DK), F32) if zero_init else s0_ref[0, d, h].T

    for d in range(2):
        u = dirs[d][3][...].astype(BF16)
        z = _dot(u, w2_ref[:, d * kw:(d + 1) * kw]) + ba_ref[:, d * kw:(d + 1) * kw]
        la = _log_sigmoid(z) / GLA_TAU
        la_scr[d] = la
        totals = jnp.sum(la.reshape(n_sub, GLA_SUB, kw), axis=1)
        worst = jnp.min(totals) if d == 0 else jnp.minimum(worst, jnp.min(totals))
    decay_bounded = worst * LOG2E > -GLA_MAX_EXP2

    ti = lax.broadcasted_iota(jnp.int32, (GLA_SUB, GLA_SUB), 0)
    si = lax.broadcasted_iota(jnp.int32, (GLA_SUB, GLA_SUB), 1)
    s_lane = lax.broadcasted_iota(jnp.int32, (GLA_SUB, GLA_SUB), 1)

    def sub_chunk(j, carry, bounded):
        steps = [(d, t) for d in range(2) for t in range(GLA_STEPS)]
        chains = [(d, h, t) for d, t in steps for h in range(GLA_HEADS)]
        rows = {(d, t): pl.ds(pl.multiple_of(
            ((j * GLA_STEPS + t) if d == 0 else n_sub - 1 - (j * GLA_STEPS + t)) * GLA_SUB, GLA_SUB), GLA_SUB)
            for d, t in steps}
        causal = [ti >= si, ti <= si]
        edge = [GLA_SUB - 1, 0]
        bc_all = {(d, t): jnp.dot(jnp.where(causal[d], 1.0, 0.0).astype(F32), la_scr[d, rows[d, t], :], precision=HI,
                                  preferred_element_type=F32) * LOG2E for d, t in steps}
        q, k, v, bc2, b_last, k_dec, q_dec, a = {}, {}, {}, {}, {}, {}, {}, {}
        for c in chains:
            d, h, t = c
            q_ref, k_ref, v_ref, _ = dirs[d]
            bc2[c] = bc_all[d, t][:, h * GLA_DK:(h + 1) * GLA_DK]
            q[c] = q_ref[rows[d, t], h * GLA_DK:(h + 1) * GLA_DK] * GLA_DK ** -0.5
            k[c] = k_ref[rows[d, t], h * GLA_DK:(h + 1) * GLA_DK]
            v[c] = v_ref[rows[d, t], h * GLA_DV:(h + 1) * GLA_DV].astype(BF16)
            b_last[c] = bc2[c][edge[d]:edge[d] + 1, :]
            k_dec[c] = (k[c] * jnp.exp2(b_last[c] - bc2[c])).astype(BF16)
            q_dec[c] = q[c] * jnp.exp2(bc2[c])
        for c in chains:
            if bounded:
                a[c] = _dot_nt((q_dec[c] * jnp.exp2(-b_last[c])).astype(BF16), k_dec[c])
            else:
                acc = jnp.zeros((GLA_SUB, GLA_SUB), F32)
                for s in range(GLA_SUB):
                    decay = jnp.exp2(bc2[c] - bc2[c][s:s + 1, :])
                    col = jnp.sum(q[c] * (k[c][s:s + 1, :] * decay), axis=1, keepdims=True)
                    acc = jnp.where(s_lane == s, col, acc)
                a[c] = acc
        kv = {c: _dot_tn(v[c], k_dec[c]) for c in chains}
        intra = {c: _dot(jnp.where(causal[c[0]], a[c], 0.0).astype(BF16), v[c]) for c in chains}
        state = {(d, h): st_scr[d, h] for d in range(2) for h in range(GLA_HEADS)}
        for t in range(GLA_STEPS):
            now = [c for c in chains if c[2] == t]
            inter = {c: _dot_nt(q_dec[c].astype(BF16), state[c[0], c[1]].astype(BF16)) for c in now}
            for c in now:
                d, h, _ = c
                o_out[d][rows[d, t], h * GLA_DV:(h + 1) * GLA_DV] = intra[c] + inter[c]
                state[d, h] = jnp.exp2(b_last[c]) * state[d, h] + kv[c]
        for (d, h), st in state.items():
            st_scr[d, h] = st
        return carry

    @pl.when(decay_bounded)
    def _():
        lax.fori_loop(0, n_sub // GLA_STEPS, functools.partial(sub_chunk, bounded=True), 0)

    @pl.when(jnp.logical_not(decay_bounded))
    def _():
        lax.fori_loop(0, n_sub // GLA_STEPS, functools.partial(sub_chunk, bounded=False), 0)

    @pl.when(c == last)
    def _():
        for d in range(2):
            for h in range(GLA_HEADS):
                s_ref[0, d, h] = st_scr[d, h].T


def _gla_scan(p, u, w2, b_a, state, n_seq, seq_len):
    rows = p.shape[0]
    length = min(GLA_BLOCK, seq_len)
    nc = seq_len // length
    kw = GLA_HEADS * GLA_DK
    vw = GLA_HEADS * GLA_DV

    def fwd(b, c):
        return b * nc + c

    def bwd(b, c):
        return b * nc + nc - 1 - c

    args, specs = [], []
    for pos in (fwd, bwd):
        args += [p, p, p, u]
        specs += [pl.BlockSpec((length, kw), lambda b, c, pos=pos: (pos(b, c), 0)),
                  pl.BlockSpec((length, kw), lambda b, c, pos=pos: (pos(b, c), 1)),
                  pl.BlockSpec((length, vw), lambda b, c, pos=pos: (pos(b, c), 2 * kw // vw)),
                  pl.BlockSpec((length, LANES), lambda b, c, pos=pos: (pos(b, c), 0))]
    args += [w2, b_a]
    specs += [pl.BlockSpec(w2.shape, lambda b, c: (0, 0)), pl.BlockSpec(b_a.shape, lambda b, c: (0, 0))]
    s_spec = pl.BlockSpec((1, 2, GLA_HEADS, GLA_DK, GLA_DV), lambda b, c: (b, 0, 0, 0, 0))
    zero_init = state is None
    if not zero_init:
        args.append(state)
        specs.append(s_spec)
    return pl.pallas_call(
        functools.partial(_gla_scan_kernel, zero_init=zero_init), grid=(n_seq, nc), in_specs=specs,
        out_specs=[pl.BlockSpec((length, vw), lambda b, c: (fwd(b, c), 0)),
                   pl.BlockSpec((length, vw), lambda b, c: (bwd(b, c), 0)), s_spec],
        out_shape=[jax.ShapeDtypeStruct((rows, vw), F32), jax.ShapeDtypeStruct((rows, vw), F32),
                   jax.ShapeDtypeStruct((n_seq, 2, GLA_HEADS, GLA_DK, GLA_DV), F32)],
        scratch_shapes=[pltpu.VMEM((2, GLA_HEADS, GLA_DV, GLA_DK), F32), pltpu.VMEM((2, length, kw), F32)],
        compiler_params=_cparams("parallel", "arbitrary"), name="gla_scan",
    )(*args)


def kernel(x_prompt, x_sample, cache_k_0, cache_v_0, state_mlstm_C_1, state_mlstm_n_1, state_mlstm_m_1, state_gla_S_2, cache_k_3, cache_v_3, c, c_ctx, w_mod, b_mod, norm1_g, norm2_g, final_g, router_w, router_b, moe_wg, moe_wu, moe_wd, attn0_w_qkv, attn0_sink, attn0_w_o, mlstm1_w_in, mlstm1_b_gates, mlstm1_conv, mlstm1_norm_g, mlstm1_w_out, gla2_w_in, gla2_w_a1, gla2_w_a2, gla2_b_a, gla2_norm_g, gla2_w_out, attn3_w_qkv, attn3_sink, attn3_w_o):
    n_ctx, ctx_len, d = x_prompt.shape
    n_lat, lat_len, _ = x_sample.shape
    depth = w_mod.shape[0]

    cvec = jnp.concatenate([c_ctx[None, :], c, jnp.zeros((8 - 1 - n_lat, d), F32)], axis=0)
    mod = _modulation(cvec, w_mod, b_mod).reshape(depth, 8, 6, 1, d)

    def mods(layer, kind, latent):
        return mod[layer, 1:1 + n_lat, kind] if latent else mod[layer, 0:1, kind]

    rw_hi = router_w.T.astype(BF16)
    rw_lo = (router_w.T - rw_hi.astype(F32)).astype(BF16)
    rwt = jnp.concatenate([rw_hi, rw_lo], axis=0)
    rb = router_b.reshape(-1, 1)
    attn_w = {0: (attn0_w_qkv.astype(BF16), attn0_sink, attn0_w_o.astype(BF16), cache_k_0, cache_v_0),
              3: (attn3_w_qkv.astype(BF16), attn3_sink, attn3_w_o.astype(BF16), cache_k_3, cache_v_3)}
    ml_qw = ML_HEADS * ML_DK
    ml_vw = ML_HEADS * ML_DV
    ml_main = 2 * ml_qw + 2 * ml_vw
    ml_w_qk = mlstm1_w_in[:, :2 * ml_qw].astype(BF16)
    ml_w_vo = mlstm1_w_in[:, 2 * ml_qw:ml_main].astype(BF16)
    ml_w_gates = mlstm1_w_in[:, ml_main:]
    ml_w_out = mlstm1_w_out.astype(BF16)
    gla_kw = GLA_HEADS * GLA_DK
    gla_w_in = gla2_w_in.astype(BF16)
    gla_w_a1 = jnp.pad(jnp.concatenate([gla2_w_a1[0], gla2_w_a1[1]], axis=1),
                       ((0, 0), (0, LANES - 2 * GLA_RANK))).astype(BF16)
    gla_w2 = jnp.zeros((LANES, 2 * gla_kw), F32)
    gla_w2 = gla_w2.at[:GLA_RANK, :gla_kw].set(gla2_w_a2[0]).at[GLA_RANK:2 * GLA_RANK, gla_kw:].set(gla2_w_a2[1])
    gla_w2 = gla_w2.astype(BF16)
    gla_ba = gla2_b_a.reshape(1, 2 * gla_kw)
    gla_w_out = gla2_w_out.astype(BF16)

    new_state = []

    def mixer(layer, s, count0):
        latent, n_seq, seq_len, x, h = s["latent"], s["n_seq"], s["seq_len"], s["x"], s["h"]
        tail = (mods(layer, 2, latent), norm2_g[layer], mods(layer, 4, latent), mods(layer, 3, latent), rwt, rb,
                count0)
        kind = layer % 3
        if kind == 0:
            w_qkv, sink, w_o, ck, cv = attn_w[layer]
            qkv = _matmul(h, w_qkv)
            if latent:
                att = _attn_lat(qkv, ck, cv, sink, n_seq, seq_len)
            else:
                att = _attn_ctx(qkv, sink, n_seq, seq_len)
                qw = ATT_HEADS * HEAD_DIM
                kw = ATT_KV * HEAD_DIM
                new_state.append(qkv[:, qw:qw + kw].reshape(n_seq, seq_len, ATT_KV, HEAD_DIM))
                new_state.append(qkv[:, qw + kw:].reshape(n_seq, seq_len, ATT_KV, HEAD_DIM))
            return _proj("plain", (att,), w_o, x, *tail)
        if kind == 1:
            p = _matmul(h, ml_w_vo)
            g, gt = _ml_gates(h, ml_w_gates, mlstm1_b_gates)
            qk = _ml_qk(h, ml_w_qk, mlstm1_conv, seq_len)
            st = (state_mlstm_C_1, state_mlstm_n_1, state_mlstm_m_1) if latent else None
            hf, hb, fin = _ml_scan(qk, p, g, gt, st, n_seq, seq_len)
            if not latent:
                new_state.extend(fin)
            return _proj("mlstm", (hf, hb, p, 1, mlstm1_norm_g), ml_w_out, x, *tail)
        p = _matmul(h, gla_w_in)
        u = _matmul(h, gla_w_a1)
        of, ob, s_fin = _gla_scan(p, u, gla_w2, gla_ba, state_gla_S_2 if latent else None, n_seq, seq_len)
        if not latent:
            new_state.append(s_fin)
        gla_vw = GLA_HEADS * GLA_DV
        return _proj("gla", (of, ob, p, (2 * gla_kw + gla_vw) // gla_vw, gla2_norm_g), gla_w_out, x, *tail)

    streams = [dict(latent=False, n_seq=n_ctx, seq_len=ctx_len, x=x_prompt.reshape(n_ctx * ctx_len, d)),
               dict(latent=True, n_seq=n_lat, seq_len=lat_len, x=x_sample.reshape(n_lat * lat_len, d))]
    for s in streams:
        s["h"] = _rownorm(s["x"], norm1_g[0], mods(0, 1, s["latent"]), mods(0, 0, s["latent"]))
    for layer in range(depth):
        counts = jnp.zeros((N_EXPERTS, LANES), jnp.int32)
        for s in streams:
            s["x"], s["h2"], s["meta"], s["wcol"], counts = mixer(layer, s, counts)
        cnt = counts[:, 0]
        for s in streams:
            s["slots"] = _slots(s["meta"], cnt)
        xs, info = _dispatch([s["h2"] for s in streams], jnp.concatenate([s["slots"] for s in streams], axis=1), cnt)
        ys = _ffn(xs, info, moe_wg, moe_wu, moe_wd, layer)
        for s in streams:
            latent = s["latent"]
            gate2 = mods(layer, 5, latent)
            if layer + 1 < depth:
                s["x"], s["h"] = _combine(s["x"], ys, s["slots"], s["wcol"], gate2, norm1_g[layer + 1],
                                          mod=(mods(layer + 1, 1, latent), mods(layer + 1, 0, latent)), out_x=True,
                                          h_dtype=F32 if (layer + 1) % 3 == 1 else BF16)
            else:
                s["out"] = _combine(s["x"], ys, s["slots"], s["wcol"], gate2, final_g, h_dtype=F32)
    y_prompt = streams[0]["out"].reshape(n_ctx, ctx_len, d)
    y_sample = streams[1]["out"].reshape(n_lat, lat_len, d)
    return (y_prompt, y_sample, *new_state)
```

```python
import functools

import jax
import jax.numpy as jnp
from jax import lax
from jax.experimental import pallas as pl
from jax.experimental.pallas import tpu as pltpu

F32 = jnp.float32
BF16 = jnp.bfloat16
HI = lax.Precision.HIGHEST

EPS = 1e-6
LOG2E = 1.4426950408889634
GRID_W = 64
ATT_HEADS = 16
ATT_KV = 4
ATT_GROUP = ATT_HEADS // ATT_KV
HEAD_DIM = 64
WINDOW = 128
Q_BLOCK = 128
ROPE_BASE = 10000.0
ML_HEADS = 8
ML_DK = 128
ML_DV = 256
ML_CHUNK = 128
GLA_HEADS = 4
GLA_DK = 128
GLA_DV = 256
GLA_RANK = 16
GLA_TAU = 16.0
GLA_SUB = 16
GLA_BLOCK = 256
GLA_STEPS = 16
GLA_MAX_EXP2 = 80.0
N_EXPERTS = 16
N_GROUPS = 4
GROUP_SIZE = N_EXPERTS // N_GROUPS
LANES = 128
VMEM_LIMIT = 56 * 1024 * 1024


def _cparams(*sem):
    return pltpu.CompilerParams(dimension_semantics=sem, vmem_limit_bytes=VMEM_LIMIT)


def _dot(a, b):
    return jnp.dot(a, b, preferred_element_type=F32)


def _dot_nt(a, b, precision=None):
    return lax.dot_general(a, b, (((1,), (1,)), ((), ())), precision=precision, preferred_element_type=F32)


def _dot_tn(a, b):
    return lax.dot_general(a, b, (((0,), (0,)), ((), ())), preferred_element_type=F32)


def _sigmoid(x):
    return 1.0 / (1.0 + jnp.exp(-x))


def _silu(x):
    return x * _sigmoid(x)


def _log_sigmoid(x):
    return jnp.minimum(x, 0.0) - jnp.log(1.0 + jnp.exp(-jnp.abs(x)))


def _rms_rows(x, g):
    ms = jnp.mean(x * x, axis=-1, keepdims=True)
    return x * lax.rsqrt(ms + EPS) * g


def _mod_kernel(c_ref, w_ref, b_ref, o_ref):
    s = _silu(c_ref[...])
    o_ref[0] = _dot(s.astype(BF16), w_ref[0].astype(BF16)) + b_ref[0]


def _modulation(cvec, w_mod, b_mod):
    depth, d, n6 = w_mod.shape
    tn = 1536
    return pl.pallas_call(
        _mod_kernel,
        grid=(depth, n6 // tn),
        in_specs=[pl.BlockSpec((8, d), lambda l, j: (0, 0)),
                  pl.BlockSpec((1, d, tn), lambda l, j: (l, 0, j)),
                  pl.BlockSpec((1, 1, tn), lambda l, j: (l, 0, j))],
        out_specs=pl.BlockSpec((1, 8, tn), lambda l, j: (l, 0, j)),
        out_shape=jax.ShapeDtypeStruct((depth, 8, n6), F32),
        compiler_params=_cparams("parallel", "parallel"),
        name="adaln_modulation",
    )(cvec, w_mod, b_mod.reshape(depth, 1, n6))


def _route(h, rwt, rb, carry):
    tm = h.shape[0]
    h_hi = h.astype(BF16)
    h_lo = (h - h_hi.astype(F32)).astype(BF16)
    by_hi = _dot_nt(rwt, h_hi)
    logits = by_hi[:N_EXPERTS] + by_hi[N_EXPERTS:] + _dot_nt(rwt[:N_EXPERTS], h_lo)
    scores = _sigmoid(logits)
    sel = scores + rb
    expert = lax.broadcasted_iota(jnp.int32, sel.shape, 0)
    pos = expert % GROUP_SIZE
    grp = expert // GROUP_SIZE

    def mate(x, k):
        ahead = pltpu.roll(x, N_EXPERTS - k, axis=0)
        behind = pltpu.roll(x, GROUP_SIZE - k, axis=0)
        return jnp.where(pos + k < GROUP_SIZE, ahead, behind)

    beaten = jnp.zeros_like(sel)
    for k in range(1, GROUP_SIZE):
        other = mate(sel, k)
        other_first = (pos + k) % GROUP_SIZE < pos
        beaten = beaten + jnp.where(other_first, jnp.where(other >= sel, 1.0, 0.0), jnp.where(other > sel, 1.0, 0.0))
    top2 = jnp.where(beaten < 2.0, sel, 0.0)
    gscore = top2
    for k in range(1, GROUP_SIZE):
        gscore = gscore + mate(top2, k)
    lost = jnp.zeros_like(sel)
    for k in range(1, N_GROUPS):
        other = pltpu.roll(gscore, N_EXPERTS - GROUP_SIZE * k, axis=0)
        other_first = (grp + k) % N_GROUPS < grp
        lost = lost + jnp.where(other_first, jnp.where(other >= gscore, 1.0, 0.0),
                                jnp.where(other > gscore, 1.0, 0.0))
    picked = jnp.where(lost < 0.5, jnp.where(beaten < 2.0, 1.0, 0.0), 0.0)
    chosen = picked > 0.5
    weight = jnp.where(chosen, scores, 0.0)
    wsum = jnp.sum(weight, axis=0, keepdims=True)
    e_f = expert.astype(F32)
    e_a = jnp.min(jnp.where(chosen, e_f, float(N_EXPERTS)), axis=0, keepdims=True)
    e_b = jnp.max(jnp.where(chosen, e_f, -1.0), axis=0, keepdims=True)
    before = (lax.broadcasted_iota(jnp.int32, (tm, tm), 0) < lax.broadcasted_iota(jnp.int32, (tm, tm), 1))
    rank = _dot(picked.astype(BF16), jnp.where(before, 1.0, 0.0).astype(BF16)) + carry
    is_a = e_f == e_a
    is_b = e_f == e_b
    r_a = jnp.sum(jnp.where(is_a, rank, 0.0), axis=0, keepdims=True)
    r_b = jnp.sum(jnp.where(is_b, rank, 0.0), axis=0, keepdims=True)
    w_a = jnp.sum(jnp.where(is_a, weight, 0.0), axis=0, keepdims=True)
    w_b = jnp.sum(jnp.where(is_b, weight, 0.0), axis=0, keepdims=True)
    meta = jnp.concatenate([e_a, e_b, r_a, r_b, jnp.zeros((4, tm), F32)], axis=0).astype(jnp.int32)
    wcol = jnp.concatenate([w_a / wsum, w_b / wsum, jnp.zeros((LANES - 2, tm), F32)], axis=0).T
    return meta, wcol, carry + jnp.sum(picked, axis=1, keepdims=True)


def _norm_mod(x, g_ref, mod_refs):
    h = _rms_rows(x, g_ref[...])
    if mod_refs is not None:
        a_ref, s_ref = mod_refs
        h = h * (1.0 + a_ref[0]) + s_ref[0]
    return h


def _rownorm_kernel(x_ref, g_ref, a_ref, s_ref, h_ref):
    h_ref[...] = _norm_mod(x_ref[...], g_ref, (a_ref, s_ref)).astype(h_ref.dtype)


def _mod_spec(n_mod, rows, tm, d):
    per = (rows // n_mod) // tm
    return pl.BlockSpec((1, 1, d), lambda i, *_: (i // per, 0, 0))


def _rownorm(x, g, scale, shift, *, tm=512):
    rows, d = x.shape
    row_spec = pl.BlockSpec((tm, d), lambda i: (i, 0))
    return pl.pallas_call(
        _rownorm_kernel, grid=(rows // tm,),
        in_specs=[row_spec, pl.BlockSpec((1, d), lambda i: (0, 0)), _mod_spec(scale.shape[0], rows, tm, d),
                  _mod_spec(shift.shape[0], rows, tm, d)],
        out_specs=row_spec, out_shape=jax.ShapeDtypeStruct((rows, d), BF16),
        compiler_params=_cparams("parallel"), name="rownorm",
    )(x, g.reshape(1, d), scale, shift)


def _mm_kernel(a_ref, w_ref, o_ref):
    o_ref[...] = _dot(a_ref[...].astype(BF16), w_ref[...]).astype(o_ref.dtype)


def _matmul(a, w, *, out_dtype=F32, tm=1024):
    m, k = a.shape
    n = w.shape[1]
    tn = next(t for t in (1024, 768, 512, LANES) if n % t == 0)
    return pl.pallas_call(
        _mm_kernel, grid=(m // tm, n // tn),
        in_specs=[pl.BlockSpec((tm, k), lambda i, j: (i, 0)), pl.BlockSpec((k, tn), lambda i, j: (0, j))],
        out_specs=pl.BlockSpec((tm, tn), lambda i, j: (i, j)),
        out_shape=jax.ShapeDtypeStruct((m, n), out_dtype),
        compiler_params=_cparams("parallel", "parallel"), name="matmul",
    )(a, w)


def _head_norm(x, g, n_heads, dv):
    outs = []
    for h in range(n_heads):
        xs = x[:, h * dv:(h + 1) * dv]
        ms = jnp.mean(xs * xs, axis=-1, keepdims=True)
        outs.append(xs * lax.rsqrt(ms + EPS) * g[:, h * dv:(h + 1) * dv])
    return jnp.concatenate(outs, axis=1)


def _proj_kernel(*refs, pre):
    refs = list(refs)
    if pre == "plain":
        a = refs.pop(0)[...]
    else:
        f_ref, b_ref, p_ref, hg_ref = refs.pop(0), refs.pop(0), refs.pop(0), refs.pop(0)
        hsum = f_ref[...] + b_ref[...]
        if pre == "mlstm":
            a = _sigmoid(p_ref[...]) * _head_norm(hsum, hg_ref[...], ML_HEADS, ML_DV)
        else:
            a = _head_norm(hsum, hg_ref[...], GLA_HEADS, GLA_DV) * _silu(p_ref[...])
        a = a.astype(BF16)
    w_ref, x_ref, gate_ref, g_ref, a_ref, s_ref, rwt_ref, rb_ref, count0_ref = refs[:9]
    xo_ref, h_ref, meta_ref, wcol_ref, count_ref, carry_ref = refs[9:]

    @pl.when(pl.program_id(0) == 0)
    def _():
        carry_ref[...] = count0_ref[...].astype(F32)

    x = x_ref[...] + gate_ref[0] * _dot(a, w_ref[...])
    xo_ref[...] = x
    h = _norm_mod(x, g_ref, (a_ref, s_ref))
    _rows_to_tiles(h_ref, h)
    meta, wcol, carry = _route(h, rwt_ref[...], rb_ref[...], carry_ref[:, 0:1])
    meta_ref[...] = meta
    wcol_ref[...] = wcol
    carry_ref[...] = jnp.broadcast_to(carry, carry_ref.shape)
    count_ref[...] = jnp.broadcast_to(carry, count_ref.shape).astype(jnp.int32)


def _proj(pre, pre_args, w_out, x, gate, g, scale, shift, rwt, rb, count0, *, tm=512):
    rows, d = x.shape
    k = w_out.shape[0]
    row_spec = pl.BlockSpec((tm, d), lambda i: (i, 0))
    if pre == "plain":
        args, specs = [pre_args[0]], [pl.BlockSpec((tm, k), lambda i: (i, 0))]
    else:
        hf, hb, p, col_block, hg = pre_args
        wide = pl.BlockSpec((tm, k), lambda i: (i, 0))
        args = [hf, hb, p, hg.reshape(1, k)]
        specs = [wide, wide, pl.BlockSpec((tm, k), lambda i: (i, col_block)), pl.BlockSpec((1, k), lambda i: (0, 0))]
    args += [w_out, x, gate, g.reshape(1, d), scale, shift, rwt, rb, count0]
    specs += [pl.BlockSpec((k, d), lambda i: (0, 0)), row_spec, _mod_spec(gate.shape[0], rows, tm, d),
              pl.BlockSpec((1, d), lambda i: (0, 0)), _mod_spec(scale.shape[0], rows, tm, d),
              _mod_spec(shift.shape[0], rows, tm, d), pl.BlockSpec(rwt.shape, lambda i: (0, 0)),
              pl.BlockSpec(rb.shape, lambda i: (0, 0)), pl.BlockSpec(count0.shape, lambda i: (0, 0))]
    return pl.pallas_call(
        functools.partial(_proj_kernel, pre=pre), grid=(rows // tm,), in_specs=specs,
        out_specs=[row_spec, pl.BlockSpec((tm * SUBLANES, LANES), lambda i: (i, 0)),
                   pl.BlockSpec((8, tm), lambda i: (0, i)),
                   pl.BlockSpec((tm, LANES), lambda i: (i, 0)), pl.BlockSpec((N_EXPERTS, LANES), lambda i: (0, 0))],
        out_shape=[jax.ShapeDtypeStruct((rows, d), F32), jax.ShapeDtypeStruct((rows * SUBLANES, LANES), F32),
                   jax.ShapeDtypeStruct((8, rows), jnp.int32), jax.ShapeDtypeStruct((rows, LANES), F32),
                   jax.ShapeDtypeStruct((N_EXPERTS, LANES), jnp.int32)],
        scratch_shapes=[pltpu.VMEM((N_EXPERTS, LANES), F32)],
        compiler_params=_cparams("arbitrary"), name="proj_" + pre,
    )(*args)


MOE_TILE = 512
MOE_TILE_SHIFT = 9
MOE_TOKENS = 512
FFN_PARTS = 2


SUBLANES = 8


def _rows_to_tiles(ref, x, lead=(), first=0):
    rows = x.shape[0]
    for c in range(SUBLANES):
        ref[(*lead, pl.ds(first * SUBLANES + c, rows, stride=SUBLANES), slice(None))] = x[:, c * LANES:(c + 1) * LANES]


def _tiles_to_rows(ref, rows, lead=(), first=0):
    return jnp.concatenate([ref[(*lead, pl.ds(first * SUBLANES + c, rows, stride=SUBLANES), slice(None))]
                            for c in range(SUBLANES)], axis=1)


def _slot_tiles(rows):
    return (2 * rows) // MOE_TILE + N_EXPERTS


def _expert_offsets(cnt_ref, off_ref):
    def per_expert(e, k):
        off_ref[e] = k * MOE_TILE
        return k + ((cnt_ref[e] + MOE_TILE - 1) >> MOE_TILE_SHIFT)
    return lax.fori_loop(0, N_EXPERTS, per_expert, 0)


def _slots_kernel(cnt_ref, meta_ref, slot_ref, off_ref):
    @pl.when(pl.program_id(0) == 0)
    def _():
        _expert_offsets(cnt_ref, off_ref)

    e_a, e_b = meta_ref[0:1, :], meta_ref[1:2, :]
    off_a = jnp.zeros_like(e_a)
    off_b = jnp.zeros_like(e_b)
    for e in range(N_EXPERTS):
        off_a = jnp.where(e_a == e, off_ref[e], off_a)
        off_b = jnp.where(e_b == e, off_ref[e], off_b)
    slot_ref[...] = jnp.concatenate([off_a + meta_ref[2:3, :], off_b + meta_ref[3:4, :],
                                     jnp.zeros((6, e_a.shape[1]), jnp.int32)], axis=0)


def _slots(meta, counts, *, tm=1024):
    rows = meta.shape[1]
    grid_spec = pltpu.PrefetchScalarGridSpec(
        num_scalar_prefetch=1, grid=(rows // tm,),
        in_specs=[pl.BlockSpec((8, tm), lambda i, cnt: (0, i))],
        out_specs=pl.BlockSpec((8, tm), lambda i, cnt: (0, i)),
        scratch_shapes=[pltpu.SMEM((N_EXPERTS,), jnp.int32)])
    return pl.pallas_call(
        _slots_kernel, grid_spec=grid_spec, out_shape=jax.ShapeDtypeStruct((8, rows), jnp.int32),
        compiler_params=_cparams("arbitrary"), name="moe_slots",
    )(counts, meta)


def _dispatch_kernel(*refs, steps):
    sa_ref, sb_ref, cnt_ref = refs[:3]
    h_refs = refs[3:3 + len(steps)]
    xs_ref, info_ref, off_ref, zero_ref, sem = refs[3 + len(steps):]
    i = pl.program_id(0)
    tm = h_refs[0].shape[0] // SUBLANES
    n_tiles = info_ref.shape[0] - 1
    tile_rows = MOE_TILE * SUBLANES

    def tile_copy(tile):
        return pltpu.make_async_copy(zero_ref, xs_ref.at[pl.ds(tile * tile_rows, tile_rows), :], sem)

    @pl.when(i == 0)
    def _():
        zero_ref[...] = jnp.zeros_like(zero_ref)
        used = _expert_offsets(cnt_ref, off_ref)

        def per_expert(e, _):
            first = off_ref[e] >> MOE_TILE_SHIFT
            nt = (cnt_ref[e] + MOE_TILE - 1) >> MOE_TILE_SHIFT

            def fill(j, _):
                info_ref[first + j] = e
                return 0
            lax.fori_loop(0, nt, fill, 0)

            @pl.when(nt > 0)
            def _():
                tile_copy(first + nt - 1).start()
                tile_copy(first + nt - 1).wait()
            return 0
        lax.fori_loop(0, N_EXPERTS, per_expert, 0)
        info_ref[n_tiles] = used

        def tail(j, _):
            info_ref[j] = N_EXPERTS - 1
            tile_copy(j).start()
            tile_copy(j).wait()
            return 0
        lax.fori_loop(used, n_tiles, tail, 0)

    base = i * tm

    def copy_rows(h_ref):
        def row_copy(t, slot):
            dst = pl.multiple_of(slot * SUBLANES, SUBLANES)
            return pltpu.make_async_copy(h_ref.at[pl.ds(t * SUBLANES, SUBLANES), :],
                                         xs_ref.at[pl.ds(dst, SUBLANES), :], sem)

        for t in range(tm):
            row_copy(t, sa_ref[base + t]).start(priority=0)
            row_copy(t, sb_ref[base + t]).start(priority=1)
        for _ in range(2):
            pltpu.make_async_copy(h_ref, xs_ref.at[pl.ds(0, tm * SUBLANES), :], sem).wait()

    first = 0
    for h_ref, n in zip(h_refs, steps):
        pl.when(jnp.logical_and(i >= first, i < first + n))(functools.partial(copy_rows, h_ref))
        first += n


def _dispatch(hs, slots, counts):
    tm = MOE_TOKENS
    steps = tuple(h.shape[0] // (tm * SUBLANES) for h in hs)
    n_tiles = _slot_tiles(sum(steps) * tm)
    specs, first = [], 0
    for n in steps:
        specs.append(pl.BlockSpec((tm * SUBLANES, LANES),
                                  lambda i, *_, first=first, n=n: (jnp.clip(i - first, 0, n - 1), 0)))
        first += n
    grid_spec = pltpu.PrefetchScalarGridSpec(
        num_scalar_prefetch=3, grid=(sum(steps),), in_specs=specs,
        out_specs=[pl.BlockSpec(memory_space=pl.ANY), pl.BlockSpec(memory_space=pltpu.SMEM)],
        scratch_shapes=[pltpu.SMEM((N_EXPERTS,), jnp.int32), pltpu.VMEM((MOE_TILE * SUBLANES, LANES), F32),
                        pltpu.SemaphoreType.DMA(())])
    return pl.pallas_call(
        functools.partial(_dispatch_kernel, steps=steps), grid_spec=grid_spec,
        out_shape=[jax.ShapeDtypeStruct((n_tiles * MOE_TILE * SUBLANES, LANES), F32),
                   jax.ShapeDtypeStruct((n_tiles + 1,), jnp.int32)],
        compiler_params=_cparams("arbitrary"), name="moe_dispatch",
    )(slots[0], slots[1], counts, *hs)


def _ffn_kernel(info_ref, xs_ref, wg_ref, wu_ref, wd_ref, ys_ref, wg_s, wu_s, wd_s):
    i = pl.program_id(0)
    used = info_ref[info_ref.shape[0] - 1]
    fresh = jnp.logical_or(i == 0, info_ref[i] != info_ref[jnp.maximum(i - 1, 0)])

    @pl.when(jnp.logical_and(i < used, fresh))
    def _():
        wg_s[...] = wg_ref[0, 0].astype(BF16)
        wu_s[...] = wu_ref[0, 0].astype(BF16)
        wd_s[...] = wd_ref[0, 0].astype(BF16)

    @pl.when(i < used)
    def _():
        part = MOE_TILE // FFN_PARTS
        x = [_tiles_to_rows(xs_ref, part, first=p * part).astype(BF16) for p in range(FFN_PARTS)]
        gate = [_dot(xp, wg_s[...]) for xp in x]
        up = [_dot(xp, wu_s[...]) for xp in x]
        hid = [(_silu(g) * u).astype(BF16) for g, u in zip(gate, up)]
        y = [_dot(hp, wd_s[...]) for hp in hid]
        for p in range(FFN_PARTS):
            _rows_to_tiles(ys_ref, y[p], first=p * part)

    @pl.when(i >= used)
    def _():
        ys_ref[...] = jnp.zeros_like(ys_ref)


def _ffn(xs, info, wg, wu, wd, layer):
    tile_rows = MOE_TILE * SUBLANES
    n_tiles = xs.shape[0] // tile_rows
    d, f = wg.shape[2:]

    def w_map(i, info):
        return (layer, info[i], 0, 0)

    grid_spec = pltpu.PrefetchScalarGridSpec(
        num_scalar_prefetch=1, grid=(n_tiles,),
        in_specs=[pl.BlockSpec((tile_rows, LANES), lambda i, info: (jnp.minimum(i, info[n_tiles] - 1), 0)),
                  pl.BlockSpec((1, 1, d, f), w_map), pl.BlockSpec((1, 1, d, f), w_map),
                  pl.BlockSpec((1, 1, f, d), w_map)],
        out_specs=pl.BlockSpec((tile_rows, LANES), lambda i, info: (i, 0)),
        scratch_shapes=[pltpu.VMEM((d, f), BF16), pltpu.VMEM((d, f), BF16), pltpu.VMEM((f, d), BF16)])
    return pl.pallas_call(
        _ffn_kernel, grid_spec=grid_spec, out_shape=jax.ShapeDtypeStruct(xs.shape, F32),
        compiler_params=_cparams("arbitrary"), name="moe_ffn",
    )(info, xs, wg, wu, wd)


def _combine_kernel(*refs, has_mod, out_x):
    refs = list(refs)
    sa_ref, sb_ref, x_ref, ys_ref, wcol_ref, gate_ref, g_ref = refs[:7]
    refs = refs[7:]
    mod_refs = (refs.pop(0), refs.pop(0)) if has_mod else None
    xo_ref = refs.pop(0) if out_x else None
    h_ref, buf_a, buf_b, sems = refs
    i = pl.program_id(0)
    tm = x_ref.shape[0]

    def issue(tile, slot):
        base = tile * tm
        for t in range(tm):
            dst = pl.ds(t * SUBLANES, SUBLANES)
            src_a = pl.multiple_of(sa_ref[base + t] * SUBLANES, SUBLANES)
            src_b = pl.multiple_of(sb_ref[base + t] * SUBLANES, SUBLANES)
            pltpu.make_async_copy(ys_ref.at[pl.ds(src_a, SUBLANES), :], buf_a.at[slot, dst, :],
                                  sems.at[slot]).start(priority=0)
            pltpu.make_async_copy(ys_ref.at[pl.ds(src_b, SUBLANES), :], buf_b.at[slot, dst, :],
                                  sems.at[slot]).start(priority=1)

    @pl.when(i == 0)
    def _():
        issue(0, 0)

    @pl.when(i + 1 < pl.num_programs(0))
    def _():
        issue(i + 1, (i + 1) % 2)

    slot = i % 2
    for buf in (buf_a, buf_b):
        pltpu.make_async_copy(ys_ref.at[pl.ds(0, tm * SUBLANES), :], buf.at[slot], sems.at[slot]).wait()
    y = (wcol_ref[:, 0:1] * _tiles_to_rows(buf_a, tm, lead=(slot,))
         + wcol_ref[:, 1:2] * _tiles_to_rows(buf_b, tm, lead=(slot,)))
    x = x_ref[...] + gate_ref[0] * y
    if out_x:
        xo_ref[...] = x
    h_ref[...] = _norm_mod(x, g_ref, mod_refs).astype(h_ref.dtype)


def _combine(x, ys, slots, wcol, gate, g, *, mod=None, out_x=False, h_dtype=BF16):
    rows, d = x.shape
    tm = MOE_TOKENS
    row_spec = pl.BlockSpec((tm, d), lambda i, *_: (i, 0))
    args = [x, ys, wcol, gate, g.reshape(1, d)]
    specs = [row_spec, pl.BlockSpec(memory_space=pl.ANY), pl.BlockSpec((tm, LANES), lambda i, *_: (i, 0)),
             _mod_spec(gate.shape[0], rows, tm, d), pl.BlockSpec((1, d), lambda i, *_: (0, 0))]
    if mod is not None:
        for m in mod:
            args.append(m)
            specs.append(_mod_spec(m.shape[0], rows, tm, d))
    out_shape, out_specs = [], []
    if out_x:
        out_shape.append(jax.ShapeDtypeStruct((rows, d), F32))
        out_specs.append(row_spec)
    out_shape.append(jax.ShapeDtypeStruct((rows, d), h_dtype))
    out_specs.append(row_spec)
    grid_spec = pltpu.PrefetchScalarGridSpec(
        num_scalar_prefetch=2, grid=(rows // tm,), in_specs=specs, out_specs=out_specs,
        scratch_shapes=[pltpu.VMEM((2, tm * SUBLANES, LANES), F32), pltpu.VMEM((2, tm * SUBLANES, LANES), F32),
                        pltpu.SemaphoreType.DMA((2,))])
    outs = pl.pallas_call(
        functools.partial(_combine_kernel, has_mod=mod is not None, out_x=out_x), grid_spec=grid_spec,
        out_shape=out_shape, compiler_params=_cparams("arbitrary"), name="moe_combine",
    )(slots[0], slots[1], *args)
    return outs if out_x else outs[0]


def _softmax_av(groups):
    maxes = []
    for scores, _, sink in groups:
        m = sink
        for s in scores:
            m = jnp.maximum(m, jnp.broadcast_to(jnp.max(s, axis=-1, keepdims=True), sink.shape))
        maxes.append(m)
    probs = [[jnp.exp(s - jnp.concatenate([m] * (s.shape[1] // LANES), axis=1)).astype(BF16) for s in scores]
             for (scores, _, _), m in zip(groups, maxes)]
    outs = []
    for (_, values, sink), m, ps in zip(groups, maxes, probs):
        hd = values[0].shape[1]
        tot = None
        for p, v in zip(ps, values):
            n = v.shape[0]
            v_ext = jnp.concatenate([v, jnp.zeros((n, LANES - hd), BF16), jnp.ones((n, LANES), BF16)], axis=1)
            pv = _dot(p, v_ext)
            tot = pv if tot is None else tot + pv
        outs.append((tot, sink, m, hd))
    return [tot[:, :hd] / (tot[:, LANES:] + jnp.exp(sink - m))[:, :hd] for tot, sink, m, hd in outs]


def _sink_column(sink_ref, kv, rows):
    return jnp.concatenate([jnp.full((rows, LANES), sink_ref[kv * ATT_GROUP + g], F32) for g in range(ATT_GROUP)],
                           axis=0)


def _attn_ctx_kernel(sink_ref, qkv_ref, o_ref):
    t = qkv_ref.shape[0]
    qw = ATT_HEADS * HEAD_DIM
    kw = ATT_KV * HEAD_DIM
    groups = []
    for kv in range(ATT_KV):
        q = jnp.concatenate(
            [qkv_ref[:, (kv * ATT_GROUP + g) * HEAD_DIM:(kv * ATT_GROUP + g + 1) * HEAD_DIM] for g in range(ATT_GROUP)],
            axis=0).astype(BF16)
        k = qkv_ref[:, qw + kv * HEAD_DIM:qw + (kv + 1) * HEAD_DIM].astype(BF16)
        v = qkv_ref[:, qw + kw + kv * HEAD_DIM:qw + kw + (kv + 1) * HEAD_DIM].astype(BF16)
        groups.append(([_dot_nt(q, k) * HEAD_DIM ** -0.5], [v], _sink_column(sink_ref, kv, t)))
    heads_out = [o[g * t:(g + 1) * t] for o in _softmax_av(groups) for g in range(ATT_GROUP)]
    o_ref[...] = jnp.concatenate(heads_out, axis=1).astype(o_ref.dtype)


def _attn_ctx(qkv, sink, n_seq, seq_len):
    rows, cols = qkv.shape
    return pl.pallas_call(
        _attn_ctx_kernel, grid=(n_seq,),
        in_specs=[pl.BlockSpec(memory_space=pltpu.SMEM), pl.BlockSpec((seq_len, cols), lambda b: (b, 0))],
        out_specs=pl.BlockSpec((seq_len, ATT_HEADS * HEAD_DIM), lambda b: (b, 0)),
        out_shape=jax.ShapeDtypeStruct((rows, ATT_HEADS * HEAD_DIM), BF16),
        compiler_params=_cparams("parallel"), name="attn_context",
    )(sink, qkv)


def _rope_block(x, cos, sin_signed):
    lane = lax.broadcasted_iota(jnp.int32, x.shape, 1)
    nf = HEAD_DIM // 4
    partner = jnp.where((lane % (2 * nf)) < nf, pltpu.roll(x, LANES - nf, axis=1), pltpu.roll(x, nf, axis=1))
    return x * cos + partner * sin_signed


def _attn_lat_kernel(sink_ref, qkv_ref, ck_ref, cv_ref, cos_ref, sin_ref, o_ref, k_scr):
    i = pl.program_id(1)
    t = qkv_ref.shape[0]
    qw = ATT_HEADS * HEAD_DIM
    kw = ATT_KV * HEAD_DIM
    span = Q_BLOCK + 2 * WINDOW

    @pl.when(i == 0)
    def _():
        for c in range(kw // LANES):
            blk = qkv_ref[:, qw + c * LANES:qw + (c + 1) * LANES]
            k_scr[:, c * LANES:(c + 1) * LANES] = _rope_block(blk, cos_ref[...], sin_ref[...]).astype(BF16)

    r0 = pl.multiple_of(i * Q_BLOCK, Q_BLOCK)
    ws = pl.multiple_of(jnp.clip(r0 - WINDOW, 0, t - span), Q_BLOCK)
    cos_q = cos_ref[pl.ds(r0, Q_BLOCK), :]
    sin_q = sin_ref[pl.ds(r0, Q_BLOCK), :]
    qpos = r0 + lax.broadcasted_iota(jnp.int32, (Q_BLOCK, span), 0)
    kpos = ws + lax.broadcasted_iota(jnp.int32, (Q_BLOCK, span), 1)
    band = jnp.abs(qpos - kpos) <= WINDOW
    band = jnp.concatenate([band] * ATT_GROUP, axis=0)
    roped = [_rope_block(qkv_ref[pl.ds(r0, Q_BLOCK), c * LANES:(c + 1) * LANES], cos_q, sin_q)
             for c in range(qw // LANES)]
    groups = []
    for kv in range(ATT_KV):
        heads = []
        for g in range(ATT_GROUP):
            c, half = divmod((kv * ATT_GROUP + g) * HEAD_DIM, LANES)
            heads.append(roped[c][:, half:half + HEAD_DIM])
        q = jnp.concatenate(heads, axis=0).astype(BF16)
        ck = ck_ref[0, :, kv * HEAD_DIM:(kv + 1) * HEAD_DIM].astype(BF16)
        cv = cv_ref[0, :, kv * HEAD_DIM:(kv + 1) * HEAD_DIM].astype(BF16)
        kwin = k_scr[pl.ds(ws, span), kv * HEAD_DIM:(kv + 1) * HEAD_DIM]
        vwin = qkv_ref[pl.ds(ws, span), qw + kw + kv * HEAD_DIM:qw + kw + (kv + 1) * HEAD_DIM].astype(BF16)
        s_ctx = _dot_nt(q, ck) * HEAD_DIM ** -0.5
        s_win = jnp.where(band, _dot_nt(q, kwin) * HEAD_DIM ** -0.5, -jnp.inf)
        groups.append(([s_ctx, s_win], [cv, vwin], _sink_column(sink_ref, kv, Q_BLOCK)))
    heads_out = [o[g * Q_BLOCK:(g + 1) * Q_BLOCK] for o in _softmax_av(groups) for g in range(ATT_GROUP)]
    o_ref[...] = jnp.concatenate(heads_out, axis=1).astype(o_ref.dtype)


def _rope_tables(seq_len):
    pos = jnp.arange(seq_len, dtype=jnp.int32)
    row = (pos // GRID_W).astype(F32)
    col = (pos % GRID_W).astype(F32)
    nf = HEAD_DIM // 4
    inv = ROPE_BASE ** (-jnp.arange(nf, dtype=F32) / nf)
    ang_r = row[:, None] * inv[None, :]
    ang_c = col[:, None] * inv[None, :]
    cos_h = jnp.concatenate([jnp.cos(ang_r), jnp.cos(ang_r), jnp.cos(ang_c), jnp.cos(ang_c)], axis=1)
    sin_h = jnp.concatenate([-jnp.sin(ang_r), jnp.sin(ang_r), -jnp.sin(ang_c), jnp.sin(ang_c)], axis=1)
    reps = LANES // HEAD_DIM
    return jnp.tile(cos_h, (1, reps)), jnp.tile(sin_h, (1, reps))


def _attn_lat(qkv, cache_k, cache_v, sink, n_seq, seq_len):
    rows, cols = qkv.shape
    past = cache_k.shape[1]
    kw = ATT_KV * HEAD_DIM
    cos, sin = _rope_tables(seq_len)
    return pl.pallas_call(
        _attn_lat_kernel, grid=(n_seq, seq_len // Q_BLOCK),
        in_specs=[pl.BlockSpec(memory_space=pltpu.SMEM),
                  pl.BlockSpec((seq_len, cols), lambda b, i: (b, 0)),
                  pl.BlockSpec((1, past, kw), lambda b, i: (b, 0, 0)),
                  pl.BlockSpec((1, past, kw), lambda b, i: (b, 0, 0)),
                  pl.BlockSpec((seq_len, LANES), lambda b, i: (0, 0)),
                  pl.BlockSpec((seq_len, LANES), lambda b, i: (0, 0))],
        out_specs=pl.BlockSpec((Q_BLOCK, ATT_HEADS * HEAD_DIM), lambda b, i: (b * (seq_len // Q_BLOCK) + i, 0)),
        out_shape=jax.ShapeDtypeStruct((rows, ATT_HEADS * HEAD_DIM), BF16),
        scratch_shapes=[pltpu.VMEM((seq_len, kw), BF16)],
        compiler_params=_cparams("parallel", "arbitrary"), name="attn_latent",
    )(sink, qkv, cache_k.reshape(n_seq, past, kw), cache_v.reshape(n_seq, past, kw), cos, sin)


def _split_bf16(w):
    hi = w.astype(BF16)
    return hi, (w - hi.astype(F32)).astype(BF16)


def _select_lanes(x, sel):
    x1 = x.astype(BF16)
    rest = x - x1.astype(F32)
    x2 = rest.astype(BF16)
    x3 = (rest - x2.astype(F32)).astype(BF16)
    return _dot(x1, sel) + _dot(x2, sel) + _dot(x3, sel)


def _ml_gates_kernel(h_ref, w_ref, wt_ref, b_ref, bt_ref, g_ref, gt_ref):
    h = h_ref[...]
    h_hi = h.astype(BF16)
    h_lo = (h - h_hi.astype(F32)).astype(BF16)
    ng = gt_ref.shape[0]
    by_hi = _dot(h_hi, w_ref[...])
    g_ref[...] = by_hi[:, :LANES] + by_hi[:, LANES:] + _dot(h_lo, w_ref[:, :LANES]) + b_ref[...]
    by_hi_t = _dot_nt(wt_ref[...], h_hi)
    gt_ref[...] = by_hi_t[:ng] + by_hi_t[ng:] + _dot_nt(wt_ref[:ng, :], h_lo) + bt_ref[...]


def _ml_gates(h, w_gates, b_gates, *, tm=512):
    rows, d = h.shape
    ng = w_gates.shape[1]
    w_hi, w_lo = _split_bf16(jnp.pad(w_gates, ((0, 0), (0, LANES - ng))))
    wt_hi, wt_lo = _split_bf16(w_gates.T)
    b_pad = jnp.pad(b_gates, (0, LANES - ng)).reshape(1, LANES)
    return pl.pallas_call(
        _ml_gates_kernel, grid=(rows // tm,),
        in_specs=[pl.BlockSpec((tm, d), lambda i: (i, 0)), pl.BlockSpec((d, 2 * LANES), lambda i: (0, 0)),
                  pl.BlockSpec((2 * ng, d), lambda i: (0, 0)), pl.BlockSpec((1, LANES), lambda i: (0, 0)),
                  pl.BlockSpec((ng, 1), lambda i: (0, 0))],
        out_specs=[pl.BlockSpec((tm, LANES), lambda i: (i, 0)), pl.BlockSpec((ng, tm), lambda i: (0, i))],
        out_shape=[jax.ShapeDtypeStruct((rows, LANES), F32), jax.ShapeDtypeStruct((ng, rows), F32)],
        compiler_params=_cparams("parallel"), name="mlstm_gates",
    )(h, jnp.concatenate([w_hi, w_lo], axis=1), jnp.concatenate([wt_hi, wt_lo], axis=0), b_pad,
      b_gates.reshape(ng, 1))


def _ml_qk_kernel(h_ref, w_ref, cw_ref, o_ref, *, seq_len, k_scale):
    j = pl.program_id(1)
    x = _dot(h_ref[...].astype(BF16), w_ref[...])
    t = x.shape[0]
    pos = lax.broadcasted_iota(jnp.int32, x.shape, 0) % seq_len
    prev = jnp.where(pos == 0, 0.0, pltpu.roll(x, 1, axis=0))
    nxt = jnp.where(pos == seq_len - 1, 0.0, pltpu.roll(x, t - 1, axis=0))
    y = prev * cw_ref[0:1, :] + x * cw_ref[1:2, :] + nxt * cw_ref[2:3, :]
    scale = jnp.where(j >= pl.num_programs(1) // 2, k_scale, 1.0).astype(F32)
    o_ref[...] = (_silu(y) * scale).astype(o_ref.dtype)


def _ml_qk(h, w_qk, conv_w, seq_len, *, tm=1024, tn=1024):
    rows, d = h.shape
    width = w_qk.shape[1]
    return pl.pallas_call(
        functools.partial(_ml_qk_kernel, seq_len=seq_len, k_scale=ML_DK ** -0.5), grid=(rows // tm, width // tn),
        in_specs=[pl.BlockSpec((tm, d), lambda i, j: (i, 0)), pl.BlockSpec((d, tn), lambda i, j: (0, j)),
                  pl.BlockSpec((3, tn), lambda i, j: (0, j))],
        out_specs=pl.BlockSpec((tm, tn), lambda i, j: (i, j)),
        out_shape=jax.ShapeDtypeStruct((rows, width), BF16),
        compiler_params=_cparams("parallel", "parallel"), name="mlstm_qk",
    )(h, w_qk, conv_w)


def _ml_scan_kernel(*refs, zero_init):
    refs = list(refs)
    dirs = [tuple(refs[0:5]), tuple(refs[5:10])]
    sel_ref = refs[10]
    refs = refs[11:]
    if not zero_init:
        c0_ref, n0_ref, m0_ref = refs[:3]
        refs = refs[3:]
    hf_ref, hb_ref, c_ref, n_ref, m_ref = refs
    h_out = (hf_ref, hb_ref)
    c = pl.program_id(1)
    last = pl.num_programs(1) - 1

    @pl.when(c == 0)
    def _():
        if zero_init:
            c_ref[...] = jnp.zeros_like(c_ref)
            n_ref[...] = jnp.zeros_like(n_ref)
            m_ref[...] = jnp.zeros_like(m_ref)
        else:
            c_ref[...] = c0_ref[...]
            n_ref[...] = n0_ref[...]
            m_ref[...] = m0_ref[...]

    length = hf_ref.shape[0]
    ti = lax.broadcasted_iota(jnp.int32, (length, length), 0)
    si = lax.broadcasted_iota(jnp.int32, (length, length), 1)
    causal = [ti >= si, ti <= si]
    edge = [length - 1, 0]
    b_row, i_rep, b_rep = [], [], []
    for d in range(2):
        g_ref, gt_ref = dirs[d][3], dirs[d][4]
        tri = jnp.where(causal[d], 1.0, 0.0).astype(F32)
        g_col = g_ref[...]
        f_row = _log_sigmoid(gt_ref[...])
        b_col = jnp.dot(tri, _log_sigmoid(g_col), precision=HI, preferred_element_type=F32)
        b_row.append(_dot_nt(f_row, tri, precision=HI))
        i_rep.append(_select_lanes(g_col, sel_ref[d, 0]))
        b_rep.append(_select_lanes(b_col, sel_ref[d, 1]))
    chains = [(d, h) for d in range(2) for h in range(ML_HEADS)]
    q = {(d, h): dirs[d][0][:, h * ML_DK:(h + 1) * ML_DK] for d, h in chains}
    k = {(d, h): dirs[d][1][:, h * ML_DK:(h + 1) * ML_DK] for d, h in chains}
    v = {(d, h): dirs[d][2][:, h * ML_DV:(h + 1) * ML_DV].astype(BF16) for d, h in chains}
    cst = {c: c_ref[0, c[0], c[1]] for c in chains}
    nst = {c: n_ref[0, c[0], c[1]] for c in chains}
    m_rep = {c: m_ref[0, c[0], c[1]] for c in chains}
    bc = {(d, h): b_rep[d][:, h * LANES:(h + 1) * LANES] for d, h in chains}
    i_col = {(d, h): i_rep[d][:, h * LANES:(h + 1) * LANES] for d, h in chains}
    qk_raw = {c: _dot_nt(q[c], k[c]) for c in chains}
    inter = {c: _dot_nt(q[c], jnp.concatenate([cst[c], jnp.broadcast_to(nst[c], (LANES, ML_DK))],
                                              axis=0).astype(BF16)) for c in chains}
    a_row = {}
    for d, h in chains:
        ji = d * 2 * ML_HEADS + h
        a_row[d, h] = dirs[d][4][ji:ji + 1, :] - b_row[d][ji + ML_HEADS:ji + ML_HEADS + 1, :]
    amat = {c: jnp.where(causal[c[0]], a_row[c], -jnp.inf) for c in chains}
    u = {c: jnp.maximum(m_rep[c], jnp.broadcast_to(jnp.max(amat[c], axis=1, keepdims=True), (length, LANES)))
         for c in chains}
    qk = {c: (qk_raw[c] * jnp.exp(amat[c] - u[c])).astype(BF16) for c in chains}
    intra = {c: _dot(qk[c], jnp.concatenate([v[c], jnp.ones((length, LANES), BF16)], axis=1)) for c in chains}
    b_last = {c: bc[c][edge[c[0]]:edge[c[0]] + 1, :] for c in chains}
    wlog_row = {c: b_last[c] + a_row[c] for c in chains}
    m_new = {c: jnp.maximum(b_last[c] + m_rep[c], jnp.max(wlog_row[c], axis=1, keepdims=True)) for c in chains}
    ws_row = {c: jnp.exp(wlog_row[c] - m_new[c]) for c in chains}
    kw = {c: (jnp.exp(b_last[c] - bc[c] + i_col[c] - m_new[c]) * k[c].astype(F32)).astype(BF16) for c in chains}
    c_upd = {c: _dot_tn(v[c], kw[c]) for c in chains}
    n_upd = {c: _dot(jnp.broadcast_to(ws_row[c], (8, length)).astype(BF16), k[c])[0:1] for c in chains}
    for c in chains:
        d, h = c
        sc = jnp.exp(m_rep[c] - u[c])
        tot = jnp.concatenate([sc] * (ML_DV // LANES + 1), axis=1) * inter[c] + intra[c]
        inv = 1.0 / jnp.maximum(jnp.abs(tot[:, ML_DV:]), jnp.exp(-(bc[c] + u[c])))
        h_out[d][:, h * ML_DV:(h + 1) * ML_DV] = tot[:, :ML_DV] * jnp.concatenate([inv] * (ML_DV // LANES), axis=1)
        decay = jnp.exp(b_last[c] + m_rep[c] - m_new[c])
        c_ref[0, d, h] = decay * cst[c] + c_upd[c]
        n_ref[0, d, h] = decay * nst[c] + n_upd[c]
        m_ref[0, d, h] = m_new[c]


def _ml_scan(qk, p, g, gt, state, n_seq, seq_len):
    rows = qk.shape[0]
    length = min(ML_CHUNK, seq_len)
    nc = seq_len // length
    qw = ML_HEADS * ML_DK
    vw = ML_HEADS * ML_DV
    ng = gt.shape[0]

    def fwd(b, c):
        return b * nc + c

    def bwd(b, c):
        return b * nc + nc - 1 - c

    args, specs = [], []
    for pos in (fwd, bwd):
        args += [qk, qk, p, g, gt]
        specs += [pl.BlockSpec((length, qw), lambda b, c, pos=pos: (pos(b, c), 0)),
                  pl.BlockSpec((length, qw), lambda b, c, pos=pos: (pos(b, c), 1)),
                  pl.BlockSpec((length, vw), lambda b, c, pos=pos: (pos(b, c), 0)),
                  pl.BlockSpec((length, LANES), lambda b, c, pos=pos: (pos(b, c), 0)),
                  pl.BlockSpec((ng, length), lambda b, c, pos=pos: (0, pos(b, c)))]
    assert length == LANES and ML_DK == LANES
    gate_lane = jnp.arange(LANES)[:, None]
    head = (jnp.arange(ML_HEADS * LANES) // LANES)[None, :]
    sel = jnp.stack([jnp.stack([gate_lane == (2 * d + kind) * ML_HEADS + head for kind in range(2)])
                     for d in range(2)]).astype(BF16)
    args.append(sel)
    specs.append(pl.BlockSpec(sel.shape, lambda b, c: (0, 0, 0, 0)))
    c_spec = pl.BlockSpec((1, 2, ML_HEADS, ML_DV, ML_DK), lambda b, c: (b, 0, 0, 0, 0))
    n_spec = pl.BlockSpec((1, 2, ML_HEADS, 1, ML_DK), lambda b, c: (b, 0, 0, 0, 0))
    zero_init = state is None
    if not zero_init:
        c0, n0, m0 = state
        args += [c0, n0.reshape(n_seq, 2, ML_HEADS, 1, ML_DK),
                 jnp.broadcast_to(m0[..., None, None], (n_seq, 2, ML_HEADS, 1, ML_DK))]
        specs += [c_spec, n_spec, n_spec]
    hf, hb, c_fin, n_fin, m_fin = pl.pallas_call(
        functools.partial(_ml_scan_kernel, zero_init=zero_init), grid=(n_seq, nc), in_specs=specs,
        out_specs=[pl.BlockSpec((length, vw), lambda b, c: (fwd(b, c), 0)),
                   pl.BlockSpec((length, vw), lambda b, c: (bwd(b, c), 0)), c_spec, n_spec, n_spec],
        out_shape=[jax.ShapeDtypeStruct((rows, vw), F32), jax.ShapeDtypeStruct((rows, vw), F32),
                   jax.ShapeDtypeStruct((n_seq, 2, ML_HEADS, ML_DV, ML_DK), F32),
                   jax.ShapeDtypeStruct((n_seq, 2, ML_HEADS, 1, ML_DK), F32),
                   jax.ShapeDtypeStruct((n_seq, 2, ML_HEADS, 1, ML_DK), F32)],
        compiler_params=_cparams("parallel", "arbitrary"), name="mlstm_scan",
    )(*args)
    return hf, hb, (c_fin, n_fin[:, :, :, 0, :], m_fin[:, :, :, 0, 0])


def _gla_scan_kernel(*refs, zero_init):
    refs = list(refs)
    dirs = [tuple(refs[0:4]), tuple(refs[4:8])]
    w2_ref, ba_ref = refs[8:10]
    refs = refs[10:]
    if not zero_init:
        s0_ref = refs.pop(0)
    of_ref, ob_ref, s_ref, st_scr, la_scr = refs
    o_out = (of_ref, ob_ref)
    c = pl.program_id(1)
    last = pl.num_programs(1) - 1
    kw = GLA_HEADS * GLA_DK
    n_sub = of_ref.shape[0] // GLA_SUB

    @pl.when(c == 0)
    def _():
        for d in range(2):
            for h in range(GLA_HEADS):
                st_scr[d, h] = jnp.zeros((GLA_DV, GLA_DK), F32) if zero_init else s0_ref[0, d, h].T

    for d in range(2):
        u = dirs[d][3][...].astype(BF16)
        z = _dot(u, w2_ref[:, d * kw:(d + 1) * kw]) + ba_ref[:, d * kw:(d + 1) * kw]
        la = _log_sigmoid(z) / GLA_TAU
        la_scr[d] = la
        totals = jnp.sum(la.reshape(n_sub, GLA_SUB, kw), axis=1)
        worst = jnp.min(totals) if d == 0 else jnp.minimum(worst, jnp.min(totals))
    decay_bounded = worst * LOG2E > -GLA_MAX_EXP2

    ti = lax.broadcasted_iota(jnp.int32, (GLA_SUB, GLA_SUB), 0)
    si = lax.broadcasted_iota(jnp.int32, (GLA_SUB, GLA_SUB), 1)
    s_lane = lax.broadcasted_iota(jnp.int32, (GLA_SUB, GLA_SUB), 1)

    def sub_chunk(j, carry, bounded):
        steps = [(d, t) for d in range(2) for t in range(GLA_STEPS)]
        chains = [(d, h, t) for d, t in steps for h in range(GLA_HEADS)]
        rows = {(d, t): pl.ds(pl.multiple_of(
            ((j * GLA_STEPS + t) if d == 0 else n_sub - 1 - (j * GLA_STEPS + t)) * GLA_SUB, GLA_SUB), GLA_SUB)
            for d, t in steps}
        causal = [ti >= si, ti <= si]
        edge = [GLA_SUB - 1, 0]
        bc_all = {(d, t): jnp.dot(jnp.where(causal[d], 1.0, 0.0).astype(F32), la_scr[d, rows[d, t], :], precision=HI,
                                  preferred_element_type=F32) * LOG2E for d, t in steps}
        q, k, v, bc2, b_last, k_dec, q_dec, a = {}, {}, {}, {}, {}, {}, {}, {}
        for c in chains:
            d, h, t = c
            q_ref, k_ref, v_ref, _ = dirs[d]
            bc2[c] = bc_all[d, t][:, h * GLA_DK:(h + 1) * GLA_DK]
            q[c] = q_ref[rows[d, t], h * GLA_DK:(h + 1) * GLA_DK] * GLA_DK ** -0.5
            k[c] = k_ref[rows[d, t], h * GLA_DK:(h + 1) * GLA_DK]
            v[c] = v_ref[rows[d, t], h * GLA_DV:(h + 1) * GLA_DV].astype(BF16)
            b_last[c] = bc2[c][edge[d]:edge[d] + 1, :]
            k_dec[c] = (k[c] * jnp.exp2(b_last[c] - bc2[c])).astype(BF16)
            q_dec[c] = q[c] * jnp.exp2(bc2[c])
        for c in chains:
            if bounded:
                a[c] = _dot_nt((q_dec[c] * jnp.exp2(-b_last[c])).astype(BF16), k_dec[c])
            else:
                acc = jnp.zeros((GLA_SUB, GLA_SUB), F32)
                for s in range(GLA_SUB):
                    decay = jnp.exp2(bc2[c] - bc2[c][s:s + 1, :])
                    col = jnp.sum(q[c] * (k[c][s:s + 1, :] * decay), axis=1, keepdims=True)
                    acc = jnp.where(s_lane == s, col, acc)
                a[c] = acc
        kv = {c: _dot_tn(v[c], k_dec[c]) for c in chains}
        intra = {c: _dot(jnp.where(causal[c[0]], a[c], 0.0).astype(BF16), v[c]) for c in chains}
        state = {(d, h): st_scr[d, h] for d in range(2) for h in range(GLA_HEADS)}
        for t in range(GLA_STEPS):
            now = [c for c in chains if c[2] == t]
            inter = {c: _dot_nt(q_dec[c].astype(BF16), state[c[0], c[1]].astype(BF16)) for c in now}
            for c in now:
                d, h, _ = c
                o_out[d][rows[d, t], h * GLA_DV:(h + 1) * GLA_DV] = intra[c] + inter[c]
                state[d, h] = jnp.exp2(b_last[c]) * state[d, h] + kv[c]
        for (d, h), st in state.items():
            st_scr[d, h] = st
        return carry

    @pl.when(decay_bounded)
    def _():
        lax.fori_loop(0, n_sub // GLA_STEPS, functools.partial(sub_chunk, bounded=True), 0)

    @pl.when(jnp.logical_not(decay_bounded))
    def _():
        lax.fori_loop(0, n_sub // GLA_STEPS, functools.partial(sub_chunk, bounded=False), 0)

    @pl.when(c == last)
    def _():
        for d in range(2):
            for h in range(GLA_HEADS):
                s_ref[0, d, h] = st_scr[d, h].T


def _gla_scan(p, u, w2, b_a, state, n_seq, seq_len):
    rows = p.shape[0]
    length = min(GLA_BLOCK, seq_len)
    nc = seq_len // length
    kw = GLA_HEADS * GLA_DK
    vw = GLA_HEADS * GLA_DV

    def fwd(b, c):
        return b * nc + c

    def bwd(b, c):
        return b * nc + nc - 1 - c

    args, specs = [], []
    for pos in (fwd, bwd):
        args += [p, p, p, u]
        specs += [pl.BlockSpec((length, kw), lambda b, c, pos=pos: (pos(b, c), 0)),
                  pl.BlockSpec((length, kw), lambda b, c, pos=pos: (pos(b, c), 1)),
                  pl.BlockSpec((length, vw), lambda b, c, pos=pos: (pos(b, c), 2 * kw // vw)),
                  pl.BlockSpec((length, LANES), lambda b, c, pos=pos: (pos(b, c), 0))]
    args += [w2, b_a]
    specs += [pl.BlockSpec(w2.shape, lambda b, c: (0, 0)), pl.BlockSpec(b_a.shape, lambda b, c: (0, 0))]
    s_spec = pl.BlockSpec((1, 2, GLA_HEADS, GLA_DK, GLA_DV), lambda b, c: (b, 0, 0, 0, 0))
    zero_init = state is None
    if not zero_init:
        args.append(state)
        specs.append(s_spec)
    return pl.pallas_call(
        functools.partial(_gla_scan_kernel, zero_init=zero_init), grid=(n_seq, nc), in_specs=specs,
        out_specs=[pl.BlockSpec((length, vw), lambda b, c: (fwd(b, c), 0)),
                   pl.BlockSpec((length, vw), lambda b, c: (bwd(b, c), 0)), s_spec],
        out_shape=[jax.ShapeDtypeStruct((rows, vw), F32), jax.ShapeDtypeStruct((rows, vw), F32),
                   jax.ShapeDtypeStruct((n_seq, 2, GLA_HEADS, GLA_DK, GLA_DV), F32)],
        scratch_shapes=[pltpu.VMEM((2, GLA_HEADS, GLA_DV, GLA_DK), F32), pltpu.VMEM((2, length, kw), F32)],
        compiler_params=_cparams("parallel", "arbitrary"), name="gla_scan",
    )(*args)


def kernel(x_prompt, x_sample, cache_k_0, cache_v_0, state_mlstm_C_1, state_mlstm_n_1, state_mlstm_m_1, state_gla_S_2, cache_k_3, cache_v_3, c, c_ctx, w_mod, b_mod, norm1_g, norm2_g, final_g, router_w, router_b, moe_wg, moe_wu, moe_wd, attn0_w_qkv, attn0_sink, attn0_w_o, mlstm1_w_in, mlstm1_b_gates, mlstm1_conv, mlstm1_norm_g, mlstm1_w_out, gla2_w_in, gla2_w_a1, gla2_w_a2, gla2_b_a, gla2_norm_g, gla2_w_out, attn3_w_qkv, attn3_sink, attn3_w_o):
    n_ctx, ctx_len, d = x_prompt.shape
    n_lat, lat_len, _ = x_sample.shape
    depth = w_mod.shape[0]

    cvec = jnp.concatenate([c_ctx[None, :], c, jnp.zeros((8 - 1 - n_lat, d), F32)], axis=0)
    mod = _modulation(cvec, w_mod, b_mod).reshape(depth, 8, 6, 1, d)

    def mods(layer, kind, latent):
        return mod[layer, 1:1 + n_lat, kind] if latent else mod[layer, 0:1, kind]

    rw_hi = router_w.T.astype(BF16)
    rw_lo = (router_w.T - rw_hi.astype(F32)).astype(BF16)
    rwt = jnp.concatenate([rw_hi, rw_lo], axis=0)
    rb = router_b.reshape(-1, 1)
    attn_w = {0: (attn0_w_qkv.astype(BF16), attn0_sink, attn0_w_o.astype(BF16), cache_k_0, cache_v_0),
              3: (attn3_w_qkv.astype(BF16), attn3_sink, attn3_w_o.astype(BF16), cache_k_3, cache_v_3)}
    ml_qw = ML_HEADS * ML_DK
    ml_vw = ML_HEADS * ML_DV
    ml_main = 2 * ml_qw + 2 * ml_vw
    ml_w_qk = mlstm1_w_in[:, :2 * ml_qw].astype(BF16)
    ml_w_vo = mlstm1_w_in[:, 2 * ml_qw:ml_main].astype(BF16)
    ml_w_gates = mlstm1_w_in[:, ml_main:]
    ml_w_out = mlstm1_w_out.astype(BF16)
    gla_kw = GLA_HEADS * GLA_DK
    gla_w_in = gla2_w_in.astype(BF16)
    gla_w_a1 = jnp.pad(jnp.concatenate([gla2_w_a1[0], gla2_w_a1[1]], axis=1),
                       ((0, 0), (0, LANES - 2 * GLA_RANK))).astype(BF16)
    gla_w2 = jnp.zeros((LANES, 2 * gla_kw), F32)
    gla_w2 = gla_w2.at[:GLA_RANK, :gla_kw].set(gla2_w_a2[0]).at[GLA_RANK:2 * GLA_RANK, gla_kw:].set(gla2_w_a2[1])
    gla_w2 = gla_w2.astype(BF16)
    gla_ba = gla2_b_a.reshape(1, 2 * gla_kw)
    gla_w_out = gla2_w_out.astype(BF16)

    new_state = []

    def mixer(layer, s, count0):
        latent, n_seq, seq_len, x, h = s["latent"], s["n_seq"], s["seq_len"], s["x"], s["h"]
        tail = (mods(layer, 2, latent), norm2_g[layer], mods(layer, 4, latent), mods(layer, 3, latent), rwt, rb,
                count0)
        kind = layer % 3
        if kind == 0:
            w_qkv, sink, w_o, ck, cv = attn_w[layer]
            qkv = _matmul(h, w_qkv)
            if latent:
                att = _attn_lat(qkv, ck, cv, sink, n_seq, seq_len)
            else:
                att = _attn_ctx(qkv, sink, n_seq, seq_len)
                qw = ATT_HEADS * HEAD_DIM
                kw = ATT_KV * HEAD_DIM
                new_state.append(qkv[:, qw:qw + kw].reshape(n_seq, seq_len, ATT_KV, HEAD_DIM))
                new_state.append(qkv[:, qw + kw:].reshape(n_seq, seq_len, ATT_KV, HEAD_DIM))
            return _proj("plain", (att,), w_o, x, *tail)
        if kind == 1:
            p = _matmul(h, ml_w_vo)
            g, gt = _ml_gates(h, ml_w_gates, mlstm1_b_gates)
            qk = _ml_qk(h, ml_w_qk, mlstm1_conv, seq_len)
            st = (state_mlstm_C_1, state_mlstm_n_1, state_mlstm_m_1) if latent else None
            hf, hb, fin = _ml_scan(qk, p, g, gt, st, n_seq, seq_len)
            if not latent:
                new_state.extend(fin)
            return _proj("mlstm", (hf, hb, p, 1, mlstm1_norm_g), ml_w_out, x, *tail)
        p = _matmul(h, gla_w_in)
        u = _matmul(h, gla_w_a1)
        of, ob, s_fin = _gla_scan(p, u, gla_w2, gla_ba, state_gla_S_2 if latent else None, n_seq, seq_len)
        if not latent:
            new_state.append(s_fin)
        gla_vw = GLA_HEADS * GLA_DV
        return _proj("gla", (of, ob, p, (2 * gla_kw + gla_vw) // gla_vw, gla2_norm_g), gla_w_out, x, *tail)

    streams = [dict(latent=False, n_seq=n_ctx, seq_len=ctx_len, x=x_prompt.reshape(n_ctx * ctx_len, d)),
               dict(latent=True, n_seq=n_lat, seq_len=lat_len, x=x_sample.reshape(n_lat * lat_len, d))]
    for s in streams:
        s["h"] = _rownorm(s["x"], norm1_g[0], mods(0, 1, s["latent"]), mods(0, 0, s["latent"]))
    for layer in range(depth):
        counts = jnp.zeros((N_EXPERTS, LANES), jnp.int32)
        for s in streams:
            s["x"], s["h2"], s["meta"], s["wcol"], counts = mixer(layer, s, counts)
        cnt = counts[:, 0]
        for s in streams:
            s["slots"] = _slots(s["meta"], cnt)
        xs, info = _dispatch([s["h2"] for s in streams], jnp.concatenate([s["slots"] for s in streams], axis=1), cnt)
        ys = _ffn(xs, info, moe_wg, moe_wu, moe_wd, layer)
        for s in streams:
            latent = s["latent"]
            gate2 = mods(layer, 5, latent)
            if layer + 1 < depth:
                s["x"], s["h"] = _combine(s["x"], ys, s["slots"], s["wcol"], gate2, norm1_g[layer + 1],
                                          mod=(mods(layer + 1, 1, latent), mods(layer + 1, 0, latent)), out_x=True,
                                          h_dtype=F32 if (layer + 1) % 3 == 1 else BF16)
            else:
                s["out"] = _combine(s["x"], ys, s["slots"], s["wcol"], gate2, final_g, h_dtype=F32)
    y_prompt = streams[0]["out"].reshape(n_ctx, ctx_len, d)
    y_sample = streams[1]["out"].reshape(n_lat, lat_len, d)
    return (y_prompt, y_sample, *new_state)
```
